```python
import jax, jax.numpy as jnp
from jax import lax
import numpy as np

D_MODEL = 2048
BATCH = 8
SEQ = 4096
DEPTH = 1

CHUNK = 64
N_PREV_CHUNKS = 8
BAND_CHUNKS = N_PREV_CHUNKS + 1
BAND_LEN = BAND_CHUNKS * CHUNK
ATT_HEADS = 8
ATT_HEAD_DIM = 128
ATT_WIDTH = ATT_HEADS * ATT_HEAD_DIM
MAX_REL_DIST = 128
N_REL = 2 * MAX_REL_DIST + 1
RET_HEADS = 8
RET_KEY_DIM = 128
RET_VAL_DIM = 128
RET_QK_WIDTH = RET_HEADS * RET_KEY_DIM
RET_V_WIDTH = RET_HEADS * RET_VAL_DIM
ROPE_BASE = 10000.0
D_FF = 5632
EPS = 1e-6
IN_SPLIT_WIDTHS = (ATT_WIDTH, ATT_WIDTH, ATT_WIDTH, RET_QK_WIDTH, RET_QK_WIDTH,
                   RET_V_WIDTH, RET_V_WIDTH, D_MODEL, D_MODEL)
IN_WIDTH = sum(IN_SPLIT_WIDTHS)
IN_SPLIT_POINTS = tuple(int(v) for v in np.cumsum(IN_SPLIT_WIDTHS)[:-1])

kernel_name = "macaron_gated_chunkattn_retention_block"


def rmsnorm(x, g):
    xf = x.astype(jnp.float32)
    y = xf * lax.rsqrt(jnp.mean(xf * xf, axis=-1, keepdims=True) + EPS)
    return (y * g.astype(jnp.float32)).astype(x.dtype)


def swiglu_ffn(x, w_gate, w_up, w_down):
    return (jax.nn.silu(x @ w_gate) * (x @ w_up)) @ w_down


def rotary(x):
    S, d = x.shape[1], x.shape[-1]
    inv = 1.0 / (ROPE_BASE ** (jnp.arange(0, d, 2, dtype=jnp.float32) / d))
    ang = jnp.arange(S, dtype=jnp.float32)[:, None] * inv[None, :]
    cos = jnp.cos(ang)[None, :, None, :].astype(x.dtype)
    sin = jnp.sin(ang)[None, :, None, :].astype(x.dtype)
    x1, x2 = x[..., : d // 2], x[..., d // 2:]
    return jnp.concatenate([x1 * cos - x2 * sin, x1 * sin + x2 * cos], axis=-1)


def chunked_band_attention(q, k, v, rel_bias_table):
    B, S, H, hd = q.shape
    nc = S // CHUNK
    qc = q.reshape(B, nc, CHUNK, H, hd)
    kc = k.reshape(B, nc, CHUNK, H, hd)
    vc = v.reshape(B, nc, CHUNK, H, hd)
    pad = ((0, 0), (N_PREV_CHUNKS, 0), (0, 0), (0, 0), (0, 0))
    kp = jnp.pad(kc, pad)
    vp = jnp.pad(vc, pad)
    k_band = jnp.concatenate([kp[:, j:j + nc] for j in range(BAND_CHUNKS)], axis=2)
    v_band = jnp.concatenate([vp[:, j:j + nc] for j in range(BAND_CHUNKS)], axis=2)
    scores = jnp.einsum('bcqhd,bckhd->bhcqk', qc, k_band,
                        preferred_element_type=jnp.float32) * (hd ** -0.5)
    q_pos = jnp.arange(CHUNK) + N_PREV_CHUNKS * CHUNK
    k_pos = jnp.arange(BAND_LEN)
    rel_idx = jnp.clip(q_pos[:, None] - k_pos[None, :], -MAX_REL_DIST, MAX_REL_DIST) + MAX_REL_DIST
    bias = rel_bias_table.astype(jnp.float32)[:, rel_idx]
    band_chunk = k_pos // CHUNK
    valid = (jnp.arange(nc)[:, None] - N_PREV_CHUNKS + band_chunk[None, :]) >= 0
    scores = scores + bias[None, :, None, :, :]
    scores = jnp.where(valid[None, None, :, None, :], scores, jnp.float32(-1e30))
    p = jax.nn.softmax(scores, axis=-1).astype(v.dtype)
    out = jnp.einsum('bhcqk,bckhd->bcqhd', p, v_band)
    return out.reshape(B, S, H * hd)


def chunkwise_retention(q, k, v):
    B, S, H, dk = q.shape
    dv = v.shape[-1]
    nc = S // CHUNK
    f32 = jnp.float32
    gamma = 1.0 - 2.0 ** (-5.0 - jnp.arange(H, dtype=f32))
    log_g = jnp.log(gamma)
    idx = jnp.arange(CHUNK, dtype=f32)
    intra = jnp.exp(log_g[:, None, None] * jnp.abs(idx[:, None] - idx[None, :]))
    q_decay = jnp.exp(log_g[:, None] * (idx[None, :] + 1.0))
    k_decay = jnp.exp(log_g[:, None] * (CHUNK - 1.0 - idx[None, :]))
    chunk_decay = jnp.exp(log_g * CHUNK)

    def to_chunks(t):
        return t.astype(f32).reshape(B, nc, CHUNK, H, t.shape[-1]).transpose(1, 0, 3, 2, 4)

    qc = to_chunks(q) * (dk ** -0.5)
    kc = to_chunks(k)
    vc = to_chunks(v)

    def step(state, inp):
        qi, ki, vi = inp
        s = jnp.einsum('bhqd,bhkd->bhqk', qi, ki) * intra[None]
        inner = jnp.einsum('bhqk,bhkv->bhqv', s, vi)
        cross = jnp.einsum('bhqd,bhdv->bhqv', qi, state) * q_decay[None, :, :, None]
        new_state = state * chunk_decay[None, :, None, None] + jnp.einsum(
            'bhkd,bhkv->bhdv', ki * k_decay[None, :, :, None], vi)
        return new_state, inner + cross

    state0 = jnp.zeros((B, H, dk, dv), f32)
    _, out = lax.scan(step, state0, (qc, kc, vc))
    out = out.transpose(1, 0, 3, 2, 4).reshape(B, S, H, dv)
    out = out * lax.rsqrt(jnp.mean(out * out, axis=-1, keepdims=True) + EPS)
    return out.astype(v.dtype)


def _fwd_setup_inputs(seed: int = 0) -> dict:
    key = jax.random.key(seed)
    ks = jax.random.split(key, 16)
    f32 = jnp.float32

    def w(k, shape, fan_in):
        return jax.random.normal(k, shape, f32) * (fan_in ** -0.5)

    def gain(k):
        return 1.0 + 0.1 * jax.random.normal(k, (DEPTH, D_MODEL), f32)

    return {
        "x": jax.random.normal(ks[0], (BATCH, SEQ, D_MODEL), f32),
        "norm_ffn1_g": gain(ks[1]),
        "ffn1_w_gate": w(ks[2], (DEPTH, D_MODEL, D_FF), D_MODEL),
        "ffn1_w_up": w(ks[3], (DEPTH, D_MODEL, D_FF), D_MODEL),
        "ffn1_w_down": w(ks[4], (DEPTH, D_FF, D_MODEL), D_FF),
        "norm_mix_g": gain(ks[5]),
        "w_in": w(ks[6], (DEPTH, D_MODEL, IN_WIDTH), D_MODEL),
        "rel_bias": 0.5 * jax.random.normal(ks[7], (DEPTH, ATT_HEADS, N_REL), f32),
        "w_out_att": w(ks[8], (DEPTH, ATT_WIDTH, D_MODEL), ATT_WIDTH),
        "w_out_ret": w(ks[9], (DEPTH, RET_V_WIDTH, D_MODEL), RET_V_WIDTH),
        "w_out": w(ks[10], (DEPTH, D_MODEL, D_MODEL), D_MODEL),
        "norm_ffn2_g": gain(ks[11]),
        "ffn2_w_gate": w(ks[12], (DEPTH, D_MODEL, D_FF), D_MODEL),
        "ffn2_w_up": w(ks[13], (DEPTH, D_MODEL, D_FF), D_MODEL),
        "ffn2_w_down": w(ks[14], (DEPTH, D_FF, D_MODEL), D_FF),
        "norm_final_g": 1.0 + 0.1 * jax.random.normal(ks[15], (D_MODEL,), f32),
    }


def _fwd_reference(x, norm_ffn1_g, ffn1_w_gate, ffn1_w_up, ffn1_w_down, norm_mix_g, w_in, rel_bias,
              w_out_att, w_out_ret, w_out, norm_ffn2_g, ffn2_w_gate, ffn2_w_up, ffn2_w_down,
              norm_final_g):
    B, S, _ = x.shape
    for l in range(DEPTH):
        x = x + 0.5 * swiglu_ffn(rmsnorm(x, norm_ffn1_g[l]), ffn1_w_gate[l], ffn1_w_up[l], ffn1_w_down[l])

        h = rmsnorm(x, norm_mix_g[l])
        proj = h @ w_in[l]
        q_a, k_a, v_a, q_r, k_r, v_r, g_r, gate_a, gate_r = jnp.split(proj, IN_SPLIT_POINTS, axis=-1)

        att = chunked_band_attention(
            q_a.reshape(B, S, ATT_HEADS, ATT_HEAD_DIM),
            k_a.reshape(B, S, ATT_HEADS, ATT_HEAD_DIM),
            v_a.reshape(B, S, ATT_HEADS, ATT_HEAD_DIM),
            rel_bias[l])
        branch_a = att @ w_out_att[l]

        ret = chunkwise_retention(
            rotary(q_r.reshape(B, S, RET_HEADS, RET_KEY_DIM)),
            rotary(k_r.reshape(B, S, RET_HEADS, RET_KEY_DIM)),
            v_r.reshape(B, S, RET_HEADS, RET_VAL_DIM))
        ret = jax.nn.silu(g_r) * ret.reshape(B, S, RET_V_WIDTH)
        branch_r = ret @ w_out_ret[l]

        merged = jax.nn.sigmoid(gate_a) * branch_a + jax.nn.sigmoid(gate_r) * branch_r
        x = x + merged @ w_out[l]

        x = x + 0.5 * swiglu_ffn(rmsnorm(x, norm_ffn2_g[l]), ffn2_w_gate[l], ffn2_w_up[l], ffn2_w_down[l])
    return rmsnorm(x, norm_final_g)


import jax as _jax
import jax.numpy as _jnp

TWIN_FORMAT = 'train_step'
FWD_PARAMS = ['x', 'norm_ffn1_g', 'ffn1_w_gate', 'ffn1_w_up', 'ffn1_w_down', 'norm_mix_g', 'w_in', 'rel_bias', 'w_out_att', 'w_out_ret', 'w_out', 'norm_ffn2_g', 'ffn2_w_gate', 'ffn2_w_up', 'ffn2_w_down', 'norm_final_g']
TWIN_WEIGHTS = ['norm_ffn1_g', 'ffn1_w_gate', 'ffn1_w_up', 'ffn1_w_down', 'norm_mix_g', 'w_in', 'rel_bias', 'w_out_att', 'w_out_ret', 'w_out', 'norm_ffn2_g', 'ffn2_w_gate', 'ffn2_w_up', 'ffn2_w_down', 'norm_final_g']
TWIN_DIFF_INPUT = 'x'
TWIN_INPUTS = ['x', 'norm_ffn1_g', 'ffn1_w_gate', 'ffn1_w_up', 'ffn1_w_down', 'norm_mix_g', 'w_in', 'rel_bias', 'w_out_att', 'w_out_ret', 'w_out', 'norm_ffn2_g', 'ffn2_w_gate', 'ffn2_w_up', 'ffn2_w_down', 'norm_final_g', 'loss_target', 'm_norm_ffn1_g', 'm_ffn1_w_gate', 'm_ffn1_w_up', 'm_ffn1_w_down', 'm_norm_mix_g', 'm_w_in', 'm_rel_bias', 'm_w_out_att', 'm_w_out_ret', 'm_w_out', 'm_norm_ffn2_g', 'm_ffn2_w_gate', 'm_ffn2_w_up', 'm_ffn2_w_down', 'm_norm_final_g', 'v_norm_ffn1_g', 'v_ffn1_w_gate', 'v_ffn1_w_up', 'v_ffn1_w_down', 'v_norm_mix_g', 'v_w_in', 'v_rel_bias', 'v_w_out_att', 'v_w_out_ret', 'v_w_out', 'v_norm_ffn2_g', 'v_ffn2_w_gate', 'v_ffn2_w_up', 'v_ffn2_w_down', 'v_norm_final_g']
TWIN_OUTPUTS = ['loss', 'grad_x', 'grad_norm_ffn1_g', 'grad_ffn1_w_gate', 'grad_ffn1_w_up', 'grad_ffn1_w_down', 'grad_norm_mix_g', 'grad_w_in', 'grad_rel_bias', 'grad_w_out_att', 'grad_w_out_ret', 'grad_w_out', 'grad_norm_ffn2_g', 'grad_ffn2_w_gate', 'grad_ffn2_w_up', 'grad_ffn2_w_down', 'grad_norm_final_g', 'delta_norm_ffn1_g', 'delta_ffn1_w_gate', 'delta_ffn1_w_up', 'delta_ffn1_w_down', 'delta_norm_mix_g', 'delta_w_in', 'delta_rel_bias', 'delta_w_out_att', 'delta_w_out_ret', 'delta_w_out', 'delta_norm_ffn2_g', 'delta_ffn2_w_gate', 'delta_ffn2_w_up', 'delta_ffn2_w_down', 'delta_norm_final_g', 'new_m_norm_ffn1_g', 'new_m_ffn1_w_gate', 'new_m_ffn1_w_up', 'new_m_ffn1_w_down', 'new_m_norm_mix_g', 'new_m_w_in', 'new_m_rel_bias', 'new_m_w_out_att', 'new_m_w_out_ret', 'new_m_w_out', 'new_m_norm_ffn2_g', 'new_m_ffn2_w_gate', 'new_m_ffn2_w_up', 'new_m_ffn2_w_down', 'new_m_norm_final_g', 'new_v_norm_ffn1_g', 'new_v_ffn1_w_gate', 'new_v_ffn1_w_up', 'new_v_ffn1_w_down', 'new_v_norm_mix_g', 'new_v_w_in', 'new_v_rel_bias', 'new_v_w_out_att', 'new_v_w_out_ret', 'new_v_w_out', 'new_v_norm_ffn2_g', 'new_v_ffn2_w_gate', 'new_v_ffn2_w_up', 'new_v_ffn2_w_down', 'new_v_norm_final_g']
TWIN_LEAF_KINDS = {'loss': 'loss', 'grad_x': 'grad_x', 'grad_norm_ffn1_g': 'grad_w', 'grad_ffn1_w_gate': 'grad_w', 'grad_ffn1_w_up': 'grad_w', 'grad_ffn1_w_down': 'grad_w', 'grad_norm_mix_g': 'grad_w', 'grad_w_in': 'grad_w', 'grad_rel_bias': 'grad_w', 'grad_w_out_att': 'grad_w', 'grad_w_out_ret': 'grad_w', 'grad_w_out': 'grad_w', 'grad_norm_ffn2_g': 'grad_w', 'grad_ffn2_w_gate': 'grad_w', 'grad_ffn2_w_up': 'grad_w', 'grad_ffn2_w_down': 'grad_w', 'grad_norm_final_g': 'grad_w', 'delta_norm_ffn1_g': 'delta_w', 'delta_ffn1_w_gate': 'delta_w', 'delta_ffn1_w_up': 'delta_w', 'delta_ffn1_w_down': 'delta_w', 'delta_norm_mix_g': 'delta_w', 'delta_w_in': 'delta_w', 'delta_rel_bias': 'delta_w', 'delta_w_out_att': 'delta_w', 'delta_w_out_ret': 'delta_w', 'delta_w_out': 'delta_w', 'delta_norm_ffn2_g': 'delta_w', 'delta_ffn2_w_gate': 'delta_w', 'delta_ffn2_w_up': 'delta_w', 'delta_ffn2_w_down': 'delta_w', 'delta_norm_final_g': 'delta_w', 'new_m_norm_ffn1_g': 'new_m', 'new_m_ffn1_w_gate': 'new_m', 'new_m_ffn1_w_up': 'new_m', 'new_m_ffn1_w_down': 'new_m', 'new_m_norm_mix_g': 'new_m', 'new_m_w_in': 'new_m', 'new_m_rel_bias': 'new_m', 'new_m_w_out_att': 'new_m', 'new_m_w_out_ret': 'new_m', 'new_m_w_out': 'new_m', 'new_m_norm_ffn2_g': 'new_m', 'new_m_ffn2_w_gate': 'new_m', 'new_m_ffn2_w_up': 'new_m', 'new_m_ffn2_w_down': 'new_m', 'new_m_norm_final_g': 'new_m', 'new_v_norm_ffn1_g': 'new_v', 'new_v_ffn1_w_gate': 'new_v', 'new_v_ffn1_w_up': 'new_v', 'new_v_ffn1_w_down': 'new_v', 'new_v_norm_mix_g': 'new_v', 'new_v_w_in': 'new_v', 'new_v_rel_bias': 'new_v', 'new_v_w_out_att': 'new_v', 'new_v_w_out_ret': 'new_v', 'new_v_w_out': 'new_v', 'new_v_norm_ffn2_g': 'new_v', 'new_v_ffn2_w_gate': 'new_v', 'new_v_ffn2_w_up': 'new_v', 'new_v_ffn2_w_down': 'new_v', 'new_v_norm_final_g': 'new_v'}


def _forward(args):
    return _fwd_reference(*[args[k] for k in FWD_PARAMS])


def _output_shape():
    def fwd():
        inp = _fwd_setup_inputs(0)
        return _fwd_reference(*[inp[k] for k in FWD_PARAMS])
    out = _jax.eval_shape(fwd)
    return out.shape, out.dtype

N_MICROBATCH = 1
ADAM_LR = 0.001
ADAM_B1 = 0.9
ADAM_B2 = 0.999
ADAM_EPS = 1e-08
ADAM_WD = 0.01
ADAM_STEP = 10
PER_EXAMPLE_BATCH_AXIS = {'x': 0, 'loss_target': 0}
SHARED_INPUTS = []
_WEIGHT_DTYPES = {'norm_ffn1_g': _jnp.float32, 'ffn1_w_gate': _jnp.float32, 'ffn1_w_up': _jnp.float32, 'ffn1_w_down': _jnp.float32, 'norm_mix_g': _jnp.float32, 'w_in': _jnp.float32, 'rel_bias': _jnp.float32, 'w_out_att': _jnp.float32, 'w_out_ret': _jnp.float32, 'w_out': _jnp.float32, 'norm_ffn2_g': _jnp.float32, 'ffn2_w_gate': _jnp.float32, 'ffn2_w_up': _jnp.float32, 'ffn2_w_down': _jnp.float32, 'norm_final_g': _jnp.float32}
MOMENT_SCALE = {'norm_ffn1_g': 4.421212e-02, 'ffn1_w_gate': 1.916341e-02, 'ffn1_w_up': 1.859421e-02, 'ffn1_w_down': 3.090195e-02, 'norm_mix_g': 5.901783e-02, 'w_in': 2.505811e-02, 'rel_bias': 7.282720e-03, 'w_out_att': 7.561415e-03, 'w_out_ret': 2.826602e-02, 'w_out': 2.878991e-02, 'norm_ffn2_g': 3.454107e-02, 'ffn2_w_gate': 1.498957e-02, 'ffn2_w_up': 1.473335e-02, 'ffn2_w_down': 2.451652e-02, 'norm_final_g': 1.603202e+01}


def _to_microbatches(a, axis):
    t = _jnp.moveaxis(a, axis, 0)
    t = t.reshape((N_MICROBATCH, t.shape[0] // N_MICROBATCH) + t.shape[1:])
    return _jnp.moveaxis(t, 1, axis + 1)


def setup_inputs(seed: int = 0) -> dict:
    inp = _fwd_setup_inputs(seed)
    key = _jax.random.fold_in(_jax.random.key(seed), 7919)
    shape, _ = _output_shape()
    out = dict(inp)
    out["loss_target"] = _jax.random.normal(_jax.random.fold_in(key, 0), shape, _jnp.float32)
    for i, name in enumerate(TWIN_WEIGHTS):
        w = inp[name].astype(_jnp.float32)
        if MOMENT_SCALE is None:
            s = _jnp.sqrt(_jnp.mean(_jnp.square(w)) + 1e-30)
        else:
            s = MOMENT_SCALE[name]
        km, kv = _jax.random.split(_jax.random.fold_in(key, i + 1))
        out[name] = w
        out["m_" + name] = s * _jax.random.normal(km, w.shape, _jnp.float32)
        out["v_" + name] = (s * s) * _jax.random.uniform(kv, w.shape, _jnp.float32, 0.5, 1.5)
    if N_MICROBATCH > 1:
        for name, axis in PER_EXAMPLE_BATCH_AXIS.items():
            out[name] = _to_microbatches(out[name], axis)
    return {'x': out['x'], 'norm_ffn1_g': out['norm_ffn1_g'], 'ffn1_w_gate': out['ffn1_w_gate'], 'ffn1_w_up': out['ffn1_w_up'], 'ffn1_w_down': out['ffn1_w_down'], 'norm_mix_g': out['norm_mix_g'], 'w_in': out['w_in'], 'rel_bias': out['rel_bias'], 'w_out_att': out['w_out_att'], 'w_out_ret': out['w_out_ret'], 'w_out': out['w_out'], 'norm_ffn2_g': out['norm_ffn2_g'], 'ffn2_w_gate': out['ffn2_w_gate'], 'ffn2_w_up': out['ffn2_w_up'], 'ffn2_w_down': out['ffn2_w_down'], 'norm_final_g': out['norm_final_g'], 'loss_target': out['loss_target'], 'm_norm_ffn1_g': out['m_norm_ffn1_g'], 'm_ffn1_w_gate': out['m_ffn1_w_gate'], 'm_ffn1_w_up': out['m_ffn1_w_up'], 'm_ffn1_w_down': out['m_ffn1_w_down'], 'm_norm_mix_g': out['m_norm_mix_g'], 'm_w_in': out['m_w_in'], 'm_rel_bias': out['m_rel_bias'], 'm_w_out_att': out['m_w_out_att'], 'm_w_out_ret': out['m_w_out_ret'], 'm_w_out': out['m_w_out'], 'm_norm_ffn2_g': out['m_norm_ffn2_g'], 'm_ffn2_w_gate': out['m_ffn2_w_gate'], 'm_ffn2_w_up': out['m_ffn2_w_up'], 'm_ffn2_w_down': out['m_ffn2_w_down'], 'm_norm_final_g': out['m_norm_final_g'], 'v_norm_ffn1_g': out['v_norm_ffn1_g'], 'v_ffn1_w_gate': out['v_ffn1_w_gate'], 'v_ffn1_w_up': out['v_ffn1_w_up'], 'v_ffn1_w_down': out['v_ffn1_w_down'], 'v_norm_mix_g': out['v_norm_mix_g'], 'v_w_in': out['v_w_in'], 'v_rel_bias': out['v_rel_bias'], 'v_w_out_att': out['v_w_out_att'], 'v_w_out_ret': out['v_w_out_ret'], 'v_w_out': out['v_w_out'], 'v_norm_ffn2_g': out['v_norm_ffn2_g'], 'v_ffn2_w_gate': out['v_ffn2_w_gate'], 'v_ffn2_w_up': out['v_ffn2_w_up'], 'v_ffn2_w_down': out['v_ffn2_w_down'], 'v_norm_final_g': out['v_norm_final_g']}


def _loss(weights, diff, rest, loss_target):
    with _jax.named_scope("forward"):
        args = {**rest, TWIN_DIFF_INPUT: diff, **{k: w.astype(_WEIGHT_DTYPES[k]) for k, w in weights.items()}}
        y = _forward(args)
    with _jax.named_scope("loss_head"):
        err = _jnp.square(y.astype(_jnp.float32) - loss_target)
        return 0.5 * _jnp.sum(_jnp.mean(err, axis=-1)) if err.ndim else 0.5 * err


def _adamw(w, g, m, v):
    m = ADAM_B1 * m + (1.0 - ADAM_B1) * g
    v = ADAM_B2 * v + (1.0 - ADAM_B2) * _jnp.square(g)
    m_hat = m / (1.0 - ADAM_B1 ** ADAM_STEP)
    v_hat = v / (1.0 - ADAM_B2 ** ADAM_STEP)
    delta = -ADAM_LR * (m_hat / (_jnp.sqrt(v_hat) + ADAM_EPS) + ADAM_WD * w)
    return delta, m, v


def reference(x, norm_ffn1_g, ffn1_w_gate, ffn1_w_up, ffn1_w_down, norm_mix_g, w_in, rel_bias, w_out_att, w_out_ret, w_out, norm_ffn2_g, ffn2_w_gate, ffn2_w_up, ffn2_w_down, norm_final_g, loss_target, m_norm_ffn1_g, m_ffn1_w_gate, m_ffn1_w_up, m_ffn1_w_down, m_norm_mix_g, m_w_in, m_rel_bias, m_w_out_att, m_w_out_ret, m_w_out, m_norm_ffn2_g, m_ffn2_w_gate, m_ffn2_w_up, m_ffn2_w_down, m_norm_final_g, v_norm_ffn1_g, v_ffn1_w_gate, v_ffn1_w_up, v_ffn1_w_down, v_norm_mix_g, v_w_in, v_rel_bias, v_w_out_att, v_w_out_ret, v_w_out, v_norm_ffn2_g, v_ffn2_w_gate, v_ffn2_w_up, v_ffn2_w_down, v_norm_final_g):
    given = dict(x=x, norm_ffn1_g=norm_ffn1_g, ffn1_w_gate=ffn1_w_gate, ffn1_w_up=ffn1_w_up, ffn1_w_down=ffn1_w_down, norm_mix_g=norm_mix_g, w_in=w_in, rel_bias=rel_bias, w_out_att=w_out_att, w_out_ret=w_out_ret, w_out=w_out, norm_ffn2_g=norm_ffn2_g, ffn2_w_gate=ffn2_w_gate, ffn2_w_up=ffn2_w_up, ffn2_w_down=ffn2_w_down, norm_final_g=norm_final_g, loss_target=loss_target, m_norm_ffn1_g=m_norm_ffn1_g, m_ffn1_w_gate=m_ffn1_w_gate, m_ffn1_w_up=m_ffn1_w_up, m_ffn1_w_down=m_ffn1_w_down, m_norm_mix_g=m_norm_mix_g, m_w_in=m_w_in, m_rel_bias=m_rel_bias, m_w_out_att=m_w_out_att, m_w_out_ret=m_w_out_ret, m_w_out=m_w_out, m_norm_ffn2_g=m_norm_ffn2_g, m_ffn2_w_gate=m_ffn2_w_gate, m_ffn2_w_up=m_ffn2_w_up, m_ffn2_w_down=m_ffn2_w_down, m_norm_final_g=m_norm_final_g, v_norm_ffn1_g=v_norm_ffn1_g, v_ffn1_w_gate=v_ffn1_w_gate, v_ffn1_w_up=v_ffn1_w_up, v_ffn1_w_down=v_ffn1_w_down, v_norm_mix_g=v_norm_mix_g, v_w_in=v_w_in, v_rel_bias=v_rel_bias, v_w_out_att=v_w_out_att, v_w_out_ret=v_w_out_ret, v_w_out=v_w_out, v_norm_ffn2_g=v_norm_ffn2_g, v_ffn2_w_gate=v_ffn2_w_gate, v_ffn2_w_up=v_ffn2_w_up, v_ffn2_w_down=v_ffn2_w_down, v_norm_final_g=v_norm_final_g)
    weights = {n: given[n] for n in TWIN_WEIGHTS}
    shared = {n: given[n] for n in SHARED_INPUTS}
    per_example = {n: given[n] for n in ['x']}
    grad_fn = _jax.value_and_grad(_loss, argnums=(0, 1))

    def one_microbatch(ex, loss_target):
        ex = dict(ex)
        diff = ex.pop(TWIN_DIFF_INPUT)
        return grad_fn(weights, diff, {**shared, **ex}, loss_target)

    if N_MICROBATCH == 1:
        loss, (grad_w, grad_x) = one_microbatch(per_example, given["loss_target"])
    else:
        def body(carry, xs):
            loss_sum, grad_sum = carry
            l_k, (gw_k, gx_k) = one_microbatch(xs[0], xs[1])
            with _jax.named_scope("update"):
                return (loss_sum + l_k, _jax.tree.map(_jnp.add, grad_sum, gw_k)), gx_k

        init = (_jnp.zeros((), _jnp.float32), _jax.tree.map(_jnp.zeros_like, weights))
        (loss, grad_w), grad_x = _jax.lax.scan(body, init, (per_example, given["loss_target"]))
    with _jax.named_scope("update"):
        delta_w, new_m, new_v = {}, {}, {}
        for n in TWIN_WEIGHTS:
            delta_w[n], new_m[n], new_v[n] = _adamw(weights[n], grad_w[n], given["m_" + n], given["v_" + n])
    return (loss, grad_x, *[grad_w[n] for n in TWIN_WEIGHTS], *[delta_w[n] for n in TWIN_WEIGHTS],
            *[new_m[n] for n in TWIN_WEIGHTS], *[new_v[n] for n in TWIN_WEIGHTS])
```

```python
import functools
import math

import jax
import jax.numpy as jnp
import numpy as np
from jax import lax
from jax.experimental import pallas as pl
from jax.experimental.pallas import tpu as pltpu

F32 = jnp.float32
CDT = jnp.bfloat16

N_DEV = 8
CHUNK = 64
N_PREV_CHUNKS = 8
BAND = N_PREV_CHUNKS * CHUNK
HEAD_DIM = 128
MAX_REL_DIST = 128
N_REL = 2 * MAX_REL_DIST + 1
N_REL_PAD = 384
ROPE_BASE = 10000.0
EPS = 1e-6
NEG = -1e30
LANE = 128
ATT_TQ = 256
RET_BLK = 64
VMEM_LIMIT = 48 * 1024 * 1024

ADAM_LR = 0.001
ADAM_B1 = 0.9
ADAM_B2 = 0.999
ADAM_EPS = 1e-08
ADAM_WD = 0.01
ADAM_STEP = 10

MESH_AXES = ("x", "y", "c")
_NT = (((1,), (1,)), ((), ()))
_TN = (((0,), (0,)), ((), ()))


def _round_up(v, m):
    return (v + m - 1) // m * m


def _params(sem=None):
    return pltpu.CompilerParams(dimension_semantics=sem, vmem_limit_bytes=VMEM_LIMIT)


def _sigmoid(v):
    return 1.0 / (1.0 + jnp.exp(-v))


def _bspec(kind, tm, w, off, order):
    def wrap(f):
        if order == "ji":
            return lambda j, i: f(i, j)
        return lambda i, j: f(i, j)
    if kind == "full":
        return pl.BlockSpec((tm, w), wrap(lambda i, j: (i, 0)))
    if kind == "col":
        return pl.BlockSpec((tm, w), wrap(lambda i, j: (i, j + off)))
    assert kind == "3d"
    return pl.BlockSpec((None, tm, w), wrap(lambda i, j: (j, i, 0)))


def _wspec(w, order):
    if order == "ji":
        return pl.BlockSpec((None,) + w.shape[1:], lambda j, i: (j, 0, 0))
    return pl.BlockSpec((None,) + w.shape[1:], lambda i, j: (j, 0, 0))


def _width(arr, kind, w):
    return arr.shape[-1] if kind in ("full", "3d") else w


def mm_block(name, T, tm, lhs, wts, extras, outs, epilogue):
    nl, nw, ne = len(lhs), len(wts), len(extras)
    ni = T // tm

    def body(*refs):
        l = refs[:nl]
        w = refs[nl:nl + nw]
        e = refs[nl + nw:nl + nw + ne]
        o = refs[nl + nw + ne:]
        prods = []
        for k, (_, li, tr) in enumerate(wts):
            a = l[li][...]
            if tr:
                prods.append(lax.dot_general(a, w[k][...], _NT, preferred_element_type=F32))
            else:
                prods.append(jnp.dot(a, w[k][...], preferred_element_type=F32))
        res = epilogue(prods, [r[...].astype(F32) for r in e])
        for r, val in zip(o, res):
            r[...] = val.astype(r.dtype)

    in_specs = [_bspec(k, tm, _width(a, k, w), off, "ji") for (a, k, w, off) in lhs]
    in_specs += [_wspec(w, "ji") for (w, _, _) in wts]
    in_specs += [_bspec(k, tm, _width(a, k, w), off, "ji") for (a, k, w, off) in extras]
    out_specs, out_shape = [], []
    for (kind, w, dt) in outs:
        out_specs.append(_bspec(kind, tm, w, 0, "ji"))
        if kind == "3d":
            out_shape.append(jax.ShapeDtypeStruct((N_DEV, T, w), dt))
        else:
            out_shape.append(jax.ShapeDtypeStruct((T, N_DEV * w), dt))
    args = [a for (a, _, _, _) in lhs] + [w for (w, _, _) in wts] + [a for (a, _, _, _) in extras]
    return pl.pallas_call(
        body, name=name, grid=(N_DEV, ni), in_specs=in_specs, out_specs=out_specs,
        out_shape=out_shape, compiler_params=_params(("parallel", "parallel")))(*args)


def mm_reduce_j(name, T, tm, pairs, out_w, out_dtype, res=None, scale=1.0):
    np_ = len(pairs)
    ni = T // tm

    def body(*refs):
        xs = refs[:np_]
        ws = refs[np_:2 * np_]
        rest = refs[2 * np_:]
        if res is not None:
            res_ref, o_ref, acc = rest
        else:
            o_ref, acc = rest
        j = pl.program_id(1)

        @pl.when(j == 0)
        def _():
            acc[...] = jnp.zeros_like(acc)

        for k, p in enumerate(pairs):
            if p[5]:
                acc[...] += lax.dot_general(xs[k][...], ws[k][...], _NT, preferred_element_type=F32)
            else:
                acc[...] += jnp.dot(xs[k][...], ws[k][...], preferred_element_type=F32)

        @pl.when(j == N_DEV - 1)
        def _():
            if res is not None:
                o_ref[...] = (res_ref[...] + scale * acc[...]).astype(o_ref.dtype)
            else:
                o_ref[...] = acc[...].astype(o_ref.dtype)

    in_specs = [_bspec(k, tm, _width(x, k, w), off, "ij") for (x, k, w, off, _, _) in pairs]
    in_specs += [_wspec(p[4], "ij") for p in pairs]
    args = [p[0] for p in pairs] + [p[4] for p in pairs]
    if res is not None:
        in_specs.append(pl.BlockSpec((tm, out_w), lambda i, j: (i, 0)))
        args.append(res)
    return pl.pallas_call(
        body, name=name, grid=(ni, N_DEV), in_specs=in_specs,
        out_specs=pl.BlockSpec((tm, out_w), lambda i, j: (i, 0)),
        out_shape=jax.ShapeDtypeStruct((T, out_w), out_dtype),
        scratch_shapes=[pltpu.VMEM((tm, out_w), F32)],
        compiler_params=_params(("parallel", "arbitrary")))(*args)


def mm_reduce_i(name, T, tm, a, b):
    ni = T // tm
    rows = _width(a[0], a[1], a[2])
    cols = _width(b[0], b[1], b[2])

    def body(a_ref, b_ref, o_ref, acc):
        i = pl.program_id(1)

        @pl.when(i == 0)
        def _():
            acc[...] = jnp.zeros_like(acc)

        acc[...] += lax.dot_general(a_ref[...], b_ref[...], _TN, preferred_element_type=F32)

        @pl.when(i == ni - 1)
        def _():
            o_ref[...] = acc[...].astype(o_ref.dtype)

    return pl.pallas_call(
        body, name=name, grid=(N_DEV, ni),
        in_specs=[_bspec(a[1], tm, rows, a[3], "ji"), _bspec(b[1], tm, cols, b[3], "ji")],
        out_specs=pl.BlockSpec((None, rows, cols), lambda j, i: (j, 0, 0)),
        out_shape=jax.ShapeDtypeStruct((N_DEV, rows, cols), CDT),
        scratch_shapes=[pltpu.VMEM((rows, cols), F32)],
        compiler_params=_params(("parallel", "arbitrary")))(a[0], b[0])


def _rms_bwd_math(xv, g, dy):
    r = lax.rsqrt(jnp.mean(xv * xv, axis=-1, keepdims=True) + EPS)
    xn = xv * r
    dxn = dy * g
    dx = r * (dxn - xn * jnp.mean(dxn * xn, axis=-1, keepdims=True))
    dg = jnp.sum(dy * xn, axis=0, keepdims=True)
    return dx, dg


def rmsnorm_fwd(name, x, g, tm):
    T, D = x.shape

    def body(x_ref, g_ref, o_ref):
        xv = x_ref[...]
        r = lax.rsqrt(jnp.mean(xv * xv, axis=-1, keepdims=True) + EPS)
        o_ref[...] = (xv * r * g_ref[...]).astype(o_ref.dtype)

    return pl.pallas_call(
        body, name=name, grid=(T // tm,),
        in_specs=[pl.BlockSpec((tm, D), lambda i: (i, 0)), pl.BlockSpec((1, D), lambda i: (0, 0))],
        out_specs=pl.BlockSpec((tm, D), lambda i: (i, 0)),
        out_shape=jax.ShapeDtypeStruct((T, D), CDT),
        compiler_params=_params(("parallel",)))(x, g)


def rmsnorm_bwd(name, x, g, dh, dres, cscale, tm):
    T, D = x.shape

    def body(x_ref, g_ref, dh_ref, dres_ref, dx_ref, dxc_ref, dg_ref):
        i = pl.program_id(0)
        dx, dg = _rms_bwd_math(x_ref[...], g_ref[...], dh_ref[...])
        dx = dres_ref[...] + dx
        dx_ref[...] = dx
        dxc_ref[...] = (cscale * dx).astype(dxc_ref.dtype)

        @pl.when(i == 0)
        def _():
            dg_ref[...] = jnp.zeros_like(dg_ref)

        dg_ref[...] += dg

    row = pl.BlockSpec((tm, D), lambda i: (i, 0))
    vec = pl.BlockSpec((1, D), lambda i: (0, 0))
    return pl.pallas_call(
        body, name=name, grid=(T // tm,), in_specs=[row, vec, row, row],
        out_specs=[row, row, vec],
        out_shape=[jax.ShapeDtypeStruct((T, D), F32), jax.ShapeDtypeStruct((T, D), CDT),
                   jax.ShapeDtypeStruct((1, D), F32)],
        compiler_params=_params(("arbitrary",)))(x, g, dh, dres)


def loss_head(x, g, tgt, tm):
    T, D = x.shape

    def body(x_ref, g_ref, t_ref, dx_ref, dxc_ref, dg_ref, loss_ref):
        i = pl.program_id(0)
        xv = x_ref[...]
        gv = g_ref[...]
        r = lax.rsqrt(jnp.mean(xv * xv, axis=-1, keepdims=True) + EPS)
        err = xv * r * gv - t_ref[...]
        part = jnp.sum(jnp.mean(err * err, axis=-1, keepdims=True), axis=0, keepdims=True)
        dx, dg = _rms_bwd_math(xv, gv, err / D)
        dx_ref[...] = dx
        dxc_ref[...] = (0.5 * dx).astype(dxc_ref.dtype)

        @pl.when(i == 0)
        def _():
            dg_ref[...] = jnp.zeros_like(dg_ref)
            loss_ref[...] = jnp.zeros_like(loss_ref)

        dg_ref[...] += dg
        loss_ref[...] += jnp.broadcast_to(0.5 * part, loss_ref.shape)

    row = pl.BlockSpec((tm, D), lambda i: (i, 0))
    vec = pl.BlockSpec((1, D), lambda i: (0, 0))
    return pl.pallas_call(
        body, name="loss_head", grid=(T // tm,), in_specs=[row, vec, row],
        out_specs=[row, row, vec, pl.BlockSpec((1, LANE), lambda i: (0, 0))],
        out_shape=[jax.ShapeDtypeStruct((T, D), F32), jax.ShapeDtypeStruct((T, D), CDT),
                   jax.ShapeDtypeStruct((1, D), F32), jax.ShapeDtypeStruct((1, LANE), F32)],
        compiler_params=_params(("arbitrary",)))(x, g, tgt)


def _rel_index(tq, kw):
    qi = lax.broadcasted_iota(jnp.int32, (tq, kw), 0)
    kj = lax.broadcasted_iota(jnp.int32, (tq, kw), 1)
    return jnp.clip(qi - kj + BAND, -MAX_REL_DIST, MAX_REL_DIST) + MAX_REL_DIST, qi, kj


def attn_bias(rel_bias, tq):
    H = rel_bias.shape[0]
    kw = BAND + tq

    def body(tbl_ref, o_ref):
        h = pl.program_id(0)
        rel, qi, kj = _rel_index(tq, kw)
        qc = qi // CHUNK
        kc = kj // CHUNK - N_PREV_CHUNKS
        valid = (kc <= qc) & (kc >= qc - N_PREV_CHUNKS)

        def step(r, b):
            return jnp.where(rel == r, tbl_ref[h, r], b)

        b = lax.fori_loop(0, N_REL, step, jnp.zeros((tq, kw), F32))
        o_ref[...] = jnp.where(valid, b, NEG)

    return pl.pallas_call(
        body, name="attn_bias", grid=(H,),
        in_specs=[pl.BlockSpec(memory_space=pltpu.SMEM)],
        out_specs=pl.BlockSpec((None, tq, kw), lambda h: (h, 0, 0)),
        out_shape=jax.ShapeDtypeStruct((H, tq, kw), F32),
        compiler_params=_params(("parallel",)))(rel_bias)


def attn_bias_grad(dst, tq):
    H = dst.shape[0]
    kw = BAND + tq

    def body(d_ref, o_ref):
        rel, _, _ = _rel_index(tq, kw)
        d = d_ref[...]
        lane = lax.broadcasted_iota(jnp.int32, (1, N_REL_PAD), 1)

        def step(r, acc):
            s = jnp.sum(jnp.sum(jnp.where(rel == r, d, 0.0), axis=0, keepdims=True), axis=1, keepdims=True)
            return jnp.where(lane == r, s, acc)

        o_ref[...] = lax.fori_loop(0, N_REL, step, jnp.zeros((1, N_REL_PAD), F32))

    return pl.pallas_call(
        body, name="attn_bias_grad", grid=(H,),
        in_specs=[pl.BlockSpec((None, tq, kw), lambda h: (h, 0, 0))],
        out_specs=pl.BlockSpec((None, 1, N_REL_PAD), lambda h: (h, 0, 0)),
        out_shape=jax.ShapeDtypeStruct((H, 1, N_REL_PAD), F32),
        compiler_params=_params(("parallel",)))(dst)


def _attn_scores(q, kpad, bm_ref, start, kw):
    k = kpad[pl.ds(start, kw), :]
    s = lax.dot_general(q, k, _NT, preferred_element_type=F32) * (HEAD_DIM ** -0.5) + bm_ref[...]
    col = lax.broadcasted_iota(jnp.int32, s.shape, 1)
    s = jnp.where(col < BAND - start, NEG, s)
    m = jnp.max(s, axis=-1, keepdims=True)
    p = jnp.exp(s - m)
    return p / jnp.sum(p, axis=-1, keepdims=True), k


def _fill_padded(pad_ref, src_ref, T):
    pad_ref[pl.ds(0, BAND), :] = jnp.zeros((BAND, HEAD_DIM), pad_ref.dtype)
    pad_ref[pl.ds(BAND, T), :] = src_ref[...]


def attn_fwd(proj, biasm, A, tq):
    T = proj.shape[0]
    H = A // HEAD_DIM
    kw = BAND + tq

    def body(q_ref, k_ref, v_ref, bm_ref, o_ref, kpad, vpad):
        qi = pl.program_id(1)

        @pl.when(qi == 0)
        def _():
            _fill_padded(kpad, k_ref, T)
            _fill_padded(vpad, v_ref, T)

        start = pl.multiple_of(qi * tq, tq)
        p, _ = _attn_scores(q_ref[...], kpad, bm_ref, start, kw)
        v = vpad[pl.ds(start, kw), :]
        o_ref[...] = jnp.dot(p.astype(CDT), v, preferred_element_type=F32).astype(o_ref.dtype)

    return pl.pallas_call(
        body, name="attn_fwd", grid=(H, T // tq),
        in_specs=[pl.BlockSpec((tq, HEAD_DIM), lambda h, i: (i, h)),
                  pl.BlockSpec((T, HEAD_DIM), lambda h, i: (0, H + h)),
                  pl.BlockSpec((T, HEAD_DIM), lambda h, i: (0, 2 * H + h)),
                  pl.BlockSpec((None, tq, kw), lambda h, i: (h, 0, 0))],
        out_specs=pl.BlockSpec((tq, HEAD_DIM), lambda h, i: (i, h)),
        out_shape=jax.ShapeDtypeStruct((T, A), CDT),
        scratch_shapes=[pltpu.VMEM((BAND + T, HEAD_DIM), CDT), pltpu.VMEM((BAND + T, HEAD_DIM), CDT)],
        compiler_params=_params(("parallel", "arbitrary")))(proj, proj, proj, biasm)


def attn_bwd(proj, biasm, datt, A, tq):
    T = proj.shape[0]
    H = A // HEAD_DIM
    kw = BAND + tq
    nq = T // tq
    scale = HEAD_DIM ** -0.5

    def body(q_ref, k_ref, v_ref, bm_ref, do_ref, dq_ref, dk_ref, dv_ref, dst_ref,
             kpad, vpad, dkacc, dvacc):
        qi = pl.program_id(1)

        @pl.when(qi == 0)
        def _():
            _fill_padded(kpad, k_ref, T)
            _fill_padded(vpad, v_ref, T)
            dkacc[...] = jnp.zeros_like(dkacc)
            dvacc[...] = jnp.zeros_like(dvacc)
            dst_ref[...] = jnp.zeros_like(dst_ref)

        start = pl.multiple_of(qi * tq, tq)
        q = q_ref[...]
        p, k = _attn_scores(q, kpad, bm_ref, start, kw)
        v = vpad[pl.ds(start, kw), :]
        do = do_ref[...]
        dp = lax.dot_general(do, v, _NT, preferred_element_type=F32)
        ds = p * (dp - jnp.sum(dp * p, axis=-1, keepdims=True))
        dst_ref[...] += ds
        dsb = ds.astype(CDT)
        dq_ref[...] = (jnp.dot(dsb, k, preferred_element_type=F32) * scale).astype(dq_ref.dtype)
        dkacc[pl.ds(start, kw), :] += lax.dot_general(dsb, q, _TN, preferred_element_type=F32) * scale
        dvacc[pl.ds(start, kw), :] += lax.dot_general(p.astype(CDT), do, _TN, preferred_element_type=F32)

        @pl.when(qi == nq - 1)
        def _():
            dk_ref[...] = dkacc[pl.ds(BAND, T), :].astype(dk_ref.dtype)
            dv_ref[...] = dvacc[pl.ds(BAND, T), :].astype(dv_ref.dtype)

    blk = pl.BlockSpec((tq, HEAD_DIM), lambda h, i: (i, h))
    col = pl.BlockSpec((T, HEAD_DIM), lambda h, i: (0, h))
    bias = pl.BlockSpec((None, tq, kw), lambda h, i: (h, 0, 0))
    return pl.pallas_call(
        body, name="attn_bwd", grid=(H, nq),
        in_specs=[blk,
                  pl.BlockSpec((T, HEAD_DIM), lambda h, i: (0, H + h)),
                  pl.BlockSpec((T, HEAD_DIM), lambda h, i: (0, 2 * H + h)),
                  bias, blk],
        out_specs=[blk, col, col, bias],
        out_shape=[jax.ShapeDtypeStruct((T, A), CDT), jax.ShapeDtypeStruct((T, A), CDT),
                   jax.ShapeDtypeStruct((T, A), CDT), jax.ShapeDtypeStruct((H, tq, kw), F32)],
        scratch_shapes=[pltpu.VMEM((BAND + T, HEAD_DIM), CDT), pltpu.VMEM((BAND + T, HEAD_DIM), CDT),
                        pltpu.VMEM((BAND + T, HEAD_DIM), F32), pltpu.VMEM((BAND + T, HEAD_DIM), F32)],
        compiler_params=_params(("parallel", "arbitrary")))(proj, proj, proj, biasm, datt)


def _retention_tables(T, H, blk):
    half = HEAD_DIM // 2
    inv = 1.0 / (ROPE_BASE ** (jnp.arange(0, HEAD_DIM, 2, dtype=F32) / HEAD_DIM))
    ang = jnp.arange(T, dtype=F32)[:, None] * inv[None, :]
    cos, sin = jnp.cos(ang), jnp.sin(ang)
    rc = jnp.concatenate([cos, cos], axis=1)
    rs = jnp.concatenate([-sin, sin], axis=1)
    assert rc.shape == (T, 2 * half)
    log_g = jnp.log(1.0 - 2.0 ** (-5.0 - jnp.arange(H, dtype=F32)))[:, None, None]
    idx = jnp.arange(blk, dtype=F32)
    n, m = idx[:, None], idx[None, :]
    same = (n // CHUNK) == (m // CHUNK)
    earlier = (m // CHUNK) < (n // CHUNK)
    dist = jnp.where(same, jnp.abs(n - m), n - m)[None]
    dmat = jnp.where((same | earlier)[None], jnp.exp(log_g * dist), 0.0)
    ones = jnp.ones((1, 1, HEAD_DIM), F32)
    qd = jnp.exp(log_g * (idx[None, :, None] + 1.0)) * ones
    kd = jnp.exp(log_g * (blk - 1.0 - idx[None, :, None])) * ones
    cd = jnp.exp(log_g * blk) * jnp.ones((1, 8, HEAD_DIM), F32)
    return rc, rs, dmat, qd, kd, cd


def _rot(v, rc, rs):
    return v * rc + pltpu.roll(v, HEAD_DIM // 2, 1) * rs


def _rot_bwd(dv, rc, rs):
    return dv * rc + pltpu.roll(dv * rs, HEAD_DIM // 2, 1)


def ret_fwd(proj, tables, A, blk):
    T = proj.shape[0]
    H = A // HEAD_DIM
    nb = T // blk
    rc, rs, dmat, qd, kd, cd = tables
    scale = HEAD_DIM ** -0.5

    def body(q_ref, k_ref, v_ref, g_ref, rc_ref, rs_ref, d_ref, qd_ref, kd_ref, cd_ref,
             y_ref, o_ref, st_ref, state):
        b = pl.program_id(1)

        @pl.when(b == 0)
        def _():
            state[...] = jnp.zeros_like(state)

        c, s = rc_ref[...], rs_ref[...]
        qs = (_rot(q_ref[...].astype(F32), c, s) * scale).astype(CDT)
        kr = _rot(k_ref[...].astype(F32), c, s)
        v = v_ref[...]
        sb = state[...].astype(CDT)
        a = lax.dot_general(qs, kr.astype(CDT), _NT, preferred_element_type=F32) * d_ref[...]
        o = jnp.dot(a.astype(CDT), v, preferred_element_type=F32)
        o = o + jnp.dot(qs, sb, preferred_element_type=F32) * qd_ref[...]
        st_ref[...] = sb
        state[...] = state[...] * cd_ref[0:1, :] + lax.dot_general(
            (kr * kd_ref[...]).astype(CDT), v, _TN, preferred_element_type=F32)
        o_ref[...] = o
        on = o * lax.rsqrt(jnp.mean(o * o, axis=-1, keepdims=True) + EPS)
        g = g_ref[...].astype(F32)
        y_ref[...] = (g * _sigmoid(g) * on).astype(y_ref.dtype)

    def pj(off):
        return pl.BlockSpec((blk, HEAD_DIM), lambda h, i: (i, off * H + h))

    tok = pl.BlockSpec((blk, HEAD_DIM), lambda h, i: (i, 0))
    out = pl.BlockSpec((blk, HEAD_DIM), lambda h, i: (i, h))

    def per_head(r):
        return pl.BlockSpec((None, r, HEAD_DIM), lambda h, i: (h, 0, 0))

    return pl.pallas_call(
        body, name="ret_fwd", grid=(H, nb),
        in_specs=[pj(3), pj(4), pj(5), pj(6), tok, tok,
                  pl.BlockSpec((None, blk, blk), lambda h, i: (h, 0, 0)),
                  per_head(blk), per_head(blk), per_head(8)],
        out_specs=[out, out, pl.BlockSpec((None, None, HEAD_DIM, HEAD_DIM), lambda h, i: (h, i, 0, 0))],
        out_shape=[jax.ShapeDtypeStruct((T, A), CDT), jax.ShapeDtypeStruct((T, A), F32),
                   jax.ShapeDtypeStruct((H, nb, HEAD_DIM, HEAD_DIM), CDT)],
        scratch_shapes=[pltpu.VMEM((HEAD_DIM, HEAD_DIM), F32)],
        compiler_params=_params(("parallel", "arbitrary")))(
            proj, proj, proj, proj, rc, rs, dmat, qd, kd, cd)


def ret_bwd(proj, tables, o_raw, states, dy, A, blk):
    T = proj.shape[0]
    H = A // HEAD_DIM
    nb = T // blk
    rc, rs, dmat, qd, kd, cd = tables
    scale = HEAD_DIM ** -0.5

    def body(q_ref, k_ref, v_ref, g_ref, rc_ref, rs_ref, d_ref, qd_ref, kd_ref, cd_ref,
             o_ref, st_ref, dy_ref, dq_ref, dk_ref, dv_ref, dg_ref, dstate):
        b = pl.program_id(1)

        @pl.when(b == 0)
        def _():
            dstate[...] = jnp.zeros_like(dstate)

        c, s = rc_ref[...], rs_ref[...]
        qs = (_rot(q_ref[...].astype(F32), c, s) * scale).astype(CDT)
        kr = _rot(k_ref[...].astype(F32), c, s)
        krb = kr.astype(CDT)
        kdb = (kr * kd_ref[...]).astype(CDT)
        v = v_ref[...]
        dmat_v = d_ref[...]
        a = lax.dot_general(qs, krb, _NT, preferred_element_type=F32) * dmat_v

        o = o_ref[...]
        r = lax.rsqrt(jnp.mean(o * o, axis=-1, keepdims=True) + EPS)
        on = o * r
        g = g_ref[...].astype(F32)
        sg = _sigmoid(g)
        dyv = dy_ref[...].astype(F32)
        dg_ref[...] = (dyv * on * (sg * (1.0 + g * (1.0 - sg)))).astype(dg_ref.dtype)
        don = dyv * (g * sg)
        do = r * (don - on * jnp.mean(don * on, axis=-1, keepdims=True))
        dob = do.astype(CDT)
        doq = (do * qd_ref[...]).astype(CDT)
        dsb = dstate[...].astype(CDT)

        dv = lax.dot_general(a.astype(CDT), dob, _TN, preferred_element_type=F32)
        dv = dv + jnp.dot(kdb, dsb, preferred_element_type=F32)
        dv_ref[...] = dv.astype(dv_ref.dtype)
        dpb = (lax.dot_general(dob, v, _NT, preferred_element_type=F32) * dmat_v).astype(CDT)
        dqs = jnp.dot(dpb, krb, preferred_element_type=F32)
        dqs = dqs + lax.dot_general(doq, st_ref[...], _NT, preferred_element_type=F32)
        dkr = lax.dot_general(dpb, qs, _TN, preferred_element_type=F32)
        dkr = dkr + lax.dot_general(v, dsb, _NT, preferred_element_type=F32) * kd_ref[...]
        dstate[...] = dstate[...] * cd_ref[0:1, :] + lax.dot_general(
            qs, doq, _TN, preferred_element_type=F32)
        dq_ref[...] = _rot_bwd(dqs * scale, c, s).astype(dq_ref.dtype)
        dk_ref[...] = _rot_bwd(dkr, c, s).astype(dk_ref.dtype)

    def pj(off):
        return pl.BlockSpec((blk, HEAD_DIM), lambda h, i: (nb - 1 - i, off * H + h))

    tok = pl.BlockSpec((blk, HEAD_DIM), lambda h, i: (nb - 1 - i, 0))
    out = pl.BlockSpec((blk, HEAD_DIM), lambda h, i: (nb - 1 - i, h))

    def per_head(r, w):
        return pl.BlockSpec((None, r, w), lambda h, i: (h, 0, 0))

    shp = jax.ShapeDtypeStruct((T, A), CDT)
    return pl.pallas_call(
        body, name="ret_bwd", grid=(H, nb),
        in_specs=[pj(3), pj(4), pj(5), pj(6), tok, tok, per_head(blk, blk),
                  per_head(blk, HEAD_DIM), per_head(blk, HEAD_DIM), per_head(8, HEAD_DIM),
                  out, pl.BlockSpec((None, None, HEAD_DIM, HEAD_DIM), lambda h, i: (h, nb - 1 - i, 0, 0)),
                  out],
        out_specs=[out, out, out, out], out_shape=[shp, shp, shp, shp],
        scratch_shapes=[pltpu.VMEM((HEAD_DIM, HEAD_DIM), F32)],
        compiler_params=_params(("parallel", "arbitrary")))(
            proj, proj, proj, proj, rc, rs, dmat, qd, kd, cd, o_raw, states, dy)


def _mesh_pos():
    return lax.axis_index("x"), lax.axis_index("y"), lax.axis_index("c")


def _flat(pos):
    return 4 * pos[0] + 2 * pos[1] + pos[2]


_HBM = pl.BlockSpec(memory_space=pltpu.HBM)


def all_gather_blocks(shards):
    n = len(shards)

    def body(*refs):
        ins, outs = refs[:n], refs[n:2 * n]
        send_sems, recv_sems, local_sems = refs[2 * n:]
        x, y, c = _mesh_pos()
        me, sibling = (x, y, c), (x, y, 1 - c)
        chips = [(1 - x, y), (x, 1 - y), (1 - x, 1 - y)]

        def copy(t, k, block, to, src=None):
            dst = outs[t].at[_flat(block)]
            return pltpu.make_async_remote_copy(
                src_ref=dst if src is None else src, dst_ref=dst,
                send_sem=send_sems.at[t, k], recv_sem=recv_sems.at[t, k],
                device_id=to, device_id_type=pl.DeviceIdType.MESH)

        mine, first, passed = [], [], []
        for t in range(n):
            cp = pltpu.make_async_copy(ins[t], outs[t].at[_flat(me)], local_sems.at[t])
            cp.start()
            mine.append(cp)
            row = [copy(t, 0, me, sibling, src=ins[t])]
            row += [copy(t, 1 + j, me, (*chip, c), src=ins[t]) for j, chip in enumerate(chips)]
            for cp in row:
                cp.start()
            first += row
        for t in range(n):
            for j, chip in enumerate(chips):
                copy(t, 1 + j, (*chip, c), me).wait_recv()
                cp = copy(t, 4 + j, (*chip, c), sibling)
                cp.start()
                passed.append(cp)
        for t in range(n):
            copy(t, 0, sibling, me).wait_recv()
            for j, chip in enumerate(chips):
                copy(t, 4 + j, (*chip, 1 - c), me).wait_recv()
        for cp in first + passed:
            cp.wait_send()
        for cp in mine:
            cp.wait()

    return pl.pallas_call(
        body, name="all_gather_weights",
        in_specs=[_HBM] * n, out_specs=[_HBM] * n,
        out_shape=[jax.ShapeDtypeStruct((N_DEV,) + s.shape, s.dtype) for s in shards],
        scratch_shapes=[pltpu.SemaphoreType.DMA((n, 7)), pltpu.SemaphoreType.DMA((n, 7)),
                        pltpu.SemaphoreType.DMA((n,))],
        )(*shards)


def exchange_blocks(blocked, whole):
    nb_, nw_ = len(blocked), len(whole)
    n = nb_ + nw_

    def body(*refs):
        ins, outs = refs[:n], refs[n:2 * n]
        send_sems, recv_sems, local_sems = refs[2 * n:]
        x, y, c = _mesh_pos()
        me = (x, y, c)
        copies, locals_ = [], []
        for t in range(n):
            def src(pos):
                return ins[t].at[_flat(pos)] if t < nb_ else ins[t]
            cp = pltpu.make_async_copy(src(me), outs[t].at[_flat(me)], local_sems.at[t])
            cp.start()
            locals_.append(cp)
            for k in range(1, N_DEV):
                peer = tuple(1 - v if bit else v for v, bit in zip(me, (k >> 2, (k >> 1) & 1, k & 1)))
                send = pltpu.make_async_remote_copy(
                    src_ref=src(peer), dst_ref=outs[t].at[_flat(me)],
                    send_sem=send_sems.at[t, k - 1], recv_sem=recv_sems.at[t, k - 1],
                    device_id=peer, device_id_type=pl.DeviceIdType.MESH)
                send.start()
                recv = pltpu.make_async_remote_copy(
                    src_ref=src(peer), dst_ref=outs[t].at[_flat(peer)],
                    send_sem=send_sems.at[t, k - 1], recv_sem=recv_sems.at[t, k - 1],
                    device_id=peer, device_id_type=pl.DeviceIdType.MESH)
                copies.append((send, recv))
        for send, recv in copies:
            recv.wait_recv()
        for send, recv in copies:
            send.wait_send()
        for cp in locals_:
            cp.wait()

    out_shape = [jax.ShapeDtypeStruct(a.shape, a.dtype) for a in blocked]
    out_shape += [jax.ShapeDtypeStruct((N_DEV,) + a.shape, a.dtype) for a in whole]
    return pl.pallas_call(
        body, name="exchange_grads",
        in_specs=[_HBM] * n, out_specs=[_HBM] * n, out_shape=out_shape,
        scratch_shapes=[pltpu.SemaphoreType.DMA((n, 7)), pltpu.SemaphoreType.DMA((n, 7)),
                        pltpu.SemaphoreType.DMA((n,))],
        )(*blocked, *whole)


def _adamw_math(w, g, m, v):
    m = ADAM_B1 * m + (1.0 - ADAM_B1) * g
    v = ADAM_B2 * v + (1.0 - ADAM_B2) * (g * g)
    m_hat = m / (1.0 - ADAM_B1 ** ADAM_STEP)
    v_hat = v / (1.0 - ADAM_B2 ** ADAM_STEP)
    delta = -ADAM_LR * (m_hat / (jnp.sqrt(v_hat) + ADAM_EPS) + ADAM_WD * w)
    return delta, m, v


def reduce_adamw(name, land, w, m, v, tr):
    R, C = w.shape
    Cp = land.shape[2]

    def body(l_ref, w_ref, m_ref, v_ref, g_ref, d_ref, nm_ref, nv_ref):
        g = l_ref[0, :, 0:C].astype(F32)
        for s in range(1, N_DEV):
            g = g + l_ref[s, :, 0:C].astype(F32)
        delta, nm, nv = _adamw_math(w_ref[...], g, m_ref[...], v_ref[...])
        g_ref[...] = g
        d_ref[...] = delta
        nm_ref[...] = nm
        nv_ref[...] = nv

    blk = pl.BlockSpec((tr, C), lambda i: (i, 0))
    shp = jax.ShapeDtypeStruct((R, C), F32)
    return pl.pallas_call(
        body, name=name, grid=(R // tr,),
        in_specs=[pl.BlockSpec((N_DEV, tr, Cp), lambda i: (0, i, 0)), blk, blk, blk],
        out_specs=[blk, blk, blk, blk], out_shape=[shp, shp, shp, shp],
        compiler_params=_params(("parallel",)))(land, w, m, v)


def _row_tile(r, cap):
    t = min(r, cap)
    while r % t or t % 8:
        t -= 8
    return t


def kernel(x, norm_ffn1_g, ffn1_w_gate, ffn1_w_up, ffn1_w_down, norm_mix_g, w_in, rel_bias, w_out_att, w_out_ret, w_out, norm_ffn2_g, ffn2_w_gate, ffn2_w_up, ffn2_w_down, norm_final_g, loss_target, m_norm_ffn1_g, m_ffn1_w_gate, m_ffn1_w_up, m_ffn1_w_down, m_norm_mix_g, m_w_in, m_rel_bias, m_w_out_att, m_w_out_ret, m_w_out, m_norm_ffn2_g, m_ffn2_w_gate, m_ffn2_w_up, m_ffn2_w_down, m_norm_final_g, v_norm_ffn1_g, v_ffn1_w_gate, v_ffn1_w_up, v_ffn1_w_down, v_norm_mix_g, v_w_in, v_rel_bias, v_w_out_att, v_w_out_ret, v_w_out, v_norm_ffn2_g, v_ffn2_w_gate, v_ffn2_w_up, v_ffn2_w_down, v_norm_final_g):
    T, D = x.shape[1], x.shape[2]
    A = w_out_att.shape[1]
    H = A // HEAD_DIM
    nf = ffn1_w_gate.shape[2]
    nfp = _round_up(nf, LANE)
    nin = w_in.shape[2]
    nd = w_out.shape[1]
    assert nin % LANE == 0 and nd % LANE == 0 and (7 * A) % nd == 0 and T % ATT_TQ == 0
    tm = min(512, T)
    tn = min(256, T)
    x0 = x[0]
    tgt = loss_target[0]

    def colpad(w):
        return jnp.pad(w[0].astype(CDT), ((0, 0), (0, nfp - nf)))

    def rowpad(w):
        return jnp.pad(w[0].astype(CDT), ((0, nfp - nf), (0, 0)))

    shards = [colpad(ffn1_w_gate), colpad(ffn1_w_up), rowpad(ffn1_w_down), w_in[0].astype(CDT),
              w_out_att[0].astype(CDT), w_out_ret[0].astype(CDT), w_out[0].astype(CDT),
              colpad(ffn2_w_gate), colpad(ffn2_w_up), rowpad(ffn2_w_down)]
    Wg1, Wu1, Wd1, Win, Woa, Wor, Wo, Wg2, Wu2, Wd2 = all_gather_blocks(shards)

    def swiglu(prods, _):
        a, b = prods
        return a, b, a * _sigmoid(a) * b

    def ffn_fwd(tag, xin, g, Wg, Wu, Wd):
        h = rmsnorm_fwd(tag + "_norm", xin, g, tn)
        a, b, mid = mm_block(tag + "_up", T, tm, [(h, "full", D, 0)], [(Wg, 0, False), (Wu, 0, False)],
                             [], [("3d", nfp, CDT)] * 3, swiglu)
        xo = mm_reduce_j(tag + "_down", T, tm, [(mid, "3d", nfp, 0, Wd, False)], D, F32, res=xin, scale=0.5)
        return h, a, b, mid, xo

    h1, a1, b1, mid1, x1 = ffn_fwd("ffn1", x0, norm_ffn1_g, Wg1, Wu1, Wd1)
    h2 = rmsnorm_fwd("mix_norm", x1, norm_mix_g, tn)
    proj, = mm_block("in_proj", T, tm, [(h2, "full", D, 0)], [(Win, 0, False)], [], [("col", nin, CDT)],
                     lambda p, _: p)
    biasm = attn_bias(rel_bias[0], ATT_TQ)
    att = attn_fwd(proj, biasm, A, ATT_TQ)
    tables = _retention_tables(T, H, RET_BLK)
    retg, ret_raw, states = ret_fwd(proj, tables, A, RET_BLK)
    goff = 7 * A // nd

    def merge(prods, ex):
        ba, br = prods
        ga, gr = ex
        return ba, br, _sigmoid(ga) * ba + _sigmoid(gr) * br

    ba, br, merged = mm_block(
        "branches", T, tm, [(att, "full", A, 0), (retg, "full", A, 0)], [(Woa, 0, False), (Wor, 1, False)],
        [(proj, "col", nd, goff), (proj, "col", nd, goff + N_DEV)], [("col", nd, CDT)] * 3, merge)
    x2 = mm_reduce_j("out_proj", T, tm, [(merged, "col", nd, 0, Wo, False)], D, F32, res=x1, scale=1.0)
    h3, a2, b2, mid2, x3 = ffn_fwd("ffn2", x2, norm_ffn2_g, Wg2, Wu2, Wd2)

    dx3, dx3h, dgf, loss_part = loss_head(x3, norm_final_g.reshape(1, D), tgt, tn)

    def swiglu_bwd(prods, ex):
        dm, = prods
        a, b = ex
        sg = _sigmoid(a)
        return dm * b * (sg * (1.0 + a * (1.0 - sg))), dm * (a * sg)

    def ffn_bwd(tag, dxh, h, a, b, mid, Wg, Wu, Wd):
        da, db = mm_block(tag + "_down_bwd", T, tm, [(dxh, "full", D, 0)], [(Wd, 0, True)],
                          [(a, "3d", nfp, 0), (b, "3d", nfp, 0)], [("3d", nfp, CDT)] * 2, swiglu_bwd)
        dWd = mm_reduce_i(tag + "_dwd", T, tm, (mid, "3d", nfp, 0), (dxh, "full", D, 0))
        dWg = mm_reduce_i(tag + "_dwg", T, tm, (h, "full", D, 0), (da, "3d", nfp, 0))
        dWu = mm_reduce_i(tag + "_dwu", T, tm, (h, "full", D, 0), (db, "3d", nfp, 0))
        dh = mm_reduce_j(tag + "_up_bwd", T, tm, [(da, "3d", nfp, 0, Wg, True), (db, "3d", nfp, 0, Wu, True)],
                         D, F32)
        return dh, dWg, dWu, dWd

    dh3, dWg2, dWu2, dWd2 = ffn_bwd("ffn2", dx3h, h3, a2, b2, mid2, Wg2, Wu2, Wd2)
    dx2, dx2c, dg2 = rmsnorm_bwd("ffn2_norm_bwd", x2, norm_ffn2_g, dh3, dx3, 1.0, tn)

    def merge_bwd(prods, ex):
        dmg, = prods
        ba_, br_, ga, gr = ex
        sa, sr = _sigmoid(ga), _sigmoid(gr)
        return dmg * sa, dmg * sr, dmg * ba_ * sa * (1.0 - sa), dmg * br_ * sr * (1.0 - sr)

    dba, dbr, dga, dgr = mm_block(
        "out_proj_bwd", T, tm, [(dx2c, "full", D, 0)], [(Wo, 0, True)],
        [(ba, "col", nd, 0), (br, "col", nd, 0), (proj, "col", nd, goff), (proj, "col", nd, goff + N_DEV)],
        [("col", nd, CDT)] * 4, merge_bwd)
    dWo = mm_reduce_i("dwo", T, tm, (merged, "col", nd, 0), (dx2c, "full", D, 0))
    datt = mm_reduce_j("att_out_bwd", T, tm, [(dba, "col", nd, 0, Woa, True)], A, CDT)
    dretg = mm_reduce_j("ret_out_bwd", T, tm, [(dbr, "col", nd, 0, Wor, True)], A, CDT)
    dWoa = mm_reduce_i("dwoa", T, tm, (att, "full", A, 0), (dba, "col", nd, 0))
    dWor = mm_reduce_i("dwor", T, tm, (retg, "full", A, 0), (dbr, "col", nd, 0))
    dq_r, dk_r, dv_r, dg_r = ret_bwd(proj, tables, ret_raw, states, dretg, A, RET_BLK)
    dq_a, dk_a, dv_a, dst = attn_bwd(proj, biasm, datt, A, ATT_TQ)
    dbias = attn_bias_grad(dst, ATT_TQ)
    dproj = jnp.concatenate([dq_a, dk_a, dv_a, dq_r, dk_r, dv_r, dg_r, dga, dgr], axis=1)
    dWin = mm_reduce_i("dwin", T, tm, (h2, "full", D, 0), (dproj, "col", nin, 0))
    dh2 = mm_reduce_j("in_proj_bwd", T, tm, [(dproj, "col", nin, 0, Win, True)], D, F32)
    dx1, dx1h, dgm = rmsnorm_bwd("mix_norm_bwd", x1, norm_mix_g, dh2, dx2, 0.5, tn)
    dh1, dWg1, dWu1, dWd1 = ffn_bwd("ffn1", dx1h, h1, a1, b1, mid1, Wg1, Wu1, Wd1)
    grad_x, _, dg1 = rmsnorm_bwd("ffn1_norm_bwd", x0, norm_ffn1_g, dh1, dx1, 1.0, tn)

    dgains = jnp.concatenate([dg1, dgm, dg2, dgf, jnp.zeros((4, D), F32)], axis=0)
    dbias = dbias.reshape(H, N_REL_PAD)
    landed = exchange_blocks([dWg1, dWu1, dWd1, dWin, dWoa, dWor, dWo, dWg2, dWu2, dWd2], [dgains, dbias])
    lWg1, lWu1, lWd1, lWin, lWoa, lWor, lWo, lWg2, lWu2, lWd2, lgains, lbias = landed

    def upd(name, land, w, m, v):
        r = w.shape[1]
        outs = reduce_adamw(name, land, w[0], m[0], v[0], _row_tile(r, 256))
        return [o[None] for o in outs]

    res = {
        "ffn1_w_gate": upd("adamw_wg1", lWg1, ffn1_w_gate, m_ffn1_w_gate, v_ffn1_w_gate),
        "ffn1_w_up": upd("adamw_wu1", lWu1, ffn1_w_up, m_ffn1_w_up, v_ffn1_w_up),
        "ffn1_w_down": upd("adamw_wd1", lWd1, ffn1_w_down, m_ffn1_w_down, v_ffn1_w_down),
        "w_in": upd("adamw_win", lWin, w_in, m_w_in, v_w_in),
        "w_out_att": upd("adamw_woa", lWoa, w_out_att, m_w_out_att, v_w_out_att),
        "w_out_ret": upd("adamw_wor", lWor, w_out_ret, m_w_out_ret, v_w_out_ret),
        "w_out": upd("adamw_wo", lWo, w_out, m_w_out, v_w_out),
        "ffn2_w_gate": upd("adamw_wg2", lWg2, ffn2_w_gate, m_ffn2_w_gate, v_ffn2_w_gate),
        "ffn2_w_up": upd("adamw_wu2", lWu2, ffn2_w_up, m_ffn2_w_up, v_ffn2_w_up),
        "ffn2_w_down": upd("adamw_wd2", lWd2, ffn2_w_down, m_ffn2_w_down, v_ffn2_w_down),
    }

    def stack_gains(a, b, c_, d):
        return jnp.concatenate([a, b, c_, d.reshape(1, D), jnp.zeros((4, D), F32)], axis=0)

    gw = stack_gains(norm_ffn1_g, norm_mix_g, norm_ffn2_g, norm_final_g)
    gm = stack_gains(m_norm_ffn1_g, m_norm_mix_g, m_norm_ffn2_g, m_norm_final_g)
    gv = stack_gains(v_norm_ffn1_g, v_norm_mix_g, v_norm_ffn2_g, v_norm_final_g)
    gains = reduce_adamw("adamw_gains", lgains, gw, gm, gv, 8)

    def padb(t):
        return jnp.pad(t[0], ((0, 0), (0, N_REL_PAD - N_REL)))

    bias = [o[:, :N_REL][None] for o in
            reduce_adamw("adamw_bias", lbias, padb(rel_bias), padb(m_rel_bias), padb(v_rel_bias), H)]
    res["norm_ffn1_g"] = [o[0:1] for o in gains]
    res["norm_mix_g"] = [o[1:2] for o in gains]
    res["norm_ffn2_g"] = [o[2:3] for o in gains]
    res["norm_final_g"] = [o[3] for o in gains]
    res["rel_bias"] = bias

    loss = lax.psum(loss_part[0, 0], MESH_AXES)
    names = ["norm_ffn1_g", "ffn1_w_gate", "ffn1_w_up", "ffn1_w_down", "norm_mix_g", "w_in", "rel_bias",
             "w_out_att", "w_out_ret", "w_out", "norm_ffn2_g", "ffn2_w_gate", "ffn2_w_up", "ffn2_w_down",
             "norm_final_g"]
    out = [loss, grad_x[None]]
    for k in range(4):
        out += [res[nm][k] for nm in names]
    return tuple(out)
```

```python
import functools
import math

import jax
import jax.numpy as jnp
import numpy as np
from jax import lax
from jax.experimental import pallas as pl
from jax.experimental.pallas import tpu as pltpu

F32 = jnp.float32
CDT = jnp.bfloat16

N_DEV = 8
CHUNK = 64
N_PREV_CHUNKS = 8
BAND = N_PREV_CHUNKS * CHUNK
HEAD_DIM = 128
MAX_REL_DIST = 128
N_REL = 2 * MAX_REL_DIST + 1
N_REL_PAD = 384
ROPE_BASE = 10000.0
EPS = 1e-6
NEG = -1e30
LANE = 128
ATT_TQ = 256
RET_BLK = 256
VMEM_LIMIT = 48 * 1024 * 1024

ADAM_LR = 0.001
ADAM_B1 = 0.9
ADAM_B2 = 0.999
ADAM_EPS = 1e-08
ADAM_WD = 0.01
ADAM_STEP = 10

MESH_AXES = ("x", "y", "c")
_NT = (((1,), (1,)), ((), ()))
_TN = (((0,), (0,)), ((), ()))


def _round_up(v, m):
    return (v + m - 1) // m * m


def _params(sem=None):
    return pltpu.CompilerParams(dimension_semantics=sem, vmem_limit_bytes=VMEM_LIMIT)


def _sigmoid(v):
    return 1.0 / (1.0 + jnp.exp(-v))


def _bspec(kind, tm, w, off, order):
    def wrap(f):
        if order == "ji":
            return lambda j, i: f(i, j)
        return lambda i, j: f(i, j)
    if kind == "full":
        return pl.BlockSpec((tm, w), wrap(lambda i, j: (i, 0)))
    if kind == "col":
        return pl.BlockSpec((tm, w), wrap(lambda i, j: (i, j + off)))
    assert kind == "3d"
    return pl.BlockSpec((None, tm, w), wrap(lambda i, j: (j, i, 0)))


def _wspec(w, order):
    if order == "ji":
        return pl.BlockSpec((None,) + w.shape[1:], lambda j, i: (j, 0, 0))
    return pl.BlockSpec((None,) + w.shape[1:], lambda i, j: (j, 0, 0))


def _width(arr, kind, w):
    return arr.shape[-1] if kind in ("full", "3d") else w


def mm_block(name, T, tm, lhs, wts, extras, outs, epilogue):
    nl, nw, ne = len(lhs), len(wts), len(extras)
    ni = T // tm

    def body(*refs):
        l = refs[:nl]
        w = refs[nl:nl + nw]
        e = refs[nl + nw:nl + nw + ne]
        o = refs[nl + nw + ne:]
        prods = []
        for k, (_, li, tr) in enumerate(wts):
            a = l[li][...]
            if tr:
                prods.append(lax.dot_general(a, w[k][...], _NT, preferred_element_type=F32))
            else:
                prods.append(jnp.dot(a, w[k][...], preferred_element_type=F32))
        res = epilogue(prods, [r[...].astype(F32) for r in e])
        for r, val in zip(o, res):
            r[...] = val.astype(r.dtype)

    in_specs = [_bspec(k, tm, _width(a, k, w), off, "ji") for (a, k, w, off) in lhs]
    in_specs += [_wspec(w, "ji") for (w, _, _) in wts]
    in_specs += [_bspec(k, tm, _width(a, k, w), off, "ji") for (a, k, w, off) in extras]
    out_specs, out_shape = [], []
    for (kind, w, dt) in outs:
        out_specs.append(_bspec(kind, tm, w, 0, "ji"))
        if kind == "3d":
            out_shape.append(jax.ShapeDtypeStruct((N_DEV, T, w), dt))
        else:
            out_shape.append(jax.ShapeDtypeStruct((T, N_DEV * w), dt))
    args = [a for (a, _, _, _) in lhs] + [w for (w, _, _) in wts] + [a for (a, _, _, _) in extras]
    return pl.pallas_call(
        body, name=name, grid=(N_DEV, ni), in_specs=in_specs, out_specs=out_specs,
        out_shape=out_shape, compiler_params=_params(("parallel", "parallel")))(*args)


def mm_reduce_j(name, T, tm, pairs, out_w, out_dtype, res=None, scale=1.0):
    np_ = len(pairs)
    ni = T // tm

    def body(*refs):
        xs = refs[:np_]
        ws = refs[np_:2 * np_]
        rest = refs[2 * np_:]
        if res is not None:
            res_ref, o_ref, acc = rest
        else:
            o_ref, acc = rest
        j = pl.program_id(1)

        @pl.when(j == 0)
        def _():
            acc[...] = jnp.zeros_like(acc)

        for k, p in enumerate(pairs):
            if p[5]:
                acc[...] += lax.dot_general(xs[k][...], ws[k][...], _NT, preferred_element_type=F32)
            else:
                acc[...] += jnp.dot(xs[k][...], ws[k][...], preferred_element_type=F32)

        @pl.when(j == N_DEV - 1)
        def _():
            if res is not None:
                o_ref[...] = (res_ref[...] + scale * acc[...]).astype(o_ref.dtype)
            else:
                o_ref[...] = acc[...].astype(o_ref.dtype)

    in_specs = [_bspec(k, tm, _width(x, k, w), off, "ij") for (x, k, w, off, _, _) in pairs]
    in_specs += [_wspec(p[4], "ij") for p in pairs]
    args = [p[0] for p in pairs] + [p[4] for p in pairs]
    if res is not None:
        in_specs.append(pl.BlockSpec((tm, out_w), lambda i, j: (i, 0)))
        args.append(res)
    return pl.pallas_call(
        body, name=name, grid=(ni, N_DEV), in_specs=in_specs,
        out_specs=pl.BlockSpec((tm, out_w), lambda i, j: (i, 0)),
        out_shape=jax.ShapeDtypeStruct((T, out_w), out_dtype),
        scratch_shapes=[pltpu.VMEM((tm, out_w), F32)],
        compiler_params=_params(("parallel", "arbitrary")))(*args)


def mm_reduce_i(name, T, tm, a, b):
    ni = T // tm
    rows = _width(a[0], a[1], a[2])
    cols = _width(b[0], b[1], b[2])

    def body(a_ref, b_ref, o_ref, acc):
        i = pl.program_id(1)

        @pl.when(i == 0)
        def _():
            acc[...] = jnp.zeros_like(acc)

        acc[...] += lax.dot_general(a_ref[...], b_ref[...], _TN, preferred_element_type=F32)

        @pl.when(i == ni - 1)
        def _():
            o_ref[...] = acc[...].astype(o_ref.dtype)

    return pl.pallas_call(
        body, name=name, grid=(N_DEV, ni),
        in_specs=[_bspec(a[1], tm, rows, a[3], "ji"), _bspec(b[1], tm, cols, b[3], "ji")],
        out_specs=pl.BlockSpec((None, rows, cols), lambda j, i: (j, 0, 0)),
        out_shape=jax.ShapeDtypeStruct((N_DEV, rows, cols), CDT),
        scratch_shapes=[pltpu.VMEM((rows, cols), F32)],
        compiler_params=_params(("parallel", "arbitrary")))(a[0], b[0])


def _rms_bwd_math(xv, g, dy):
    r = lax.rsqrt(jnp.mean(xv * xv, axis=-1, keepdims=True) + EPS)
    xn = xv * r
    dxn = dy * g
    dx = r * (dxn - xn * jnp.mean(dxn * xn, axis=-1, keepdims=True))
    dg = jnp.sum(dy * xn, axis=0, keepdims=True)
    return dx, dg


def rmsnorm_fwd(name, x, g, tm):
    T, D = x.shape

    def body(x_ref, g_ref, o_ref):
        xv = x_ref[...]
        r = lax.rsqrt(jnp.mean(xv * xv, axis=-1, keepdims=True) + EPS)
        o_ref[...] = (xv * r * g_ref[...]).astype(o_ref.dtype)

    return pl.pallas_call(
        body, name=name, grid=(T // tm,),
        in_specs=[pl.BlockSpec((tm, D), lambda i: (i, 0)), pl.BlockSpec((1, D), lambda i: (0, 0))],
        out_specs=pl.BlockSpec((tm, D), lambda i: (i, 0)),
        out_shape=jax.ShapeDtypeStruct((T, D), CDT),
        compiler_params=_params(("parallel",)))(x, g)


def rmsnorm_bwd(name, x, g, dh, dres, cscale, tm):
    T, D = x.shape

    def body(x_ref, g_ref, dh_ref, dres_ref, dx_ref, dxc_ref, dg_ref):
        i = pl.program_id(0)
        dx, dg = _rms_bwd_math(x_ref[...], g_ref[...], dh_ref[...])
        dx = dres_ref[...] + dx
        dx_ref[...] = dx
        dxc_ref[...] = (cscale * dx).astype(dxc_ref.dtype)

        @pl.when(i == 0)
        def _():
            dg_ref[...] = jnp.zeros_like(dg_ref)

        dg_ref[...] += dg

    row = pl.BlockSpec((tm, D), lambda i: (i, 0))
    vec = pl.BlockSpec((1, D), lambda i: (0, 0))
    return pl.pallas_call(
        body, name=name, grid=(T // tm,), in_specs=[row, vec, row, row],
        out_specs=[row, row, vec],
        out_shape=[jax.ShapeDtypeStruct((T, D), F32), jax.ShapeDtypeStruct((T, D), CDT),
                   jax.ShapeDtypeStruct((1, D), F32)],
        compiler_params=_params(("arbitrary",)))(x, g, dh, dres)


def loss_head(x, g, tgt, tm):
    T, D = x.shape

    def body(x_ref, g_ref, t_ref, dx_ref, dxc_ref, dg_ref, loss_ref):
        i = pl.program_id(0)
        xv = x_ref[...]
        gv = g_ref[...]
        r = lax.rsqrt(jnp.mean(xv * xv, axis=-1, keepdims=True) + EPS)
        err = xv * r * gv - t_ref[...]
        part = jnp.sum(jnp.mean(err * err, axis=-1, keepdims=True), axis=0, keepdims=True)
        dx, dg = _rms_bwd_math(xv, gv, err / D)
        dx_ref[...] = dx
        dxc_ref[...] = (0.5 * dx).astype(dxc_ref.dtype)

        @pl.when(i == 0)
        def _():
            dg_ref[...] = jnp.zeros_like(dg_ref)
            loss_ref[...] = jnp.zeros_like(loss_ref)

        dg_ref[...] += dg
        loss_ref[...] += jnp.broadcast_to(0.5 * part, loss_ref.shape)

    row = pl.BlockSpec((tm, D), lambda i: (i, 0))
    vec = pl.BlockSpec((1, D), lambda i: (0, 0))
    return pl.pallas_call(
        body, name="loss_head", grid=(T // tm,), in_specs=[row, vec, row],
        out_specs=[row, row, vec, pl.BlockSpec((1, LANE), lambda i: (0, 0))],
        out_shape=[jax.ShapeDtypeStruct((T, D), F32), jax.ShapeDtypeStruct((T, D), CDT),
                   jax.ShapeDtypeStruct((1, D), F32), jax.ShapeDtypeStruct((1, LANE), F32)],
        compiler_params=_params(("arbitrary",)))(x, g, tgt)


def _skew_rows(z, left):
    tq, kw = z.shape
    row = lax.broadcasted_iota(jnp.int32, (tq, kw), 0)
    s = 1
    while s < tq:
        z = jnp.where((row & s) != 0, pltpu.roll(z, kw - s if left else s, 1), z)
        s *= 2
    return z


REL_HI = BAND + MAX_REL_DIST
REL_LO = BAND - MAX_REL_DIST


def attn_bias(rel_bias, tq):
    H = rel_bias.shape[0]
    kw = BAND + tq
    by_skew = rel_bias[:, jnp.clip(REL_HI - jnp.arange(kw), 0, 2 * MAX_REL_DIST)].reshape(H, 1, kw)

    def body(t_ref, o_ref):
        t = t_ref[...]
        qi = lax.broadcasted_iota(jnp.int32, (tq, kw), 0)
        kj = lax.broadcasted_iota(jnp.int32, (tq, kw), 1)
        b = _skew_rows(jnp.broadcast_to(t, (tq, kw)), left=False)
        b = jnp.where(kj < qi, t[:, 0:1], b)
        qc = qi // CHUNK
        kc = kj // CHUNK - N_PREV_CHUNKS
        valid = (kc <= qc) & (kc >= qc - N_PREV_CHUNKS)
        o_ref[...] = jnp.where(valid, b, NEG)

    return pl.pallas_call(
        body, name="attn_bias", grid=(H,),
        in_specs=[pl.BlockSpec((None, 1, kw), lambda h: (h, 0, 0))],
        out_specs=pl.BlockSpec((None, tq, kw), lambda h: (h, 0, 0)),
        out_shape=jax.ShapeDtypeStruct((H, tq, kw), F32),
        compiler_params=_params(("parallel",)))(by_skew)


def attn_bias_grad(dst, tq):
    H = dst.shape[0]
    kw = BAND + tq

    def body(d_ref, o_ref):
        z = _skew_rows(d_ref[...], left=True)
        qi = lax.broadcasted_iota(jnp.int32, (tq, kw), 0)
        kj = lax.broadcasted_iota(jnp.int32, (tq, kw), 1)
        wrapped = kj + qi >= kw
        c = jnp.sum(jnp.where(wrapped, 0.0, z), axis=0, keepdims=True)
        cw = jnp.sum(jnp.sum(jnp.where(wrapped, z, 0.0), axis=0, keepdims=True), axis=1, keepdims=True)
        lane = lax.broadcasted_iota(jnp.int32, (1, kw), 1)
        ahead = jnp.sum(jnp.where(lane >= REL_HI, c, 0.0), axis=1, keepdims=True)
        behind = jnp.sum(jnp.where(lane <= REL_LO, c, 0.0), axis=1, keepdims=True) + cw
        o_ref[...] = jnp.where(lane == REL_HI, ahead, jnp.where(lane == REL_LO, behind, c))

    by_skew = pl.pallas_call(
        body, name="attn_bias_grad", grid=(H,),
        in_specs=[pl.BlockSpec((None, tq, kw), lambda h: (h, 0, 0))],
        out_specs=pl.BlockSpec((None, 1, kw), lambda h: (h, 0, 0)),
        out_shape=jax.ShapeDtypeStruct((H, 1, kw), F32),
        compiler_params=_params(("parallel",)))(dst)
    return by_skew[:, 0, REL_LO:REL_HI + 1][:, ::-1]


def _attn_scores(q, kpad, bm_ref, start, kw):
    k = kpad[pl.ds(start, kw), :]
    s = lax.dot_general(q, k, _NT, preferred_element_type=F32) * (HEAD_DIM ** -0.5) + bm_ref[...]
    col = lax.broadcasted_iota(jnp.int32, s.shape, 1)
    s = jnp.where(col < BAND - start, NEG, s)
    m = jnp.max(s, axis=-1, keepdims=True)
    p = jnp.exp(s - m)
    return p / jnp.sum(p, axis=-1, keepdims=True), k


def _fill_padded(pad_ref, src_ref, T):
    pad_ref[pl.ds(0, BAND), :] = jnp.zeros((BAND, HEAD_DIM), pad_ref.dtype)
    pad_ref[pl.ds(BAND, T), :] = src_ref[...]


def attn_fwd(proj, biasm, A, tq):
    T = proj.shape[0]
    H = A // HEAD_DIM
    kw = BAND + tq

    def body(q_ref, k_ref, v_ref, bm_ref, o_ref, kpad, vpad):
        qi = pl.program_id(1)

        @pl.when(qi == 0)
        def _():
            _fill_padded(kpad, k_ref, T)
            _fill_padded(vpad, v_ref, T)

        start = pl.multiple_of(qi * tq, tq)
        p, _ = _attn_scores(q_ref[...], kpad, bm_ref, start, kw)
        v = vpad[pl.ds(start, kw), :]
        o_ref[...] = jnp.dot(p.astype(CDT), v, preferred_element_type=F32).astype(o_ref.dtype)

    return pl.pallas_call(
        body, name="attn_fwd", grid=(H, T // tq),
        in_specs=[pl.BlockSpec((tq, HEAD_DIM), lambda h, i: (i, h)),
                  pl.BlockSpec((T, HEAD_DIM), lambda h, i: (0, H + h)),
                  pl.BlockSpec((T, HEAD_DIM), lambda h, i: (0, 2 * H + h)),
                  pl.BlockSpec((None, tq, kw), lambda h, i: (h, 0, 0))],
        out_specs=pl.BlockSpec((tq, HEAD_DIM), lambda h, i: (i, h)),
        out_shape=jax.ShapeDtypeStruct((T, A), CDT),
        scratch_shapes=[pltpu.VMEM((BAND + T, HEAD_DIM), CDT), pltpu.VMEM((BAND + T, HEAD_DIM), CDT)],
        compiler_params=_params(("parallel", "arbitrary")))(proj, proj, proj, biasm)


def attn_bwd(proj, biasm, datt, A, tq):
    T = proj.shape[0]
    H = A // HEAD_DIM
    kw = BAND + tq
    nq = T // tq
    scale = HEAD_DIM ** -0.5

    def body(q_ref, k_ref, v_ref, bm_ref, do_ref, dq_ref, dk_ref, dv_ref, dst_ref,
             kpad, vpad, dkacc, dvacc):
        qi = pl.program_id(1)

        @pl.when(qi == 0)
        def _():
            _fill_padded(kpad, k_ref, T)
            _fill_padded(vpad, v_ref, T)
            dkacc[...] = jnp.zeros_like(dkacc)
            dvacc[...] = jnp.zeros_like(dvacc)
            dst_ref[...] = jnp.zeros_like(dst_ref)

        start = pl.multiple_of(qi * tq, tq)
        q = q_ref[...]
        p, k = _attn_scores(q, kpad, bm_ref, start, kw)
        v = vpad[pl.ds(start, kw), :]
        do = do_ref[...]
        dp = lax.dot_general(do, v, _NT, preferred_element_type=F32)
        ds = p * (dp - jnp.sum(dp * p, axis=-1, keepdims=True))
        dst_ref[...] += ds
        dsb = ds.astype(CDT)
        dq_ref[...] = (jnp.dot(dsb, k, preferred_element_type=F32) * scale).astype(dq_ref.dtype)
        dkacc[pl.ds(start, kw), :] += lax.dot_general(dsb, q, _TN, preferred_element_type=F32) * scale
        dvacc[pl.ds(start, kw), :] += lax.dot_general(p.astype(CDT), do, _TN, preferred_element_type=F32)

        @pl.when(qi == nq - 1)
        def _():
            dk_ref[...] = dkacc[pl.ds(BAND, T), :].astype(dk_ref.dtype)
            dv_ref[...] = dvacc[pl.ds(BAND, T), :].astype(dv_ref.dtype)

    blk = pl.BlockSpec((tq, HEAD_DIM), lambda h, i: (i, h))
    col = pl.BlockSpec((T, HEAD_DIM), lambda h, i: (0, h))
    bias = pl.BlockSpec((None, tq, kw), lambda h, i: (h, 0, 0))
    return pl.pallas_call(
        body, name="attn_bwd", grid=(H, nq),
        in_specs=[blk,
                  pl.BlockSpec((T, HEAD_DIM), lambda h, i: (0, H + h)),
                  pl.BlockSpec((T, HEAD_DIM), lambda h, i: (0, 2 * H + h)),
                  bias, blk],
        out_specs=[blk, col, col, bias],
        out_shape=[jax.ShapeDtypeStruct((T, A), CDT), jax.ShapeDtypeStruct((T, A), CDT),
                   jax.ShapeDtypeStruct((T, A), CDT), jax.ShapeDtypeStruct((H, tq, kw), F32)],
        scratch_shapes=[pltpu.VMEM((BAND + T, HEAD_DIM), CDT), pltpu.VMEM((BAND + T, HEAD_DIM), CDT),
                        pltpu.VMEM((BAND + T, HEAD_DIM), F32), pltpu.VMEM((BAND + T, HEAD_DIM), F32)],
        compiler_params=_params(("parallel", "arbitrary")))(proj, proj, proj, biasm, datt)


def _retention_tables(T, H, blk):
    half = HEAD_DIM // 2
    inv = 1.0 / (ROPE_BASE ** (jnp.arange(0, HEAD_DIM, 2, dtype=F32) / HEAD_DIM))
    ang = jnp.arange(T, dtype=F32)[:, None] * inv[None, :]
    cos, sin = jnp.cos(ang), jnp.sin(ang)
    rc = jnp.concatenate([cos, cos], axis=1)
    rs = jnp.concatenate([-sin, sin], axis=1)
    assert rc.shape == (T, 2 * half)
    log_g = jnp.log(1.0 - 2.0 ** (-5.0 - jnp.arange(H, dtype=F32)))[:, None, None]
    idx = jnp.arange(blk, dtype=F32)
    n, m = idx[:, None], idx[None, :]
    same = (n // CHUNK) == (m // CHUNK)
    earlier = (m // CHUNK) < (n // CHUNK)
    dist = jnp.where(same, jnp.abs(n - m), n - m)[None]
    dmat = jnp.where((same | earlier)[None], jnp.exp(log_g * dist), 0.0)
    ones = jnp.ones((1, 1, HEAD_DIM), F32)
    qd = jnp.exp(log_g * (idx[None, :, None] + 1.0)) * ones
    kd = jnp.exp(log_g * (blk - 1.0 - idx[None, :, None])) * ones
    cd = jnp.exp(log_g * blk) * jnp.ones((1, 8, HEAD_DIM), F32)
    return rc, rs, dmat, qd, kd, cd


def _rot(v, rc, rs):
    return v * rc + pltpu.roll(v, HEAD_DIM // 2, 1) * rs


def _rot_bwd(dv, rc, rs):
    return dv * rc + pltpu.roll(dv * rs, HEAD_DIM // 2, 1)


def ret_fwd(proj, tables, A, blk):
    T = proj.shape[0]
    H = A // HEAD_DIM
    nb = T // blk
    rc, rs, dmat, qd, kd, cd = tables
    scale = HEAD_DIM ** -0.5

    def body(q_ref, k_ref, v_ref, g_ref, rc_ref, rs_ref, d_ref, qd_ref, kd_ref, cd_ref,
             y_ref, o_ref, st_ref, state):
        b = pl.program_id(1)

        @pl.when(b == 0)
        def _():
            state[...] = jnp.zeros_like(state)

        c, s = rc_ref[...], rs_ref[...]
        qs = (_rot(q_ref[...].astype(F32), c, s) * scale).astype(CDT)
        kr = _rot(k_ref[...].astype(F32), c, s)
        v = v_ref[...]
        sb = state[...].astype(CDT)
        a = lax.dot_general(qs, kr.astype(CDT), _NT, preferred_element_type=F32) * d_ref[...]
        o = jnp.dot(a.astype(CDT), v, preferred_element_type=F32)
        o = o + jnp.dot(qs, sb, preferred_element_type=F32) * qd_ref[...]
        st_ref[...] = sb
        state[...] = state[...] * cd_ref[0:1, :] + lax.dot_general(
            (kr * kd_ref[...]).astype(CDT), v, _TN, preferred_element_type=F32)
        o_ref[...] = o
        on = o * lax.rsqrt(jnp.mean(o * o, axis=-1, keepdims=True) + EPS)
        g = g_ref[...].astype(F32)
        y_ref[...] = (g * _sigmoid(g) * on).astype(y_ref.dtype)

    def pj(off):
        return pl.BlockSpec((blk, HEAD_DIM), lambda h, i: (i, off * H + h))

    tok = pl.BlockSpec((blk, HEAD_DIM), lambda h, i: (i, 0))
    out = pl.BlockSpec((blk, HEAD_DIM), lambda h, i: (i, h))

    def per_head(r):
        return pl.BlockSpec((None, r, HEAD_DIM), lambda h, i: (h, 0, 0))

    return pl.pallas_call(
        body, name="ret_fwd", grid=(H, nb),
        in_specs=[pj(3), pj(4), pj(5), pj(6), tok, tok,
                  pl.BlockSpec((None, blk, blk), lambda h, i: (h, 0, 0)),
                  per_head(blk), per_head(blk), per_head(8)],
        out_specs=[out, out, pl.BlockSpec((None, None, HEAD_DIM, HEAD_DIM), lambda h, i: (h, i, 0, 0))],
        out_shape=[jax.ShapeDtypeStruct((T, A), CDT), jax.ShapeDtypeStruct((T, A), F32),
                   jax.ShapeDtypeStruct((H, nb, HEAD_DIM, HEAD_DIM), CDT)],
        scratch_shapes=[pltpu.VMEM((HEAD_DIM, HEAD_DIM), F32)],
        compiler_params=_params(("parallel", "arbitrary")))(
            proj, proj, proj, proj, rc, rs, dmat, qd, kd, cd)


def ret_bwd(proj, tables, o_raw, states, dy, A, blk):
    T = proj.shape[0]
    H = A // HEAD_DIM
    nb = T // blk
    rc, rs, dmat, qd, kd, cd = tables
    scale = HEAD_DIM ** -0.5

    def body(q_ref, k_ref, v_ref, g_ref, rc_ref, rs_ref, d_ref, qd_ref, kd_ref, cd_ref,
             o_ref, st_ref, dy_ref, dq_ref, dk_ref, dv_ref, dg_ref, dstate):
        b = pl.program_id(1)

        @pl.when(b == 0)
        def _():
            dstate[...] = jnp.zeros_like(dstate)

        c, s = rc_ref[...], rs_ref[...]
        qs = (_rot(q_ref[...].astype(F32), c, s) * scale).astype(CDT)
        kr = _rot(k_ref[...].astype(F32), c, s)
        krb = kr.astype(CDT)
        kdb = (kr * kd_ref[...]).astype(CDT)
        v = v_ref[...]
        dmat_v = d_ref[...]
        a = lax.dot_general(qs, krb, _NT, preferred_element_type=F32) * dmat_v

        o = o_ref[...]
        r = lax.rsqrt(jnp.mean(o * o, axis=-1, keepdims=True) + EPS)
        on = o * r
        g = g_ref[...].astype(F32)
        sg = _sigmoid(g)
        dyv = dy_ref[...].astype(F32)
        dg_ref[...] = (dyv * on * (sg * (1.0 + g * (1.0 - sg)))).astype(dg_ref.dtype)
        don = dyv * (g * sg)
        do = r * (don - on * jnp.mean(don * on, axis=-1, keepdims=True))
        dob = do.astype(CDT)
        doq = (do * qd_ref[...]).astype(CDT)
        dsb = dstate[...].astype(CDT)

        dv = lax.dot_general(a.astype(CDT), dob, _TN, preferred_element_type=F32)
        dv = dv + jnp.dot(kdb, dsb, preferred_element_type=F32)
        dv_ref[...] = dv.astype(dv_ref.dtype)
        dpb = (lax.dot_general(dob, v, _NT, preferred_element_type=F32) * dmat_v).astype(CDT)
        dqs = jnp.dot(dpb, krb, preferred_element_type=F32)
        dqs = dqs + lax.dot_general(doq, st_ref[...], _NT, preferred_element_type=F32)
        dkr = lax.dot_general(dpb, qs, _TN, preferred_element_type=F32)
        dkr = dkr + lax.dot_general(v, dsb, _NT, preferred_element_type=F32) * kd_ref[...]
        dstate[...] = dstate[...] * cd_ref[0:1, :] + lax.dot_general(
            qs, doq, _TN, preferred_element_type=F32)
        dq_ref[...] = _rot_bwd(dqs * scale, c, s).astype(dq_ref.dtype)
        dk_ref[...] = _rot_bwd(dkr, c, s).astype(dk_ref.dtype)

    def pj(off):
        return pl.BlockSpec((blk, HEAD_DIM), lambda h, i: (nb - 1 - i, off * H + h))

    tok = pl.BlockSpec((blk, HEAD_DIM), lambda h, i: (nb - 1 - i, 0))
    out = pl.BlockSpec((blk, HEAD_DIM), lambda h, i: (nb - 1 - i, h))

    def per_head(r, w):
        return pl.BlockSpec((None, r, w), lambda h, i: (h, 0, 0))

    shp = jax.ShapeDtypeStruct((T, A), CDT)
    return pl.pallas_call(
        body, name="ret_bwd", grid=(H, nb),
        in_specs=[pj(3), pj(4), pj(5), pj(6), tok, tok, per_head(blk, blk),
                  per_head(blk, HEAD_DIM), per_head(blk, HEAD_DIM), per_head(8, HEAD_DIM),
                  out, pl.BlockSpec((None, None, HEAD_DIM, HEAD_DIM), lambda h, i: (h, nb - 1 - i, 0, 0)),
                  out],
        out_specs=[out, out, out, out], out_shape=[shp, shp, shp, shp],
        scratch_shapes=[pltpu.VMEM((HEAD_DIM, HEAD_DIM), F32)],
        compiler_params=_params(("parallel", "arbitrary")))(
            proj, proj, proj, proj, rc, rs, dmat, qd, kd, cd, o_raw, states, dy)


def _mesh_pos():
    return lax.axis_index("x"), lax.axis_index("y"), lax.axis_index("c")


def _flat(pos):
    return 4 * pos[0] + 2 * pos[1] + pos[2]


_HBM = pl.BlockSpec(memory_space=pltpu.HBM)


def all_gather_blocks(shards):
    n = len(shards)

    def body(*refs):
        ins, outs = refs[:n], refs[n:2 * n]
        send_sems, recv_sems, local_sems = refs[2 * n:]
        x, y, c = _mesh_pos()
        me, sibling = (x, y, c), (x, y, 1 - c)
        chips = [(1 - x, y), (x, 1 - y), (1 - x, 1 - y)]

        def copy(t, k, block, to, src=None):
            dst = outs[t].at[_flat(block)]
            return pltpu.make_async_remote_copy(
                src_ref=dst if src is None else src, dst_ref=dst,
                send_sem=send_sems.at[t, k], recv_sem=recv_sems.at[t, k],
                device_id=to, device_id_type=pl.DeviceIdType.MESH)

        mine, first, passed = [], [], []
        for t in range(n):
            cp = pltpu.make_async_copy(ins[t], outs[t].at[_flat(me)], local_sems.at[t])
            cp.start()
            mine.append(cp)
            row = [copy(t, 0, me, sibling, src=ins[t])]
            row += [copy(t, 1 + j, me, (*chip, c), src=ins[t]) for j, chip in enumerate(chips)]
            for cp in row:
                cp.start()
            first += row
        for t in range(n):
            for j, chip in enumerate(chips):
                copy(t, 1 + j, (*chip, c), me).wait_recv()
                cp = copy(t, 4 + j, (*chip, c), sibling)
                cp.start()
                passed.append(cp)
        for t in range(n):
            copy(t, 0, sibling, me).wait_recv()
            for j, chip in enumerate(chips):
                copy(t, 4 + j, (*chip, 1 - c), me).wait_recv()
        for cp in first + passed:
            cp.wait_send()
        for cp in mine:
            cp.wait()

    return pl.pallas_call(
        body, name="all_gather_weights",
        in_specs=[_HBM] * n, out_specs=[_HBM] * n,
        out_shape=[jax.ShapeDtypeStruct((N_DEV,) + s.shape, s.dtype) for s in shards],
        scratch_shapes=[pltpu.SemaphoreType.DMA((n, 7)), pltpu.SemaphoreType.DMA((n, 7)),
                        pltpu.SemaphoreType.DMA((n,))],
        )(*shards)


def exchange_blocks(blocked, whole):
    nb_, nw_ = len(blocked), len(whole)
    n = nb_ + nw_

    def body(*refs):
        ins, outs = refs[:n], refs[n:2 * n]
        send_sems, recv_sems, local_sems = refs[2 * n:]
        x, y, c = _mesh_pos()
        me = (x, y, c)
        copies, locals_ = [], []
        for t in range(n):
            def src(pos):
                return ins[t].at[_flat(pos)] if t < nb_ else ins[t]
            cp = pltpu.make_async_copy(src(me), outs[t].at[_flat(me)], local_sems.at[t])
            cp.start()
            locals_.append(cp)
            for k in range(1, N_DEV):
                peer = tuple(1 - v if bit else v for v, bit in zip(me, (k >> 2, (k >> 1) & 1, k & 1)))
                send = pltpu.make_async_remote_copy(
                    src_ref=src(peer), dst_ref=outs[t].at[_flat(me)],
                    send_sem=send_sems.at[t, k - 1], recv_sem=recv_sems.at[t, k - 1],
                    device_id=peer, device_id_type=pl.DeviceIdType.MESH)
                send.start()
                recv = pltpu.make_async_remote_copy(
                    src_ref=src(peer), dst_ref=outs[t].at[_flat(peer)],
                    send_sem=send_sems.at[t, k - 1], recv_sem=recv_sems.at[t, k - 1],
                    device_id=peer, device_id_type=pl.DeviceIdType.MESH)
                copies.append((send, recv))
        for send, recv in copies:
            recv.wait_recv()
        for send, recv in copies:
            send.wait_send()
        for cp in locals_:
            cp.wait()

    out_shape = [jax.ShapeDtypeStruct(a.shape, a.dtype) for a in blocked]
    out_shape += [jax.ShapeDtypeStruct((N_DEV,) + a.shape, a.dtype) for a in whole]
    return pl.pallas_call(
        body, name="exchange_grads",
        in_specs=[_HBM] * n, out_specs=[_HBM] * n, out_shape=out_shape,
        scratch_shapes=[pltpu.SemaphoreType.DMA((n, 7)), pltpu.SemaphoreType.DMA((n, 7)),
                        pltpu.SemaphoreType.DMA((n,))],
        )(*blocked, *whole)


def _adamw_math(w, g, m, v):
    m = ADAM_B1 * m + (1.0 - ADAM_B1) * g
    v = ADAM_B2 * v + (1.0 - ADAM_B2) * (g * g)
    m_hat = m / (1.0 - ADAM_B1 ** ADAM_STEP)
    v_hat = v / (1.0 - ADAM_B2 ** ADAM_STEP)
    delta = -ADAM_LR * (m_hat / (jnp.sqrt(v_hat) + ADAM_EPS) + ADAM_WD * w)
    return delta, m, v


def reduce_adamw(name, land, w, m, v, tr):
    R, C = w.shape
    Cp = land.shape[2]

    def body(l_ref, w_ref, m_ref, v_ref, g_ref, d_ref, nm_ref, nv_ref):
        g = l_ref[0, :, 0:C].astype(F32)
        for s in range(1, N_DEV):
            g = g + l_ref[s, :, 0:C].astype(F32)
        delta, nm, nv = _adamw_math(w_ref[...], g, m_ref[...], v_ref[...])
        g_ref[...] = g
        d_ref[...] = delta
        nm_ref[...] = nm
        nv_ref[...] = nv

    blk = pl.BlockSpec((tr, C), lambda i: (i, 0))
    shp = jax.ShapeDtypeStruct((R, C), F32)
    return pl.pallas_call(
        body, name=name, grid=(R // tr,),
        in_specs=[pl.BlockSpec((N_DEV, tr, Cp), lambda i: (0, i, 0)), blk, blk, blk],
        out_specs=[blk, blk, blk, blk], out_shape=[shp, shp, shp, shp],
        compiler_params=_params(("parallel",)))(land, w, m, v)


def _row_tile(r, cap):
    t = min(r, cap)
    while r % t or t % 8:
        t -= 8
    return t


def kernel(x, norm_ffn1_g, ffn1_w_gate, ffn1_w_up, ffn1_w_down, norm_mix_g, w_in, rel_bias, w_out_att, w_out_ret, w_out, norm_ffn2_g, ffn2_w_gate, ffn2_w_up, ffn2_w_down, norm_final_g, loss_target, m_norm_ffn1_g, m_ffn1_w_gate, m_ffn1_w_up, m_ffn1_w_down, m_norm_mix_g, m_w_in, m_rel_bias, m_w_out_att, m_w_out_ret, m_w_out, m_norm_ffn2_g, m_ffn2_w_gate, m_ffn2_w_up, m_ffn2_w_down, m_norm_final_g, v_norm_ffn1_g, v_ffn1_w_gate, v_ffn1_w_up, v_ffn1_w_down, v_norm_mix_g, v_w_in, v_rel_bias, v_w_out_att, v_w_out_ret, v_w_out, v_norm_ffn2_g, v_ffn2_w_gate, v_ffn2_w_up, v_ffn2_w_down, v_norm_final_g):
    T, D = x.shape[1], x.shape[2]
    A = w_out_att.shape[1]
    H = A // HEAD_DIM
    nf = ffn1_w_gate.shape[2]
    nfp = _round_up(nf, LANE)
    nin = w_in.shape[2]
    nd = w_out.shape[1]
    assert nin % LANE == 0 and nd % LANE == 0 and (7 * A) % nd == 0 and T % ATT_TQ == 0
    tm = min(512, T)
    tn = min(256, T)
    x0 = x[0]
    tgt = loss_target[0]

    def colpad(w):
        return jnp.pad(w[0].astype(CDT), ((0, 0), (0, nfp - nf)))

    def rowpad(w):
        return jnp.pad(w[0].astype(CDT), ((0, nfp - nf), (0, 0)))

    shards = [colpad(ffn1_w_gate), colpad(ffn1_w_up), rowpad(ffn1_w_down), w_in[0].astype(CDT),
              w_out_att[0].astype(CDT), w_out_ret[0].astype(CDT), w_out[0].astype(CDT),
              colpad(ffn2_w_gate), colpad(ffn2_w_up), rowpad(ffn2_w_down)]
    Wg1, Wu1, Wd1, Win, Woa, Wor, Wo, Wg2, Wu2, Wd2 = all_gather_blocks(shards)

    def swiglu(prods, _):
        a, b = prods
        return a, b, a * _sigmoid(a) * b

    def ffn_fwd(tag, xin, g, Wg, Wu, Wd):
        h = rmsnorm_fwd(tag + "_norm", xin, g, tn)
        a, b, mid = mm_block(tag + "_up", T, tm, [(h, "full", D, 0)], [(Wg, 0, False), (Wu, 0, False)],
                             [], [("3d", nfp, CDT)] * 3, swiglu)
        xo = mm_reduce_j(tag + "_down", T, tm, [(mid, "3d", nfp, 0, Wd, False)], D, F32, res=xin, scale=0.5)
        return h, a, b, mid, xo

    h1, a1, b1, mid1, x1 = ffn_fwd("ffn1", x0, norm_ffn1_g, Wg1, Wu1, Wd1)
    h2 = rmsnorm_fwd("mix_norm", x1, norm_mix_g, tn)
    proj, = mm_block("in_proj", T, tm, [(h2, "full", D, 0)], [(Win, 0, False)], [], [("col", nin, CDT)],
                     lambda p, _: p)
    biasm = attn_bias(rel_bias[0], ATT_TQ)
    att = attn_fwd(proj, biasm, A, ATT_TQ)
    tables = _retention_tables(T, H, RET_BLK)
    retg, ret_raw, states = ret_fwd(proj, tables, A, RET_BLK)
    goff = 7 * A // nd

    def merge(prods, ex):
        ba, br = prods
        ga, gr = ex
        return ba, br, _sigmoid(ga) * ba + _sigmoid(gr) * br

    ba, br, merged = mm_block(
        "branches", T, tm, [(att, "full", A, 0), (retg, "full", A, 0)], [(Woa, 0, False), (Wor, 1, False)],
        [(proj, "col", nd, goff), (proj, "col", nd, goff + N_DEV)], [("col", nd, CDT)] * 3, merge)
    x2 = mm_reduce_j("out_proj", T, tm, [(merged, "col", nd, 0, Wo, False)], D, F32, res=x1, scale=1.0)
    h3, a2, b2, mid2, x3 = ffn_fwd("ffn2", x2, norm_ffn2_g, Wg2, Wu2, Wd2)

    dx3, dx3h, dgf, loss_part = loss_head(x3, norm_final_g.reshape(1, D), tgt, tn)

    def swiglu_bwd(prods, ex):
        dm, = prods
        a, b = ex
        sg = _sigmoid(a)
        return dm * b * (sg * (1.0 + a * (1.0 - sg))), dm * (a * sg)

    def ffn_bwd(tag, dxh, h, a, b, mid, Wg, Wu, Wd):
        da, db = mm_block(tag + "_down_bwd", T, tm, [(dxh, "full", D, 0)], [(Wd, 0, True)],
                          [(a, "3d", nfp, 0), (b, "3d", nfp, 0)], [("3d", nfp, CDT)] * 2, swiglu_bwd)
        dWd = mm_reduce_i(tag + "_dwd", T, tm, (mid, "3d", nfp, 0), (dxh, "full", D, 0))
        dWg = mm_reduce_i(tag + "_dwg", T, tm, (h, "full", D, 0), (da, "3d", nfp, 0))
        dWu = mm_reduce_i(tag + "_dwu", T, tm, (h, "full", D, 0), (db, "3d", nfp, 0))
        dh = mm_reduce_j(tag + "_up_bwd", T, tm, [(da, "3d", nfp, 0, Wg, True), (db, "3d", nfp, 0, Wu, True)],
                         D, F32)
        return dh, dWg, dWu, dWd

    dh3, dWg2, dWu2, dWd2 = ffn_bwd("ffn2", dx3h, h3, a2, b2, mid2, Wg2, Wu2, Wd2)
    dx2, dx2c, dg2 = rmsnorm_bwd("ffn2_norm_bwd", x2, norm_ffn2_g, dh3, dx3, 1.0, tn)

    def merge_bwd(prods, ex):
        dmg, = prods
        ba_, br_, ga, gr = ex
        sa, sr = _sigmoid(ga), _sigmoid(gr)
        return dmg * sa, dmg * sr, dmg * ba_ * sa * (1.0 - sa), dmg * br_ * sr * (1.0 - sr)

    dba, dbr, dga, dgr = mm_block(
        "out_proj_bwd", T, tm, [(dx2c, "full", D, 0)], [(Wo, 0, True)],
        [(ba, "col", nd, 0), (br, "col", nd, 0), (proj, "col", nd, goff), (proj, "col", nd, goff + N_DEV)],
        [("col", nd, CDT)] * 4, merge_bwd)
    dWo = mm_reduce_i("dwo", T, tm, (merged, "col", nd, 0), (dx2c, "full", D, 0))
    datt = mm_reduce_j("att_out_bwd", T, tm, [(dba, "col", nd, 0, Woa, True)], A, CDT)
    dretg = mm_reduce_j("ret_out_bwd", T, tm, [(dbr, "col", nd, 0, Wor, True)], A, CDT)
    dWoa = mm_reduce_i("dwoa", T, tm, (att, "full", A, 0), (dba, "col", nd, 0))
    dWor = mm_reduce_i("dwor", T, tm, (retg, "full", A, 0), (dbr, "col", nd, 0))
    dq_r, dk_r, dv_r, dg_r = ret_bwd(proj, tables, ret_raw, states, dretg, A, RET_BLK)
    dq_a, dk_a, dv_a, dst = attn_bwd(proj, biasm, datt, A, ATT_TQ)
    dbias = jnp.pad(attn_bias_grad(dst, ATT_TQ), ((0, 0), (0, N_REL_PAD - N_REL)))
    dproj = jnp.concatenate([dq_a, dk_a, dv_a, dq_r, dk_r, dv_r, dg_r, dga, dgr], axis=1)
    dWin = mm_reduce_i("dwin", T, tm, (h2, "full", D, 0), (dproj, "col", nin, 0))
    dh2 = mm_reduce_j("in_proj_bwd", T, tm, [(dproj, "col", nin, 0, Win, True)], D, F32)
    dx1, dx1h, dgm = rmsnorm_bwd("mix_norm_bwd", x1, norm_mix_g, dh2, dx2, 0.5, tn)
    dh1, dWg1, dWu1, dWd1 = ffn_bwd("ffn1", dx1h, h1, a1, b1, mid1, Wg1, Wu1, Wd1)
    grad_x, _, dg1 = rmsnorm_bwd("ffn1_norm_bwd", x0, norm_ffn1_g, dh1, dx1, 1.0, tn)

    dgains = jnp.concatenate([dg1, dgm, dg2, dgf, jnp.zeros((4, D), F32)], axis=0)
    landed = exchange_blocks([dWg1, dWu1, dWd1, dWin, dWoa, dWor, dWo, dWg2, dWu2, dWd2], [dgains, dbias])
    lWg1, lWu1, lWd1, lWin, lWoa, lWor, lWo, lWg2, lWu2, lWd2, lgains, lbias = landed

    def upd(name, land, w, m, v):
        r = w.shape[1]
        outs = reduce_adamw(name, land, w[0], m[0], v[0], _row_tile(r, 256))
        return [o[None] for o in outs]

    res = {
        "ffn1_w_gate": upd("adamw_wg1", lWg1, ffn1_w_gate, m_ffn1_w_gate, v_ffn1_w_gate),
        "ffn1_w_up": upd("adamw_wu1", lWu1, ffn1_w_up, m_ffn1_w_up, v_ffn1_w_up),
        "ffn1_w_down": upd("adamw_wd1", lWd1, ffn1_w_down, m_ffn1_w_down, v_ffn1_w_down),
        "w_in": upd("adamw_win", lWin, w_in, m_w_in, v_w_in),
        "w_out_att": upd("adamw_woa", lWoa, w_out_att, m_w_out_att, v_w_out_att),
        "w_out_ret": upd("adamw_wor", lWor, w_out_ret, m_w_out_ret, v_w_out_ret),
        "w_out": upd("adamw_wo", lWo, w_out, m_w_out, v_w_out),
        "ffn2_w_gate": upd("adamw_wg2", lWg2, ffn2_w_gate, m_ffn2_w_gate, v_ffn2_w_gate),
        "ffn2_w_up": upd("adamw_wu2", lWu2, ffn2_w_up, m_ffn2_w_up, v_ffn2_w_up),
        "ffn2_w_down": upd("adamw_wd2", lWd2, ffn2_w_down, m_ffn2_w_down, v_ffn2_w_down),
    }

    def stack_gains(a, b, c_, d):
        return jnp.concatenate([a, b, c_, d.reshape(1, D), jnp.zeros((4, D), F32)], axis=0)

    gw = stack_gains(norm_ffn1_g, norm_mix_g, norm_ffn2_g, norm_final_g)
    gm = stack_gains(m_norm_ffn1_g, m_norm_mix_g, m_norm_ffn2_g, m_norm_final_g)
    gv = stack_gains(v_norm_ffn1_g, v_norm_mix_g, v_norm_ffn2_g, v_norm_final_g)
    gains = reduce_adamw("adamw_gains", lgains, gw, gm, gv, 8)

    def padb(t):
        return jnp.pad(t[0], ((0, 0), (0, N_REL_PAD - N_REL)))

    bias = [o[:, :N_REL][None] for o in
            reduce_adamw("adamw_bias", lbias, padb(rel_bias), padb(m_rel_bias), padb(v_rel_bias), H)]
    res["norm_ffn1_g"] = [o[0:1] for o in gains]
    res["norm_mix_g"] = [o[1:2] for o in gains]
    res["norm_ffn2_g"] = [o[2:3] for o in gains]
    res["norm_final_g"] = [o[3] for o in gains]
    res["rel_bias"] = bias

    loss = lax.psum(loss_part[0, 0], MESH_AXES)
    names = ["norm_ffn1_g", "ffn1_w_gate", "ffn1_w_up", "ffn1_w_down", "norm_mix_g", "w_in", "rel_bias",
             "w_out_att", "w_out_ret", "w_out", "norm_ffn2_g", "ffn2_w_gate", "ffn2_w_up", "ffn2_w_down",
             "norm_final_g"]
    out = [loss, grad_x[None]]
    for k in range(4):
        out += [res[nm][k] for nm in names]
    return tuple(out)
```

```python
import functools
import math

import jax
import jax.numpy as jnp
import numpy as np
from jax import lax
from jax.experimental import pallas as pl
from jax.experimental.pallas import tpu as pltpu

F32 = jnp.float32
CDT = jnp.bfloat16

N_DEV = 8
CHUNK = 64
N_PREV_CHUNKS = 8
BAND = N_PREV_CHUNKS * CHUNK
HEAD_DIM = 128
MAX_REL_DIST = 128
N_REL = 2 * MAX_REL_DIST + 1
N_REL_PAD = 384
ROPE_BASE = 10000.0
EPS = 1e-6
NEG = -1e30
LANE = 128
ATT_TQ = 256
RET_BLK = 256
VMEM_LIMIT = 48 * 1024 * 1024

ADAM_LR = 0.001
ADAM_B1 = 0.9
ADAM_B2 = 0.999
ADAM_EPS = 1e-08
ADAM_WD = 0.01
ADAM_STEP = 10

MESH_AXES = ("x", "y", "c")
_NT = (((1,), (1,)), ((), ()))
_TN = (((0,), (0,)), ((), ()))


def _round_up(v, m):
    return (v + m - 1) // m * m


def _params(sem=None):
    return pltpu.CompilerParams(dimension_semantics=sem, vmem_limit_bytes=VMEM_LIMIT)


def _sigmoid(v):
    return 1.0 / (1.0 + jnp.exp(-v))


def _bspec(kind, tm, w, off, order):
    def wrap(f):
        if order == "ji":
            return lambda j, i: f(i, j)
        return lambda i, j: f(i, j)
    if kind == "full":
        return pl.BlockSpec((tm, w), wrap(lambda i, j: (i, 0)))
    if kind == "col":
        return pl.BlockSpec((tm, w), wrap(lambda i, j: (i, j + off)))
    assert kind == "3d"
    return pl.BlockSpec((None, tm, w), wrap(lambda i, j: (j, i, 0)))


def _wspec(w, order):
    if order == "ji":
        return pl.BlockSpec((None,) + w.shape[1:], lambda j, i: (j, 0, 0))
    return pl.BlockSpec((None,) + w.shape[1:], lambda i, j: (j, 0, 0))


def _width(arr, kind, w):
    return arr.shape[-1] if kind in ("full", "3d") else w


def mm_block(name, T, tm, lhs, wts, extras, outs, epilogue):
    nl, nw, ne = len(lhs), len(wts), len(extras)
    ni = T // tm

    def body(*refs):
        l = refs[:nl]
        w = refs[nl:nl + nw]
        e = refs[nl + nw:nl + nw + ne]
        o = refs[nl + nw + ne:]
        prods = []
        for k, (_, li, tr) in enumerate(wts):
            a = l[li][...]
            if tr:
                prods.append(lax.dot_general(a, w[k][...], _NT, preferred_element_type=F32))
            else:
                prods.append(jnp.dot(a, w[k][...], preferred_element_type=F32))
        res = epilogue(prods, [r[...].astype(F32) for r in e])
        for r, val in zip(o, res):
            r[...] = val.astype(r.dtype)

    in_specs = [_bspec(k, tm, _width(a, k, w), off, "ji") for (a, k, w, off) in lhs]
    in_specs += [_wspec(w, "ji") for (w, _, _) in wts]
    in_specs += [_bspec(k, tm, _width(a, k, w), off, "ji") for (a, k, w, off) in extras]
    out_specs, out_shape = [], []
    for (kind, w, dt) in outs:
        out_specs.append(_bspec(kind, tm, w, 0, "ji"))
        if kind == "3d":
            out_shape.append(jax.ShapeDtypeStruct((N_DEV, T, w), dt))
        else:
            out_shape.append(jax.ShapeDtypeStruct((T, N_DEV * w), dt))
    args = [a for (a, _, _, _) in lhs] + [w for (w, _, _) in wts] + [a for (a, _, _, _) in extras]
    return pl.pallas_call(
        body, name=name, grid=(N_DEV, ni), in_specs=in_specs, out_specs=out_specs,
        out_shape=out_shape, compiler_params=_params(("parallel", "parallel")))(*args)


def mm_reduce_j(name, T, tm, pairs, out_w, out_dtype, res=None, scale=1.0, after=()):
    np_ = len(pairs)
    ni = T // tm

    def body(*refs):
        xs = refs[:np_]
        ws = refs[np_:2 * np_]
        rest = refs[2 * np_:len(refs) - 2 - len(after)] + refs[len(refs) - 2:]
        if res is not None:
            res_ref, o_ref, acc = rest
        else:
            o_ref, acc = rest
        j = pl.program_id(1)

        @pl.when(j == 0)
        def _():
            acc[...] = jnp.zeros_like(acc)

        for k, p in enumerate(pairs):
            if p[5]:
                acc[...] += lax.dot_general(xs[k][...], ws[k][...], _NT, preferred_element_type=F32)
            else:
                acc[...] += jnp.dot(xs[k][...], ws[k][...], preferred_element_type=F32)

        @pl.when(j == N_DEV - 1)
        def _():
            if res is not None:
                o_ref[...] = (res_ref[...] + scale * acc[...]).astype(o_ref.dtype)
            else:
                o_ref[...] = acc[...].astype(o_ref.dtype)

    in_specs = [_bspec(k, tm, _width(x, k, w), off, "ij") for (x, k, w, off, _, _) in pairs]
    in_specs += [_wspec(p[4], "ij") for p in pairs]
    args = [p[0] for p in pairs] + [p[4] for p in pairs]
    if res is not None:
        in_specs.append(pl.BlockSpec((tm, out_w), lambda i, j: (i, 0)))
        args.append(res)
    in_specs += [_ANY] * len(after)
    args += list(after)
    return pl.pallas_call(
        body, name=name, grid=(ni, N_DEV), in_specs=in_specs,
        out_specs=pl.BlockSpec((tm, out_w), lambda i, j: (i, 0)),
        out_shape=jax.ShapeDtypeStruct((T, out_w), out_dtype),
        scratch_shapes=[pltpu.VMEM((tm, out_w), F32)],
        compiler_params=_params(("parallel", "arbitrary")))(*args)


def mm_reduce_i(name, T, tm, a, b):
    ni = T // tm
    rows = _width(a[0], a[1], a[2])
    cols = _width(b[0], b[1], b[2])

    def body(a_ref, b_ref, o_ref, acc):
        i = pl.program_id(1)

        @pl.when(i == 0)
        def _():
            acc[...] = jnp.zeros_like(acc)

        acc[...] += lax.dot_general(a_ref[...], b_ref[...], _TN, preferred_element_type=F32)

        @pl.when(i == ni - 1)
        def _():
            o_ref[...] = acc[...].astype(o_ref.dtype)

    return pl.pallas_call(
        body, name=name, grid=(N_DEV, ni),
        in_specs=[_bspec(a[1], tm, rows, a[3], "ji"), _bspec(b[1], tm, cols, b[3], "ji")],
        out_specs=pl.BlockSpec((None, rows, cols), lambda j, i: (j, 0, 0)),
        out_shape=jax.ShapeDtypeStruct((N_DEV, rows, cols), CDT),
        scratch_shapes=[pltpu.VMEM((rows, cols), F32)],
        compiler_params=_params(("parallel", "arbitrary")))(a[0], b[0])


def _rms_bwd_math(xv, g, dy):
    r = lax.rsqrt(jnp.mean(xv * xv, axis=-1, keepdims=True) + EPS)
    xn = xv * r
    dxn = dy * g
    dx = r * (dxn - xn * jnp.mean(dxn * xn, axis=-1, keepdims=True))
    dg = jnp.sum(dy * xn, axis=0, keepdims=True)
    return dx, dg


def rmsnorm_fwd(name, x, g, tm, after=()):
    T, D = x.shape

    def body(x_ref, g_ref, *rest):
        o_ref = rest[-1]
        xv = x_ref[...]
        r = lax.rsqrt(jnp.mean(xv * xv, axis=-1, keepdims=True) + EPS)
        o_ref[...] = (xv * r * g_ref[...]).astype(o_ref.dtype)

    return pl.pallas_call(
        body, name=name, grid=(T // tm,),
        in_specs=[pl.BlockSpec((tm, D), lambda i: (i, 0)), pl.BlockSpec((1, D), lambda i: (0, 0))]
        + [_ANY] * len(after),
        out_specs=pl.BlockSpec((tm, D), lambda i: (i, 0)),
        out_shape=jax.ShapeDtypeStruct((T, D), CDT),
        compiler_params=_params(("parallel",)))(x, g, *after)


def rmsnorm_bwd(name, x, g, dh, dres, cscale, tm):
    T, D = x.shape

    def body(x_ref, g_ref, dh_ref, dres_ref, dx_ref, dxc_ref, dg_ref):
        i = pl.program_id(0)
        dx, dg = _rms_bwd_math(x_ref[...], g_ref[...], dh_ref[...])
        dx = dres_ref[...] + dx
        dx_ref[...] = dx
        dxc_ref[...] = (cscale * dx).astype(dxc_ref.dtype)

        @pl.when(i == 0)
        def _():
            dg_ref[...] = jnp.zeros_like(dg_ref)

        dg_ref[...] += dg

    row = pl.BlockSpec((tm, D), lambda i: (i, 0))
    vec = pl.BlockSpec((1, D), lambda i: (0, 0))
    return pl.pallas_call(
        body, name=name, grid=(T // tm,), in_specs=[row, vec, row, row],
        out_specs=[row, row, vec],
        out_shape=[jax.ShapeDtypeStruct((T, D), F32), jax.ShapeDtypeStruct((T, D), CDT),
                   jax.ShapeDtypeStruct((1, D), F32)],
        compiler_params=_params(("arbitrary",)))(x, g, dh, dres)


def loss_head(x, g, tgt, tm):
    T, D = x.shape

    def body(x_ref, g_ref, t_ref, dx_ref, dxc_ref, dg_ref, loss_ref):
        i = pl.program_id(0)
        xv = x_ref[...]
        gv = g_ref[...]
        r = lax.rsqrt(jnp.mean(xv * xv, axis=-1, keepdims=True) + EPS)
        err = xv * r * gv - t_ref[...]
        part = jnp.sum(jnp.mean(err * err, axis=-1, keepdims=True), axis=0, keepdims=True)
        dx, dg = _rms_bwd_math(xv, gv, err / D)
        dx_ref[...] = dx
        dxc_ref[...] = (0.5 * dx).astype(dxc_ref.dtype)

        @pl.when(i == 0)
        def _():
            dg_ref[...] = jnp.zeros_like(dg_ref)
            loss_ref[...] = jnp.zeros_like(loss_ref)

        dg_ref[...] += dg
        loss_ref[...] += jnp.broadcast_to(0.5 * part, loss_ref.shape)

    row = pl.BlockSpec((tm, D), lambda i: (i, 0))
    vec = pl.BlockSpec((1, D), lambda i: (0, 0))
    return pl.pallas_call(
        body, name="loss_head", grid=(T // tm,), in_specs=[row, vec, row],
        out_specs=[row, row, vec, pl.BlockSpec((1, LANE), lambda i: (0, 0))],
        out_shape=[jax.ShapeDtypeStruct((T, D), F32), jax.ShapeDtypeStruct((T, D), CDT),
                   jax.ShapeDtypeStruct((1, D), F32), jax.ShapeDtypeStruct((1, LANE), F32)],
        compiler_params=_params(("arbitrary",)))(x, g, tgt)


def _skew_rows(z, left):
    tq, kw = z.shape
    row = lax.broadcasted_iota(jnp.int32, (tq, kw), 0)
    s = 1
    while s < tq:
        z = jnp.where((row & s) != 0, pltpu.roll(z, kw - s if left else s, 1), z)
        s *= 2
    return z


REL_HI = BAND + MAX_REL_DIST
REL_LO = BAND - MAX_REL_DIST


def attn_bias(rel_bias, tq):
    H = rel_bias.shape[0]
    kw = BAND + tq
    by_skew = jnp.concatenate(
        [jnp.broadcast_to(rel_bias[:, N_REL - 1:], (H, REL_LO)), rel_bias[:, ::-1],
         jnp.broadcast_to(rel_bias[:, :1], (H, kw - REL_HI - 1))], axis=1).reshape(H, 1, kw)

    def body(t_ref, o_ref):
        t = t_ref[...]
        qi = lax.broadcasted_iota(jnp.int32, (tq, kw), 0)
        kj = lax.broadcasted_iota(jnp.int32, (tq, kw), 1)
        b = _skew_rows(jnp.broadcast_to(t, (tq, kw)), left=False)
        b = jnp.where(kj < qi, t[:, 0:1], b)
        qc = qi // CHUNK
        kc = kj // CHUNK - N_PREV_CHUNKS
        valid = (kc <= qc) & (kc >= qc - N_PREV_CHUNKS)
        o_ref[...] = jnp.where(valid, b, NEG)

    return pl.pallas_call(
        body, name="attn_bias", grid=(H,),
        in_specs=[pl.BlockSpec((None, 1, kw), lambda h: (h, 0, 0))],
        out_specs=pl.BlockSpec((None, tq, kw), lambda h: (h, 0, 0)),
        out_shape=jax.ShapeDtypeStruct((H, tq, kw), F32),
        compiler_params=_params(("parallel",)))(by_skew)


def attn_bias_grad(dst, tq):
    H = dst.shape[0]
    kw = BAND + tq

    def body(d_ref, o_ref):
        z = _skew_rows(d_ref[...], left=True)
        qi = lax.broadcasted_iota(jnp.int32, (tq, kw), 0)
        kj = lax.broadcasted_iota(jnp.int32, (tq, kw), 1)
        wrapped = kj + qi >= kw
        c = jnp.sum(jnp.where(wrapped, 0.0, z), axis=0, keepdims=True)
        cw = jnp.sum(jnp.sum(jnp.where(wrapped, z, 0.0), axis=0, keepdims=True), axis=1, keepdims=True)
        lane = lax.broadcasted_iota(jnp.int32, (1, kw), 1)
        ahead = jnp.sum(jnp.where(lane >= REL_HI, c, 0.0), axis=1, keepdims=True)
        behind = jnp.sum(jnp.where(lane <= REL_LO, c, 0.0), axis=1, keepdims=True) + cw
        o_ref[...] = jnp.where(lane == REL_HI, ahead, jnp.where(lane == REL_LO, behind, c))

    by_skew = pl.pallas_call(
        body, name="attn_bias_grad", grid=(H,),
        in_specs=[pl.BlockSpec((None, tq, kw), lambda h: (h, 0, 0))],
        out_specs=pl.BlockSpec((None, 1, kw), lambda h: (h, 0, 0)),
        out_shape=jax.ShapeDtypeStruct((H, 1, kw), F32),
        compiler_params=_params(("parallel",)))(dst)
    return by_skew[:, 0, REL_LO:REL_HI + 1][:, ::-1]


def _attn_scores(q, kpad, bm_ref, start, kw):
    k = kpad[pl.ds(start, kw), :]
    s = lax.dot_general(q, k, _NT, preferred_element_type=F32) * (HEAD_DIM ** -0.5) + bm_ref[...]
    col = lax.broadcasted_iota(jnp.int32, s.shape, 1)
    s = jnp.where(col < BAND - start, NEG, s)
    m = jnp.max(s, axis=-1, keepdims=True)
    p = jnp.exp(s - m)
    return p / jnp.sum(p, axis=-1, keepdims=True), k


def _fill_padded(pad_ref, src_ref, T):
    pad_ref[pl.ds(0, BAND), :] = jnp.zeros((BAND, HEAD_DIM), pad_ref.dtype)
    pad_ref[pl.ds(BAND, T), :] = src_ref[...]


def attn_fwd(proj, biasm, A, tq):
    T = proj.shape[0]
    H = A // HEAD_DIM
    kw = BAND + tq

    def body(q_ref, k_ref, v_ref, bm_ref, o_ref, kpad, vpad):
        qi = pl.program_id(1)

        @pl.when(qi == 0)
        def _():
            _fill_padded(kpad, k_ref, T)
            _fill_padded(vpad, v_ref, T)

        start = pl.multiple_of(qi * tq, tq)
        p, _ = _attn_scores(q_ref[...], kpad, bm_ref, start, kw)
        v = vpad[pl.ds(start, kw), :]
        o_ref[...] = jnp.dot(p.astype(CDT), v, preferred_element_type=F32).astype(o_ref.dtype)

    return pl.pallas_call(
        body, name="attn_fwd", grid=(H, T // tq),
        in_specs=[pl.BlockSpec((tq, HEAD_DIM), lambda h, i: (i, h)),
                  pl.BlockSpec((T, HEAD_DIM), lambda h, i: (0, H + h)),
                  pl.BlockSpec((T, HEAD_DIM), lambda h, i: (0, 2 * H + h)),
                  pl.BlockSpec((None, tq, kw), lambda h, i: (h, 0, 0))],
        out_specs=pl.BlockSpec((tq, HEAD_DIM), lambda h, i: (i, h)),
        out_shape=jax.ShapeDtypeStruct((T, A), CDT),
        scratch_shapes=[pltpu.VMEM((BAND + T, HEAD_DIM), CDT), pltpu.VMEM((BAND + T, HEAD_DIM), CDT)],
        compiler_params=_params(("parallel", "arbitrary")))(proj, proj, proj, biasm)


def attn_bwd(proj, biasm, datt, A, tq):
    T = proj.shape[0]
    H = A // HEAD_DIM
    kw = BAND + tq
    nq = T // tq
    scale = HEAD_DIM ** -0.5

    def body(q_ref, k_ref, v_ref, bm_ref, do_ref, dq_ref, dk_ref, dv_ref, dst_ref,
             kpad, vpad, dkacc, dvacc):
        qi = pl.program_id(1)

        @pl.when(qi == 0)
        def _():
            _fill_padded(kpad, k_ref, T)
            _fill_padded(vpad, v_ref, T)
            dkacc[...] = jnp.zeros_like(dkacc)
            dvacc[...] = jnp.zeros_like(dvacc)
            dst_ref[...] = jnp.zeros_like(dst_ref)

        start = pl.multiple_of(qi * tq, tq)
        q = q_ref[...]
        p, k = _attn_scores(q, kpad, bm_ref, start, kw)
        v = vpad[pl.ds(start, kw), :]
        do = do_ref[...]
        dp = lax.dot_general(do, v, _NT, preferred_element_type=F32)
        ds = p * (dp - jnp.sum(dp * p, axis=-1, keepdims=True))
        dst_ref[...] += ds
        dsb = ds.astype(CDT)
        dq_ref[...] = (jnp.dot(dsb, k, preferred_element_type=F32) * scale).astype(dq_ref.dtype)
        dkacc[pl.ds(start, kw), :] += lax.dot_general(dsb, q, _TN, preferred_element_type=F32) * scale
        dvacc[pl.ds(start, kw), :] += lax.dot_general(p.astype(CDT), do, _TN, preferred_element_type=F32)

        @pl.when(qi == nq - 1)
        def _():
            dk_ref[...] = dkacc[pl.ds(BAND, T), :].astype(dk_ref.dtype)
            dv_ref[...] = dvacc[pl.ds(BAND, T), :].astype(dv_ref.dtype)

    blk = pl.BlockSpec((tq, HEAD_DIM), lambda h, i: (i, h))
    col = pl.BlockSpec((T, HEAD_DIM), lambda h, i: (0, h))
    bias = pl.BlockSpec((None, tq, kw), lambda h, i: (h, 0, 0))
    return pl.pallas_call(
        body, name="attn_bwd", grid=(H, nq),
        in_specs=[blk,
                  pl.BlockSpec((T, HEAD_DIM), lambda h, i: (0, H + h)),
                  pl.BlockSpec((T, HEAD_DIM), lambda h, i: (0, 2 * H + h)),
                  bias, blk],
        out_specs=[blk, col, col, bias],
        out_shape=[jax.ShapeDtypeStruct((T, A), CDT), jax.ShapeDtypeStruct((T, A), CDT),
                   jax.ShapeDtypeStruct((T, A), CDT), jax.ShapeDtypeStruct((H, tq, kw), F32)],
        scratch_shapes=[pltpu.VMEM((BAND + T, HEAD_DIM), CDT), pltpu.VMEM((BAND + T, HEAD_DIM), CDT),
                        pltpu.VMEM((BAND + T, HEAD_DIM), F32), pltpu.VMEM((BAND + T, HEAD_DIM), F32)],
        compiler_params=_params(("parallel", "arbitrary")))(proj, proj, proj, biasm, datt)


def _retention_tables(T, H, blk):
    half = HEAD_DIM // 2
    inv = 1.0 / (ROPE_BASE ** (jnp.arange(0, HEAD_DIM, 2, dtype=F32) / HEAD_DIM))
    ang = jnp.arange(T, dtype=F32)[:, None] * inv[None, :]
    cos, sin = jnp.cos(ang), jnp.sin(ang)
    rc = jnp.concatenate([cos, cos], axis=1)
    rs = jnp.concatenate([-sin, sin], axis=1)
    assert rc.shape == (T, 2 * half)
    log_g = jnp.log(1.0 - 2.0 ** (-5.0 - jnp.arange(H, dtype=F32)))[:, None, None]
    idx = jnp.arange(blk, dtype=F32)
    n, m = idx[:, None], idx[None, :]
    same = (n // CHUNK) == (m // CHUNK)
    earlier = (m // CHUNK) < (n // CHUNK)
    dist = jnp.where(same, jnp.abs(n - m), n - m)[None]
    dmat = jnp.where((same | earlier)[None], jnp.exp(log_g * dist), 0.0)
    ones = jnp.ones((1, 1, HEAD_DIM), F32)
    qd = jnp.exp(log_g * (idx[None, :, None] + 1.0)) * ones
    kd = jnp.exp(log_g * (blk - 1.0 - idx[None, :, None])) * ones
    cd = jnp.exp(log_g * blk) * jnp.ones((1, 8, HEAD_DIM), F32)
    return rc, rs, dmat, qd, kd, cd


def _rot(v, rc, rs):
    return v * rc + pltpu.roll(v, HEAD_DIM // 2, 1) * rs


def _rot_bwd(dv, rc, rs):
    return dv * rc + pltpu.roll(dv * rs, HEAD_DIM // 2, 1)


def ret_fwd(proj, tables, A, blk):
    T = proj.shape[0]
    H = A // HEAD_DIM
    nb = T // blk
    rc, rs, dmat, qd, kd, cd = tables
    scale = HEAD_DIM ** -0.5

    def body(q_ref, k_ref, v_ref, g_ref, rc_ref, rs_ref, d_ref, qd_ref, kd_ref, cd_ref,
             y_ref, o_ref, st_ref, state):
        b = pl.program_id(1)

        @pl.when(b == 0)
        def _():
            state[...] = jnp.zeros_like(state)

        c, s = rc_ref[...], rs_ref[...]
        qs = (_rot(q_ref[...].astype(F32), c, s) * scale).astype(CDT)
        kr = _rot(k_ref[...].astype(F32), c, s)
        v = v_ref[...]
        sb = state[...].astype(CDT)
        a = lax.dot_general(qs, kr.astype(CDT), _NT, preferred_element_type=F32) * d_ref[...]
        o = jnp.dot(a.astype(CDT), v, preferred_element_type=F32)
        o = o + jnp.dot(qs, sb, preferred_element_type=F32) * qd_ref[...]
        st_ref[...] = sb
        state[...] = state[...] * cd_ref[0:1, :] + lax.dot_general(
            (kr * kd_ref[...]).astype(CDT), v, _TN, preferred_element_type=F32)
        o_ref[...] = o
        on = o * lax.rsqrt(jnp.mean(o * o, axis=-1, keepdims=True) + EPS)
        g = g_ref[...].astype(F32)
        y_ref[...] = (g * _sigmoid(g) * on).astype(y_ref.dtype)

    def pj(off):
        return pl.BlockSpec((blk, HEAD_DIM), lambda h, i: (i, off * H + h))

    tok = pl.BlockSpec((blk, HEAD_DIM), lambda h, i: (i, 0))
    out = pl.BlockSpec((blk, HEAD_DIM), lambda h, i: (i, h))

    def per_head(r):
        return pl.BlockSpec((None, r, HEAD_DIM), lambda h, i: (h, 0, 0))

    return pl.pallas_call(
        body, name="ret_fwd", grid=(H, nb),
        in_specs=[pj(3), pj(4), pj(5), pj(6), tok, tok,
                  pl.BlockSpec((None, blk, blk), lambda h, i: (h, 0, 0)),
                  per_head(blk), per_head(blk), per_head(8)],
        out_specs=[out, out, pl.BlockSpec((None, None, HEAD_DIM, HEAD_DIM), lambda h, i: (h, i, 0, 0))],
        out_shape=[jax.ShapeDtypeStruct((T, A), CDT), jax.ShapeDtypeStruct((T, A), F32),
                   jax.ShapeDtypeStruct((H, nb, HEAD_DIM, HEAD_DIM), CDT)],
        scratch_shapes=[pltpu.VMEM((HEAD_DIM, HEAD_DIM), F32)],
        compiler_params=_params(("parallel", "arbitrary")))(
            proj, proj, proj, proj, rc, rs, dmat, qd, kd, cd)


def ret_bwd(proj, tables, o_raw, states, dy, A, blk):
    T = proj.shape[0]
    H = A // HEAD_DIM
    nb = T // blk
    rc, rs, dmat, qd, kd, cd = tables
    scale = HEAD_DIM ** -0.5

    def body(q_ref, k_ref, v_ref, g_ref, rc_ref, rs_ref, d_ref, qd_ref, kd_ref, cd_ref,
             o_ref, st_ref, dy_ref, dq_ref, dk_ref, dv_ref, dg_ref, dstate):
        b = pl.program_id(1)

        @pl.when(b == 0)
        def _():
            dstate[...] = jnp.zeros_like(dstate)

        c, s = rc_ref[...], rs_ref[...]
        qs = (_rot(q_ref[...].astype(F32), c, s) * scale).astype(CDT)
        kr = _rot(k_ref[...].astype(F32), c, s)
        krb = kr.astype(CDT)
        kdb = (kr * kd_ref[...]).astype(CDT)
        v = v_ref[...]
        dmat_v = d_ref[...]
        a = lax.dot_general(qs, krb, _NT, preferred_element_type=F32) * dmat_v

        o = o_ref[...]
        r = lax.rsqrt(jnp.mean(o * o, axis=-1, keepdims=True) + EPS)
        on = o * r
        g = g_ref[...].astype(F32)
        sg = _sigmoid(g)
        dyv = dy_ref[...].astype(F32)
        dg_ref[...] = (dyv * on * (sg * (1.0 + g * (1.0 - sg)))).astype(dg_ref.dtype)
        don = dyv * (g * sg)
        do = r * (don - on * jnp.mean(don * on, axis=-1, keepdims=True))
        dob = do.astype(CDT)
        doq = (do * qd_ref[...]).astype(CDT)
        dsb = dstate[...].astype(CDT)

        dv = lax.dot_general(a.astype(CDT), dob, _TN, preferred_element_type=F32)
        dv = dv + jnp.dot(kdb, dsb, preferred_element_type=F32)
        dv_ref[...] = dv.astype(dv_ref.dtype)
        dpb = (lax.dot_general(dob, v, _NT, preferred_element_type=F32) * dmat_v).astype(CDT)
        dqs = jnp.dot(dpb, krb, preferred_element_type=F32)
        dqs = dqs + lax.dot_general(doq, st_ref[...], _NT, preferred_element_type=F32)
        dkr = lax.dot_general(dpb, qs, _TN, preferred_element_type=F32)
        dkr = dkr + lax.dot_general(v, dsb, _NT, preferred_element_type=F32) * kd_ref[...]
        dstate[...] = dstate[...] * cd_ref[0:1, :] + lax.dot_general(
            qs, doq, _TN, preferred_element_type=F32)
        dq_ref[...] = _rot_bwd(dqs * scale, c, s).astype(dq_ref.dtype)
        dk_ref[...] = _rot_bwd(dkr, c, s).astype(dk_ref.dtype)

    def pj(off):
        return pl.BlockSpec((blk, HEAD_DIM), lambda h, i: (nb - 1 - i, off * H + h))

    tok = pl.BlockSpec((blk, HEAD_DIM), lambda h, i: (nb - 1 - i, 0))
    out = pl.BlockSpec((blk, HEAD_DIM), lambda h, i: (nb - 1 - i, h))

    def per_head(r, w):
        return pl.BlockSpec((None, r, w), lambda h, i: (h, 0, 0))

    shp = jax.ShapeDtypeStruct((T, A), CDT)
    return pl.pallas_call(
        body, name="ret_bwd", grid=(H, nb),
        in_specs=[pj(3), pj(4), pj(5), pj(6), tok, tok, per_head(blk, blk),
                  per_head(blk, HEAD_DIM), per_head(blk, HEAD_DIM), per_head(8, HEAD_DIM),
                  out, pl.BlockSpec((None, None, HEAD_DIM, HEAD_DIM), lambda h, i: (h, nb - 1 - i, 0, 0)),
                  out],
        out_specs=[out, out, out, out], out_shape=[shp, shp, shp, shp],
        scratch_shapes=[pltpu.VMEM((HEAD_DIM, HEAD_DIM), F32)],
        compiler_params=_params(("parallel", "arbitrary")))(
            proj, proj, proj, proj, rc, rs, dmat, qd, kd, cd, o_raw, states, dy)


def _mesh_pos():
    return lax.axis_index("x"), lax.axis_index("y"), lax.axis_index("c")


def _flat(pos):
    return 4 * pos[0] + 2 * pos[1] + pos[2]


_HBM = pl.BlockSpec(memory_space=pltpu.HBM)


def all_gather_blocks(shards):
    n = len(shards)

    def body(*refs):
        ins, outs = refs[:n], refs[n:2 * n]
        send_sems, recv_sems, local_sems = refs[2 * n:]
        x, y, c = _mesh_pos()
        me, sibling = (x, y, c), (x, y, 1 - c)
        chips = [(1 - x, y), (x, 1 - y), (1 - x, 1 - y)]

        def copy(t, k, block, to, src=None):
            dst = outs[t].at[_flat(block)]
            return pltpu.make_async_remote_copy(
                src_ref=dst if src is None else src, dst_ref=dst,
                send_sem=send_sems.at[t, k], recv_sem=recv_sems.at[t, k],
                device_id=to, device_id_type=pl.DeviceIdType.MESH)

        mine, first, passed = [], [], []
        for t in range(n):
            cp = pltpu.make_async_copy(ins[t], outs[t].at[_flat(me)], local_sems.at[t])
            cp.start()
            mine.append(cp)
            row = [copy(t, 0, me, sibling, src=ins[t])]
            row += [copy(t, 1 + j, me, (*chip, c), src=ins[t]) for j, chip in enumerate(chips)]
            for cp in row:
                cp.start()
            first += row
        for t in range(n):
            for j, chip in enumerate(chips):
                copy(t, 1 + j, (*chip, c), me).wait_recv()
                cp = copy(t, 4 + j, (*chip, c), sibling)
                cp.start()
                passed.append(cp)
        for t in range(n):
            copy(t, 0, sibling, me).wait_recv()
            for j, chip in enumerate(chips):
                copy(t, 4 + j, (*chip, 1 - c), me).wait_recv()
        for cp in first + passed:
            cp.wait_send()
        for cp in mine:
            cp.wait()

    return pl.pallas_call(
        body, name="all_gather_weights",
        in_specs=[_HBM] * n, out_specs=[_HBM] * n,
        out_shape=[jax.ShapeDtypeStruct((N_DEV,) + s.shape, s.dtype) for s in shards],
        scratch_shapes=[pltpu.SemaphoreType.DMA((n, 7)), pltpu.SemaphoreType.DMA((n, 7)),
                        pltpu.SemaphoreType.DMA((n,))],
        )(*shards)


def exchange_blocks(blocked, whole):
    nb_, nw_ = len(blocked), len(whole)
    n = nb_ + nw_

    def body(*refs):
        ins, outs = refs[:n], refs[n:2 * n]
        send_sems, recv_sems, local_sems = refs[2 * n:]
        x, y, c = _mesh_pos()
        me = (x, y, c)
        copies, locals_ = [], []
        for t in range(n):
            def src(pos):
                return ins[t].at[_flat(pos)] if t < nb_ else ins[t]
            cp = pltpu.make_async_copy(src(me), outs[t].at[_flat(me)], local_sems.at[t])
            cp.start()
            locals_.append(cp)
            for k in range(1, N_DEV):
                peer = tuple(1 - v if bit else v for v, bit in zip(me, (k >> 2, (k >> 1) & 1, k & 1)))
                send = pltpu.make_async_remote_copy(
                    src_ref=src(peer), dst_ref=outs[t].at[_flat(me)],
                    send_sem=send_sems.at[t, k - 1], recv_sem=recv_sems.at[t, k - 1],
                    device_id=peer, device_id_type=pl.DeviceIdType.MESH)
                send.start()
                recv = pltpu.make_async_remote_copy(
                    src_ref=src(peer), dst_ref=outs[t].at[_flat(peer)],
                    send_sem=send_sems.at[t, k - 1], recv_sem=recv_sems.at[t, k - 1],
                    device_id=peer, device_id_type=pl.DeviceIdType.MESH)
                copies.append((send, recv))
        for send, recv in copies:
            recv.wait_recv()
        for send, recv in copies:
            send.wait_send()
        for cp in locals_:
            cp.wait()

    out_shape = [jax.ShapeDtypeStruct(a.shape, a.dtype) for a in blocked]
    out_shape += [jax.ShapeDtypeStruct((N_DEV,) + a.shape, a.dtype) for a in whole]
    return pl.pallas_call(
        body, name="exchange_grads",
        in_specs=[_HBM] * n, out_specs=[_HBM] * n, out_shape=out_shape,
        scratch_shapes=[pltpu.SemaphoreType.DMA((n, 7)), pltpu.SemaphoreType.DMA((n, 7)),
                        pltpu.SemaphoreType.DMA((n,))],
        )(*blocked, *whole)


_SEM = pl.BlockSpec(memory_space=pltpu.SEMAPHORE)
_ANY = pl.BlockSpec(memory_space=pl.ANY)
_EFFECT = pltpu.SideEffectType.DATAFLOW_SIDE_EFFECTING


def _peer(me, k):
    return tuple(1 - v if bit else v for v, bit in zip(me, (k >> 2, (k >> 1) & 1, k & 1)))


def _sem_index(t, k):
    return t * (N_DEV - 1) + k - 1


def exchange_start(name, blocked, whole, after):
    nb_ = len(blocked)
    srcs = list(blocked) + list(whole)
    n, na = len(srcs), len(after)
    lands = [lax.empty(a.shape, a.dtype) for a in blocked]
    lands += [lax.empty((N_DEV,) + a.shape, a.dtype) for a in whole]

    def body(*refs):
        ins, ls = refs[:n], refs[n:2 * n]
        send_sems, recv_sems = refs[2 * n + na], refs[2 * n + na + 1]
        token, local_sems = refs[-2], refs[-1]
        me = _mesh_pos()
        locals_ = []
        for t in range(n):
            def src(pos):
                return ins[t].at[_flat(pos)] if t < nb_ else ins[t]
            cp = pltpu.make_async_copy(src(me), ls[t].at[_flat(me)], local_sems.at[t])
            cp.start()
            locals_.append(cp)
            for k in range(1, N_DEV):
                peer = _peer(me, k)
                pltpu.make_async_remote_copy(
                    src_ref=src(peer), dst_ref=ls[t].at[_flat(me)],
                    send_sem=send_sems.at[_sem_index(t, k)], recv_sem=recv_sems.at[_sem_index(t, k)],
                    device_id=peer, device_id_type=pl.DeviceIdType.MESH).start()
        for cp in locals_:
            cp.wait()
        token[...] = jnp.zeros_like(token)

    out_shape = [pltpu.SemaphoreType.DMA((n * (N_DEV - 1),)), pltpu.SemaphoreType.DMA((n * (N_DEV - 1),))]
    out_shape += [pltpu.HBM(a.shape, a.dtype) for a in srcs + lands]
    out_shape.append(jax.ShapeDtypeStruct((8, LANE), F32))
    args = [pltpu.with_memory_space_constraint(a, pltpu.HBM) for a in srcs + lands] + list(after)
    outs = pl.pallas_call(
        body, name=name, out_shape=out_shape,
        in_specs=[_HBM] * (2 * n) + [_ANY] * na,
        out_specs=[_SEM, _SEM] + [_HBM] * (2 * n) + [pl.BlockSpec(memory_space=pltpu.VMEM)],
        input_output_aliases={i: 2 + i for i in range(2 * n)},
        scratch_shapes=[pltpu.SemaphoreType.DMA((n,))],
        compiler_params=pltpu.CompilerParams(has_side_effects=_EFFECT))(*args)
    return outs[0], outs[1], list(outs[2:2 + n]), list(outs[2 + n:2 + 2 * n]), outs[-1]


def exchange_wait(name, nb_, started, after):
    send_sems, recv_sems, srcs, lands, _ = started
    n, na = len(srcs), len(after)

    def body(*refs):
        ins, ls = refs[:n], refs[n:2 * n]
        send_sems_ref, recv_sems_ref = refs[2 * n], refs[2 * n + 1]
        me = _mesh_pos()
        for t in range(n):
            for k in range(1, N_DEV):
                peer = _peer(me, k)
                cp = pltpu.make_async_remote_copy(
                    src_ref=ins[t].at[_flat(peer)] if t < nb_ else ins[t], dst_ref=ls[t].at[_flat(peer)],
                    send_sem=send_sems_ref.at[_sem_index(t, k)], recv_sem=recv_sems_ref.at[_sem_index(t, k)],
                    device_id=peer, device_id_type=pl.DeviceIdType.MESH)
                cp.wait_send()
                cp.wait_recv()

    outs = pl.pallas_call(
        body, name=name, out_shape=[pltpu.HBM(a.shape, a.dtype) for a in srcs + lands],
        in_specs=[_HBM] * (2 * n) + [_SEM, _SEM] + [_ANY] * na, out_specs=[_HBM] * (2 * n),
        input_output_aliases={i: i for i in range(2 * n)},
        compiler_params=pltpu.CompilerParams(has_side_effects=_EFFECT))(
            *srcs, *lands, send_sems, recv_sems, *after)
    return list(outs[n:])


def _adamw_math(w, g, m, v):
    m = ADAM_B1 * m + (1.0 - ADAM_B1) * g
    v = ADAM_B2 * v + (1.0 - ADAM_B2) * (g * g)
    m_hat = m / (1.0 - ADAM_B1 ** ADAM_STEP)
    v_hat = v / (1.0 - ADAM_B2 ** ADAM_STEP)
    delta = -ADAM_LR * (m_hat / (jnp.sqrt(v_hat) + ADAM_EPS) + ADAM_WD * w)
    return delta, m, v


def reduce_adamw(name, land, w, m, v, tr):
    R, C = w.shape
    Cp = land.shape[2]

    def body(l_ref, w_ref, m_ref, v_ref, g_ref, d_ref, nm_ref, nv_ref):
        g = l_ref[0, :, 0:C].astype(F32)
        for s in range(1, N_DEV):
            g = g + l_ref[s, :, 0:C].astype(F32)
        delta, nm, nv = _adamw_math(w_ref[...], g, m_ref[...], v_ref[...])
        g_ref[...] = g
        d_ref[...] = delta
        nm_ref[...] = nm
        nv_ref[...] = nv

    blk = pl.BlockSpec((tr, C), lambda i: (i, 0))
    shp = jax.ShapeDtypeStruct((R, C), F32)
    return pl.pallas_call(
        body, name=name, grid=(R // tr,),
        in_specs=[pl.BlockSpec((N_DEV, tr, Cp), lambda i: (0, i, 0)), blk, blk, blk],
        out_specs=[blk, blk, blk, blk], out_shape=[shp, shp, shp, shp],
        compiler_params=_params(("parallel",)))(land, w, m, v)


def _row_tile(r, cap):
    t = min(r, cap)
    while r % t or t % 8:
        t -= 8
    return t


def kernel(x, norm_ffn1_g, ffn1_w_gate, ffn1_w_up, ffn1_w_down, norm_mix_g, w_in, rel_bias, w_out_att, w_out_ret, w_out, norm_ffn2_g, ffn2_w_gate, ffn2_w_up, ffn2_w_down, norm_final_g, loss_target, m_norm_ffn1_g, m_ffn1_w_gate, m_ffn1_w_up, m_ffn1_w_down, m_norm_mix_g, m_w_in, m_rel_bias, m_w_out_att, m_w_out_ret, m_w_out, m_norm_ffn2_g, m_ffn2_w_gate, m_ffn2_w_up, m_ffn2_w_down, m_norm_final_g, v_norm_ffn1_g, v_ffn1_w_gate, v_ffn1_w_up, v_ffn1_w_down, v_norm_mix_g, v_w_in, v_rel_bias, v_w_out_att, v_w_out_ret, v_w_out, v_norm_ffn2_g, v_ffn2_w_gate, v_ffn2_w_up, v_ffn2_w_down, v_norm_final_g):
    T, D = x.shape[1], x.shape[2]
    A = w_out_att.shape[1]
    H = A // HEAD_DIM
    nf = ffn1_w_gate.shape[2]
    nfp = _round_up(nf, LANE)
    nin = w_in.shape[2]
    nd = w_out.shape[1]
    assert nin % LANE == 0 and nd % LANE == 0 and (7 * A) % nd == 0 and T % ATT_TQ == 0
    tm = min(512, T)
    tn = min(256, T)
    x0 = x[0]
    tgt = loss_target[0]

    def colpad(w):
        return jnp.pad(w[0].astype(CDT), ((0, 0), (0, nfp - nf)))

    def rowpad(w):
        return jnp.pad(w[0].astype(CDT), ((0, nfp - nf), (0, 0)))

    Wg1, Wu1 = all_gather_blocks([colpad(ffn1_w_gate), colpad(ffn1_w_up)])
    later = [("wd1", [rowpad(ffn1_w_down)]), ("win", [w_in[0].astype(CDT)]),
             ("wout", [w_out_att[0].astype(CDT), w_out_ret[0].astype(CDT), w_out[0].astype(CDT)]),
             ("wgu2", [colpad(ffn2_w_gate), colpad(ffn2_w_up)]), ("wd2", [rowpad(ffn2_w_down)])]
    ag_started = {}
    order = Wg1
    for tag, ws in later:
        ag_started[tag] = exchange_start("ag_start_" + tag, [], ws, [order])
        order = ag_started[tag][4]

    def gathered(tag, after):
        return exchange_wait("ag_wait_" + tag, 0, ag_started[tag], [after])

    def swiglu(prods, _):
        a, b = prods
        return a, b, a * _sigmoid(a) * b

    def ffn_fwd(tag, xin, g, get_wgu, get_wd, after=()):
        h = rmsnorm_fwd(tag + "_norm", xin, g, tn, after)
        Wg, Wu = get_wgu(h)
        a, b, mid = mm_block(tag + "_up", T, tm, [(h, "full", D, 0)], [(Wg, 0, False), (Wu, 0, False)],
                             [], [("3d", nfp, CDT)] * 3, swiglu)
        Wd, = get_wd(mid)
        xo = mm_reduce_j(tag + "_down", T, tm, [(mid, "3d", nfp, 0, Wd, False)], D, F32, res=xin, scale=0.5)
        return h, a, b, mid, xo, (Wg, Wu, Wd)

    h1, a1, b1, mid1, x1, (_, _, Wd1) = ffn_fwd(
        "ffn1", x0, norm_ffn1_g, lambda h: (Wg1, Wu1), lambda mid: gathered("wd1", mid), after=[order])
    h2 = rmsnorm_fwd("mix_norm", x1, norm_mix_g, tn)
    Win, = gathered("win", h2)
    proj, = mm_block("in_proj", T, tm, [(h2, "full", D, 0)], [(Win, 0, False)], [], [("col", nin, CDT)],
                     lambda p, _: p)
    biasm = attn_bias(rel_bias[0], ATT_TQ)
    att = attn_fwd(proj, biasm, A, ATT_TQ)
    tables = _retention_tables(T, H, RET_BLK)
    retg, ret_raw, states = ret_fwd(proj, tables, A, RET_BLK)
    Woa, Wor, Wo = gathered("wout", retg)
    goff = 7 * A // nd

    def merge(prods, ex):
        ba, br = prods
        ga, gr = ex
        return ba, br, _sigmoid(ga) * ba + _sigmoid(gr) * br

    ba, br, merged = mm_block(
        "branches", T, tm, [(att, "full", A, 0), (retg, "full", A, 0)], [(Woa, 0, False), (Wor, 1, False)],
        [(proj, "col", nd, goff), (proj, "col", nd, goff + N_DEV)], [("col", nd, CDT)] * 3, merge)
    x2 = mm_reduce_j("out_proj", T, tm, [(merged, "col", nd, 0, Wo, False)], D, F32, res=x1, scale=1.0)
    h3, a2, b2, mid2, x3, (Wg2, Wu2, Wd2) = ffn_fwd(
        "ffn2", x2, norm_ffn2_g, lambda h: gathered("wgu2", h), lambda mid: gathered("wd2", mid))

    dx3, dx3h, dgf, loss_part = loss_head(x3, norm_final_g.reshape(1, D), tgt, tn)

    def swiglu_bwd(prods, ex):
        dm, = prods
        a, b = ex
        sg = _sigmoid(a)
        return dm * b * (sg * (1.0 + a * (1.0 - sg))), dm * (a * sg)

    def ffn_bwd(tag, dxh, h, a, b, mid, Wg, Wu, Wd):
        da, db = mm_block(tag + "_down_bwd", T, tm, [(dxh, "full", D, 0)], [(Wd, 0, True)],
                          [(a, "3d", nfp, 0), (b, "3d", nfp, 0)], [("3d", nfp, CDT)] * 2, swiglu_bwd)
        dWd = mm_reduce_i(tag + "_dwd", T, tm, (mid, "3d", nfp, 0), (dxh, "full", D, 0))
        dWg = mm_reduce_i(tag + "_dwg", T, tm, (h, "full", D, 0), (da, "3d", nfp, 0))
        dWu = mm_reduce_i(tag + "_dwu", T, tm, (h, "full", D, 0), (db, "3d", nfp, 0))
        sent = exchange_start("rs_start_" + tag, [dWg, dWu, dWd], [], [])
        dh = mm_reduce_j(tag + "_up_bwd", T, tm, [(da, "3d", nfp, 0, Wg, True), (db, "3d", nfp, 0, Wu, True)],
                         D, F32, after=[sent[4]])
        return dh, sent

    dh3, sent_ffn2 = ffn_bwd("ffn2", dx3h, h3, a2, b2, mid2, Wg2, Wu2, Wd2)
    dx2, dx2c, dg2 = rmsnorm_bwd("ffn2_norm_bwd", x2, norm_ffn2_g, dh3, dx3, 1.0, tn)

    def merge_bwd(prods, ex):
        dmg, = prods
        ba_, br_, ga, gr = ex
        sa, sr = _sigmoid(ga), _sigmoid(gr)
        return dmg * sa, dmg * sr, dmg * ba_ * sa * (1.0 - sa), dmg * br_ * sr * (1.0 - sr)

    dba, dbr, dga, dgr = mm_block(
        "out_proj_bwd", T, tm, [(dx2c, "full", D, 0)], [(Wo, 0, True)],
        [(ba, "col", nd, 0), (br, "col", nd, 0), (proj, "col", nd, goff), (proj, "col", nd, goff + N_DEV)],
        [("col", nd, CDT)] * 4, merge_bwd)
    dWo = mm_reduce_i("dwo", T, tm, (merged, "col", nd, 0), (dx2c, "full", D, 0))
    dWoa = mm_reduce_i("dwoa", T, tm, (att, "full", A, 0), (dba, "col", nd, 0))
    dWor = mm_reduce_i("dwor", T, tm, (retg, "full", A, 0), (dbr, "col", nd, 0))
    sent_mix = exchange_start("rs_start_mix", [dWoa, dWor, dWo], [], [])
    datt = mm_reduce_j("att_out_bwd", T, tm, [(dba, "col", nd, 0, Woa, True)], A, CDT, after=[sent_mix[4]])
    dretg = mm_reduce_j("ret_out_bwd", T, tm, [(dbr, "col", nd, 0, Wor, True)], A, CDT)
    dq_r, dk_r, dv_r, dg_r = ret_bwd(proj, tables, ret_raw, states, dretg, A, RET_BLK)
    dq_a, dk_a, dv_a, dst = attn_bwd(proj, biasm, datt, A, ATT_TQ)
    dbias = jnp.pad(attn_bias_grad(dst, ATT_TQ), ((0, 0), (0, N_REL_PAD - N_REL)))
    dproj = jnp.concatenate([dq_a, dk_a, dv_a, dq_r, dk_r, dv_r, dg_r, dga, dgr], axis=1)
    dWin = mm_reduce_i("dwin", T, tm, (h2, "full", D, 0), (dproj, "col", nin, 0))
    sent_win = exchange_start("rs_start_win", [dWin], [], [])
    dh2 = mm_reduce_j("in_proj_bwd", T, tm, [(dproj, "col", nin, 0, Win, True)], D, F32, after=[sent_win[4]])
    dx1, dx1h, dgm = rmsnorm_bwd("mix_norm_bwd", x1, norm_mix_g, dh2, dx2, 0.5, tn)
    dh1, sent_ffn1 = ffn_bwd("ffn1", dx1h, h1, a1, b1, mid1, Wg1, Wu1, Wd1)
    grad_x, _, dg1 = rmsnorm_bwd("ffn1_norm_bwd", x0, norm_ffn1_g, dh1, dx1, 1.0, tn)

    dgains = jnp.concatenate([dg1, dgm, dg2, dgf, jnp.zeros((4, D), F32)], axis=0)
    lgains, lbias = exchange_blocks([], [dgains, dbias])

    def upd(name, land, w, m, v):
        r = w.shape[1]
        outs = reduce_adamw(name, land, w[0], m[0], v[0], _row_tile(r, 256))
        return [o[None] for o in outs]

    res = {}
    lWg2, lWu2, lWd2 = exchange_wait("rs_wait_ffn2", 3, sent_ffn2, [lgains])
    res["ffn2_w_gate"] = upd("adamw_wg2", lWg2, ffn2_w_gate, m_ffn2_w_gate, v_ffn2_w_gate)
    res["ffn2_w_up"] = upd("adamw_wu2", lWu2, ffn2_w_up, m_ffn2_w_up, v_ffn2_w_up)
    res["ffn2_w_down"] = upd("adamw_wd2", lWd2, ffn2_w_down, m_ffn2_w_down, v_ffn2_w_down)
    lWoa, lWor, lWo = exchange_wait("rs_wait_mix", 3, sent_mix, [res["ffn2_w_down"][1]])
    res["w_out_att"] = upd("adamw_woa", lWoa, w_out_att, m_w_out_att, v_w_out_att)
    res["w_out_ret"] = upd("adamw_wor", lWor, w_out_ret, m_w_out_ret, v_w_out_ret)
    res["w_out"] = upd("adamw_wo", lWo, w_out, m_w_out, v_w_out)
    lWin, = exchange_wait("rs_wait_win", 1, sent_win, [res["w_out"][1]])
    res["w_in"] = upd("adamw_win", lWin, w_in, m_w_in, v_w_in)
    lWg1, lWu1, lWd1 = exchange_wait("rs_wait_ffn1", 3, sent_ffn1, [res["w_in"][1]])
    res["ffn1_w_gate"] = upd("adamw_wg1", lWg1, ffn1_w_gate, m_ffn1_w_gate, v_ffn1_w_gate)
    res["ffn1_w_up"] = upd("adamw_wu1", lWu1, ffn1_w_up, m_ffn1_w_up, v_ffn1_w_up)
    res["ffn1_w_down"] = upd("adamw_wd1", lWd1, ffn1_w_down, m_ffn1_w_down, v_ffn1_w_down)

    def stack_gains(a, b, c_, d):
        return jnp.concatenate([a, b, c_, d.reshape(1, D), jnp.zeros((4, D), F32)], axis=0)

    gw = stack_gains(norm_ffn1_g, norm_mix_g, norm_ffn2_g, norm_final_g)
    gm = stack_gains(m_norm_ffn1_g, m_norm_mix_g, m_norm_ffn2_g, m_norm_final_g)
    gv = stack_gains(v_norm_ffn1_g, v_norm_mix_g, v_norm_ffn2_g, v_norm_final_g)
    gains = reduce_adamw("adamw_gains", lgains, gw, gm, gv, 8)

    def padb(t):
        return jnp.pad(t[0], ((0, 0), (0, N_REL_PAD - N_REL)))

    bias = [o[:, :N_REL][None] for o in
            reduce_adamw("adamw_bias", lbias, padb(rel_bias), padb(m_rel_bias), padb(v_rel_bias), H)]
    res["norm_ffn1_g"] = [o[0:1] for o in gains]
    res["norm_mix_g"] = [o[1:2] for o in gains]
    res["norm_ffn2_g"] = [o[2:3] for o in gains]
    res["norm_final_g"] = [o[3] for o in gains]
    res["rel_bias"] = bias

    loss = lax.psum(loss_part[0, 0], MESH_AXES)
    names = ["norm_ffn1_g", "ffn1_w_gate", "ffn1_w_up", "ffn1_w_down", "norm_mix_g", "w_in", "rel_bias",
             "w_out_att", "w_out_ret", "w_out", "norm_ffn2_g", "ffn2_w_gate", "ffn2_w_up", "ffn2_w_down",
             "norm_final_g"]
    out = [loss, grad_x[None]]
    for k in range(4):
        out += [res[nm][k] for nm in names]
    return tuple(out)
```

```python
import functools
import math

import jax
import jax.numpy as jnp
import numpy as np
from jax import lax
from jax.experimental import pallas as pl
from jax.experimental.pallas import tpu as pltpu

F32 = jnp.float32
CDT = jnp.bfloat16

N_DEV = 8
CHUNK = 64
N_PREV_CHUNKS = 8
BAND = N_PREV_CHUNKS * CHUNK
HEAD_DIM = 128
MAX_REL_DIST = 128
N_REL = 2 * MAX_REL_DIST + 1
N_REL_PAD = 384
ROPE_BASE = 10000.0
EPS = 1e-6
NEG = -1e30
LANE = 128
ATT_TQ = 256
RET_BLK = 256
VMEM_LIMIT = 48 * 1024 * 1024

ADAM_LR = 0.001
ADAM_B1 = 0.9
ADAM_B2 = 0.999
ADAM_EPS = 1e-08
ADAM_WD = 0.01
ADAM_STEP = 10

MESH_AXES = ("x", "y", "c")
_NT = (((1,), (1,)), ((), ()))
_TN = (((0,), (0,)), ((), ()))


def _round_up(v, m):
    return (v + m - 1) // m * m


def _params(sem=None):
    return pltpu.CompilerParams(dimension_semantics=sem, vmem_limit_bytes=VMEM_LIMIT)


def _sigmoid(v):
    return 1.0 / (1.0 + jnp.exp(-v))


def _bspec(kind, tm, w, off, order):
    def wrap(f):
        if order == "ji":
            return lambda j, i: f(i, j)
        return lambda i, j: f(i, j)
    if kind == "full":
        return pl.BlockSpec((tm, w), wrap(lambda i, j: (i, 0)))
    if kind == "col":
        return pl.BlockSpec((tm, w), wrap(lambda i, j: (i, j + off)))
    assert kind == "3d"
    return pl.BlockSpec((None, tm, w), wrap(lambda i, j: (j, i, 0)))


def _wspec(w, order):
    if order == "ji":
        return pl.BlockSpec((None,) + w.shape[1:], lambda j, i: (j, 0, 0))
    return pl.BlockSpec((None,) + w.shape[1:], lambda i, j: (j, 0, 0))


def _width(arr, kind, w):
    return arr.shape[-1] if kind in ("full", "3d") else w


def mm_block(name, T, tm, lhs, wts, extras, outs, epilogue):
    nl, nw, ne = len(lhs), len(wts), len(extras)
    ni = T // tm

    def body(*refs):
        l = refs[:nl]
        w = refs[nl:nl + nw]
        e = refs[nl + nw:nl + nw + ne]
        o = refs[nl + nw + ne:]
        prods = []
        for k, (_, li, tr) in enumerate(wts):
            a = l[li][...]
            if tr:
                prods.append(lax.dot_general(a, w[k][...], _NT, preferred_element_type=F32))
            else:
                prods.append(jnp.dot(a, w[k][...], preferred_element_type=F32))
        res = epilogue(prods, [r[...].astype(F32) for r in e])
        for r, val in zip(o, res):
            r[...] = val.astype(r.dtype)

    in_specs = [_bspec(k, tm, _width(a, k, w), off, "ji") for (a, k, w, off) in lhs]
    in_specs += [_wspec(w, "ji") for (w, _, _) in wts]
    in_specs += [_bspec(k, tm, _width(a, k, w), off, "ji") for (a, k, w, off) in extras]
    out_specs, out_shape = [], []
    for (kind, w, dt) in outs:
        out_specs.append(_bspec(kind, tm, w, 0, "ji"))
        if kind == "3d":
            out_shape.append(jax.ShapeDtypeStruct((N_DEV, T, w), dt))
        else:
            out_shape.append(jax.ShapeDtypeStruct((T, N_DEV * w), dt))
    args = [a for (a, _, _, _) in lhs] + [w for (w, _, _) in wts] + [a for (a, _, _, _) in extras]
    return pl.pallas_call(
        body, name=name, grid=(N_DEV, ni), in_specs=in_specs, out_specs=out_specs,
        out_shape=out_shape, compiler_params=_params(("parallel", "parallel")))(*args)


def mm_reduce_j(name, T, tm, pairs, out_w, out_dtype, res=None, scale=1.0, after=()):
    np_ = len(pairs)
    ni = T // tm

    def body(*refs):
        xs = refs[:np_]
        ws = refs[np_:2 * np_]
        rest = refs[2 * np_:len(refs) - 2 - len(after)] + refs[len(refs) - 2:]
        if res is not None:
            res_ref, o_ref, acc = rest
        else:
            o_ref, acc = rest
        j = pl.program_id(1)

        @pl.when(j == 0)
        def _():
            acc[...] = jnp.zeros_like(acc)

        for k, p in enumerate(pairs):
            if p[5]:
                acc[...] += lax.dot_general(xs[k][...], ws[k][...], _NT, preferred_element_type=F32)
            else:
                acc[...] += jnp.dot(xs[k][...], ws[k][...], preferred_element_type=F32)

        @pl.when(j == N_DEV - 1)
        def _():
            if res is not None:
                o_ref[...] = (res_ref[...] + scale * acc[...]).astype(o_ref.dtype)
            else:
                o_ref[...] = acc[...].astype(o_ref.dtype)

    in_specs = [_bspec(k, tm, _width(x, k, w), off, "ij") for (x, k, w, off, _, _) in pairs]
    in_specs += [_wspec(p[4], "ij") for p in pairs]
    args = [p[0] for p in pairs] + [p[4] for p in pairs]
    if res is not None:
        in_specs.append(pl.BlockSpec((tm, out_w), lambda i, j: (i, 0)))
        args.append(res)
    in_specs += [_ANY] * len(after)
    args += list(after)
    return pl.pallas_call(
        body, name=name, grid=(ni, N_DEV), in_specs=in_specs,
        out_specs=pl.BlockSpec((tm, out_w), lambda i, j: (i, 0)),
        out_shape=jax.ShapeDtypeStruct((T, out_w), out_dtype),
        scratch_shapes=[pltpu.VMEM((tm, out_w), F32)],
        compiler_params=_params(("parallel", "arbitrary")))(*args)


def mm_reduce_i(name, T, tm, a, b):
    ni = T // tm
    rows = _width(a[0], a[1], a[2])
    cols = _width(b[0], b[1], b[2])

    def body(a_ref, b_ref, o_ref, acc):
        i = pl.program_id(1)

        @pl.when(i == 0)
        def _():
            acc[...] = jnp.zeros_like(acc)

        acc[...] += lax.dot_general(a_ref[...], b_ref[...], _TN, preferred_element_type=F32)

        @pl.when(i == ni - 1)
        def _():
            o_ref[...] = acc[...].astype(o_ref.dtype)

    return pl.pallas_call(
        body, name=name, grid=(N_DEV, ni),
        in_specs=[_bspec(a[1], tm, rows, a[3], "ji"), _bspec(b[1], tm, cols, b[3], "ji")],
        out_specs=pl.BlockSpec((None, rows, cols), lambda j, i: (j, 0, 0)),
        out_shape=jax.ShapeDtypeStruct((N_DEV, rows, cols), CDT),
        scratch_shapes=[pltpu.VMEM((rows, cols), F32)],
        compiler_params=_params(("parallel", "arbitrary")))(a[0], b[0])


def _rms_bwd_math(xv, g, dy):
    r = lax.rsqrt(jnp.mean(xv * xv, axis=-1, keepdims=True) + EPS)
    xn = xv * r
    dxn = dy * g
    dx = r * (dxn - xn * jnp.mean(dxn * xn, axis=-1, keepdims=True))
    dg = jnp.sum(dy * xn, axis=0, keepdims=True)
    return dx, dg


def rmsnorm_fwd(name, x, g, tm, after=()):
    T, D = x.shape

    def body(x_ref, g_ref, *rest):
        o_ref = rest[-1]
        xv = x_ref[...]
        r = lax.rsqrt(jnp.mean(xv * xv, axis=-1, keepdims=True) + EPS)
        o_ref[...] = (xv * r * g_ref[...]).astype(o_ref.dtype)

    return pl.pallas_call(
        body, name=name, grid=(T // tm,),
        in_specs=[pl.BlockSpec((tm, D), lambda i: (i, 0)), pl.BlockSpec((1, D), lambda i: (0, 0))]
        + [_ANY] * len(after),
        out_specs=pl.BlockSpec((tm, D), lambda i: (i, 0)),
        out_shape=jax.ShapeDtypeStruct((T, D), CDT),
        compiler_params=_params(("parallel",)))(x, g, *after)


def rmsnorm_bwd(name, x, g, dh, dres, cscale, tm):
    T, D = x.shape

    def body(x_ref, g_ref, dh_ref, dres_ref, dx_ref, dxc_ref, dg_ref):
        i = pl.program_id(0)
        dx, dg = _rms_bwd_math(x_ref[...], g_ref[...], dh_ref[...])
        dx = dres_ref[...] + dx
        dx_ref[...] = dx
        dxc_ref[...] = (cscale * dx).astype(dxc_ref.dtype)

        @pl.when(i == 0)
        def _():
            dg_ref[...] = jnp.zeros_like(dg_ref)

        dg_ref[...] += dg

    row = pl.BlockSpec((tm, D), lambda i: (i, 0))
    vec = pl.BlockSpec((1, D), lambda i: (0, 0))
    return pl.pallas_call(
        body, name=name, grid=(T // tm,), in_specs=[row, vec, row, row],
        out_specs=[row, row, vec],
        out_shape=[jax.ShapeDtypeStruct((T, D), F32), jax.ShapeDtypeStruct((T, D), CDT),
                   jax.ShapeDtypeStruct((1, D), F32)],
        compiler_params=_params(("arbitrary",)))(x, g, dh, dres)


def loss_head(x, g, tgt, tm):
    T, D = x.shape

    def body(x_ref, g_ref, t_ref, dx_ref, dxc_ref, dg_ref, loss_ref):
        i = pl.program_id(0)
        xv = x_ref[...]
        gv = g_ref[...]
        r = lax.rsqrt(jnp.mean(xv * xv, axis=-1, keepdims=True) + EPS)
        err = xv * r * gv - t_ref[...]
        part = jnp.sum(jnp.mean(err * err, axis=-1, keepdims=True), axis=0, keepdims=True)
        dx, dg = _rms_bwd_math(xv, gv, err / D)
        dx_ref[...] = dx
        dxc_ref[...] = (0.5 * dx).astype(dxc_ref.dtype)

        @pl.when(i == 0)
        def _():
            dg_ref[...] = jnp.zeros_like(dg_ref)
            loss_ref[...] = jnp.zeros_like(loss_ref)

        dg_ref[...] += dg
        loss_ref[...] += jnp.broadcast_to(0.5 * part, loss_ref.shape)

    row = pl.BlockSpec((tm, D), lambda i: (i, 0))
    vec = pl.BlockSpec((1, D), lambda i: (0, 0))
    return pl.pallas_call(
        body, name="loss_head", grid=(T // tm,), in_specs=[row, vec, row],
        out_specs=[row, row, vec, pl.BlockSpec((1, LANE), lambda i: (0, 0))],
        out_shape=[jax.ShapeDtypeStruct((T, D), F32), jax.ShapeDtypeStruct((T, D), CDT),
                   jax.ShapeDtypeStruct((1, D), F32), jax.ShapeDtypeStruct((1, LANE), F32)],
        compiler_params=_params(("arbitrary",)))(x, g, tgt)


def _skew_rows(z, left):
    tq, kw = z.shape
    row = lax.broadcasted_iota(jnp.int32, (tq, kw), 0)
    s = 1
    while s < tq:
        z = jnp.where((row & s) != 0, pltpu.roll(z, kw - s if left else s, 1), z)
        s *= 2
    return z


REL_HI = BAND + MAX_REL_DIST
REL_LO = BAND - MAX_REL_DIST


def attn_bias(rel_bias, tq):
    H = rel_bias.shape[0]
    kw = BAND + tq
    by_skew = jnp.concatenate(
        [jnp.broadcast_to(rel_bias[:, N_REL - 1:], (H, REL_LO)), rel_bias[:, ::-1],
         jnp.broadcast_to(rel_bias[:, :1], (H, kw - REL_HI - 1))], axis=1).reshape(H, 1, kw)

    def body(t_ref, o_ref):
        t = t_ref[...]
        qi = lax.broadcasted_iota(jnp.int32, (tq, kw), 0)
        kj = lax.broadcasted_iota(jnp.int32, (tq, kw), 1)
        b = _skew_rows(jnp.broadcast_to(t, (tq, kw)), left=False)
        b = jnp.where(kj < qi, t[:, 0:1], b)
        qc = qi // CHUNK
        kc = kj // CHUNK - N_PREV_CHUNKS
        valid = (kc <= qc) & (kc >= qc - N_PREV_CHUNKS)
        o_ref[...] = jnp.where(valid, b, NEG)

    return pl.pallas_call(
        body, name="attn_bias", grid=(H,),
        in_specs=[pl.BlockSpec((None, 1, kw), lambda h: (h, 0, 0))],
        out_specs=pl.BlockSpec((None, tq, kw), lambda h: (h, 0, 0)),
        out_shape=jax.ShapeDtypeStruct((H, tq, kw), F32),
        compiler_params=_params(("parallel",)))(by_skew)


def attn_bias_grad(dst, tq):
    H = dst.shape[0]
    kw = BAND + tq

    def body(d_ref, o_ref):
        z = _skew_rows(d_ref[...], left=True)
        qi = lax.broadcasted_iota(jnp.int32, (tq, kw), 0)
        kj = lax.broadcasted_iota(jnp.int32, (tq, kw), 1)
        wrapped = kj + qi >= kw
        c = jnp.sum(jnp.where(wrapped, 0.0, z), axis=0, keepdims=True)
        cw = jnp.sum(jnp.sum(jnp.where(wrapped, z, 0.0), axis=0, keepdims=True), axis=1, keepdims=True)
        lane = lax.broadcasted_iota(jnp.int32, (1, kw), 1)
        ahead = jnp.sum(jnp.where(lane >= REL_HI, c, 0.0), axis=1, keepdims=True)
        behind = jnp.sum(jnp.where(lane <= REL_LO, c, 0.0), axis=1, keepdims=True) + cw
        o_ref[...] = jnp.where(lane == REL_HI, ahead, jnp.where(lane == REL_LO, behind, c))

    by_skew = pl.pallas_call(
        body, name="attn_bias_grad", grid=(H,),
        in_specs=[pl.BlockSpec((None, tq, kw), lambda h: (h, 0, 0))],
        out_specs=pl.BlockSpec((None, 1, kw), lambda h: (h, 0, 0)),
        out_shape=jax.ShapeDtypeStruct((H, 1, kw), F32),
        compiler_params=_params(("parallel",)))(dst)
    return by_skew[:, 0, REL_LO:REL_HI + 1][:, ::-1]


def _attn_scores(q, kpad, bm_ref, start, kw):
    k = kpad[pl.ds(start, kw), :]
    s = lax.dot_general(q, k, _NT, preferred_element_type=F32) * (HEAD_DIM ** -0.5) + bm_ref[...]
    col = lax.broadcasted_iota(jnp.int32, s.shape, 1)
    s = jnp.where(col < BAND - start, NEG, s)
    m = jnp.max(s, axis=-1, keepdims=True)
    p = jnp.exp(s - m)
    return p / jnp.sum(p, axis=-1, keepdims=True), k


def _fill_padded(pad_ref, src_ref, T):
    pad_ref[pl.ds(0, BAND), :] = jnp.zeros((BAND, HEAD_DIM), pad_ref.dtype)
    pad_ref[pl.ds(BAND, T), :] = src_ref[...]


def attn_fwd(proj, biasm, A, tq):
    T = proj.shape[0]
    H = A // HEAD_DIM
    kw = BAND + tq

    def body(q_ref, k_ref, v_ref, bm_ref, o_ref, kpad, vpad):
        qi = pl.program_id(1)

        @pl.when(qi == 0)
        def _():
            _fill_padded(kpad, k_ref, T)
            _fill_padded(vpad, v_ref, T)

        start = pl.multiple_of(qi * tq, tq)
        p, _ = _attn_scores(q_ref[...], kpad, bm_ref, start, kw)
        v = vpad[pl.ds(start, kw), :]
        o_ref[...] = jnp.dot(p.astype(CDT), v, preferred_element_type=F32).astype(o_ref.dtype)

    return pl.pallas_call(
        body, name="attn_fwd", grid=(H, T // tq),
        in_specs=[pl.BlockSpec((tq, HEAD_DIM), lambda h, i: (i, h)),
                  pl.BlockSpec((T, HEAD_DIM), lambda h, i: (0, H + h)),
                  pl.BlockSpec((T, HEAD_DIM), lambda h, i: (0, 2 * H + h)),
                  pl.BlockSpec((None, tq, kw), lambda h, i: (h, 0, 0))],
        out_specs=pl.BlockSpec((tq, HEAD_DIM), lambda h, i: (i, h)),
        out_shape=jax.ShapeDtypeStruct((T, A), CDT),
        scratch_shapes=[pltpu.VMEM((BAND + T, HEAD_DIM), CDT), pltpu.VMEM((BAND + T, HEAD_DIM), CDT)],
        compiler_params=_params(("parallel", "arbitrary")))(proj, proj, proj, biasm)


def attn_bwd(proj, biasm, datt, A, tq):
    T = proj.shape[0]
    H = A // HEAD_DIM
    kw = BAND + tq
    nq = T // tq
    scale = HEAD_DIM ** -0.5

    def body(q_ref, k_ref, v_ref, bm_ref, do_ref, dq_ref, dk_ref, dv_ref, dst_ref,
             kpad, vpad, dkacc, dvacc):
        qi = pl.program_id(1)

        @pl.when(qi == 0)
        def _():
            _fill_padded(kpad, k_ref, T)
            _fill_padded(vpad, v_ref, T)
            dkacc[...] = jnp.zeros_like(dkacc)
            dvacc[...] = jnp.zeros_like(dvacc)
            dst_ref[...] = jnp.zeros_like(dst_ref)

        start = pl.multiple_of(qi * tq, tq)
        q = q_ref[...]
        p, k = _attn_scores(q, kpad, bm_ref, start, kw)
        v = vpad[pl.ds(start, kw), :]
        do = do_ref[...]
        dp = lax.dot_general(do, v, _NT, preferred_element_type=F32)
        ds = p * (dp - jnp.sum(dp * p, axis=-1, keepdims=True))
        dst_ref[...] += ds
        dsb = ds.astype(CDT)
        dq_ref[...] = (jnp.dot(dsb, k, preferred_element_type=F32) * scale).astype(dq_ref.dtype)
        dkacc[pl.ds(start, kw), :] += lax.dot_general(dsb, q, _TN, preferred_element_type=F32) * scale
        dvacc[pl.ds(start, kw), :] += lax.dot_general(p.astype(CDT), do, _TN, preferred_element_type=F32)

        @pl.when(qi == nq - 1)
        def _():
            dk_ref[...] = dkacc[pl.ds(BAND, T), :].astype(dk_ref.dtype)
            dv_ref[...] = dvacc[pl.ds(BAND, T), :].astype(dv_ref.dtype)

    blk = pl.BlockSpec((tq, HEAD_DIM), lambda h, i: (i, h))
    col = pl.BlockSpec((T, HEAD_DIM), lambda h, i: (0, h))
    bias = pl.BlockSpec((None, tq, kw), lambda h, i: (h, 0, 0))
    return pl.pallas_call(
        body, name="attn_bwd", grid=(H, nq),
        in_specs=[blk,
                  pl.BlockSpec((T, HEAD_DIM), lambda h, i: (0, H + h)),
                  pl.BlockSpec((T, HEAD_DIM), lambda h, i: (0, 2 * H + h)),
                  bias, blk],
        out_specs=[blk, col, col, bias],
        out_shape=[jax.ShapeDtypeStruct((T, A), CDT), jax.ShapeDtypeStruct((T, A), CDT),
                   jax.ShapeDtypeStruct((T, A), CDT), jax.ShapeDtypeStruct((H, tq, kw), F32)],
        scratch_shapes=[pltpu.VMEM((BAND + T, HEAD_DIM), CDT), pltpu.VMEM((BAND + T, HEAD_DIM), CDT),
                        pltpu.VMEM((BAND + T, HEAD_DIM), F32), pltpu.VMEM((BAND + T, HEAD_DIM), F32)],
        compiler_params=_params(("parallel", "arbitrary")))(proj, proj, proj, biasm, datt)


def _retention_tables(T, H, blk):
    half = HEAD_DIM // 2
    inv = 1.0 / (ROPE_BASE ** (jnp.arange(0, HEAD_DIM, 2, dtype=F32) / HEAD_DIM))
    ang = jnp.arange(T, dtype=F32)[:, None] * inv[None, :]
    cos, sin = jnp.cos(ang), jnp.sin(ang)
    rc = jnp.concatenate([cos, cos], axis=1)
    rs = jnp.concatenate([-sin, sin], axis=1)
    assert rc.shape == (T, 2 * half)
    log_g = jnp.log(1.0 - 2.0 ** (-5.0 - jnp.arange(H, dtype=F32)))[:, None, None]
    idx = jnp.arange(blk, dtype=F32)
    n, m = idx[:, None], idx[None, :]
    same = (n // CHUNK) == (m // CHUNK)
    earlier = (m // CHUNK) < (n // CHUNK)
    dist = jnp.where(same, jnp.abs(n - m), n - m)[None]
    dmat = jnp.where((same | earlier)[None], jnp.exp(log_g * dist), 0.0)
    ones = jnp.ones((1, 1, HEAD_DIM), F32)
    qd = jnp.exp(log_g * (idx[None, :, None] + 1.0)) * ones
    kd = jnp.exp(log_g * (blk - 1.0 - idx[None, :, None])) * ones
    cd = jnp.exp(log_g * blk) * jnp.ones((1, 8, HEAD_DIM), F32)
    return rc, rs, dmat, qd, kd, cd


def _rot(v, rc, rs):
    return v * rc + pltpu.roll(v, HEAD_DIM // 2, 1) * rs


def _rot_bwd(dv, rc, rs):
    return dv * rc + pltpu.roll(dv * rs, HEAD_DIM // 2, 1)


def ret_fwd(proj, tables, A, blk):
    T = proj.shape[0]
    H = A // HEAD_DIM
    nb = T // blk
    rc, rs, dmat, qd, kd, cd = tables
    scale = HEAD_DIM ** -0.5

    def body(q_ref, k_ref, v_ref, g_ref, rc_ref, rs_ref, d_ref, qd_ref, kd_ref, cd_ref,
             y_ref, o_ref, st_ref, state):
        b = pl.program_id(1)

        @pl.when(b == 0)
        def _():
            state[...] = jnp.zeros_like(state)

        c, s = rc_ref[...], rs_ref[...]
        qs = (_rot(q_ref[...].astype(F32), c, s) * scale).astype(CDT)
        kr = _rot(k_ref[...].astype(F32), c, s)
        v = v_ref[...]
        sb = state[...].astype(CDT)
        a = lax.dot_general(qs, kr.astype(CDT), _NT, preferred_element_type=F32) * d_ref[...]
        o = jnp.dot(a.astype(CDT), v, preferred_element_type=F32)
        o = o + jnp.dot(qs, sb, preferred_element_type=F32) * qd_ref[...]
        st_ref[...] = sb
        state[...] = state[...] * cd_ref[0:1, :] + lax.dot_general(
            (kr * kd_ref[...]).astype(CDT), v, _TN, preferred_element_type=F32)
        o_ref[...] = o
        on = o * lax.rsqrt(jnp.mean(o * o, axis=-1, keepdims=True) + EPS)
        g = g_ref[...].astype(F32)
        y_ref[...] = (g * _sigmoid(g) * on).astype(y_ref.dtype)

    def pj(off):
        return pl.BlockSpec((blk, HEAD_DIM), lambda h, i: (i, off * H + h))

    tok = pl.BlockSpec((blk, HEAD_DIM), lambda h, i: (i, 0))
    out = pl.BlockSpec((blk, HEAD_DIM), lambda h, i: (i, h))

    def per_head(r):
        return pl.BlockSpec((None, r, HEAD_DIM), lambda h, i: (h, 0, 0))

    return pl.pallas_call(
        body, name="ret_fwd", grid=(H, nb),
        in_specs=[pj(3), pj(4), pj(5), pj(6), tok, tok,
                  pl.BlockSpec((None, blk, blk), lambda h, i: (h, 0, 0)),
                  per_head(blk), per_head(blk), per_head(8)],
        out_specs=[out, out, pl.BlockSpec((None, None, HEAD_DIM, HEAD_DIM), lambda h, i: (h, i, 0, 0))],
        out_shape=[jax.ShapeDtypeStruct((T, A), CDT), jax.ShapeDtypeStruct((T, A), F32),
                   jax.ShapeDtypeStruct((H, nb, HEAD_DIM, HEAD_DIM), CDT)],
        scratch_shapes=[pltpu.VMEM((HEAD_DIM, HEAD_DIM), F32)],
        compiler_params=_params(("parallel", "arbitrary")))(
            proj, proj, proj, proj, rc, rs, dmat, qd, kd, cd)


def ret_bwd(proj, tables, o_raw, states, dy, A, blk):
    T = proj.shape[0]
    H = A // HEAD_DIM
    nb = T // blk
    rc, rs, dmat, qd, kd, cd = tables
    scale = HEAD_DIM ** -0.5

    def body(q_ref, k_ref, v_ref, g_ref, rc_ref, rs_ref, d_ref, qd_ref, kd_ref, cd_ref,
             o_ref, st_ref, dy_ref, dq_ref, dk_ref, dv_ref, dg_ref, dstate):
        b = pl.program_id(1)

        @pl.when(b == 0)
        def _():
            dstate[...] = jnp.zeros_like(dstate)

        c, s = rc_ref[...], rs_ref[...]
        qs = (_rot(q_ref[...].astype(F32), c, s) * scale).astype(CDT)
        kr = _rot(k_ref[...].astype(F32), c, s)
        krb = kr.astype(CDT)
        kdb = (kr * kd_ref[...]).astype(CDT)
        v = v_ref[...]
        dmat_v = d_ref[...]
        a = lax.dot_general(qs, krb, _NT, preferred_element_type=F32) * dmat_v

        o = o_ref[...]
        r = lax.rsqrt(jnp.mean(o * o, axis=-1, keepdims=True) + EPS)
        on = o * r
        g = g_ref[...].astype(F32)
        sg = _sigmoid(g)
        dyv = dy_ref[...].astype(F32)
        dg_ref[...] = (dyv * on * (sg * (1.0 + g * (1.0 - sg)))).astype(dg_ref.dtype)
        don = dyv * (g * sg)
        do = r * (don - on * jnp.mean(don * on, axis=-1, keepdims=True))
        dob = do.astype(CDT)
        doq = (do * qd_ref[...]).astype(CDT)
        dsb = dstate[...].astype(CDT)

        dv = lax.dot_general(a.astype(CDT), dob, _TN, preferred_element_type=F32)
        dv = dv + jnp.dot(kdb, dsb, preferred_element_type=F32)
        dv_ref[...] = dv.astype(dv_ref.dtype)
        dpb = (lax.dot_general(dob, v, _NT, preferred_element_type=F32) * dmat_v).astype(CDT)
        dqs = jnp.dot(dpb, krb, preferred_element_type=F32)
        dqs = dqs + lax.dot_general(doq, st_ref[...], _NT, preferred_element_type=F32)
        dkr = lax.dot_general(dpb, qs, _TN, preferred_element_type=F32)
        dkr = dkr + lax.dot_general(v, dsb, _NT, preferred_element_type=F32) * kd_ref[...]
        dstate[...] = dstate[...] * cd_ref[0:1, :] + lax.dot_general(
            qs, doq, _TN, preferred_element_type=F32)
        dq_ref[...] = _rot_bwd(dqs * scale, c, s).astype(dq_ref.dtype)
        dk_ref[...] = _rot_bwd(dkr, c, s).astype(dk_ref.dtype)

    def pj(off):
        return pl.BlockSpec((blk, HEAD_DIM), lambda h, i: (nb - 1 - i, off * H + h))

    tok = pl.BlockSpec((blk, HEAD_DIM), lambda h, i: (nb - 1 - i, 0))
    out = pl.BlockSpec((blk, HEAD_DIM), lambda h, i: (nb - 1 - i, h))

    def per_head(r, w):
        return pl.BlockSpec((None, r, w), lambda h, i: (h, 0, 0))

    shp = jax.ShapeDtypeStruct((T, A), CDT)
    return pl.pallas_call(
        body, name="ret_bwd", grid=(H, nb),
        in_specs=[pj(3), pj(4), pj(5), pj(6), tok, tok, per_head(blk, blk),
                  per_head(blk, HEAD_DIM), per_head(blk, HEAD_DIM), per_head(8, HEAD_DIM),
                  out, pl.BlockSpec((None, None, HEAD_DIM, HEAD_DIM), lambda h, i: (h, nb - 1 - i, 0, 0)),
                  out],
        out_specs=[out, out, out, out], out_shape=[shp, shp, shp, shp],
        scratch_shapes=[pltpu.VMEM((HEAD_DIM, HEAD_DIM), F32)],
        compiler_params=_params(("parallel", "arbitrary")))(
            proj, proj, proj, proj, rc, rs, dmat, qd, kd, cd, o_raw, states, dy)


def _mesh_pos():
    return lax.axis_index("x"), lax.axis_index("y"), lax.axis_index("c")


def _flat(pos):
    return 4 * pos[0] + 2 * pos[1] + pos[2]


_HBM = pl.BlockSpec(memory_space=pltpu.HBM)


def cast_shard(name, w, rows_p, cols_p, me_arr):
    r, c = w.shape
    tr = _row_tile(math.gcd(r, rows_p), 256)
    nr = r // tr

    def body(me_ref, w_ref, o_ref):
        i = pl.program_id(0)
        o_ref[...] = jnp.zeros_like(o_ref)

        @pl.when(i < nr)
        def _():
            o_ref[:, 0:c] = w_ref[...].astype(o_ref.dtype)

    return pl.pallas_call(
        body, name=name,
        grid_spec=pltpu.PrefetchScalarGridSpec(
            num_scalar_prefetch=1, grid=(rows_p // tr,),
            in_specs=[pl.BlockSpec((tr, c), lambda i, me: (jnp.minimum(i, nr - 1), 0))],
            out_specs=pl.BlockSpec((None, tr, cols_p), lambda i, me: (me[0], i, 0))),
        out_shape=jax.ShapeDtypeStruct((N_DEV, rows_p, cols_p), CDT),
        compiler_params=_params(("arbitrary",)))(me_arr, w)


def all_gather_blocks(slotted):
    n = len(slotted)

    def body(*refs):
        ins, outs = refs[:n], refs[n:2 * n]
        send_sems, recv_sems = refs[2 * n:]
        x, y, c = _mesh_pos()
        me, sibling = (x, y, c), (x, y, 1 - c)
        chips = [(1 - x, y), (x, 1 - y), (1 - x, 1 - y)]

        def copy(t, k, block, to, own=False):
            dst = outs[t].at[_flat(block)]
            return pltpu.make_async_remote_copy(
                src_ref=ins[t].at[_flat(block)] if own else dst, dst_ref=dst,
                send_sem=send_sems.at[t, k], recv_sem=recv_sems.at[t, k],
                device_id=to, device_id_type=pl.DeviceIdType.MESH)

        first, passed = [], []
        for t in range(n):
            row = [copy(t, 0, me, sibling, own=True)]
            row += [copy(t, 1 + j, me, (*chip, c), own=True) for j, chip in enumerate(chips)]
            for cp in row:
                cp.start()
            first += row
        for t in range(n):
            for j, chip in enumerate(chips):
                copy(t, 1 + j, (*chip, c), me).wait_recv()
                cp = copy(t, 4 + j, (*chip, c), sibling)
                cp.start()
                passed.append(cp)
        for t in range(n):
            copy(t, 0, sibling, me).wait_recv()
            for j, chip in enumerate(chips):
                copy(t, 4 + j, (*chip, 1 - c), me).wait_recv()
        for cp in first + passed:
            cp.wait_send()

    return pl.pallas_call(
        body, name="all_gather_weights",
        in_specs=[_HBM] * n, out_specs=[_HBM] * n,
        out_shape=[jax.ShapeDtypeStruct(s.shape, s.dtype) for s in slotted],
        input_output_aliases={t: t for t in range(n)},
        scratch_shapes=[pltpu.SemaphoreType.DMA((n, 7)), pltpu.SemaphoreType.DMA((n, 7))],
        )(*slotted)


def exchange_blocks(blocked, whole):
    nb_, nw_ = len(blocked), len(whole)
    n = nb_ + nw_

    def body(*refs):
        ins, outs = refs[:n], refs[n:2 * n]
        send_sems, recv_sems, local_sems = refs[2 * n:]
        x, y, c = _mesh_pos()
        me = (x, y, c)
        copies, locals_ = [], []
        for t in range(n):
            def src(pos):
                return ins[t].at[_flat(pos)] if t < nb_ else ins[t]
            cp = pltpu.make_async_copy(src(me), outs[t].at[_flat(me)], local_sems.at[t])
            cp.start()
            locals_.append(cp)
            for k in range(1, N_DEV):
                peer = tuple(1 - v if bit else v for v, bit in zip(me, (k >> 2, (k >> 1) & 1, k & 1)))
                send = pltpu.make_async_remote_copy(
                    src_ref=src(peer), dst_ref=outs[t].at[_flat(me)],
                    send_sem=send_sems.at[t, k - 1], recv_sem=recv_sems.at[t, k - 1],
                    device_id=peer, device_id_type=pl.DeviceIdType.MESH)
                send.start()
                recv = pltpu.make_async_remote_copy(
                    src_ref=src(peer), dst_ref=outs[t].at[_flat(peer)],
                    send_sem=send_sems.at[t, k - 1], recv_sem=recv_sems.at[t, k - 1],
                    device_id=peer, device_id_type=pl.DeviceIdType.MESH)
                copies.append((send, recv))
        for send, recv in copies:
            recv.wait_recv()
        for send, recv in copies:
            send.wait_send()
        for cp in locals_:
            cp.wait()

    out_shape = [jax.ShapeDtypeStruct(a.shape, a.dtype) for a in blocked]
    out_shape += [jax.ShapeDtypeStruct((N_DEV,) + a.shape, a.dtype) for a in whole]
    return pl.pallas_call(
        body, name="exchange_grads",
        in_specs=[_HBM] * n, out_specs=[_HBM] * n, out_shape=out_shape,
        scratch_shapes=[pltpu.SemaphoreType.DMA((n, 7)), pltpu.SemaphoreType.DMA((n, 7)),
                        pltpu.SemaphoreType.DMA((n,))],
        )(*blocked, *whole)


_SEM = pl.BlockSpec(memory_space=pltpu.SEMAPHORE)
_ANY = pl.BlockSpec(memory_space=pl.ANY)
_EFFECT = pltpu.SideEffectType.DATAFLOW_SIDE_EFFECTING


def _peer(me, k):
    return tuple(1 - v if bit else v for v, bit in zip(me, (k >> 2, (k >> 1) & 1, k & 1)))


def _sem_index(t, k):
    return t * (N_DEV - 1) + k - 1


def _split_ends(bufs, n, gather, me, k):
    peer = _peer(me, k)
    if gather:
        return (lambda t: bufs[t].at[_flat(me)]), (lambda t: bufs[t].at[_flat(me)]), (lambda t: bufs[t].at[_flat(peer)])
    return (lambda t: bufs[t].at[_flat(peer)]), (lambda t: bufs[n + t].at[k - 1]), (lambda t: bufs[n + t].at[k - 1])


def exchange_start(name, arrays, gather, after):
    n, na = len(arrays), len(after)
    bufs = list(arrays)
    if not gather:
        bufs += [lax.empty((N_DEV - 1,) + a.shape[1:], a.dtype) for a in arrays]
    nb = len(bufs)

    def body(*refs):
        send_sems, recv_sems = refs[nb + na], refs[nb + na + 1]
        token = refs[-1]
        me = _mesh_pos()
        for t in range(n):
            for k in range(1, N_DEV):
                src, dst, _ = _split_ends(refs[:nb], n, gather, me, k)
                pltpu.make_async_remote_copy(
                    src_ref=src(t), dst_ref=dst(t),
                    send_sem=send_sems.at[_sem_index(t, k)], recv_sem=recv_sems.at[_sem_index(t, k)],
                    device_id=_peer(me, k), device_id_type=pl.DeviceIdType.MESH).start()
        token[...] = jnp.zeros_like(token)

    out_shape = [pltpu.SemaphoreType.DMA((n * (N_DEV - 1),)), pltpu.SemaphoreType.DMA((n * (N_DEV - 1),))]
    out_shape += [pltpu.HBM(a.shape, a.dtype) for a in bufs]
    out_shape.append(jax.ShapeDtypeStruct((8, LANE), F32))
    args = [pltpu.with_memory_space_constraint(a, pltpu.HBM) for a in bufs] + list(after)
    outs = pl.pallas_call(
        body, name=name, out_shape=out_shape,
        in_specs=[_HBM] * nb + [_ANY] * na,
        out_specs=[_SEM, _SEM] + [_HBM] * nb + [pl.BlockSpec(memory_space=pltpu.VMEM)],
        input_output_aliases={i: 2 + i for i in range(nb)},
        compiler_params=pltpu.CompilerParams(has_side_effects=_EFFECT))(*args)
    return outs[0], outs[1], list(outs[2:2 + nb]), outs[-1]


def exchange_wait(name, started, gather, after):
    send_sems, recv_sems, bufs, _ = started
    nb, na = len(bufs), len(after)
    n = nb if gather else nb // 2

    def body(*refs):
        send_sems_ref, recv_sems_ref = refs[nb], refs[nb + 1]
        me = _mesh_pos()
        for t in range(n):
            for k in range(1, N_DEV):
                src, _, land = _split_ends(refs[:nb], n, gather, me, k)
                cp = pltpu.make_async_remote_copy(
                    src_ref=src(t), dst_ref=land(t),
                    send_sem=send_sems_ref.at[_sem_index(t, k)], recv_sem=recv_sems_ref.at[_sem_index(t, k)],
                    device_id=_peer(me, k), device_id_type=pl.DeviceIdType.MESH)
                cp.wait_send()
                cp.wait_recv()

    outs = pl.pallas_call(
        body, name=name, out_shape=[pltpu.HBM(a.shape, a.dtype) for a in bufs],
        in_specs=[_HBM] * nb + [_SEM, _SEM] + [_ANY] * na, out_specs=[_HBM] * nb,
        input_output_aliases={i: i for i in range(nb)},
        compiler_params=pltpu.CompilerParams(has_side_effects=_EFFECT))(
            *bufs, send_sems, recv_sems, *after)
    return list(outs)


def _adamw_math(w, g, m, v):
    m = ADAM_B1 * m + (1.0 - ADAM_B1) * g
    v = ADAM_B2 * v + (1.0 - ADAM_B2) * (g * g)
    m_hat = m / (1.0 - ADAM_B1 ** ADAM_STEP)
    v_hat = v / (1.0 - ADAM_B2 ** ADAM_STEP)
    delta = -ADAM_LR * (m_hat / (jnp.sqrt(v_hat) + ADAM_EPS) + ADAM_WD * w)
    return delta, m, v


def reduce_adamw(name, land, w, m, v, tr, own=None, me_arr=None):
    R, C = w.shape
    S, _, Cp = land.shape

    def body(*refs):
        if own is not None:
            _, own_ref, l_ref, w_ref, m_ref, v_ref, g_ref, d_ref, nm_ref, nv_ref = refs
            g = own_ref[:, 0:C].astype(F32)
            first = 0
        else:
            l_ref, w_ref, m_ref, v_ref, g_ref, d_ref, nm_ref, nv_ref = refs
            g = l_ref[0, :, 0:C].astype(F32)
            first = 1
        for s in range(first, S):
            g = g + l_ref[s, :, 0:C].astype(F32)
        delta, nm, nv = _adamw_math(w_ref[...], g, m_ref[...], v_ref[...])
        g_ref[...] = g
        d_ref[...] = delta
        nm_ref[...] = nm
        nv_ref[...] = nv

    shp = jax.ShapeDtypeStruct((R, C), F32)
    if own is None:
        blk = pl.BlockSpec((tr, C), lambda i: (i, 0))
        return pl.pallas_call(
            body, name=name, grid=(R // tr,),
            in_specs=[pl.BlockSpec((S, tr, Cp), lambda i: (0, i, 0)), blk, blk, blk],
            out_specs=[blk, blk, blk, blk], out_shape=[shp, shp, shp, shp],
            compiler_params=_params(("parallel",)))(land, w, m, v)
    blk = pl.BlockSpec((tr, C), lambda i, me: (i, 0))
    return pl.pallas_call(
        body, name=name,
        grid_spec=pltpu.PrefetchScalarGridSpec(
            num_scalar_prefetch=1, grid=(R // tr,),
            in_specs=[pl.BlockSpec((None, tr, Cp), lambda i, me: (me[0], i, 0)),
                      pl.BlockSpec((S, tr, Cp), lambda i, me: (0, i, 0)), blk, blk, blk],
            out_specs=[blk, blk, blk, blk]),
        out_shape=[shp, shp, shp, shp],
        compiler_params=_params(("parallel",)))(me_arr, own, land, w, m, v)


def _row_tile(r, cap):
    t = min(r, cap)
    while r % t or t % 8:
        t -= 8
    return t


def kernel(x, norm_ffn1_g, ffn1_w_gate, ffn1_w_up, ffn1_w_down, norm_mix_g, w_in, rel_bias, w_out_att, w_out_ret, w_out, norm_ffn2_g, ffn2_w_gate, ffn2_w_up, ffn2_w_down, norm_final_g, loss_target, m_norm_ffn1_g, m_ffn1_w_gate, m_ffn1_w_up, m_ffn1_w_down, m_norm_mix_g, m_w_in, m_rel_bias, m_w_out_att, m_w_out_ret, m_w_out, m_norm_ffn2_g, m_ffn2_w_gate, m_ffn2_w_up, m_ffn2_w_down, m_norm_final_g, v_norm_ffn1_g, v_ffn1_w_gate, v_ffn1_w_up, v_ffn1_w_down, v_norm_mix_g, v_w_in, v_rel_bias, v_w_out_att, v_w_out_ret, v_w_out, v_norm_ffn2_g, v_ffn2_w_gate, v_ffn2_w_up, v_ffn2_w_down, v_norm_final_g):
    T, D = x.shape[1], x.shape[2]
    A = w_out_att.shape[1]
    H = A // HEAD_DIM
    nf = ffn1_w_gate.shape[2]
    nfp = _round_up(nf, LANE)
    nin = w_in.shape[2]
    nd = w_out.shape[1]
    assert nin % LANE == 0 and nd % LANE == 0 and (7 * A) % nd == 0 and T % ATT_TQ == 0
    tm = min(512, T)
    tn = min(256, T)
    x0 = x[0]
    tgt = loss_target[0]

    me_arr = (4 * lax.axis_index("x") + 2 * lax.axis_index("y") + lax.axis_index("c")).astype(jnp.int32).reshape(1)

    def slot(tag, w, rows_p=None, cols_p=None):
        return cast_shard("cast_" + tag, w[0], rows_p or w.shape[1], cols_p or w.shape[2], me_arr)

    Wg1, Wu1 = all_gather_blocks([slot("wg1", ffn1_w_gate, cols_p=nfp), slot("wu1", ffn1_w_up, cols_p=nfp)])
    later = [("wd1", [slot("wd1", ffn1_w_down, rows_p=nfp)]), ("win", [slot("win", w_in)]),
             ("wout", [slot("woa", w_out_att), slot("wor", w_out_ret), slot("wo", w_out)]),
             ("wgu2", [slot("wg2", ffn2_w_gate, cols_p=nfp), slot("wu2", ffn2_w_up, cols_p=nfp)]),
             ("wd2", [slot("wd2", ffn2_w_down, rows_p=nfp)])]
    ag_started = {}
    order = Wg1
    for tag, ws in later:
        ag_started[tag] = exchange_start("ag_start_" + tag, ws, True, [order])
        order = ag_started[tag][3]

    def gathered(tag, after):
        return exchange_wait("ag_wait_" + tag, ag_started[tag], True, [after])

    def swiglu(prods, _):
        a, b = prods
        return a, b, a * _sigmoid(a) * b

    def ffn_fwd(tag, xin, g, get_wgu, get_wd, after=()):
        h = rmsnorm_fwd(tag + "_norm", xin, g, tn, after)
        Wg, Wu = get_wgu(h)
        a, b, mid = mm_block(tag + "_up", T, tm, [(h, "full", D, 0)], [(Wg, 0, False), (Wu, 0, False)],
                             [], [("3d", nfp, CDT)] * 3, swiglu)
        Wd, = get_wd(mid)
        xo = mm_reduce_j(tag + "_down", T, tm, [(mid, "3d", nfp, 0, Wd, False)], D, F32, res=xin, scale=0.5)
        return h, a, b, mid, xo, (Wg, Wu, Wd)

    h1, a1, b1, mid1, x1, (_, _, Wd1) = ffn_fwd(
        "ffn1", x0, norm_ffn1_g, lambda h: (Wg1, Wu1), lambda mid: gathered("wd1", mid), after=[order])
    h2 = rmsnorm_fwd("mix_norm", x1, norm_mix_g, tn)
    Win, = gathered("win", h2)
    proj, = mm_block("in_proj", T, tm, [(h2, "full", D, 0)], [(Win, 0, False)], [], [("col", nin, CDT)],
                     lambda p, _: p)
    biasm = attn_bias(rel_bias[0], ATT_TQ)
    att = attn_fwd(proj, biasm, A, ATT_TQ)
    tables = _retention_tables(T, H, RET_BLK)
    retg, ret_raw, states = ret_fwd(proj, tables, A, RET_BLK)
    Woa, Wor, Wo = gathered("wout", retg)
    goff = 7 * A // nd

    def merge(prods, ex):
        ba, br = prods
        ga, gr = ex
        return ba, br, _sigmoid(ga) * ba + _sigmoid(gr) * br

    ba, br, merged = mm_block(
        "branches", T, tm, [(att, "full", A, 0), (retg, "full", A, 0)], [(Woa, 0, False), (Wor, 1, False)],
        [(proj, "col", nd, goff), (proj, "col", nd, goff + N_DEV)], [("col", nd, CDT)] * 3, merge)
    x2 = mm_reduce_j("out_proj", T, tm, [(merged, "col", nd, 0, Wo, False)], D, F32, res=x1, scale=1.0)
    h3, a2, b2, mid2, x3, (Wg2, Wu2, Wd2) = ffn_fwd(
        "ffn2", x2, norm_ffn2_g, lambda h: gathered("wgu2", h), lambda mid: gathered("wd2", mid))

    dx3, dx3h, dgf, loss_part = loss_head(x3, norm_final_g.reshape(1, D), tgt, tn)

    def swiglu_bwd(prods, ex):
        dm, = prods
        a, b = ex
        sg = _sigmoid(a)
        return dm * b * (sg * (1.0 + a * (1.0 - sg))), dm * (a * sg)

    def ffn_bwd(tag, dxh, h, a, b, mid, Wg, Wu, Wd):
        da, db = mm_block(tag + "_down_bwd", T, tm, [(dxh, "full", D, 0)], [(Wd, 0, True)],
                          [(a, "3d", nfp, 0), (b, "3d", nfp, 0)], [("3d", nfp, CDT)] * 2, swiglu_bwd)
        dWd = mm_reduce_i(tag + "_dwd", T, tm, (mid, "3d", nfp, 0), (dxh, "full", D, 0))
        dWg = mm_reduce_i(tag + "_dwg", T, tm, (h, "full", D, 0), (da, "3d", nfp, 0))
        dWu = mm_reduce_i(tag + "_dwu", T, tm, (h, "full", D, 0), (db, "3d", nfp, 0))
        sent = exchange_start("rs_start_" + tag, [dWg, dWu, dWd], False, [])
        dh = mm_reduce_j(tag + "_up_bwd", T, tm, [(da, "3d", nfp, 0, Wg, True), (db, "3d", nfp, 0, Wu, True)],
                         D, F32, after=[sent[3]])
        return dh, sent

    dh3, sent_ffn2 = ffn_bwd("ffn2", dx3h, h3, a2, b2, mid2, Wg2, Wu2, Wd2)
    dx2, dx2c, dg2 = rmsnorm_bwd("ffn2_norm_bwd", x2, norm_ffn2_g, dh3, dx3, 1.0, tn)

    def merge_bwd(prods, ex):
        dmg, = prods
        ba_, br_, ga, gr = ex
        sa, sr = _sigmoid(ga), _sigmoid(gr)
        return dmg * sa, dmg * sr, dmg * ba_ * sa * (1.0 - sa), dmg * br_ * sr * (1.0 - sr)

    dba, dbr, dga, dgr = mm_block(
        "out_proj_bwd", T, tm, [(dx2c, "full", D, 0)], [(Wo, 0, True)],
        [(ba, "col", nd, 0), (br, "col", nd, 0), (proj, "col", nd, goff), (proj, "col", nd, goff + N_DEV)],
        [("col", nd, CDT)] * 4, merge_bwd)
    dWo = mm_reduce_i("dwo", T, tm, (merged, "col", nd, 0), (dx2c, "full", D, 0))
    dWoa = mm_reduce_i("dwoa", T, tm, (att, "full", A, 0), (dba, "col", nd, 0))
    dWor = mm_reduce_i("dwor", T, tm, (retg, "full", A, 0), (dbr, "col", nd, 0))
    sent_mix = exchange_start("rs_start_mix", [dWoa, dWor, dWo], False, [])
    datt = mm_reduce_j("att_out_bwd", T, tm, [(dba, "col", nd, 0, Woa, True)], A, CDT, after=[sent_mix[3]])
    dretg = mm_reduce_j("ret_out_bwd", T, tm, [(dbr, "col", nd, 0, Wor, True)], A, CDT)
    dq_r, dk_r, dv_r, dg_r = ret_bwd(proj, tables, ret_raw, states, dretg, A, RET_BLK)
    dq_a, dk_a, dv_a, dst = attn_bwd(proj, biasm, datt, A, ATT_TQ)
    dbias = jnp.pad(attn_bias_grad(dst, ATT_TQ), ((0, 0), (0, N_REL_PAD - N_REL)))
    dproj = jnp.concatenate([dq_a, dk_a, dv_a, dq_r, dk_r, dv_r, dg_r, dga, dgr], axis=1)
    dWin = mm_reduce_i("dwin", T, tm, (h2, "full", D, 0), (dproj, "col", nin, 0))
    sent_win = exchange_start("rs_start_win", [dWin], False, [])
    dh2 = mm_reduce_j("in_proj_bwd", T, tm, [(dproj, "col", nin, 0, Win, True)], D, F32, after=[sent_win[3]])
    dx1, dx1h, dgm = rmsnorm_bwd("mix_norm_bwd", x1, norm_mix_g, dh2, dx2, 0.5, tn)
    dh1, sent_ffn1 = ffn_bwd("ffn1", dx1h, h1, a1, b1, mid1, Wg1, Wu1, Wd1)
    grad_x, _, dg1 = rmsnorm_bwd("ffn1_norm_bwd", x0, norm_ffn1_g, dh1, dx1, 1.0, tn)

    dgains = jnp.concatenate([dg1, dgm, dg2, dgf, jnp.zeros((4, D), F32)], axis=0)
    lgains, lbias = exchange_blocks([], [dgains, dbias])

    def upd(name, own, land, w, m, v):
        r = w.shape[1]
        outs = reduce_adamw(name, land, w[0], m[0], v[0], _row_tile(r, 256), own=own, me_arr=me_arr)
        return [o[None] for o in outs]

    res = {}
    oWg2, oWu2, oWd2, lWg2, lWu2, lWd2 = exchange_wait("rs_wait_ffn2", sent_ffn2, False, [lgains])
    res["ffn2_w_gate"] = upd("adamw_wg2", oWg2, lWg2, ffn2_w_gate, m_ffn2_w_gate, v_ffn2_w_gate)
    res["ffn2_w_up"] = upd("adamw_wu2", oWu2, lWu2, ffn2_w_up, m_ffn2_w_up, v_ffn2_w_up)
    res["ffn2_w_down"] = upd("adamw_wd2", oWd2, lWd2, ffn2_w_down, m_ffn2_w_down, v_ffn2_w_down)
    oWoa, oWor, oWo, lWoa, lWor, lWo = exchange_wait("rs_wait_mix", sent_mix, False, [res["ffn2_w_down"][1]])
    res["w_out_att"] = upd("adamw_woa", oWoa, lWoa, w_out_att, m_w_out_att, v_w_out_att)
    res["w_out_ret"] = upd("adamw_wor", oWor, lWor, w_out_ret, m_w_out_ret, v_w_out_ret)
    res["w_out"] = upd("adamw_wo", oWo, lWo, w_out, m_w_out, v_w_out)
    oWin, lWin = exchange_wait("rs_wait_win", sent_win, False, [res["w_out"][1]])
    res["w_in"] = upd("adamw_win", oWin, lWin, w_in, m_w_in, v_w_in)
    oWg1, oWu1, oWd1, lWg1, lWu1, lWd1 = exchange_wait("rs_wait_ffn1", sent_ffn1, False, [res["w_in"][1]])
    res["ffn1_w_gate"] = upd("adamw_wg1", oWg1, lWg1, ffn1_w_gate, m_ffn1_w_gate, v_ffn1_w_gate)
    res["ffn1_w_up"] = upd("adamw_wu1", oWu1, lWu1, ffn1_w_up, m_ffn1_w_up, v_ffn1_w_up)
    res["ffn1_w_down"] = upd("adamw_wd1", oWd1, lWd1, ffn1_w_down, m_ffn1_w_down, v_ffn1_w_down)

    def stack_gains(a, b, c_, d):
        return jnp.concatenate([a, b, c_, d.reshape(1, D), jnp.zeros((4, D), F32)], axis=0)

    gw = stack_gains(norm_ffn1_g, norm_mix_g, norm_ffn2_g, norm_final_g)
    gm = stack_gains(m_norm_ffn1_g, m_norm_mix_g, m_norm_ffn2_g, m_norm_final_g)
    gv = stack_gains(v_norm_ffn1_g, v_norm_mix_g, v_norm_ffn2_g, v_norm_final_g)
    gains = reduce_adamw("adamw_gains", lgains, gw, gm, gv, 8)

    def padb(t):
        return jnp.pad(t[0], ((0, 0), (0, N_REL_PAD - N_REL)))

    bias = [o[:, :N_REL][None] for o in
            reduce_adamw("adamw_bias", lbias, padb(rel_bias), padb(m_rel_bias), padb(v_rel_bias), H)]
    res["norm_ffn1_g"] = [o[0:1] for o in gains]
    res["norm_mix_g"] = [o[1:2] for o in gains]
    res["norm_ffn2_g"] = [o[2:3] for o in gains]
    res["norm_final_g"] = [o[3] for o in gains]
    res["rel_bias"] = bias

    loss = lax.psum(loss_part[0, 0], MESH_AXES)
    names = ["norm_ffn1_g", "ffn1_w_gate", "ffn1_w_up", "ffn1_w_down", "norm_mix_g", "w_in", "rel_bias",
             "w_out_att", "w_out_ret", "w_out", "norm_ffn2_g", "ffn2_w_gate", "ffn2_w_up", "ffn2_w_down",
             "norm_final_g"]
    out = [loss, grad_x[None]]
    for k in range(4):
        out += [res[nm][k] for nm in names]
    return tuple(out)
```

```python
import functools
import math

import jax
import jax.numpy as jnp
import numpy as np
from jax import lax
from jax.experimental import pallas as pl
from jax.experimental.pallas import tpu as pltpu

F32 = jnp.float32
CDT = jnp.bfloat16

N_DEV = 8
CHUNK = 64
N_PREV_CHUNKS = 8
BAND = N_PREV_CHUNKS * CHUNK
HEAD_DIM = 128
MAX_REL_DIST = 128
N_REL = 2 * MAX_REL_DIST + 1
N_REL_PAD = 384
ROPE_BASE = 10000.0
EPS = 1e-6
NEG = -1e30
LANE = 128
ATT_TQ = 256
RET_BLK = 256
VMEM_LIMIT = 48 * 1024 * 1024

ADAM_LR = 0.001
ADAM_B1 = 0.9
ADAM_B2 = 0.999
ADAM_EPS = 1e-08
ADAM_WD = 0.01
ADAM_STEP = 10

MESH_AXES = ("x", "y", "c")
_NT = (((1,), (1,)), ((), ()))
_TN = (((0,), (0,)), ((), ()))


def _round_up(v, m):
    return (v + m - 1) // m * m


def _params(sem=None):
    return pltpu.CompilerParams(dimension_semantics=sem, vmem_limit_bytes=VMEM_LIMIT)


def _sigmoid(v):
    return 1.0 / (1.0 + jnp.exp(-v))


def _bspec(kind, tm, w, off, order, jmap=lambda j: j):
    def wrap(f):
        if order == "ji":
            return lambda j, i: f(i, jmap(j))
        return lambda i, j: f(i, jmap(j))
    if kind == "full":
        return pl.BlockSpec((tm, w), wrap(lambda i, j: (i, 0)))
    if kind == "col":
        return pl.BlockSpec((tm, w), wrap(lambda i, j: (i, j + off)))
    assert kind == "3d"
    return pl.BlockSpec((None, tm, w), wrap(lambda i, j: (j, i, 0)))


def _wspec(w, order, jmap=lambda j: j):
    if order == "ji":
        return pl.BlockSpec((None,) + w.shape[1:], lambda j, i: (jmap(j), 0, 0))
    return pl.BlockSpec((None,) + w.shape[1:], lambda i, j: (jmap(j), 0, 0))


def _width(arr, kind, w):
    return arr.shape[-1] if kind in ("full", "3d") else w


def mm_block(name, T, tm, lhs, wts, extras, outs, epilogue):
    nl, nw, ne = len(lhs), len(wts), len(extras)
    ni = T // tm

    def body(*refs):
        l = refs[:nl]
        w = refs[nl:nl + nw]
        e = refs[nl + nw:nl + nw + ne]
        o = refs[nl + nw + ne:]
        prods = []
        for k, (_, li, tr) in enumerate(wts):
            a = l[li][...]
            if tr:
                prods.append(lax.dot_general(a, w[k][...], _NT, preferred_element_type=F32))
            else:
                prods.append(jnp.dot(a, w[k][...], preferred_element_type=F32))
        res = epilogue(prods, [r[...].astype(F32) for r in e])
        for r, val in zip(o, res):
            r[...] = val.astype(r.dtype)

    in_specs = [_bspec(k, tm, _width(a, k, w), off, "ji") for (a, k, w, off) in lhs]
    in_specs += [_wspec(w, "ji") for (w, _, _) in wts]
    in_specs += [_bspec(k, tm, _width(a, k, w), off, "ji") for (a, k, w, off) in extras]
    out_specs, out_shape = [], []
    for (kind, w, dt) in outs:
        out_specs.append(_bspec(kind, tm, w, 0, "ji"))
        if kind == "3d":
            out_shape.append(jax.ShapeDtypeStruct((N_DEV, T, w), dt))
        else:
            out_shape.append(jax.ShapeDtypeStruct((T, N_DEV * w), dt))
    args = [a for (a, _, _, _) in lhs] + [w for (w, _, _) in wts] + [a for (a, _, _, _) in extras]
    return pl.pallas_call(
        body, name=name, grid=(N_DEV, ni), in_specs=in_specs, out_specs=out_specs,
        out_shape=out_shape, compiler_params=_params(("parallel", "parallel")))(*args)


def mm_reduce_j(name, T, tm, pairs, out_w, out_dtype, res=None, scale=1.0, after=(), jstep=1):
    terms = [(p, u) for u in range(jstep) for p in pairs]
    nt = len(terms)
    nj = N_DEV // jstep
    ni = T // tm

    def body(*refs):
        xs = refs[:nt]
        ws = refs[nt:2 * nt]
        rest = refs[2 * nt:len(refs) - 2 - len(after)] + refs[len(refs) - 2:]
        if res is not None:
            res_ref, o_ref, acc = rest
        else:
            o_ref, acc = rest
        j = pl.program_id(1)

        @pl.when(j == 0)
        def _():
            acc[...] = jnp.zeros_like(acc)

        tot = None
        for k, (p, _) in enumerate(terms):
            if p[5]:
                d = lax.dot_general(xs[k][...], ws[k][...], _NT, preferred_element_type=F32)
            else:
                d = jnp.dot(xs[k][...], ws[k][...], preferred_element_type=F32)
            tot = d if tot is None else tot + d
        acc[...] += tot

        @pl.when(j == nj - 1)
        def _():
            if res is not None:
                o_ref[...] = (res_ref[...] + scale * acc[...]).astype(o_ref.dtype)
            else:
                o_ref[...] = acc[...].astype(o_ref.dtype)

    def jmap(u):
        return lambda j: j * jstep + u

    in_specs = [_bspec(p[1], tm, _width(p[0], p[1], p[2]), p[3], "ij", jmap(u)) for (p, u) in terms]
    in_specs += [_wspec(p[4], "ij", jmap(u)) for (p, u) in terms]
    args = [p[0] for (p, _) in terms] + [p[4] for (p, _) in terms]
    if res is not None:
        in_specs.append(pl.BlockSpec((tm, out_w), lambda i, j: (i, 0)))
        args.append(res)
    in_specs += [_ANY] * len(after)
    args += list(after)
    return pl.pallas_call(
        body, name=name, grid=(ni, nj), in_specs=in_specs,
        out_specs=pl.BlockSpec((tm, out_w), lambda i, j: (i, 0)),
        out_shape=jax.ShapeDtypeStruct((T, out_w), out_dtype),
        scratch_shapes=[pltpu.VMEM((tm, out_w), F32)],
        compiler_params=_params(("parallel", "arbitrary")))(*args)


def mm_reduce_i(name, T, tm, a, b):
    ni = T // tm
    rows = _width(a[0], a[1], a[2])
    cols = _width(b[0], b[1], b[2])

    def body(a_ref, b_ref, o_ref, acc):
        i = pl.program_id(1)

        @pl.when(i == 0)
        def _():
            acc[...] = jnp.zeros_like(acc)

        acc[...] += lax.dot_general(a_ref[...], b_ref[...], _TN, preferred_element_type=F32)

        @pl.when(i == ni - 1)
        def _():
            o_ref[...] = acc[...].astype(o_ref.dtype)

    return pl.pallas_call(
        body, name=name, grid=(N_DEV, ni),
        in_specs=[_bspec(a[1], tm, rows, a[3], "ji"), _bspec(b[1], tm, cols, b[3], "ji")],
        out_specs=pl.BlockSpec((None, rows, cols), lambda j, i: (j, 0, 0)),
        out_shape=jax.ShapeDtypeStruct((N_DEV, rows, cols), CDT),
        scratch_shapes=[pltpu.VMEM((rows, cols), F32)],
        compiler_params=_params(("parallel", "arbitrary")))(a[0], b[0])


def _rms_bwd_math(xv, g, dy):
    r = lax.rsqrt(jnp.mean(xv * xv, axis=-1, keepdims=True) + EPS)
    xn = xv * r
    dxn = dy * g
    dx = r * (dxn - xn * jnp.mean(dxn * xn, axis=-1, keepdims=True))
    dg = jnp.sum(dy * xn, axis=0, keepdims=True)
    return dx, dg


def rmsnorm_fwd(name, x, g, tm, after=()):
    T, D = x.shape

    def body(x_ref, g_ref, *rest):
        o_ref = rest[-1]
        xv = x_ref[...]
        r = lax.rsqrt(jnp.mean(xv * xv, axis=-1, keepdims=True) + EPS)
        o_ref[...] = (xv * r * g_ref[...]).astype(o_ref.dtype)

    return pl.pallas_call(
        body, name=name, grid=(T // tm,),
        in_specs=[pl.BlockSpec((tm, D), lambda i: (i, 0)), pl.BlockSpec((1, D), lambda i: (0, 0))]
        + [_ANY] * len(after),
        out_specs=pl.BlockSpec((tm, D), lambda i: (i, 0)),
        out_shape=jax.ShapeDtypeStruct((T, D), CDT),
        compiler_params=_params(("parallel",)))(x, g, *after)


def rmsnorm_bwd(name, x, g, dh, dres, cscale, tm):
    T, D = x.shape

    def body(x_ref, g_ref, dh_ref, dres_ref, dx_ref, dxc_ref, dg_ref):
        i = pl.program_id(0)
        dx, dg = _rms_bwd_math(x_ref[...], g_ref[...], dh_ref[...])
        dx = dres_ref[...] + dx
        dx_ref[...] = dx
        dxc_ref[...] = (cscale * dx).astype(dxc_ref.dtype)

        @pl.when(i == 0)
        def _():
            dg_ref[...] = jnp.zeros_like(dg_ref)

        dg_ref[...] += dg

    row = pl.BlockSpec((tm, D), lambda i: (i, 0))
    vec = pl.BlockSpec((1, D), lambda i: (0, 0))
    return pl.pallas_call(
        body, name=name, grid=(T // tm,), in_specs=[row, vec, row, row],
        out_specs=[row, row, vec],
        out_shape=[jax.ShapeDtypeStruct((T, D), F32), jax.ShapeDtypeStruct((T, D), CDT),
                   jax.ShapeDtypeStruct((1, D), F32)],
        compiler_params=_params(("arbitrary",)))(x, g, dh, dres)


def loss_head(x, g, tgt, tm):
    T, D = x.shape

    def body(x_ref, g_ref, t_ref, dx_ref, dxc_ref, dg_ref, loss_ref):
        i = pl.program_id(0)
        xv = x_ref[...]
        gv = g_ref[...]
        r = lax.rsqrt(jnp.mean(xv * xv, axis=-1, keepdims=True) + EPS)
        err = xv * r * gv - t_ref[...]
        part = jnp.sum(jnp.mean(err * err, axis=-1, keepdims=True), axis=0, keepdims=True)
        dx, dg = _rms_bwd_math(xv, gv, err / D)
        dx_ref[...] = dx
        dxc_ref[...] = (0.5 * dx).astype(dxc_ref.dtype)

        @pl.when(i == 0)
        def _():
            dg_ref[...] = jnp.zeros_like(dg_ref)
            loss_ref[...] = jnp.zeros_like(loss_ref)

        dg_ref[...] += dg
        loss_ref[...] += jnp.broadcast_to(0.5 * part, loss_ref.shape)

    row = pl.BlockSpec((tm, D), lambda i: (i, 0))
    vec = pl.BlockSpec((1, D), lambda i: (0, 0))
    return pl.pallas_call(
        body, name="loss_head", grid=(T // tm,), in_specs=[row, vec, row],
        out_specs=[row, row, vec, pl.BlockSpec((1, LANE), lambda i: (0, 0))],
        out_shape=[jax.ShapeDtypeStruct((T, D), F32), jax.ShapeDtypeStruct((T, D), CDT),
                   jax.ShapeDtypeStruct((1, D), F32), jax.ShapeDtypeStruct((1, LANE), F32)],
        compiler_params=_params(("arbitrary",)))(x, g, tgt)


def _skew_rows(z, left):
    tq, kw = z.shape
    row = lax.broadcasted_iota(jnp.int32, (tq, kw), 0)
    s = 1
    while s < tq:
        z = jnp.where((row & s) != 0, pltpu.roll(z, kw - s if left else s, 1), z)
        s *= 2
    return z


REL_HI = BAND + MAX_REL_DIST
REL_LO = BAND - MAX_REL_DIST


def attn_bias(rel_bias, tq):
    H = rel_bias.shape[0]
    kw = BAND + tq
    by_skew = jnp.concatenate(
        [jnp.broadcast_to(rel_bias[:, N_REL - 1:], (H, REL_LO)), rel_bias[:, ::-1],
         jnp.broadcast_to(rel_bias[:, :1], (H, kw - REL_HI - 1))], axis=1).reshape(H, 1, kw)

    def body(t_ref, o_ref):
        t = t_ref[...]
        qi = lax.broadcasted_iota(jnp.int32, (tq, kw), 0)
        kj = lax.broadcasted_iota(jnp.int32, (tq, kw), 1)
        b = _skew_rows(jnp.broadcast_to(t, (tq, kw)), left=False)
        b = jnp.where(kj < qi, t[:, 0:1], b)
        qc = qi // CHUNK
        kc = kj // CHUNK - N_PREV_CHUNKS
        valid = (kc <= qc) & (kc >= qc - N_PREV_CHUNKS)
        o_ref[...] = jnp.where(valid, b, NEG)

    return pl.pallas_call(
        body, name="attn_bias", grid=(H,),
        in_specs=[pl.BlockSpec((None, 1, kw), lambda h: (h, 0, 0))],
        out_specs=pl.BlockSpec((None, tq, kw), lambda h: (h, 0, 0)),
        out_shape=jax.ShapeDtypeStruct((H, tq, kw), F32),
        compiler_params=_params(("parallel",)))(by_skew)


def attn_bias_grad(dst, tq):
    H = dst.shape[0]
    kw = BAND + tq

    def body(d_ref, o_ref):
        z = _skew_rows(d_ref[...], left=True)
        qi = lax.broadcasted_iota(jnp.int32, (tq, kw), 0)
        kj = lax.broadcasted_iota(jnp.int32, (tq, kw), 1)
        wrapped = kj + qi >= kw
        c = jnp.sum(jnp.where(wrapped, 0.0, z), axis=0, keepdims=True)
        cw = jnp.sum(jnp.sum(jnp.where(wrapped, z, 0.0), axis=0, keepdims=True), axis=1, keepdims=True)
        lane = lax.broadcasted_iota(jnp.int32, (1, kw), 1)
        ahead = jnp.sum(jnp.where(lane >= REL_HI, c, 0.0), axis=1, keepdims=True)
        behind = jnp.sum(jnp.where(lane <= REL_LO, c, 0.0), axis=1, keepdims=True) + cw
        o_ref[...] = jnp.where(lane == REL_HI, ahead, jnp.where(lane == REL_LO, behind, c))

    by_skew = pl.pallas_call(
        body, name="attn_bias_grad", grid=(H,),
        in_specs=[pl.BlockSpec((None, tq, kw), lambda h: (h, 0, 0))],
        out_specs=pl.BlockSpec((None, 1, kw), lambda h: (h, 0, 0)),
        out_shape=jax.ShapeDtypeStruct((H, 1, kw), F32),
        compiler_params=_params(("parallel",)))(dst)
    return by_skew[:, 0, REL_LO:REL_HI + 1][:, ::-1]


def _attn_scores(q, kpad, bm_ref, start, kw):
    k = kpad[pl.ds(start, kw), :]
    s = lax.dot_general(q, k, _NT, preferred_element_type=F32) * (HEAD_DIM ** -0.5) + bm_ref[...]
    col = lax.broadcasted_iota(jnp.int32, s.shape, 1)
    s = jnp.where(col < BAND - start, NEG, s)
    m = jnp.max(s, axis=-1, keepdims=True)
    p = jnp.exp(s - m)
    return p / jnp.sum(p, axis=-1, keepdims=True), k


def _fill_padded(pad_ref, src_ref, T):
    pad_ref[pl.ds(0, BAND), :] = jnp.zeros((BAND, HEAD_DIM), pad_ref.dtype)
    pad_ref[pl.ds(BAND, T), :] = src_ref[...]


def attn_fwd(proj, biasm, A, tq):
    T = proj.shape[0]
    H = A // HEAD_DIM
    kw = BAND + tq

    def body(q_ref, k_ref, v_ref, bm_ref, o_ref, kpad, vpad):
        qi = pl.program_id(1)

        @pl.when(qi == 0)
        def _():
            _fill_padded(kpad, k_ref, T)
            _fill_padded(vpad, v_ref, T)

        start = pl.multiple_of(qi * tq, tq)
        p, _ = _attn_scores(q_ref[...], kpad, bm_ref, start, kw)
        v = vpad[pl.ds(start, kw), :]
        o_ref[...] = jnp.dot(p.astype(CDT), v, preferred_element_type=F32).astype(o_ref.dtype)

    return pl.pallas_call(
        body, name="attn_fwd", grid=(H, T // tq),
        in_specs=[pl.BlockSpec((tq, HEAD_DIM), lambda h, i: (i, h)),
                  pl.BlockSpec((T, HEAD_DIM), lambda h, i: (0, H + h)),
                  pl.BlockSpec((T, HEAD_DIM), lambda h, i: (0, 2 * H + h)),
                  pl.BlockSpec((None, tq, kw), lambda h, i: (h, 0, 0))],
        out_specs=pl.BlockSpec((tq, HEAD_DIM), lambda h, i: (i, h)),
        out_shape=jax.ShapeDtypeStruct((T, A), CDT),
        scratch_shapes=[pltpu.VMEM((BAND + T, HEAD_DIM), CDT), pltpu.VMEM((BAND + T, HEAD_DIM), CDT)],
        compiler_params=_params(("parallel", "arbitrary")))(proj, proj, proj, biasm)


def attn_bwd(proj, biasm, datt, A, tq):
    T = proj.shape[0]
    H = A // HEAD_DIM
    kw = BAND + tq
    nq = T // tq
    scale = HEAD_DIM ** -0.5

    def body(q_ref, k_ref, v_ref, bm_ref, do_ref, dq_ref, dk_ref, dv_ref, dst_ref,
             kpad, vpad, dkacc, dvacc):
        qi = pl.program_id(1)

        @pl.when(qi == 0)
        def _():
            _fill_padded(kpad, k_ref, T)
            _fill_padded(vpad, v_ref, T)
            dkacc[...] = jnp.zeros_like(dkacc)
            dvacc[...] = jnp.zeros_like(dvacc)
            dst_ref[...] = jnp.zeros_like(dst_ref)

        start = pl.multiple_of(qi * tq, tq)
        q = q_ref[...]
        p, k = _attn_scores(q, kpad, bm_ref, start, kw)
        v = vpad[pl.ds(start, kw), :]
        do = do_ref[...]
        dp = lax.dot_general(do, v, _NT, preferred_element_type=F32)
        ds = p * (dp - jnp.sum(dp * p, axis=-1, keepdims=True))
        dst_ref[...] += ds
        dsb = ds.astype(CDT)
        dq_ref[...] = (jnp.dot(dsb, k, preferred_element_type=F32) * scale).astype(dq_ref.dtype)
        dkacc[pl.ds(start, kw), :] += lax.dot_general(dsb, q, _TN, preferred_element_type=F32) * scale
        dvacc[pl.ds(start, kw), :] += lax.dot_general(p.astype(CDT), do, _TN, preferred_element_type=F32)

        @pl.when(qi == nq - 1)
        def _():
            dk_ref[...] = dkacc[pl.ds(BAND, T), :].astype(dk_ref.dtype)
            dv_ref[...] = dvacc[pl.ds(BAND, T), :].astype(dv_ref.dtype)

    blk = pl.BlockSpec((tq, HEAD_DIM), lambda h, i: (i, h))
    col = pl.BlockSpec((T, HEAD_DIM), lambda h, i: (0, h))
    bias = pl.BlockSpec((None, tq, kw), lambda h, i: (h, 0, 0))
    return pl.pallas_call(
        body, name="attn_bwd", grid=(H, nq),
        in_specs=[blk,
                  pl.BlockSpec((T, HEAD_DIM), lambda h, i: (0, H + h)),
                  pl.BlockSpec((T, HEAD_DIM), lambda h, i: (0, 2 * H + h)),
                  bias, blk],
        out_specs=[blk, col, col, bias],
        out_shape=[jax.ShapeDtypeStruct((T, A), CDT), jax.ShapeDtypeStruct((T, A), CDT),
                   jax.ShapeDtypeStruct((T, A), CDT), jax.ShapeDtypeStruct((H, tq, kw), F32)],
        scratch_shapes=[pltpu.VMEM((BAND + T, HEAD_DIM), CDT), pltpu.VMEM((BAND + T, HEAD_DIM), CDT),
                        pltpu.VMEM((BAND + T, HEAD_DIM), F32), pltpu.VMEM((BAND + T, HEAD_DIM), F32)],
        compiler_params=_params(("parallel", "arbitrary")))(proj, proj, proj, biasm, datt)


def _retention_tables(T, H, blk):
    half = HEAD_DIM // 2
    inv = 1.0 / (ROPE_BASE ** (jnp.arange(0, HEAD_DIM, 2, dtype=F32) / HEAD_DIM))
    ang = jnp.arange(T, dtype=F32)[:, None] * inv[None, :]
    cos, sin = jnp.cos(ang), jnp.sin(ang)
    rc = jnp.concatenate([cos, cos], axis=1)
    rs = jnp.concatenate([-sin, sin], axis=1)
    assert rc.shape == (T, 2 * half)
    log_g = jnp.log(1.0 - 2.0 ** (-5.0 - jnp.arange(H, dtype=F32)))[:, None, None]
    idx = jnp.arange(blk, dtype=F32)
    n, m = idx[:, None], idx[None, :]
    same = (n // CHUNK) == (m // CHUNK)
    earlier = (m // CHUNK) < (n // CHUNK)
    dist = jnp.where(same, jnp.abs(n - m), n - m)[None]
    dmat = jnp.where((same | earlier)[None], jnp.exp(log_g * dist), 0.0)
    ones = jnp.ones((1, 1, HEAD_DIM), F32)
    qd = jnp.exp(log_g * (idx[None, :, None] + 1.0)) * ones
    kd = jnp.exp(log_g * (blk - 1.0 - idx[None, :, None])) * ones
    cd = jnp.exp(log_g * blk) * jnp.ones((1, 8, HEAD_DIM), F32)
    return rc, rs, dmat, qd, kd, cd


def _rot(v, rc, rs):
    return v * rc + pltpu.roll(v, HEAD_DIM // 2, 1) * rs


def _rot_bwd(dv, rc, rs):
    return dv * rc + pltpu.roll(dv * rs, HEAD_DIM // 2, 1)


def ret_fwd(proj, tables, A, blk):
    T = proj.shape[0]
    H = A // HEAD_DIM
    nb = T // blk
    rc, rs, dmat, qd, kd, cd = tables
    scale = HEAD_DIM ** -0.5

    def body(q_ref, k_ref, v_ref, g_ref, rc_ref, rs_ref, d_ref, qd_ref, kd_ref, cd_ref,
             y_ref, o_ref, st_ref, state):
        b = pl.program_id(1)

        @pl.when(b == 0)
        def _():
            state[...] = jnp.zeros_like(state)

        c, s = rc_ref[...], rs_ref[...]
        qs = (_rot(q_ref[...].astype(F32), c, s) * scale).astype(CDT)
        kr = _rot(k_ref[...].astype(F32), c, s)
        v = v_ref[...]
        sb = state[...].astype(CDT)
        a = lax.dot_general(qs, kr.astype(CDT), _NT, preferred_element_type=F32) * d_ref[...]
        o = jnp.dot(a.astype(CDT), v, preferred_element_type=F32)
        o = o + jnp.dot(qs, sb, preferred_element_type=F32) * qd_ref[...]
        st_ref[...] = sb
        state[...] = state[...] * cd_ref[0:1, :] + lax.dot_general(
            (kr * kd_ref[...]).astype(CDT), v, _TN, preferred_element_type=F32)
        o_ref[...] = o
        on = o * lax.rsqrt(jnp.mean(o * o, axis=-1, keepdims=True) + EPS)
        g = g_ref[...].astype(F32)
        y_ref[...] = (g * _sigmoid(g) * on).astype(y_ref.dtype)

    def pj(off):
        return pl.BlockSpec((blk, HEAD_DIM), lambda h, i: (i, off * H + h))

    tok = pl.BlockSpec((blk, HEAD_DIM), lambda h, i: (i, 0))
    out = pl.BlockSpec((blk, HEAD_DIM), lambda h, i: (i, h))

    def per_head(r):
        return pl.BlockSpec((None, r, HEAD_DIM), lambda h, i: (h, 0, 0))

    return pl.pallas_call(
        body, name="ret_fwd", grid=(H, nb),
        in_specs=[pj(3), pj(4), pj(5), pj(6), tok, tok,
                  pl.BlockSpec((None, blk, blk), lambda h, i: (h, 0, 0)),
                  per_head(blk), per_head(blk), per_head(8)],
        out_specs=[out, out, pl.BlockSpec((None, None, HEAD_DIM, HEAD_DIM), lambda h, i: (h, i, 0, 0))],
        out_shape=[jax.ShapeDtypeStruct((T, A), CDT), jax.ShapeDtypeStruct((T, A), F32),
                   jax.ShapeDtypeStruct((H, nb, HEAD_DIM, HEAD_DIM), CDT)],
        scratch_shapes=[pltpu.VMEM((HEAD_DIM, HEAD_DIM), F32)],
        compiler_params=_params(("parallel", "arbitrary")))(
            proj, proj, proj, proj, rc, rs, dmat, qd, kd, cd)


def ret_bwd(proj, tables, o_raw, states, dy, A, blk):
    T = proj.shape[0]
    H = A // HEAD_DIM
    nb = T // blk
    rc, rs, dmat, qd, kd, cd = tables
    scale = HEAD_DIM ** -0.5

    def body(q_ref, k_ref, v_ref, g_ref, rc_ref, rs_ref, d_ref, qd_ref, kd_ref, cd_ref,
             o_ref, st_ref, dy_ref, dq_ref, dk_ref, dv_ref, dg_ref, dstate):
        b = pl.program_id(1)

        @pl.when(b == 0)
        def _():
            dstate[...] = jnp.zeros_like(dstate)

        c, s = rc_ref[...], rs_ref[...]
        qs = (_rot(q_ref[...].astype(F32), c, s) * scale).astype(CDT)
        kr = _rot(k_ref[...].astype(F32), c, s)
        krb = kr.astype(CDT)
        kdb = (kr * kd_ref[...]).astype(CDT)
        v = v_ref[...]
        dmat_v = d_ref[...]
        a = lax.dot_general(qs, krb, _NT, preferred_element_type=F32) * dmat_v

        o = o_ref[...]
        r = lax.rsqrt(jnp.mean(o * o, axis=-1, keepdims=True) + EPS)
        on = o * r
        g = g_ref[...].astype(F32)
        sg = _sigmoid(g)
        dyv = dy_ref[...].astype(F32)
        dg_ref[...] = (dyv * on * (sg * (1.0 + g * (1.0 - sg)))).astype(dg_ref.dtype)
        don = dyv * (g * sg)
        do = r * (don - on * jnp.mean(don * on, axis=-1, keepdims=True))
        dob = do.astype(CDT)
        doq = (do * qd_ref[...]).astype(CDT)
        dsb = dstate[...].astype(CDT)

        dv = lax.dot_general(a.astype(CDT), dob, _TN, preferred_element_type=F32)
        dv = dv + jnp.dot(kdb, dsb, preferred_element_type=F32)
        dv_ref[...] = dv.astype(dv_ref.dtype)
        dpb = (lax.dot_general(dob, v, _NT, preferred_element_type=F32) * dmat_v).astype(CDT)
        dqs = jnp.dot(dpb, krb, preferred_element_type=F32)
        dqs = dqs + lax.dot_general(doq, st_ref[...], _NT, preferred_element_type=F32)
        dkr = lax.dot_general(dpb, qs, _TN, preferred_element_type=F32)
        dkr = dkr + lax.dot_general(v, dsb, _NT, preferred_element_type=F32) * kd_ref[...]
        dstate[...] = dstate[...] * cd_ref[0:1, :] + lax.dot_general(
            qs, doq, _TN, preferred_element_type=F32)
        dq_ref[...] = _rot_bwd(dqs * scale, c, s).astype(dq_ref.dtype)
        dk_ref[...] = _rot_bwd(dkr, c, s).astype(dk_ref.dtype)

    def pj(off):
        return pl.BlockSpec((blk, HEAD_DIM), lambda h, i: (nb - 1 - i, off * H + h))

    tok = pl.BlockSpec((blk, HEAD_DIM), lambda h, i: (nb - 1 - i, 0))
    out = pl.BlockSpec((blk, HEAD_DIM), lambda h, i: (nb - 1 - i, h))

    def per_head(r, w):
        return pl.BlockSpec((None, r, w), lambda h, i: (h, 0, 0))

    shp = jax.ShapeDtypeStruct((T, A), CDT)
    return pl.pallas_call(
        body, name="ret_bwd", grid=(H, nb),
        in_specs=[pj(3), pj(4), pj(5), pj(6), tok, tok, per_head(blk, blk),
                  per_head(blk, HEAD_DIM), per_head(blk, HEAD_DIM), per_head(8, HEAD_DIM),
                  out, pl.BlockSpec((None, None, HEAD_DIM, HEAD_DIM), lambda h, i: (h, nb - 1 - i, 0, 0)),
                  out],
        out_specs=[out, out, out, out], out_shape=[shp, shp, shp, shp],
        scratch_shapes=[pltpu.VMEM((HEAD_DIM, HEAD_DIM), F32)],
        compiler_params=_params(("parallel", "arbitrary")))(
            proj, proj, proj, proj, rc, rs, dmat, qd, kd, cd, o_raw, states, dy)


def _mesh_pos():
    return lax.axis_index("x"), lax.axis_index("y"), lax.axis_index("c")


def _flat(pos):
    return 4 * pos[0] + 2 * pos[1] + pos[2]


_HBM = pl.BlockSpec(memory_space=pltpu.HBM)


def cast_shard(name, w, rows_p, cols_p, me_arr):
    r, c = w.shape
    tr = _row_tile(math.gcd(r, rows_p), 256)
    nr = r // tr

    def body(me_ref, w_ref, o_ref):
        i = pl.program_id(0)
        o_ref[...] = jnp.zeros_like(o_ref)

        @pl.when(i < nr)
        def _():
            o_ref[:, 0:c] = w_ref[...].astype(o_ref.dtype)

    return pl.pallas_call(
        body, name=name,
        grid_spec=pltpu.PrefetchScalarGridSpec(
            num_scalar_prefetch=1, grid=(rows_p // tr,),
            in_specs=[pl.BlockSpec((tr, c), lambda i, me: (jnp.minimum(i, nr - 1), 0))],
            out_specs=pl.BlockSpec((None, tr, cols_p), lambda i, me: (me[0], i, 0))),
        out_shape=jax.ShapeDtypeStruct((N_DEV, rows_p, cols_p), CDT),
        compiler_params=_params(("arbitrary",)))(me_arr, w)


def all_gather_blocks(slotted):
    n = len(slotted)

    def body(*refs):
        ins, outs = refs[:n], refs[n:2 * n]
        send_sems, recv_sems = refs[2 * n:]
        x, y, c = _mesh_pos()
        me, sibling = (x, y, c), (x, y, 1 - c)
        chips = [(1 - x, y), (x, 1 - y), (1 - x, 1 - y)]

        def copy(t, k, block, to, own=False):
            dst = outs[t].at[_flat(block)]
            return pltpu.make_async_remote_copy(
                src_ref=ins[t].at[_flat(block)] if own else dst, dst_ref=dst,
                send_sem=send_sems.at[t, k], recv_sem=recv_sems.at[t, k],
                device_id=to, device_id_type=pl.DeviceIdType.MESH)

        first, passed = [], []
        for t in range(n):
            row = [copy(t, 0, me, sibling, own=True)]
            row += [copy(t, 1 + j, me, (*chip, c), own=True) for j, chip in enumerate(chips)]
            for cp in row:
                cp.start()
            first += row
        for t in range(n):
            for j, chip in enumerate(chips):
                copy(t, 1 + j, (*chip, c), me).wait_recv()
                cp = copy(t, 4 + j, (*chip, c), sibling)
                cp.start()
                passed.append(cp)
        for t in range(n):
            copy(t, 0, sibling, me).wait_recv()
            for j, chip in enumerate(chips):
                copy(t, 4 + j, (*chip, 1 - c), me).wait_recv()
        for cp in first + passed:
            cp.wait_send()

    return pl.pallas_call(
        body, name="all_gather_weights",
        in_specs=[_HBM] * n, out_specs=[_HBM] * n,
        out_shape=[jax.ShapeDtypeStruct(s.shape, s.dtype) for s in slotted],
        input_output_aliases={t: t for t in range(n)},
        scratch_shapes=[pltpu.SemaphoreType.DMA((n, 7)), pltpu.SemaphoreType.DMA((n, 7))],
        )(*slotted)


def exchange_partials(arrays, after):
    n, na = len(arrays), len(after)

    def body(*refs):
        ins, outs = refs[:n], refs[n + na:2 * n + na]
        send_sems, recv_sems, local_sems = refs[2 * n + na:]
        me = _mesh_pos()
        copies, locals_ = [], []
        for t in range(n):
            cp = pltpu.make_async_copy(ins[t], outs[t].at[_flat(me)], local_sems.at[t])
            cp.start()
            locals_.append(cp)
            for k in range(1, N_DEV):
                peer = _peer(me, k)
                send = pltpu.make_async_remote_copy(
                    src_ref=ins[t], dst_ref=outs[t].at[_flat(me)],
                    send_sem=send_sems.at[t, k - 1], recv_sem=recv_sems.at[t, k - 1],
                    device_id=peer, device_id_type=pl.DeviceIdType.MESH)
                send.start()
                recv = pltpu.make_async_remote_copy(
                    src_ref=ins[t], dst_ref=outs[t].at[_flat(peer)],
                    send_sem=send_sems.at[t, k - 1], recv_sem=recv_sems.at[t, k - 1],
                    device_id=peer, device_id_type=pl.DeviceIdType.MESH)
                copies.append((send, recv))
        for send, recv in copies:
            recv.wait_recv()
        for send, recv in copies:
            send.wait_send()
        for cp in locals_:
            cp.wait()

    return pl.pallas_call(
        body, name="exchange_partials",
        in_specs=[_HBM] * n + [_ANY] * na, out_specs=[_HBM] * n,
        out_shape=[jax.ShapeDtypeStruct((N_DEV,) + a.shape, a.dtype) for a in arrays],
        scratch_shapes=[pltpu.SemaphoreType.DMA((n, 7)), pltpu.SemaphoreType.DMA((n, 7)),
                        pltpu.SemaphoreType.DMA((n,))],
        )(*arrays, *after)


_SEM = pl.BlockSpec(memory_space=pltpu.SEMAPHORE)
_ANY = pl.BlockSpec(memory_space=pl.ANY)
_EFFECT = pltpu.SideEffectType.DATAFLOW_SIDE_EFFECTING


def _peer(me, k):
    return tuple(1 - v if bit else v for v, bit in zip(me, (k >> 2, (k >> 1) & 1, k & 1)))


def _sem_index(t, k):
    return t * (N_DEV - 1) + k - 1


def _split_ends(bufs, n, gather, me, k):
    peer = _peer(me, k)
    if gather:
        return (lambda t: bufs[t].at[_flat(me)]), (lambda t: bufs[t].at[_flat(me)]), (lambda t: bufs[t].at[_flat(peer)])
    return (lambda t: bufs[t].at[_flat(peer)]), (lambda t: bufs[n + t].at[k - 1]), (lambda t: bufs[n + t].at[k - 1])


def exchange_start(name, arrays, gather, after):
    n, na = len(arrays), len(after)
    bufs = list(arrays)
    if not gather:
        bufs += [lax.empty((N_DEV - 1,) + a.shape[1:], a.dtype) for a in arrays]
    nb = len(bufs)

    def body(*refs):
        send_sems, recv_sems = refs[nb + na], refs[nb + na + 1]
        token = refs[-1]
        me = _mesh_pos()
        for t in range(n):
            for k in range(1, N_DEV):
                src, dst, _ = _split_ends(refs[:nb], n, gather, me, k)
                pltpu.make_async_remote_copy(
                    src_ref=src(t), dst_ref=dst(t),
                    send_sem=send_sems.at[_sem_index(t, k)], recv_sem=recv_sems.at[_sem_index(t, k)],
                    device_id=_peer(me, k), device_id_type=pl.DeviceIdType.MESH).start()
        token[...] = jnp.zeros_like(token)

    out_shape = [pltpu.SemaphoreType.DMA((n * (N_DEV - 1),)), pltpu.SemaphoreType.DMA((n * (N_DEV - 1),))]
    out_shape += [pltpu.HBM(a.shape, a.dtype) for a in bufs]
    out_shape.append(jax.ShapeDtypeStruct((8, LANE), F32))
    args = [pltpu.with_memory_space_constraint(a, pltpu.HBM) for a in bufs] + list(after)
    outs = pl.pallas_call(
        body, name=name, out_shape=out_shape,
        in_specs=[_HBM] * nb + [_ANY] * na,
        out_specs=[_SEM, _SEM] + [_HBM] * nb + [pl.BlockSpec(memory_space=pltpu.VMEM)],
        input_output_aliases={i: 2 + i for i in range(nb)},
        compiler_params=pltpu.CompilerParams(has_side_effects=_EFFECT))(*args)
    return outs[0], outs[1], list(outs[2:2 + nb]), outs[-1]


def exchange_wait(name, started, gather, after):
    send_sems, recv_sems, bufs, _ = started
    nb, na = len(bufs), len(after)
    n = nb if gather else nb // 2

    def body(*refs):
        send_sems_ref, recv_sems_ref = refs[nb], refs[nb + 1]
        me = _mesh_pos()
        for t in range(n):
            for k in range(1, N_DEV):
                src, _, land = _split_ends(refs[:nb], n, gather, me, k)
                cp = pltpu.make_async_remote_copy(
                    src_ref=src(t), dst_ref=land(t),
                    send_sem=send_sems_ref.at[_sem_index(t, k)], recv_sem=recv_sems_ref.at[_sem_index(t, k)],
                    device_id=_peer(me, k), device_id_type=pl.DeviceIdType.MESH)
                cp.wait_send()
                cp.wait_recv()

    outs = pl.pallas_call(
        body, name=name, out_shape=[pltpu.HBM(a.shape, a.dtype) for a in bufs],
        in_specs=[_HBM] * nb + [_SEM, _SEM] + [_ANY] * na, out_specs=[_HBM] * nb,
        input_output_aliases={i: i for i in range(nb)},
        compiler_params=pltpu.CompilerParams(has_side_effects=_EFFECT))(
            *bufs, send_sems, recv_sems, *after)
    return list(outs)


def _adamw_math(w, g, m, v):
    m = ADAM_B1 * m + (1.0 - ADAM_B1) * g
    v = ADAM_B2 * v + (1.0 - ADAM_B2) * (g * g)
    m_hat = m / (1.0 - ADAM_B1 ** ADAM_STEP)
    v_hat = v / (1.0 - ADAM_B2 ** ADAM_STEP)
    delta = -ADAM_LR * (m_hat / (jnp.sqrt(v_hat) + ADAM_EPS) + ADAM_WD * w)
    return delta, m, v


def reduce_adamw(name, land, w, m, v, tr, own=None, me_arr=None):
    R, C = w.shape
    S, _, Cp = land.shape

    def body(*refs):
        if own is not None:
            _, own_ref, l_ref, w_ref, m_ref, v_ref, g_ref, d_ref, nm_ref, nv_ref = refs
            g = own_ref[:, 0:C].astype(F32)
            first = 0
        else:
            l_ref, w_ref, m_ref, v_ref, g_ref, d_ref, nm_ref, nv_ref = refs
            g = l_ref[0, :, 0:C].astype(F32)
            first = 1
        for s in range(first, S):
            g = g + l_ref[s, :, 0:C].astype(F32)
        delta, nm, nv = _adamw_math(w_ref[...], g, m_ref[...], v_ref[...])
        g_ref[...] = g
        d_ref[...] = delta
        nm_ref[...] = nm
        nv_ref[...] = nv

    shp = jax.ShapeDtypeStruct((R, C), F32)
    if own is None:
        blk = pl.BlockSpec((tr, C), lambda i: (i, 0))
        return pl.pallas_call(
            body, name=name, grid=(R // tr,),
            in_specs=[pl.BlockSpec((S, tr, Cp), lambda i: (0, i, 0)), blk, blk, blk],
            out_specs=[blk, blk, blk, blk], out_shape=[shp, shp, shp, shp],
            compiler_params=_params(("parallel",)))(land, w, m, v)
    blk = pl.BlockSpec((tr, C), lambda i, me: (i, 0))
    return pl.pallas_call(
        body, name=name,
        grid_spec=pltpu.PrefetchScalarGridSpec(
            num_scalar_prefetch=1, grid=(R // tr,),
            in_specs=[pl.BlockSpec((None, tr, Cp), lambda i, me: (me[0], i, 0)),
                      pl.BlockSpec((S, tr, Cp), lambda i, me: (0, i, 0)), blk, blk, blk],
            out_specs=[blk, blk, blk, blk]),
        out_shape=[shp, shp, shp, shp],
        compiler_params=_params(("parallel",)))(me_arr, own, land, w, m, v)


def _row_tile(r, cap):
    t = min(r, cap)
    while r % t or t % 8:
        t -= 8
    return t


def kernel(x, norm_ffn1_g, ffn1_w_gate, ffn1_w_up, ffn1_w_down, norm_mix_g, w_in, rel_bias, w_out_att, w_out_ret, w_out, norm_ffn2_g, ffn2_w_gate, ffn2_w_up, ffn2_w_down, norm_final_g, loss_target, m_norm_ffn1_g, m_ffn1_w_gate, m_ffn1_w_up, m_ffn1_w_down, m_norm_mix_g, m_w_in, m_rel_bias, m_w_out_att, m_w_out_ret, m_w_out, m_norm_ffn2_g, m_ffn2_w_gate, m_ffn2_w_up, m_ffn2_w_down, m_norm_final_g, v_norm_ffn1_g, v_ffn1_w_gate, v_ffn1_w_up, v_ffn1_w_down, v_norm_mix_g, v_w_in, v_rel_bias, v_w_out_att, v_w_out_ret, v_w_out, v_norm_ffn2_g, v_ffn2_w_gate, v_ffn2_w_up, v_ffn2_w_down, v_norm_final_g):
    T, D = x.shape[1], x.shape[2]
    A = w_out_att.shape[1]
    H = A // HEAD_DIM
    nf = ffn1_w_gate.shape[2]
    nfp = _round_up(nf, LANE)
    nin = w_in.shape[2]
    nd = w_out.shape[1]
    assert nin % LANE == 0 and nd % LANE == 0 and (7 * A) % nd == 0 and T % ATT_TQ == 0
    tm = min(512, T)
    tw = min(2048, T)
    tw_in = min(1024, T)
    tn = min(256, T)
    x0 = x[0]
    tgt = loss_target[0]

    me_arr = (4 * lax.axis_index("x") + 2 * lax.axis_index("y") + lax.axis_index("c")).astype(jnp.int32).reshape(1)

    def slot(tag, w, rows_p=None, cols_p=None):
        return cast_shard("cast_" + tag, w[0], rows_p or w.shape[1], cols_p or w.shape[2], me_arr)

    Wg1, Wu1 = all_gather_blocks([slot("wg1", ffn1_w_gate, cols_p=nfp), slot("wu1", ffn1_w_up, cols_p=nfp)])
    later = [("wd1", [slot("wd1", ffn1_w_down, rows_p=nfp)]), ("win", [slot("win", w_in)]),
             ("wout", [slot("woa", w_out_att), slot("wor", w_out_ret), slot("wo", w_out)]),
             ("wgu2", [slot("wg2", ffn2_w_gate, cols_p=nfp), slot("wu2", ffn2_w_up, cols_p=nfp)]),
             ("wd2", [slot("wd2", ffn2_w_down, rows_p=nfp)])]
    ag_started = {}
    order = Wg1
    for tag, ws in later:
        ag_started[tag] = exchange_start("ag_start_" + tag, ws, True, [order])
        order = ag_started[tag][3]

    def gathered(tag, after):
        return exchange_wait("ag_wait_" + tag, ag_started[tag], True, [after])

    def swiglu(prods, _):
        a, b = prods
        return a, b, a * _sigmoid(a) * b

    def ffn_fwd(tag, xin, g, get_wgu, get_wd, after=()):
        h = rmsnorm_fwd(tag + "_norm", xin, g, tn, after)
        Wg, Wu = get_wgu(h)
        a, b, mid = mm_block(tag + "_up", T, tm, [(h, "full", D, 0)], [(Wg, 0, False), (Wu, 0, False)],
                             [], [("3d", nfp, CDT)] * 3, swiglu)
        Wd, = get_wd(mid)
        xo = mm_reduce_j(tag + "_down", T, tm, [(mid, "3d", nfp, 0, Wd, False)], D, F32, res=xin, scale=0.5,
                         jstep=2)
        return h, a, b, mid, xo, (Wg, Wu, Wd)

    h1, a1, b1, mid1, x1, (_, _, Wd1) = ffn_fwd(
        "ffn1", x0, norm_ffn1_g, lambda h: (Wg1, Wu1), lambda mid: gathered("wd1", mid), after=[order])
    h2 = rmsnorm_fwd("mix_norm", x1, norm_mix_g, tn)
    Win, = gathered("win", h2)
    proj, = mm_block("in_proj", T, tm, [(h2, "full", D, 0)], [(Win, 0, False)], [], [("col", nin, CDT)],
                     lambda p, _: p)
    biasm = attn_bias(rel_bias[0], ATT_TQ)
    att = attn_fwd(proj, biasm, A, ATT_TQ)
    tables = _retention_tables(T, H, RET_BLK)
    retg, ret_raw, states = ret_fwd(proj, tables, A, RET_BLK)
    Woa, Wor, Wo = gathered("wout", retg)
    goff = 7 * A // nd

    def merge(prods, ex):
        ba, br = prods
        ga, gr = ex
        return ba, br, _sigmoid(ga) * ba + _sigmoid(gr) * br

    ba, br, merged = mm_block(
        "branches", T, tm, [(att, "full", A, 0), (retg, "full", A, 0)], [(Woa, 0, False), (Wor, 1, False)],
        [(proj, "col", nd, goff), (proj, "col", nd, goff + N_DEV)], [("col", nd, CDT)] * 3, merge)
    x2 = mm_reduce_j("out_proj", T, tm, [(merged, "col", nd, 0, Wo, False)], D, F32, res=x1, scale=1.0,
                     jstep=N_DEV)
    h3, a2, b2, mid2, x3, (Wg2, Wu2, Wd2) = ffn_fwd(
        "ffn2", x2, norm_ffn2_g, lambda h: gathered("wgu2", h), lambda mid: gathered("wd2", mid))

    dx3, dx3h, dgf, loss_part = loss_head(x3, norm_final_g.reshape(1, D), tgt, tn)

    def swiglu_bwd(prods, ex):
        dm, = prods
        a, b = ex
        sg = _sigmoid(a)
        return dm * b * (sg * (1.0 + a * (1.0 - sg))), dm * (a * sg)

    def ffn_bwd(tag, dxh, h, a, b, mid, Wg, Wu, Wd):
        da, db = mm_block(tag + "_down_bwd", T, tm, [(dxh, "full", D, 0)], [(Wd, 0, True)],
                          [(a, "3d", nfp, 0), (b, "3d", nfp, 0)], [("3d", nfp, CDT)] * 2, swiglu_bwd)
        dWd = mm_reduce_i(tag + "_dwd", T, tw, (mid, "3d", nfp, 0), (dxh, "full", D, 0))
        dWg = mm_reduce_i(tag + "_dwg", T, tw, (h, "full", D, 0), (da, "3d", nfp, 0))
        dWu = mm_reduce_i(tag + "_dwu", T, tw, (h, "full", D, 0), (db, "3d", nfp, 0))
        sent = exchange_start("rs_start_" + tag, [dWg, dWu, dWd], False, [])
        dh = mm_reduce_j(tag + "_up_bwd", T, tm, [(da, "3d", nfp, 0, Wg, True), (db, "3d", nfp, 0, Wu, True)],
                         D, F32, after=[sent[3]], jstep=2)
        return dh, sent

    dh3, sent_ffn2 = ffn_bwd("ffn2", dx3h, h3, a2, b2, mid2, Wg2, Wu2, Wd2)
    dx2, dx2c, dg2 = rmsnorm_bwd("ffn2_norm_bwd", x2, norm_ffn2_g, dh3, dx3, 1.0, tn)

    def merge_bwd(prods, ex):
        dmg, = prods
        ba_, br_, ga, gr = ex
        sa, sr = _sigmoid(ga), _sigmoid(gr)
        return dmg * sa, dmg * sr, dmg * ba_ * sa * (1.0 - sa), dmg * br_ * sr * (1.0 - sr)

    dba, dbr, dga, dgr = mm_block(
        "out_proj_bwd", T, tm, [(dx2c, "full", D, 0)], [(Wo, 0, True)],
        [(ba, "col", nd, 0), (br, "col", nd, 0), (proj, "col", nd, goff), (proj, "col", nd, goff + N_DEV)],
        [("col", nd, CDT)] * 4, merge_bwd)
    dWo = mm_reduce_i("dwo", T, tw, (merged, "col", nd, 0), (dx2c, "full", D, 0))
    dWoa = mm_reduce_i("dwoa", T, tw, (att, "full", A, 0), (dba, "col", nd, 0))
    dWor = mm_reduce_i("dwor", T, tw, (retg, "full", A, 0), (dbr, "col", nd, 0))
    sent_mix = exchange_start("rs_start_mix", [dWoa, dWor, dWo], False, [])
    datt = mm_reduce_j("att_out_bwd", T, tm, [(dba, "col", nd, 0, Woa, True)], A, CDT, after=[sent_mix[3]],
                       jstep=N_DEV)
    dretg = mm_reduce_j("ret_out_bwd", T, tm, [(dbr, "col", nd, 0, Wor, True)], A, CDT, jstep=N_DEV)
    dq_r, dk_r, dv_r, dg_r = ret_bwd(proj, tables, ret_raw, states, dretg, A, RET_BLK)
    dq_a, dk_a, dv_a, dst = attn_bwd(proj, biasm, datt, A, ATT_TQ)
    dbias = jnp.pad(attn_bias_grad(dst, ATT_TQ), ((0, 0), (0, N_REL_PAD - N_REL)))
    dproj = jnp.concatenate([dq_a, dk_a, dv_a, dq_r, dk_r, dv_r, dg_r, dga, dgr], axis=1)
    dWin = mm_reduce_i("dwin", T, tw_in, (h2, "full", D, 0), (dproj, "col", nin, 0))
    sent_win = exchange_start("rs_start_win", [dWin], False, [])
    dh2 = mm_reduce_j("in_proj_bwd", T, tm, [(dproj, "col", nin, 0, Win, True)], D, F32, after=[sent_win[3]],
                      jstep=2)
    dx1, dx1h, dgm = rmsnorm_bwd("mix_norm_bwd", x1, norm_mix_g, dh2, dx2, 0.5, tn)
    dh1, sent_ffn1 = ffn_bwd("ffn1", dx1h, h1, a1, b1, mid1, Wg1, Wu1, Wd1)
    grad_x, _, dg1 = rmsnorm_bwd("ffn1_norm_bwd", x0, norm_ffn1_g, dh1, dx1, 1.0, tn)

    dgains = jnp.concatenate([dg1, dgm, dg2, dgf, jnp.zeros((4, D), F32)], axis=0)

    def upd(name, own, land, w, m, v):
        r = w.shape[1]
        outs = reduce_adamw(name, land, w[0], m[0], v[0], _row_tile(r, 256), own=own, me_arr=me_arr)
        return [o[None] for o in outs]

    res = {}
    oWg2, oWu2, oWd2, lWg2, lWu2, lWd2 = exchange_wait("rs_wait_ffn2", sent_ffn2, False, [grad_x])
    res["ffn2_w_gate"] = upd("adamw_wg2", oWg2, lWg2, ffn2_w_gate, m_ffn2_w_gate, v_ffn2_w_gate)
    res["ffn2_w_up"] = upd("adamw_wu2", oWu2, lWu2, ffn2_w_up, m_ffn2_w_up, v_ffn2_w_up)
    res["ffn2_w_down"] = upd("adamw_wd2", oWd2, lWd2, ffn2_w_down, m_ffn2_w_down, v_ffn2_w_down)
    oWoa, oWor, oWo, lWoa, lWor, lWo = exchange_wait("rs_wait_mix", sent_mix, False, [res["ffn2_w_down"][1]])
    res["w_out_att"] = upd("adamw_woa", oWoa, lWoa, w_out_att, m_w_out_att, v_w_out_att)
    res["w_out_ret"] = upd("adamw_wor", oWor, lWor, w_out_ret, m_w_out_ret, v_w_out_ret)
    res["w_out"] = upd("adamw_wo", oWo, lWo, w_out, m_w_out, v_w_out)
    oWin, lWin = exchange_wait("rs_wait_win", sent_win, False, [res["w_out"][1]])
    res["w_in"] = upd("adamw_win", oWin, lWin, w_in, m_w_in, v_w_in)
    lgains, lbias = exchange_partials([dgains, dbias], [res["w_in"][1]])
    oWg1, oWu1, oWd1, lWg1, lWu1, lWd1 = exchange_wait("rs_wait_ffn1", sent_ffn1, False, [lgains])
    res["ffn1_w_gate"] = upd("adamw_wg1", oWg1, lWg1, ffn1_w_gate, m_ffn1_w_gate, v_ffn1_w_gate)
    res["ffn1_w_up"] = upd("adamw_wu1", oWu1, lWu1, ffn1_w_up, m_ffn1_w_up, v_ffn1_w_up)
    res["ffn1_w_down"] = upd("adamw_wd1", oWd1, lWd1, ffn1_w_down, m_ffn1_w_down, v_ffn1_w_down)

    def stack_gains(a, b, c_, d):
        return jnp.concatenate([a, b, c_, d.reshape(1, D), jnp.zeros((4, D), F32)], axis=0)

    gw = stack_gains(norm_ffn1_g, norm_mix_g, norm_ffn2_g, norm_final_g)
    gm = stack_gains(m_norm_ffn1_g, m_norm_mix_g, m_norm_ffn2_g, m_norm_final_g)
    gv = stack_gains(v_norm_ffn1_g, v_norm_mix_g, v_norm_ffn2_g, v_norm_final_g)
    gains = reduce_adamw("adamw_gains", lgains, gw, gm, gv, 8)

    def padb(t):
        return jnp.pad(t[0], ((0, 0), (0, N_REL_PAD - N_REL)))

    bias = [o[:, :N_REL][None] for o in
            reduce_adamw("adamw_bias", lbias, padb(rel_bias), padb(m_rel_bias), padb(v_rel_bias), H)]
    res["norm_ffn1_g"] = [o[0:1] for o in gains]
    res["norm_mix_g"] = [o[1:2] for o in gains]
    res["norm_ffn2_g"] = [o[2:3] for o in gains]
    res["norm_final_g"] = [o[3] for o in gains]
    res["rel_bias"] = bias

    loss = lax.psum(loss_part[0, 0], MESH_AXES)
    names = ["norm_ffn1_g", "ffn1_w_gate", "ffn1_w_up", "ffn1_w_down", "norm_mix_g", "w_in", "rel_bias",
             "w_out_att", "w_out_ret", "w_out", "norm_ffn2_g", "ffn2_w_gate", "ffn2_w_up", "ffn2_w_down",
             "norm_final_g"]
    out = [loss, grad_x[None]]
    for k in range(4):
        out += [res[nm][k] for nm in names]
    return tuple(out)
```

```python
import functools
import math

import jax
import jax.numpy as jnp
import numpy as np
from jax import lax
from jax.experimental import pallas as pl
from jax.experimental.pallas import tpu as pltpu

F32 = jnp.float32
CDT = jnp.bfloat16

N_DEV = 8
CHUNK = 64
N_PREV_CHUNKS = 8
BAND = N_PREV_CHUNKS * CHUNK
HEAD_DIM = 128
MAX_REL_DIST = 128
N_REL = 2 * MAX_REL_DIST + 1
N_REL_PAD = 384
ROPE_BASE = 10000.0
EPS = 1e-6
NEG = -1e30
LANE = 128
ATT_TQ = 256
RET_BLK = 256
VMEM_LIMIT = 48 * 1024 * 1024

ADAM_LR = 0.001
ADAM_B1 = 0.9
ADAM_B2 = 0.999
ADAM_EPS = 1e-08
ADAM_WD = 0.01
ADAM_STEP = 10

MESH_AXES = ("x", "y", "c")
_NT = (((1,), (1,)), ((), ()))
_TN = (((0,), (0,)), ((), ()))


def _round_up(v, m):
    return (v + m - 1) // m * m


def _params(sem=None):
    return pltpu.CompilerParams(dimension_semantics=sem, vmem_limit_bytes=VMEM_LIMIT)


def _sigmoid(v):
    return 0.5 * (jnp.tanh(0.5 * v) + 1.0)


def _bspec(kind, tm, w, off, order, jmap=lambda j: j):
    def wrap(f):
        if order == "ji":
            return lambda j, i: f(i, jmap(j))
        return lambda i, j: f(i, jmap(j))
    if kind == "full":
        return pl.BlockSpec((tm, w), wrap(lambda i, j: (i, 0)))
    if kind == "col":
        return pl.BlockSpec((tm, w), wrap(lambda i, j: (i, j + off)))
    assert kind == "3d"
    return pl.BlockSpec((None, tm, w), wrap(lambda i, j: (j, i, 0)))


def _wspec(w, order, jmap=lambda j: j):
    if order == "ji":
        return pl.BlockSpec((None,) + w.shape[1:], lambda j, i: (jmap(j), 0, 0))
    return pl.BlockSpec((None,) + w.shape[1:], lambda i, j: (jmap(j), 0, 0))


def _width(arr, kind, w):
    return arr.shape[-1] if kind in ("full", "3d") else w


def mm_block(name, T, tm, lhs, wts, extras, outs, epilogue):
    nl, nw, ne = len(lhs), len(wts), len(extras)
    ni = T // tm

    def body(*refs):
        l = refs[:nl]
        w = refs[nl:nl + nw]
        e = refs[nl + nw:nl + nw + ne]
        o = refs[nl + nw + ne:]
        prods = []
        for k, (_, li, tr) in enumerate(wts):
            a = l[li][...]
            if tr:
                prods.append(lax.dot_general(a, w[k][...], _NT, preferred_element_type=F32))
            else:
                prods.append(jnp.dot(a, w[k][...], preferred_element_type=F32))
        res = epilogue(prods, [r[...].astype(F32) for r in e])
        for r, val in zip(o, res):
            r[...] = val.astype(r.dtype)

    in_specs = [_bspec(k, tm, _width(a, k, w), off, "ji") for (a, k, w, off) in lhs]
    in_specs += [_wspec(w, "ji") for (w, _, _) in wts]
    in_specs += [_bspec(k, tm, _width(a, k, w), off, "ji") for (a, k, w, off) in extras]
    out_specs, out_shape = [], []
    for (kind, w, dt) in outs:
        out_specs.append(_bspec(kind, tm, w, 0, "ji"))
        if kind == "3d":
            out_shape.append(jax.ShapeDtypeStruct((N_DEV, T, w), dt))
        else:
            out_shape.append(jax.ShapeDtypeStruct((T, N_DEV * w), dt))
    args = [a for (a, _, _, _) in lhs] + [w for (w, _, _) in wts] + [a for (a, _, _, _) in extras]
    return pl.pallas_call(
        body, name=name, grid=(N_DEV, ni), in_specs=in_specs, out_specs=out_specs,
        out_shape=out_shape, compiler_params=_params(("parallel", "parallel")))(*args)


def mm_reduce_j(name, T, tm, pairs, out_w, out_dtype, res=None, scale=1.0, after=(), jstep=1):
    terms = [(p, u) for u in range(jstep) for p in pairs]
    nt = len(terms)
    nj = N_DEV // jstep
    ni = T // tm

    def body(*refs):
        xs = refs[:nt]
        ws = refs[nt:2 * nt]
        rest = refs[2 * nt:len(refs) - 2 - len(after)] + refs[len(refs) - 2:]
        if res is not None:
            res_ref, o_ref, acc = rest
        else:
            o_ref, acc = rest
        j = pl.program_id(1)

        @pl.when(j == 0)
        def _():
            acc[...] = jnp.zeros_like(acc)

        tot = None
        for k, (p, _) in enumerate(terms):
            if p[5]:
                d = lax.dot_general(xs[k][...], ws[k][...], _NT, preferred_element_type=F32)
            else:
                d = jnp.dot(xs[k][...], ws[k][...], preferred_element_type=F32)
            tot = d if tot is None else tot + d
        acc[...] += tot

        @pl.when(j == nj - 1)
        def _():
            if res is not None:
                o_ref[...] = (res_ref[...] + scale * acc[...]).astype(o_ref.dtype)
            else:
                o_ref[...] = acc[...].astype(o_ref.dtype)

    def jmap(u):
        return lambda j: j * jstep + u

    in_specs = [_bspec(p[1], tm, _width(p[0], p[1], p[2]), p[3], "ij", jmap(u)) for (p, u) in terms]
    in_specs += [_wspec(p[4], "ij", jmap(u)) for (p, u) in terms]
    args = [p[0] for (p, _) in terms] + [p[4] for (p, _) in terms]
    if res is not None:
        in_specs.append(pl.BlockSpec((tm, out_w), lambda i, j: (i, 0)))
        args.append(res)
    in_specs += [_ANY] * len(after)
    args += list(after)
    return pl.pallas_call(
        body, name=name, grid=(ni, nj), in_specs=in_specs,
        out_specs=pl.BlockSpec((tm, out_w), lambda i, j: (i, 0)),
        out_shape=jax.ShapeDtypeStruct((T, out_w), out_dtype),
        scratch_shapes=[pltpu.VMEM((tm, out_w), F32)],
        compiler_params=_params(("parallel", "arbitrary")))(*args)


def mm_reduce_i(name, T, tm, a, b):
    ni = T // tm
    rows = _width(a[0], a[1], a[2])
    cols = _width(b[0], b[1], b[2])

    def body(a_ref, b_ref, o_ref, acc):
        i = pl.program_id(1)

        @pl.when(i == 0)
        def _():
            acc[...] = jnp.zeros_like(acc)

        acc[...] += lax.dot_general(a_ref[...], b_ref[...], _TN, preferred_element_type=F32)

        @pl.when(i == ni - 1)
        def _():
            o_ref[...] = acc[...].astype(o_ref.dtype)

    return pl.pallas_call(
        body, name=name, grid=(N_DEV, ni),
        in_specs=[_bspec(a[1], tm, rows, a[3], "ji"), _bspec(b[1], tm, cols, b[3], "ji")],
        out_specs=pl.BlockSpec((None, rows, cols), lambda j, i: (j, 0, 0)),
        out_shape=jax.ShapeDtypeStruct((N_DEV, rows, cols), CDT),
        scratch_shapes=[pltpu.VMEM((rows, cols), F32)],
        compiler_params=_params(("parallel", "arbitrary")))(a[0], b[0])


def _rms_bwd_math(xv, g, dy):
    r = lax.rsqrt(jnp.mean(xv * xv, axis=-1, keepdims=True) + EPS)
    xn = xv * r
    dxn = dy * g
    dx = r * (dxn - xn * jnp.mean(dxn * xn, axis=-1, keepdims=True))
    dg = jnp.sum(dy * xn, axis=0, keepdims=True)
    return dx, dg


def rmsnorm_fwd(name, x, g, tm, after=()):
    T, D = x.shape

    def body(x_ref, g_ref, *rest):
        o_ref = rest[-1]
        xv = x_ref[...]
        r = lax.rsqrt(jnp.mean(xv * xv, axis=-1, keepdims=True) + EPS)
        o_ref[...] = (xv * r * g_ref[...]).astype(o_ref.dtype)

    return pl.pallas_call(
        body, name=name, grid=(T // tm,),
        in_specs=[pl.BlockSpec((tm, D), lambda i: (i, 0)), pl.BlockSpec((1, D), lambda i: (0, 0))]
        + [_ANY] * len(after),
        out_specs=pl.BlockSpec((tm, D), lambda i: (i, 0)),
        out_shape=jax.ShapeDtypeStruct((T, D), CDT),
        compiler_params=_params(("parallel",)))(x, g, *after)


def rmsnorm_bwd(name, x, g, dh, dres, cscale, tm, after=()):
    T, D = x.shape

    def body(x_ref, g_ref, dh_ref, dres_ref, *rest):
        dx_ref, dxc_ref, dg_ref = rest[len(after):]
        i = pl.program_id(0)
        dx, dg = _rms_bwd_math(x_ref[...], g_ref[...], dh_ref[...])
        dx = dres_ref[...] + dx
        dx_ref[...] = dx
        dxc_ref[...] = (cscale * dx).astype(dxc_ref.dtype)

        @pl.when(i == 0)
        def _():
            dg_ref[...] = jnp.zeros_like(dg_ref)

        dg_ref[...] += dg

    row = pl.BlockSpec((tm, D), lambda i: (i, 0))
    vec = pl.BlockSpec((1, D), lambda i: (0, 0))
    return pl.pallas_call(
        body, name=name, grid=(T // tm,), in_specs=[row, vec, row, row] + [_ANY] * len(after),
        out_specs=[row, row, vec],
        out_shape=[jax.ShapeDtypeStruct((T, D), F32), jax.ShapeDtypeStruct((T, D), CDT),
                   jax.ShapeDtypeStruct((1, D), F32)],
        compiler_params=_params(("arbitrary",)))(x, g, dh, dres, *after)


def loss_head(x, g, tgt, tm):
    T, D = x.shape

    def body(x_ref, g_ref, t_ref, dx_ref, dxc_ref, dg_ref, loss_ref):
        i = pl.program_id(0)
        xv = x_ref[...]
        gv = g_ref[...]
        r = lax.rsqrt(jnp.mean(xv * xv, axis=-1, keepdims=True) + EPS)
        err = xv * r * gv - t_ref[...]
        part = jnp.sum(jnp.mean(err * err, axis=-1, keepdims=True), axis=0, keepdims=True)
        dx, dg = _rms_bwd_math(xv, gv, err / D)
        dx_ref[...] = dx
        dxc_ref[...] = (0.5 * dx).astype(dxc_ref.dtype)

        @pl.when(i == 0)
        def _():
            dg_ref[...] = jnp.zeros_like(dg_ref)
            loss_ref[...] = jnp.zeros_like(loss_ref)

        dg_ref[...] += dg
        loss_ref[...] += jnp.broadcast_to(0.5 * part, loss_ref.shape)

    row = pl.BlockSpec((tm, D), lambda i: (i, 0))
    vec = pl.BlockSpec((1, D), lambda i: (0, 0))
    return pl.pallas_call(
        body, name="loss_head", grid=(T // tm,), in_specs=[row, vec, row],
        out_specs=[row, row, vec, pl.BlockSpec((1, LANE), lambda i: (0, 0))],
        out_shape=[jax.ShapeDtypeStruct((T, D), F32), jax.ShapeDtypeStruct((T, D), CDT),
                   jax.ShapeDtypeStruct((1, D), F32), jax.ShapeDtypeStruct((1, LANE), F32)],
        compiler_params=_params(("arbitrary",)))(x, g, tgt)


def _skew_rows(z, left):
    tq, kw = z.shape
    row = lax.broadcasted_iota(jnp.int32, (tq, kw), 0)
    s = 1
    while s < tq:
        z = jnp.where((row & s) != 0, pltpu.roll(z, kw - s if left else s, 1), z)
        s *= 2
    return z


REL_HI = BAND + MAX_REL_DIST
REL_LO = BAND - MAX_REL_DIST


def attn_bias(rel_bias, tq):
    H = rel_bias.shape[0]
    kw = BAND + tq
    by_skew = jnp.concatenate(
        [jnp.broadcast_to(rel_bias[:, N_REL - 1:], (H, REL_LO)), rel_bias[:, ::-1],
         jnp.broadcast_to(rel_bias[:, :1], (H, kw - REL_HI - 1))], axis=1).reshape(H, 1, kw)

    def body(t_ref, o_ref):
        t = t_ref[...]
        qi = lax.broadcasted_iota(jnp.int32, (tq, kw), 0)
        kj = lax.broadcasted_iota(jnp.int32, (tq, kw), 1)
        b = _skew_rows(jnp.broadcast_to(t, (tq, kw)), left=False)
        b = jnp.where(kj < qi, t[:, 0:1], b)
        qc = qi // CHUNK
        kc = kj // CHUNK - N_PREV_CHUNKS
        valid = (kc <= qc) & (kc >= qc - N_PREV_CHUNKS)
        o_ref[...] = jnp.where(valid, b, NEG)

    return pl.pallas_call(
        body, name="attn_bias", grid=(H,),
        in_specs=[pl.BlockSpec((None, 1, kw), lambda h: (h, 0, 0))],
        out_specs=pl.BlockSpec((None, tq, kw), lambda h: (h, 0, 0)),
        out_shape=jax.ShapeDtypeStruct((H, tq, kw), F32),
        compiler_params=_params(("parallel",)))(by_skew)


def attn_bias_grad(dst, tq):
    H = dst.shape[0]
    kw = BAND + tq

    def body(d_ref, o_ref):
        z = _skew_rows(d_ref[...], left=True)
        qi = lax.broadcasted_iota(jnp.int32, (tq, kw), 0)
        kj = lax.broadcasted_iota(jnp.int32, (tq, kw), 1)
        wrapped = kj + qi >= kw
        c = jnp.sum(jnp.where(wrapped, 0.0, z), axis=0, keepdims=True)
        cw = jnp.sum(jnp.sum(jnp.where(wrapped, z, 0.0), axis=0, keepdims=True), axis=1, keepdims=True)
        lane = lax.broadcasted_iota(jnp.int32, (1, kw), 1)
        ahead = jnp.sum(jnp.where(lane >= REL_HI, c, 0.0), axis=1, keepdims=True)
        behind = jnp.sum(jnp.where(lane <= REL_LO, c, 0.0), axis=1, keepdims=True) + cw
        o_ref[...] = jnp.where(lane == REL_HI, ahead, jnp.where(lane == REL_LO, behind, c))

    by_skew = pl.pallas_call(
        body, name="attn_bias_grad", grid=(H,),
        in_specs=[pl.BlockSpec((None, tq, kw), lambda h: (h, 0, 0))],
        out_specs=pl.BlockSpec((None, 1, kw), lambda h: (h, 0, 0)),
        out_shape=jax.ShapeDtypeStruct((H, 1, kw), F32),
        compiler_params=_params(("parallel",)))(dst)
    return by_skew[:, 0, REL_LO:REL_HI + 1][:, ::-1]


def _attn_scores(q, kpad, bm_ref, start, kw):
    k = kpad[pl.ds(start, kw), :]
    s = lax.dot_general(q, k, _NT, preferred_element_type=F32) * (HEAD_DIM ** -0.5) + bm_ref[...]
    col = lax.broadcasted_iota(jnp.int32, s.shape, 1)
    s = jnp.where(col < BAND - start, NEG, s)
    m = jnp.max(s, axis=-1, keepdims=True)
    p = jnp.exp(s - m)
    return p / jnp.sum(p, axis=-1, keepdims=True), k


def _fill_padded(pad_ref, src_ref, T):
    pad_ref[pl.ds(0, BAND), :] = jnp.zeros((BAND, HEAD_DIM), pad_ref.dtype)
    pad_ref[pl.ds(BAND, T), :] = src_ref[...]


def attn_fwd(proj, biasm, A, tq):
    T = proj.shape[0]
    H = A // HEAD_DIM
    kw = BAND + tq

    def body(q_ref, k_ref, v_ref, bm_ref, o_ref, kpad, vpad):
        qi = pl.program_id(1)

        @pl.when(qi == 0)
        def _():
            _fill_padded(kpad, k_ref, T)
            _fill_padded(vpad, v_ref, T)

        start = pl.multiple_of(qi * tq, tq)
        p, _ = _attn_scores(q_ref[...], kpad, bm_ref, start, kw)
        v = vpad[pl.ds(start, kw), :]
        o_ref[...] = jnp.dot(p.astype(CDT), v, preferred_element_type=F32).astype(o_ref.dtype)

    return pl.pallas_call(
        body, name="attn_fwd", grid=(H, T // tq),
        in_specs=[pl.BlockSpec((tq, HEAD_DIM), lambda h, i: (i, h)),
                  pl.BlockSpec((T, HEAD_DIM), lambda h, i: (0, H + h)),
                  pl.BlockSpec((T, HEAD_DIM), lambda h, i: (0, 2 * H + h)),
                  pl.BlockSpec((None, tq, kw), lambda h, i: (h, 0, 0))],
        out_specs=pl.BlockSpec((tq, HEAD_DIM), lambda h, i: (i, h)),
        out_shape=jax.ShapeDtypeStruct((T, A), CDT),
        scratch_shapes=[pltpu.VMEM((BAND + T, HEAD_DIM), CDT), pltpu.VMEM((BAND + T, HEAD_DIM), CDT)],
        compiler_params=_params(("parallel", "arbitrary")))(proj, proj, proj, biasm)


def attn_bwd(proj, biasm, datt, A, tq):
    T = proj.shape[0]
    H = A // HEAD_DIM
    kw = BAND + tq
    nq = T // tq
    scale = HEAD_DIM ** -0.5

    def body(q_ref, k_ref, v_ref, bm_ref, do_ref, dq_ref, dk_ref, dv_ref, dst_ref,
             kpad, vpad, dkacc, dvacc):
        qi = pl.program_id(1)

        @pl.when(qi == 0)
        def _():
            _fill_padded(kpad, k_ref, T)
            _fill_padded(vpad, v_ref, T)
            dkacc[...] = jnp.zeros_like(dkacc)
            dvacc[...] = jnp.zeros_like(dvacc)
            dst_ref[...] = jnp.zeros_like(dst_ref)

        start = pl.multiple_of(qi * tq, tq)
        q = q_ref[...]
        p, k = _attn_scores(q, kpad, bm_ref, start, kw)
        v = vpad[pl.ds(start, kw), :]
        do = do_ref[...]
        dp = lax.dot_general(do, v, _NT, preferred_element_type=F32)
        ds = p * (dp - jnp.sum(dp * p, axis=-1, keepdims=True))
        dst_ref[...] += ds
        dsb = ds.astype(CDT)
        dq_ref[...] = (jnp.dot(dsb, k, preferred_element_type=F32) * scale).astype(dq_ref.dtype)
        dkacc[pl.ds(start, kw), :] += lax.dot_general(dsb, q, _TN, preferred_element_type=F32) * scale
        dvacc[pl.ds(start, kw), :] += lax.dot_general(p.astype(CDT), do, _TN, preferred_element_type=F32)

        @pl.when(qi == nq - 1)
        def _():
            dk_ref[...] = dkacc[pl.ds(BAND, T), :].astype(dk_ref.dtype)
            dv_ref[...] = dvacc[pl.ds(BAND, T), :].astype(dv_ref.dtype)

    blk = pl.BlockSpec((tq, HEAD_DIM), lambda h, i: (i, h))
    col = pl.BlockSpec((T, HEAD_DIM), lambda h, i: (0, h))
    bias = pl.BlockSpec((None, tq, kw), lambda h, i: (h, 0, 0))
    return pl.pallas_call(
        body, name="attn_bwd", grid=(H, nq),
        in_specs=[blk,
                  pl.BlockSpec((T, HEAD_DIM), lambda h, i: (0, H + h)),
                  pl.BlockSpec((T, HEAD_DIM), lambda h, i: (0, 2 * H + h)),
                  bias, blk],
        out_specs=[blk, col, col, bias],
        out_shape=[jax.ShapeDtypeStruct((T, A), CDT), jax.ShapeDtypeStruct((T, A), CDT),
                   jax.ShapeDtypeStruct((T, A), CDT), jax.ShapeDtypeStruct((H, tq, kw), F32)],
        scratch_shapes=[pltpu.VMEM((BAND + T, HEAD_DIM), CDT), pltpu.VMEM((BAND + T, HEAD_DIM), CDT),
                        pltpu.VMEM((BAND + T, HEAD_DIM), F32), pltpu.VMEM((BAND + T, HEAD_DIM), F32)],
        compiler_params=_params(("parallel", "arbitrary")))(proj, proj, proj, biasm, datt)


def _retention_tables(T, H, blk):
    half = HEAD_DIM // 2
    inv = 1.0 / (ROPE_BASE ** (jnp.arange(0, HEAD_DIM, 2, dtype=F32) / HEAD_DIM))
    ang = jnp.arange(T, dtype=F32)[:, None] * inv[None, :]
    cos, sin = jnp.cos(ang), jnp.sin(ang)
    rc = jnp.concatenate([cos, cos], axis=1)
    rs = jnp.concatenate([-sin, sin], axis=1)
    assert rc.shape == (T, 2 * half)
    log_g = jnp.log(1.0 - 2.0 ** (-5.0 - jnp.arange(H, dtype=F32)))[:, None, None]
    idx = jnp.arange(blk, dtype=F32)
    n, m = idx[:, None], idx[None, :]
    same = (n // CHUNK) == (m // CHUNK)
    earlier = (m // CHUNK) < (n // CHUNK)
    dist = jnp.where(same, jnp.abs(n - m), n - m)[None]
    dmat = jnp.where((same | earlier)[None], jnp.exp(log_g * dist), 0.0)
    ones = jnp.ones((1, 1, HEAD_DIM), F32)
    qd = jnp.exp(log_g * (idx[None, :, None] + 1.0)) * ones
    kd = jnp.exp(log_g * (blk - 1.0 - idx[None, :, None])) * ones
    cd = jnp.exp(log_g * blk) * jnp.ones((1, 8, HEAD_DIM), F32)
    return rc, rs, dmat, qd, kd, cd


def _rot(v, rc, rs):
    return v * rc + pltpu.roll(v, HEAD_DIM // 2, 1) * rs


def _rot_bwd(dv, rc, rs):
    return dv * rc + pltpu.roll(dv * rs, HEAD_DIM // 2, 1)


def ret_fwd(proj, tables, A, blk):
    T = proj.shape[0]
    H = A // HEAD_DIM
    nb = T // blk
    rc, rs, dmat, qd, kd, cd = tables
    scale = HEAD_DIM ** -0.5

    def body(q_ref, k_ref, v_ref, g_ref, rc_ref, rs_ref, d_ref, qd_ref, kd_ref, cd_ref,
             y_ref, o_ref, st_ref, state):
        b = pl.program_id(1)

        @pl.when(b == 0)
        def _():
            state[...] = jnp.zeros_like(state)

        c, s = rc_ref[...], rs_ref[...]
        qs = (_rot(q_ref[...].astype(F32), c, s) * scale).astype(CDT)
        kr = _rot(k_ref[...].astype(F32), c, s)
        v = v_ref[...]
        sb = state[...].astype(CDT)
        a = lax.dot_general(qs, kr.astype(CDT), _NT, preferred_element_type=F32) * d_ref[...]
        o = jnp.dot(a.astype(CDT), v, preferred_element_type=F32)
        o = o + jnp.dot(qs, sb, preferred_element_type=F32) * qd_ref[...]
        st_ref[...] = sb
        state[...] = state[...] * cd_ref[0:1, :] + lax.dot_general(
            (kr * kd_ref[...]).astype(CDT), v, _TN, preferred_element_type=F32)
        o_ref[...] = o
        on = o * lax.rsqrt(jnp.mean(o * o, axis=-1, keepdims=True) + EPS)
        g = g_ref[...].astype(F32)
        y_ref[...] = (g * _sigmoid(g) * on).astype(y_ref.dtype)

    def pj(off):
        return pl.BlockSpec((blk, HEAD_DIM), lambda h, i: (i, off * H + h))

    tok = pl.BlockSpec((blk, HEAD_DIM), lambda h, i: (i, 0))
    out = pl.BlockSpec((blk, HEAD_DIM), lambda h, i: (i, h))

    def per_head(r):
        return pl.BlockSpec((None, r, HEAD_DIM), lambda h, i: (h, 0, 0))

    return pl.pallas_call(
        body, name="ret_fwd", grid=(H, nb),
        in_specs=[pj(3), pj(4), pj(5), pj(6), tok, tok,
                  pl.BlockSpec((None, blk, blk), lambda h, i: (h, 0, 0)),
                  per_head(blk), per_head(blk), per_head(8)],
        out_specs=[out, out, pl.BlockSpec((None, None, HEAD_DIM, HEAD_DIM), lambda h, i: (h, i, 0, 0))],
        out_shape=[jax.ShapeDtypeStruct((T, A), CDT), jax.ShapeDtypeStruct((T, A), F32),
                   jax.ShapeDtypeStruct((H, nb, HEAD_DIM, HEAD_DIM), CDT)],
        scratch_shapes=[pltpu.VMEM((HEAD_DIM, HEAD_DIM), F32)],
        compiler_params=_params(("parallel", "arbitrary")))(
            proj, proj, proj, proj, rc, rs, dmat, qd, kd, cd)


def ret_bwd(proj, tables, o_raw, states, dy, A, blk):
    T = proj.shape[0]
    H = A // HEAD_DIM
    nb = T // blk
    rc, rs, dmat, qd, kd, cd = tables
    scale = HEAD_DIM ** -0.5

    def body(q_ref, k_ref, v_ref, g_ref, rc_ref, rs_ref, d_ref, qd_ref, kd_ref, cd_ref,
             o_ref, st_ref, dy_ref, dq_ref, dk_ref, dv_ref, dg_ref, dstate):
        b = pl.program_id(1)

        @pl.when(b == 0)
        def _():
            dstate[...] = jnp.zeros_like(dstate)

        c, s = rc_ref[...], rs_ref[...]
        qs = (_rot(q_ref[...].astype(F32), c, s) * scale).astype(CDT)
        kr = _rot(k_ref[...].astype(F32), c, s)
        krb = kr.astype(CDT)
        kdb = (kr * kd_ref[...]).astype(CDT)
        v = v_ref[...]
        dmat_v = d_ref[...]
        a = lax.dot_general(qs, krb, _NT, preferred_element_type=F32) * dmat_v

        o = o_ref[...]
        r = lax.rsqrt(jnp.mean(o * o, axis=-1, keepdims=True) + EPS)
        on = o * r
        g = g_ref[...].astype(F32)
        sg = _sigmoid(g)
        dyv = dy_ref[...].astype(F32)
        dg_ref[...] = (dyv * on * (sg * (1.0 + g * (1.0 - sg)))).astype(dg_ref.dtype)
        don = dyv * (g * sg)
        do = r * (don - on * jnp.mean(don * on, axis=-1, keepdims=True))
        dob = do.astype(CDT)
        doq = (do * qd_ref[...]).astype(CDT)
        dsb = dstate[...].astype(CDT)

        dv = lax.dot_general(a.astype(CDT), dob, _TN, preferred_element_type=F32)
        dv = dv + jnp.dot(kdb, dsb, preferred_element_type=F32)
        dv_ref[...] = dv.astype(dv_ref.dtype)
        dpb = (lax.dot_general(dob, v, _NT, preferred_element_type=F32) * dmat_v).astype(CDT)
        dqs = jnp.dot(dpb, krb, preferred_element_type=F32)
        dqs = dqs + lax.dot_general(doq, st_ref[...], _NT, preferred_element_type=F32)
        dkr = lax.dot_general(dpb, qs, _TN, preferred_element_type=F32)
        dkr = dkr + lax.dot_general(v, dsb, _NT, preferred_element_type=F32) * kd_ref[...]
        dstate[...] = dstate[...] * cd_ref[0:1, :] + lax.dot_general(
            qs, doq, _TN, preferred_element_type=F32)
        dq_ref[...] = _rot_bwd(dqs * scale, c, s).astype(dq_ref.dtype)
        dk_ref[...] = _rot_bwd(dkr, c, s).astype(dk_ref.dtype)

    def pj(off):
        return pl.BlockSpec((blk, HEAD_DIM), lambda h, i: (nb - 1 - i, off * H + h))

    tok = pl.BlockSpec((blk, HEAD_DIM), lambda h, i: (nb - 1 - i, 0))
    out = pl.BlockSpec((blk, HEAD_DIM), lambda h, i: (nb - 1 - i, h))

    def per_head(r, w):
        return pl.BlockSpec((None, r, w), lambda h, i: (h, 0, 0))

    shp = jax.ShapeDtypeStruct((T, A), CDT)
    return pl.pallas_call(
        body, name="ret_bwd", grid=(H, nb),
        in_specs=[pj(3), pj(4), pj(5), pj(6), tok, tok, per_head(blk, blk),
                  per_head(blk, HEAD_DIM), per_head(blk, HEAD_DIM), per_head(8, HEAD_DIM),
                  out, pl.BlockSpec((None, None, HEAD_DIM, HEAD_DIM), lambda h, i: (h, nb - 1 - i, 0, 0)),
                  out],
        out_specs=[out, out, out, out], out_shape=[shp, shp, shp, shp],
        scratch_shapes=[pltpu.VMEM((HEAD_DIM, HEAD_DIM), F32)],
        compiler_params=_params(("parallel", "arbitrary")))(
            proj, proj, proj, proj, rc, rs, dmat, qd, kd, cd, o_raw, states, dy)


def _mesh_pos():
    return lax.axis_index("x"), lax.axis_index("y"), lax.axis_index("c")


def _flat(pos):
    return 4 * pos[0] + 2 * pos[1] + pos[2]


_HBM = pl.BlockSpec(memory_space=pltpu.HBM)


def cast_shard(name, w, rows_p, cols_p, me_arr):
    r, c = w.shape
    tr = _row_tile(math.gcd(r, rows_p), 256)
    nr = r // tr

    def body(me_ref, w_ref, o_ref):
        i = pl.program_id(0)
        o_ref[...] = jnp.zeros_like(o_ref)

        @pl.when(i < nr)
        def _():
            o_ref[:, 0:c] = w_ref[...].astype(o_ref.dtype)

    return pl.pallas_call(
        body, name=name,
        grid_spec=pltpu.PrefetchScalarGridSpec(
            num_scalar_prefetch=1, grid=(rows_p // tr,),
            in_specs=[pl.BlockSpec((tr, c), lambda i, me: (jnp.minimum(i, nr - 1), 0))],
            out_specs=pl.BlockSpec((None, tr, cols_p), lambda i, me: (me[0], i, 0))),
        out_shape=jax.ShapeDtypeStruct((N_DEV, rows_p, cols_p), CDT),
        compiler_params=_params(("arbitrary",)))(me_arr, w)


def all_gather_blocks(slotted):
    n = len(slotted)

    def body(*refs):
        ins, outs = refs[:n], refs[n:2 * n]
        send_sems, recv_sems = refs[2 * n:]
        x, y, c = _mesh_pos()
        me, sibling = (x, y, c), (x, y, 1 - c)
        chips = [(1 - x, y), (x, 1 - y), (1 - x, 1 - y)]

        def copy(t, k, block, to, own=False):
            dst = outs[t].at[_flat(block)]
            return pltpu.make_async_remote_copy(
                src_ref=ins[t].at[_flat(block)] if own else dst, dst_ref=dst,
                send_sem=send_sems.at[t, k], recv_sem=recv_sems.at[t, k],
                device_id=to, device_id_type=pl.DeviceIdType.MESH)

        first, passed = [], []
        for t in range(n):
            row = [copy(t, 0, me, sibling, own=True)]
            row += [copy(t, 1 + j, me, (*chip, c), own=True) for j, chip in enumerate(chips)]
            for cp in row:
                cp.start()
            first += row
        for t in range(n):
            for j, chip in enumerate(chips):
                copy(t, 1 + j, (*chip, c), me).wait_recv()
                cp = copy(t, 4 + j, (*chip, c), sibling)
                cp.start()
                passed.append(cp)
        for t in range(n):
            copy(t, 0, sibling, me).wait_recv()
            for j, chip in enumerate(chips):
                copy(t, 4 + j, (*chip, 1 - c), me).wait_recv()
        for cp in first + passed:
            cp.wait_send()

    return pl.pallas_call(
        body, name="all_gather_weights",
        in_specs=[_HBM] * n, out_specs=[_HBM] * n,
        out_shape=[jax.ShapeDtypeStruct(s.shape, s.dtype) for s in slotted],
        input_output_aliases={t: t for t in range(n)},
        scratch_shapes=[pltpu.SemaphoreType.DMA((n, 7)), pltpu.SemaphoreType.DMA((n, 7))],
        )(*slotted)


def exchange_partials(arrays, after):
    n, na = len(arrays), len(after)

    def body(*refs):
        ins, outs = refs[:n], refs[n + na:2 * n + na]
        send_sems, recv_sems, local_sems = refs[2 * n + na:]
        me = _mesh_pos()
        copies, locals_ = [], []
        for t in range(n):
            cp = pltpu.make_async_copy(ins[t], outs[t].at[_flat(me)], local_sems.at[t])
            cp.start()
            locals_.append(cp)
            for k in range(1, N_DEV):
                peer = _peer(me, k)
                send = pltpu.make_async_remote_copy(
                    src_ref=ins[t], dst_ref=outs[t].at[_flat(me)],
                    send_sem=send_sems.at[t, k - 1], recv_sem=recv_sems.at[t, k - 1],
                    device_id=peer, device_id_type=pl.DeviceIdType.MESH)
                send.start()
                recv = pltpu.make_async_remote_copy(
                    src_ref=ins[t], dst_ref=outs[t].at[_flat(peer)],
                    send_sem=send_sems.at[t, k - 1], recv_sem=recv_sems.at[t, k - 1],
                    device_id=peer, device_id_type=pl.DeviceIdType.MESH)
                copies.append((send, recv))
        for send, recv in copies:
            recv.wait_recv()
        for send, recv in copies:
            send.wait_send()
        for cp in locals_:
            cp.wait()

    return pl.pallas_call(
        body, name="exchange_partials",
        in_specs=[_HBM] * n + [_ANY] * na, out_specs=[_HBM] * n,
        out_shape=[jax.ShapeDtypeStruct((N_DEV,) + a.shape, a.dtype) for a in arrays],
        scratch_shapes=[pltpu.SemaphoreType.DMA((n, 7)), pltpu.SemaphoreType.DMA((n, 7)),
                        pltpu.SemaphoreType.DMA((n,))],
        )(*arrays, *after)


_SEM = pl.BlockSpec(memory_space=pltpu.SEMAPHORE)
_ANY = pl.BlockSpec(memory_space=pl.ANY)
_EFFECT = pltpu.SideEffectType.DATAFLOW_SIDE_EFFECTING


def _peer(me, k):
    return tuple(1 - v if bit else v for v, bit in zip(me, (k >> 2, (k >> 1) & 1, k & 1)))


_CHIP_MASKS = (2, 4, 6)

_EXCHANGE_MODES = {"gather": (7, None), "gather_chips": (4, None), "forward": (3, None),
                   "scatter": (7, 7), "scatter_pair": (4, 4), "scatter_chips": (3, 3)}


def _plan(mode, bufs, n, me):
    per = _EXCHANGE_MODES[mode][0]
    sib = _peer(me, 1)
    plan = []
    for t in range(n):
        src_arr, land_arr = bufs[t], bufs[n + t] if _EXCHANGE_MODES[mode][1] else None
        if mode in ("gather", "gather_chips"):
            masks = range(1, N_DEV) if mode == "gather" else (1,) + _CHIP_MASKS
            rows = [(src_arr.at[_flat(me)], src_arr.at[_flat(me)], _peer(me, k), src_arr.at[_flat(_peer(me, k))])
                    for k in masks]
        elif mode == "forward":
            rows = [(src_arr.at[_flat(_peer(me, k))], src_arr.at[_flat(_peer(me, k))], sib,
                     src_arr.at[_flat(_peer(sib, k))]) for k in _CHIP_MASKS]
        elif mode == "scatter":
            rows = [(src_arr.at[_flat(_peer(me, k))], land_arr.at[k - 1], _peer(me, k), land_arr.at[k - 1])
                    for k in range(1, N_DEV)]
        elif mode == "scatter_pair":
            rows = [(src_arr.at[_flat(_peer(me, q + 1))], land_arr.at[qi], sib, land_arr.at[qi])
                    for qi, q in enumerate((0,) + _CHIP_MASKS)]
        else:
            assert mode == "scatter_chips"
            rows = [(src_arr.at[qi + 1], land_arr.at[qi], _peer(me, q), land_arr.at[qi])
                    for qi, q in enumerate(_CHIP_MASKS)]
        assert len(rows) == per
        plan += [(t * per + s,) + row for s, row in enumerate(rows)]
    return plan


def exchange_start(name, arrays, mode, after):
    n, na = len(arrays), len(after)
    per, slots = _EXCHANGE_MODES[mode]
    bufs = list(arrays)
    if slots:
        bufs += [lax.empty((slots,) + a.shape[1:], a.dtype) for a in arrays]
    nb = len(bufs)

    def body(*refs):
        send_sems, recv_sems = refs[nb + na], refs[nb + na + 1]
        token = refs[-1]
        for s, src, dst, dev, _ in _plan(mode, refs[:nb], n, _mesh_pos()):
            pltpu.make_async_remote_copy(
                src_ref=src, dst_ref=dst, send_sem=send_sems.at[s], recv_sem=recv_sems.at[s],
                device_id=dev, device_id_type=pl.DeviceIdType.MESH).start()
        token[...] = jnp.zeros_like(token)

    out_shape = [pltpu.SemaphoreType.DMA((n * per,)), pltpu.SemaphoreType.DMA((n * per,))]
    out_shape += [pltpu.HBM(a.shape, a.dtype) for a in bufs]
    out_shape.append(jax.ShapeDtypeStruct((8, LANE), F32))
    args = [pltpu.with_memory_space_constraint(a, pltpu.HBM) for a in bufs] + list(after)
    outs = pl.pallas_call(
        body, name=name, out_shape=out_shape,
        in_specs=[_HBM] * nb + [_ANY] * na,
        out_specs=[_SEM, _SEM] + [_HBM] * nb + [pl.BlockSpec(memory_space=pltpu.VMEM)],
        input_output_aliases={i: 2 + i for i in range(nb)},
        compiler_params=pltpu.CompilerParams(has_side_effects=_EFFECT))(*args)
    return outs[0], outs[1], list(outs[2:2 + nb]), outs[-1]


def exchange_wait(name, started, mode, after):
    send_sems, recv_sems, bufs, _ = started
    nb, na = len(bufs), len(after)
    n = nb // 2 if _EXCHANGE_MODES[mode][1] else nb

    def body(*refs):
        send_sems_ref, recv_sems_ref = refs[nb], refs[nb + 1]
        for s, src, _, dev, land in _plan(mode, refs[:nb], n, _mesh_pos()):
            cp = pltpu.make_async_remote_copy(
                src_ref=src, dst_ref=land, send_sem=send_sems_ref.at[s], recv_sem=recv_sems_ref.at[s],
                device_id=dev, device_id_type=pl.DeviceIdType.MESH)
            cp.wait_send()
            cp.wait_recv()

    outs = pl.pallas_call(
        body, name=name, out_shape=[pltpu.HBM(a.shape, a.dtype) for a in bufs],
        in_specs=[_HBM] * nb + [_SEM, _SEM] + [_ANY] * na, out_specs=[_HBM] * nb,
        input_output_aliases={i: i for i in range(nb)},
        compiler_params=pltpu.CompilerParams(has_side_effects=_EFFECT))(
            *bufs, send_sems, recv_sems, *after)
    return list(outs)


def pair_sum(name, own, landed, blocks):
    _, r, c = own.shape
    tr = _row_tile(r, 256)

    def body(idx_ref, o_ref, l_ref, s_ref):
        s_ref[...] = (o_ref[...].astype(F32) + l_ref[...].astype(F32)).astype(s_ref.dtype)

    blk = pl.BlockSpec((None, tr, c), lambda q, i, idx: (q, i, 0))
    return pl.pallas_call(
        body, name=name,
        grid_spec=pltpu.PrefetchScalarGridSpec(
            num_scalar_prefetch=1, grid=(4, r // tr),
            in_specs=[pl.BlockSpec((None, tr, c), lambda q, i, idx: (idx[q], i, 0)), blk],
            out_specs=blk),
        out_shape=jax.ShapeDtypeStruct((4, r, c), own.dtype),
        compiler_params=_params(("parallel", "parallel")))(blocks, own, landed)


def _adamw_math(w, g, m, v):
    m = ADAM_B1 * m + (1.0 - ADAM_B1) * g
    v = ADAM_B2 * v + (1.0 - ADAM_B2) * (g * g)
    m_hat = m / (1.0 - ADAM_B1 ** ADAM_STEP)
    v_hat = v / (1.0 - ADAM_B2 ** ADAM_STEP)
    delta = -ADAM_LR * (m_hat / (jnp.sqrt(v_hat) + ADAM_EPS) + ADAM_WD * w)
    return delta, m, v


def reduce_adamw(name, land, w, m, v, tr, own=None, me_arr=None):
    R, C = w.shape
    S, _, Cp = land.shape

    def body(*refs):
        if own is not None:
            _, own_ref, l_ref, w_ref, m_ref, v_ref, g_ref, d_ref, nm_ref, nv_ref = refs
            g = own_ref[:, 0:C].astype(F32)
            first = 0
        else:
            l_ref, w_ref, m_ref, v_ref, g_ref, d_ref, nm_ref, nv_ref = refs
            g = l_ref[0, :, 0:C].astype(F32)
            first = 1
        for s in range(first, S):
            g = g + l_ref[s, :, 0:C].astype(F32)
        delta, nm, nv = _adamw_math(w_ref[...], g, m_ref[...], v_ref[...])
        g_ref[...] = g
        d_ref[...] = delta
        nm_ref[...] = nm
        nv_ref[...] = nv

    shp = jax.ShapeDtypeStruct((R, C), F32)
    if own is None:
        blk = pl.BlockSpec((tr, C), lambda i: (i, 0))
        return pl.pallas_call(
            body, name=name, grid=(R // tr,),
            in_specs=[pl.BlockSpec((S, tr, Cp), lambda i: (0, i, 0)), blk, blk, blk],
            out_specs=[blk, blk, blk, blk], out_shape=[shp, shp, shp, shp],
            compiler_params=_params(("parallel",)))(land, w, m, v)
    blk = pl.BlockSpec((tr, C), lambda i, me: (i, 0))
    return pl.pallas_call(
        body, name=name,
        grid_spec=pltpu.PrefetchScalarGridSpec(
            num_scalar_prefetch=1, grid=(R // tr,),
            in_specs=[pl.BlockSpec((None, tr, Cp), lambda i, me: (me[0], i, 0)),
                      pl.BlockSpec((S, tr, Cp), lambda i, me: (0, i, 0)), blk, blk, blk],
            out_specs=[blk, blk, blk, blk]),
        out_shape=[shp, shp, shp, shp],
        compiler_params=_params(("parallel",)))(me_arr, own, land, w, m, v)


def _row_tile(r, cap):
    t = min(r, cap)
    while r % t or t % 8:
        t -= 8
    return t


def kernel(x, norm_ffn1_g, ffn1_w_gate, ffn1_w_up, ffn1_w_down, norm_mix_g, w_in, rel_bias, w_out_att, w_out_ret, w_out, norm_ffn2_g, ffn2_w_gate, ffn2_w_up, ffn2_w_down, norm_final_g, loss_target, m_norm_ffn1_g, m_ffn1_w_gate, m_ffn1_w_up, m_ffn1_w_down, m_norm_mix_g, m_w_in, m_rel_bias, m_w_out_att, m_w_out_ret, m_w_out, m_norm_ffn2_g, m_ffn2_w_gate, m_ffn2_w_up, m_ffn2_w_down, m_norm_final_g, v_norm_ffn1_g, v_ffn1_w_gate, v_ffn1_w_up, v_ffn1_w_down, v_norm_mix_g, v_w_in, v_rel_bias, v_w_out_att, v_w_out_ret, v_w_out, v_norm_ffn2_g, v_ffn2_w_gate, v_ffn2_w_up, v_ffn2_w_down, v_norm_final_g):
    T, D = x.shape[1], x.shape[2]
    A = w_out_att.shape[1]
    H = A // HEAD_DIM
    nf = ffn1_w_gate.shape[2]
    nfp = _round_up(nf, LANE)
    nin = w_in.shape[2]
    nd = w_out.shape[1]
    assert nin % LANE == 0 and nd % LANE == 0 and (7 * A) % nd == 0 and T % ATT_TQ == 0
    tm = min(512, T)
    tw = min(2048, T)
    tw_in = min(1024, T)
    tn = min(256, T)
    x0 = x[0]
    tgt = loss_target[0]

    me_arr = (4 * lax.axis_index("x") + 2 * lax.axis_index("y") + lax.axis_index("c")).astype(jnp.int32).reshape(1)

    def slot(tag, w, rows_p=None, cols_p=None):
        return cast_shard("cast_" + tag, w[0], rows_p or w.shape[1], cols_p or w.shape[2], me_arr)

    Wg1, Wu1 = all_gather_blocks([slot("wg1", ffn1_w_gate, cols_p=nfp), slot("wu1", ffn1_w_up, cols_p=nfp)])
    later = [("wd1", [slot("wd1", ffn1_w_down, rows_p=nfp)], True), ("win", [slot("win", w_in)], True),
             ("wout", [slot("woa", w_out_att), slot("wor", w_out_ret), slot("wo", w_out)], False),
             ("wgu2", [slot("wg2", ffn2_w_gate, cols_p=nfp), slot("wu2", ffn2_w_up, cols_p=nfp)], False),
             ("wd2", [slot("wd2", ffn2_w_down, rows_p=nfp)], False)]
    ag_started = {}
    order = Wg1
    for tag, ws, two_level in later:
        mode = "gather_chips" if two_level else "gather"
        ag_started[tag] = (exchange_start("ag_start_" + tag, ws, mode, [order]), mode)
        order = ag_started[tag][0][3]

    def gathered(tag, after):
        started, mode = ag_started[tag]
        got = exchange_wait("ag_wait_" + tag, started, mode, [after])
        if mode == "gather":
            return got
        passing = exchange_start("ag_pass_" + tag, got, "forward", [])
        return exchange_wait("ag_passed_" + tag, passing, "forward", [passing[3]])

    xi, yi, ci = lax.axis_index("x"), lax.axis_index("y"), lax.axis_index("c")
    my_side = jnp.stack([4 * (1 - xi if q & 4 else xi) + 2 * (1 - yi if q & 2 else yi) + ci
                         for q in (0,) + _CHIP_MASKS]).astype(jnp.int32)
    first_block = jnp.zeros((1,), jnp.int32)

    def swiglu(prods, _):
        a, b = prods
        return a, b, a * _sigmoid(a) * b

    def ffn_fwd(tag, xin, g, get_wgu, get_wd, after=()):
        h = rmsnorm_fwd(tag + "_norm", xin, g, tn, after)
        Wg, Wu = get_wgu(h)
        a, b, mid = mm_block(tag + "_up", T, tm, [(h, "full", D, 0)], [(Wg, 0, False), (Wu, 0, False)],
                             [], [("3d", nfp, CDT)] * 3, swiglu)
        Wd, = get_wd(mid)
        xo = mm_reduce_j(tag + "_down", T, tm, [(mid, "3d", nfp, 0, Wd, False)], D, F32, res=xin, scale=0.5,
                         jstep=2)
        return h, a, b, mid, xo, (Wg, Wu, Wd)

    h1, a1, b1, mid1, x1, (_, _, Wd1) = ffn_fwd(
        "ffn1", x0, norm_ffn1_g, lambda h: (Wg1, Wu1), lambda mid: gathered("wd1", mid), after=[order])
    h2 = rmsnorm_fwd("mix_norm", x1, norm_mix_g, tn)
    Win, = gathered("win", h2)
    proj, = mm_block("in_proj", T, tm, [(h2, "full", D, 0)], [(Win, 0, False)], [], [("col", nin, CDT)],
                     lambda p, _: p)
    biasm = attn_bias(rel_bias[0], ATT_TQ)
    att = attn_fwd(proj, biasm, A, ATT_TQ)
    tables = _retention_tables(T, H, RET_BLK)
    retg, ret_raw, states = ret_fwd(proj, tables, A, RET_BLK)
    Woa, Wor, Wo = gathered("wout", retg)
    goff = 7 * A // nd

    def merge(prods, ex):
        ba, br = prods
        ga, gr = ex
        return ba, br, _sigmoid(ga) * ba + _sigmoid(gr) * br

    ba, br, merged = mm_block(
        "branches", T, tm, [(att, "full", A, 0), (retg, "full", A, 0)], [(Woa, 0, False), (Wor, 1, False)],
        [(proj, "col", nd, goff), (proj, "col", nd, goff + N_DEV)], [("col", nd, CDT)] * 3, merge)
    x2 = mm_reduce_j("out_proj", T, tm, [(merged, "col", nd, 0, Wo, False)], D, F32, res=x1, scale=1.0,
                     jstep=N_DEV)
    h3, a2, b2, mid2, x3, (Wg2, Wu2, Wd2) = ffn_fwd(
        "ffn2", x2, norm_ffn2_g, lambda h: gathered("wgu2", h), lambda mid: gathered("wd2", mid))

    dx3, dx3h, dgf, loss_part = loss_head(x3, norm_final_g.reshape(1, D), tgt, tn)

    def swiglu_bwd(prods, ex):
        dm, = prods
        a, b = ex
        sg = _sigmoid(a)
        return dm * b * (sg * (1.0 + a * (1.0 - sg))), dm * (a * sg)

    def ffn_bwd(tag, dxh, h, a, b, mid, Wg, Wu, Wd, two_level=False):
        da, db = mm_block(tag + "_down_bwd", T, tm, [(dxh, "full", D, 0)], [(Wd, 0, True)],
                          [(a, "3d", nfp, 0), (b, "3d", nfp, 0)], [("3d", nfp, CDT)] * 2, swiglu_bwd)
        dWd = mm_reduce_i(tag + "_dwd", T, tw, (mid, "3d", nfp, 0), (dxh, "full", D, 0))
        dWg = mm_reduce_i(tag + "_dwg", T, tw, (h, "full", D, 0), (da, "3d", nfp, 0))
        dWu = mm_reduce_i(tag + "_dwu", T, tw, (h, "full", D, 0), (db, "3d", nfp, 0))
        mode = "scatter_pair" if two_level else "scatter"
        sent = exchange_start("rs_start_" + tag, [dWg, dWu, dWd], mode, [])
        dh = mm_reduce_j(tag + "_up_bwd", T, tm, [(da, "3d", nfp, 0, Wg, True), (db, "3d", nfp, 0, Wu, True)],
                         D, F32, after=[sent[3]], jstep=2)
        if two_level:
            got = exchange_wait("rs_pair_wait_" + tag, sent, mode, [dh])
            sums = [pair_sum("%s_pair_sum_%d" % (tag, t), got[t], got[3 + t], my_side) for t in range(3)]
            sent = exchange_start("rs_chips_start_" + tag, sums, "scatter_chips", [])
        return dh, sent

    dh3, sent_ffn2 = ffn_bwd("ffn2", dx3h, h3, a2, b2, mid2, Wg2, Wu2, Wd2)
    dx2, dx2c, dg2 = rmsnorm_bwd("ffn2_norm_bwd", x2, norm_ffn2_g, dh3, dx3, 1.0, tn)

    def merge_bwd(prods, ex):
        dmg, = prods
        ba_, br_, ga, gr = ex
        sa, sr = _sigmoid(ga), _sigmoid(gr)
        return dmg * sa, dmg * sr, dmg * ba_ * sa * (1.0 - sa), dmg * br_ * sr * (1.0 - sr)

    dba, dbr, dga, dgr = mm_block(
        "out_proj_bwd", T, tm, [(dx2c, "full", D, 0)], [(Wo, 0, True)],
        [(ba, "col", nd, 0), (br, "col", nd, 0), (proj, "col", nd, goff), (proj, "col", nd, goff + N_DEV)],
        [("col", nd, CDT)] * 4, merge_bwd)
    dWo = mm_reduce_i("dwo", T, tw, (merged, "col", nd, 0), (dx2c, "full", D, 0))
    dWoa = mm_reduce_i("dwoa", T, tw, (att, "full", A, 0), (dba, "col", nd, 0))
    dWor = mm_reduce_i("dwor", T, tw, (retg, "full", A, 0), (dbr, "col", nd, 0))
    sent_mix = exchange_start("rs_start_mix", [dWoa, dWor, dWo], "scatter", [])
    datt = mm_reduce_j("att_out_bwd", T, tm, [(dba, "col", nd, 0, Woa, True)], A, CDT, after=[sent_mix[3]],
                       jstep=N_DEV)
    dretg = mm_reduce_j("ret_out_bwd", T, tm, [(dbr, "col", nd, 0, Wor, True)], A, CDT, jstep=N_DEV)
    dq_r, dk_r, dv_r, dg_r = ret_bwd(proj, tables, ret_raw, states, dretg, A, RET_BLK)
    dq_a, dk_a, dv_a, dst = attn_bwd(proj, biasm, datt, A, ATT_TQ)
    dbias = jnp.pad(attn_bias_grad(dst, ATT_TQ), ((0, 0), (0, N_REL_PAD - N_REL)))
    dproj = jnp.concatenate([dq_a, dk_a, dv_a, dq_r, dk_r, dv_r, dg_r, dga, dgr], axis=1)
    dWin = mm_reduce_i("dwin", T, tw_in, (h2, "full", D, 0), (dproj, "col", nin, 0))
    sent_win = exchange_start("rs_start_win", [dWin], "scatter", [])
    dh2 = mm_reduce_j("in_proj_bwd", T, tm, [(dproj, "col", nin, 0, Win, True)], D, F32, after=[sent_win[3]],
                      jstep=2)
    dx1, dx1h, dgm = rmsnorm_bwd("mix_norm_bwd", x1, norm_mix_g, dh2, dx2, 0.5, tn)
    dh1, sent_ffn1 = ffn_bwd("ffn1", dx1h, h1, a1, b1, mid1, Wg1, Wu1, Wd1, two_level=True)
    grad_x, _, dg1 = rmsnorm_bwd("ffn1_norm_bwd", x0, norm_ffn1_g, dh1, dx1, 1.0, tn, after=[sent_ffn1[3]])

    dgains = jnp.concatenate([dg1, dgm, dg2, dgf, jnp.zeros((4, D), F32)], axis=0)

    def upd(name, own, land, w, m, v, own_block=me_arr):
        r = w.shape[1]
        outs = reduce_adamw(name, land, w[0], m[0], v[0], _row_tile(r, 256), own=own, me_arr=own_block)
        return [o[None] for o in outs]

    res = {}
    oWg2, oWu2, oWd2, lWg2, lWu2, lWd2 = exchange_wait("rs_wait_ffn2", sent_ffn2, "scatter", [grad_x])
    res["ffn2_w_gate"] = upd("adamw_wg2", oWg2, lWg2, ffn2_w_gate, m_ffn2_w_gate, v_ffn2_w_gate)
    res["ffn2_w_up"] = upd("adamw_wu2", oWu2, lWu2, ffn2_w_up, m_ffn2_w_up, v_ffn2_w_up)
    res["ffn2_w_down"] = upd("adamw_wd2", oWd2, lWd2, ffn2_w_down, m_ffn2_w_down, v_ffn2_w_down)
    oWoa, oWor, oWo, lWoa, lWor, lWo = exchange_wait("rs_wait_mix", sent_mix, "scatter", [res["ffn2_w_down"][1]])
    res["w_out_att"] = upd("adamw_woa", oWoa, lWoa, w_out_att, m_w_out_att, v_w_out_att)
    res["w_out_ret"] = upd("adamw_wor", oWor, lWor, w_out_ret, m_w_out_ret, v_w_out_ret)
    res["w_out"] = upd("adamw_wo", oWo, lWo, w_out, m_w_out, v_w_out)
    oWin, lWin = exchange_wait("rs_wait_win", sent_win, "scatter", [res["w_out"][1]])
    res["w_in"] = upd("adamw_win", oWin, lWin, w_in, m_w_in, v_w_in)
    lgains, lbias = exchange_partials([dgains, dbias], [res["w_in"][1]])
    oWg1, oWu1, oWd1, lWg1, lWu1, lWd1 = exchange_wait("rs_wait_ffn1", sent_ffn1, "scatter_chips", [lgains])
    res["ffn1_w_gate"] = upd("adamw_wg1", oWg1, lWg1, ffn1_w_gate, m_ffn1_w_gate, v_ffn1_w_gate, first_block)
    res["ffn1_w_up"] = upd("adamw_wu1", oWu1, lWu1, ffn1_w_up, m_ffn1_w_up, v_ffn1_w_up, first_block)
    res["ffn1_w_down"] = upd("adamw_wd1", oWd1, lWd1, ffn1_w_down, m_ffn1_w_down, v_ffn1_w_down, first_block)

    def stack_gains(a, b, c_, d):
        return jnp.concatenate([a, b, c_, d.reshape(1, D), jnp.zeros((4, D), F32)], axis=0)

    gw = stack_gains(norm_ffn1_g, norm_mix_g, norm_ffn2_g, norm_final_g)
    gm = stack_gains(m_norm_ffn1_g, m_norm_mix_g, m_norm_ffn2_g, m_norm_final_g)
    gv = stack_gains(v_norm_ffn1_g, v_norm_mix_g, v_norm_ffn2_g, v_norm_final_g)
    gains = reduce_adamw("adamw_gains", lgains, gw, gm, gv, 8)

    def padb(t):
        return jnp.pad(t[0], ((0, 0), (0, N_REL_PAD - N_REL)))

    bias = [o[:, :N_REL][None] for o in
            reduce_adamw("adamw_bias", lbias, padb(rel_bias), padb(m_rel_bias), padb(v_rel_bias), H)]
    res["norm_ffn1_g"] = [o[0:1] for o in gains]
    res["norm_mix_g"] = [o[1:2] for o in gains]
    res["norm_ffn2_g"] = [o[2:3] for o in gains]
    res["norm_final_g"] = [o[3] for o in gains]
    res["rel_bias"] = bias

    loss = lax.psum(loss_part[0, 0], MESH_AXES)
    names = ["norm_ffn1_g", "ffn1_w_gate", "ffn1_w_up", "ffn1_w_down", "norm_mix_g", "w_in", "rel_bias",
             "w_out_att", "w_out_ret", "w_out", "norm_ffn2_g", "ffn2_w_gate", "ffn2_w_up", "ffn2_w_down",
             "norm_final_g"]
    out = [loss, grad_x[None]]
    for k in range(4):
        out += [res[nm][k] for nm in names]
    return tuple(out)
```

```python
import functools
import math

import jax
import jax.numpy as jnp
import numpy as np
from jax import lax
from jax.experimental import pallas as pl
from jax.experimental.pallas import tpu as pltpu

F32 = jnp.float32
CDT = jnp.bfloat16

N_DEV = 8
CHUNK = 64
N_PREV_CHUNKS = 8
BAND = N_PREV_CHUNKS * CHUNK
HEAD_DIM = 128
MAX_REL_DIST = 128
N_REL = 2 * MAX_REL_DIST + 1
N_REL_PAD = 384
ROPE_BASE = 10000.0
EPS = 1e-6
NEG = -1e30
LANE = 128
ATT_TQ = 256
RET_BLK = 256
RET_HEADS_PER_STEP = 2
VMEM_LIMIT = 48 * 1024 * 1024

ADAM_LR = 0.001
ADAM_B1 = 0.9
ADAM_B2 = 0.999
ADAM_EPS = 1e-08
ADAM_WD = 0.01
ADAM_STEP = 10

MESH_AXES = ("x", "y", "c")
_NT = (((1,), (1,)), ((), ()))
_TN = (((0,), (0,)), ((), ()))


def _round_up(v, m):
    return (v + m - 1) // m * m


def _params(sem=None):
    return pltpu.CompilerParams(dimension_semantics=sem, vmem_limit_bytes=VMEM_LIMIT)


def _sigmoid(v):
    return 0.5 * (jnp.tanh(0.5 * v) + 1.0)


def _bspec(kind, tm, w, off, order, jmap=lambda j: j):
    def wrap(f):
        if order == "ji":
            return lambda j, i: f(i, jmap(j))
        return lambda i, j: f(i, jmap(j))
    if kind == "full":
        return pl.BlockSpec((tm, w), wrap(lambda i, j: (i, 0)))
    if kind == "col":
        return pl.BlockSpec((tm, w), wrap(lambda i, j: (i, j + off)))
    assert kind == "3d"
    return pl.BlockSpec((None, tm, w), wrap(lambda i, j: (j, i, 0)))


def _wspec(w, order, jmap=lambda j: j):
    if order == "ji":
        return pl.BlockSpec((None,) + w.shape[1:], lambda j, i: (jmap(j), 0, 0))
    return pl.BlockSpec((None,) + w.shape[1:], lambda i, j: (jmap(j), 0, 0))


def _width(arr, kind, w):
    return arr.shape[-1] if kind in ("full", "3d") else w


def mm_block(name, T, tm, lhs, wts, extras, outs, epilogue, row_split=1):
    nl, nw, ne = len(lhs), len(wts), len(extras)
    ni = T // tm
    tr_ = tm // row_split

    def body(*refs):
        l = refs[:nl]
        w = refs[nl:nl + nw]
        e = refs[nl + nw:nl + nw + ne]
        o = refs[nl + nw + ne:]
        for g in range(row_split):
            rows = pl.ds(g * tr_, tr_)
            prods = []
            for k, (_, li, tr) in enumerate(wts):
                a = l[li][rows, :]
                if tr:
                    prods.append(lax.dot_general(a, w[k][...], _NT, preferred_element_type=F32))
                else:
                    prods.append(jnp.dot(a, w[k][...], preferred_element_type=F32))
            res = epilogue(prods, [r[rows, :].astype(F32) for r in e])
            for r, val in zip(o, res):
                r[rows, :] = val.astype(r.dtype)

    in_specs = [_bspec(k, tm, _width(a, k, w), off, "ji") for (a, k, w, off) in lhs]
    in_specs += [_wspec(w, "ji") for (w, _, _) in wts]
    in_specs += [_bspec(k, tm, _width(a, k, w), off, "ji") for (a, k, w, off) in extras]
    out_specs, out_shape = [], []
    for (kind, w, dt) in outs:
        out_specs.append(_bspec(kind, tm, w, 0, "ji"))
        if kind == "3d":
            out_shape.append(jax.ShapeDtypeStruct((N_DEV, T, w), dt))
        else:
            out_shape.append(jax.ShapeDtypeStruct((T, N_DEV * w), dt))
    args = [a for (a, _, _, _) in lhs] + [w for (w, _, _) in wts] + [a for (a, _, _, _) in extras]
    return pl.pallas_call(
        body, name=name, grid=(N_DEV, ni), in_specs=in_specs, out_specs=out_specs,
        out_shape=out_shape, compiler_params=_params(("parallel", "parallel")))(*args)


def mm_reduce_j(name, T, tm, pairs, out_w, out_dtype, res=None, scale=1.0, after=(), jstep=1):
    terms = [(p, u) for u in range(jstep) for p in pairs]
    nt = len(terms)
    nj = N_DEV // jstep
    ni = T // tm

    def body(*refs):
        xs = refs[:nt]
        ws = refs[nt:2 * nt]
        rest = refs[2 * nt:len(refs) - 2 - len(after)] + refs[len(refs) - 2:]
        if res is not None:
            res_ref, o_ref, acc = rest
        else:
            o_ref, acc = rest
        j = pl.program_id(1)

        @pl.when(j == 0)
        def _():
            acc[...] = jnp.zeros_like(acc)

        tot = None
        for k, (p, _) in enumerate(terms):
            if p[5]:
                d = lax.dot_general(xs[k][...], ws[k][...], _NT, preferred_element_type=F32)
            else:
                d = jnp.dot(xs[k][...], ws[k][...], preferred_element_type=F32)
            tot = d if tot is None else tot + d
        acc[...] += tot

        @pl.when(j == nj - 1)
        def _():
            if res is not None:
                o_ref[...] = (res_ref[...] + scale * acc[...]).astype(o_ref.dtype)
            else:
                o_ref[...] = acc[...].astype(o_ref.dtype)

    def jmap(u):
        return lambda j: j * jstep + u

    in_specs = [_bspec(p[1], tm, _width(p[0], p[1], p[2]), p[3], "ij", jmap(u)) for (p, u) in terms]
    in_specs += [_wspec(p[4], "ij", jmap(u)) for (p, u) in terms]
    args = [p[0] for (p, _) in terms] + [p[4] for (p, _) in terms]
    if res is not None:
        in_specs.append(pl.BlockSpec((tm, out_w), lambda i, j: (i, 0)))
        args.append(res)
    in_specs += [_ANY] * len(after)
    args += list(after)
    return pl.pallas_call(
        body, name=name, grid=(ni, nj), in_specs=in_specs,
        out_specs=pl.BlockSpec((tm, out_w), lambda i, j: (i, 0)),
        out_shape=jax.ShapeDtypeStruct((T, out_w), out_dtype),
        scratch_shapes=[pltpu.VMEM((tm, out_w), F32)],
        compiler_params=_params(("parallel", "arbitrary")))(*args)


def mm_reduce_i(name, T, tm, a, b):
    ni = T // tm
    rows = _width(a[0], a[1], a[2])
    cols = _width(b[0], b[1], b[2])

    def body(a_ref, b_ref, o_ref, acc):
        i = pl.program_id(1)

        @pl.when(i == 0)
        def _():
            acc[...] = jnp.zeros_like(acc)

        acc[...] += lax.dot_general(a_ref[...], b_ref[...], _TN, preferred_element_type=F32)

        @pl.when(i == ni - 1)
        def _():
            o_ref[...] = acc[...].astype(o_ref.dtype)

    return pl.pallas_call(
        body, name=name, grid=(N_DEV, ni),
        in_specs=[_bspec(a[1], tm, rows, a[3], "ji"), _bspec(b[1], tm, cols, b[3], "ji")],
        out_specs=pl.BlockSpec((None, rows, cols), lambda j, i: (j, 0, 0)),
        out_shape=jax.ShapeDtypeStruct((N_DEV, rows, cols), CDT),
        scratch_shapes=[pltpu.VMEM((rows, cols), F32)],
        compiler_params=_params(("parallel", "arbitrary")))(a[0], b[0])


def _rms_bwd_math(xv, g, dy):
    r = lax.rsqrt(jnp.mean(xv * xv, axis=-1, keepdims=True) + EPS)
    xn = xv * r
    dxn = dy * g
    dx = r * (dxn - xn * jnp.mean(dxn * xn, axis=-1, keepdims=True))
    dg = jnp.sum(dy * xn, axis=0, keepdims=True)
    return dx, dg


def rmsnorm_fwd(name, x, g, tm, after=()):
    T, D = x.shape

    def body(x_ref, g_ref, *rest):
        o_ref = rest[-1]
        xv = x_ref[...]
        r = lax.rsqrt(jnp.mean(xv * xv, axis=-1, keepdims=True) + EPS)
        o_ref[...] = (xv * r * g_ref[...]).astype(o_ref.dtype)

    return pl.pallas_call(
        body, name=name, grid=(T // tm,),
        in_specs=[pl.BlockSpec((tm, D), lambda i: (i, 0)), pl.BlockSpec((1, D), lambda i: (0, 0))]
        + [_ANY] * len(after),
        out_specs=pl.BlockSpec((tm, D), lambda i: (i, 0)),
        out_shape=jax.ShapeDtypeStruct((T, D), CDT),
        compiler_params=_params(("parallel",)))(x, g, *after)


def rmsnorm_bwd(name, x, g, dh, dres, cscale, tm, after=()):
    T, D = x.shape

    def body(x_ref, g_ref, dh_ref, dres_ref, *rest):
        dx_ref, dxc_ref, dg_ref = rest[len(after):]
        i = pl.program_id(0)
        dx, dg = _rms_bwd_math(x_ref[...], g_ref[...], dh_ref[...])
        dx = dres_ref[...] + dx
        dx_ref[...] = dx
        dxc_ref[...] = (cscale * dx).astype(dxc_ref.dtype)

        @pl.when(i == 0)
        def _():
            dg_ref[...] = jnp.zeros_like(dg_ref)

        dg_ref[...] += dg

    row = pl.BlockSpec((tm, D), lambda i: (i, 0))
    vec = pl.BlockSpec((1, D), lambda i: (0, 0))
    return pl.pallas_call(
        body, name=name, grid=(T // tm,), in_specs=[row, vec, row, row] + [_ANY] * len(after),
        out_specs=[row, row, vec],
        out_shape=[jax.ShapeDtypeStruct((T, D), F32), jax.ShapeDtypeStruct((T, D), CDT),
                   jax.ShapeDtypeStruct((1, D), F32)],
        compiler_params=_params(("arbitrary",)))(x, g, dh, dres, *after)


def loss_head(x, g, tgt, tm):
    T, D = x.shape

    def body(x_ref, g_ref, t_ref, dx_ref, dxc_ref, dg_ref, loss_ref):
        i = pl.program_id(0)
        xv = x_ref[...]
        gv = g_ref[...]
        r = lax.rsqrt(jnp.mean(xv * xv, axis=-1, keepdims=True) + EPS)
        err = xv * r * gv - t_ref[...]
        part = jnp.sum(jnp.mean(err * err, axis=-1, keepdims=True), axis=0, keepdims=True)
        dx, dg = _rms_bwd_math(xv, gv, err / D)
        dx_ref[...] = dx
        dxc_ref[...] = (0.5 * dx).astype(dxc_ref.dtype)

        @pl.when(i == 0)
        def _():
            dg_ref[...] = jnp.zeros_like(dg_ref)
            loss_ref[...] = jnp.zeros_like(loss_ref)

        dg_ref[...] += dg
        loss_ref[...] += jnp.broadcast_to(0.5 * part, loss_ref.shape)

    row = pl.BlockSpec((tm, D), lambda i: (i, 0))
    vec = pl.BlockSpec((1, D), lambda i: (0, 0))
    return pl.pallas_call(
        body, name="loss_head", grid=(T // tm,), in_specs=[row, vec, row],
        out_specs=[row, row, vec, pl.BlockSpec((1, LANE), lambda i: (0, 0))],
        out_shape=[jax.ShapeDtypeStruct((T, D), F32), jax.ShapeDtypeStruct((T, D), CDT),
                   jax.ShapeDtypeStruct((1, D), F32), jax.ShapeDtypeStruct((1, LANE), F32)],
        compiler_params=_params(("arbitrary",)))(x, g, tgt)


def _skew_rows(z, left):
    tq, kw = z.shape
    row = lax.broadcasted_iota(jnp.int32, (tq, kw), 0)
    s = 1
    while s < tq:
        z = jnp.where((row & s) != 0, pltpu.roll(z, kw - s if left else s, 1), z)
        s *= 2
    return z


REL_HI = BAND + MAX_REL_DIST
REL_LO = BAND - MAX_REL_DIST


def attn_bias(rel_bias, tq):
    H = rel_bias.shape[0]
    kw = BAND + tq
    by_skew = jnp.concatenate(
        [jnp.broadcast_to(rel_bias[:, N_REL - 1:], (H, REL_LO)), rel_bias[:, ::-1],
         jnp.broadcast_to(rel_bias[:, :1], (H, kw - REL_HI - 1))], axis=1).reshape(H, 1, kw)

    def body(t_ref, o_ref):
        t = t_ref[...]
        qi = lax.broadcasted_iota(jnp.int32, (tq, kw), 0)
        kj = lax.broadcasted_iota(jnp.int32, (tq, kw), 1)
        b = _skew_rows(jnp.broadcast_to(t, (tq, kw)), left=False)
        b = jnp.where(kj < qi, t[:, 0:1], b)
        qc = qi // CHUNK
        kc = kj // CHUNK - N_PREV_CHUNKS
        valid = (kc <= qc) & (kc >= qc - N_PREV_CHUNKS)
        o_ref[...] = jnp.where(valid, b, NEG)

    return pl.pallas_call(
        body, name="attn_bias", grid=(H,),
        in_specs=[pl.BlockSpec((None, 1, kw), lambda h: (h, 0, 0))],
        out_specs=pl.BlockSpec((None, tq, kw), lambda h: (h, 0, 0)),
        out_shape=jax.ShapeDtypeStruct((H, tq, kw), F32),
        compiler_params=_params(("parallel",)))(by_skew)


def attn_bias_grad(dst, tq):
    H = dst.shape[0]
    kw = BAND + tq

    def body(d_ref, o_ref):
        z = _skew_rows(d_ref[...], left=True)
        qi = lax.broadcasted_iota(jnp.int32, (tq, kw), 0)
        kj = lax.broadcasted_iota(jnp.int32, (tq, kw), 1)
        wrapped = kj + qi >= kw
        c = jnp.sum(jnp.where(wrapped, 0.0, z), axis=0, keepdims=True)
        cw = jnp.sum(jnp.sum(jnp.where(wrapped, z, 0.0), axis=0, keepdims=True), axis=1, keepdims=True)
        lane = lax.broadcasted_iota(jnp.int32, (1, kw), 1)
        ahead = jnp.sum(jnp.where(lane >= REL_HI, c, 0.0), axis=1, keepdims=True)
        behind = jnp.sum(jnp.where(lane <= REL_LO, c, 0.0), axis=1, keepdims=True) + cw
        o_ref[...] = jnp.where(lane == REL_HI, ahead, jnp.where(lane == REL_LO, behind, c))

    by_skew = pl.pallas_call(
        body, name="attn_bias_grad", grid=(H,),
        in_specs=[pl.BlockSpec((None, tq, kw), lambda h: (h, 0, 0))],
        out_specs=pl.BlockSpec((None, 1, kw), lambda h: (h, 0, 0)),
        out_shape=jax.ShapeDtypeStruct((H, 1, kw), F32),
        compiler_params=_params(("parallel",)))(dst)
    return by_skew[:, 0, REL_LO:REL_HI + 1][:, ::-1]


def _attn_scores(q, kpad, bm_ref, start, kw):
    k = kpad[pl.ds(start, kw), :]
    s = lax.dot_general(q, k, _NT, preferred_element_type=F32) * (HEAD_DIM ** -0.5) + bm_ref[...]
    col = lax.broadcasted_iota(jnp.int32, s.shape, 1)
    s = jnp.where(col < BAND - start, NEG, s)
    m = jnp.max(s, axis=-1, keepdims=True)
    p = jnp.exp(s - m)
    return p / jnp.sum(p, axis=-1, keepdims=True), k


def _fill_padded(pad_ref, src_ref, T):
    pad_ref[pl.ds(0, BAND), :] = jnp.zeros((BAND, HEAD_DIM), pad_ref.dtype)
    pad_ref[pl.ds(BAND, T), :] = src_ref[...]


def attn_fwd(proj, biasm, A, tq):
    T = proj.shape[0]
    H = A // HEAD_DIM
    kw = BAND + tq

    def body(q_ref, k_ref, v_ref, bm_ref, o_ref, kpad, vpad):
        qi = pl.program_id(1)

        @pl.when(qi == 0)
        def _():
            _fill_padded(kpad, k_ref, T)
            _fill_padded(vpad, v_ref, T)

        start = pl.multiple_of(qi * tq, tq)
        p, _ = _attn_scores(q_ref[...], kpad, bm_ref, start, kw)
        v = vpad[pl.ds(start, kw), :]
        o_ref[...] = jnp.dot(p.astype(CDT), v, preferred_element_type=F32).astype(o_ref.dtype)

    return pl.pallas_call(
        body, name="attn_fwd", grid=(H, T // tq),
        in_specs=[pl.BlockSpec((tq, HEAD_DIM), lambda h, i: (i, h)),
                  pl.BlockSpec((T, HEAD_DIM), lambda h, i: (0, H + h)),
                  pl.BlockSpec((T, HEAD_DIM), lambda h, i: (0, 2 * H + h)),
                  pl.BlockSpec((None, tq, kw), lambda h, i: (h, 0, 0))],
        out_specs=pl.BlockSpec((tq, HEAD_DIM), lambda h, i: (i, h)),
        out_shape=jax.ShapeDtypeStruct((T, A), CDT),
        scratch_shapes=[pltpu.VMEM((BAND + T, HEAD_DIM), CDT), pltpu.VMEM((BAND + T, HEAD_DIM), CDT)],
        compiler_params=_params(("parallel", "arbitrary")))(proj, proj, proj, biasm)


def attn_bwd(proj, biasm, datt, A, tq):
    T = proj.shape[0]
    H = A // HEAD_DIM
    kw = BAND + tq
    nq = T // tq
    scale = HEAD_DIM ** -0.5

    def body(q_ref, k_ref, v_ref, bm_ref, do_ref, dq_ref, dk_ref, dv_ref, dst_ref,
             kpad, vpad, dkacc, dvacc):
        qi = pl.program_id(1)

        @pl.when(qi == 0)
        def _():
            _fill_padded(kpad, k_ref, T)
            _fill_padded(vpad, v_ref, T)
            dkacc[...] = jnp.zeros_like(dkacc)
            dvacc[...] = jnp.zeros_like(dvacc)
            dst_ref[...] = jnp.zeros_like(dst_ref)

        start = pl.multiple_of(qi * tq, tq)
        q = q_ref[...]
        p, k = _attn_scores(q, kpad, bm_ref, start, kw)
        v = vpad[pl.ds(start, kw), :]
        do = do_ref[...]
        dp = lax.dot_general(do, v, _NT, preferred_element_type=F32)
        ds = p * (dp - jnp.sum(dp * p, axis=-1, keepdims=True))
        dst_ref[...] += ds
        dsb = ds.astype(CDT)
        dq_ref[...] = (jnp.dot(dsb, k, preferred_element_type=F32) * scale).astype(dq_ref.dtype)
        dkacc[pl.ds(start, kw), :] += lax.dot_general(dsb, q, _TN, preferred_element_type=F32) * scale
        dvacc[pl.ds(start, kw), :] += lax.dot_general(p.astype(CDT), do, _TN, preferred_element_type=F32)

        @pl.when(qi == nq - 1)
        def _():
            dk_ref[...] = dkacc[pl.ds(BAND, T), :].astype(dk_ref.dtype)
            dv_ref[...] = dvacc[pl.ds(BAND, T), :].astype(dv_ref.dtype)

    blk = pl.BlockSpec((tq, HEAD_DIM), lambda h, i: (i, h))
    col = pl.BlockSpec((T, HEAD_DIM), lambda h, i: (0, h))
    bias = pl.BlockSpec((None, tq, kw), lambda h, i: (h, 0, 0))
    return pl.pallas_call(
        body, name="attn_bwd", grid=(H, nq),
        in_specs=[blk,
                  pl.BlockSpec((T, HEAD_DIM), lambda h, i: (0, H + h)),
                  pl.BlockSpec((T, HEAD_DIM), lambda h, i: (0, 2 * H + h)),
                  bias, blk],
        out_specs=[blk, col, col, bias],
        out_shape=[jax.ShapeDtypeStruct((T, A), CDT), jax.ShapeDtypeStruct((T, A), CDT),
                   jax.ShapeDtypeStruct((T, A), CDT), jax.ShapeDtypeStruct((H, tq, kw), F32)],
        scratch_shapes=[pltpu.VMEM((BAND + T, HEAD_DIM), CDT), pltpu.VMEM((BAND + T, HEAD_DIM), CDT),
                        pltpu.VMEM((BAND + T, HEAD_DIM), F32), pltpu.VMEM((BAND + T, HEAD_DIM), F32)],
        compiler_params=_params(("parallel", "arbitrary")))(proj, proj, proj, biasm, datt)


def _retention_tables(T, H, blk):
    half = HEAD_DIM // 2
    inv = 1.0 / (ROPE_BASE ** (jnp.arange(0, HEAD_DIM, 2, dtype=F32) / HEAD_DIM))
    ang = jnp.arange(T, dtype=F32)[:, None] * inv[None, :]
    cos, sin = jnp.cos(ang), jnp.sin(ang)
    rc = jnp.concatenate([cos, cos], axis=1)
    rs = jnp.concatenate([-sin, sin], axis=1)
    assert rc.shape == (T, 2 * half)
    log_g = jnp.log(1.0 - 2.0 ** (-5.0 - jnp.arange(H, dtype=F32)))[:, None, None]
    idx = jnp.arange(blk, dtype=F32)
    n, m = idx[:, None], idx[None, :]
    same = (n // CHUNK) == (m // CHUNK)
    earlier = (m // CHUNK) < (n // CHUNK)
    dist = jnp.where(same, jnp.abs(n - m), n - m)[None]
    dmat = jnp.where((same | earlier)[None], jnp.exp(log_g * dist), 0.0)
    ones = jnp.ones((1, 1, HEAD_DIM), F32)
    qd = jnp.exp(log_g * (idx[None, :, None] + 1.0)) * ones
    kd = jnp.exp(log_g * (blk - 1.0 - idx[None, :, None])) * ones
    cd = jnp.exp(log_g * blk) * jnp.ones((1, 8, HEAD_DIM), F32)
    return rc, rs, dmat, qd, kd, cd


def _rot(v, rc, rs):
    return v * rc + pltpu.roll(v, HEAD_DIM // 2, 1) * rs


def _rot_bwd(dv, rc, rs):
    return dv * rc + pltpu.roll(dv * rs, HEAD_DIM // 2, 1)


def ret_fwd(proj, tables, A, blk):
    T = proj.shape[0]
    H = A // HEAD_DIM
    nb = T // blk
    hp = RET_HEADS_PER_STEP
    rc, rs, dmat, qd, kd, cd = tables
    scale = HEAD_DIM ** -0.5

    def body(q_ref, k_ref, v_ref, g_ref, rc_ref, rs_ref, d_ref, qd_ref, kd_ref, cd_ref,
             y_ref, o_ref, st_ref, state):
        b = pl.program_id(1)

        @pl.when(b == 0)
        def _():
            state[...] = jnp.zeros_like(state)

        c, s = rc_ref[...], rs_ref[...]
        for u in range(hp):
            cols = pl.ds(u * HEAD_DIM, HEAD_DIM)
            qs = (_rot(q_ref[:, cols].astype(F32), c, s) * scale).astype(CDT)
            kr = _rot(k_ref[:, cols].astype(F32), c, s)
            v = v_ref[:, cols]
            sb = state[u].astype(CDT)
            a = lax.dot_general(qs, kr.astype(CDT), _NT, preferred_element_type=F32) * d_ref[u]
            o = jnp.dot(a.astype(CDT), v, preferred_element_type=F32)
            o = o + jnp.dot(qs, sb, preferred_element_type=F32) * qd_ref[u]
            st_ref[u] = sb
            state[u] = state[u] * cd_ref[u, 0:1, :] + lax.dot_general(
                (kr * kd_ref[u]).astype(CDT), v, _TN, preferred_element_type=F32)
            o_ref[:, cols] = o
            on = o * lax.rsqrt(jnp.mean(o * o, axis=-1, keepdims=True) + EPS)
            g = g_ref[:, cols].astype(F32)
            y_ref[:, cols] = (g * _sigmoid(g) * on).astype(y_ref.dtype)

    w = hp * HEAD_DIM

    def pj(off):
        return pl.BlockSpec((blk, w), lambda h, i: (i, off * H // hp + h))

    tok = pl.BlockSpec((blk, HEAD_DIM), lambda h, i: (i, 0))
    out = pl.BlockSpec((blk, w), lambda h, i: (i, h))

    def per_head(r, c):
        return pl.BlockSpec((hp, r, c), lambda h, i: (h, 0, 0))

    return pl.pallas_call(
        body, name="ret_fwd", grid=(H // hp, nb),
        in_specs=[pj(3), pj(4), pj(5), pj(6), tok, tok, per_head(blk, blk),
                  per_head(blk, HEAD_DIM), per_head(blk, HEAD_DIM), per_head(8, HEAD_DIM)],
        out_specs=[out, out, pl.BlockSpec((hp, None, HEAD_DIM, HEAD_DIM), lambda h, i: (h, i, 0, 0))],
        out_shape=[jax.ShapeDtypeStruct((T, A), CDT), jax.ShapeDtypeStruct((T, A), F32),
                   jax.ShapeDtypeStruct((H, nb, HEAD_DIM, HEAD_DIM), CDT)],
        scratch_shapes=[pltpu.VMEM((hp, HEAD_DIM, HEAD_DIM), F32)],
        compiler_params=_params(("parallel", "arbitrary")))(
            proj, proj, proj, proj, rc, rs, dmat, qd, kd, cd)


def ret_bwd(proj, tables, o_raw, states, dy, A, blk):
    T = proj.shape[0]
    H = A // HEAD_DIM
    nb = T // blk
    hp = RET_HEADS_PER_STEP
    rc, rs, dmat, qd, kd, cd = tables
    scale = HEAD_DIM ** -0.5

    def body(q_ref, k_ref, v_ref, g_ref, rc_ref, rs_ref, d_ref, qd_ref, kd_ref, cd_ref,
             o_ref, st_ref, dy_ref, dq_ref, dk_ref, dv_ref, dg_ref, dstate):
        b = pl.program_id(1)

        @pl.when(b == 0)
        def _():
            dstate[...] = jnp.zeros_like(dstate)

        c, s = rc_ref[...], rs_ref[...]
        for u in range(hp):
            cols = pl.ds(u * HEAD_DIM, HEAD_DIM)
            qs = (_rot(q_ref[:, cols].astype(F32), c, s) * scale).astype(CDT)
            kr = _rot(k_ref[:, cols].astype(F32), c, s)
            krb = kr.astype(CDT)
            kdb = (kr * kd_ref[u]).astype(CDT)
            v = v_ref[:, cols]
            dmat_v = d_ref[u]
            a = lax.dot_general(qs, krb, _NT, preferred_element_type=F32) * dmat_v

            o = o_ref[:, cols]
            r = lax.rsqrt(jnp.mean(o * o, axis=-1, keepdims=True) + EPS)
            on = o * r
            g = g_ref[:, cols].astype(F32)
            sg = _sigmoid(g)
            dyv = dy_ref[:, cols].astype(F32)
            dg_ref[:, cols] = (dyv * on * (sg * (1.0 + g * (1.0 - sg)))).astype(dg_ref.dtype)
            don = dyv * (g * sg)
            do = r * (don - on * jnp.mean(don * on, axis=-1, keepdims=True))
            dob = do.astype(CDT)
            doq = (do * qd_ref[u]).astype(CDT)
            dsb = dstate[u].astype(CDT)

            dv = lax.dot_general(a.astype(CDT), dob, _TN, preferred_element_type=F32)
            dv = dv + jnp.dot(kdb, dsb, preferred_element_type=F32)
            dv_ref[:, cols] = dv.astype(dv_ref.dtype)
            dpb = (lax.dot_general(dob, v, _NT, preferred_element_type=F32) * dmat_v).astype(CDT)
            dqs = jnp.dot(dpb, krb, preferred_element_type=F32)
            dqs = dqs + lax.dot_general(doq, st_ref[u], _NT, preferred_element_type=F32)
            dkr = lax.dot_general(dpb, qs, _TN, preferred_element_type=F32)
            dkr = dkr + lax.dot_general(v, dsb, _NT, preferred_element_type=F32) * kd_ref[u]
            dstate[u] = dstate[u] * cd_ref[u, 0:1, :] + lax.dot_general(
                qs, doq, _TN, preferred_element_type=F32)
            dq_ref[:, cols] = _rot_bwd(dqs * scale, c, s).astype(dq_ref.dtype)
            dk_ref[:, cols] = _rot_bwd(dkr, c, s).astype(dk_ref.dtype)

    w = hp * HEAD_DIM

    def pj(off):
        return pl.BlockSpec((blk, w), lambda h, i: (nb - 1 - i, off * H // hp + h))

    tok = pl.BlockSpec((blk, HEAD_DIM), lambda h, i: (nb - 1 - i, 0))
    out = pl.BlockSpec((blk, w), lambda h, i: (nb - 1 - i, h))

    def per_head(r, c):
        return pl.BlockSpec((hp, r, c), lambda h, i: (h, 0, 0))

    shp = jax.ShapeDtypeStruct((T, A), CDT)
    return pl.pallas_call(
        body, name="ret_bwd", grid=(H // hp, nb),
        in_specs=[pj(3), pj(4), pj(5), pj(6), tok, tok, per_head(blk, blk),
                  per_head(blk, HEAD_DIM), per_head(blk, HEAD_DIM), per_head(8, HEAD_DIM),
                  out, pl.BlockSpec((hp, None, HEAD_DIM, HEAD_DIM), lambda h, i: (h, nb - 1 - i, 0, 0)),
                  out],
        out_specs=[out, out, out, out], out_shape=[shp, shp, shp, shp],
        scratch_shapes=[pltpu.VMEM((hp, HEAD_DIM, HEAD_DIM), F32)],
        compiler_params=_params(("parallel", "arbitrary")))(
            proj, proj, proj, proj, rc, rs, dmat, qd, kd, cd, o_raw, states, dy)


def _mesh_pos():
    return lax.axis_index("x"), lax.axis_index("y"), lax.axis_index("c")


def _flat(pos):
    return 4 * pos[0] + 2 * pos[1] + pos[2]


_HBM = pl.BlockSpec(memory_space=pltpu.HBM)


def cast_shard(name, w, rows_p, cols_p, me_arr):
    r, c = w.shape
    tr = _row_tile(math.gcd(r, rows_p), 256)
    nr = r // tr

    def body(me_ref, w_ref, o_ref):
        i = pl.program_id(0)
        o_ref[...] = jnp.zeros_like(o_ref)

        @pl.when(i < nr)
        def _():
            o_ref[:, 0:c] = w_ref[...].astype(o_ref.dtype)

    return pl.pallas_call(
        body, name=name,
        grid_spec=pltpu.PrefetchScalarGridSpec(
            num_scalar_prefetch=1, grid=(rows_p // tr,),
            in_specs=[pl.BlockSpec((tr, c), lambda i, me: (jnp.minimum(i, nr - 1), 0))],
            out_specs=pl.BlockSpec((None, tr, cols_p), lambda i, me: (me[0], i, 0))),
        out_shape=jax.ShapeDtypeStruct((N_DEV, rows_p, cols_p), CDT),
        compiler_params=_params(("arbitrary",)))(me_arr, w)


def all_gather_blocks(slotted):
    n = len(slotted)

    def body(*refs):
        ins, outs = refs[:n], refs[n:2 * n]
        send_sems, recv_sems = refs[2 * n:]
        x, y, c = _mesh_pos()
        me, sibling = (x, y, c), (x, y, 1 - c)
        chips = [(1 - x, y), (x, 1 - y), (1 - x, 1 - y)]

        def copy(t, k, block, to, own=False):
            dst = outs[t].at[_flat(block)]
            return pltpu.make_async_remote_copy(
                src_ref=ins[t].at[_flat(block)] if own else dst, dst_ref=dst,
                send_sem=send_sems.at[t, k], recv_sem=recv_sems.at[t, k],
                device_id=to, device_id_type=pl.DeviceIdType.MESH)

        first, passed = [], []
        for t in range(n):
            row = [copy(t, 0, me, sibling, own=True)]
            row += [copy(t, 1 + j, me, (*chip, c), own=True) for j, chip in enumerate(chips)]
            for cp in row:
                cp.start()
            first += row
        for t in range(n):
            for j, chip in enumerate(chips):
                copy(t, 1 + j, (*chip, c), me).wait_recv()
                cp = copy(t, 4 + j, (*chip, c), sibling)
                cp.start()
                passed.append(cp)
        for t in range(n):
            copy(t, 0, sibling, me).wait_recv()
            for j, chip in enumerate(chips):
                copy(t, 4 + j, (*chip, 1 - c), me).wait_recv()
        for cp in first + passed:
            cp.wait_send()

    return pl.pallas_call(
        body, name="all_gather_weights",
        in_specs=[_HBM] * n, out_specs=[_HBM] * n,
        out_shape=[jax.ShapeDtypeStruct(s.shape, s.dtype) for s in slotted],
        input_output_aliases={t: t for t in range(n)},
        scratch_shapes=[pltpu.SemaphoreType.DMA((n, 7)), pltpu.SemaphoreType.DMA((n, 7))],
        )(*slotted)


def exchange_partials(arrays, after):
    n, na = len(arrays), len(after)

    def body(*refs):
        ins, outs = refs[:n], refs[n + na:2 * n + na]
        send_sems, recv_sems, local_sems = refs[2 * n + na:]
        me = _mesh_pos()
        copies, locals_ = [], []
        for t in range(n):
            cp = pltpu.make_async_copy(ins[t], outs[t].at[_flat(me)], local_sems.at[t])
            cp.start()
            locals_.append(cp)
            for k in range(1, N_DEV):
                peer = _peer(me, k)
                send = pltpu.make_async_remote_copy(
                    src_ref=ins[t], dst_ref=outs[t].at[_flat(me)],
                    send_sem=send_sems.at[t, k - 1], recv_sem=recv_sems.at[t, k - 1],
                    device_id=peer, device_id_type=pl.DeviceIdType.MESH)
                send.start()
                recv = pltpu.make_async_remote_copy(
                    src_ref=ins[t], dst_ref=outs[t].at[_flat(peer)],
                    send_sem=send_sems.at[t, k - 1], recv_sem=recv_sems.at[t, k - 1],
                    device_id=peer, device_id_type=pl.DeviceIdType.MESH)
                copies.append((send, recv))
        for send, recv in copies:
            recv.wait_recv()
        for send, recv in copies:
            send.wait_send()
        for cp in locals_:
            cp.wait()

    return pl.pallas_call(
        body, name="exchange_partials",
        in_specs=[_HBM] * n + [_ANY] * na, out_specs=[_HBM] * n,
        out_shape=[jax.ShapeDtypeStruct((N_DEV,) + a.shape, a.dtype) for a in arrays],
        scratch_shapes=[pltpu.SemaphoreType.DMA((n, 7)), pltpu.SemaphoreType.DMA((n, 7)),
                        pltpu.SemaphoreType.DMA((n,))],
        )(*arrays, *after)


_SEM = pl.BlockSpec(memory_space=pltpu.SEMAPHORE)
_ANY = pl.BlockSpec(memory_space=pl.ANY)
_EFFECT = pltpu.SideEffectType.DATAFLOW_SIDE_EFFECTING


def _peer(me, k):
    return tuple(1 - v if bit else v for v, bit in zip(me, (k >> 2, (k >> 1) & 1, k & 1)))


_CHIP_MASKS = (2, 4, 6)

_EXCHANGE_MODES = {"gather": (7, None), "gather_chips": (4, None), "forward": (3, None),
                   "scatter": (7, 7), "scatter_pair": (4, 4), "scatter_chips": (3, 3)}


def _plan(mode, bufs, n, me):
    per = _EXCHANGE_MODES[mode][0]
    sib = _peer(me, 1)
    plan = []
    for t in range(n):
        src_arr, land_arr = bufs[t], bufs[n + t] if _EXCHANGE_MODES[mode][1] else None
        if mode in ("gather", "gather_chips"):
            masks = range(1, N_DEV) if mode == "gather" else (1,) + _CHIP_MASKS
            rows = [(src_arr.at[_flat(me)], src_arr.at[_flat(me)], _peer(me, k), src_arr.at[_flat(_peer(me, k))])
                    for k in masks]
        elif mode == "forward":
            rows = [(src_arr.at[_flat(_peer(me, k))], src_arr.at[_flat(_peer(me, k))], sib,
                     src_arr.at[_flat(_peer(sib, k))]) for k in _CHIP_MASKS]
        elif mode == "scatter":
            rows = [(src_arr.at[_flat(_peer(me, k))], land_arr.at[k - 1], _peer(me, k), land_arr.at[k - 1])
                    for k in range(1, N_DEV)]
        elif mode == "scatter_pair":
            rows = [(src_arr.at[_flat(_peer(me, q + 1))], land_arr.at[qi], sib, land_arr.at[qi])
                    for qi, q in enumerate((0,) + _CHIP_MASKS)]
        else:
            assert mode == "scatter_chips"
            rows = [(src_arr.at[qi + 1], land_arr.at[qi], _peer(me, q), land_arr.at[qi])
                    for qi, q in enumerate(_CHIP_MASKS)]
        assert len(rows) == per
        plan += [(t * per + s,) + row for s, row in enumerate(rows)]
    return plan


def exchange_start(name, arrays, mode, after):
    n, na = len(arrays), len(after)
    per, slots = _EXCHANGE_MODES[mode]
    bufs = list(arrays)
    if slots:
        bufs += [lax.empty((slots,) + a.shape[1:], a.dtype) for a in arrays]
    nb = len(bufs)

    def body(*refs):
        send_sems, recv_sems = refs[nb + na], refs[nb + na + 1]
        token = refs[-1]
        for s, src, dst, dev, _ in _plan(mode, refs[:nb], n, _mesh_pos()):
            pltpu.make_async_remote_copy(
                src_ref=src, dst_ref=dst, send_sem=send_sems.at[s], recv_sem=recv_sems.at[s],
                device_id=dev, device_id_type=pl.DeviceIdType.MESH).start()
        token[...] = jnp.zeros_like(token)

    out_shape = [pltpu.SemaphoreType.DMA((n * per,)), pltpu.SemaphoreType.DMA((n * per,))]
    out_shape += [pltpu.HBM(a.shape, a.dtype) for a in bufs]
    out_shape.append(jax.ShapeDtypeStruct((8, LANE), F32))
    args = [pltpu.with_memory_space_constraint(a, pltpu.HBM) for a in bufs] + list(after)
    outs = pl.pallas_call(
        body, name=name, out_shape=out_shape,
        in_specs=[_HBM] * nb + [_ANY] * na,
        out_specs=[_SEM, _SEM] + [_HBM] * nb + [pl.BlockSpec(memory_space=pltpu.VMEM)],
        input_output_aliases={i: 2 + i for i in range(nb)},
        compiler_params=pltpu.CompilerParams(has_side_effects=_EFFECT))(*args)
    return outs[0], outs[1], list(outs[2:2 + nb]), outs[-1]


def exchange_wait(name, started, mode, after):
    send_sems, recv_sems, bufs, _ = started
    nb, na = len(bufs), len(after)
    n = nb // 2 if _EXCHANGE_MODES[mode][1] else nb

    def body(*refs):
        send_sems_ref, recv_sems_ref = refs[nb], refs[nb + 1]
        for s, src, _, dev, land in _plan(mode, refs[:nb], n, _mesh_pos()):
            cp = pltpu.make_async_remote_copy(
                src_ref=src, dst_ref=land, send_sem=send_sems_ref.at[s], recv_sem=recv_sems_ref.at[s],
                device_id=dev, device_id_type=pl.DeviceIdType.MESH)
            cp.wait_send()
            cp.wait_recv()

    outs = pl.pallas_call(
        body, name=name, out_shape=[pltpu.HBM(a.shape, a.dtype) for a in bufs],
        in_specs=[_HBM] * nb + [_SEM, _SEM] + [_ANY] * na, out_specs=[_HBM] * nb,
        input_output_aliases={i: i for i in range(nb)},
        compiler_params=pltpu.CompilerParams(has_side_effects=_EFFECT))(
            *bufs, send_sems, recv_sems, *after)
    return list(outs)


def pair_sum(name, own, landed, blocks):
    _, r, c = own.shape
    tr = _row_tile(r, 256)

    def body(idx_ref, o_ref, l_ref, s_ref):
        s_ref[...] = (o_ref[...].astype(F32) + l_ref[...].astype(F32)).astype(s_ref.dtype)

    blk = pl.BlockSpec((None, tr, c), lambda q, i, idx: (q, i, 0))
    return pl.pallas_call(
        body, name=name,
        grid_spec=pltpu.PrefetchScalarGridSpec(
            num_scalar_prefetch=1, grid=(4, r // tr),
            in_specs=[pl.BlockSpec((None, tr, c), lambda q, i, idx: (idx[q], i, 0)), blk],
            out_specs=blk),
        out_shape=jax.ShapeDtypeStruct((4, r, c), own.dtype),
        compiler_params=_params(("parallel", "parallel")))(blocks, own, landed)


def _adamw_math(w, g, m, v):
    m = ADAM_B1 * m + (1.0 - ADAM_B1) * g
    v = ADAM_B2 * v + (1.0 - ADAM_B2) * (g * g)
    m_hat = m / (1.0 - ADAM_B1 ** ADAM_STEP)
    v_hat = v / (1.0 - ADAM_B2 ** ADAM_STEP)
    delta = -ADAM_LR * (m_hat / (jnp.sqrt(v_hat) + ADAM_EPS) + ADAM_WD * w)
    return delta, m, v


def reduce_adamw(name, land, w, m, v, tr, own=None, me_arr=None):
    R, C = w.shape
    S, _, Cp = land.shape

    def body(*refs):
        if own is not None:
            _, own_ref, l_ref, w_ref, m_ref, v_ref, g_ref, d_ref, nm_ref, nv_ref = refs
            g = own_ref[:, 0:C].astype(F32)
            first = 0
        else:
            l_ref, w_ref, m_ref, v_ref, g_ref, d_ref, nm_ref, nv_ref = refs
            g = l_ref[0, :, 0:C].astype(F32)
            first = 1
        for s in range(first, S):
            g = g + l_ref[s, :, 0:C].astype(F32)
        delta, nm, nv = _adamw_math(w_ref[...], g, m_ref[...], v_ref[...])
        g_ref[...] = g
        d_ref[...] = delta
        nm_ref[...] = nm
        nv_ref[...] = nv

    shp = jax.ShapeDtypeStruct((R, C), F32)
    if own is None:
        blk = pl.BlockSpec((tr, C), lambda i: (i, 0))
        return pl.pallas_call(
            body, name=name, grid=(R // tr,),
            in_specs=[pl.BlockSpec((S, tr, Cp), lambda i: (0, i, 0)), blk, blk, blk],
            out_specs=[blk, blk, blk, blk], out_shape=[shp, shp, shp, shp],
            compiler_params=_params(("parallel",)))(land, w, m, v)
    blk = pl.BlockSpec((tr, C), lambda i, me: (i, 0))
    return pl.pallas_call(
        body, name=name,
        grid_spec=pltpu.PrefetchScalarGridSpec(
            num_scalar_prefetch=1, grid=(R // tr,),
            in_specs=[pl.BlockSpec((None, tr, Cp), lambda i, me: (me[0], i, 0)),
                      pl.BlockSpec((S, tr, Cp), lambda i, me: (0, i, 0)), blk, blk, blk],
            out_specs=[blk, blk, blk, blk]),
        out_shape=[shp, shp, shp, shp],
        compiler_params=_params(("parallel",)))(me_arr, own, land, w, m, v)


def _row_tile(r, cap):
    t = min(r, cap)
    while r % t or t % 8:
        t -= 8
    return t


def kernel(x, norm_ffn1_g, ffn1_w_gate, ffn1_w_up, ffn1_w_down, norm_mix_g, w_in, rel_bias, w_out_att, w_out_ret, w_out, norm_ffn2_g, ffn2_w_gate, ffn2_w_up, ffn2_w_down, norm_final_g, loss_target, m_norm_ffn1_g, m_ffn1_w_gate, m_ffn1_w_up, m_ffn1_w_down, m_norm_mix_g, m_w_in, m_rel_bias, m_w_out_att, m_w_out_ret, m_w_out, m_norm_ffn2_g, m_ffn2_w_gate, m_ffn2_w_up, m_ffn2_w_down, m_norm_final_g, v_norm_ffn1_g, v_ffn1_w_gate, v_ffn1_w_up, v_ffn1_w_down, v_norm_mix_g, v_w_in, v_rel_bias, v_w_out_att, v_w_out_ret, v_w_out, v_norm_ffn2_g, v_ffn2_w_gate, v_ffn2_w_up, v_ffn2_w_down, v_norm_final_g):
    T, D = x.shape[1], x.shape[2]
    A = w_out_att.shape[1]
    H = A // HEAD_DIM
    nf = ffn1_w_gate.shape[2]
    nfp = _round_up(nf, LANE)
    nin = w_in.shape[2]
    nd = w_out.shape[1]
    assert nin % LANE == 0 and nd % LANE == 0 and (7 * A) % nd == 0 and T % ATT_TQ == 0
    tm = min(512, T)
    tw = min(2048, T)
    tw_in = min(1024, T)
    tn = min(256, T)
    x0 = x[0]
    tgt = loss_target[0]

    me_arr = (4 * lax.axis_index("x") + 2 * lax.axis_index("y") + lax.axis_index("c")).astype(jnp.int32).reshape(1)

    def slot(tag, w, rows_p=None, cols_p=None):
        return cast_shard("cast_" + tag, w[0], rows_p or w.shape[1], cols_p or w.shape[2], me_arr)

    def slot_t(tag, w):
        return cast_shard("cast_" + tag, jnp.transpose(w[0]), nfp, w.shape[1], me_arr)

    Wg1, Wu1 = all_gather_blocks([slot_t("wg1", ffn1_w_gate), slot_t("wu1", ffn1_w_up)])
    later = [("wd1", [slot("wd1", ffn1_w_down, rows_p=nfp)], True), ("win", [slot("win", w_in)], True),
             ("wout", [slot("woa", w_out_att), slot("wor", w_out_ret), slot("wo", w_out)], False),
             ("wgu2", [slot_t("wg2", ffn2_w_gate), slot_t("wu2", ffn2_w_up)], False),
             ("wd2", [slot("wd2", ffn2_w_down, rows_p=nfp)], False)]
    ag_started = {}
    order = Wg1
    for tag, ws, two_level in later:
        mode = "gather_chips" if two_level else "gather"
        ag_started[tag] = (exchange_start("ag_start_" + tag, ws, mode, [order]), mode)
        order = ag_started[tag][0][3]

    def gathered(tag, after):
        started, mode = ag_started[tag]
        got = exchange_wait("ag_wait_" + tag, started, mode, [after])
        if mode == "gather":
            return got
        passing = exchange_start("ag_pass_" + tag, got, "forward", [])
        return exchange_wait("ag_passed_" + tag, passing, "forward", [passing[3]])

    xi, yi, ci = lax.axis_index("x"), lax.axis_index("y"), lax.axis_index("c")
    my_side = jnp.stack([4 * (1 - xi if q & 4 else xi) + 2 * (1 - yi if q & 2 else yi) + ci
                         for q in (0,) + _CHIP_MASKS]).astype(jnp.int32)
    first_block = jnp.zeros((1,), jnp.int32)

    def swiglu(prods, _):
        a, b = prods
        return a, b, a * _sigmoid(a) * b

    def ffn_fwd(tag, xin, g, get_wgu, get_wd, after=()):
        h = rmsnorm_fwd(tag + "_norm", xin, g, tn, after)
        Wg, Wu = get_wgu(h)
        a, b, mid = mm_block(tag + "_up", T, tm, [(h, "full", D, 0)], [(Wg, 0, True), (Wu, 0, True)],
                             [], [("3d", nfp, CDT)] * 3, swiglu, row_split=2)
        Wd, = get_wd(mid)
        xo = mm_reduce_j(tag + "_down", T, tm, [(mid, "3d", nfp, 0, Wd, False)], D, F32, res=xin, scale=0.5,
                         jstep=2)
        return h, a, b, mid, xo, (Wg, Wu, Wd)

    h1, a1, b1, mid1, x1, (_, _, Wd1) = ffn_fwd(
        "ffn1", x0, norm_ffn1_g, lambda h: (Wg1, Wu1), lambda mid: gathered("wd1", mid), after=[order])
    h2 = rmsnorm_fwd("mix_norm", x1, norm_mix_g, tn)
    Win, = gathered("win", h2)
    proj, = mm_block("in_proj", T, tm, [(h2, "full", D, 0)], [(Win, 0, False)], [], [("col", nin, CDT)],
                     lambda p, _: p)
    biasm = attn_bias(rel_bias[0], ATT_TQ)
    att = attn_fwd(proj, biasm, A, ATT_TQ)
    tables = _retention_tables(T, H, RET_BLK)
    retg, ret_raw, states = ret_fwd(proj, tables, A, RET_BLK)
    Woa, Wor, Wo = gathered("wout", retg)
    goff = 7 * A // nd

    def merge(prods, ex):
        ba, br = prods
        ga, gr = ex
        return ba, br, _sigmoid(ga) * ba + _sigmoid(gr) * br

    ba, br, merged = mm_block(
        "branches", T, tm, [(att, "full", A, 0), (retg, "full", A, 0)], [(Woa, 0, False), (Wor, 1, False)],
        [(proj, "col", nd, goff), (proj, "col", nd, goff + N_DEV)], [("col", nd, CDT)] * 3, merge, row_split=2)
    x2 = mm_reduce_j("out_proj", T, tm, [(merged, "col", nd, 0, Wo, False)], D, F32, res=x1, scale=1.0,
                     jstep=N_DEV)
    h3, a2, b2, mid2, x3, (Wg2, Wu2, Wd2) = ffn_fwd(
        "ffn2", x2, norm_ffn2_g, lambda h: gathered("wgu2", h), lambda mid: gathered("wd2", mid))

    dx3, dx3h, dgf, loss_part = loss_head(x3, norm_final_g.reshape(1, D), tgt, tn)

    def swiglu_bwd(prods, ex):
        dm, = prods
        a, b = ex
        sg = _sigmoid(a)
        return dm * b * (sg * (1.0 + a * (1.0 - sg))), dm * (a * sg)

    def ffn_bwd(tag, dxh, h, a, b, mid, Wg, Wu, Wd, two_level=False):
        da, db = mm_block(tag + "_down_bwd", T, tm, [(dxh, "full", D, 0)], [(Wd, 0, True)],
                          [(a, "3d", nfp, 0), (b, "3d", nfp, 0)], [("3d", nfp, CDT)] * 2, swiglu_bwd, row_split=2)
        dWd = mm_reduce_i(tag + "_dwd", T, tw, (mid, "3d", nfp, 0), (dxh, "full", D, 0))
        dWg = mm_reduce_i(tag + "_dwg", T, tw, (da, "3d", nfp, 0), (h, "full", D, 0))
        dWu = mm_reduce_i(tag + "_dwu", T, tw, (db, "3d", nfp, 0), (h, "full", D, 0))
        mode = "scatter_pair" if two_level else "scatter"
        sent = exchange_start("rs_start_" + tag, [dWg, dWu, dWd], mode, [])
        dh = mm_reduce_j(tag + "_up_bwd", T, tm, [(da, "3d", nfp, 0, Wg, False), (db, "3d", nfp, 0, Wu, False)],
                         D, F32, after=[sent[3]], jstep=2)
        if two_level:
            got = exchange_wait("rs_pair_wait_" + tag, sent, mode, [dh])
            sums = [pair_sum("%s_pair_sum_%d" % (tag, t), got[t], got[3 + t], my_side) for t in range(3)]
            sent = exchange_start("rs_chips_start_" + tag, sums, "scatter_chips", [])
        return dh, sent

    dh3, sent_ffn2 = ffn_bwd("ffn2", dx3h, h3, a2, b2, mid2, Wg2, Wu2, Wd2)
    dx2, dx2c, dg2 = rmsnorm_bwd("ffn2_norm_bwd", x2, norm_ffn2_g, dh3, dx3, 1.0, tn)

    def merge_bwd(prods, ex):
        dmg, = prods
        ba_, br_, ga, gr = ex
        sa, sr = _sigmoid(ga), _sigmoid(gr)
        return dmg * sa, dmg * sr, dmg * ba_ * sa * (1.0 - sa), dmg * br_ * sr * (1.0 - sr)

    dba, dbr, dga, dgr = mm_block(
        "out_proj_bwd", T, tm, [(dx2c, "full", D, 0)], [(Wo, 0, True)],
        [(ba, "col", nd, 0), (br, "col", nd, 0), (proj, "col", nd, goff), (proj, "col", nd, goff + N_DEV)],
        [("col", nd, CDT)] * 4, merge_bwd, row_split=2)
    dWo = mm_reduce_i("dwo", T, tw, (merged, "col", nd, 0), (dx2c, "full", D, 0))
    dWoa = mm_reduce_i("dwoa", T, tw, (att, "full", A, 0), (dba, "col", nd, 0))
    dWor = mm_reduce_i("dwor", T, tw, (retg, "full", A, 0), (dbr, "col", nd, 0))
    sent_mix = exchange_start("rs_start_mix", [dWoa, dWor, dWo], "scatter", [])
    datt = mm_reduce_j("att_out_bwd", T, tm, [(dba, "col", nd, 0, Woa, True)], A, CDT, after=[sent_mix[3]],
                       jstep=N_DEV)
    dretg = mm_reduce_j("ret_out_bwd", T, tm, [(dbr, "col", nd, 0, Wor, True)], A, CDT, jstep=N_DEV)
    dq_r, dk_r, dv_r, dg_r = ret_bwd(proj, tables, ret_raw, states, dretg, A, RET_BLK)
    dq_a, dk_a, dv_a, dst = attn_bwd(proj, biasm, datt, A, ATT_TQ)
    dbias = jnp.pad(attn_bias_grad(dst, ATT_TQ), ((0, 0), (0, N_REL_PAD - N_REL)))
    dproj = jnp.concatenate([dq_a, dk_a, dv_a, dq_r, dk_r, dv_r, dg_r, dga, dgr], axis=1)
    dWin = mm_reduce_i("dwin", T, tw_in, (h2, "full", D, 0), (dproj, "col", nin, 0))
    sent_win = exchange_start("rs_start_win", [dWin], "scatter", [])
    dh2 = mm_reduce_j("in_proj_bwd", T, tm, [(dproj, "col", nin, 0, Win, True)], D, F32, after=[sent_win[3]],
                      jstep=2)
    dx1, dx1h, dgm = rmsnorm_bwd("mix_norm_bwd", x1, norm_mix_g, dh2, dx2, 0.5, tn)
    dh1, sent_ffn1 = ffn_bwd("ffn1", dx1h, h1, a1, b1, mid1, Wg1, Wu1, Wd1, two_level=True)
    grad_x, _, dg1 = rmsnorm_bwd("ffn1_norm_bwd", x0, norm_ffn1_g, dh1, dx1, 1.0, tn, after=[sent_ffn1[3]])

    dgains = jnp.concatenate([dg1, dgm, dg2, dgf, jnp.zeros((4, D), F32)], axis=0)

    def upd(name, own, land, w, m, v, own_block=me_arr, transposed=False):
        w2, m2, v2 = [jnp.transpose(t[0]) if transposed else t[0] for t in (w, m, v)]
        outs = reduce_adamw(name, land, w2, m2, v2, _row_tile(w2.shape[0], 256), own=own, me_arr=own_block)
        return [jnp.transpose(o)[None] if transposed else o[None] for o in outs]

    res = {}
    oWg2, oWu2, oWd2, lWg2, lWu2, lWd2 = exchange_wait("rs_wait_ffn2", sent_ffn2, "scatter", [grad_x])
    res["ffn2_w_gate"] = upd("adamw_wg2", oWg2, lWg2, ffn2_w_gate, m_ffn2_w_gate, v_ffn2_w_gate, transposed=True)
    res["ffn2_w_up"] = upd("adamw_wu2", oWu2, lWu2, ffn2_w_up, m_ffn2_w_up, v_ffn2_w_up, transposed=True)
    res["ffn2_w_down"] = upd("adamw_wd2", oWd2, lWd2, ffn2_w_down, m_ffn2_w_down, v_ffn2_w_down)
    oWoa, oWor, oWo, lWoa, lWor, lWo = exchange_wait("rs_wait_mix", sent_mix, "scatter", [res["ffn2_w_down"][1]])
    res["w_out_att"] = upd("adamw_woa", oWoa, lWoa, w_out_att, m_w_out_att, v_w_out_att)
    res["w_out_ret"] = upd("adamw_wor", oWor, lWor, w_out_ret, m_w_out_ret, v_w_out_ret)
    res["w_out"] = upd("adamw_wo", oWo, lWo, w_out, m_w_out, v_w_out)
    oWin, lWin = exchange_wait("rs_wait_win", sent_win, "scatter", [res["w_out"][1]])
    res["w_in"] = upd("adamw_win", oWin, lWin, w_in, m_w_in, v_w_in)
    lgains, lbias = exchange_partials([dgains, dbias], [res["w_in"][1]])
    oWg1, oWu1, oWd1, lWg1, lWu1, lWd1 = exchange_wait("rs_wait_ffn1", sent_ffn1, "scatter_chips", [lgains])
    res["ffn1_w_gate"] = upd("adamw_wg1", oWg1, lWg1, ffn1_w_gate, m_ffn1_w_gate, v_ffn1_w_gate, first_block, transposed=True)
    res["ffn1_w_up"] = upd("adamw_wu1", oWu1, lWu1, ffn1_w_up, m_ffn1_w_up, v_ffn1_w_up, first_block, transposed=True)
    res["ffn1_w_down"] = upd("adamw_wd1", oWd1, lWd1, ffn1_w_down, m_ffn1_w_down, v_ffn1_w_down, first_block)

    def stack_gains(a, b, c_, d):
        return jnp.concatenate([a, b, c_, d.reshape(1, D), jnp.zeros((4, D), F32)], axis=0)

    gw = stack_gains(norm_ffn1_g, norm_mix_g, norm_ffn2_g, norm_final_g)
    gm = stack_gains(m_norm_ffn1_g, m_norm_mix_g, m_norm_ffn2_g, m_norm_final_g)
    gv = stack_gains(v_norm_ffn1_g, v_norm_mix_g, v_norm_ffn2_g, v_norm_final_g)
    gains = reduce_adamw("adamw_gains", lgains, gw, gm, gv, 8)

    def padb(t):
        return jnp.pad(t[0], ((0, 0), (0, N_REL_PAD - N_REL)))

    bias = [o[:, :N_REL][None] for o in
            reduce_adamw("adamw_bias", lbias, padb(rel_bias), padb(m_rel_bias), padb(v_rel_bias), H)]
    res["norm_ffn1_g"] = [o[0:1] for o in gains]
    res["norm_mix_g"] = [o[1:2] for o in gains]
    res["norm_ffn2_g"] = [o[2:3] for o in gains]
    res["norm_final_g"] = [o[3] for o in gains]
    res["rel_bias"] = bias

    loss = lax.psum(loss_part[0, 0], MESH_AXES)
    names = ["norm_ffn1_g", "ffn1_w_gate", "ffn1_w_up", "ffn1_w_down", "norm_mix_g", "w_in", "rel_bias",
             "w_out_att", "w_out_ret", "w_out", "norm_ffn2_g", "ffn2_w_gate", "ffn2_w_up", "ffn2_w_down",
             "norm_final_g"]
    out = [loss, grad_x[None]]
    for k in range(4):
        out += [res[nm][k] for nm in names]
    return tuple(out)
```

```python
import functools
import math

import jax
import jax.numpy as jnp
import numpy as np
from jax import lax
from jax.experimental import pallas as pl
from jax.experimental.pallas import tpu as pltpu

F32 = jnp.float32
CDT = jnp.bfloat16

N_DEV = 8
CHUNK = 64
N_PREV_CHUNKS = 8
BAND = N_PREV_CHUNKS * CHUNK
HEAD_DIM = 128
MAX_REL_DIST = 128
N_REL = 2 * MAX_REL_DIST + 1
N_REL_PAD = 384
ROPE_BASE = 10000.0
EPS = 1e-6
NEG = -1e30
LANE = 128
ATT_TQ = 256
RET_BLK = 256
RET_HEADS_PER_STEP = 2
VMEM_LIMIT = 48 * 1024 * 1024

ADAM_LR = 0.001
ADAM_B1 = 0.9
ADAM_B2 = 0.999
ADAM_EPS = 1e-08
ADAM_WD = 0.01
ADAM_STEP = 10

MESH_AXES = ("x", "y", "c")
_NT = (((1,), (1,)), ((), ()))
_TN = (((0,), (0,)), ((), ()))


def _round_up(v, m):
    return (v + m - 1) // m * m


def _params(sem=None):
    return pltpu.CompilerParams(dimension_semantics=sem, vmem_limit_bytes=VMEM_LIMIT)


def _sigmoid(v):
    return 0.5 * (jnp.tanh(0.5 * v) + 1.0)


def _bspec(kind, tm, w, off, order, jmap=lambda j: j):
    def wrap(f):
        if order == "ji":
            return lambda j, i: f(i, jmap(j))
        return lambda i, j: f(i, jmap(j))
    if kind == "full":
        return pl.BlockSpec((tm, w), wrap(lambda i, j: (i, 0)))
    if kind == "col":
        return pl.BlockSpec((tm, w), wrap(lambda i, j: (i, j + off)))
    assert kind == "3d"
    return pl.BlockSpec((None, tm, w), wrap(lambda i, j: (j, i, 0)))


def _wspec(w, order, jmap=lambda j: j):
    if order == "ji":
        return pl.BlockSpec((None,) + w.shape[1:], lambda j, i: (jmap(j), 0, 0))
    return pl.BlockSpec((None,) + w.shape[1:], lambda i, j: (jmap(j), 0, 0))


def _width(arr, kind, w):
    return arr.shape[-1] if kind in ("full", "3d") else w


def mm_block(name, T, tm, lhs, wts, extras, outs, epilogue, row_split=1, order="ji"):
    nl, nw, ne = len(lhs), len(wts), len(extras)
    ni = T // tm
    tr_ = tm // row_split

    def body(*refs):
        l = refs[:nl]
        w = refs[nl:nl + nw]
        e = refs[nl + nw:nl + nw + ne]
        o = refs[nl + nw + ne:]
        for g in range(row_split):
            rows = pl.ds(g * tr_, tr_)
            prods = []
            for k, (_, li, tr) in enumerate(wts):
                a = l[li][rows, :]
                if tr:
                    prods.append(lax.dot_general(a, w[k][...], _NT, preferred_element_type=F32))
                else:
                    prods.append(jnp.dot(a, w[k][...], preferred_element_type=F32))
            res = epilogue(prods, [r[rows, :].astype(F32) for r in e])
            for r, val in zip(o, res):
                r[rows, :] = val.astype(r.dtype)

    in_specs = [_bspec(k, tm, _width(a, k, w), off, order) for (a, k, w, off) in lhs]
    in_specs += [_wspec(w, order) for (w, _, _) in wts]
    in_specs += [_bspec(k, tm, _width(a, k, w), off, order) for (a, k, w, off) in extras]
    out_specs, out_shape = [], []
    for (kind, w, dt) in outs:
        out_specs.append(_bspec(kind, tm, w, 0, order))
        if kind == "3d":
            out_shape.append(jax.ShapeDtypeStruct((N_DEV, T, w), dt))
        else:
            out_shape.append(jax.ShapeDtypeStruct((T, N_DEV * w), dt))
    args = [a for (a, _, _, _) in lhs] + [w for (w, _, _) in wts] + [a for (a, _, _, _) in extras]
    return pl.pallas_call(
        body, name=name, grid=(N_DEV, ni) if order == "ji" else (ni, N_DEV), in_specs=in_specs,
        out_specs=out_specs, out_shape=out_shape, compiler_params=_params(("parallel", "parallel")))(*args)


def mm_reduce_j(name, T, tm, pairs, out_w, out_dtype, res=None, scale=1.0, after=(), jstep=1):
    terms = [(p, u) for u in range(jstep) for p in pairs]
    nt = len(terms)
    nj = N_DEV // jstep
    ni = T // tm

    def body(*refs):
        xs = refs[:nt]
        ws = refs[nt:2 * nt]
        rest = refs[2 * nt:len(refs) - 2 - len(after)] + refs[len(refs) - 2:]
        if res is not None:
            res_ref, o_ref, acc = rest
        else:
            o_ref, acc = rest
        j = pl.program_id(1)

        @pl.when(j == 0)
        def _():
            acc[...] = jnp.zeros_like(acc)

        tot = None
        for k, (p, _) in enumerate(terms):
            if p[5]:
                d = lax.dot_general(xs[k][...], ws[k][...], _NT, preferred_element_type=F32)
            else:
                d = jnp.dot(xs[k][...], ws[k][...], preferred_element_type=F32)
            tot = d if tot is None else tot + d
        acc[...] += tot

        @pl.when(j == nj - 1)
        def _():
            if res is not None:
                o_ref[...] = (res_ref[...] + scale * acc[...]).astype(o_ref.dtype)
            else:
                o_ref[...] = acc[...].astype(o_ref.dtype)

    def jmap(u):
        return lambda j: j * jstep + u

    in_specs = [_bspec(p[1], tm, _width(p[0], p[1], p[2]), p[3], "ij", jmap(u)) for (p, u) in terms]
    in_specs += [_wspec(p[4], "ij", jmap(u)) for (p, u) in terms]
    args = [p[0] for (p, _) in terms] + [p[4] for (p, _) in terms]
    if res is not None:
        in_specs.append(pl.BlockSpec((tm, out_w), lambda i, j: (i, 0)))
        args.append(res)
    in_specs += [_ANY] * len(after)
    args += list(after)
    return pl.pallas_call(
        body, name=name, grid=(ni, nj), in_specs=in_specs,
        out_specs=pl.BlockSpec((tm, out_w), lambda i, j: (i, 0)),
        out_shape=jax.ShapeDtypeStruct((T, out_w), out_dtype),
        scratch_shapes=[pltpu.VMEM((tm, out_w), F32)],
        compiler_params=_params(("parallel", "arbitrary")))(*args)


def mm_reduce_i(name, T, tm, a, b, after=()):
    ni = T // tm
    rows = _width(a[0], a[1], a[2])
    cols = _width(b[0], b[1], b[2])

    def body(a_ref, b_ref, *rest):
        o_ref, acc = rest[len(after):]
        i = pl.program_id(1)

        @pl.when(i == 0)
        def _():
            acc[...] = jnp.zeros_like(acc)

        acc[...] += lax.dot_general(a_ref[...], b_ref[...], _TN, preferred_element_type=F32)

        @pl.when(i == ni - 1)
        def _():
            o_ref[...] = acc[...].astype(o_ref.dtype)

    return pl.pallas_call(
        body, name=name, grid=(N_DEV, ni),
        in_specs=[_bspec(a[1], tm, rows, a[3], "ji"), _bspec(b[1], tm, cols, b[3], "ji")] + [_ANY] * len(after),
        out_specs=pl.BlockSpec((None, rows, cols), lambda j, i: (j, 0, 0)),
        out_shape=jax.ShapeDtypeStruct((N_DEV, rows, cols), CDT),
        scratch_shapes=[pltpu.VMEM((rows, cols), F32)],
        compiler_params=_params(("parallel", "arbitrary")))(a[0], b[0], *after)


def _rms_bwd_math(xv, g, dy):
    r = lax.rsqrt(jnp.mean(xv * xv, axis=-1, keepdims=True) + EPS)
    xn = xv * r
    dxn = dy * g
    dx = r * (dxn - xn * jnp.mean(dxn * xn, axis=-1, keepdims=True))
    dg = jnp.sum(dy * xn, axis=0, keepdims=True)
    return dx, dg


def rmsnorm_fwd(name, x, g, tm, after=()):
    T, D = x.shape

    def body(x_ref, g_ref, *rest):
        o_ref = rest[-1]
        xv = x_ref[...]
        r = lax.rsqrt(jnp.mean(xv * xv, axis=-1, keepdims=True) + EPS)
        o_ref[...] = (xv * r * g_ref[...]).astype(o_ref.dtype)

    return pl.pallas_call(
        body, name=name, grid=(T // tm,),
        in_specs=[pl.BlockSpec((tm, D), lambda i: (i, 0)), pl.BlockSpec((1, D), lambda i: (0, 0))]
        + [_ANY] * len(after),
        out_specs=pl.BlockSpec((tm, D), lambda i: (i, 0)),
        out_shape=jax.ShapeDtypeStruct((T, D), CDT),
        compiler_params=_params(("parallel",)))(x, g, *after)


def rmsnorm_bwd(name, x, g, dh, dres, cscale, tm, after=()):
    T, D = x.shape

    def body(x_ref, g_ref, dh_ref, dres_ref, *rest):
        dx_ref, dxc_ref, dg_ref = rest[len(after):]
        i = pl.program_id(0)
        dx, dg = _rms_bwd_math(x_ref[...], g_ref[...], dh_ref[...])
        dx = dres_ref[...] + dx
        dx_ref[...] = dx
        dxc_ref[...] = (cscale * dx).astype(dxc_ref.dtype)

        @pl.when(i == 0)
        def _():
            dg_ref[...] = jnp.zeros_like(dg_ref)

        dg_ref[...] += dg

    row = pl.BlockSpec((tm, D), lambda i: (i, 0))
    vec = pl.BlockSpec((1, D), lambda i: (0, 0))
    return pl.pallas_call(
        body, name=name, grid=(T // tm,), in_specs=[row, vec, row, row] + [_ANY] * len(after),
        out_specs=[row, row, vec],
        out_shape=[jax.ShapeDtypeStruct((T, D), F32), jax.ShapeDtypeStruct((T, D), CDT),
                   jax.ShapeDtypeStruct((1, D), F32)],
        compiler_params=_params(("arbitrary",)))(x, g, dh, dres, *after)


def loss_head(x, g, tgt, tm):
    T, D = x.shape

    def body(x_ref, g_ref, t_ref, dx_ref, dxc_ref, dg_ref, loss_ref):
        i = pl.program_id(0)
        xv = x_ref[...]
        gv = g_ref[...]
        r = lax.rsqrt(jnp.mean(xv * xv, axis=-1, keepdims=True) + EPS)
        err = xv * r * gv - t_ref[...]
        part = jnp.sum(jnp.mean(err * err, axis=-1, keepdims=True), axis=0, keepdims=True)
        dx, dg = _rms_bwd_math(xv, gv, err / D)
        dx_ref[...] = dx
        dxc_ref[...] = (0.5 * dx).astype(dxc_ref.dtype)

        @pl.when(i == 0)
        def _():
            dg_ref[...] = jnp.zeros_like(dg_ref)
            loss_ref[...] = jnp.zeros_like(loss_ref)

        dg_ref[...] += dg
        loss_ref[...] += jnp.broadcast_to(0.5 * part, loss_ref.shape)

    row = pl.BlockSpec((tm, D), lambda i: (i, 0))
    vec = pl.BlockSpec((1, D), lambda i: (0, 0))
    return pl.pallas_call(
        body, name="loss_head", grid=(T // tm,), in_specs=[row, vec, row],
        out_specs=[row, row, vec, pl.BlockSpec((1, LANE), lambda i: (0, 0))],
        out_shape=[jax.ShapeDtypeStruct((T, D), F32), jax.ShapeDtypeStruct((T, D), CDT),
                   jax.ShapeDtypeStruct((1, D), F32), jax.ShapeDtypeStruct((1, LANE), F32)],
        compiler_params=_params(("arbitrary",)))(x, g, tgt)


def _skew_rows(z, left):
    tq, kw = z.shape
    row = lax.broadcasted_iota(jnp.int32, (tq, kw), 0)
    s = 1
    while s < tq:
        z = jnp.where((row & s) != 0, pltpu.roll(z, kw - s if left else s, 1), z)
        s *= 2
    return z


REL_HI = BAND + MAX_REL_DIST
REL_LO = BAND - MAX_REL_DIST


def attn_bias(rel_bias, tq):
    H = rel_bias.shape[0]
    kw = BAND + tq
    by_skew = jnp.concatenate(
        [jnp.broadcast_to(rel_bias[:, N_REL - 1:], (H, REL_LO)), rel_bias[:, ::-1],
         jnp.broadcast_to(rel_bias[:, :1], (H, kw - REL_HI - 1))], axis=1).reshape(H, 1, kw)

    def body(t_ref, o_ref):
        t = t_ref[...]
        qi = lax.broadcasted_iota(jnp.int32, (tq, kw), 0)
        kj = lax.broadcasted_iota(jnp.int32, (tq, kw), 1)
        b = _skew_rows(jnp.broadcast_to(t, (tq, kw)), left=False)
        b = jnp.where(kj < qi, t[:, 0:1], b)
        qc = qi // CHUNK
        kc = kj // CHUNK - N_PREV_CHUNKS
        valid = (kc <= qc) & (kc >= qc - N_PREV_CHUNKS)
        o_ref[...] = jnp.where(valid, b, NEG)

    return pl.pallas_call(
        body, name="attn_bias", grid=(H,),
        in_specs=[pl.BlockSpec((None, 1, kw), lambda h: (h, 0, 0))],
        out_specs=pl.BlockSpec((None, tq, kw), lambda h: (h, 0, 0)),
        out_shape=jax.ShapeDtypeStruct((H, tq, kw), F32),
        compiler_params=_params(("parallel",)))(by_skew)


def attn_bias_grad(dst, tq):
    H = dst.shape[0]
    kw = BAND + tq

    def body(d_ref, o_ref):
        z = _skew_rows(d_ref[...], left=True)
        qi = lax.broadcasted_iota(jnp.int32, (tq, kw), 0)
        kj = lax.broadcasted_iota(jnp.int32, (tq, kw), 1)
        wrapped = kj + qi >= kw
        c = jnp.sum(jnp.where(wrapped, 0.0, z), axis=0, keepdims=True)
        cw = jnp.sum(jnp.sum(jnp.where(wrapped, z, 0.0), axis=0, keepdims=True), axis=1, keepdims=True)
        lane = lax.broadcasted_iota(jnp.int32, (1, kw), 1)
        ahead = jnp.sum(jnp.where(lane >= REL_HI, c, 0.0), axis=1, keepdims=True)
        behind = jnp.sum(jnp.where(lane <= REL_LO, c, 0.0), axis=1, keepdims=True) + cw
        o_ref[...] = jnp.where(lane == REL_HI, ahead, jnp.where(lane == REL_LO, behind, c))

    by_skew = pl.pallas_call(
        body, name="attn_bias_grad", grid=(H,),
        in_specs=[pl.BlockSpec((None, tq, kw), lambda h: (h, 0, 0))],
        out_specs=pl.BlockSpec((None, 1, kw), lambda h: (h, 0, 0)),
        out_shape=jax.ShapeDtypeStruct((H, 1, kw), F32),
        compiler_params=_params(("parallel",)))(dst)
    return by_skew[:, 0, REL_LO:REL_HI + 1][:, ::-1]


def _attn_scores(q, kpad, bm_ref, start, kw):
    k = kpad[pl.ds(start, kw), :]
    s = lax.dot_general(q, k, _NT, preferred_element_type=F32) * (HEAD_DIM ** -0.5) + bm_ref[...]
    col = lax.broadcasted_iota(jnp.int32, s.shape, 1)
    s = jnp.where(col < BAND - start, NEG, s)
    m = jnp.max(s, axis=-1, keepdims=True)
    p = jnp.exp(s - m)
    return p / jnp.sum(p, axis=-1, keepdims=True), k


def _fill_padded(pad_ref, src_ref, T):
    pad_ref[pl.ds(0, BAND), :] = jnp.zeros((BAND, HEAD_DIM), pad_ref.dtype)
    pad_ref[pl.ds(BAND, T), :] = src_ref[...]


def attn_fwd(proj, biasm, A, tq):
    T = proj.shape[0]
    H = A // HEAD_DIM
    kw = BAND + tq

    def body(q_ref, k_ref, v_ref, bm_ref, o_ref, kpad, vpad):
        qi = pl.program_id(1)

        @pl.when(qi == 0)
        def _():
            _fill_padded(kpad, k_ref, T)
            _fill_padded(vpad, v_ref, T)

        start = pl.multiple_of(qi * tq, tq)
        p, _ = _attn_scores(q_ref[...], kpad, bm_ref, start, kw)
        v = vpad[pl.ds(start, kw), :]
        o_ref[...] = jnp.dot(p.astype(CDT), v, preferred_element_type=F32).astype(o_ref.dtype)

    return pl.pallas_call(
        body, name="attn_fwd", grid=(H, T // tq),
        in_specs=[pl.BlockSpec((tq, HEAD_DIM), lambda h, i: (i, h)),
                  pl.BlockSpec((T, HEAD_DIM), lambda h, i: (0, H + h)),
                  pl.BlockSpec((T, HEAD_DIM), lambda h, i: (0, 2 * H + h)),
                  pl.BlockSpec((None, tq, kw), lambda h, i: (h, 0, 0))],
        out_specs=pl.BlockSpec((tq, HEAD_DIM), lambda h, i: (i, h)),
        out_shape=jax.ShapeDtypeStruct((T, A), CDT),
        scratch_shapes=[pltpu.VMEM((BAND + T, HEAD_DIM), CDT), pltpu.VMEM((BAND + T, HEAD_DIM), CDT)],
        compiler_params=_params(("parallel", "arbitrary")))(proj, proj, proj, biasm)


def attn_bwd(proj, biasm, datt, A, tq):
    T = proj.shape[0]
    H = A // HEAD_DIM
    kw = BAND + tq
    nq = T // tq
    scale = HEAD_DIM ** -0.5

    def body(q_ref, k_ref, v_ref, bm_ref, do_ref, dq_ref, dk_ref, dv_ref, dst_ref,
             kpad, vpad, dkacc, dvacc):
        qi = pl.program_id(1)

        @pl.when(qi == 0)
        def _():
            _fill_padded(kpad, k_ref, T)
            _fill_padded(vpad, v_ref, T)
            dkacc[...] = jnp.zeros_like(dkacc)
            dvacc[...] = jnp.zeros_like(dvacc)
            dst_ref[...] = jnp.zeros_like(dst_ref)

        start = pl.multiple_of(qi * tq, tq)
        q = q_ref[...]
        p, k = _attn_scores(q, kpad, bm_ref, start, kw)
        v = vpad[pl.ds(start, kw), :]
        do = do_ref[...]
        dp = lax.dot_general(do, v, _NT, preferred_element_type=F32)
        ds = p * (dp - jnp.sum(dp * p, axis=-1, keepdims=True))
        dst_ref[...] += ds
        dsb = ds.astype(CDT)
        dq_ref[...] = (jnp.dot(dsb, k, preferred_element_type=F32) * scale).astype(dq_ref.dtype)
        dkacc[pl.ds(start, kw), :] += lax.dot_general(dsb, q, _TN, preferred_element_type=F32) * scale
        dvacc[pl.ds(start, kw), :] += lax.dot_general(p.astype(CDT), do, _TN, preferred_element_type=F32)

        @pl.when(qi == nq - 1)
        def _():
            dk_ref[...] = dkacc[pl.ds(BAND, T), :].astype(dk_ref.dtype)
            dv_ref[...] = dvacc[pl.ds(BAND, T), :].astype(dv_ref.dtype)

    blk = pl.BlockSpec((tq, HEAD_DIM), lambda h, i: (i, h))
    col = pl.BlockSpec((T, HEAD_DIM), lambda h, i: (0, h))
    bias = pl.BlockSpec((None, tq, kw), lambda h, i: (h, 0, 0))
    return pl.pallas_call(
        body, name="attn_bwd", grid=(H, nq),
        in_specs=[blk,
                  pl.BlockSpec((T, HEAD_DIM), lambda h, i: (0, H + h)),
                  pl.BlockSpec((T, HEAD_DIM), lambda h, i: (0, 2 * H + h)),
                  bias, blk],
        out_specs=[blk, col, col, bias],
        out_shape=[jax.ShapeDtypeStruct((T, A), CDT), jax.ShapeDtypeStruct((T, A), CDT),
                   jax.ShapeDtypeStruct((T, A), CDT), jax.ShapeDtypeStruct((H, tq, kw), F32)],
        scratch_shapes=[pltpu.VMEM((BAND + T, HEAD_DIM), CDT), pltpu.VMEM((BAND + T, HEAD_DIM), CDT),
                        pltpu.VMEM((BAND + T, HEAD_DIM), F32), pltpu.VMEM((BAND + T, HEAD_DIM), F32)],
        compiler_params=_params(("parallel", "arbitrary")))(proj, proj, proj, biasm, datt)


def _retention_tables(T, H, blk):
    half = HEAD_DIM // 2
    inv = 1.0 / (ROPE_BASE ** (jnp.arange(0, HEAD_DIM, 2, dtype=F32) / HEAD_DIM))
    ang = jnp.arange(T, dtype=F32)[:, None] * inv[None, :]
    cos, sin = jnp.cos(ang), jnp.sin(ang)
    rc = jnp.concatenate([cos, cos], axis=1)
    rs = jnp.concatenate([-sin, sin], axis=1)
    assert rc.shape == (T, 2 * half)
    log_g = jnp.log(1.0 - 2.0 ** (-5.0 - jnp.arange(H, dtype=F32)))[:, None, None]
    idx = jnp.arange(blk, dtype=F32)
    n, m = idx[:, None], idx[None, :]
    same = (n // CHUNK) == (m // CHUNK)
    earlier = (m // CHUNK) < (n // CHUNK)
    dist = jnp.where(same, jnp.abs(n - m), n - m)[None]
    dmat = jnp.where((same | earlier)[None], jnp.exp(log_g * dist), 0.0)
    ones = jnp.ones((1, 1, HEAD_DIM), F32)
    qd = jnp.exp(log_g * (idx[None, :, None] + 1.0)) * ones
    kd = jnp.exp(log_g * (blk - 1.0 - idx[None, :, None])) * ones
    cd = jnp.exp(log_g * blk) * jnp.ones((1, 8, HEAD_DIM), F32)
    return rc, rs, dmat, qd, kd, cd


def _rot(v, rc, rs):
    return v * rc + pltpu.roll(v, HEAD_DIM // 2, 1) * rs


def _rot_bwd(dv, rc, rs):
    return dv * rc + pltpu.roll(dv * rs, HEAD_DIM // 2, 1)


def ret_fwd(proj, tables, A, blk):
    T = proj.shape[0]
    H = A // HEAD_DIM
    nb = T // blk
    hp = RET_HEADS_PER_STEP
    rc, rs, dmat, qd, kd, cd = tables
    scale = HEAD_DIM ** -0.5

    def body(q_ref, k_ref, v_ref, g_ref, rc_ref, rs_ref, d_ref, qd_ref, kd_ref, cd_ref,
             y_ref, o_ref, st_ref, state):
        b = pl.program_id(1)

        @pl.when(b == 0)
        def _():
            state[...] = jnp.zeros_like(state)

        c, s = rc_ref[...], rs_ref[...]
        for u in range(hp):
            cols = pl.ds(u * HEAD_DIM, HEAD_DIM)
            qs = (_rot(q_ref[:, cols].astype(F32), c, s) * scale).astype(CDT)
            kr = _rot(k_ref[:, cols].astype(F32), c, s)
            v = v_ref[:, cols]
            sb = state[u].astype(CDT)
            a = lax.dot_general(qs, kr.astype(CDT), _NT, preferred_element_type=F32) * d_ref[u]
            o = jnp.dot(a.astype(CDT), v, preferred_element_type=F32)
            o = o + jnp.dot(qs, sb, preferred_element_type=F32) * qd_ref[u]
            st_ref[u] = sb
            state[u] = state[u] * cd_ref[u, 0:1, :] + lax.dot_general(
                (kr * kd_ref[u]).astype(CDT), v, _TN, preferred_element_type=F32)
            o_ref[:, cols] = o
            on = o * lax.rsqrt(jnp.mean(o * o, axis=-1, keepdims=True) + EPS)
            g = g_ref[:, cols].astype(F32)
            y_ref[:, cols] = (g * _sigmoid(g) * on).astype(y_ref.dtype)

    w = hp * HEAD_DIM

    def pj(off):
        return pl.BlockSpec((blk, w), lambda h, i: (i, off * H // hp + h))

    tok = pl.BlockSpec((blk, HEAD_DIM), lambda h, i: (i, 0))
    out = pl.BlockSpec((blk, w), lambda h, i: (i, h))

    def per_head(r, c):
        return pl.BlockSpec((hp, r, c), lambda h, i: (h, 0, 0))

    return pl.pallas_call(
        body, name="ret_fwd", grid=(H // hp, nb),
        in_specs=[pj(3), pj(4), pj(5), pj(6), tok, tok, per_head(blk, blk),
                  per_head(blk, HEAD_DIM), per_head(blk, HEAD_DIM), per_head(8, HEAD_DIM)],
        out_specs=[out, out, pl.BlockSpec((hp, None, HEAD_DIM, HEAD_DIM), lambda h, i: (h, i, 0, 0))],
        out_shape=[jax.ShapeDtypeStruct((T, A), CDT), jax.ShapeDtypeStruct((T, A), F32),
                   jax.ShapeDtypeStruct((H, nb, HEAD_DIM, HEAD_DIM), CDT)],
        scratch_shapes=[pltpu.VMEM((hp, HEAD_DIM, HEAD_DIM), F32)],
        compiler_params=_params(("parallel", "arbitrary")))(
            proj, proj, proj, proj, rc, rs, dmat, qd, kd, cd)


def ret_bwd(proj, tables, o_raw, states, dy, A, blk):
    T = proj.shape[0]
    H = A // HEAD_DIM
    nb = T // blk
    hp = RET_HEADS_PER_STEP
    rc, rs, dmat, qd, kd, cd = tables
    scale = HEAD_DIM ** -0.5

    def body(q_ref, k_ref, v_ref, g_ref, rc_ref, rs_ref, d_ref, qd_ref, kd_ref, cd_ref,
             o_ref, st_ref, dy_ref, dq_ref, dk_ref, dv_ref, dg_ref, dstate):
        b = pl.program_id(1)

        @pl.when(b == 0)
        def _():
            dstate[...] = jnp.zeros_like(dstate)

        c, s = rc_ref[...], rs_ref[...]
        for u in range(hp):
            cols = pl.ds(u * HEAD_DIM, HEAD_DIM)
            qs = (_rot(q_ref[:, cols].astype(F32), c, s) * scale).astype(CDT)
            kr = _rot(k_ref[:, cols].astype(F32), c, s)
            krb = kr.astype(CDT)
            kdb = (kr * kd_ref[u]).astype(CDT)
            v = v_ref[:, cols]
            dmat_v = d_ref[u]
            a = lax.dot_general(qs, krb, _NT, preferred_element_type=F32) * dmat_v

            o = o_ref[:, cols]
            r = lax.rsqrt(jnp.mean(o * o, axis=-1, keepdims=True) + EPS)
            on = o * r
            g = g_ref[:, cols].astype(F32)
            sg = _sigmoid(g)
            dyv = dy_ref[:, cols].astype(F32)
            dg_ref[:, cols] = (dyv * on * (sg * (1.0 + g * (1.0 - sg)))).astype(dg_ref.dtype)
            don = dyv * (g * sg)
            do = r * (don - on * jnp.mean(don * on, axis=-1, keepdims=True))
            dob = do.astype(CDT)
            doq = (do * qd_ref[u]).astype(CDT)
            dsb = dstate[u].astype(CDT)

            dv = lax.dot_general(a.astype(CDT), dob, _TN, preferred_element_type=F32)
            dv = dv + jnp.dot(kdb, dsb, preferred_element_type=F32)
            dv_ref[:, cols] = dv.astype(dv_ref.dtype)
            dpb = (lax.dot_general(dob, v, _NT, preferred_element_type=F32) * dmat_v).astype(CDT)
            dqs = jnp.dot(dpb, krb, preferred_element_type=F32)
            dqs = dqs + lax.dot_general(doq, st_ref[u], _NT, preferred_element_type=F32)
            dkr = lax.dot_general(dpb, qs, _TN, preferred_element_type=F32)
            dkr = dkr + lax.dot_general(v, dsb, _NT, preferred_element_type=F32) * kd_ref[u]
            dstate[u] = dstate[u] * cd_ref[u, 0:1, :] + lax.dot_general(
                qs, doq, _TN, preferred_element_type=F32)
            dq_ref[:, cols] = _rot_bwd(dqs * scale, c, s).astype(dq_ref.dtype)
            dk_ref[:, cols] = _rot_bwd(dkr, c, s).astype(dk_ref.dtype)

    w = hp * HEAD_DIM

    def pj(off):
        return pl.BlockSpec((blk, w), lambda h, i: (nb - 1 - i, off * H // hp + h))

    tok = pl.BlockSpec((blk, HEAD_DIM), lambda h, i: (nb - 1 - i, 0))
    out = pl.BlockSpec((blk, w), lambda h, i: (nb - 1 - i, h))

    def per_head(r, c):
        return pl.BlockSpec((hp, r, c), lambda h, i: (h, 0, 0))

    shp = jax.ShapeDtypeStruct((T, A), CDT)
    return pl.pallas_call(
        body, name="ret_bwd", grid=(H // hp, nb),
        in_specs=[pj(3), pj(4), pj(5), pj(6), tok, tok, per_head(blk, blk),
                  per_head(blk, HEAD_DIM), per_head(blk, HEAD_DIM), per_head(8, HEAD_DIM),
                  out, pl.BlockSpec((hp, None, HEAD_DIM, HEAD_DIM), lambda h, i: (h, nb - 1 - i, 0, 0)),
                  out],
        out_specs=[out, out, out, out], out_shape=[shp, shp, shp, shp],
        scratch_shapes=[pltpu.VMEM((hp, HEAD_DIM, HEAD_DIM), F32)],
        compiler_params=_params(("parallel", "arbitrary")))(
            proj, proj, proj, proj, rc, rs, dmat, qd, kd, cd, o_raw, states, dy)


def _mesh_pos():
    return lax.axis_index("x"), lax.axis_index("y"), lax.axis_index("c")


def _flat(pos):
    return 4 * pos[0] + 2 * pos[1] + pos[2]


_HBM = pl.BlockSpec(memory_space=pltpu.HBM)


def cast_shard(name, w, rows_p, cols_p, me_arr, after=()):
    r, c = w.shape
    tr = _row_tile(math.gcd(r, rows_p), 256)
    nr = r // tr

    def body(me_ref, w_ref, *rest):
        o_ref = rest[-1]
        i = pl.program_id(0)
        o_ref[...] = jnp.zeros_like(o_ref)

        @pl.when(i < nr)
        def _():
            o_ref[:, 0:c] = w_ref[...].astype(o_ref.dtype)

    return pl.pallas_call(
        body, name=name,
        grid_spec=pltpu.PrefetchScalarGridSpec(
            num_scalar_prefetch=1, grid=(rows_p // tr,),
            in_specs=[pl.BlockSpec((tr, c), lambda i, me: (jnp.minimum(i, nr - 1), 0))] + [_ANY] * len(after),
            out_specs=pl.BlockSpec((None, tr, cols_p), lambda i, me: (me[0], i, 0))),
        out_shape=jax.ShapeDtypeStruct((N_DEV, rows_p, cols_p), CDT),
        compiler_params=_params(("arbitrary",)))(me_arr, w, *after)


def exchange_partials(arrays, after):
    n, na = len(arrays), len(after)

    def body(*refs):
        ins, outs = refs[:n], refs[n + na:2 * n + na]
        send_sems, recv_sems, local_sems = refs[2 * n + na:]
        me = _mesh_pos()
        copies, locals_ = [], []
        for t in range(n):
            cp = pltpu.make_async_copy(ins[t], outs[t].at[_flat(me)], local_sems.at[t])
            cp.start()
            locals_.append(cp)
            for k in range(1, N_DEV):
                peer = _peer(me, k)
                send = pltpu.make_async_remote_copy(
                    src_ref=ins[t], dst_ref=outs[t].at[_flat(me)],
                    send_sem=send_sems.at[t, k - 1], recv_sem=recv_sems.at[t, k - 1],
                    device_id=peer, device_id_type=pl.DeviceIdType.MESH)
                send.start()
                recv = pltpu.make_async_remote_copy(
                    src_ref=ins[t], dst_ref=outs[t].at[_flat(peer)],
                    send_sem=send_sems.at[t, k - 1], recv_sem=recv_sems.at[t, k - 1],
                    device_id=peer, device_id_type=pl.DeviceIdType.MESH)
                copies.append((send, recv))
        for send, recv in copies:
            recv.wait_recv()
        for send, recv in copies:
            send.wait_send()
        for cp in locals_:
            cp.wait()

    return pl.pallas_call(
        body, name="exchange_partials",
        in_specs=[_HBM] * n + [_ANY] * na, out_specs=[_HBM] * n,
        out_shape=[jax.ShapeDtypeStruct((N_DEV,) + a.shape, a.dtype) for a in arrays],
        scratch_shapes=[pltpu.SemaphoreType.DMA((n, 7)), pltpu.SemaphoreType.DMA((n, 7)),
                        pltpu.SemaphoreType.DMA((n,))],
        )(*arrays, *after)


_SEM = pl.BlockSpec(memory_space=pltpu.SEMAPHORE)
_ANY = pl.BlockSpec(memory_space=pl.ANY)
_EFFECT = pltpu.SideEffectType.DATAFLOW_SIDE_EFFECTING


def _peer(me, k):
    return tuple(1 - v if bit else v for v, bit in zip(me, (k >> 2, (k >> 1) & 1, k & 1)))


_CHIP_MASKS = (2, 4, 6)

_EXCHANGE_MODES = {"gather": (7, None), "gather_chips": (4, None), "forward": (3, None),
                   "scatter": (7, 7), "scatter_pair": (4, 4), "scatter_chips": (3, 3)}


def _plan(mode, bufs, n, me):
    per = _EXCHANGE_MODES[mode][0]
    sib = _peer(me, 1)
    plan = []
    for t in range(n):
        src_arr, land_arr = bufs[t], bufs[n + t] if _EXCHANGE_MODES[mode][1] else None
        if mode in ("gather", "gather_chips"):
            masks = range(1, N_DEV) if mode == "gather" else (1,) + _CHIP_MASKS
            rows = [(src_arr.at[_flat(me)], src_arr.at[_flat(me)], _peer(me, k), src_arr.at[_flat(_peer(me, k))])
                    for k in masks]
        elif mode == "forward":
            rows = [(src_arr.at[_flat(_peer(me, k))], src_arr.at[_flat(_peer(me, k))], sib,
                     src_arr.at[_flat(_peer(sib, k))]) for k in _CHIP_MASKS]
        elif mode == "scatter":
            rows = [(src_arr.at[_flat(_peer(me, k))], land_arr.at[k - 1], _peer(me, k), land_arr.at[k - 1])
                    for k in range(1, N_DEV)]
        elif mode == "scatter_pair":
            rows = [(src_arr.at[_flat(_peer(me, q + 1))], land_arr.at[qi], sib, land_arr.at[qi])
                    for qi, q in enumerate((0,) + _CHIP_MASKS)]
        else:
            assert mode == "scatter_chips"
            rows = [(src_arr.at[qi + 1], land_arr.at[qi], _peer(me, q), land_arr.at[qi])
                    for qi, q in enumerate(_CHIP_MASKS)]
        assert len(rows) == per
        plan += [(t * per + s,) + row for s, row in enumerate(rows)]
    return plan


def exchange_start(name, arrays, mode, after):
    n, na = len(arrays), len(after)
    per, slots = _EXCHANGE_MODES[mode]
    bufs = list(arrays)
    if slots:
        bufs += [lax.empty((slots,) + a.shape[1:], a.dtype) for a in arrays]
    nb = len(bufs)

    def body(*refs):
        send_sems, recv_sems = refs[nb + na], refs[nb + na + 1]
        token = refs[-1]
        for s, src, dst, dev, _ in _plan(mode, refs[:nb], n, _mesh_pos()):
            pltpu.make_async_remote_copy(
                src_ref=src, dst_ref=dst, send_sem=send_sems.at[s], recv_sem=recv_sems.at[s],
                device_id=dev, device_id_type=pl.DeviceIdType.MESH).start()
        token[...] = jnp.zeros_like(token)

    out_shape = [pltpu.SemaphoreType.DMA((n * per,)), pltpu.SemaphoreType.DMA((n * per,))]
    out_shape += [pltpu.HBM(a.shape, a.dtype) for a in bufs]
    out_shape.append(jax.ShapeDtypeStruct((8, LANE), F32))
    args = [pltpu.with_memory_space_constraint(a, pltpu.HBM) for a in bufs] + list(after)
    outs = pl.pallas_call(
        body, name=name, out_shape=out_shape,
        in_specs=[_HBM] * nb + [_ANY] * na,
        out_specs=[_SEM, _SEM] + [_HBM] * nb + [pl.BlockSpec(memory_space=pltpu.VMEM)],
        input_output_aliases={i: 2 + i for i in range(nb)},
        compiler_params=pltpu.CompilerParams(has_side_effects=_EFFECT))(*args)
    return outs[0], outs[1], list(outs[2:2 + nb]), outs[-1]


def exchange_wait(name, started, mode, after):
    send_sems, recv_sems, bufs, _ = started
    nb, na = len(bufs), len(after)
    n = nb // 2 if _EXCHANGE_MODES[mode][1] else nb

    def body(*refs):
        send_sems_ref, recv_sems_ref = refs[nb], refs[nb + 1]
        for s, src, _, dev, land in _plan(mode, refs[:nb], n, _mesh_pos()):
            cp = pltpu.make_async_remote_copy(
                src_ref=src, dst_ref=land, send_sem=send_sems_ref.at[s], recv_sem=recv_sems_ref.at[s],
                device_id=dev, device_id_type=pl.DeviceIdType.MESH)
            cp.wait_send()
            cp.wait_recv()

    outs = pl.pallas_call(
        body, name=name, out_shape=[pltpu.HBM(a.shape, a.dtype) for a in bufs],
        in_specs=[_HBM] * nb + [_SEM, _SEM] + [_ANY] * na, out_specs=[_HBM] * nb,
        input_output_aliases={i: i for i in range(nb)},
        compiler_params=pltpu.CompilerParams(has_side_effects=_EFFECT))(
            *bufs, send_sems, recv_sems, *after)
    return list(outs)


def pair_sum(name, own, landed, blocks):
    _, r, c = own.shape
    tr = _row_tile(r, 256)

    def body(idx_ref, o_ref, l_ref, s_ref):
        s_ref[...] = (o_ref[...].astype(F32) + l_ref[...].astype(F32)).astype(s_ref.dtype)

    blk = pl.BlockSpec((None, tr, c), lambda q, i, idx: (q, i, 0))
    return pl.pallas_call(
        body, name=name,
        grid_spec=pltpu.PrefetchScalarGridSpec(
            num_scalar_prefetch=1, grid=(4, r // tr),
            in_specs=[pl.BlockSpec((None, tr, c), lambda q, i, idx: (idx[q], i, 0)), blk],
            out_specs=blk),
        out_shape=jax.ShapeDtypeStruct((4, r, c), own.dtype),
        compiler_params=_params(("parallel", "parallel")))(blocks, own, landed)


def _adamw_math(w, g, m, v):
    m = ADAM_B1 * m + (1.0 - ADAM_B1) * g
    v = ADAM_B2 * v + (1.0 - ADAM_B2) * (g * g)
    m_hat = m / (1.0 - ADAM_B1 ** ADAM_STEP)
    v_hat = v / (1.0 - ADAM_B2 ** ADAM_STEP)
    delta = -ADAM_LR * (m_hat / (jnp.sqrt(v_hat) + ADAM_EPS) + ADAM_WD * w)
    return delta, m, v


def reduce_adamw(name, land, w, m, v, tr, own=None, me_arr=None):
    R, C = w.shape
    S, _, Cp = land.shape

    def body(*refs):
        if own is not None:
            _, own_ref, l_ref, w_ref, m_ref, v_ref, g_ref, d_ref, nm_ref, nv_ref = refs
            g = own_ref[:, 0:C].astype(F32)
            first = 0
        else:
            l_ref, w_ref, m_ref, v_ref, g_ref, d_ref, nm_ref, nv_ref = refs
            g = l_ref[0, :, 0:C].astype(F32)
            first = 1
        for s in range(first, S):
            g = g + l_ref[s, :, 0:C].astype(F32)
        delta, nm, nv = _adamw_math(w_ref[...], g, m_ref[...], v_ref[...])
        g_ref[...] = g
        d_ref[...] = delta
        nm_ref[...] = nm
        nv_ref[...] = nv

    shp = jax.ShapeDtypeStruct((R, C), F32)
    if own is None:
        blk = pl.BlockSpec((tr, C), lambda i: (i, 0))
        return pl.pallas_call(
            body, name=name, grid=(R // tr,),
            in_specs=[pl.BlockSpec((S, tr, Cp), lambda i: (0, i, 0)), blk, blk, blk],
            out_specs=[blk, blk, blk, blk], out_shape=[shp, shp, shp, shp],
            compiler_params=_params(("parallel",)))(land, w, m, v)
    blk = pl.BlockSpec((tr, C), lambda i, me: (i, 0))
    return pl.pallas_call(
        body, name=name,
        grid_spec=pltpu.PrefetchScalarGridSpec(
            num_scalar_prefetch=1, grid=(R // tr,),
            in_specs=[pl.BlockSpec((None, tr, Cp), lambda i, me: (me[0], i, 0)),
                      pl.BlockSpec((S, tr, Cp), lambda i, me: (0, i, 0)), blk, blk, blk],
            out_specs=[blk, blk, blk, blk]),
        out_shape=[shp, shp, shp, shp],
        compiler_params=_params(("parallel",)))(me_arr, own, land, w, m, v)


def _row_tile(r, cap):
    t = min(r, cap)
    while r % t or t % 8:
        t -= 8
    return t


def kernel(x, norm_ffn1_g, ffn1_w_gate, ffn1_w_up, ffn1_w_down, norm_mix_g, w_in, rel_bias, w_out_att, w_out_ret, w_out, norm_ffn2_g, ffn2_w_gate, ffn2_w_up, ffn2_w_down, norm_final_g, loss_target, m_norm_ffn1_g, m_ffn1_w_gate, m_ffn1_w_up, m_ffn1_w_down, m_norm_mix_g, m_w_in, m_rel_bias, m_w_out_att, m_w_out_ret, m_w_out, m_norm_ffn2_g, m_ffn2_w_gate, m_ffn2_w_up, m_ffn2_w_down, m_norm_final_g, v_norm_ffn1_g, v_ffn1_w_gate, v_ffn1_w_up, v_ffn1_w_down, v_norm_mix_g, v_w_in, v_rel_bias, v_w_out_att, v_w_out_ret, v_w_out, v_norm_ffn2_g, v_ffn2_w_gate, v_ffn2_w_up, v_ffn2_w_down, v_norm_final_g):
    T, D = x.shape[1], x.shape[2]
    A = w_out_att.shape[1]
    H = A // HEAD_DIM
    nf = ffn1_w_gate.shape[2]
    nfp = _round_up(nf, LANE)
    nin = w_in.shape[2]
    nd = w_out.shape[1]
    assert nin % LANE == 0 and nd % LANE == 0 and (7 * A) % nd == 0 and T % ATT_TQ == 0
    tm = min(512, T)
    tw = min(2048, T)
    tw_in = min(1024, T)
    tn = min(256, T)
    x0 = x[0]
    tgt = loss_target[0]

    me_arr = (4 * lax.axis_index("x") + 2 * lax.axis_index("y") + lax.axis_index("c")).astype(jnp.int32).reshape(1)

    def slot(tag, w, after, rows_p=None):
        return cast_shard("cast_" + tag, w[0], rows_p or w.shape[1], w.shape[2], me_arr, after)

    def slot_t(tag, w, after):
        return cast_shard("cast_" + tag, jnp.transpose(w[0]), nfp, w.shape[1], me_arr, after)

    groups = [("wgu1", [("wg1", ffn1_w_gate, slot_t), ("wu1", ffn1_w_up, slot_t)], True),
              ("wd1", [("wd1", ffn1_w_down, functools.partial(slot, rows_p=nfp))], True),
              ("win", [("win", w_in, slot)], True),
              ("wout", [("woa", w_out_att, slot), ("wor", w_out_ret, slot), ("wo", w_out, slot)], False),
              ("wgu2", [("wg2", ffn2_w_gate, slot_t), ("wu2", ffn2_w_up, slot_t)], False),
              ("wd2", [("wd2", ffn2_w_down, functools.partial(slot, rows_p=nfp))], False)]
    ag_started = {}
    order = []
    for tag, members, two_level in groups:
        mode = "gather_chips" if two_level else "gather"
        started = exchange_start("ag_start_" + tag, [make(nm, w, order) for nm, w, make in members], mode, order)
        ag_started[tag] = (started, mode)
        order = [started[3]]

    def gathered(tag, after):
        started, mode = ag_started[tag]
        got = exchange_wait("ag_wait_" + tag, started, mode, [after])
        if mode == "gather":
            return got
        passing = exchange_start("ag_pass_" + tag, got, "forward", [])
        return exchange_wait("ag_passed_" + tag, passing, "forward", [passing[3]])

    xi, yi, ci = lax.axis_index("x"), lax.axis_index("y"), lax.axis_index("c")
    my_side = jnp.stack([4 * (1 - xi if q & 4 else xi) + 2 * (1 - yi if q & 2 else yi) + ci
                         for q in (0,) + _CHIP_MASKS]).astype(jnp.int32)
    first_block = jnp.zeros((1,), jnp.int32)

    def swiglu(prods, _):
        a, b = prods
        return a, b, a * _sigmoid(a) * b

    def ffn_fwd(tag, xin, g, get_wgu, get_wd, after=()):
        h = rmsnorm_fwd(tag + "_norm", xin, g, tn, after)
        Wg, Wu = get_wgu(h)
        a, b, mid = mm_block(tag + "_up", T, tm, [(h, "full", D, 0)], [(Wg, 0, True), (Wu, 0, True)],
                             [], [("3d", nfp, CDT)] * 3, swiglu)
        Wd, = get_wd(mid)
        xo = mm_reduce_j(tag + "_down", T, tm, [(mid, "3d", nfp, 0, Wd, False)], D, F32, res=xin, scale=0.5,
                         jstep=2)
        return h, a, b, mid, xo, (Wg, Wu, Wd)

    h1, a1, b1, mid1, x1, (Wg1, Wu1, Wd1) = ffn_fwd(
        "ffn1", x0, norm_ffn1_g, lambda h: gathered("wgu1", h), lambda mid: gathered("wd1", mid), after=order)
    h2 = rmsnorm_fwd("mix_norm", x1, norm_mix_g, tn)
    Win, = gathered("win", h2)
    proj, = mm_block("in_proj", T, tm, [(h2, "full", D, 0)], [(Win, 0, False)], [], [("col", nin, CDT)],
                     lambda p, _: p)
    biasm = attn_bias(rel_bias[0], ATT_TQ)
    att = attn_fwd(proj, biasm, A, ATT_TQ)
    tables = _retention_tables(T, H, RET_BLK)
    retg, ret_raw, states = ret_fwd(proj, tables, A, RET_BLK)
    Woa, Wor, Wo = gathered("wout", retg)
    goff = 7 * A // nd

    def merge(prods, ex):
        ba, br = prods
        ga, gr = ex
        return ba, br, _sigmoid(ga) * ba + _sigmoid(gr) * br

    ba, br, merged = mm_block(
        "branches", T, tm, [(att, "full", A, 0), (retg, "full", A, 0)], [(Woa, 0, False), (Wor, 1, False)],
        [(proj, "col", nd, goff), (proj, "col", nd, goff + N_DEV)], [("col", nd, CDT)] * 3, merge, order="ij")
    x2 = mm_reduce_j("out_proj", T, tm, [(merged, "col", nd, 0, Wo, False)], D, F32, res=x1, scale=1.0,
                     jstep=N_DEV)
    h3, a2, b2, mid2, x3, (Wg2, Wu2, Wd2) = ffn_fwd(
        "ffn2", x2, norm_ffn2_g, lambda h: gathered("wgu2", h), lambda mid: gathered("wd2", mid))

    dx3, dx3h, dgf, loss_part = loss_head(x3, norm_final_g.reshape(1, D), tgt, tn)

    def swiglu_bwd(prods, ex):
        dm, = prods
        a, b = ex
        sg = _sigmoid(a)
        return dm * b * (sg * (1.0 + a * (1.0 - sg))), dm * (a * sg)

    def ffn_bwd(tag, dxh, h, a, b, mid, Wg, Wu, Wd, two_level=False):
        da, db = mm_block(tag + "_down_bwd", T, tm, [(dxh, "full", D, 0)], [(Wd, 0, True)],
                          [(a, "3d", nfp, 0), (b, "3d", nfp, 0)], [("3d", nfp, CDT)] * 2, swiglu_bwd)

        def up_bwd(after):
            return mm_reduce_j(tag + "_up_bwd", T, tm,
                               [(da, "3d", nfp, 0, Wg, False), (db, "3d", nfp, 0, Wu, False)],
                               D, F32, after=after, jstep=2)

        if not two_level:
            dWd = mm_reduce_i(tag + "_dwd", T, tw, (mid, "3d", nfp, 0), (dxh, "full", D, 0))
            dWg = mm_reduce_i(tag + "_dwg", T, tw, (da, "3d", nfp, 0), (h, "full", D, 0))
            dWu = mm_reduce_i(tag + "_dwu", T, tw, (db, "3d", nfp, 0), (h, "full", D, 0))
            sent = exchange_start("rs_start_" + tag, [dWg, dWu, dWd], "scatter", [])
            return up_bwd([sent[3]]), sent

        def swap(nm, grad, after):
            return exchange_start("rs_pair_start_%s_%s" % (tag, nm), [grad], "scatter_pair", after)

        def to_chips(nm, swapping, after):
            own, landed = exchange_wait("rs_pair_wait_%s_%s" % (tag, nm), swapping, "scatter_pair", [after])
            sums = pair_sum("%s_pair_sum_%s" % (tag, nm), own, landed, my_side)
            return exchange_start("rs_chips_start_%s_%s" % (tag, nm), [sums], "scatter_chips", [])

        dWd = mm_reduce_i(tag + "_dwd", T, tw, (mid, "3d", nfp, 0), (dxh, "full", D, 0))
        swap_d = swap("d", dWd, [])
        dWg = mm_reduce_i(tag + "_dwg", T, tw, (da, "3d", nfp, 0), (h, "full", D, 0), after=[swap_d[3]])
        sent_d = to_chips("d", swap_d, dWg)
        swap_g = swap("g", dWg, [sent_d[3]])
        dWu = mm_reduce_i(tag + "_dwu", T, tw, (db, "3d", nfp, 0), (h, "full", D, 0), after=[swap_g[3]])
        sent_g = to_chips("g", swap_g, dWu)
        swap_u = swap("u", dWu, [sent_g[3]])
        dh = up_bwd([swap_u[3]])
        sent_u = to_chips("u", swap_u, dh)
        return dh, [sent_g, sent_u, sent_d]

    dh3, sent_ffn2 = ffn_bwd("ffn2", dx3h, h3, a2, b2, mid2, Wg2, Wu2, Wd2)
    dx2, dx2c, dg2 = rmsnorm_bwd("ffn2_norm_bwd", x2, norm_ffn2_g, dh3, dx3, 1.0, tn)

    def merge_bwd(prods, ex):
        dmg, = prods
        ba_, br_, ga, gr = ex
        sa, sr = _sigmoid(ga), _sigmoid(gr)
        return dmg * sa, dmg * sr, dmg * ba_ * sa * (1.0 - sa), dmg * br_ * sr * (1.0 - sr)

    dba, dbr, dga, dgr = mm_block(
        "out_proj_bwd", T, tm, [(dx2c, "full", D, 0)], [(Wo, 0, True)],
        [(ba, "col", nd, 0), (br, "col", nd, 0), (proj, "col", nd, goff), (proj, "col", nd, goff + N_DEV)],
        [("col", nd, CDT)] * 4, merge_bwd, order="ij")
    dWo = mm_reduce_i("dwo", T, tw, (merged, "col", nd, 0), (dx2c, "full", D, 0))
    dWoa = mm_reduce_i("dwoa", T, tw, (att, "full", A, 0), (dba, "col", nd, 0))
    dWor = mm_reduce_i("dwor", T, tw, (retg, "full", A, 0), (dbr, "col", nd, 0))
    sent_mix = exchange_start("rs_start_mix", [dWoa, dWor, dWo], "scatter", [])
    datt = mm_reduce_j("att_out_bwd", T, tm, [(dba, "col", nd, 0, Woa, True)], A, CDT, after=[sent_mix[3]],
                       jstep=N_DEV)
    dretg = mm_reduce_j("ret_out_bwd", T, tm, [(dbr, "col", nd, 0, Wor, True)], A, CDT, jstep=N_DEV)
    dq_r, dk_r, dv_r, dg_r = ret_bwd(proj, tables, ret_raw, states, dretg, A, RET_BLK)
    dq_a, dk_a, dv_a, dst = attn_bwd(proj, biasm, datt, A, ATT_TQ)
    dbias = jnp.pad(attn_bias_grad(dst, ATT_TQ), ((0, 0), (0, N_REL_PAD - N_REL)))
    dproj = jnp.concatenate([dq_a, dk_a, dv_a, dq_r, dk_r, dv_r, dg_r, dga, dgr], axis=1)
    dWin = mm_reduce_i("dwin", T, tw_in, (h2, "full", D, 0), (dproj, "col", nin, 0))
    sent_win = exchange_start("rs_start_win", [dWin], "scatter", [])
    dh2 = mm_reduce_j("in_proj_bwd", T, tm, [(dproj, "col", nin, 0, Win, True)], D, F32, after=[sent_win[3]],
                      jstep=2)
    dx1, dx1h, dgm = rmsnorm_bwd("mix_norm_bwd", x1, norm_mix_g, dh2, dx2, 0.5, tn)
    dh1, sent_ffn1 = ffn_bwd("ffn1", dx1h, h1, a1, b1, mid1, Wg1, Wu1, Wd1, two_level=True)
    grad_x, _, dg1 = rmsnorm_bwd("ffn1_norm_bwd", x0, norm_ffn1_g, dh1, dx1, 1.0, tn, after=[sent_ffn1[1][3]])

    dgains = jnp.concatenate([dg1, dgm, dg2, dgf, jnp.zeros((4, D), F32)], axis=0)

    def upd(name, own, land, w, m, v, own_block=me_arr, transposed=False):
        w2, m2, v2 = [jnp.transpose(t[0]) if transposed else t[0] for t in (w, m, v)]
        outs = reduce_adamw(name, land, w2, m2, v2, _row_tile(w2.shape[0], 256), own=own, me_arr=own_block)
        return [jnp.transpose(o)[None] if transposed else o[None] for o in outs]

    res = {}
    oWg2, oWu2, oWd2, lWg2, lWu2, lWd2 = exchange_wait("rs_wait_ffn2", sent_ffn2, "scatter", [grad_x])
    res["ffn2_w_gate"] = upd("adamw_wg2", oWg2, lWg2, ffn2_w_gate, m_ffn2_w_gate, v_ffn2_w_gate, transposed=True)
    res["ffn2_w_up"] = upd("adamw_wu2", oWu2, lWu2, ffn2_w_up, m_ffn2_w_up, v_ffn2_w_up, transposed=True)
    res["ffn2_w_down"] = upd("adamw_wd2", oWd2, lWd2, ffn2_w_down, m_ffn2_w_down, v_ffn2_w_down)
    oWoa, oWor, oWo, lWoa, lWor, lWo = exchange_wait("rs_wait_mix", sent_mix, "scatter", [res["ffn2_w_down"][1]])
    res["w_out_att"] = upd("adamw_woa", oWoa, lWoa, w_out_att, m_w_out_att, v_w_out_att)
    res["w_out_ret"] = upd("adamw_wor", oWor, lWor, w_out_ret, m_w_out_ret, v_w_out_ret)
    res["w_out"] = upd("adamw_wo", oWo, lWo, w_out, m_w_out, v_w_out)
    oWin, lWin = exchange_wait("rs_wait_win", sent_win, "scatter", [res["w_out"][1]])
    res["w_in"] = upd("adamw_win", oWin, lWin, w_in, m_w_in, v_w_in)
    lgains, lbias = exchange_partials([dgains, dbias], [res["w_in"][1]])
    (oWg1, lWg1), (oWu1, lWu1), (oWd1, lWd1) = [
        exchange_wait("rs_wait_ffn1_" + nm, started, "scatter_chips", [lgains])
        for nm, started in zip("gud", sent_ffn1)]
    res["ffn1_w_gate"] = upd("adamw_wg1", oWg1, lWg1, ffn1_w_gate, m_ffn1_w_gate, v_ffn1_w_gate, first_block, transposed=True)
    res["ffn1_w_up"] = upd("adamw_wu1", oWu1, lWu1, ffn1_w_up, m_ffn1_w_up, v_ffn1_w_up, first_block, transposed=True)
    res["ffn1_w_down"] = upd("adamw_wd1", oWd1, lWd1, ffn1_w_down, m_ffn1_w_down, v_ffn1_w_down, first_block)

    def stack_gains(a, b, c_, d):
        return jnp.concatenate([a, b, c_, d.reshape(1, D), jnp.zeros((4, D), F32)], axis=0)

    gw = stack_gains(norm_ffn1_g, norm_mix_g, norm_ffn2_g, norm_final_g)
    gm = stack_gains(m_norm_ffn1_g, m_norm_mix_g, m_norm_ffn2_g, m_norm_final_g)
    gv = stack_gains(v_norm_ffn1_g, v_norm_mix_g, v_norm_ffn2_g, v_norm_final_g)
    gains = reduce_adamw("adamw_gains", lgains, gw, gm, gv, 8)

    def padb(t):
        return jnp.pad(t[0], ((0, 0), (0, N_REL_PAD - N_REL)))

    bias = [o[:, :N_REL][None] for o in
            reduce_adamw("adamw_bias", lbias, padb(rel_bias), padb(m_rel_bias), padb(v_rel_bias), H)]
    res["norm_ffn1_g"] = [o[0:1] for o in gains]
    res["norm_mix_g"] = [o[1:2] for o in gains]
    res["norm_ffn2_g"] = [o[2:3] for o in gains]
    res["norm_final_g"] = [o[3] for o in gains]
    res["rel_bias"] = bias

    loss = lax.psum(loss_part[0, 0], MESH_AXES)
    names = ["norm_ffn1_g", "ffn1_w_gate", "ffn1_w_up", "ffn1_w_down", "norm_mix_g", "w_in", "rel_bias",
             "w_out_att", "w_out_ret", "w_out", "norm_ffn2_g", "ffn2_w_gate", "ffn2_w_up", "ffn2_w_down",
             "norm_final_g"]
    out = [loss, grad_x[None]]
    for k in range(4):
        out += [res[nm][k] for nm in names]
    return tuple(out)
```

```python
import functools
import math

import jax
import jax.numpy as jnp
import numpy as np
from jax import lax
from jax.experimental import pallas as pl
from jax.experimental.pallas import tpu as pltpu

F32 = jnp.float32
CDT = jnp.bfloat16

N_DEV = 8
CHUNK = 64
N_PREV_CHUNKS = 8
BAND = N_PREV_CHUNKS * CHUNK
HEAD_DIM = 128
MAX_REL_DIST = 128
N_REL = 2 * MAX_REL_DIST + 1
N_REL_PAD = 384
ROPE_BASE = 10000.0
EPS = 1e-6
NEG = -1e30
LANE = 128
ATT_TQ = 256
RET_BLK = 256
RET_HEADS_PER_STEP = 2
VMEM_LIMIT = 48 * 1024 * 1024

ADAM_LR = 0.001
ADAM_B1 = 0.9
ADAM_B2 = 0.999
ADAM_EPS = 1e-08
ADAM_WD = 0.01
ADAM_STEP = 10

MESH_AXES = ("x", "y", "c")
_NT = (((1,), (1,)), ((), ()))
_TN = (((0,), (0,)), ((), ()))


def _round_up(v, m):
    return (v + m - 1) // m * m


def _params(sem=None):
    return pltpu.CompilerParams(dimension_semantics=sem, vmem_limit_bytes=VMEM_LIMIT)


def _sigmoid(v):
    return 0.5 * (jnp.tanh(0.5 * v) + 1.0)


def _bspec(kind, tm, w, off, order, jmap=lambda j: j):
    def wrap(f):
        if order == "ji":
            return lambda j, i: f(i, jmap(j))
        return lambda i, j: f(i, jmap(j))
    if kind == "full":
        return pl.BlockSpec((tm, w), wrap(lambda i, j: (i, 0)))
    if kind == "col":
        return pl.BlockSpec((tm, w), wrap(lambda i, j: (i, j + off)))
    assert kind == "3d"
    return pl.BlockSpec((None, tm, w), wrap(lambda i, j: (j, i, 0)))


def _wspec(w, order, jmap=lambda j: j):
    if order == "ji":
        return pl.BlockSpec((None,) + w.shape[1:], lambda j, i: (jmap(j), 0, 0))
    return pl.BlockSpec((None,) + w.shape[1:], lambda i, j: (jmap(j), 0, 0))


def _width(arr, kind, w):
    return arr.shape[-1] if kind in ("full", "3d") else w


def mm_block(name, T, tm, lhs, wts, extras, outs, epilogue, order="ji"):
    nl, nw, ne = len(lhs), len(wts), len(extras)
    ni = T // tm

    def body(*refs):
        l = refs[:nl]
        w = refs[nl:nl + nw]
        e = refs[nl + nw:nl + nw + ne]
        o = refs[nl + nw + ne:]
        prods = []
        for k, (_, li, tr) in enumerate(wts):
            a = l[li][...]
            if tr:
                prods.append(lax.dot_general(a, w[k][...], _NT, preferred_element_type=F32))
            else:
                prods.append(jnp.dot(a, w[k][...], preferred_element_type=F32))
        res = epilogue(prods, [r[...].astype(F32) for r in e])
        for r, val in zip(o, res):
            r[...] = val.astype(r.dtype)

    in_specs = [_bspec(k, tm, _width(a, k, w), off, order) for (a, k, w, off) in lhs]
    in_specs += [_wspec(w, order) for (w, _, _) in wts]
    in_specs += [_bspec(k, tm, _width(a, k, w), off, order) for (a, k, w, off) in extras]
    out_specs, out_shape = [], []
    for (kind, w, dt) in outs:
        out_specs.append(_bspec(kind, tm, w, 0, order))
        if kind == "3d":
            out_shape.append(jax.ShapeDtypeStruct((N_DEV, T, w), dt))
        else:
            out_shape.append(jax.ShapeDtypeStruct((T, N_DEV * w), dt))
    args = [a for (a, _, _, _) in lhs] + [w for (w, _, _) in wts] + [a for (a, _, _, _) in extras]
    return pl.pallas_call(
        body, name=name, grid=(N_DEV, ni) if order == "ji" else (ni, N_DEV), in_specs=in_specs,
        out_specs=out_specs, out_shape=out_shape, compiler_params=_params(("parallel", "parallel")))(*args)


def mm_reduce_j(name, T, tm, pairs, out_w, out_dtype, res=None, scale=1.0, after=(), jstep=1):
    terms = [(p, u) for u in range(jstep) for p in pairs]
    nt = len(terms)
    nj = N_DEV // jstep
    ni = T // tm

    def body(*refs):
        xs = refs[:nt]
        ws = refs[nt:2 * nt]
        rest = refs[2 * nt:len(refs) - 2 - len(after)] + refs[len(refs) - 2:]
        if res is not None:
            res_ref, o_ref, acc = rest
        else:
            o_ref, acc = rest
        j = pl.program_id(1)

        @pl.when(j == 0)
        def _():
            acc[...] = jnp.zeros_like(acc)

        tot = None
        for k, (p, _) in enumerate(terms):
            if p[5]:
                d = lax.dot_general(xs[k][...], ws[k][...], _NT, preferred_element_type=F32)
            else:
                d = jnp.dot(xs[k][...], ws[k][...], preferred_element_type=F32)
            tot = d if tot is None else tot + d
        acc[...] += tot

        @pl.when(j == nj - 1)
        def _():
            if res is not None:
                o_ref[...] = (res_ref[...] + scale * acc[...]).astype(o_ref.dtype)
            else:
                o_ref[...] = acc[...].astype(o_ref.dtype)

    def jmap(u):
        return lambda j: j * jstep + u

    in_specs = [_bspec(p[1], tm, _width(p[0], p[1], p[2]), p[3], "ij", jmap(u)) for (p, u) in terms]
    in_specs += [_wspec(p[4], "ij", jmap(u)) for (p, u) in terms]
    args = [p[0] for (p, _) in terms] + [p[4] for (p, _) in terms]
    if res is not None:
        in_specs.append(pl.BlockSpec((tm, out_w), lambda i, j: (i, 0)))
        args.append(res)
    in_specs += [_ANY] * len(after)
    args += list(after)
    return pl.pallas_call(
        body, name=name, grid=(ni, nj), in_specs=in_specs,
        out_specs=pl.BlockSpec((tm, out_w), lambda i, j: (i, 0)),
        out_shape=jax.ShapeDtypeStruct((T, out_w), out_dtype),
        scratch_shapes=[pltpu.VMEM((tm, out_w), F32)],
        compiler_params=_params(("parallel", "arbitrary")))(*args)


def mm_reduce_i(name, T, tm, a, b, after=()):
    ni = T // tm
    rows = _width(a[0], a[1], a[2])
    cols = _width(b[0], b[1], b[2])

    def body(a_ref, b_ref, *rest):
        o_ref, acc = rest[len(after):]
        i = pl.program_id(1)

        @pl.when(i == 0)
        def _():
            acc[...] = jnp.zeros_like(acc)

        acc[...] += lax.dot_general(a_ref[...], b_ref[...], _TN, preferred_element_type=F32)

        @pl.when(i == ni - 1)
        def _():
            o_ref[...] = acc[...].astype(o_ref.dtype)

    return pl.pallas_call(
        body, name=name, grid=(N_DEV, ni),
        in_specs=[_bspec(a[1], tm, rows, a[3], "ji"), _bspec(b[1], tm, cols, b[3], "ji")] + [_ANY] * len(after),
        out_specs=pl.BlockSpec((None, rows, cols), lambda j, i: (j, 0, 0)),
        out_shape=jax.ShapeDtypeStruct((N_DEV, rows, cols), CDT),
        scratch_shapes=[pltpu.VMEM((rows, cols), F32)],
        compiler_params=_params(("parallel", "arbitrary")))(a[0], b[0], *after)


def _rms_bwd_math(xv, g, dy):
    r = lax.rsqrt(jnp.mean(xv * xv, axis=-1, keepdims=True) + EPS)
    xn = xv * r
    dxn = dy * g
    dx = r * (dxn - xn * jnp.mean(dxn * xn, axis=-1, keepdims=True))
    dg = jnp.sum(dy * xn, axis=0, keepdims=True)
    return dx, dg


def rmsnorm_fwd(name, x, g, tm, after=()):
    T, D = x.shape

    def body(x_ref, g_ref, *rest):
        o_ref = rest[-1]
        xv = x_ref[...]
        r = lax.rsqrt(jnp.mean(xv * xv, axis=-1, keepdims=True) + EPS)
        o_ref[...] = (xv * r * g_ref[...]).astype(o_ref.dtype)

    return pl.pallas_call(
        body, name=name, grid=(T // tm,),
        in_specs=[pl.BlockSpec((tm, D), lambda i: (i, 0)), pl.BlockSpec((1, D), lambda i: (0, 0))]
        + [_ANY] * len(after),
        out_specs=pl.BlockSpec((tm, D), lambda i: (i, 0)),
        out_shape=jax.ShapeDtypeStruct((T, D), CDT),
        compiler_params=_params(("parallel",)))(x, g, *after)


def rmsnorm_bwd(name, x, g, dh, dres, cscale, tm, after=()):
    T, D = x.shape

    def body(x_ref, g_ref, dh_ref, dres_ref, *rest):
        dx_ref, dxc_ref, dg_ref = rest[len(after):]
        i = pl.program_id(0)
        dx, dg = _rms_bwd_math(x_ref[...], g_ref[...], dh_ref[...])
        dx = dres_ref[...] + dx
        dx_ref[...] = dx
        dxc_ref[...] = (cscale * dx).astype(dxc_ref.dtype)

        @pl.when(i == 0)
        def _():
            dg_ref[...] = jnp.zeros_like(dg_ref)

        dg_ref[...] += dg

    row = pl.BlockSpec((tm, D), lambda i: (i, 0))
    vec = pl.BlockSpec((1, D), lambda i: (0, 0))
    return pl.pallas_call(
        body, name=name, grid=(T // tm,), in_specs=[row, vec, row, row] + [_ANY] * len(after),
        out_specs=[row, row, vec],
        out_shape=[jax.ShapeDtypeStruct((T, D), F32), jax.ShapeDtypeStruct((T, D), CDT),
                   jax.ShapeDtypeStruct((1, D), F32)],
        compiler_params=_params(("arbitrary",)))(x, g, dh, dres, *after)


def loss_head(x, g, tgt, tm):
    T, D = x.shape

    def body(x_ref, g_ref, t_ref, dx_ref, dxc_ref, dg_ref, loss_ref):
        i = pl.program_id(0)
        xv = x_ref[...]
        gv = g_ref[...]
        r = lax.rsqrt(jnp.mean(xv * xv, axis=-1, keepdims=True) + EPS)
        err = xv * r * gv - t_ref[...]
        part = jnp.sum(jnp.mean(err * err, axis=-1, keepdims=True), axis=0, keepdims=True)
        dx, dg = _rms_bwd_math(xv, gv, err / D)
        dx_ref[...] = dx
        dxc_ref[...] = (0.5 * dx).astype(dxc_ref.dtype)

        @pl.when(i == 0)
        def _():
            dg_ref[...] = jnp.zeros_like(dg_ref)
            loss_ref[...] = jnp.zeros_like(loss_ref)

        dg_ref[...] += dg
        loss_ref[...] += jnp.broadcast_to(0.5 * part, loss_ref.shape)

    row = pl.BlockSpec((tm, D), lambda i: (i, 0))
    vec = pl.BlockSpec((1, D), lambda i: (0, 0))
    return pl.pallas_call(
        body, name="loss_head", grid=(T // tm,), in_specs=[row, vec, row],
        out_specs=[row, row, vec, pl.BlockSpec((1, LANE), lambda i: (0, 0))],
        out_shape=[jax.ShapeDtypeStruct((T, D), F32), jax.ShapeDtypeStruct((T, D), CDT),
                   jax.ShapeDtypeStruct((1, D), F32), jax.ShapeDtypeStruct((1, LANE), F32)],
        compiler_params=_params(("arbitrary",)))(x, g, tgt)


def _skew_rows(z, left):
    tq, kw = z.shape
    row = lax.broadcasted_iota(jnp.int32, (tq, kw), 0)
    s = 1
    while s < tq:
        z = jnp.where((row & s) != 0, pltpu.roll(z, kw - s if left else s, 1), z)
        s *= 2
    return z


REL_HI = BAND + MAX_REL_DIST
REL_LO = BAND - MAX_REL_DIST


def attn_bias(rel_bias, tq):
    H = rel_bias.shape[0]
    kw = BAND + tq
    by_skew = jnp.concatenate(
        [jnp.broadcast_to(rel_bias[:, N_REL - 1:], (H, REL_LO)), rel_bias[:, ::-1],
         jnp.broadcast_to(rel_bias[:, :1], (H, kw - REL_HI - 1))], axis=1).reshape(H, 1, kw)

    def body(t_ref, o_ref):
        t = t_ref[...]
        qi = lax.broadcasted_iota(jnp.int32, (tq, kw), 0)
        kj = lax.broadcasted_iota(jnp.int32, (tq, kw), 1)
        b = _skew_rows(jnp.broadcast_to(t, (tq, kw)), left=False)
        b = jnp.where(kj < qi, t[:, 0:1], b)
        qc = qi // CHUNK
        kc = kj // CHUNK - N_PREV_CHUNKS
        valid = (kc <= qc) & (kc >= qc - N_PREV_CHUNKS)
        o_ref[...] = jnp.where(valid, b, NEG)

    return pl.pallas_call(
        body, name="attn_bias", grid=(H,),
        in_specs=[pl.BlockSpec((None, 1, kw), lambda h: (h, 0, 0))],
        out_specs=pl.BlockSpec((None, tq, kw), lambda h: (h, 0, 0)),
        out_shape=jax.ShapeDtypeStruct((H, tq, kw), F32),
        compiler_params=_params(("parallel",)))(by_skew)


def attn_bias_grad(dst, tq):
    H = dst.shape[0]
    kw = BAND + tq

    def body(d_ref, o_ref):
        z = _skew_rows(d_ref[...], left=True)
        qi = lax.broadcasted_iota(jnp.int32, (tq, kw), 0)
        kj = lax.broadcasted_iota(jnp.int32, (tq, kw), 1)
        wrapped = kj + qi >= kw
        c = jnp.sum(jnp.where(wrapped, 0.0, z), axis=0, keepdims=True)
        cw = jnp.sum(jnp.sum(jnp.where(wrapped, z, 0.0), axis=0, keepdims=True), axis=1, keepdims=True)
        lane = lax.broadcasted_iota(jnp.int32, (1, kw), 1)
        ahead = jnp.sum(jnp.where(lane >= REL_HI, c, 0.0), axis=1, keepdims=True)
        behind = jnp.sum(jnp.where(lane <= REL_LO, c, 0.0), axis=1, keepdims=True) + cw
        o_ref[...] = jnp.where(lane == REL_HI, ahead, jnp.where(lane == REL_LO, behind, c))

    by_skew = pl.pallas_call(
        body, name="attn_bias_grad", grid=(H,),
        in_specs=[pl.BlockSpec((None, tq, kw), lambda h: (h, 0, 0))],
        out_specs=pl.BlockSpec((None, 1, kw), lambda h: (h, 0, 0)),
        out_shape=jax.ShapeDtypeStruct((H, 1, kw), F32),
        compiler_params=_params(("parallel",)))(dst)
    return by_skew[:, 0, REL_LO:REL_HI + 1][:, ::-1]


def _attn_scores(q, kpad, bm_ref, start, kw):
    k = kpad[pl.ds(start, kw), :]
    s = lax.dot_general(q, k, _NT, preferred_element_type=F32) * (HEAD_DIM ** -0.5) + bm_ref[...]
    col = lax.broadcasted_iota(jnp.int32, s.shape, 1)
    s = jnp.where(col < BAND - start, NEG, s)
    m = jnp.max(s, axis=-1, keepdims=True)
    e = jnp.exp(s - m)
    return e, 1.0 / jnp.sum(e, axis=-1, keepdims=True), k


def _fill_padded(pad_ref, src_ref, T):
    pad_ref[pl.ds(0, BAND), :] = jnp.zeros((BAND, HEAD_DIM), pad_ref.dtype)
    pad_ref[pl.ds(BAND, T), :] = src_ref[...]


def attn_fwd(proj, biasm, A, tq):
    T = proj.shape[0]
    H = A // HEAD_DIM
    kw = BAND + tq

    def body(q_ref, k_ref, v_ref, bm_ref, o_ref, kpad, vpad):
        qi = pl.program_id(1)

        @pl.when(qi == 0)
        def _():
            _fill_padded(kpad, k_ref, T)
            _fill_padded(vpad, v_ref, T)

        start = pl.multiple_of(qi * tq, tq)
        e, rinv, _ = _attn_scores(q_ref[...], kpad, bm_ref, start, kw)
        v = vpad[pl.ds(start, kw), :]
        o_ref[...] = (jnp.dot(e.astype(CDT), v, preferred_element_type=F32) * rinv).astype(o_ref.dtype)

    return pl.pallas_call(
        body, name="attn_fwd", grid=(H, T // tq),
        in_specs=[pl.BlockSpec((tq, HEAD_DIM), lambda h, i: (i, h)),
                  pl.BlockSpec((T, HEAD_DIM), lambda h, i: (0, H + h)),
                  pl.BlockSpec((T, HEAD_DIM), lambda h, i: (0, 2 * H + h)),
                  pl.BlockSpec((None, tq, kw), lambda h, i: (h, 0, 0))],
        out_specs=pl.BlockSpec((tq, HEAD_DIM), lambda h, i: (i, h)),
        out_shape=jax.ShapeDtypeStruct((T, A), CDT),
        scratch_shapes=[pltpu.VMEM((BAND + T, HEAD_DIM), CDT), pltpu.VMEM((BAND + T, HEAD_DIM), CDT)],
        compiler_params=_params(("parallel", "arbitrary")))(proj, proj, proj, biasm)


def attn_bwd(proj, biasm, datt, A, tq):
    T = proj.shape[0]
    H = A // HEAD_DIM
    kw = BAND + tq
    nq = T // tq
    scale = HEAD_DIM ** -0.5

    def body(q_ref, k_ref, v_ref, bm_ref, do_ref, dq_ref, dk_ref, dv_ref, dst_ref,
             kpad, vpad, dkacc, dvacc):
        qi = pl.program_id(1)

        @pl.when(qi == 0)
        def _():
            _fill_padded(kpad, k_ref, T)
            _fill_padded(vpad, v_ref, T)
            dkacc[...] = jnp.zeros_like(dkacc)
            dvacc[...] = jnp.zeros_like(dvacc)
            dst_ref[...] = jnp.zeros_like(dst_ref)

        start = pl.multiple_of(qi * tq, tq)
        q = q_ref[...]
        e, rinv, k = _attn_scores(q, kpad, bm_ref, start, kw)
        p = e * rinv
        v = vpad[pl.ds(start, kw), :]
        do = do_ref[...]
        dp = lax.dot_general(do, v, _NT, preferred_element_type=F32)
        ds = p * (dp - jnp.sum(dp * p, axis=-1, keepdims=True))
        dst_ref[...] += ds
        dsb = ds.astype(CDT)
        dq_ref[...] = (jnp.dot(dsb, k, preferred_element_type=F32) * scale).astype(dq_ref.dtype)
        dkacc[pl.ds(start, kw), :] += lax.dot_general(dsb, q, _TN, preferred_element_type=F32) * scale
        dvacc[pl.ds(start, kw), :] += lax.dot_general(p.astype(CDT), do, _TN, preferred_element_type=F32)

        @pl.when(qi == nq - 1)
        def _():
            dk_ref[...] = dkacc[pl.ds(BAND, T), :].astype(dk_ref.dtype)
            dv_ref[...] = dvacc[pl.ds(BAND, T), :].astype(dv_ref.dtype)

    blk = pl.BlockSpec((tq, HEAD_DIM), lambda h, i: (i, h))
    col = pl.BlockSpec((T, HEAD_DIM), lambda h, i: (0, h))
    bias = pl.BlockSpec((None, tq, kw), lambda h, i: (h, 0, 0))
    return pl.pallas_call(
        body, name="attn_bwd", grid=(H, nq),
        in_specs=[blk,
                  pl.BlockSpec((T, HEAD_DIM), lambda h, i: (0, H + h)),
                  pl.BlockSpec((T, HEAD_DIM), lambda h, i: (0, 2 * H + h)),
                  bias, blk],
        out_specs=[blk, col, col, bias],
        out_shape=[jax.ShapeDtypeStruct((T, A), CDT), jax.ShapeDtypeStruct((T, A), CDT),
                   jax.ShapeDtypeStruct((T, A), CDT), jax.ShapeDtypeStruct((H, tq, kw), F32)],
        scratch_shapes=[pltpu.VMEM((BAND + T, HEAD_DIM), CDT), pltpu.VMEM((BAND + T, HEAD_DIM), CDT),
                        pltpu.VMEM((BAND + T, HEAD_DIM), F32), pltpu.VMEM((BAND + T, HEAD_DIM), F32)],
        compiler_params=_params(("parallel", "arbitrary")))(proj, proj, proj, biasm, datt)


def _retention_tables(T, H, blk):
    half = HEAD_DIM // 2
    inv = 1.0 / (ROPE_BASE ** (jnp.arange(0, HEAD_DIM, 2, dtype=F32) / HEAD_DIM))
    ang = jnp.arange(T, dtype=F32)[:, None] * inv[None, :]
    cos, sin = jnp.cos(ang), jnp.sin(ang)
    rc = jnp.concatenate([cos, cos], axis=1)
    rs = jnp.concatenate([-sin, sin], axis=1)
    assert rc.shape == (T, 2 * half)
    log_g = jnp.log(1.0 - 2.0 ** (-5.0 - jnp.arange(H, dtype=F32)))[:, None, None]
    idx = jnp.arange(blk, dtype=F32)
    n, m = idx[:, None], idx[None, :]
    same = (n // CHUNK) == (m // CHUNK)
    earlier = (m // CHUNK) < (n // CHUNK)
    dist = jnp.where(same, jnp.abs(n - m), n - m)[None]
    dmat = jnp.where((same | earlier)[None], jnp.exp(log_g * dist), 0.0)
    ones = jnp.ones((1, 1, HEAD_DIM), F32)
    qd = jnp.exp(log_g * (idx[None, :, None] + 1.0)) * ones
    kd = jnp.exp(log_g * (blk - 1.0 - idx[None, :, None])) * ones
    cd = jnp.exp(log_g * blk) * jnp.ones((1, 8, HEAD_DIM), F32)
    return rc, rs, dmat, qd, kd, cd


def _rot(v, rc, rs):
    return v * rc + pltpu.roll(v, HEAD_DIM // 2, 1) * rs


def _rot_bwd(dv, rc, rs):
    return dv * rc + pltpu.roll(dv * rs, HEAD_DIM // 2, 1)


def ret_fwd(proj, tables, A, blk):
    T = proj.shape[0]
    H = A // HEAD_DIM
    nb = T // blk
    hp = RET_HEADS_PER_STEP
    rc, rs, dmat, qd, kd, cd = tables
    scale = HEAD_DIM ** -0.5

    def body(q_ref, k_ref, v_ref, g_ref, rc_ref, rs_ref, d_ref, qd_ref, kd_ref, cd_ref,
             y_ref, o_ref, st_ref, state):
        b = pl.program_id(1)

        @pl.when(b == 0)
        def _():
            state[...] = jnp.zeros_like(state)

        c, s = rc_ref[...], rs_ref[...]
        for u in range(hp):
            cols = pl.ds(u * HEAD_DIM, HEAD_DIM)
            qs = (_rot(q_ref[:, cols].astype(F32), c, s) * scale).astype(CDT)
            kr = _rot(k_ref[:, cols].astype(F32), c, s)
            v = v_ref[:, cols]
            sb = state[u].astype(CDT)
            a = lax.dot_general(qs, kr.astype(CDT), _NT, preferred_element_type=F32) * d_ref[u]
            o = jnp.dot(a.astype(CDT), v, preferred_element_type=F32)
            o = o + jnp.dot(qs, sb, preferred_element_type=F32) * qd_ref[u]
            st_ref[u] = sb
            state[u] = state[u] * cd_ref[u, 0:1, :] + lax.dot_general(
                (kr * kd_ref[u]).astype(CDT), v, _TN, preferred_element_type=F32)
            o_ref[:, cols] = o
            on = o * lax.rsqrt(jnp.mean(o * o, axis=-1, keepdims=True) + EPS)
            g = g_ref[:, cols].astype(F32)
            y_ref[:, cols] = (g * _sigmoid(g) * on).astype(y_ref.dtype)

    w = hp * HEAD_DIM

    def pj(off):
        return pl.BlockSpec((blk, w), lambda h, i: (i, off * H // hp + h))

    tok = pl.BlockSpec((blk, HEAD_DIM), lambda h, i: (i, 0))
    out = pl.BlockSpec((blk, w), lambda h, i: (i, h))

    def per_head(r, c):
        return pl.BlockSpec((hp, r, c), lambda h, i: (h, 0, 0))

    return pl.pallas_call(
        body, name="ret_fwd", grid=(H // hp, nb),
        in_specs=[pj(3), pj(4), pj(5), pj(6), tok, tok, per_head(blk, blk),
                  per_head(blk, HEAD_DIM), per_head(blk, HEAD_DIM), per_head(8, HEAD_DIM)],
        out_specs=[out, out, pl.BlockSpec((hp, None, HEAD_DIM, HEAD_DIM), lambda h, i: (h, i, 0, 0))],
        out_shape=[jax.ShapeDtypeStruct((T, A), CDT), jax.ShapeDtypeStruct((T, A), F32),
                   jax.ShapeDtypeStruct((H, nb, HEAD_DIM, HEAD_DIM), CDT)],
        scratch_shapes=[pltpu.VMEM((hp, HEAD_DIM, HEAD_DIM), F32)],
        compiler_params=_params(("parallel", "arbitrary")))(
            proj, proj, proj, proj, rc, rs, dmat, qd, kd, cd)


def ret_bwd(proj, tables, o_raw, states, dy, A, blk):
    T = proj.shape[0]
    H = A // HEAD_DIM
    nb = T // blk
    hp = RET_HEADS_PER_STEP
    rc, rs, dmat, qd, kd, cd = tables
    scale = HEAD_DIM ** -0.5

    def body(q_ref, k_ref, v_ref, g_ref, rc_ref, rs_ref, d_ref, qd_ref, kd_ref, cd_ref,
             o_ref, st_ref, dy_ref, dq_ref, dk_ref, dv_ref, dg_ref, dstate):
        b = pl.program_id(1)

        @pl.when(b == 0)
        def _():
            dstate[...] = jnp.zeros_like(dstate)

        c, s = rc_ref[...], rs_ref[...]
        for u in range(hp):
            cols = pl.ds(u * HEAD_DIM, HEAD_DIM)
            qs = (_rot(q_ref[:, cols].astype(F32), c, s) * scale).astype(CDT)
            kr = _rot(k_ref[:, cols].astype(F32), c, s)
            krb = kr.astype(CDT)
            kdb = (kr * kd_ref[u]).astype(CDT)
            v = v_ref[:, cols]
            dmat_v = d_ref[u]
            a = lax.dot_general(qs, krb, _NT, preferred_element_type=F32) * dmat_v

            o = o_ref[:, cols]
            r = lax.rsqrt(jnp.mean(o * o, axis=-1, keepdims=True) + EPS)
            on = o * r
            g = g_ref[:, cols].astype(F32)
            sg = _sigmoid(g)
            dyv = dy_ref[:, cols].astype(F32)
            dg_ref[:, cols] = (dyv * on * (sg * (1.0 + g * (1.0 - sg)))).astype(dg_ref.dtype)
            don = dyv * (g * sg)
            do = r * (don - on * jnp.mean(don * on, axis=-1, keepdims=True))
            dob = do.astype(CDT)
            doq = (do * qd_ref[u]).astype(CDT)
            dsb = dstate[u].astype(CDT)

            dv = lax.dot_general(a.astype(CDT), dob, _TN, preferred_element_type=F32)
            dv = dv + jnp.dot(kdb, dsb, preferred_element_type=F32)
            dv_ref[:, cols] = dv.astype(dv_ref.dtype)
            dpb = (lax.dot_general(dob, v, _NT, preferred_element_type=F32) * dmat_v).astype(CDT)
            dqs = jnp.dot(dpb, krb, preferred_element_type=F32)
            dqs = dqs + lax.dot_general(doq, st_ref[u], _NT, preferred_element_type=F32)
            dkr = lax.dot_general(dpb, qs, _TN, preferred_element_type=F32)
            dkr = dkr + lax.dot_general(v, dsb, _NT, preferred_element_type=F32) * kd_ref[u]
            dstate[u] = dstate[u] * cd_ref[u, 0:1, :] + lax.dot_general(
                qs, doq, _TN, preferred_element_type=F32)
            dq_ref[:, cols] = _rot_bwd(dqs * scale, c, s).astype(dq_ref.dtype)
            dk_ref[:, cols] = _rot_bwd(dkr, c, s).astype(dk_ref.dtype)

    w = hp * HEAD_DIM

    def pj(off):
        return pl.BlockSpec((blk, w), lambda h, i: (nb - 1 - i, off * H // hp + h))

    tok = pl.BlockSpec((blk, HEAD_DIM), lambda h, i: (nb - 1 - i, 0))
    out = pl.BlockSpec((blk, w), lambda h, i: (nb - 1 - i, h))

    def per_head(r, c):
        return pl.BlockSpec((hp, r, c), lambda h, i: (h, 0, 0))

    shp = jax.ShapeDtypeStruct((T, A), CDT)
    return pl.pallas_call(
        body, name="ret_bwd", grid=(H // hp, nb),
        in_specs=[pj(3), pj(4), pj(5), pj(6), tok, tok, per_head(blk, blk),
                  per_head(blk, HEAD_DIM), per_head(blk, HEAD_DIM), per_head(8, HEAD_DIM),
                  out, pl.BlockSpec((hp, None, HEAD_DIM, HEAD_DIM), lambda h, i: (h, nb - 1 - i, 0, 0)),
                  out],
        out_specs=[out, out, out, out], out_shape=[shp, shp, shp, shp],
        scratch_shapes=[pltpu.VMEM((hp, HEAD_DIM, HEAD_DIM), F32)],
        compiler_params=_params(("parallel", "arbitrary")))(
            proj, proj, proj, proj, rc, rs, dmat, qd, kd, cd, o_raw, states, dy)


def _mesh_pos():
    return lax.axis_index("x"), lax.axis_index("y"), lax.axis_index("c")


def _flat(pos):
    return 4 * pos[0] + 2 * pos[1] + pos[2]


_HBM = pl.BlockSpec(memory_space=pltpu.HBM)


def cast_shard(name, w, rows_p, cols_p, me_arr, after=()):
    r, c = w.shape
    tr = _row_tile(math.gcd(r, rows_p), 256)
    nr = r // tr

    def body(me_ref, w_ref, *rest):
        o_ref = rest[-1]
        i = pl.program_id(0)
        o_ref[...] = jnp.zeros_like(o_ref)

        @pl.when(i < nr)
        def _():
            o_ref[:, 0:c] = w_ref[...].astype(o_ref.dtype)

    return pl.pallas_call(
        body, name=name,
        grid_spec=pltpu.PrefetchScalarGridSpec(
            num_scalar_prefetch=1, grid=(rows_p // tr,),
            in_specs=[pl.BlockSpec((tr, c), lambda i, me: (jnp.minimum(i, nr - 1), 0))] + [_ANY] * len(after),
            out_specs=pl.BlockSpec((None, tr, cols_p), lambda i, me: (me[0], i, 0))),
        out_shape=jax.ShapeDtypeStruct((N_DEV, rows_p, cols_p), CDT),
        compiler_params=_params(("arbitrary",)))(me_arr, w, *after)


def exchange_partials(arrays, after):
    n, na = len(arrays), len(after)

    def body(*refs):
        ins, outs = refs[:n], refs[n + na:2 * n + na]
        send_sems, recv_sems, local_sems = refs[2 * n + na:]
        me = _mesh_pos()
        copies, locals_ = [], []
        for t in range(n):
            cp = pltpu.make_async_copy(ins[t], outs[t].at[_flat(me)], local_sems.at[t])
            cp.start()
            locals_.append(cp)
            for k in range(1, N_DEV):
                peer = _peer(me, k)
                send = pltpu.make_async_remote_copy(
                    src_ref=ins[t], dst_ref=outs[t].at[_flat(me)],
                    send_sem=send_sems.at[t, k - 1], recv_sem=recv_sems.at[t, k - 1],
                    device_id=peer, device_id_type=pl.DeviceIdType.MESH)
                send.start()
                recv = pltpu.make_async_remote_copy(
                    src_ref=ins[t], dst_ref=outs[t].at[_flat(peer)],
                    send_sem=send_sems.at[t, k - 1], recv_sem=recv_sems.at[t, k - 1],
                    device_id=peer, device_id_type=pl.DeviceIdType.MESH)
                copies.append((send, recv))
        for send, recv in copies:
            recv.wait_recv()
        for send, recv in copies:
            send.wait_send()
        for cp in locals_:
            cp.wait()

    return pl.pallas_call(
        body, name="exchange_partials",
        in_specs=[_HBM] * n + [_ANY] * na, out_specs=[_HBM] * n,
        out_shape=[jax.ShapeDtypeStruct((N_DEV,) + a.shape, a.dtype) for a in arrays],
        scratch_shapes=[pltpu.SemaphoreType.DMA((n, 7)), pltpu.SemaphoreType.DMA((n, 7)),
                        pltpu.SemaphoreType.DMA((n,))],
        )(*arrays, *after)


_SEM = pl.BlockSpec(memory_space=pltpu.SEMAPHORE)
_ANY = pl.BlockSpec(memory_space=pl.ANY)
_EFFECT = pltpu.SideEffectType.DATAFLOW_SIDE_EFFECTING


def _peer(me, k):
    return tuple(1 - v if bit else v for v, bit in zip(me, (k >> 2, (k >> 1) & 1, k & 1)))


_CHIP_MASKS = (2, 4, 6)

_EXCHANGE_MODES = {"gather": (7, None), "gather_chips": (4, None), "forward": (3, None),
                   "scatter": (7, 7), "scatter_pair": (4, 4), "scatter_chips": (3, 3)}


def _plan(mode, bufs, n, me):
    per = _EXCHANGE_MODES[mode][0]
    sib = _peer(me, 1)
    plan = []
    for t in range(n):
        src_arr, land_arr = bufs[t], bufs[n + t] if _EXCHANGE_MODES[mode][1] else None
        if mode in ("gather", "gather_chips"):
            masks = range(1, N_DEV) if mode == "gather" else (1,) + _CHIP_MASKS
            rows = [(src_arr.at[_flat(me)], src_arr.at[_flat(me)], _peer(me, k), src_arr.at[_flat(_peer(me, k))])
                    for k in masks]
        elif mode == "forward":
            rows = [(src_arr.at[_flat(_peer(me, k))], src_arr.at[_flat(_peer(me, k))], sib,
                     src_arr.at[_flat(_peer(sib, k))]) for k in _CHIP_MASKS]
        elif mode == "scatter":
            rows = [(src_arr.at[_flat(_peer(me, k))], land_arr.at[k - 1], _peer(me, k), land_arr.at[k - 1])
                    for k in range(1, N_DEV)]
        elif mode == "scatter_pair":
            rows = [(src_arr.at[_flat(_peer(me, q + 1))], land_arr.at[qi], sib, land_arr.at[qi])
                    for qi, q in enumerate((0,) + _CHIP_MASKS)]
        else:
            assert mode == "scatter_chips"
            rows = [(src_arr.at[qi + 1], land_arr.at[qi], _peer(me, q), land_arr.at[qi])
                    for qi, q in enumerate(_CHIP_MASKS)]
        assert len(rows) == per
        plan += [(t * per + s,) + row for s, row in enumerate(rows)]
    return plan


def exchange_start(name, arrays, mode, after):
    n, na = len(arrays), len(after)
    per, slots = _EXCHANGE_MODES[mode]
    bufs = list(arrays)
    if slots:
        bufs += [lax.empty((slots,) + a.shape[1:], a.dtype) for a in arrays]
    nb = len(bufs)

    def body(*refs):
        send_sems, recv_sems = refs[nb + na], refs[nb + na + 1]
        token = refs[-1]
        for s, src, dst, dev, _ in _plan(mode, refs[:nb], n, _mesh_pos()):
            pltpu.make_async_remote_copy(
                src_ref=src, dst_ref=dst, send_sem=send_sems.at[s], recv_sem=recv_sems.at[s],
                device_id=dev, device_id_type=pl.DeviceIdType.MESH).start()
        token[...] = jnp.zeros_like(token)

    out_shape = [pltpu.SemaphoreType.DMA((n * per,)), pltpu.SemaphoreType.DMA((n * per,))]
    out_shape += [pltpu.HBM(a.shape, a.dtype) for a in bufs]
    out_shape.append(jax.ShapeDtypeStruct((8, LANE), F32))
    args = [pltpu.with_memory_space_constraint(a, pltpu.HBM) for a in bufs] + list(after)
    outs = pl.pallas_call(
        body, name=name, out_shape=out_shape,
        in_specs=[_HBM] * nb + [_ANY] * na,
        out_specs=[_SEM, _SEM] + [_HBM] * nb + [pl.BlockSpec(memory_space=pltpu.VMEM)],
        input_output_aliases={i: 2 + i for i in range(nb)},
        compiler_params=pltpu.CompilerParams(has_side_effects=_EFFECT))(*args)
    return outs[0], outs[1], list(outs[2:2 + nb]), outs[-1]


def exchange_wait(name, started, mode, after):
    send_sems, recv_sems, bufs, _ = started
    nb, na = len(bufs), len(after)
    n = nb // 2 if _EXCHANGE_MODES[mode][1] else nb

    def body(*refs):
        send_sems_ref, recv_sems_ref = refs[nb], refs[nb + 1]
        for s, src, _, dev, land in _plan(mode, refs[:nb], n, _mesh_pos()):
            cp = pltpu.make_async_remote_copy(
                src_ref=src, dst_ref=land, send_sem=send_sems_ref.at[s], recv_sem=recv_sems_ref.at[s],
                device_id=dev, device_id_type=pl.DeviceIdType.MESH)
            cp.wait_send()
            cp.wait_recv()

    outs = pl.pallas_call(
        body, name=name, out_shape=[pltpu.HBM(a.shape, a.dtype) for a in bufs],
        in_specs=[_HBM] * nb + [_SEM, _SEM] + [_ANY] * na, out_specs=[_HBM] * nb,
        input_output_aliases={i: i for i in range(nb)},
        compiler_params=pltpu.CompilerParams(has_side_effects=_EFFECT))(
            *bufs, send_sems, recv_sems, *after)
    return list(outs)


def pair_sum(name, own, landed, blocks):
    _, r, c = own.shape
    tr = _row_tile(r, 256)

    def body(idx_ref, o_ref, l_ref, s_ref):
        s_ref[...] = (o_ref[...].astype(F32) + l_ref[...].astype(F32)).astype(s_ref.dtype)

    blk = pl.BlockSpec((None, tr, c), lambda q, i, idx: (q, i, 0))
    return pl.pallas_call(
        body, name=name,
        grid_spec=pltpu.PrefetchScalarGridSpec(
            num_scalar_prefetch=1, grid=(4, r // tr),
            in_specs=[pl.BlockSpec((None, tr, c), lambda q, i, idx: (idx[q], i, 0)), blk],
            out_specs=blk),
        out_shape=jax.ShapeDtypeStruct((4, r, c), own.dtype),
        compiler_params=_params(("parallel", "parallel")))(blocks, own, landed)


def _adamw_math(w, g, m, v):
    m = ADAM_B1 * m + (1.0 - ADAM_B1) * g
    v = ADAM_B2 * v + (1.0 - ADAM_B2) * (g * g)
    m_hat = m / (1.0 - ADAM_B1 ** ADAM_STEP)
    v_hat = v / (1.0 - ADAM_B2 ** ADAM_STEP)
    delta = -ADAM_LR * (m_hat / (jnp.sqrt(v_hat) + ADAM_EPS) + ADAM_WD * w)
    return delta, m, v


def reduce_adamw(name, land, w, m, v, tr, own=None, me_arr=None):
    R, C = w.shape
    S, _, Cp = land.shape

    def body(*refs):
        if own is not None:
            _, own_ref, l_ref, w_ref, m_ref, v_ref, g_ref, d_ref, nm_ref, nv_ref = refs
            g = own_ref[:, 0:C].astype(F32)
            first = 0
        else:
            l_ref, w_ref, m_ref, v_ref, g_ref, d_ref, nm_ref, nv_ref = refs
            g = l_ref[0, :, 0:C].astype(F32)
            first = 1
        for s in range(first, S):
            g = g + l_ref[s, :, 0:C].astype(F32)
        delta, nm, nv = _adamw_math(w_ref[...], g, m_ref[...], v_ref[...])
        g_ref[...] = g
        d_ref[...] = delta
        nm_ref[...] = nm
        nv_ref[...] = nv

    shp = jax.ShapeDtypeStruct((R, C), F32)
    if own is None:
        blk = pl.BlockSpec((tr, C), lambda i: (i, 0))
        return pl.pallas_call(
            body, name=name, grid=(R // tr,),
            in_specs=[pl.BlockSpec((S, tr, Cp), lambda i: (0, i, 0)), blk, blk, blk],
            out_specs=[blk, blk, blk, blk], out_shape=[shp, shp, shp, shp],
            compiler_params=_params(("parallel",)))(land, w, m, v)
    blk = pl.BlockSpec((tr, C), lambda i, me: (i, 0))
    return pl.pallas_call(
        body, name=name,
        grid_spec=pltpu.PrefetchScalarGridSpec(
            num_scalar_prefetch=1, grid=(R // tr,),
            in_specs=[pl.BlockSpec((None, tr, Cp), lambda i, me: (me[0], i, 0)),
                      pl.BlockSpec((S, tr, Cp), lambda i, me: (0, i, 0)), blk, blk, blk],
            out_specs=[blk, blk, blk, blk]),
        out_shape=[shp, shp, shp, shp],
        compiler_params=_params(("parallel",)))(me_arr, own, land, w, m, v)


def _row_tile(r, cap):
    t = min(r, cap)
    while r % t or t % 8:
        t -= 8
    return t


def kernel(x, norm_ffn1_g, ffn1_w_gate, ffn1_w_up, ffn1_w_down, norm_mix_g, w_in, rel_bias, w_out_att, w_out_ret, w_out, norm_ffn2_g, ffn2_w_gate, ffn2_w_up, ffn2_w_down, norm_final_g, loss_target, m_norm_ffn1_g, m_ffn1_w_gate, m_ffn1_w_up, m_ffn1_w_down, m_norm_mix_g, m_w_in, m_rel_bias, m_w_out_att, m_w_out_ret, m_w_out, m_norm_ffn2_g, m_ffn2_w_gate, m_ffn2_w_up, m_ffn2_w_down, m_norm_final_g, v_norm_ffn1_g, v_ffn1_w_gate, v_ffn1_w_up, v_ffn1_w_down, v_norm_mix_g, v_w_in, v_rel_bias, v_w_out_att, v_w_out_ret, v_w_out, v_norm_ffn2_g, v_ffn2_w_gate, v_ffn2_w_up, v_ffn2_w_down, v_norm_final_g):
    T, D = x.shape[1], x.shape[2]
    A = w_out_att.shape[1]
    H = A // HEAD_DIM
    nf = ffn1_w_gate.shape[2]
    nfp = _round_up(nf, LANE)
    nin = w_in.shape[2]
    nd = w_out.shape[1]
    assert nin % LANE == 0 and nd % LANE == 0 and (7 * A) % nd == 0 and T % ATT_TQ == 0
    tm = min(512, T)
    tw = min(2048, T)
    tw_in = min(1024, T)
    tn = min(256, T)
    x0 = x[0]
    tgt = loss_target[0]

    me_arr = (4 * lax.axis_index("x") + 2 * lax.axis_index("y") + lax.axis_index("c")).astype(jnp.int32).reshape(1)

    def slot(tag, w, after, rows_p=None):
        return cast_shard("cast_" + tag, w[0], rows_p or w.shape[1], w.shape[2], me_arr, after)

    def slot_t(tag, w, after):
        return cast_shard("cast_" + tag, jnp.transpose(w[0]), nfp, w.shape[1], me_arr, after)

    groups = [("wg1", [("wg1", ffn1_w_gate, slot_t)], True), ("wu1", [("wu1", ffn1_w_up, slot_t)], True),
              ("wd1", [("wd1", ffn1_w_down, functools.partial(slot, rows_p=nfp))], True),
              ("win", [("win", w_in, slot)], True),
              ("wout", [("woa", w_out_att, slot), ("wor", w_out_ret, slot), ("wo", w_out, slot)], False),
              ("wgu2", [("wg2", ffn2_w_gate, slot_t), ("wu2", ffn2_w_up, slot_t)], False),
              ("wd2", [("wd2", ffn2_w_down, functools.partial(slot, rows_p=nfp))], False)]
    ag_started = {}
    order = []
    for tag, members, two_level in groups:
        mode = "gather_chips" if two_level else "gather"
        started = exchange_start("ag_start_" + tag, [make(nm, w, order) for nm, w, make in members], mode, order)
        ag_started[tag] = (started, mode)
        order = [started[3]]

    def gathered(tag, after):
        started, mode = ag_started[tag]
        got = exchange_wait("ag_wait_" + tag, started, mode, [after])
        if mode == "gather":
            return got
        passing = exchange_start("ag_pass_" + tag, got, "forward", [])
        return exchange_wait("ag_passed_" + tag, passing, "forward", [passing[3]])

    xi, yi, ci = lax.axis_index("x"), lax.axis_index("y"), lax.axis_index("c")
    my_side = jnp.stack([4 * (1 - xi if q & 4 else xi) + 2 * (1 - yi if q & 2 else yi) + ci
                         for q in (0,) + _CHIP_MASKS]).astype(jnp.int32)
    first_block = jnp.zeros((1,), jnp.int32)

    def swiglu(prods, _):
        a, b = prods
        return a, b, a * _sigmoid(a) * b

    def ffn_fwd(tag, xin, g, get_wgu, get_wd, after=()):
        h = rmsnorm_fwd(tag + "_norm", xin, g, tn, after)
        if isinstance(get_wgu, tuple):
            Wg, = get_wgu[0](h)
            a, = mm_block(tag + "_gate", T, tm, [(h, "full", D, 0)], [(Wg, 0, True)], [], [("3d", nfp, CDT)],
                          lambda p, _: p)
            Wu, = get_wgu[1](a)
            b, mid = mm_block(tag + "_up", T, tm, [(h, "full", D, 0)], [(Wu, 0, True)], [(a, "3d", nfp, 0)],
                              [("3d", nfp, CDT)] * 2, lambda p, ex: (p[0], ex[0] * _sigmoid(ex[0]) * p[0]))
        else:
            Wg, Wu = get_wgu(h)
            a, b, mid = mm_block(tag + "_up", T, tm, [(h, "full", D, 0)], [(Wg, 0, True), (Wu, 0, True)],
                                 [], [("3d", nfp, CDT)] * 3, swiglu)
        Wd, = get_wd(mid)
        xo = mm_reduce_j(tag + "_down", T, tm, [(mid, "3d", nfp, 0, Wd, False)], D, F32, res=xin, scale=0.5,
                         jstep=2)
        return h, a, b, mid, xo, (Wg, Wu, Wd)

    h1, a1, b1, mid1, x1, (Wg1, Wu1, Wd1) = ffn_fwd(
        "ffn1", x0, norm_ffn1_g, (lambda h: gathered("wg1", h), lambda a: gathered("wu1", a)),
        lambda mid: gathered("wd1", mid), after=order)
    h2 = rmsnorm_fwd("mix_norm", x1, norm_mix_g, tn)
    Win, = gathered("win", h2)
    proj, = mm_block("in_proj", T, tm, [(h2, "full", D, 0)], [(Win, 0, False)], [], [("col", nin, CDT)],
                     lambda p, _: p)
    biasm = attn_bias(rel_bias[0], ATT_TQ)
    att = attn_fwd(proj, biasm, A, ATT_TQ)
    tables = _retention_tables(T, H, RET_BLK)
    retg, ret_raw, states = ret_fwd(proj, tables, A, RET_BLK)
    Woa, Wor, Wo = gathered("wout", retg)
    goff = 7 * A // nd

    def merge(prods, ex):
        ba, br = prods
        ga, gr = ex
        return ba, br, _sigmoid(ga) * ba + _sigmoid(gr) * br

    ba, br, merged = mm_block(
        "branches", T, tm, [(att, "full", A, 0), (retg, "full", A, 0)], [(Woa, 0, False), (Wor, 1, False)],
        [(proj, "col", nd, goff), (proj, "col", nd, goff + N_DEV)], [("col", nd, CDT)] * 3, merge, order="ij")
    x2 = mm_reduce_j("out_proj", T, tm, [(merged, "col", nd, 0, Wo, False)], D, F32, res=x1, scale=1.0,
                     jstep=N_DEV)
    h3, a2, b2, mid2, x3, (Wg2, Wu2, Wd2) = ffn_fwd(
        "ffn2", x2, norm_ffn2_g, lambda h: gathered("wgu2", h), lambda mid: gathered("wd2", mid))

    dx3, dx3h, dgf, loss_part = loss_head(x3, norm_final_g.reshape(1, D), tgt, tn)

    def swiglu_bwd(prods, ex):
        dm, = prods
        a, b = ex
        sg = _sigmoid(a)
        return dm * b * (sg * (1.0 + a * (1.0 - sg))), dm * (a * sg)

    def ffn_bwd(tag, dxh, h, a, b, mid, Wg, Wu, Wd, two_level=False):
        da, db = mm_block(tag + "_down_bwd", T, tm, [(dxh, "full", D, 0)], [(Wd, 0, True)],
                          [(a, "3d", nfp, 0), (b, "3d", nfp, 0)], [("3d", nfp, CDT)] * 2, swiglu_bwd)

        def up_bwd(after):
            return mm_reduce_j(tag + "_up_bwd", T, tm,
                               [(da, "3d", nfp, 0, Wg, False), (db, "3d", nfp, 0, Wu, False)],
                               D, F32, after=after, jstep=2)

        if not two_level:
            dWd = mm_reduce_i(tag + "_dwd", T, tw, (mid, "3d", nfp, 0), (dxh, "full", D, 0))
            dWg = mm_reduce_i(tag + "_dwg", T, tw, (da, "3d", nfp, 0), (h, "full", D, 0))
            dWu = mm_reduce_i(tag + "_dwu", T, tw, (db, "3d", nfp, 0), (h, "full", D, 0))
            sent = exchange_start("rs_start_" + tag, [dWg, dWu, dWd], "scatter", [])
            return up_bwd([sent[3]]), sent

        def swap(nm, grad, after):
            return exchange_start("rs_pair_start_%s_%s" % (tag, nm), [grad], "scatter_pair", after)

        def to_chips(nm, swapping, after):
            own, landed = exchange_wait("rs_pair_wait_%s_%s" % (tag, nm), swapping, "scatter_pair", [after])
            sums = pair_sum("%s_pair_sum_%s" % (tag, nm), own, landed, my_side)
            return exchange_start("rs_chips_start_%s_%s" % (tag, nm), [sums], "scatter_chips", [])

        dWd = mm_reduce_i(tag + "_dwd", T, tw, (mid, "3d", nfp, 0), (dxh, "full", D, 0))
        swap_d = swap("d", dWd, [])
        dWg = mm_reduce_i(tag + "_dwg", T, tw, (da, "3d", nfp, 0), (h, "full", D, 0), after=[swap_d[3]])
        sent_d = to_chips("d", swap_d, dWg)
        swap_g = swap("g", dWg, [sent_d[3]])
        dWu = mm_reduce_i(tag + "_dwu", T, tw, (db, "3d", nfp, 0), (h, "full", D, 0), after=[swap_g[3]])
        sent_g = to_chips("g", swap_g, dWu)
        swap_u = swap("u", dWu, [sent_g[3]])
        dh = up_bwd([swap_u[3]])
        sent_u = to_chips("u", swap_u, dh)
        return dh, [sent_g, sent_u, sent_d]

    dh3, sent_ffn2 = ffn_bwd("ffn2", dx3h, h3, a2, b2, mid2, Wg2, Wu2, Wd2)
    dx2, dx2c, dg2 = rmsnorm_bwd("ffn2_norm_bwd", x2, norm_ffn2_g, dh3, dx3, 1.0, tn)

    def merge_bwd(prods, ex):
        dmg, = prods
        ba_, br_, ga, gr = ex
        sa, sr = _sigmoid(ga), _sigmoid(gr)
        return dmg * sa, dmg * sr, dmg * ba_ * sa * (1.0 - sa), dmg * br_ * sr * (1.0 - sr)

    dba, dbr, dga, dgr = mm_block(
        "out_proj_bwd", T, tm, [(dx2c, "full", D, 0)], [(Wo, 0, True)],
        [(ba, "col", nd, 0), (br, "col", nd, 0), (proj, "col", nd, goff), (proj, "col", nd, goff + N_DEV)],
        [("col", nd, CDT)] * 4, merge_bwd, order="ij")
    dWo = mm_reduce_i("dwo", T, tw, (merged, "col", nd, 0), (dx2c, "full", D, 0))
    dWoa = mm_reduce_i("dwoa", T, tw, (att, "full", A, 0), (dba, "col", nd, 0))
    dWor = mm_reduce_i("dwor", T, tw, (retg, "full", A, 0), (dbr, "col", nd, 0))
    sent_mix = exchange_start("rs_start_mix", [dWoa, dWor, dWo], "scatter", [])
    datt = mm_reduce_j("att_out_bwd", T, tm, [(dba, "col", nd, 0, Woa, True)], A, CDT, after=[sent_mix[3]],
                       jstep=N_DEV)
    dretg = mm_reduce_j("ret_out_bwd", T, tm, [(dbr, "col", nd, 0, Wor, True)], A, CDT, jstep=N_DEV)
    dq_r, dk_r, dv_r, dg_r = ret_bwd(proj, tables, ret_raw, states, dretg, A, RET_BLK)
    dq_a, dk_a, dv_a, dst = attn_bwd(proj, biasm, datt, A, ATT_TQ)
    dbias = jnp.pad(attn_bias_grad(dst, ATT_TQ), ((0, 0), (0, N_REL_PAD - N_REL)))
    dproj = jnp.concatenate([dq_a, dk_a, dv_a, dq_r, dk_r, dv_r, dg_r, dga, dgr], axis=1)
    dWin = mm_reduce_i("dwin", T, tw_in, (h2, "full", D, 0), (dproj, "col", nin, 0))
    sent_win = exchange_start("rs_start_win", [dWin], "scatter", [])
    dh2 = mm_reduce_j("in_proj_bwd", T, tm, [(dproj, "col", nin, 0, Win, True)], D, F32, after=[sent_win[3]],
                      jstep=2)
    dx1, dx1h, dgm = rmsnorm_bwd("mix_norm_bwd", x1, norm_mix_g, dh2, dx2, 0.5, tn)
    dh1, sent_ffn1 = ffn_bwd("ffn1", dx1h, h1, a1, b1, mid1, Wg1, Wu1, Wd1, two_level=True)
    grad_x, _, dg1 = rmsnorm_bwd("ffn1_norm_bwd", x0, norm_ffn1_g, dh1, dx1, 1.0, tn, after=[sent_ffn1[1][3]])

    dgains = jnp.concatenate([dg1, dgm, dg2, dgf, jnp.zeros((4, D), F32)], axis=0)

    def upd(name, own, land, w, m, v, own_block=me_arr, transposed=False):
        w2, m2, v2 = [jnp.transpose(t[0]) if transposed else t[0] for t in (w, m, v)]
        outs = reduce_adamw(name, land, w2, m2, v2, _row_tile(w2.shape[0], 256), own=own, me_arr=own_block)
        return [jnp.transpose(o)[None] if transposed else o[None] for o in outs]

    res = {}
    oWg2, oWu2, oWd2, lWg2, lWu2, lWd2 = exchange_wait("rs_wait_ffn2", sent_ffn2, "scatter", [grad_x])
    res["ffn2_w_gate"] = upd("adamw_wg2", oWg2, lWg2, ffn2_w_gate, m_ffn2_w_gate, v_ffn2_w_gate, transposed=True)
    res["ffn2_w_up"] = upd("adamw_wu2", oWu2, lWu2, ffn2_w_up, m_ffn2_w_up, v_ffn2_w_up, transposed=True)
    res["ffn2_w_down"] = upd("adamw_wd2", oWd2, lWd2, ffn2_w_down, m_ffn2_w_down, v_ffn2_w_down)
    oWoa, oWor, oWo, lWoa, lWor, lWo = exchange_wait("rs_wait_mix", sent_mix, "scatter", [res["ffn2_w_down"][1]])
    res["w_out_att"] = upd("adamw_woa", oWoa, lWoa, w_out_att, m_w_out_att, v_w_out_att)
    res["w_out_ret"] = upd("adamw_wor", oWor, lWor, w_out_ret, m_w_out_ret, v_w_out_ret)
    res["w_out"] = upd("adamw_wo", oWo, lWo, w_out, m_w_out, v_w_out)
    oWin, lWin = exchange_wait("rs_wait_win", sent_win, "scatter", [res["w_out"][1]])
    res["w_in"] = upd("adamw_win", oWin, lWin, w_in, m_w_in, v_w_in)
    lgains, lbias = exchange_partials([dgains, dbias], [res["w_in"][1]])
    (oWg1, lWg1), (oWu1, lWu1), (oWd1, lWd1) = [
        exchange_wait("rs_wait_ffn1_" + nm, started, "scatter_chips", [lgains])
        for nm, started in zip("gud", sent_ffn1)]
    res["ffn1_w_gate"] = upd("adamw_wg1", oWg1, lWg1, ffn1_w_gate, m_ffn1_w_gate, v_ffn1_w_gate, first_block, transposed=True)
    res["ffn1_w_up"] = upd("adamw_wu1", oWu1, lWu1, ffn1_w_up, m_ffn1_w_up, v_ffn1_w_up, first_block, transposed=True)
    res["ffn1_w_down"] = upd("adamw_wd1", oWd1, lWd1, ffn1_w_down, m_ffn1_w_down, v_ffn1_w_down, first_block)

    def stack_gains(a, b, c_, d):
        return jnp.concatenate([a, b, c_, d.reshape(1, D), jnp.zeros((4, D), F32)], axis=0)

    gw = stack_gains(norm_ffn1_g, norm_mix_g, norm_ffn2_g, norm_final_g)
    gm = stack_gains(m_norm_ffn1_g, m_norm_mix_g, m_norm_ffn2_g, m_norm_final_g)
    gv = stack_gains(v_norm_ffn1_g, v_norm_mix_g, v_norm_ffn2_g, v_norm_final_g)
    gains = reduce_adamw("adamw_gains", lgains, gw, gm, gv, 8)

    def padb(t):
        return jnp.pad(t[0], ((0, 0), (0, N_REL_PAD - N_REL)))

    bias = [o[:, :N_REL][None] for o in
            reduce_adamw("adamw_bias", lbias, padb(rel_bias), padb(m_rel_bias), padb(v_rel_bias), H)]
    res["norm_ffn1_g"] = [o[0:1] for o in gains]
    res["norm_mix_g"] = [o[1:2] for o in gains]
    res["norm_ffn2_g"] = [o[2:3] for o in gains]
    res["norm_final_g"] = [o[3] for o in gains]
    res["rel_bias"] = bias

    loss = lax.psum(loss_part[0, 0], MESH_AXES)
    names = ["norm_ffn1_g", "ffn1_w_gate", "ffn1_w_up", "ffn1_w_down", "norm_mix_g", "w_in", "rel_bias",
             "w_out_att", "w_out_ret", "w_out", "norm_ffn2_g", "ffn2_w_gate", "ffn2_w_up", "ffn2_w_down",
             "norm_final_g"]
    out = [loss, grad_x[None]]
    for k in range(4):
        out += [res[nm][k] for nm in names]
    return tuple(out)
```

```python
import functools
import math

import jax
import jax.numpy as jnp
import numpy as np
from jax import lax
from jax.experimental import pallas as pl
from jax.experimental.pallas import tpu as pltpu

F32 = jnp.float32
CDT = jnp.bfloat16

N_DEV = 8
CHUNK = 64
N_PREV_CHUNKS = 8
BAND = N_PREV_CHUNKS * CHUNK
HEAD_DIM = 128
MAX_REL_DIST = 128
N_REL = 2 * MAX_REL_DIST + 1
N_REL_PAD = 384
ROPE_BASE = 10000.0
EPS = 1e-6
NEG = -1e30
LANE = 128
ATT_TQ = 256
RET_BLK = 256
RET_HEADS_PER_STEP = 4
VMEM_LIMIT = 48 * 1024 * 1024

ADAM_LR = 0.001
ADAM_B1 = 0.9
ADAM_B2 = 0.999
ADAM_EPS = 1e-08
ADAM_WD = 0.01
ADAM_STEP = 10

MESH_AXES = ("x", "y", "c")
_NT = (((1,), (1,)), ((), ()))
_TN = (((0,), (0,)), ((), ()))


def _round_up(v, m):
    return (v + m - 1) // m * m


def _params(sem=None):
    return pltpu.CompilerParams(dimension_semantics=sem, vmem_limit_bytes=VMEM_LIMIT)


def _sigmoid(v):
    return 0.5 * (jnp.tanh(0.5 * v) + 1.0)


def _bspec(kind, tm, w, off, order, jmap=lambda j: j):
    def wrap(f):
        if order == "ji":
            return lambda j, i: f(i, jmap(j))
        return lambda i, j: f(i, jmap(j))
    if kind == "full":
        return pl.BlockSpec((tm, w), wrap(lambda i, j: (i, 0)))
    if kind == "col":
        return pl.BlockSpec((tm, w), wrap(lambda i, j: (i, j + off)))
    assert kind == "3d"
    return pl.BlockSpec((None, tm, w), wrap(lambda i, j: (j, i, 0)))


def _wspec(w, order, jmap=lambda j: j):
    if order == "ji":
        return pl.BlockSpec((None,) + w.shape[1:], lambda j, i: (jmap(j), 0, 0))
    return pl.BlockSpec((None,) + w.shape[1:], lambda i, j: (jmap(j), 0, 0))


def _width(arr, kind, w):
    return arr.shape[-1] if kind in ("full", "3d") else w


def mm_block(name, T, tm, lhs, wts, extras, outs, epilogue, order="ji"):
    nl, nw, ne = len(lhs), len(wts), len(extras)
    ni = T // tm

    def body(*refs):
        l = refs[:nl]
        w = refs[nl:nl + nw]
        e = refs[nl + nw:nl + nw + ne]
        o = refs[nl + nw + ne:]
        prods = []
        for k, (_, li, tr) in enumerate(wts):
            a = l[li][...]
            if tr:
                prods.append(lax.dot_general(a, w[k][...], _NT, preferred_element_type=F32))
            else:
                prods.append(jnp.dot(a, w[k][...], preferred_element_type=F32))
        res = epilogue(prods, [r[...].astype(F32) for r in e])
        for r, val in zip(o, res):
            r[...] = val.astype(r.dtype)

    in_specs = [_bspec(k, tm, _width(a, k, w), off, order) for (a, k, w, off) in lhs]
    in_specs += [_wspec(w, order) for (w, _, _) in wts]
    in_specs += [_bspec(k, tm, _width(a, k, w), off, order) for (a, k, w, off) in extras]
    out_specs, out_shape = [], []
    for (kind, w, dt) in outs:
        out_specs.append(_bspec(kind, tm, w, 0, order))
        if kind == "3d":
            out_shape.append(jax.ShapeDtypeStruct((N_DEV, T, w), dt))
        else:
            out_shape.append(jax.ShapeDtypeStruct((T, N_DEV * w), dt))
    args = [a for (a, _, _, _) in lhs] + [w for (w, _, _) in wts] + [a for (a, _, _, _) in extras]
    return pl.pallas_call(
        body, name=name, grid=(N_DEV, ni) if order == "ji" else (ni, N_DEV), in_specs=in_specs,
        out_specs=out_specs, out_shape=out_shape, compiler_params=_params(("parallel", "parallel")))(*args)


def mm_reduce_j(name, T, tm, pairs, out_w, out_dtype, res=None, scale=1.0, after=(), jstep=1):
    terms = [(p, u) for u in range(jstep) for p in pairs]
    nt = len(terms)
    nj = N_DEV // jstep
    ni = T // tm

    def body(*refs):
        xs = refs[:nt]
        ws = refs[nt:2 * nt]
        rest = refs[2 * nt:len(refs) - 2 - len(after)] + refs[len(refs) - 2:]
        if res is not None:
            res_ref, o_ref, acc = rest
        else:
            o_ref, acc = rest
        j = pl.program_id(1)

        @pl.when(j == 0)
        def _():
            acc[...] = jnp.zeros_like(acc)

        tot = None
        for k, (p, _) in enumerate(terms):
            if p[5]:
                d = lax.dot_general(xs[k][...], ws[k][...], _NT, preferred_element_type=F32)
            else:
                d = jnp.dot(xs[k][...], ws[k][...], preferred_element_type=F32)
            tot = d if tot is None else tot + d
        acc[...] += tot

        @pl.when(j == nj - 1)
        def _():
            if res is not None:
                o_ref[...] = (res_ref[...] + scale * acc[...]).astype(o_ref.dtype)
            else:
                o_ref[...] = acc[...].astype(o_ref.dtype)

    def jmap(u):
        return lambda j: j * jstep + u

    in_specs = [_bspec(p[1], tm, _width(p[0], p[1], p[2]), p[3], "ij", jmap(u)) for (p, u) in terms]
    in_specs += [_wspec(p[4], "ij", jmap(u)) for (p, u) in terms]
    args = [p[0] for (p, _) in terms] + [p[4] for (p, _) in terms]
    if res is not None:
        in_specs.append(pl.BlockSpec((tm, out_w), lambda i, j: (i, 0)))
        args.append(res)
    in_specs += [_ANY] * len(after)
    args += list(after)
    return pl.pallas_call(
        body, name=name, grid=(ni, nj), in_specs=in_specs,
        out_specs=pl.BlockSpec((tm, out_w), lambda i, j: (i, 0)),
        out_shape=jax.ShapeDtypeStruct((T, out_w), out_dtype),
        scratch_shapes=[pltpu.VMEM((tm, out_w), F32)],
        compiler_params=_params(("parallel", "arbitrary")))(*args)


def mm_reduce_i(name, T, tm, a, b, after=()):
    ni = T // tm
    rows = _width(a[0], a[1], a[2])
    cols = _width(b[0], b[1], b[2])

    def body(a_ref, b_ref, *rest):
        o_ref, acc = rest[len(after):]
        i = pl.program_id(1)

        @pl.when(i == 0)
        def _():
            acc[...] = jnp.zeros_like(acc)

        acc[...] += lax.dot_general(a_ref[...], b_ref[...], _TN, preferred_element_type=F32)

        @pl.when(i == ni - 1)
        def _():
            o_ref[...] = acc[...].astype(o_ref.dtype)

    return pl.pallas_call(
        body, name=name, grid=(N_DEV, ni),
        in_specs=[_bspec(a[1], tm, rows, a[3], "ji"), _bspec(b[1], tm, cols, b[3], "ji")] + [_ANY] * len(after),
        out_specs=pl.BlockSpec((None, rows, cols), lambda j, i: (j, 0, 0)),
        out_shape=jax.ShapeDtypeStruct((N_DEV, rows, cols), CDT),
        scratch_shapes=[pltpu.VMEM((rows, cols), F32)],
        compiler_params=_params(("parallel", "arbitrary")))(a[0], b[0], *after)


def _rms_bwd_math(xv, g, dy):
    r = lax.rsqrt(jnp.mean(xv * xv, axis=-1, keepdims=True) + EPS)
    xn = xv * r
    dxn = dy * g
    dx = r * (dxn - xn * jnp.mean(dxn * xn, axis=-1, keepdims=True))
    dg = jnp.sum(dy * xn, axis=0, keepdims=True)
    return dx, dg


def rmsnorm_fwd(name, x, g, tm, after=()):
    T, D = x.shape

    def body(x_ref, g_ref, *rest):
        o_ref = rest[-1]
        xv = x_ref[...]
        r = lax.rsqrt(jnp.mean(xv * xv, axis=-1, keepdims=True) + EPS)
        o_ref[...] = (xv * r * g_ref[...]).astype(o_ref.dtype)

    return pl.pallas_call(
        body, name=name, grid=(T // tm,),
        in_specs=[pl.BlockSpec((tm, D), lambda i: (i, 0)), pl.BlockSpec((1, D), lambda i: (0, 0))]
        + [_ANY] * len(after),
        out_specs=pl.BlockSpec((tm, D), lambda i: (i, 0)),
        out_shape=jax.ShapeDtypeStruct((T, D), CDT),
        compiler_params=_params(("parallel",)))(x, g, *after)


def rmsnorm_bwd(name, x, g, dh, dres, cscale, tm, after=()):
    T, D = x.shape

    def body(x_ref, g_ref, dh_ref, dres_ref, *rest):
        dx_ref, dxc_ref, dg_ref = rest[len(after):]
        i = pl.program_id(0)
        dx, dg = _rms_bwd_math(x_ref[...], g_ref[...], dh_ref[...].astype(F32))
        dx = dres_ref[...] + dx
        dx_ref[...] = dx
        dxc_ref[...] = (cscale * dx).astype(dxc_ref.dtype)

        @pl.when(i == 0)
        def _():
            dg_ref[...] = jnp.zeros_like(dg_ref)

        dg_ref[...] += dg

    row = pl.BlockSpec((tm, D), lambda i: (i, 0))
    vec = pl.BlockSpec((1, D), lambda i: (0, 0))
    return pl.pallas_call(
        body, name=name, grid=(T // tm,), in_specs=[row, vec, row, row] + [_ANY] * len(after),
        out_specs=[row, row, vec],
        out_shape=[jax.ShapeDtypeStruct((T, D), F32), jax.ShapeDtypeStruct((T, D), CDT),
                   jax.ShapeDtypeStruct((1, D), F32)],
        compiler_params=_params(("arbitrary",)))(x, g, dh, dres, *after)


def loss_head(x, g, tgt, tm):
    T, D = x.shape

    def body(x_ref, g_ref, t_ref, dx_ref, dxc_ref, dg_ref, loss_ref):
        i = pl.program_id(0)
        xv = x_ref[...]
        gv = g_ref[...]
        r = lax.rsqrt(jnp.mean(xv * xv, axis=-1, keepdims=True) + EPS)
        err = xv * r * gv - t_ref[...]
        part = jnp.sum(jnp.mean(err * err, axis=-1, keepdims=True), axis=0, keepdims=True)
        dx, dg = _rms_bwd_math(xv, gv, err / D)
        dx_ref[...] = dx
        dxc_ref[...] = (0.5 * dx).astype(dxc_ref.dtype)

        @pl.when(i == 0)
        def _():
            dg_ref[...] = jnp.zeros_like(dg_ref)
            loss_ref[...] = jnp.zeros_like(loss_ref)

        dg_ref[...] += dg
        loss_ref[...] += jnp.broadcast_to(0.5 * part, loss_ref.shape)

    row = pl.BlockSpec((tm, D), lambda i: (i, 0))
    vec = pl.BlockSpec((1, D), lambda i: (0, 0))
    return pl.pallas_call(
        body, name="loss_head", grid=(T // tm,), in_specs=[row, vec, row],
        out_specs=[row, row, vec, pl.BlockSpec((1, LANE), lambda i: (0, 0))],
        out_shape=[jax.ShapeDtypeStruct((T, D), F32), jax.ShapeDtypeStruct((T, D), CDT),
                   jax.ShapeDtypeStruct((1, D), F32), jax.ShapeDtypeStruct((1, LANE), F32)],
        compiler_params=_params(("arbitrary",)))(x, g, tgt)


def _skew_rows(z, left):
    tq, kw = z.shape
    row = lax.broadcasted_iota(jnp.int32, (tq, kw), 0)
    s = 1
    while s < tq:
        z = jnp.where((row & s) != 0, pltpu.roll(z, kw - s if left else s, 1), z)
        s *= 2
    return z


REL_HI = BAND + MAX_REL_DIST
REL_LO = BAND - MAX_REL_DIST


def attn_bias(rel_bias, tq):
    H = rel_bias.shape[0]
    kw = BAND + tq
    by_skew = jnp.concatenate(
        [jnp.broadcast_to(rel_bias[:, N_REL - 1:], (H, REL_LO)), rel_bias[:, ::-1],
         jnp.broadcast_to(rel_bias[:, :1], (H, kw - REL_HI - 1))], axis=1).reshape(H, 1, kw)

    def body(t_ref, o_ref):
        t = t_ref[...]
        qi = lax.broadcasted_iota(jnp.int32, (tq, kw), 0)
        kj = lax.broadcasted_iota(jnp.int32, (tq, kw), 1)
        b = _skew_rows(jnp.broadcast_to(t, (tq, kw)), left=False)
        b = jnp.where(kj < qi, t[:, 0:1], b)
        qc = qi // CHUNK
        kc = kj // CHUNK - N_PREV_CHUNKS
        valid = (kc <= qc) & (kc >= qc - N_PREV_CHUNKS)
        o_ref[...] = jnp.where(valid, b, NEG)

    return pl.pallas_call(
        body, name="attn_bias", grid=(H,),
        in_specs=[pl.BlockSpec((None, 1, kw), lambda h: (h, 0, 0))],
        out_specs=pl.BlockSpec((None, tq, kw), lambda h: (h, 0, 0)),
        out_shape=jax.ShapeDtypeStruct((H, tq, kw), F32),
        compiler_params=_params(("parallel",)))(by_skew)


def attn_bias_grad(dst, tq):
    H = dst.shape[0]
    kw = BAND + tq

    def body(d_ref, o_ref):
        z = _skew_rows(d_ref[...], left=True)
        qi = lax.broadcasted_iota(jnp.int32, (tq, kw), 0)
        kj = lax.broadcasted_iota(jnp.int32, (tq, kw), 1)
        wrapped = kj + qi >= kw
        c = jnp.sum(jnp.where(wrapped, 0.0, z), axis=0, keepdims=True)
        cw = jnp.sum(jnp.sum(jnp.where(wrapped, z, 0.0), axis=0, keepdims=True), axis=1, keepdims=True)
        lane = lax.broadcasted_iota(jnp.int32, (1, kw), 1)
        ahead = jnp.sum(jnp.where(lane >= REL_HI, c, 0.0), axis=1, keepdims=True)
        behind = jnp.sum(jnp.where(lane <= REL_LO, c, 0.0), axis=1, keepdims=True) + cw
        o_ref[...] = jnp.where(lane == REL_HI, ahead, jnp.where(lane == REL_LO, behind, c))

    by_skew = pl.pallas_call(
        body, name="attn_bias_grad", grid=(H,),
        in_specs=[pl.BlockSpec((None, tq, kw), lambda h: (h, 0, 0))],
        out_specs=pl.BlockSpec((None, 1, kw), lambda h: (h, 0, 0)),
        out_shape=jax.ShapeDtypeStruct((H, 1, kw), F32),
        compiler_params=_params(("parallel",)))(dst)
    return by_skew[:, 0, REL_LO:REL_HI + 1][:, ::-1]


def _attn_scores(q, kpad, bm_ref, start, kw):
    k = kpad[pl.ds(start, kw), :]
    s = lax.dot_general(q, k, _NT, preferred_element_type=F32) * (HEAD_DIM ** -0.5) + bm_ref[...]
    col = lax.broadcasted_iota(jnp.int32, s.shape, 1)
    s = jnp.where(col < BAND - start, NEG, s)
    m = jnp.max(s, axis=-1, keepdims=True)
    e = jnp.exp(s - m)
    return e, 1.0 / jnp.sum(e, axis=-1, keepdims=True), k


def _fill_padded(pad_ref, src_ref, T):
    pad_ref[pl.ds(0, BAND), :] = jnp.zeros((BAND, HEAD_DIM), pad_ref.dtype)
    pad_ref[pl.ds(BAND, T), :] = src_ref[...]


def attn_fwd(proj, biasm, A, tq):
    T = proj.shape[0]
    H = A // HEAD_DIM
    kw = BAND + tq

    def body(q_ref, k_ref, v_ref, bm_ref, o_ref, kpad, vpad):
        qi = pl.program_id(1)

        @pl.when(qi == 0)
        def _():
            _fill_padded(kpad, k_ref, T)
            _fill_padded(vpad, v_ref, T)

        start = pl.multiple_of(qi * tq, tq)
        e, rinv, _ = _attn_scores(q_ref[...], kpad, bm_ref, start, kw)
        v = vpad[pl.ds(start, kw), :]
        o_ref[...] = (jnp.dot(e.astype(CDT), v, preferred_element_type=F32) * rinv).astype(o_ref.dtype)

    return pl.pallas_call(
        body, name="attn_fwd", grid=(H, T // tq),
        in_specs=[pl.BlockSpec((tq, HEAD_DIM), lambda h, i: (i, h)),
                  pl.BlockSpec((T, HEAD_DIM), lambda h, i: (0, H + h)),
                  pl.BlockSpec((T, HEAD_DIM), lambda h, i: (0, 2 * H + h)),
                  pl.BlockSpec((None, tq, kw), lambda h, i: (h, 0, 0))],
        out_specs=pl.BlockSpec((tq, HEAD_DIM), lambda h, i: (i, h)),
        out_shape=jax.ShapeDtypeStruct((T, A), CDT),
        scratch_shapes=[pltpu.VMEM((BAND + T, HEAD_DIM), CDT), pltpu.VMEM((BAND + T, HEAD_DIM), CDT)],
        compiler_params=_params(("parallel", "arbitrary")))(proj, proj, proj, biasm)


def attn_bwd(proj, biasm, datt, A, tq):
    T = proj.shape[0]
    H = A // HEAD_DIM
    kw = BAND + tq
    nq = T // tq
    scale = HEAD_DIM ** -0.5

    def body(q_ref, k_ref, v_ref, bm_ref, do_ref, dq_ref, dk_ref, dv_ref, dst_ref,
             kpad, vpad, dkacc, dvacc):
        qi = pl.program_id(1)

        @pl.when(qi == 0)
        def _():
            _fill_padded(kpad, k_ref, T)
            _fill_padded(vpad, v_ref, T)
            dkacc[...] = jnp.zeros_like(dkacc)
            dvacc[...] = jnp.zeros_like(dvacc)
            dst_ref[...] = jnp.zeros_like(dst_ref)

        start = pl.multiple_of(qi * tq, tq)
        q = q_ref[...]
        e, rinv, k = _attn_scores(q, kpad, bm_ref, start, kw)
        p = e * rinv
        v = vpad[pl.ds(start, kw), :]
        do = do_ref[...]
        dp = lax.dot_general(do, v, _NT, preferred_element_type=F32)
        ds = p * (dp - jnp.sum(dp * p, axis=-1, keepdims=True))
        dst_ref[...] += ds
        dsb = ds.astype(CDT)
        dq_ref[...] = (jnp.dot(dsb, k, preferred_element_type=F32) * scale).astype(dq_ref.dtype)
        dkacc[pl.ds(start, kw), :] += lax.dot_general(dsb, q, _TN, preferred_element_type=F32) * scale
        dvacc[pl.ds(start, kw), :] += lax.dot_general(p.astype(CDT), do, _TN, preferred_element_type=F32)

        @pl.when(qi == nq - 1)
        def _():
            dk_ref[...] = dkacc[pl.ds(BAND, T), :].astype(dk_ref.dtype)
            dv_ref[...] = dvacc[pl.ds(BAND, T), :].astype(dv_ref.dtype)

    blk = pl.BlockSpec((tq, HEAD_DIM), lambda h, i: (i, h))
    col = pl.BlockSpec((T, HEAD_DIM), lambda h, i: (0, h))
    bias = pl.BlockSpec((None, tq, kw), lambda h, i: (h, 0, 0))
    return pl.pallas_call(
        body, name="attn_bwd", grid=(H, nq),
        in_specs=[blk,
                  pl.BlockSpec((T, HEAD_DIM), lambda h, i: (0, H + h)),
                  pl.BlockSpec((T, HEAD_DIM), lambda h, i: (0, 2 * H + h)),
                  bias, blk],
        out_specs=[blk, col, col, bias],
        out_shape=[jax.ShapeDtypeStruct((T, A), CDT), jax.ShapeDtypeStruct((T, A), CDT),
                   jax.ShapeDtypeStruct((T, A), CDT), jax.ShapeDtypeStruct((H, tq, kw), F32)],
        scratch_shapes=[pltpu.VMEM((BAND + T, HEAD_DIM), CDT), pltpu.VMEM((BAND + T, HEAD_DIM), CDT),
                        pltpu.VMEM((BAND + T, HEAD_DIM), F32), pltpu.VMEM((BAND + T, HEAD_DIM), F32)],
        compiler_params=_params(("parallel", "arbitrary")))(proj, proj, proj, biasm, datt)


def _retention_tables(T, H, blk):
    half = HEAD_DIM // 2
    inv = 1.0 / (ROPE_BASE ** (jnp.arange(0, HEAD_DIM, 2, dtype=F32) / HEAD_DIM))
    ang = jnp.arange(T, dtype=F32)[:, None] * inv[None, :]
    cos, sin = jnp.cos(ang), jnp.sin(ang)
    rc = jnp.concatenate([cos, cos], axis=1)
    rs = jnp.concatenate([-sin, sin], axis=1)
    assert rc.shape == (T, 2 * half)
    log_g = jnp.log(1.0 - 2.0 ** (-5.0 - jnp.arange(H, dtype=F32)))[:, None, None]
    idx = jnp.arange(blk, dtype=F32)
    n, m = idx[:, None], idx[None, :]
    same = (n // CHUNK) == (m // CHUNK)
    earlier = (m // CHUNK) < (n // CHUNK)
    dist = jnp.where(same, jnp.abs(n - m), n - m)[None]
    dmat = jnp.where((same | earlier)[None], jnp.exp(log_g * dist), 0.0)
    ones = jnp.ones((1, 1, HEAD_DIM), F32)
    qd = jnp.exp(log_g * (idx[None, :, None] + 1.0)) * ones
    kd = jnp.exp(log_g * (blk - 1.0 - idx[None, :, None])) * ones
    cd = jnp.exp(log_g * blk) * jnp.ones((1, 8, HEAD_DIM), F32)
    return rc, rs, dmat, qd, kd, cd


def _rot(v, rc, rs):
    return v * rc + pltpu.roll(v, HEAD_DIM // 2, 1) * rs


def _rot_bwd(dv, rc, rs):
    return dv * rc + pltpu.roll(dv * rs, HEAD_DIM // 2, 1)


def ret_fwd(proj, tables, A, blk):
    T = proj.shape[0]
    H = A // HEAD_DIM
    nb = T // blk
    hp = RET_HEADS_PER_STEP
    rc, rs, dmat, qd, kd, cd = tables
    scale = HEAD_DIM ** -0.5

    def body(q_ref, k_ref, v_ref, g_ref, rc_ref, rs_ref, d_ref, qd_ref, kd_ref, cd_ref,
             y_ref, o_ref, st_ref, state):
        b = pl.program_id(1)

        @pl.when(b == 0)
        def _():
            state[...] = jnp.zeros_like(state)

        c, s = rc_ref[...], rs_ref[...]
        for u in range(hp):
            cols = pl.ds(u * HEAD_DIM, HEAD_DIM)
            qs = (_rot(q_ref[:, cols].astype(F32), c, s) * scale).astype(CDT)
            kr = _rot(k_ref[:, cols].astype(F32), c, s)
            v = v_ref[:, cols]
            sb = state[u].astype(CDT)
            a = lax.dot_general(qs, kr.astype(CDT), _NT, preferred_element_type=F32) * d_ref[u]
            o = jnp.dot(a.astype(CDT), v, preferred_element_type=F32)
            o = o + jnp.dot(qs, sb, preferred_element_type=F32) * qd_ref[u]
            st_ref[u] = sb
            state[u] = state[u] * cd_ref[u, 0:1, :] + lax.dot_general(
                (kr * kd_ref[u]).astype(CDT), v, _TN, preferred_element_type=F32)
            o_ref[:, cols] = o
            on = o * lax.rsqrt(jnp.mean(o * o, axis=-1, keepdims=True) + EPS)
            g = g_ref[:, cols].astype(F32)
            y_ref[:, cols] = (g * _sigmoid(g) * on).astype(y_ref.dtype)

    w = hp * HEAD_DIM

    def pj(off):
        return pl.BlockSpec((blk, w), lambda h, i: (i, off * H // hp + h))

    tok = pl.BlockSpec((blk, HEAD_DIM), lambda h, i: (i, 0))
    out = pl.BlockSpec((blk, w), lambda h, i: (i, h))

    def per_head(r, c):
        return pl.BlockSpec((hp, r, c), lambda h, i: (h, 0, 0))

    return pl.pallas_call(
        body, name="ret_fwd", grid=(H // hp, nb),
        in_specs=[pj(3), pj(4), pj(5), pj(6), tok, tok, per_head(blk, blk),
                  per_head(blk, HEAD_DIM), per_head(blk, HEAD_DIM), per_head(8, HEAD_DIM)],
        out_specs=[out, out, pl.BlockSpec((hp, None, HEAD_DIM, HEAD_DIM), lambda h, i: (h, i, 0, 0))],
        out_shape=[jax.ShapeDtypeStruct((T, A), CDT), jax.ShapeDtypeStruct((T, A), F32),
                   jax.ShapeDtypeStruct((H, nb, HEAD_DIM, HEAD_DIM), CDT)],
        scratch_shapes=[pltpu.VMEM((hp, HEAD_DIM, HEAD_DIM), F32)],
        compiler_params=_params(("parallel", "arbitrary")))(
            proj, proj, proj, proj, rc, rs, dmat, qd, kd, cd)


def ret_bwd(proj, tables, o_raw, states, dy, A, blk):
    T = proj.shape[0]
    H = A // HEAD_DIM
    nb = T // blk
    hp = RET_HEADS_PER_STEP
    rc, rs, dmat, qd, kd, cd = tables
    scale = HEAD_DIM ** -0.5

    def body(q_ref, k_ref, v_ref, g_ref, rc_ref, rs_ref, d_ref, qd_ref, kd_ref, cd_ref,
             o_ref, st_ref, dy_ref, dq_ref, dk_ref, dv_ref, dg_ref, dstate):
        b = pl.program_id(1)

        @pl.when(b == 0)
        def _():
            dstate[...] = jnp.zeros_like(dstate)

        c, s = rc_ref[...], rs_ref[...]
        for u in range(hp):
            cols = pl.ds(u * HEAD_DIM, HEAD_DIM)
            qs = (_rot(q_ref[:, cols].astype(F32), c, s) * scale).astype(CDT)
            kr = _rot(k_ref[:, cols].astype(F32), c, s)
            krb = kr.astype(CDT)
            kdb = (kr * kd_ref[u]).astype(CDT)
            v = v_ref[:, cols]
            dmat_v = d_ref[u]
            a = lax.dot_general(qs, krb, _NT, preferred_element_type=F32) * dmat_v

            o = o_ref[:, cols]
            r = lax.rsqrt(jnp.mean(o * o, axis=-1, keepdims=True) + EPS)
            on = o * r
            g = g_ref[:, cols].astype(F32)
            sg = _sigmoid(g)
            dyv = dy_ref[:, cols].astype(F32)
            dg_ref[:, cols] = (dyv * on * (sg * (1.0 + g * (1.0 - sg)))).astype(dg_ref.dtype)
            don = dyv * (g * sg)
            do = r * (don - on * jnp.mean(don * on, axis=-1, keepdims=True))
            dob = do.astype(CDT)
            doq = (do * qd_ref[u]).astype(CDT)
            dsb = dstate[u].astype(CDT)

            dv = lax.dot_general(a.astype(CDT), dob, _TN, preferred_element_type=F32)
            dv = dv + jnp.dot(kdb, dsb, preferred_element_type=F32)
            dv_ref[:, cols] = dv.astype(dv_ref.dtype)
            dpb = (lax.dot_general(dob, v, _NT, preferred_element_type=F32) * dmat_v).astype(CDT)
            dqs = jnp.dot(dpb, krb, preferred_element_type=F32)
            dqs = dqs + lax.dot_general(doq, st_ref[u], _NT, preferred_element_type=F32)
            dkr = lax.dot_general(dpb, qs, _TN, preferred_element_type=F32)
            dkr = dkr + lax.dot_general(v, dsb, _NT, preferred_element_type=F32) * kd_ref[u]
            dstate[u] = dstate[u] * cd_ref[u, 0:1, :] + lax.dot_general(
                qs, doq, _TN, preferred_element_type=F32)
            dq_ref[:, cols] = _rot_bwd(dqs * scale, c, s).astype(dq_ref.dtype)
            dk_ref[:, cols] = _rot_bwd(dkr, c, s).astype(dk_ref.dtype)

    w = hp * HEAD_DIM

    def pj(off):
        return pl.BlockSpec((blk, w), lambda h, i: (nb - 1 - i, off * H // hp + h))

    tok = pl.BlockSpec((blk, HEAD_DIM), lambda h, i: (nb - 1 - i, 0))
    out = pl.BlockSpec((blk, w), lambda h, i: (nb - 1 - i, h))

    def per_head(r, c):
        return pl.BlockSpec((hp, r, c), lambda h, i: (h, 0, 0))

    shp = jax.ShapeDtypeStruct((T, A), CDT)
    return pl.pallas_call(
        body, name="ret_bwd", grid=(H // hp, nb),
        in_specs=[pj(3), pj(4), pj(5), pj(6), tok, tok, per_head(blk, blk),
                  per_head(blk, HEAD_DIM), per_head(blk, HEAD_DIM), per_head(8, HEAD_DIM),
                  out, pl.BlockSpec((hp, None, HEAD_DIM, HEAD_DIM), lambda h, i: (h, nb - 1 - i, 0, 0)),
                  out],
        out_specs=[out, out, out, out], out_shape=[shp, shp, shp, shp],
        scratch_shapes=[pltpu.VMEM((hp, HEAD_DIM, HEAD_DIM), F32)],
        compiler_params=_params(("parallel", "arbitrary")))(
            proj, proj, proj, proj, rc, rs, dmat, qd, kd, cd, o_raw, states, dy)


def _mesh_pos():
    return lax.axis_index("x"), lax.axis_index("y"), lax.axis_index("c")


def _flat(pos):
    return 4 * pos[0] + 2 * pos[1] + pos[2]


_HBM = pl.BlockSpec(memory_space=pltpu.HBM)


def cast_shard(name, w, rows_p, cols_p, me_arr, after=()):
    r, c = w.shape
    tr = _row_tile(math.gcd(r, rows_p), 256)
    nr = r // tr

    def body(me_ref, w_ref, *rest):
        o_ref = rest[-1]
        i = pl.program_id(0)
        o_ref[...] = jnp.zeros_like(o_ref)

        @pl.when(i < nr)
        def _():
            o_ref[:, 0:c] = w_ref[...].astype(o_ref.dtype)

    return pl.pallas_call(
        body, name=name,
        grid_spec=pltpu.PrefetchScalarGridSpec(
            num_scalar_prefetch=1, grid=(rows_p // tr,),
            in_specs=[pl.BlockSpec((tr, c), lambda i, me: (jnp.minimum(i, nr - 1), 0))] + [_ANY] * len(after),
            out_specs=pl.BlockSpec((None, tr, cols_p), lambda i, me: (me[0], i, 0))),
        out_shape=jax.ShapeDtypeStruct((N_DEV, rows_p, cols_p), CDT),
        compiler_params=_params(("arbitrary",)))(me_arr, w, *after)


def exchange_partials(arrays, after):
    n, na = len(arrays), len(after)

    def body(*refs):
        ins, outs = refs[:n], refs[n + na:2 * n + na]
        send_sems, recv_sems, local_sems = refs[2 * n + na:]
        me = _mesh_pos()
        copies, locals_ = [], []
        for t in range(n):
            cp = pltpu.make_async_copy(ins[t], outs[t].at[_flat(me)], local_sems.at[t])
            cp.start()
            locals_.append(cp)
            for k in range(1, N_DEV):
                peer = _peer(me, k)
                send = pltpu.make_async_remote_copy(
                    src_ref=ins[t], dst_ref=outs[t].at[_flat(me)],
                    send_sem=send_sems.at[t, k - 1], recv_sem=recv_sems.at[t, k - 1],
                    device_id=peer, device_id_type=pl.DeviceIdType.MESH)
                send.start()
                recv = pltpu.make_async_remote_copy(
                    src_ref=ins[t], dst_ref=outs[t].at[_flat(peer)],
                    send_sem=send_sems.at[t, k - 1], recv_sem=recv_sems.at[t, k - 1],
                    device_id=peer, device_id_type=pl.DeviceIdType.MESH)
                copies.append((send, recv))
        for send, recv in copies:
            recv.wait_recv()
        for send, recv in copies:
            send.wait_send()
        for cp in locals_:
            cp.wait()

    return pl.pallas_call(
        body, name="exchange_partials",
        in_specs=[_HBM] * n + [_ANY] * na, out_specs=[_HBM] * n,
        out_shape=[jax.ShapeDtypeStruct((N_DEV,) + a.shape, a.dtype) for a in arrays],
        scratch_shapes=[pltpu.SemaphoreType.DMA((n, 7)), pltpu.SemaphoreType.DMA((n, 7)),
                        pltpu.SemaphoreType.DMA((n,))],
        )(*arrays, *after)


_SEM = pl.BlockSpec(memory_space=pltpu.SEMAPHORE)
_ANY = pl.BlockSpec(memory_space=pl.ANY)
_EFFECT = pltpu.SideEffectType.DATAFLOW_SIDE_EFFECTING


def _peer(me, k):
    return tuple(1 - v if bit else v for v, bit in zip(me, (k >> 2, (k >> 1) & 1, k & 1)))


_CHIP_MASKS = (2, 4, 6)

_EXCHANGE_MODES = {"gather": (7, None), "gather_chips": (4, None), "forward": (3, None),
                   "scatter": (7, 7), "scatter_pair": (4, 4), "scatter_chips": (3, 3)}


def _plan(mode, bufs, n, me):
    per = _EXCHANGE_MODES[mode][0]
    sib = _peer(me, 1)
    plan = []
    for t in range(n):
        src_arr, land_arr = bufs[t], bufs[n + t] if _EXCHANGE_MODES[mode][1] else None
        if mode in ("gather", "gather_chips"):
            masks = range(1, N_DEV) if mode == "gather" else (1,) + _CHIP_MASKS
            rows = [(src_arr.at[_flat(me)], src_arr.at[_flat(me)], _peer(me, k), src_arr.at[_flat(_peer(me, k))])
                    for k in masks]
        elif mode == "forward":
            rows = [(src_arr.at[_flat(_peer(me, k))], src_arr.at[_flat(_peer(me, k))], sib,
                     src_arr.at[_flat(_peer(sib, k))]) for k in _CHIP_MASKS]
        elif mode == "scatter":
            rows = [(src_arr.at[_flat(_peer(me, k))], land_arr.at[k - 1], _peer(me, k), land_arr.at[k - 1])
                    for k in range(1, N_DEV)]
        elif mode == "scatter_pair":
            rows = [(src_arr.at[_flat(_peer(me, q + 1))], land_arr.at[qi], sib, land_arr.at[qi])
                    for qi, q in enumerate((0,) + _CHIP_MASKS)]
        else:
            assert mode == "scatter_chips"
            rows = [(src_arr.at[qi + 1], land_arr.at[qi], _peer(me, q), land_arr.at[qi])
                    for qi, q in enumerate(_CHIP_MASKS)]
        assert len(rows) == per
        plan += [(t * per + s,) + row for s, row in enumerate(rows)]
    return plan


def exchange_start(name, arrays, mode, after):
    n, na = len(arrays), len(after)
    per, slots = _EXCHANGE_MODES[mode]
    bufs = list(arrays)
    if slots:
        bufs += [lax.empty((slots,) + a.shape[1:], a.dtype) for a in arrays]
    nb = len(bufs)

    def body(*refs):
        send_sems, recv_sems = refs[nb + na], refs[nb + na + 1]
        token = refs[-1]
        for s, src, dst, dev, _ in _plan(mode, refs[:nb], n, _mesh_pos()):
            pltpu.make_async_remote_copy(
                src_ref=src, dst_ref=dst, send_sem=send_sems.at[s], recv_sem=recv_sems.at[s],
                device_id=dev, device_id_type=pl.DeviceIdType.MESH).start()
        token[...] = jnp.zeros_like(token)

    out_shape = [pltpu.SemaphoreType.DMA((n * per,)), pltpu.SemaphoreType.DMA((n * per,))]
    out_shape += [pltpu.HBM(a.shape, a.dtype) for a in bufs]
    out_shape.append(jax.ShapeDtypeStruct((8, LANE), F32))
    args = [pltpu.with_memory_space_constraint(a, pltpu.HBM) for a in bufs] + list(after)
    outs = pl.pallas_call(
        body, name=name, out_shape=out_shape,
        in_specs=[_HBM] * nb + [_ANY] * na,
        out_specs=[_SEM, _SEM] + [_HBM] * nb + [pl.BlockSpec(memory_space=pltpu.VMEM)],
        input_output_aliases={i: 2 + i for i in range(nb)},
        compiler_params=pltpu.CompilerParams(has_side_effects=_EFFECT))(*args)
    return outs[0], outs[1], list(outs[2:2 + nb]), outs[-1]


def exchange_wait(name, started, mode, after):
    send_sems, recv_sems, bufs, _ = started
    nb, na = len(bufs), len(after)
    n = nb // 2 if _EXCHANGE_MODES[mode][1] else nb

    def body(*refs):
        send_sems_ref, recv_sems_ref = refs[nb], refs[nb + 1]
        for s, src, _, dev, land in _plan(mode, refs[:nb], n, _mesh_pos()):
            cp = pltpu.make_async_remote_copy(
                src_ref=src, dst_ref=land, send_sem=send_sems_ref.at[s], recv_sem=recv_sems_ref.at[s],
                device_id=dev, device_id_type=pl.DeviceIdType.MESH)
            cp.wait_send()
            cp.wait_recv()

    outs = pl.pallas_call(
        body, name=name, out_shape=[pltpu.HBM(a.shape, a.dtype) for a in bufs],
        in_specs=[_HBM] * nb + [_SEM, _SEM] + [_ANY] * na, out_specs=[_HBM] * nb,
        input_output_aliases={i: i for i in range(nb)},
        compiler_params=pltpu.CompilerParams(has_side_effects=_EFFECT))(
            *bufs, send_sems, recv_sems, *after)
    return list(outs)


def pair_sum(name, own, landed, blocks):
    _, r, c = own.shape
    tr = _row_tile(r, 256)

    def body(idx_ref, o_ref, l_ref, s_ref):
        s_ref[...] = (o_ref[...].astype(F32) + l_ref[...].astype(F32)).astype(s_ref.dtype)

    blk = pl.BlockSpec((None, tr, c), lambda q, i, idx: (q, i, 0))
    return pl.pallas_call(
        body, name=name,
        grid_spec=pltpu.PrefetchScalarGridSpec(
            num_scalar_prefetch=1, grid=(4, r // tr),
            in_specs=[pl.BlockSpec((None, tr, c), lambda q, i, idx: (idx[q], i, 0)), blk],
            out_specs=blk),
        out_shape=jax.ShapeDtypeStruct((4, r, c), own.dtype),
        compiler_params=_params(("parallel", "parallel")))(blocks, own, landed)


def _adamw_math(w, g, m, v):
    m = ADAM_B1 * m + (1.0 - ADAM_B1) * g
    v = ADAM_B2 * v + (1.0 - ADAM_B2) * (g * g)
    m_hat = m / (1.0 - ADAM_B1 ** ADAM_STEP)
    v_hat = v / (1.0 - ADAM_B2 ** ADAM_STEP)
    delta = -ADAM_LR * (m_hat / (jnp.sqrt(v_hat) + ADAM_EPS) + ADAM_WD * w)
    return delta, m, v


def reduce_adamw(name, land, w, m, v, tr, own=None, me_arr=None):
    R, C = w.shape
    S, _, Cp = land.shape

    def body(*refs):
        if own is not None:
            _, own_ref, l_ref, w_ref, m_ref, v_ref, g_ref, d_ref, nm_ref, nv_ref = refs
            g = own_ref[:, 0:C].astype(F32)
            first = 0
        else:
            l_ref, w_ref, m_ref, v_ref, g_ref, d_ref, nm_ref, nv_ref = refs
            g = l_ref[0, :, 0:C].astype(F32)
            first = 1
        for s in range(first, S):
            g = g + l_ref[s, :, 0:C].astype(F32)
        delta, nm, nv = _adamw_math(w_ref[...], g, m_ref[...], v_ref[...])
        g_ref[...] = g
        d_ref[...] = delta
        nm_ref[...] = nm
        nv_ref[...] = nv

    shp = jax.ShapeDtypeStruct((R, C), F32)
    if own is None:
        blk = pl.BlockSpec((tr, C), lambda i: (i, 0))
        return pl.pallas_call(
            body, name=name, grid=(R // tr,),
            in_specs=[pl.BlockSpec((S, tr, Cp), lambda i: (0, i, 0)), blk, blk, blk],
            out_specs=[blk, blk, blk, blk], out_shape=[shp, shp, shp, shp],
            compiler_params=_params(("parallel",)))(land, w, m, v)
    blk = pl.BlockSpec((tr, C), lambda i, me: (i, 0))
    return pl.pallas_call(
        body, name=name,
        grid_spec=pltpu.PrefetchScalarGridSpec(
            num_scalar_prefetch=1, grid=(R // tr,),
            in_specs=[pl.BlockSpec((None, tr, Cp), lambda i, me: (me[0], i, 0)),
                      pl.BlockSpec((S, tr, Cp), lambda i, me: (0, i, 0)), blk, blk, blk],
            out_specs=[blk, blk, blk, blk]),
        out_shape=[shp, shp, shp, shp],
        compiler_params=_params(("parallel",)))(me_arr, own, land, w, m, v)


def _row_tile(r, cap):
    t = min(r, cap)
    while r % t or t % 8:
        t -= 8
    return t


def kernel(x, norm_ffn1_g, ffn1_w_gate, ffn1_w_up, ffn1_w_down, norm_mix_g, w_in, rel_bias, w_out_att, w_out_ret, w_out, norm_ffn2_g, ffn2_w_gate, ffn2_w_up, ffn2_w_down, norm_final_g, loss_target, m_norm_ffn1_g, m_ffn1_w_gate, m_ffn1_w_up, m_ffn1_w_down, m_norm_mix_g, m_w_in, m_rel_bias, m_w_out_att, m_w_out_ret, m_w_out, m_norm_ffn2_g, m_ffn2_w_gate, m_ffn2_w_up, m_ffn2_w_down, m_norm_final_g, v_norm_ffn1_g, v_ffn1_w_gate, v_ffn1_w_up, v_ffn1_w_down, v_norm_mix_g, v_w_in, v_rel_bias, v_w_out_att, v_w_out_ret, v_w_out, v_norm_ffn2_g, v_ffn2_w_gate, v_ffn2_w_up, v_ffn2_w_down, v_norm_final_g):
    T, D = x.shape[1], x.shape[2]
    A = w_out_att.shape[1]
    H = A // HEAD_DIM
    nf = ffn1_w_gate.shape[2]
    nfp = _round_up(nf, LANE)
    nin = w_in.shape[2]
    nd = w_out.shape[1]
    assert nin % LANE == 0 and nd % LANE == 0 and (7 * A) % nd == 0 and T % ATT_TQ == 0
    tm = min(512, T)
    tw = min(2048, T)
    tw_in = min(1024, T)
    tn = min(256, T)
    x0 = x[0]
    tgt = loss_target[0]

    me_arr = (4 * lax.axis_index("x") + 2 * lax.axis_index("y") + lax.axis_index("c")).astype(jnp.int32).reshape(1)

    def slot(tag, w, after, rows_p=None):
        return cast_shard("cast_" + tag, w[0], rows_p or w.shape[1], w.shape[2], me_arr, after)

    def slot_t(tag, w, after):
        return cast_shard("cast_" + tag, jnp.transpose(w[0]), nfp, w.shape[1], me_arr, after)

    groups = [("wg1", [("wg1", ffn1_w_gate, slot_t)], True), ("wu1", [("wu1", ffn1_w_up, slot_t)], True),
              ("wd1", [("wd1", ffn1_w_down, functools.partial(slot, rows_p=nfp))], True),
              ("win", [("win", w_in, slot)], True),
              ("wout", [("woa", w_out_att, slot), ("wor", w_out_ret, slot), ("wo", w_out, slot)], False),
              ("wgu2", [("wg2", ffn2_w_gate, slot_t), ("wu2", ffn2_w_up, slot_t)], True),
              ("wd2", [("wd2", ffn2_w_down, functools.partial(slot, rows_p=nfp))], True)]
    ag_started = {}
    order = []
    for tag, members, two_level in groups:
        mode = "gather_chips" if two_level else "gather"
        started = exchange_start("ag_start_" + tag, [make(nm, w, order) for nm, w, make in members], mode, order)
        ag_started[tag] = (started, mode)
        order = [started[3]]

    def gathered(tag, after):
        started, mode = ag_started[tag]
        got = exchange_wait("ag_wait_" + tag, started, mode, [after])
        if mode == "gather":
            return got
        passing = exchange_start("ag_pass_" + tag, got, "forward", [])
        return exchange_wait("ag_passed_" + tag, passing, "forward", [passing[3]])

    xi, yi, ci = lax.axis_index("x"), lax.axis_index("y"), lax.axis_index("c")
    my_side = jnp.stack([4 * (1 - xi if q & 4 else xi) + 2 * (1 - yi if q & 2 else yi) + ci
                         for q in (0,) + _CHIP_MASKS]).astype(jnp.int32)
    first_block = jnp.zeros((1,), jnp.int32)

    def swiglu(prods, _):
        a, b = prods
        return a, b, a * _sigmoid(a) * b

    def ffn_fwd(tag, xin, g, get_wgu, get_wd, after=()):
        h = rmsnorm_fwd(tag + "_norm", xin, g, tn, after)
        if isinstance(get_wgu, tuple):
            Wg, = get_wgu[0](h)
            a, = mm_block(tag + "_gate", T, tm, [(h, "full", D, 0)], [(Wg, 0, True)], [], [("3d", nfp, CDT)],
                          lambda p, _: p)
            Wu, = get_wgu[1](a)
            b, mid = mm_block(tag + "_up", T, tm, [(h, "full", D, 0)], [(Wu, 0, True)], [(a, "3d", nfp, 0)],
                              [("3d", nfp, CDT)] * 2, lambda p, ex: (p[0], ex[0] * _sigmoid(ex[0]) * p[0]))
        else:
            Wg, Wu = get_wgu(h)
            a, b, mid = mm_block(tag + "_up", T, tm, [(h, "full", D, 0)], [(Wg, 0, True), (Wu, 0, True)],
                                 [], [("3d", nfp, CDT)] * 3, swiglu)
        Wd, = get_wd(mid)
        xo = mm_reduce_j(tag + "_down", T, tm, [(mid, "3d", nfp, 0, Wd, False)], D, F32, res=xin, scale=0.5,
                         jstep=2)
        return h, a, b, mid, xo, (Wg, Wu, Wd)

    h1, a1, b1, mid1, x1, (Wg1, Wu1, Wd1) = ffn_fwd(
        "ffn1", x0, norm_ffn1_g, (lambda h: gathered("wg1", h), lambda a: gathered("wu1", a)),
        lambda mid: gathered("wd1", mid), after=order)
    h2 = rmsnorm_fwd("mix_norm", x1, norm_mix_g, tn)
    Win, = gathered("win", h2)
    proj, = mm_block("in_proj", T, tm, [(h2, "full", D, 0)], [(Win, 0, False)], [], [("col", nin, CDT)],
                     lambda p, _: p)
    biasm = attn_bias(rel_bias[0], ATT_TQ)
    att = attn_fwd(proj, biasm, A, ATT_TQ)
    tables = _retention_tables(T, H, RET_BLK)
    retg, ret_raw, states = ret_fwd(proj, tables, A, RET_BLK)
    Woa, Wor, Wo = gathered("wout", retg)
    goff = 7 * A // nd

    def merge(prods, ex):
        ba, br = prods
        ga, gr = ex
        return ba, br, _sigmoid(ga) * ba + _sigmoid(gr) * br

    ba, br, merged = mm_block(
        "branches", T, tm, [(att, "full", A, 0), (retg, "full", A, 0)], [(Woa, 0, False), (Wor, 1, False)],
        [(proj, "col", nd, goff), (proj, "col", nd, goff + N_DEV)], [("col", nd, CDT)] * 3, merge, order="ij")
    x2 = mm_reduce_j("out_proj", T, tm, [(merged, "col", nd, 0, Wo, False)], D, F32, res=x1, scale=1.0,
                     jstep=N_DEV)
    h3, a2, b2, mid2, x3, (Wg2, Wu2, Wd2) = ffn_fwd(
        "ffn2", x2, norm_ffn2_g, lambda h: gathered("wgu2", h), lambda mid: gathered("wd2", mid))

    dx3, dx3h, dgf, loss_part = loss_head(x3, norm_final_g.reshape(1, D), tgt, tn)

    def swiglu_bwd(prods, ex):
        dm, = prods
        a, b = ex
        sg = _sigmoid(a)
        return dm * b * (sg * (1.0 + a * (1.0 - sg))), dm * (a * sg)

    def ffn_bwd(tag, dxh, h, a, b, mid, Wg, Wu, Wd, two_level=False):
        da, db = mm_block(tag + "_down_bwd", T, tm, [(dxh, "full", D, 0)], [(Wd, 0, True)],
                          [(a, "3d", nfp, 0), (b, "3d", nfp, 0)], [("3d", nfp, CDT)] * 2, swiglu_bwd)

        def up_bwd(after):
            return mm_reduce_j(tag + "_up_bwd", T, tm,
                               [(da, "3d", nfp, 0, Wg, False), (db, "3d", nfp, 0, Wu, False)],
                               D, CDT, after=after, jstep=2)

        if not two_level:
            dWd = mm_reduce_i(tag + "_dwd", T, tw, (mid, "3d", nfp, 0), (dxh, "full", D, 0))
            dWg = mm_reduce_i(tag + "_dwg", T, tw, (da, "3d", nfp, 0), (h, "full", D, 0))
            dWu = mm_reduce_i(tag + "_dwu", T, tw, (db, "3d", nfp, 0), (h, "full", D, 0))
            sent = exchange_start("rs_start_" + tag, [dWg, dWu, dWd], "scatter", [])
            return up_bwd([sent[3]]), sent

        def swap(nm, grad, after):
            return exchange_start("rs_pair_start_%s_%s" % (tag, nm), [grad], "scatter_pair", after)

        def to_chips(nm, swapping, after):
            own, landed = exchange_wait("rs_pair_wait_%s_%s" % (tag, nm), swapping, "scatter_pair", [after])
            sums = pair_sum("%s_pair_sum_%s" % (tag, nm), own, landed, my_side)
            return exchange_start("rs_chips_start_%s_%s" % (tag, nm), [sums], "scatter_chips", [])

        dWd = mm_reduce_i(tag + "_dwd", T, tw, (mid, "3d", nfp, 0), (dxh, "full", D, 0))
        swap_d = swap("d", dWd, [])
        dWg = mm_reduce_i(tag + "_dwg", T, tw, (da, "3d", nfp, 0), (h, "full", D, 0), after=[swap_d[3]])
        sent_d = to_chips("d", swap_d, dWg)
        swap_g = swap("g", dWg, [sent_d[3]])
        dWu = mm_reduce_i(tag + "_dwu", T, tw, (db, "3d", nfp, 0), (h, "full", D, 0), after=[swap_g[3]])
        sent_g = to_chips("g", swap_g, dWu)
        swap_u = swap("u", dWu, [sent_g[3]])
        dh = up_bwd([swap_u[3]])
        sent_u = to_chips("u", swap_u, dh)
        return dh, [sent_g, sent_u, sent_d]

    dh3, sent_ffn2 = ffn_bwd("ffn2", dx3h, h3, a2, b2, mid2, Wg2, Wu2, Wd2)
    dx2, dx2c, dg2 = rmsnorm_bwd("ffn2_norm_bwd", x2, norm_ffn2_g, dh3, dx3, 1.0, tn)

    def merge_bwd(prods, ex):
        dmg, = prods
        ba_, br_, ga, gr = ex
        sa, sr = _sigmoid(ga), _sigmoid(gr)
        return dmg * sa, dmg * sr, dmg * ba_ * sa * (1.0 - sa), dmg * br_ * sr * (1.0 - sr)

    dba, dbr, dga, dgr = mm_block(
        "out_proj_bwd", T, tm, [(dx2c, "full", D, 0)], [(Wo, 0, True)],
        [(ba, "col", nd, 0), (br, "col", nd, 0), (proj, "col", nd, goff), (proj, "col", nd, goff + N_DEV)],
        [("col", nd, CDT)] * 4, merge_bwd, order="ij")
    dWo = mm_reduce_i("dwo", T, tw, (merged, "col", nd, 0), (dx2c, "full", D, 0))
    dWoa = mm_reduce_i("dwoa", T, tw, (att, "full", A, 0), (dba, "col", nd, 0))
    dWor = mm_reduce_i("dwor", T, tw, (retg, "full", A, 0), (dbr, "col", nd, 0))
    sent_mix = exchange_start("rs_start_mix", [dWoa, dWor, dWo], "scatter", [])
    datt = mm_reduce_j("att_out_bwd", T, tm, [(dba, "col", nd, 0, Woa, True)], A, CDT, after=[sent_mix[3]],
                       jstep=N_DEV)
    dretg = mm_reduce_j("ret_out_bwd", T, tm, [(dbr, "col", nd, 0, Wor, True)], A, CDT, jstep=N_DEV)
    dq_r, dk_r, dv_r, dg_r = ret_bwd(proj, tables, ret_raw, states, dretg, A, RET_BLK)
    dq_a, dk_a, dv_a, dst = attn_bwd(proj, biasm, datt, A, ATT_TQ)
    dbias = jnp.pad(attn_bias_grad(dst, ATT_TQ), ((0, 0), (0, N_REL_PAD - N_REL)))
    dproj = jnp.concatenate([dq_a, dk_a, dv_a, dq_r, dk_r, dv_r, dg_r, dga, dgr], axis=1)
    dWin = mm_reduce_i("dwin", T, tw_in, (h2, "full", D, 0), (dproj, "col", nin, 0))
    sent_win = exchange_start("rs_start_win", [dWin], "scatter", [])
    dh2 = mm_reduce_j("in_proj_bwd", T, tm, [(dproj, "col", nin, 0, Win, True)], D, CDT, after=[sent_win[3]],
                      jstep=2)
    dx1, dx1h, dgm = rmsnorm_bwd("mix_norm_bwd", x1, norm_mix_g, dh2, dx2, 0.5, tn)
    dh1, sent_ffn1 = ffn_bwd("ffn1", dx1h, h1, a1, b1, mid1, Wg1, Wu1, Wd1, two_level=True)
    grad_x, _, dg1 = rmsnorm_bwd("ffn1_norm_bwd", x0, norm_ffn1_g, dh1, dx1, 1.0, tn, after=[sent_ffn1[1][3]])

    dgains = jnp.concatenate([dg1, dgm, dg2, dgf, jnp.zeros((4, D), F32)], axis=0)

    def upd(name, own, land, w, m, v, own_block=me_arr, transposed=False):
        w2, m2, v2 = [jnp.transpose(t[0]) if transposed else t[0] for t in (w, m, v)]
        outs = reduce_adamw(name, land, w2, m2, v2, _row_tile(w2.shape[0], 256), own=own, me_arr=own_block)
        return [jnp.transpose(o)[None] if transposed else o[None] for o in outs]

    res = {}
    oWg2, oWu2, oWd2, lWg2, lWu2, lWd2 = exchange_wait("rs_wait_ffn2", sent_ffn2, "scatter", [grad_x])
    res["ffn2_w_gate"] = upd("adamw_wg2", oWg2, lWg2, ffn2_w_gate, m_ffn2_w_gate, v_ffn2_w_gate, transposed=True)
    res["ffn2_w_up"] = upd("adamw_wu2", oWu2, lWu2, ffn2_w_up, m_ffn2_w_up, v_ffn2_w_up, transposed=True)
    res["ffn2_w_down"] = upd("adamw_wd2", oWd2, lWd2, ffn2_w_down, m_ffn2_w_down, v_ffn2_w_down)
    oWoa, oWor, oWo, lWoa, lWor, lWo = exchange_wait("rs_wait_mix", sent_mix, "scatter", [res["ffn2_w_down"][1]])
    res["w_out_att"] = upd("adamw_woa", oWoa, lWoa, w_out_att, m_w_out_att, v_w_out_att)
    res["w_out_ret"] = upd("adamw_wor", oWor, lWor, w_out_ret, m_w_out_ret, v_w_out_ret)
    res["w_out"] = upd("adamw_wo", oWo, lWo, w_out, m_w_out, v_w_out)
    oWin, lWin = exchange_wait("rs_wait_win", sent_win, "scatter", [res["w_out"][1]])
    res["w_in"] = upd("adamw_win", oWin, lWin, w_in, m_w_in, v_w_in)
    lgains, lbias = exchange_partials([dgains, dbias], [res["w_in"][1]])
    (oWg1, lWg1), (oWu1, lWu1), (oWd1, lWd1) = [
        exchange_wait("rs_wait_ffn1_" + nm, started, "scatter_chips", [lgains])
        for nm, started in zip("gud", sent_ffn1)]
    res["ffn1_w_gate"] = upd("adamw_wg1", oWg1, lWg1, ffn1_w_gate, m_ffn1_w_gate, v_ffn1_w_gate, first_block, transposed=True)
    res["ffn1_w_up"] = upd("adamw_wu1", oWu1, lWu1, ffn1_w_up, m_ffn1_w_up, v_ffn1_w_up, first_block, transposed=True)
    res["ffn1_w_down"] = upd("adamw_wd1", oWd1, lWd1, ffn1_w_down, m_ffn1_w_down, v_ffn1_w_down, first_block)

    def stack_gains(a, b, c_, d):
        return jnp.concatenate([a, b, c_, d.reshape(1, D), jnp.zeros((4, D), F32)], axis=0)

    gw = stack_gains(norm_ffn1_g, norm_mix_g, norm_ffn2_g, norm_final_g)
    gm = stack_gains(m_norm_ffn1_g, m_norm_mix_g, m_norm_ffn2_g, m_norm_final_g)
    gv = stack_gains(v_norm_ffn1_g, v_norm_mix_g, v_norm_ffn2_g, v_norm_final_g)
    gains = reduce_adamw("adamw_gains", lgains, gw, gm, gv, 8)

    def padb(t):
        return jnp.pad(t[0], ((0, 0), (0, N_REL_PAD - N_REL)))

    bias = [o[:, :N_REL][None] for o in
            reduce_adamw("adamw_bias", lbias, padb(rel_bias), padb(m_rel_bias), padb(v_rel_bias), H)]
    res["norm_ffn1_g"] = [o[0:1] for o in gains]
    res["norm_mix_g"] = [o[1:2] for o in gains]
    res["norm_ffn2_g"] = [o[2:3] for o in gains]
    res["norm_final_g"] = [o[3] for o in gains]
    res["rel_bias"] = bias

    loss = lax.psum(loss_part[0, 0], MESH_AXES)
    names = ["norm_ffn1_g", "ffn1_w_gate", "ffn1_w_up", "ffn1_w_down", "norm_mix_g", "w_in", "rel_bias",
             "w_out_att", "w_out_ret", "w_out", "norm_ffn2_g", "ffn2_w_gate", "ffn2_w_up", "ffn2_w_down",
             "norm_final_g"]
    out = [loss, grad_x[None]]
    for k in range(4):
        out += [res[nm][k] for nm in names]
    return tuple(out)
```

```python
import functools
import math

import jax
import jax.numpy as jnp
import numpy as np
from jax import lax
from jax.experimental import pallas as pl
from jax.experimental.pallas import tpu as pltpu

F32 = jnp.float32
CDT = jnp.bfloat16

N_DEV = 8
CHUNK = 64
N_PREV_CHUNKS = 8
BAND = N_PREV_CHUNKS * CHUNK
HEAD_DIM = 128
MAX_REL_DIST = 128
N_REL = 2 * MAX_REL_DIST + 1
N_REL_PAD = 384
ROPE_BASE = 10000.0
EPS = 1e-6
NEG = -1e30
LANE = 128
ATT_TQ = 256
RET_BLK = 256
RET_HEADS_PER_STEP = 4
VMEM_LIMIT = 48 * 1024 * 1024

ADAM_LR = 0.001
ADAM_B1 = 0.9
ADAM_B2 = 0.999
ADAM_EPS = 1e-08
ADAM_WD = 0.01
ADAM_STEP = 10

MESH_AXES = ("x", "y", "c")
_NT = (((1,), (1,)), ((), ()))
_TN = (((0,), (0,)), ((), ()))


def _round_up(v, m):
    return (v + m - 1) // m * m


def _params(sem=None):
    return pltpu.CompilerParams(dimension_semantics=sem, vmem_limit_bytes=VMEM_LIMIT)


def _sigmoid(v):
    return 0.5 * (jnp.tanh(0.5 * v) + 1.0)


def _bspec(kind, tm, w, off, order, jmap=lambda j: j):
    def wrap(f):
        if order == "ji":
            return lambda j, i: f(i, jmap(j))
        return lambda i, j: f(i, jmap(j))
    if kind == "full":
        return pl.BlockSpec((tm, w), wrap(lambda i, j: (i, 0)))
    if kind == "col":
        return pl.BlockSpec((tm, w), wrap(lambda i, j: (i, j + off)))
    assert kind == "3d"
    return pl.BlockSpec((None, tm, w), wrap(lambda i, j: (j, i, 0)))


def _wspec(w, order, jmap=lambda j: j):
    if order == "ji":
        return pl.BlockSpec((None,) + w.shape[1:], lambda j, i: (jmap(j), 0, 0))
    return pl.BlockSpec((None,) + w.shape[1:], lambda i, j: (jmap(j), 0, 0))


def _width(arr, kind, w):
    return arr.shape[-1] if kind in ("full", "3d") else w


def mm_block(name, T, tm, lhs, wts, extras, outs, epilogue, order="ji", after=(), jgroup=1):
    nl, nw, ne = len(lhs), len(wts), len(extras)
    ni = T // tm
    wo = outs[0][1]

    def body(*refs):
        l = refs[:nl]
        w = refs[nl:nl + nw * jgroup]
        e = refs[nl + nw * jgroup:nl + nw * jgroup + ne]
        o = refs[nl + nw * jgroup + ne + len(after):]
        for u in range(jgroup):
            cols = slice(None) if jgroup == 1 else pl.ds(u * wo, wo)
            prods = []
            for k, (_, li, tr) in enumerate(wts):
                a = l[li][...]
                wk = w[k * jgroup + u][...]
                if tr:
                    prods.append(lax.dot_general(a, wk, _NT, preferred_element_type=F32))
                else:
                    prods.append(jnp.dot(a, wk, preferred_element_type=F32))
            res = epilogue(prods, [r[:, cols].astype(F32) for r in e])
            for r, val in zip(o, res):
                r[:, cols] = val.astype(r.dtype)

    def blocked(kind, w, off):
        if jgroup == 1:
            return _bspec(kind, tm, w, off, order)
        assert kind == "col" and off % jgroup == 0
        return _bspec(kind, tm, w * jgroup, off // jgroup, order)

    in_specs = [_bspec(k, tm, _width(a, k, w), off, order) for (a, k, w, off) in lhs]
    in_specs += [_wspec(w, order, lambda j, u=u: j * jgroup + u) for (w, _, _) in wts for u in range(jgroup)]
    in_specs += [blocked(k, _width(a, k, w), off) for (a, k, w, off) in extras]
    in_specs += [_ANY] * len(after)
    out_specs, out_shape = [], []
    for (kind, w, dt) in outs:
        out_specs.append(blocked(kind, w, 0))
        if kind == "3d":
            out_shape.append(jax.ShapeDtypeStruct((N_DEV, T, w), dt))
        else:
            out_shape.append(jax.ShapeDtypeStruct((T, N_DEV * w), dt))
    args = [a for (a, _, _, _) in lhs] + [w for (w, _, _) in wts for _ in range(jgroup)]
    args += [a for (a, _, _, _) in extras] + list(after)
    nj = N_DEV // jgroup
    return pl.pallas_call(
        body, name=name, grid=(nj, ni) if order == "ji" else (ni, nj), in_specs=in_specs,
        out_specs=out_specs, out_shape=out_shape, compiler_params=_params(("parallel", "parallel")))(*args)


def mm_reduce_j(name, T, tm, pairs, out_w, out_dtype, res=None, scale=1.0, after=(), jstep=1):
    terms = [(p, u) for u in range(jstep) for p in pairs]
    nt = len(terms)
    nj = N_DEV // jstep
    ni = T // tm

    def body(*refs):
        xs = refs[:nt]
        ws = refs[nt:2 * nt]
        rest = refs[2 * nt:len(refs) - 2 - len(after)] + refs[len(refs) - 2:]
        if res is not None:
            res_ref, o_ref, acc = rest
        else:
            o_ref, acc = rest
        j = pl.program_id(1)

        @pl.when(j == 0)
        def _():
            acc[...] = jnp.zeros_like(acc)

        tot = None
        for k, (p, _) in enumerate(terms):
            if p[5]:
                d = lax.dot_general(xs[k][...], ws[k][...], _NT, preferred_element_type=F32)
            else:
                d = jnp.dot(xs[k][...], ws[k][...], preferred_element_type=F32)
            tot = d if tot is None else tot + d
        acc[...] += tot

        @pl.when(j == nj - 1)
        def _():
            if res is not None:
                o_ref[...] = (res_ref[...] + scale * acc[...]).astype(o_ref.dtype)
            else:
                o_ref[...] = acc[...].astype(o_ref.dtype)

    def jmap(u):
        return lambda j: j * jstep + u

    in_specs = [_bspec(p[1], tm, _width(p[0], p[1], p[2]), p[3], "ij", jmap(u)) for (p, u) in terms]
    in_specs += [_wspec(p[4], "ij", jmap(u)) for (p, u) in terms]
    args = [p[0] for (p, _) in terms] + [p[4] for (p, _) in terms]
    if res is not None:
        in_specs.append(pl.BlockSpec((tm, out_w), lambda i, j: (i, 0)))
        args.append(res)
    in_specs += [_ANY] * len(after)
    args += list(after)
    return pl.pallas_call(
        body, name=name, grid=(ni, nj), in_specs=in_specs,
        out_specs=pl.BlockSpec((tm, out_w), lambda i, j: (i, 0)),
        out_shape=jax.ShapeDtypeStruct((T, out_w), out_dtype),
        scratch_shapes=[pltpu.VMEM((tm, out_w), F32)],
        compiler_params=_params(("parallel", "arbitrary")))(*args)


def mm_reduce_i(name, T, tm, a, b, after=()):
    ni = T // tm
    rows = _width(a[0], a[1], a[2])
    cols = _width(b[0], b[1], b[2])

    def body(a_ref, b_ref, *rest):
        o_ref, acc = rest[len(after):]
        i = pl.program_id(1)

        @pl.when(i == 0)
        def _():
            acc[...] = jnp.zeros_like(acc)

        acc[...] += lax.dot_general(a_ref[...], b_ref[...], _TN, preferred_element_type=F32)

        @pl.when(i == ni - 1)
        def _():
            o_ref[...] = acc[...].astype(o_ref.dtype)

    return pl.pallas_call(
        body, name=name, grid=(N_DEV, ni),
        in_specs=[_bspec(a[1], tm, rows, a[3], "ji"), _bspec(b[1], tm, cols, b[3], "ji")] + [_ANY] * len(after),
        out_specs=pl.BlockSpec((None, rows, cols), lambda j, i: (j, 0, 0)),
        out_shape=jax.ShapeDtypeStruct((N_DEV, rows, cols), CDT),
        scratch_shapes=[pltpu.VMEM((rows, cols), F32)],
        compiler_params=_params(("parallel", "arbitrary")))(a[0], b[0], *after)


def _rms_bwd_math(xv, g, dy):
    r = lax.rsqrt(jnp.mean(xv * xv, axis=-1, keepdims=True) + EPS)
    xn = xv * r
    dxn = dy * g
    dx = r * (dxn - xn * jnp.mean(dxn * xn, axis=-1, keepdims=True))
    dg = jnp.sum(dy * xn, axis=0, keepdims=True)
    return dx, dg


def rmsnorm_fwd(name, x, g, tm, after=()):
    T, D = x.shape

    def body(x_ref, g_ref, *rest):
        o_ref = rest[-1]
        xv = x_ref[...]
        r = lax.rsqrt(jnp.mean(xv * xv, axis=-1, keepdims=True) + EPS)
        o_ref[...] = (xv * r * g_ref[...]).astype(o_ref.dtype)

    return pl.pallas_call(
        body, name=name, grid=(T // tm,),
        in_specs=[pl.BlockSpec((tm, D), lambda i: (i, 0)), pl.BlockSpec((1, D), lambda i: (0, 0))]
        + [_ANY] * len(after),
        out_specs=pl.BlockSpec((tm, D), lambda i: (i, 0)),
        out_shape=jax.ShapeDtypeStruct((T, D), CDT),
        compiler_params=_params(("parallel",)))(x, g, *after)


def rmsnorm_bwd(name, x, g, dh, dres, cscale, tm, after=()):
    T, D = x.shape

    def body(x_ref, g_ref, dh_ref, dres_ref, *rest):
        dx_ref, dxc_ref, dg_ref = rest[len(after):]
        i = pl.program_id(0)
        dx, dg = _rms_bwd_math(x_ref[...], g_ref[...], dh_ref[...].astype(F32))
        dx = dres_ref[...] + dx
        dx_ref[...] = dx
        dxc_ref[...] = (cscale * dx).astype(dxc_ref.dtype)

        @pl.when(i == 0)
        def _():
            dg_ref[...] = jnp.zeros_like(dg_ref)

        dg_ref[...] += dg

    row = pl.BlockSpec((tm, D), lambda i: (i, 0))
    vec = pl.BlockSpec((1, D), lambda i: (0, 0))
    return pl.pallas_call(
        body, name=name, grid=(T // tm,), in_specs=[row, vec, row, row] + [_ANY] * len(after),
        out_specs=[row, row, vec],
        out_shape=[jax.ShapeDtypeStruct((T, D), F32), jax.ShapeDtypeStruct((T, D), CDT),
                   jax.ShapeDtypeStruct((1, D), F32)],
        compiler_params=_params(("arbitrary",)))(x, g, dh, dres, *after)


def loss_head(x, g, tgt, tm):
    T, D = x.shape

    def body(x_ref, g_ref, t_ref, dx_ref, dxc_ref, dg_ref, loss_ref):
        i = pl.program_id(0)
        xv = x_ref[...]
        gv = g_ref[...]
        r = lax.rsqrt(jnp.mean(xv * xv, axis=-1, keepdims=True) + EPS)
        err = xv * r * gv - t_ref[...]
        part = jnp.sum(jnp.mean(err * err, axis=-1, keepdims=True), axis=0, keepdims=True)
        dx, dg = _rms_bwd_math(xv, gv, err / D)
        dx_ref[...] = dx
        dxc_ref[...] = (0.5 * dx).astype(dxc_ref.dtype)

        @pl.when(i == 0)
        def _():
            dg_ref[...] = jnp.zeros_like(dg_ref)
            loss_ref[...] = jnp.zeros_like(loss_ref)

        dg_ref[...] += dg
        loss_ref[...] += jnp.broadcast_to(0.5 * part, loss_ref.shape)

    row = pl.BlockSpec((tm, D), lambda i: (i, 0))
    vec = pl.BlockSpec((1, D), lambda i: (0, 0))
    return pl.pallas_call(
        body, name="loss_head", grid=(T // tm,), in_specs=[row, vec, row],
        out_specs=[row, row, vec, pl.BlockSpec((1, LANE), lambda i: (0, 0))],
        out_shape=[jax.ShapeDtypeStruct((T, D), F32), jax.ShapeDtypeStruct((T, D), CDT),
                   jax.ShapeDtypeStruct((1, D), F32), jax.ShapeDtypeStruct((1, LANE), F32)],
        compiler_params=_params(("arbitrary",)))(x, g, tgt)


def _skew_rows(z, left):
    tq, kw = z.shape
    row = lax.broadcasted_iota(jnp.int32, (tq, kw), 0)
    s = 1
    while s < tq:
        z = jnp.where((row & s) != 0, pltpu.roll(z, kw - s if left else s, 1), z)
        s *= 2
    return z


REL_HI = BAND + MAX_REL_DIST
REL_LO = BAND - MAX_REL_DIST


def attn_bias(rel_bias, tq):
    H = rel_bias.shape[0]
    kw = BAND + tq
    by_skew = jnp.concatenate(
        [jnp.broadcast_to(rel_bias[:, N_REL - 1:], (H, REL_LO)), rel_bias[:, ::-1],
         jnp.broadcast_to(rel_bias[:, :1], (H, kw - REL_HI - 1))], axis=1).reshape(H, 1, kw)

    def body(t_ref, o_ref):
        t = t_ref[...]
        qi = lax.broadcasted_iota(jnp.int32, (tq, kw), 0)
        kj = lax.broadcasted_iota(jnp.int32, (tq, kw), 1)
        b = _skew_rows(jnp.broadcast_to(t, (tq, kw)), left=False)
        b = jnp.where(kj < qi, t[:, 0:1], b)
        qc = qi // CHUNK
        kc = kj // CHUNK - N_PREV_CHUNKS
        valid = (kc <= qc) & (kc >= qc - N_PREV_CHUNKS)
        o_ref[...] = jnp.where(valid, b, NEG)

    return pl.pallas_call(
        body, name="attn_bias", grid=(H,),
        in_specs=[pl.BlockSpec((None, 1, kw), lambda h: (h, 0, 0))],
        out_specs=pl.BlockSpec((None, tq, kw), lambda h: (h, 0, 0)),
        out_shape=jax.ShapeDtypeStruct((H, tq, kw), F32),
        compiler_params=_params(("parallel",)))(by_skew)


def attn_bias_grad(dst, tq):
    H = dst.shape[0]
    kw = BAND + tq

    def body(d_ref, o_ref):
        z = _skew_rows(d_ref[...], left=True)
        qi = lax.broadcasted_iota(jnp.int32, (tq, kw), 0)
        kj = lax.broadcasted_iota(jnp.int32, (tq, kw), 1)
        wrapped = kj + qi >= kw
        c = jnp.sum(jnp.where(wrapped, 0.0, z), axis=0, keepdims=True)
        cw = jnp.sum(jnp.sum(jnp.where(wrapped, z, 0.0), axis=0, keepdims=True), axis=1, keepdims=True)
        lane = lax.broadcasted_iota(jnp.int32, (1, kw), 1)
        ahead = jnp.sum(jnp.where(lane >= REL_HI, c, 0.0), axis=1, keepdims=True)
        behind = jnp.sum(jnp.where(lane <= REL_LO, c, 0.0), axis=1, keepdims=True) + cw
        o_ref[...] = jnp.where(lane == REL_HI, ahead, jnp.where(lane == REL_LO, behind, c))

    by_skew = pl.pallas_call(
        body, name="attn_bias_grad", grid=(H,),
        in_specs=[pl.BlockSpec((None, tq, kw), lambda h: (h, 0, 0))],
        out_specs=pl.BlockSpec((None, 1, kw), lambda h: (h, 0, 0)),
        out_shape=jax.ShapeDtypeStruct((H, 1, kw), F32),
        compiler_params=_params(("parallel",)))(dst)
    return by_skew[:, 0, REL_LO:REL_HI + 1][:, ::-1]


def _attn_scores(q, kpad, bm_ref, start, kw):
    k = kpad[pl.ds(start, kw), :]
    s = lax.dot_general(q, k, _NT, preferred_element_type=F32) * (HEAD_DIM ** -0.5) + bm_ref[...]
    col = lax.broadcasted_iota(jnp.int32, s.shape, 1)
    s = jnp.where(col < BAND - start, NEG, s)
    m = jnp.max(s, axis=-1, keepdims=True)
    e = jnp.exp(s - m)
    return e, 1.0 / jnp.sum(e, axis=-1, keepdims=True), k


def _fill_padded(pad_ref, src_ref, T):
    pad_ref[pl.ds(0, BAND), :] = jnp.zeros((BAND, HEAD_DIM), pad_ref.dtype)
    pad_ref[pl.ds(BAND, T), :] = src_ref[...]


def attn_fwd(proj, biasm, A, tq):
    T = proj.shape[0]
    H = A // HEAD_DIM
    kw = BAND + tq

    def body(q_ref, k_ref, v_ref, bm_ref, o_ref, kpad, vpad):
        qi = pl.program_id(1)

        @pl.when(qi == 0)
        def _():
            _fill_padded(kpad, k_ref, T)
            _fill_padded(vpad, v_ref, T)

        start = pl.multiple_of(qi * tq, tq)
        e, rinv, _ = _attn_scores(q_ref[...], kpad, bm_ref, start, kw)
        v = vpad[pl.ds(start, kw), :]
        o_ref[...] = (jnp.dot(e.astype(CDT), v, preferred_element_type=F32) * rinv).astype(o_ref.dtype)

    return pl.pallas_call(
        body, name="attn_fwd", grid=(H, T // tq),
        in_specs=[pl.BlockSpec((tq, HEAD_DIM), lambda h, i: (i, h)),
                  pl.BlockSpec((T, HEAD_DIM), lambda h, i: (0, H + h)),
                  pl.BlockSpec((T, HEAD_DIM), lambda h, i: (0, 2 * H + h)),
                  pl.BlockSpec((None, tq, kw), lambda h, i: (h, 0, 0))],
        out_specs=pl.BlockSpec((tq, HEAD_DIM), lambda h, i: (i, h)),
        out_shape=jax.ShapeDtypeStruct((T, A), CDT),
        scratch_shapes=[pltpu.VMEM((BAND + T, HEAD_DIM), CDT), pltpu.VMEM((BAND + T, HEAD_DIM), CDT)],
        compiler_params=_params(("parallel", "arbitrary")))(proj, proj, proj, biasm)


def attn_bwd(proj, biasm, datt, A, tq):
    T = proj.shape[0]
    H = A // HEAD_DIM
    kw = BAND + tq
    nq = T // tq
    scale = HEAD_DIM ** -0.5

    def body(q_ref, k_ref, v_ref, bm_ref, do_ref, dq_ref, dk_ref, dv_ref, dst_ref,
             kpad, vpad, dkacc, dvacc):
        qi = pl.program_id(1)

        @pl.when(qi == 0)
        def _():
            _fill_padded(kpad, k_ref, T)
            _fill_padded(vpad, v_ref, T)
            dkacc[...] = jnp.zeros_like(dkacc)
            dvacc[...] = jnp.zeros_like(dvacc)
            dst_ref[...] = jnp.zeros_like(dst_ref)

        start = pl.multiple_of(qi * tq, tq)
        q = q_ref[...]
        e, rinv, k = _attn_scores(q, kpad, bm_ref, start, kw)
        p = e * rinv
        v = vpad[pl.ds(start, kw), :]
        do = do_ref[...]
        dp = lax.dot_general(do, v, _NT, preferred_element_type=F32)
        ds = p * (dp - jnp.sum(dp * p, axis=-1, keepdims=True))
        dst_ref[...] += ds
        dsb = ds.astype(CDT)
        dq_ref[...] = (jnp.dot(dsb, k, preferred_element_type=F32) * scale).astype(dq_ref.dtype)
        dkacc[pl.ds(start, kw), :] += lax.dot_general(dsb, q, _TN, preferred_element_type=F32) * scale
        dvacc[pl.ds(start, kw), :] += lax.dot_general(p.astype(CDT), do, _TN, preferred_element_type=F32)

        @pl.when(qi == nq - 1)
        def _():
            dk_ref[...] = dkacc[pl.ds(BAND, T), :].astype(dk_ref.dtype)
            dv_ref[...] = dvacc[pl.ds(BAND, T), :].astype(dv_ref.dtype)

    blk = pl.BlockSpec((tq, HEAD_DIM), lambda h, i: (i, h))
    col = pl.BlockSpec((T, HEAD_DIM), lambda h, i: (0, h))
    bias = pl.BlockSpec((None, tq, kw), lambda h, i: (h, 0, 0))
    return pl.pallas_call(
        body, name="attn_bwd", grid=(H, nq),
        in_specs=[blk,
                  pl.BlockSpec((T, HEAD_DIM), lambda h, i: (0, H + h)),
                  pl.BlockSpec((T, HEAD_DIM), lambda h, i: (0, 2 * H + h)),
                  bias, blk],
        out_specs=[blk, col, col, bias],
        out_shape=[jax.ShapeDtypeStruct((T, A), CDT), jax.ShapeDtypeStruct((T, A), CDT),
                   jax.ShapeDtypeStruct((T, A), CDT), jax.ShapeDtypeStruct((H, tq, kw), F32)],
        scratch_shapes=[pltpu.VMEM((BAND + T, HEAD_DIM), CDT), pltpu.VMEM((BAND + T, HEAD_DIM), CDT),
                        pltpu.VMEM((BAND + T, HEAD_DIM), F32), pltpu.VMEM((BAND + T, HEAD_DIM), F32)],
        compiler_params=_params(("parallel", "arbitrary")))(proj, proj, proj, biasm, datt)


def _retention_tables(T, H, blk):
    half = HEAD_DIM // 2
    inv = 1.0 / (ROPE_BASE ** (jnp.arange(0, HEAD_DIM, 2, dtype=F32) / HEAD_DIM))
    ang = jnp.arange(T, dtype=F32)[:, None] * inv[None, :]
    cos, sin = jnp.cos(ang), jnp.sin(ang)
    rc = jnp.concatenate([cos, cos], axis=1)
    rs = jnp.concatenate([-sin, sin], axis=1)
    assert rc.shape == (T, 2 * half)
    log_g = jnp.log(1.0 - 2.0 ** (-5.0 - jnp.arange(H, dtype=F32)))[:, None, None]
    idx = jnp.arange(blk, dtype=F32)
    n, m = idx[:, None], idx[None, :]
    same = (n // CHUNK) == (m // CHUNK)
    earlier = (m // CHUNK) < (n // CHUNK)
    dist = jnp.where(same, jnp.abs(n - m), n - m)[None]
    dmat = jnp.where((same | earlier)[None], jnp.exp(log_g * dist), 0.0)
    ones = jnp.ones((1, 1, HEAD_DIM), F32)
    qd = jnp.exp(log_g * (idx[None, :, None] + 1.0)) * ones
    kd = jnp.exp(log_g * (blk - 1.0 - idx[None, :, None])) * ones
    cd = jnp.exp(log_g * blk) * jnp.ones((1, 8, HEAD_DIM), F32)
    return rc, rs, dmat, qd, kd, cd


def _rot(v, rc, rs):
    return v * rc + pltpu.roll(v, HEAD_DIM // 2, 1) * rs


def _rot_bwd(dv, rc, rs):
    return dv * rc + pltpu.roll(dv * rs, HEAD_DIM // 2, 1)


def ret_fwd(proj, tables, A, blk):
    T = proj.shape[0]
    H = A // HEAD_DIM
    nb = T // blk
    hp = RET_HEADS_PER_STEP
    rc, rs, dmat, qd, kd, cd = tables
    scale = HEAD_DIM ** -0.5

    def body(q_ref, k_ref, v_ref, g_ref, rc_ref, rs_ref, d_ref, qd_ref, kd_ref, cd_ref,
             y_ref, o_ref, st_ref, state):
        b = pl.program_id(1)

        @pl.when(b == 0)
        def _():
            state[...] = jnp.zeros_like(state)

        c, s = rc_ref[...], rs_ref[...]
        for u in range(hp):
            cols = pl.ds(u * HEAD_DIM, HEAD_DIM)
            qs = (_rot(q_ref[:, cols].astype(F32), c, s) * scale).astype(CDT)
            kr = _rot(k_ref[:, cols].astype(F32), c, s)
            v = v_ref[:, cols]
            sb = state[u].astype(CDT)
            a = lax.dot_general(qs, kr.astype(CDT), _NT, preferred_element_type=F32) * d_ref[u]
            o = jnp.dot(a.astype(CDT), v, preferred_element_type=F32)
            o = o + jnp.dot(qs, sb, preferred_element_type=F32) * qd_ref[u]
            st_ref[u] = sb
            state[u] = state[u] * cd_ref[u, 0:1, :] + lax.dot_general(
                (kr * kd_ref[u]).astype(CDT), v, _TN, preferred_element_type=F32)
            o_ref[:, cols] = o
            on = o * lax.rsqrt(jnp.mean(o * o, axis=-1, keepdims=True) + EPS)
            g = g_ref[:, cols].astype(F32)
            y_ref[:, cols] = (g * _sigmoid(g) * on).astype(y_ref.dtype)

    w = hp * HEAD_DIM

    def pj(off):
        return pl.BlockSpec((blk, w), lambda h, i: (i, off * H // hp + h))

    tok = pl.BlockSpec((blk, HEAD_DIM), lambda h, i: (i, 0))
    out = pl.BlockSpec((blk, w), lambda h, i: (i, h))

    def per_head(r, c):
        return pl.BlockSpec((hp, r, c), lambda h, i: (h, 0, 0))

    return pl.pallas_call(
        body, name="ret_fwd", grid=(H // hp, nb),
        in_specs=[pj(3), pj(4), pj(5), pj(6), tok, tok, per_head(blk, blk),
                  per_head(blk, HEAD_DIM), per_head(blk, HEAD_DIM), per_head(8, HEAD_DIM)],
        out_specs=[out, out, pl.BlockSpec((hp, None, HEAD_DIM, HEAD_DIM), lambda h, i: (h, i, 0, 0))],
        out_shape=[jax.ShapeDtypeStruct((T, A), CDT), jax.ShapeDtypeStruct((T, A), F32),
                   jax.ShapeDtypeStruct((H, nb, HEAD_DIM, HEAD_DIM), CDT)],
        scratch_shapes=[pltpu.VMEM((hp, HEAD_DIM, HEAD_DIM), F32)],
        compiler_params=_params(("parallel", "arbitrary")))(
            proj, proj, proj, proj, rc, rs, dmat, qd, kd, cd)


def ret_bwd(proj, tables, o_raw, states, dy, A, blk):
    T = proj.shape[0]
    H = A // HEAD_DIM
    nb = T // blk
    hp = RET_HEADS_PER_STEP
    rc, rs, dmat, qd, kd, cd = tables
    scale = HEAD_DIM ** -0.5

    def body(q_ref, k_ref, v_ref, g_ref, rc_ref, rs_ref, d_ref, qd_ref, kd_ref, cd_ref,
             o_ref, st_ref, dy_ref, dq_ref, dk_ref, dv_ref, dg_ref, dstate):
        b = pl.program_id(1)

        @pl.when(b == 0)
        def _():
            dstate[...] = jnp.zeros_like(dstate)

        c, s = rc_ref[...], rs_ref[...]
        for u in range(hp):
            cols = pl.ds(u * HEAD_DIM, HEAD_DIM)
            qs = (_rot(q_ref[:, cols].astype(F32), c, s) * scale).astype(CDT)
            kr = _rot(k_ref[:, cols].astype(F32), c, s)
            krb = kr.astype(CDT)
            kdb = (kr * kd_ref[u]).astype(CDT)
            v = v_ref[:, cols]
            dmat_v = d_ref[u]
            a = lax.dot_general(qs, krb, _NT, preferred_element_type=F32) * dmat_v

            o = o_ref[:, cols]
            r = lax.rsqrt(jnp.mean(o * o, axis=-1, keepdims=True) + EPS)
            on = o * r
            g = g_ref[:, cols].astype(F32)
            sg = _sigmoid(g)
            dyv = dy_ref[:, cols].astype(F32)
            dg_ref[:, cols] = (dyv * on * (sg * (1.0 + g * (1.0 - sg)))).astype(dg_ref.dtype)
            don = dyv * (g * sg)
            do = r * (don - on * jnp.mean(don * on, axis=-1, keepdims=True))
            dob = do.astype(CDT)
            doq = (do * qd_ref[u]).astype(CDT)
            dsb = dstate[u].astype(CDT)

            dv = lax.dot_general(a.astype(CDT), dob, _TN, preferred_element_type=F32)
            dv = dv + jnp.dot(kdb, dsb, preferred_element_type=F32)
            dv_ref[:, cols] = dv.astype(dv_ref.dtype)
            dpb = (lax.dot_general(dob, v, _NT, preferred_element_type=F32) * dmat_v).astype(CDT)
            dqs = jnp.dot(dpb, krb, preferred_element_type=F32)
            dqs = dqs + lax.dot_general(doq, st_ref[u], _NT, preferred_element_type=F32)
            dkr = lax.dot_general(dpb, qs, _TN, preferred_element_type=F32)
            dkr = dkr + lax.dot_general(v, dsb, _NT, preferred_element_type=F32) * kd_ref[u]
            dstate[u] = dstate[u] * cd_ref[u, 0:1, :] + lax.dot_general(
                qs, doq, _TN, preferred_element_type=F32)
            dq_ref[:, cols] = _rot_bwd(dqs * scale, c, s).astype(dq_ref.dtype)
            dk_ref[:, cols] = _rot_bwd(dkr, c, s).astype(dk_ref.dtype)

    w = hp * HEAD_DIM

    def pj(off):
        return pl.BlockSpec((blk, w), lambda h, i: (nb - 1 - i, off * H // hp + h))

    tok = pl.BlockSpec((blk, HEAD_DIM), lambda h, i: (nb - 1 - i, 0))
    out = pl.BlockSpec((blk, w), lambda h, i: (nb - 1 - i, h))

    def per_head(r, c):
        return pl.BlockSpec((hp, r, c), lambda h, i: (h, 0, 0))

    shp = jax.ShapeDtypeStruct((T, A), CDT)
    return pl.pallas_call(
        body, name="ret_bwd", grid=(H // hp, nb),
        in_specs=[pj(3), pj(4), pj(5), pj(6), tok, tok, per_head(blk, blk),
                  per_head(blk, HEAD_DIM), per_head(blk, HEAD_DIM), per_head(8, HEAD_DIM),
                  out, pl.BlockSpec((hp, None, HEAD_DIM, HEAD_DIM), lambda h, i: (h, nb - 1 - i, 0, 0)),
                  out],
        out_specs=[out, out, out, out], out_shape=[shp, shp, shp, shp],
        scratch_shapes=[pltpu.VMEM((hp, HEAD_DIM, HEAD_DIM), F32)],
        compiler_params=_params(("parallel", "arbitrary")))(
            proj, proj, proj, proj, rc, rs, dmat, qd, kd, cd, o_raw, states, dy)


def _mesh_pos():
    return lax.axis_index("x"), lax.axis_index("y"), lax.axis_index("c")


def _flat(pos):
    return 4 * pos[0] + 2 * pos[1] + pos[2]


_HBM = pl.BlockSpec(memory_space=pltpu.HBM)


def cast_shard(name, w, rows_p, cols_p, me_arr, after=()):
    r, c = w.shape
    tr = _row_tile(math.gcd(r, rows_p), 256)
    nr = r // tr

    def body(me_ref, w_ref, *rest):
        o_ref = rest[-1]
        i = pl.program_id(0)
        o_ref[...] = jnp.zeros_like(o_ref)

        @pl.when(i < nr)
        def _():
            o_ref[:, 0:c] = w_ref[...].astype(o_ref.dtype)

    return pl.pallas_call(
        body, name=name,
        grid_spec=pltpu.PrefetchScalarGridSpec(
            num_scalar_prefetch=1, grid=(rows_p // tr,),
            in_specs=[pl.BlockSpec((tr, c), lambda i, me: (jnp.minimum(i, nr - 1), 0))] + [_ANY] * len(after),
            out_specs=pl.BlockSpec((None, tr, cols_p), lambda i, me: (me[0], i, 0))),
        out_shape=jax.ShapeDtypeStruct((N_DEV, rows_p, cols_p), CDT),
        compiler_params=_params(("arbitrary",)))(me_arr, w, *after)


def exchange_partials(arrays, after):
    n, na = len(arrays), len(after)

    def body(*refs):
        ins, outs = refs[:n], refs[n + na:2 * n + na]
        send_sems, recv_sems, local_sems = refs[2 * n + na:]
        me = _mesh_pos()
        copies, locals_ = [], []
        for t in range(n):
            cp = pltpu.make_async_copy(ins[t], outs[t].at[_flat(me)], local_sems.at[t])
            cp.start()
            locals_.append(cp)
            for k in range(1, N_DEV):
                peer = _peer(me, k)
                send = pltpu.make_async_remote_copy(
                    src_ref=ins[t], dst_ref=outs[t].at[_flat(me)],
                    send_sem=send_sems.at[t, k - 1], recv_sem=recv_sems.at[t, k - 1],
                    device_id=peer, device_id_type=pl.DeviceIdType.MESH)
                send.start()
                recv = pltpu.make_async_remote_copy(
                    src_ref=ins[t], dst_ref=outs[t].at[_flat(peer)],
                    send_sem=send_sems.at[t, k - 1], recv_sem=recv_sems.at[t, k - 1],
                    device_id=peer, device_id_type=pl.DeviceIdType.MESH)
                copies.append((send, recv))
        for send, recv in copies:
            recv.wait_recv()
        for send, recv in copies:
            send.wait_send()
        for cp in locals_:
            cp.wait()

    return pl.pallas_call(
        body, name="exchange_partials",
        in_specs=[_HBM] * n + [_ANY] * na, out_specs=[_HBM] * n,
        out_shape=[jax.ShapeDtypeStruct((N_DEV,) + a.shape, a.dtype) for a in arrays],
        scratch_shapes=[pltpu.SemaphoreType.DMA((n, 7)), pltpu.SemaphoreType.DMA((n, 7)),
                        pltpu.SemaphoreType.DMA((n,))],
        )(*arrays, *after)


_SEM = pl.BlockSpec(memory_space=pltpu.SEMAPHORE)
_ANY = pl.BlockSpec(memory_space=pl.ANY)
_EFFECT = pltpu.SideEffectType.DATAFLOW_SIDE_EFFECTING


def _peer(me, k):
    return tuple(1 - v if bit else v for v, bit in zip(me, (k >> 2, (k >> 1) & 1, k & 1)))


_CHIP_MASKS = (2, 4, 6)

_EXCHANGE_MODES = {"gather": (7, None), "gather_chips": (4, None), "forward": (3, None),
                   "scatter": (7, 7), "scatter_pair": (4, 4), "scatter_chips": (3, 3)}


def _plan(mode, bufs, n, me):
    per = _EXCHANGE_MODES[mode][0]
    sib = _peer(me, 1)
    plan = []
    for t in range(n):
        src_arr, land_arr = bufs[t], bufs[n + t] if _EXCHANGE_MODES[mode][1] else None
        if mode in ("gather", "gather_chips"):
            masks = range(1, N_DEV) if mode == "gather" else (1,) + _CHIP_MASKS
            rows = [(src_arr.at[_flat(me)], src_arr.at[_flat(me)], _peer(me, k), src_arr.at[_flat(_peer(me, k))])
                    for k in masks]
        elif mode == "forward":
            rows = [(src_arr.at[_flat(_peer(me, k))], src_arr.at[_flat(_peer(me, k))], sib,
                     src_arr.at[_flat(_peer(sib, k))]) for k in _CHIP_MASKS]
        elif mode == "scatter":
            rows = [(src_arr.at[_flat(_peer(me, k))], land_arr.at[k - 1], _peer(me, k), land_arr.at[k - 1])
                    for k in range(1, N_DEV)]
        elif mode == "scatter_pair":
            rows = [(src_arr.at[_flat(_peer(me, q + 1))], land_arr.at[qi], sib, land_arr.at[qi])
                    for qi, q in enumerate((0,) + _CHIP_MASKS)]
        else:
            assert mode == "scatter_chips"
            rows = [(src_arr.at[qi + 1], land_arr.at[qi], _peer(me, q), land_arr.at[qi])
                    for qi, q in enumerate(_CHIP_MASKS)]
        assert len(rows) == per
        plan += [(t * per + s,) + row for s, row in enumerate(rows)]
    return plan


def exchange_start(name, arrays, mode, after):
    n, na = len(arrays), len(after)
    per, slots = _EXCHANGE_MODES[mode]
    bufs = list(arrays)
    if slots:
        bufs += [lax.empty((slots,) + a.shape[1:], a.dtype) for a in arrays]
    nb = len(bufs)

    def body(*refs):
        send_sems, recv_sems = refs[nb + na], refs[nb + na + 1]
        token = refs[-1]
        for s, src, dst, dev, _ in _plan(mode, refs[:nb], n, _mesh_pos()):
            pltpu.make_async_remote_copy(
                src_ref=src, dst_ref=dst, send_sem=send_sems.at[s], recv_sem=recv_sems.at[s],
                device_id=dev, device_id_type=pl.DeviceIdType.MESH).start()
        token[...] = jnp.zeros_like(token)

    out_shape = [pltpu.SemaphoreType.DMA((n * per,)), pltpu.SemaphoreType.DMA((n * per,))]
    out_shape += [pltpu.HBM(a.shape, a.dtype) for a in bufs]
    out_shape.append(jax.ShapeDtypeStruct((8, LANE), F32))
    args = [pltpu.with_memory_space_constraint(a, pltpu.HBM) for a in bufs] + list(after)
    outs = pl.pallas_call(
        body, name=name, out_shape=out_shape,
        in_specs=[_HBM] * nb + [_ANY] * na,
        out_specs=[_SEM, _SEM] + [_HBM] * nb + [pl.BlockSpec(memory_space=pltpu.VMEM)],
        input_output_aliases={i: 2 + i for i in range(nb)},
        compiler_params=pltpu.CompilerParams(has_side_effects=_EFFECT))(*args)
    return outs[0], outs[1], list(outs[2:2 + nb]), outs[-1]


def exchange_wait(name, started, mode, after):
    send_sems, recv_sems, bufs, _ = started
    nb, na = len(bufs), len(after)
    n = nb // 2 if _EXCHANGE_MODES[mode][1] else nb

    def body(*refs):
        send_sems_ref, recv_sems_ref = refs[nb], refs[nb + 1]
        for s, src, _, dev, land in _plan(mode, refs[:nb], n, _mesh_pos()):
            cp = pltpu.make_async_remote_copy(
                src_ref=src, dst_ref=land, send_sem=send_sems_ref.at[s], recv_sem=recv_sems_ref.at[s],
                device_id=dev, device_id_type=pl.DeviceIdType.MESH)
            cp.wait_send()
            cp.wait_recv()

    outs = pl.pallas_call(
        body, name=name, out_shape=[pltpu.HBM(a.shape, a.dtype) for a in bufs],
        in_specs=[_HBM] * nb + [_SEM, _SEM] + [_ANY] * na, out_specs=[_HBM] * nb,
        input_output_aliases={i: i for i in range(nb)},
        compiler_params=pltpu.CompilerParams(has_side_effects=_EFFECT))(
            *bufs, send_sems, recv_sems, *after)
    return list(outs)


def pair_sum(name, own, landed, blocks):
    _, r, c = own.shape
    tr = _row_tile(r, 256)

    def body(idx_ref, o_ref, l_ref, s_ref):
        s_ref[...] = (o_ref[...].astype(F32) + l_ref[...].astype(F32)).astype(s_ref.dtype)

    blk = pl.BlockSpec((None, tr, c), lambda q, i, idx: (q, i, 0))
    return pl.pallas_call(
        body, name=name,
        grid_spec=pltpu.PrefetchScalarGridSpec(
            num_scalar_prefetch=1, grid=(4, r // tr),
            in_specs=[pl.BlockSpec((None, tr, c), lambda q, i, idx: (idx[q], i, 0)), blk],
            out_specs=blk),
        out_shape=jax.ShapeDtypeStruct((4, r, c), own.dtype),
        compiler_params=_params(("parallel", "parallel")))(blocks, own, landed)


def _adamw_math(w, g, m, v):
    m = ADAM_B1 * m + (1.0 - ADAM_B1) * g
    v = ADAM_B2 * v + (1.0 - ADAM_B2) * (g * g)
    m_hat = m / (1.0 - ADAM_B1 ** ADAM_STEP)
    v_hat = v / (1.0 - ADAM_B2 ** ADAM_STEP)
    delta = -ADAM_LR * (m_hat / (jnp.sqrt(v_hat) + ADAM_EPS) + ADAM_WD * w)
    return delta, m, v


def reduce_adamw(name, land, w, m, v, tr, own=None, me_arr=None):
    R, C = w.shape
    S, _, Cp = land.shape

    def body(*refs):
        if own is not None:
            _, own_ref, l_ref, w_ref, m_ref, v_ref, g_ref, d_ref, nm_ref, nv_ref = refs
            g = own_ref[:, 0:C].astype(F32)
            first = 0
        else:
            l_ref, w_ref, m_ref, v_ref, g_ref, d_ref, nm_ref, nv_ref = refs
            g = l_ref[0, :, 0:C].astype(F32)
            first = 1
        for s in range(first, S):
            g = g + l_ref[s, :, 0:C].astype(F32)
        delta, nm, nv = _adamw_math(w_ref[...], g, m_ref[...], v_ref[...])
        g_ref[...] = g
        d_ref[...] = delta
        nm_ref[...] = nm
        nv_ref[...] = nv

    shp = jax.ShapeDtypeStruct((R, C), F32)
    if own is None:
        blk = pl.BlockSpec((tr, C), lambda i: (i, 0))
        return pl.pallas_call(
            body, name=name, grid=(R // tr,),
            in_specs=[pl.BlockSpec((S, tr, Cp), lambda i: (0, i, 0)), blk, blk, blk],
            out_specs=[blk, blk, blk, blk], out_shape=[shp, shp, shp, shp],
            compiler_params=_params(("parallel",)))(land, w, m, v)
    blk = pl.BlockSpec((tr, C), lambda i, me: (i, 0))
    return pl.pallas_call(
        body, name=name,
        grid_spec=pltpu.PrefetchScalarGridSpec(
            num_scalar_prefetch=1, grid=(R // tr,),
            in_specs=[pl.BlockSpec((None, tr, Cp), lambda i, me: (me[0], i, 0)),
                      pl.BlockSpec((S, tr, Cp), lambda i, me: (0, i, 0)), blk, blk, blk],
            out_specs=[blk, blk, blk, blk]),
        out_shape=[shp, shp, shp, shp],
        compiler_params=_params(("parallel",)))(me_arr, own, land, w, m, v)


def _row_tile(r, cap):
    t = min(r, cap)
    while r % t or t % 8:
        t -= 8
    return t


def kernel(x, norm_ffn1_g, ffn1_w_gate, ffn1_w_up, ffn1_w_down, norm_mix_g, w_in, rel_bias, w_out_att, w_out_ret, w_out, norm_ffn2_g, ffn2_w_gate, ffn2_w_up, ffn2_w_down, norm_final_g, loss_target, m_norm_ffn1_g, m_ffn1_w_gate, m_ffn1_w_up, m_ffn1_w_down, m_norm_mix_g, m_w_in, m_rel_bias, m_w_out_att, m_w_out_ret, m_w_out, m_norm_ffn2_g, m_ffn2_w_gate, m_ffn2_w_up, m_ffn2_w_down, m_norm_final_g, v_norm_ffn1_g, v_ffn1_w_gate, v_ffn1_w_up, v_ffn1_w_down, v_norm_mix_g, v_w_in, v_rel_bias, v_w_out_att, v_w_out_ret, v_w_out, v_norm_ffn2_g, v_ffn2_w_gate, v_ffn2_w_up, v_ffn2_w_down, v_norm_final_g):
    T, D = x.shape[1], x.shape[2]
    A = w_out_att.shape[1]
    H = A // HEAD_DIM
    nf = ffn1_w_gate.shape[2]
    nfp = _round_up(nf, LANE)
    nin = w_in.shape[2]
    nd = w_out.shape[1]
    assert nin % LANE == 0 and nd % LANE == 0 and (7 * A) % nd == 0 and T % ATT_TQ == 0
    tm = min(512, T)
    tw = min(2048, T)
    tw_in = min(1024, T)
    tn = min(256, T)
    x0 = x[0]
    tgt = loss_target[0]

    me_arr = (4 * lax.axis_index("x") + 2 * lax.axis_index("y") + lax.axis_index("c")).astype(jnp.int32).reshape(1)

    def slot(tag, w, after, rows_p=None):
        return cast_shard("cast_" + tag, w[0], rows_p or w.shape[1], w.shape[2], me_arr, after)

    def slot_t(tag, w, after):
        return cast_shard("cast_" + tag, jnp.transpose(w[0]), nfp, w.shape[1], me_arr, after)

    groups = [("wg1", [("wg1", ffn1_w_gate, slot_t)], True), ("wu1", [("wu1", ffn1_w_up, slot_t)], True),
              ("wd1", [("wd1", ffn1_w_down, functools.partial(slot, rows_p=nfp))], True),
              ("win", [("win", w_in, slot)], True),
              ("wout", [("woa", w_out_att, slot), ("wor", w_out_ret, slot), ("wo", w_out, slot)], False),
              ("wgu2", [("wg2", ffn2_w_gate, slot_t), ("wu2", ffn2_w_up, slot_t)], True),
              ("wd2", [("wd2", ffn2_w_down, functools.partial(slot, rows_p=nfp))], True)]
    ag_started = {}
    order = []
    for tag, members, two_level in groups:
        mode = "gather_chips" if two_level else "gather"
        started = exchange_start("ag_start_" + tag, [make(nm, w, order) for nm, w, make in members], mode, order)
        ag_started[tag] = (started, mode)
        order = [started[3]]

    passing = {}

    def begin_pass(tag, after):
        started, mode = ag_started[tag]
        got = exchange_wait("ag_wait_" + tag, started, mode, [after])
        passing[tag] = exchange_start("ag_pass_" + tag, got, "forward", [])
        return passing[tag][3]

    def gathered(tag, after):
        started, mode = ag_started[tag]
        if mode == "gather":
            return exchange_wait("ag_wait_" + tag, started, mode, [after])
        if tag not in passing:
            begin_pass(tag, after)
        return exchange_wait("ag_passed_" + tag, passing[tag], "forward", [after])

    xi, yi, ci = lax.axis_index("x"), lax.axis_index("y"), lax.axis_index("c")
    my_side = jnp.stack([4 * (1 - xi if q & 4 else xi) + 2 * (1 - yi if q & 2 else yi) + ci
                         for q in (0,) + _CHIP_MASKS]).astype(jnp.int32)
    first_block = jnp.zeros((1,), jnp.int32)

    def swiglu(prods, _):
        a, b = prods
        return a, b, a * _sigmoid(a) * b

    def ffn_fwd(tag, xin, g, get_wgu, get_wd, after=()):
        h = rmsnorm_fwd(tag + "_norm", xin, g, tn, after)
        if isinstance(get_wgu, tuple):
            Wg, = get_wgu[0](h)
            a, = mm_block(tag + "_gate", T, tm, [(h, "full", D, 0)], [(Wg, 0, True)], [], [("3d", nfp, CDT)],
                          lambda p, _: p)
            Wu, = get_wgu[1](a)
            b, mid = mm_block(tag + "_up", T, tm, [(h, "full", D, 0)], [(Wu, 0, True)], [(a, "3d", nfp, 0)],
                              [("3d", nfp, CDT)] * 2, lambda p, ex: (p[0], ex[0] * _sigmoid(ex[0]) * p[0]))
        else:
            Wg, Wu = get_wgu(h)
            a, b, mid = mm_block(tag + "_up", T, tm, [(h, "full", D, 0)], [(Wg, 0, True), (Wu, 0, True)],
                                 [], [("3d", nfp, CDT)] * 3, swiglu)
        Wd, = get_wd(mid)
        xo = mm_reduce_j(tag + "_down", T, tm, [(mid, "3d", nfp, 0, Wd, False)], D, F32, res=xin, scale=0.5,
                         jstep=2)
        return h, a, b, mid, xo, (Wg, Wu, Wd)

    h1, a1, b1, mid1, x1, (Wg1, Wu1, Wd1) = ffn_fwd(
        "ffn1", x0, norm_ffn1_g, (lambda h: gathered("wg1", h), lambda a: gathered("wu1", a)),
        lambda mid: gathered("wd1", mid), after=order)
    h2 = rmsnorm_fwd("mix_norm", x1, norm_mix_g, tn)
    Win, = gathered("win", h2)
    proj, = mm_block("in_proj", T, tm, [(h2, "full", D, 0)], [(Win, 0, False)], [], [("col", nin, CDT)],
                     lambda p, _: p)
    biasm = attn_bias(rel_bias[0], ATT_TQ)
    att = attn_fwd(proj, biasm, A, ATT_TQ)
    tables = _retention_tables(T, H, RET_BLK)
    retg, ret_raw, states = ret_fwd(proj, tables, A, RET_BLK)
    Woa, Wor, Wo = gathered("wout", retg)
    goff = 7 * A // nd

    def merge(prods, ex):
        ba, br = prods
        ga, gr = ex
        return ba, br, _sigmoid(ga) * ba + _sigmoid(gr) * br

    ba, br, merged = mm_block(
        "branches", T, tm, [(att, "full", A, 0), (retg, "full", A, 0)], [(Woa, 0, False), (Wor, 1, False)],
        [(proj, "col", nd, goff), (proj, "col", nd, goff + N_DEV)], [("col", nd, CDT)] * 3, merge, order="ij", jgroup=4)
    x2 = mm_reduce_j("out_proj", T, tm, [(merged, "col", nd, 0, Wo, False)], D, F32, res=x1, scale=1.0,
                     jstep=N_DEV, after=[begin_pass("wgu2", merged)])
    h3, a2, b2, mid2, x3, (Wg2, Wu2, Wd2) = ffn_fwd(
        "ffn2", x2, norm_ffn2_g, lambda h: gathered("wgu2", h), lambda mid: gathered("wd2", mid))

    dx3, dx3h, dgf, loss_part = loss_head(x3, norm_final_g.reshape(1, D), tgt, tn)

    def swiglu_bwd(prods, ex):
        dm, = prods
        a, b = ex
        sg = _sigmoid(a)
        return dm * b * (sg * (1.0 + a * (1.0 - sg))), dm * (a * sg)

    def ffn_bwd(tag, dxh, h, a, b, mid, Wg, Wu, Wd, two_level=False):
        da, db = mm_block(tag + "_down_bwd", T, tm, [(dxh, "full", D, 0)], [(Wd, 0, True)],
                          [(a, "3d", nfp, 0), (b, "3d", nfp, 0)], [("3d", nfp, CDT)] * 2, swiglu_bwd)

        def up_bwd(after):
            return mm_reduce_j(tag + "_up_bwd", T, tm,
                               [(da, "3d", nfp, 0, Wg, False), (db, "3d", nfp, 0, Wu, False)],
                               D, CDT, after=after, jstep=2)

        if not two_level:
            dWd = mm_reduce_i(tag + "_dwd", T, tw, (mid, "3d", nfp, 0), (dxh, "full", D, 0))
            dWg = mm_reduce_i(tag + "_dwg", T, tw, (da, "3d", nfp, 0), (h, "full", D, 0))
            dWu = mm_reduce_i(tag + "_dwu", T, tw, (db, "3d", nfp, 0), (h, "full", D, 0))
            sent = exchange_start("rs_start_" + tag, [dWg, dWu, dWd], "scatter", [])
            return up_bwd([sent[3]]), sent

        def swap(nm, grad, after):
            return exchange_start("rs_pair_start_%s_%s" % (tag, nm), [grad], "scatter_pair", after)

        def to_chips(nm, swapping, after):
            own, landed = exchange_wait("rs_pair_wait_%s_%s" % (tag, nm), swapping, "scatter_pair", [after])
            sums = pair_sum("%s_pair_sum_%s" % (tag, nm), own, landed, my_side)
            return exchange_start("rs_chips_start_%s_%s" % (tag, nm), [sums], "scatter_chips", [])

        dWd = mm_reduce_i(tag + "_dwd", T, tw, (mid, "3d", nfp, 0), (dxh, "full", D, 0))
        swap_d = swap("d", dWd, [])
        dWg = mm_reduce_i(tag + "_dwg", T, tw, (da, "3d", nfp, 0), (h, "full", D, 0), after=[swap_d[3]])
        sent_d = to_chips("d", swap_d, dWg)
        swap_g = swap("g", dWg, [sent_d[3]])
        dWu = mm_reduce_i(tag + "_dwu", T, tw, (db, "3d", nfp, 0), (h, "full", D, 0), after=[swap_g[3]])
        sent_g = to_chips("g", swap_g, dWu)
        swap_u = swap("u", dWu, [sent_g[3]])
        dh = up_bwd([swap_u[3]])
        sent_u = to_chips("u", swap_u, dh)
        return dh, [sent_g, sent_u, sent_d]

    dh3, sent_ffn2 = ffn_bwd("ffn2", dx3h, h3, a2, b2, mid2, Wg2, Wu2, Wd2)
    dx2, dx2c, dg2 = rmsnorm_bwd("ffn2_norm_bwd", x2, norm_ffn2_g, dh3, dx3, 1.0, tn)

    def merge_bwd(prods, ex):
        dmg, = prods
        ba_, br_, ga, gr = ex
        sa, sr = _sigmoid(ga), _sigmoid(gr)
        return dmg * sa, dmg * sr, dmg * ba_ * sa * (1.0 - sa), dmg * br_ * sr * (1.0 - sr)

    dba, dbr, dga, dgr = mm_block(
        "out_proj_bwd", T, tm, [(dx2c, "full", D, 0)], [(Wo, 0, True)],
        [(ba, "col", nd, 0), (br, "col", nd, 0), (proj, "col", nd, goff), (proj, "col", nd, goff + N_DEV)],
        [("col", nd, CDT)] * 4, merge_bwd, order="ij", jgroup=4)
    dWo = mm_reduce_i("dwo", T, tw, (merged, "col", nd, 0), (dx2c, "full", D, 0))
    dWoa = mm_reduce_i("dwoa", T, tw, (att, "full", A, 0), (dba, "col", nd, 0))
    dWor = mm_reduce_i("dwor", T, tw, (retg, "full", A, 0), (dbr, "col", nd, 0))
    sent_mix = exchange_start("rs_start_mix", [dWoa, dWor, dWo], "scatter", [])
    datt = mm_reduce_j("att_out_bwd", T, tm, [(dba, "col", nd, 0, Woa, True)], A, CDT, after=[sent_mix[3]],
                       jstep=N_DEV)
    dretg = mm_reduce_j("ret_out_bwd", T, tm, [(dbr, "col", nd, 0, Wor, True)], A, CDT, jstep=N_DEV)
    dq_r, dk_r, dv_r, dg_r = ret_bwd(proj, tables, ret_raw, states, dretg, A, RET_BLK)
    dq_a, dk_a, dv_a, dst = attn_bwd(proj, biasm, datt, A, ATT_TQ)
    dbias = jnp.pad(attn_bias_grad(dst, ATT_TQ), ((0, 0), (0, N_REL_PAD - N_REL)))
    dproj = jnp.concatenate([dq_a, dk_a, dv_a, dq_r, dk_r, dv_r, dg_r, dga, dgr], axis=1)
    dWin = mm_reduce_i("dwin", T, tw_in, (h2, "full", D, 0), (dproj, "col", nin, 0))
    sent_win = exchange_start("rs_start_win", [dWin], "scatter", [])
    dh2 = mm_reduce_j("in_proj_bwd", T, tm, [(dproj, "col", nin, 0, Win, True)], D, CDT, after=[sent_win[3]],
                      jstep=2)
    dx1, dx1h, dgm = rmsnorm_bwd("mix_norm_bwd", x1, norm_mix_g, dh2, dx2, 0.5, tn)
    dh1, sent_ffn1 = ffn_bwd("ffn1", dx1h, h1, a1, b1, mid1, Wg1, Wu1, Wd1, two_level=True)
    grad_x, _, dg1 = rmsnorm_bwd("ffn1_norm_bwd", x0, norm_ffn1_g, dh1, dx1, 1.0, tn, after=[sent_ffn1[1][3]])

    dgains = jnp.concatenate([dg1, dgm, dg2, dgf, jnp.zeros((4, D), F32)], axis=0)

    def upd(name, own, land, w, m, v, own_block=me_arr, transposed=False):
        w2, m2, v2 = [jnp.transpose(t[0]) if transposed else t[0] for t in (w, m, v)]
        outs = reduce_adamw(name, land, w2, m2, v2, _row_tile(w2.shape[0], 256), own=own, me_arr=own_block)
        return [jnp.transpose(o)[None] if transposed else o[None] for o in outs]

    res = {}
    oWg2, oWu2, oWd2, lWg2, lWu2, lWd2 = exchange_wait("rs_wait_ffn2", sent_ffn2, "scatter", [grad_x])
    res["ffn2_w_gate"] = upd("adamw_wg2", oWg2, lWg2, ffn2_w_gate, m_ffn2_w_gate, v_ffn2_w_gate, transposed=True)
    res["ffn2_w_up"] = upd("adamw_wu2", oWu2, lWu2, ffn2_w_up, m_ffn2_w_up, v_ffn2_w_up, transposed=True)
    res["ffn2_w_down"] = upd("adamw_wd2", oWd2, lWd2, ffn2_w_down, m_ffn2_w_down, v_ffn2_w_down)
    oWoa, oWor, oWo, lWoa, lWor, lWo = exchange_wait("rs_wait_mix", sent_mix, "scatter", [res["ffn2_w_down"][1]])
    res["w_out_att"] = upd("adamw_woa", oWoa, lWoa, w_out_att, m_w_out_att, v_w_out_att)
    res["w_out_ret"] = upd("adamw_wor", oWor, lWor, w_out_ret, m_w_out_ret, v_w_out_ret)
    res["w_out"] = upd("adamw_wo", oWo, lWo, w_out, m_w_out, v_w_out)
    oWin, lWin = exchange_wait("rs_wait_win", sent_win, "scatter", [res["w_out"][1]])
    res["w_in"] = upd("adamw_win", oWin, lWin, w_in, m_w_in, v_w_in)
    lgains, lbias = exchange_partials([dgains, dbias], [res["w_in"][1]])
    (oWg1, lWg1), (oWu1, lWu1), (oWd1, lWd1) = [
        exchange_wait("rs_wait_ffn1_" + nm, started, "scatter_chips", [lgains])
        for nm, started in zip("gud", sent_ffn1)]
    res["ffn1_w_gate"] = upd("adamw_wg1", oWg1, lWg1, ffn1_w_gate, m_ffn1_w_gate, v_ffn1_w_gate, first_block, transposed=True)
    res["ffn1_w_up"] = upd("adamw_wu1", oWu1, lWu1, ffn1_w_up, m_ffn1_w_up, v_ffn1_w_up, first_block, transposed=True)
    res["ffn1_w_down"] = upd("adamw_wd1", oWd1, lWd1, ffn1_w_down, m_ffn1_w_down, v_ffn1_w_down, first_block)

    def stack_gains(a, b, c_, d):
        return jnp.concatenate([a, b, c_, d.reshape(1, D), jnp.zeros((4, D), F32)], axis=0)

    gw = stack_gains(norm_ffn1_g, norm_mix_g, norm_ffn2_g, norm_final_g)
    gm = stack_gains(m_norm_ffn1_g, m_norm_mix_g, m_norm_ffn2_g, m_norm_final_g)
    gv = stack_gains(v_norm_ffn1_g, v_norm_mix_g, v_norm_ffn2_g, v_norm_final_g)
    gains = reduce_adamw("adamw_gains", lgains, gw, gm, gv, 8)

    def padb(t):
        return jnp.pad(t[0], ((0, 0), (0, N_REL_PAD - N_REL)))

    bias = [o[:, :N_REL][None] for o in
            reduce_adamw("adamw_bias", lbias, padb(rel_bias), padb(m_rel_bias), padb(v_rel_bias), H)]
    res["norm_ffn1_g"] = [o[0:1] for o in gains]
    res["norm_mix_g"] = [o[1:2] for o in gains]
    res["norm_ffn2_g"] = [o[2:3] for o in gains]
    res["norm_final_g"] = [o[3] for o in gains]
    res["rel_bias"] = bias

    loss = lax.psum(loss_part[0, 0], MESH_AXES)
    names = ["norm_ffn1_g", "ffn1_w_gate", "ffn1_w_up", "ffn1_w_down", "norm_mix_g", "w_in", "rel_bias",
             "w_out_att", "w_out_ret", "w_out", "norm_ffn2_g", "ffn2_w_gate", "ffn2_w_up", "ffn2_w_down",
             "norm_final_g"]
    out = [loss, grad_x[None]]
    for k in range(4):
        out += [res[nm][k] for nm in names]
    return tuple(out)
```

```python
import functools
import math

import jax
import jax.numpy as jnp
import numpy as np
from jax import lax
from jax.experimental import pallas as pl
from jax.experimental.pallas import tpu as pltpu

F32 = jnp.float32
CDT = jnp.bfloat16

N_DEV = 8
CHUNK = 64
N_PREV_CHUNKS = 8
BAND = N_PREV_CHUNKS * CHUNK
HEAD_DIM = 128
MAX_REL_DIST = 128
N_REL = 2 * MAX_REL_DIST + 1
N_REL_PAD = 384
ROPE_BASE = 10000.0
EPS = 1e-6
NEG = -1e30
LANE = 128
ATT_TQ = 256
RET_BLK = 256
RET_HEADS_PER_STEP = 4
VMEM_LIMIT = 48 * 1024 * 1024

ADAM_LR = 0.001
ADAM_B1 = 0.9
ADAM_B2 = 0.999
ADAM_EPS = 1e-08
ADAM_WD = 0.01
ADAM_STEP = 10

MESH_AXES = ("x", "y", "c")
_NT = (((1,), (1,)), ((), ()))
_TN = (((0,), (0,)), ((), ()))


def _round_up(v, m):
    return (v + m - 1) // m * m


def _params(sem=None):
    return pltpu.CompilerParams(dimension_semantics=sem, vmem_limit_bytes=VMEM_LIMIT)


def _sigmoid(v):
    return 0.5 * (jnp.tanh(0.5 * v) + 1.0)


def _bspec(kind, tm, w, off, order, jmap=lambda j: j):
    def wrap(f):
        if order == "ji":
            return lambda j, i: f(i, jmap(j))
        return lambda i, j: f(i, jmap(j))
    if kind == "full":
        return pl.BlockSpec((tm, w), wrap(lambda i, j: (i, 0)))
    if kind == "col":
        return pl.BlockSpec((tm, w), wrap(lambda i, j: (i, j + off)))
    assert kind == "3d"
    return pl.BlockSpec((None, tm, w), wrap(lambda i, j: (j, i, 0)))


def _wspec(w, order, jmap=lambda j: j):
    if order == "ji":
        return pl.BlockSpec((None,) + w.shape[1:], lambda j, i: (jmap(j), 0, 0))
    return pl.BlockSpec((None,) + w.shape[1:], lambda i, j: (jmap(j), 0, 0))


def _width(arr, kind, w):
    return arr.shape[-1] if kind in ("full", "3d") else w


def mm_block(name, T, tm, lhs, wts, extras, outs, epilogue, order="ji", after=(), jgroup=1):
    nl, nw, ne = len(lhs), len(wts), len(extras)
    ni = T // tm
    wo = outs[0][1]

    def body(*refs):
        l = refs[:nl]
        w = refs[nl:nl + nw * jgroup]
        e = refs[nl + nw * jgroup:nl + nw * jgroup + ne]
        o = refs[nl + nw * jgroup + ne + len(after):]
        for u in range(jgroup):
            cols = slice(None) if jgroup == 1 else pl.ds(u * wo, wo)
            prods = []
            for k, (_, li, tr) in enumerate(wts):
                a = l[li][...]
                wk = w[k * jgroup + u][...]
                if tr:
                    prods.append(lax.dot_general(a, wk, _NT, preferred_element_type=F32))
                else:
                    prods.append(jnp.dot(a, wk, preferred_element_type=F32))
            res = epilogue(prods, [r[:, cols].astype(F32) for r in e])
            for r, val in zip(o, res):
                r[:, cols] = val.astype(r.dtype)

    def blocked(kind, w, off):
        if jgroup == 1:
            return _bspec(kind, tm, w, off, order)
        assert kind == "col" and off % jgroup == 0
        return _bspec(kind, tm, w * jgroup, off // jgroup, order)

    in_specs = [_bspec(k, tm, _width(a, k, w), off, order) for (a, k, w, off) in lhs]
    in_specs += [_wspec(w, order, lambda j, u=u: j * jgroup + u) for (w, _, _) in wts for u in range(jgroup)]
    in_specs += [blocked(k, _width(a, k, w), off) for (a, k, w, off) in extras]
    in_specs += [_ANY] * len(after)
    out_specs, out_shape = [], []
    for (kind, w, dt) in outs:
        out_specs.append(blocked(kind, w, 0))
        if kind == "3d":
            out_shape.append(jax.ShapeDtypeStruct((N_DEV, T, w), dt))
        else:
            out_shape.append(jax.ShapeDtypeStruct((T, N_DEV * w), dt))
    args = [a for (a, _, _, _) in lhs] + [w for (w, _, _) in wts for _ in range(jgroup)]
    args += [a for (a, _, _, _) in extras] + list(after)
    nj = N_DEV // jgroup
    return pl.pallas_call(
        body, name=name, grid=(nj, ni) if order == "ji" else (ni, nj), in_specs=in_specs,
        out_specs=out_specs, out_shape=out_shape, compiler_params=_params(("parallel", "parallel")))(*args)


def mm_reduce_j(name, T, tm, pairs, out_w, out_dtype, res=None, scale=1.0, after=(), jstep=1):
    terms = [(p, u) for u in range(jstep) for p in pairs]
    nt = len(terms)
    nj = N_DEV // jstep
    ni = T // tm

    def body(*refs):
        xs = refs[:nt]
        ws = refs[nt:2 * nt]
        rest = refs[2 * nt:len(refs) - 2 - len(after)] + refs[len(refs) - 2:]
        if res is not None:
            res_ref, o_ref, acc = rest
        else:
            o_ref, acc = rest
        j = pl.program_id(1)

        @pl.when(j == 0)
        def _():
            acc[...] = jnp.zeros_like(acc)

        tot = None
        for k, (p, _) in enumerate(terms):
            if p[5]:
                d = lax.dot_general(xs[k][...], ws[k][...], _NT, preferred_element_type=F32)
            else:
                d = jnp.dot(xs[k][...], ws[k][...], preferred_element_type=F32)
            tot = d if tot is None else tot + d
        acc[...] += tot

        @pl.when(j == nj - 1)
        def _():
            if res is not None:
                o_ref[...] = (res_ref[...] + scale * acc[...]).astype(o_ref.dtype)
            else:
                o_ref[...] = acc[...].astype(o_ref.dtype)

    def jmap(u):
        return lambda j: j * jstep + u

    in_specs = [_bspec(p[1], tm, _width(p[0], p[1], p[2]), p[3], "ij", jmap(u)) for (p, u) in terms]
    in_specs += [_wspec(p[4], "ij", jmap(u)) for (p, u) in terms]
    args = [p[0] for (p, _) in terms] + [p[4] for (p, _) in terms]
    if res is not None:
        in_specs.append(pl.BlockSpec((tm, out_w), lambda i, j: (i, 0)))
        args.append(res)
    in_specs += [_ANY] * len(after)
    args += list(after)
    return pl.pallas_call(
        body, name=name, grid=(ni, nj), in_specs=in_specs,
        out_specs=pl.BlockSpec((tm, out_w), lambda i, j: (i, 0)),
        out_shape=jax.ShapeDtypeStruct((T, out_w), out_dtype),
        scratch_shapes=[pltpu.VMEM((tm, out_w), F32)],
        compiler_params=_params(("parallel", "arbitrary")))(*args)


def mm_reduce_i(name, T, tm, a, b, after=()):
    ni = T // tm
    rows = _width(a[0], a[1], a[2])
    cols = _width(b[0], b[1], b[2])

    def body(a_ref, b_ref, *rest):
        o_ref, acc = rest[len(after):]
        i = pl.program_id(1)

        @pl.when(i == 0)
        def _():
            acc[...] = jnp.zeros_like(acc)

        acc[...] += lax.dot_general(a_ref[...], b_ref[...], _TN, preferred_element_type=F32)

        @pl.when(i == ni - 1)
        def _():
            o_ref[...] = acc[...].astype(o_ref.dtype)

    return pl.pallas_call(
        body, name=name, grid=(N_DEV, ni),
        in_specs=[_bspec(a[1], tm, rows, a[3], "ji"), _bspec(b[1], tm, cols, b[3], "ji")] + [_ANY] * len(after),
        out_specs=pl.BlockSpec((None, rows, cols), lambda j, i: (j, 0, 0)),
        out_shape=jax.ShapeDtypeStruct((N_DEV, rows, cols), CDT),
        scratch_shapes=[pltpu.VMEM((rows, cols), F32)],
        compiler_params=_params(("parallel", "arbitrary")))(a[0], b[0], *after)


def _rms_bwd_math(xv, g, dy):
    r = lax.rsqrt(jnp.mean(xv * xv, axis=-1, keepdims=True) + EPS)
    xn = xv * r
    dxn = dy * g
    dx = r * (dxn - xn * jnp.mean(dxn * xn, axis=-1, keepdims=True))
    dg = jnp.sum(dy * xn, axis=0, keepdims=True)
    return dx, dg


def rmsnorm_fwd(name, x, g, tm, after=()):
    T, D = x.shape

    def body(x_ref, g_ref, *rest):
        o_ref = rest[-1]
        xv = x_ref[...]
        r = lax.rsqrt(jnp.mean(xv * xv, axis=-1, keepdims=True) + EPS)
        o_ref[...] = (xv * r * g_ref[...]).astype(o_ref.dtype)

    return pl.pallas_call(
        body, name=name, grid=(T // tm,),
        in_specs=[pl.BlockSpec((tm, D), lambda i: (i, 0)), pl.BlockSpec((1, D), lambda i: (0, 0))]
        + [_ANY] * len(after),
        out_specs=pl.BlockSpec((tm, D), lambda i: (i, 0)),
        out_shape=jax.ShapeDtypeStruct((T, D), CDT),
        compiler_params=_params(("parallel",)))(x, g, *after)


def rmsnorm_bwd(name, x, g, dh, dres, cscale, tm, after=()):
    T, D = x.shape

    def body(x_ref, g_ref, dh_ref, dres_ref, *rest):
        dx_ref, dxc_ref, dg_ref = rest[len(after):]
        i = pl.program_id(0)
        dx, dg = _rms_bwd_math(x_ref[...], g_ref[...], dh_ref[...].astype(F32))
        dx = dres_ref[...] + dx
        dx_ref[...] = dx
        dxc_ref[...] = (cscale * dx).astype(dxc_ref.dtype)

        @pl.when(i == 0)
        def _():
            dg_ref[...] = jnp.zeros_like(dg_ref)

        dg_ref[...] += dg

    row = pl.BlockSpec((tm, D), lambda i: (i, 0))
    vec = pl.BlockSpec((1, D), lambda i: (0, 0))
    return pl.pallas_call(
        body, name=name, grid=(T // tm,), in_specs=[row, vec, row, row] + [_ANY] * len(after),
        out_specs=[row, row, vec],
        out_shape=[jax.ShapeDtypeStruct((T, D), F32), jax.ShapeDtypeStruct((T, D), CDT),
                   jax.ShapeDtypeStruct((1, D), F32)],
        compiler_params=_params(("arbitrary",)))(x, g, dh, dres, *after)


def loss_head(x, g, tgt, tm):
    T, D = x.shape

    def body(x_ref, g_ref, t_ref, dx_ref, dxc_ref, dg_ref, loss_ref):
        i = pl.program_id(0)
        xv = x_ref[...]
        gv = g_ref[...]
        r = lax.rsqrt(jnp.mean(xv * xv, axis=-1, keepdims=True) + EPS)
        err = xv * r * gv - t_ref[...]
        part = jnp.sum(jnp.mean(err * err, axis=-1, keepdims=True), axis=0, keepdims=True)
        dx, dg = _rms_bwd_math(xv, gv, err / D)
        dx_ref[...] = dx
        dxc_ref[...] = (0.5 * dx).astype(dxc_ref.dtype)

        @pl.when(i == 0)
        def _():
            dg_ref[...] = jnp.zeros_like(dg_ref)
            loss_ref[...] = jnp.zeros_like(loss_ref)

        dg_ref[...] += dg
        loss_ref[...] += jnp.broadcast_to(0.5 * part, loss_ref.shape)

    row = pl.BlockSpec((tm, D), lambda i: (i, 0))
    vec = pl.BlockSpec((1, D), lambda i: (0, 0))
    return pl.pallas_call(
        body, name="loss_head", grid=(T // tm,), in_specs=[row, vec, row],
        out_specs=[row, row, vec, pl.BlockSpec((1, LANE), lambda i: (0, 0))],
        out_shape=[jax.ShapeDtypeStruct((T, D), F32), jax.ShapeDtypeStruct((T, D), CDT),
                   jax.ShapeDtypeStruct((1, D), F32), jax.ShapeDtypeStruct((1, LANE), F32)],
        compiler_params=_params(("arbitrary",)))(x, g, tgt)


def _skew_rows(z, left):
    tq, kw = z.shape
    row = lax.broadcasted_iota(jnp.int32, (tq, kw), 0)
    s = 1
    while s < tq:
        z = jnp.where((row & s) != 0, pltpu.roll(z, kw - s if left else s, 1), z)
        s *= 2
    return z


REL_HI = BAND + MAX_REL_DIST
REL_LO = BAND - MAX_REL_DIST


def attn_bias(rel_bias, tq):
    H = rel_bias.shape[0]
    kw = BAND + tq
    by_skew = jnp.concatenate(
        [jnp.broadcast_to(rel_bias[:, N_REL - 1:], (H, REL_LO)), rel_bias[:, ::-1],
         jnp.broadcast_to(rel_bias[:, :1], (H, kw - REL_HI - 1))], axis=1).reshape(H, 1, kw)

    def body(t_ref, o_ref):
        t = t_ref[...]
        qi = lax.broadcasted_iota(jnp.int32, (tq, kw), 0)
        kj = lax.broadcasted_iota(jnp.int32, (tq, kw), 1)
        b = _skew_rows(jnp.broadcast_to(t, (tq, kw)), left=False)
        b = jnp.where(kj < qi, t[:, 0:1], b)
        qc = qi // CHUNK
        kc = kj // CHUNK - N_PREV_CHUNKS
        valid = (kc <= qc) & (kc >= qc - N_PREV_CHUNKS)
        o_ref[...] = jnp.where(valid, b, NEG)

    return pl.pallas_call(
        body, name="attn_bias", grid=(H,),
        in_specs=[pl.BlockSpec((None, 1, kw), lambda h: (h, 0, 0))],
        out_specs=pl.BlockSpec((None, tq, kw), lambda h: (h, 0, 0)),
        out_shape=jax.ShapeDtypeStruct((H, tq, kw), F32),
        compiler_params=_params(("parallel",)))(by_skew)


def attn_bias_grad(dst, tq):
    H = dst.shape[0]
    kw = BAND + tq

    def body(d_ref, o_ref):
        z = _skew_rows(d_ref[...], left=True)
        qi = lax.broadcasted_iota(jnp.int32, (tq, kw), 0)
        kj = lax.broadcasted_iota(jnp.int32, (tq, kw), 1)
        wrapped = kj + qi >= kw
        c = jnp.sum(jnp.where(wrapped, 0.0, z), axis=0, keepdims=True)
        cw = jnp.sum(jnp.sum(jnp.where(wrapped, z, 0.0), axis=0, keepdims=True), axis=1, keepdims=True)
        lane = lax.broadcasted_iota(jnp.int32, (1, kw), 1)
        ahead = jnp.sum(jnp.where(lane >= REL_HI, c, 0.0), axis=1, keepdims=True)
        behind = jnp.sum(jnp.where(lane <= REL_LO, c, 0.0), axis=1, keepdims=True) + cw
        o_ref[...] = jnp.where(lane == REL_HI, ahead, jnp.where(lane == REL_LO, behind, c))

    by_skew = pl.pallas_call(
        body, name="attn_bias_grad", grid=(H,),
        in_specs=[pl.BlockSpec((None, tq, kw), lambda h: (h, 0, 0))],
        out_specs=pl.BlockSpec((None, 1, kw), lambda h: (h, 0, 0)),
        out_shape=jax.ShapeDtypeStruct((H, 1, kw), F32),
        compiler_params=_params(("parallel",)))(dst)
    return by_skew[:, 0, REL_LO:REL_HI + 1][:, ::-1]


def _attn_scores(q, kpad, bm_ref, start, kw):
    k = kpad[pl.ds(start, kw), :]
    s = lax.dot_general(q, k, _NT, preferred_element_type=F32) * (HEAD_DIM ** -0.5) + bm_ref[...]
    col = lax.broadcasted_iota(jnp.int32, s.shape, 1)
    s = jnp.where(col < BAND - start, NEG, s)
    m = jnp.max(s, axis=-1, keepdims=True)
    e = jnp.exp(s - m)
    return e, 1.0 / jnp.sum(e, axis=-1, keepdims=True), k


def _fill_padded(pad_ref, src_ref, T):
    pad_ref[pl.ds(0, BAND), :] = jnp.zeros((BAND, HEAD_DIM), pad_ref.dtype)
    pad_ref[pl.ds(BAND, T), :] = src_ref[...]


def attn_fwd(proj, biasm, A, tq):
    T = proj.shape[0]
    H = A // HEAD_DIM
    kw = BAND + tq

    def body(q_ref, k_ref, v_ref, bm_ref, o_ref, kpad, vpad):
        qi = pl.program_id(1)

        @pl.when(qi == 0)
        def _():
            _fill_padded(kpad, k_ref, T)
            _fill_padded(vpad, v_ref, T)

        start = pl.multiple_of(qi * tq, tq)
        e, rinv, _ = _attn_scores(q_ref[...], kpad, bm_ref, start, kw)
        v = vpad[pl.ds(start, kw), :]
        o_ref[...] = (jnp.dot(e.astype(CDT), v, preferred_element_type=F32) * rinv).astype(o_ref.dtype)

    return pl.pallas_call(
        body, name="attn_fwd", grid=(H, T // tq),
        in_specs=[pl.BlockSpec((tq, HEAD_DIM), lambda h, i: (i, h)),
                  pl.BlockSpec((T, HEAD_DIM), lambda h, i: (0, H + h)),
                  pl.BlockSpec((T, HEAD_DIM), lambda h, i: (0, 2 * H + h)),
                  pl.BlockSpec((None, tq, kw), lambda h, i: (h, 0, 0))],
        out_specs=pl.BlockSpec((tq, HEAD_DIM), lambda h, i: (i, h)),
        out_shape=jax.ShapeDtypeStruct((T, A), CDT),
        scratch_shapes=[pltpu.VMEM((BAND + T, HEAD_DIM), CDT), pltpu.VMEM((BAND + T, HEAD_DIM), CDT)],
        compiler_params=_params(("parallel", "arbitrary")))(proj, proj, proj, biasm)


def attn_bwd(proj, biasm, datt, A, tq):
    T = proj.shape[0]
    H = A // HEAD_DIM
    kw = BAND + tq
    nq = T // tq
    scale = HEAD_DIM ** -0.5

    def body(q_ref, k_ref, v_ref, bm_ref, do_ref, dq_ref, dk_ref, dv_ref, dst_ref,
             kpad, vpad, dkacc, dvacc):
        qi = pl.program_id(1)

        @pl.when(qi == 0)
        def _():
            _fill_padded(kpad, k_ref, T)
            _fill_padded(vpad, v_ref, T)
            dkacc[...] = jnp.zeros_like(dkacc)
            dvacc[...] = jnp.zeros_like(dvacc)
            dst_ref[...] = jnp.zeros_like(dst_ref)

        start = pl.multiple_of(qi * tq, tq)
        q = q_ref[...]
        e, rinv, k = _attn_scores(q, kpad, bm_ref, start, kw)
        p = e * rinv
        v = vpad[pl.ds(start, kw), :]
        do = do_ref[...]
        dp = lax.dot_general(do, v, _NT, preferred_element_type=F32)
        ds = p * (dp - jnp.sum(dp * p, axis=-1, keepdims=True))
        dst_ref[...] += ds
        dsb = ds.astype(CDT)
        dq_ref[...] = (jnp.dot(dsb, k, preferred_element_type=F32) * scale).astype(dq_ref.dtype)
        dkacc[pl.ds(start, kw), :] += lax.dot_general(dsb, q, _TN, preferred_element_type=F32) * scale
        dvacc[pl.ds(start, kw), :] += lax.dot_general(p.astype(CDT), do, _TN, preferred_element_type=F32)

        @pl.when(qi == nq - 1)
        def _():
            dk_ref[...] = dkacc[pl.ds(BAND, T), :].astype(dk_ref.dtype)
            dv_ref[...] = dvacc[pl.ds(BAND, T), :].astype(dv_ref.dtype)

    blk = pl.BlockSpec((tq, HEAD_DIM), lambda h, i: (i, h))
    col = pl.BlockSpec((T, HEAD_DIM), lambda h, i: (0, h))
    bias = pl.BlockSpec((None, tq, kw), lambda h, i: (h, 0, 0))
    return pl.pallas_call(
        body, name="attn_bwd", grid=(H, nq),
        in_specs=[blk,
                  pl.BlockSpec((T, HEAD_DIM), lambda h, i: (0, H + h)),
                  pl.BlockSpec((T, HEAD_DIM), lambda h, i: (0, 2 * H + h)),
                  bias, blk],
        out_specs=[blk, col, col, bias],
        out_shape=[jax.ShapeDtypeStruct((T, A), CDT), jax.ShapeDtypeStruct((T, A), CDT),
                   jax.ShapeDtypeStruct((T, A), CDT), jax.ShapeDtypeStruct((H, tq, kw), F32)],
        scratch_shapes=[pltpu.VMEM((BAND + T, HEAD_DIM), CDT), pltpu.VMEM((BAND + T, HEAD_DIM), CDT),
                        pltpu.VMEM((BAND + T, HEAD_DIM), F32), pltpu.VMEM((BAND + T, HEAD_DIM), F32)],
        compiler_params=_params(("parallel", "arbitrary")))(proj, proj, proj, biasm, datt)


def _retention_tables(T, H, blk):
    half = HEAD_DIM // 2
    inv = 1.0 / (ROPE_BASE ** (jnp.arange(0, HEAD_DIM, 2, dtype=F32) / HEAD_DIM))
    ang = jnp.arange(T, dtype=F32)[:, None] * inv[None, :]
    cos, sin = jnp.cos(ang), jnp.sin(ang)
    rc = jnp.concatenate([cos, cos], axis=1)
    rs = jnp.concatenate([-sin, sin], axis=1)
    assert rc.shape == (T, 2 * half)
    log_g = jnp.log(1.0 - 2.0 ** (-5.0 - jnp.arange(H, dtype=F32)))[:, None, None]
    idx = jnp.arange(blk, dtype=F32)
    n, m = idx[:, None], idx[None, :]
    same = (n // CHUNK) == (m // CHUNK)
    earlier = (m // CHUNK) < (n // CHUNK)
    dist = jnp.where(same, jnp.abs(n - m), n - m)[None]
    dmat = jnp.where((same | earlier)[None], jnp.exp(log_g * dist), 0.0)
    ones = jnp.ones((1, 1, HEAD_DIM), F32)
    qd = jnp.exp(log_g * (idx[None, :, None] + 1.0)) * ones
    kd = jnp.exp(log_g * (blk - 1.0 - idx[None, :, None])) * ones
    cd = jnp.exp(log_g * blk) * jnp.ones((1, 8, HEAD_DIM), F32)
    return rc, rs, dmat, qd, kd, cd


def _rot(v, rc, rs):
    return v * rc + pltpu.roll(v, HEAD_DIM // 2, 1) * rs


def _rot_bwd(dv, rc, rs):
    return dv * rc + pltpu.roll(dv * rs, HEAD_DIM // 2, 1)


def ret_fwd(proj, tables, A, blk):
    T = proj.shape[0]
    H = A // HEAD_DIM
    nb = T // blk
    hp = RET_HEADS_PER_STEP
    rc, rs, dmat, qd, kd, cd = tables
    scale = HEAD_DIM ** -0.5

    def body(q_ref, k_ref, v_ref, g_ref, rc_ref, rs_ref, d_ref, qd_ref, kd_ref, cd_ref,
             y_ref, o_ref, st_ref, state):
        b = pl.program_id(1)

        @pl.when(b == 0)
        def _():
            state[...] = jnp.zeros_like(state)

        c, s = rc_ref[...], rs_ref[...]
        for u in range(hp):
            cols = pl.ds(u * HEAD_DIM, HEAD_DIM)
            qs = (_rot(q_ref[:, cols].astype(F32), c, s) * scale).astype(CDT)
            kr = _rot(k_ref[:, cols].astype(F32), c, s)
            v = v_ref[:, cols]
            sb = state[u].astype(CDT)
            a = lax.dot_general(qs, kr.astype(CDT), _NT, preferred_element_type=F32) * d_ref[u]
            o = jnp.dot(a.astype(CDT), v, preferred_element_type=F32)
            o = o + jnp.dot(qs, sb, preferred_element_type=F32) * qd_ref[u]
            st_ref[u] = sb
            state[u] = state[u] * cd_ref[u, 0:1, :] + lax.dot_general(
                (kr * kd_ref[u]).astype(CDT), v, _TN, preferred_element_type=F32)
            o_ref[:, cols] = o
            on = o * lax.rsqrt(jnp.mean(o * o, axis=-1, keepdims=True) + EPS)
            g = g_ref[:, cols].astype(F32)
            y_ref[:, cols] = (g * _sigmoid(g) * on).astype(y_ref.dtype)

    w = hp * HEAD_DIM

    def pj(off):
        return pl.BlockSpec((blk, w), lambda h, i: (i, off * H // hp + h))

    tok = pl.BlockSpec((blk, HEAD_DIM), lambda h, i: (i, 0))
    out = pl.BlockSpec((blk, w), lambda h, i: (i, h))

    def per_head(r, c):
        return pl.BlockSpec((hp, r, c), lambda h, i: (h, 0, 0))

    return pl.pallas_call(
        body, name="ret_fwd", grid=(H // hp, nb),
        in_specs=[pj(3), pj(4), pj(5), pj(6), tok, tok, per_head(blk, blk),
                  per_head(blk, HEAD_DIM), per_head(blk, HEAD_DIM), per_head(8, HEAD_DIM)],
        out_specs=[out, out, pl.BlockSpec((hp, None, HEAD_DIM, HEAD_DIM), lambda h, i: (h, i, 0, 0))],
        out_shape=[jax.ShapeDtypeStruct((T, A), CDT), jax.ShapeDtypeStruct((T, A), F32),
                   jax.ShapeDtypeStruct((H, nb, HEAD_DIM, HEAD_DIM), CDT)],
        scratch_shapes=[pltpu.VMEM((hp, HEAD_DIM, HEAD_DIM), F32)],
        compiler_params=_params(("parallel", "arbitrary")))(
            proj, proj, proj, proj, rc, rs, dmat, qd, kd, cd)


def ret_bwd(proj, tables, o_raw, states, dy, A, blk):
    T = proj.shape[0]
    H = A // HEAD_DIM
    nb = T // blk
    hp = RET_HEADS_PER_STEP
    rc, rs, dmat, qd, kd, cd = tables
    scale = HEAD_DIM ** -0.5

    def body(q_ref, k_ref, v_ref, g_ref, rc_ref, rs_ref, d_ref, qd_ref, kd_ref, cd_ref,
             o_ref, st_ref, dy_ref, dq_ref, dk_ref, dv_ref, dg_ref, dstate):
        b = pl.program_id(1)

        @pl.when(b == 0)
        def _():
            dstate[...] = jnp.zeros_like(dstate)

        c, s = rc_ref[...], rs_ref[...]
        for u in range(hp):
            cols = pl.ds(u * HEAD_DIM, HEAD_DIM)
            qs = (_rot(q_ref[:, cols].astype(F32), c, s) * scale).astype(CDT)
            kr = _rot(k_ref[:, cols].astype(F32), c, s)
            krb = kr.astype(CDT)
            kdb = (kr * kd_ref[u]).astype(CDT)
            v = v_ref[:, cols]
            dmat_v = d_ref[u]
            a = lax.dot_general(qs, krb, _NT, preferred_element_type=F32) * dmat_v

            o = o_ref[:, cols]
            r = lax.rsqrt(jnp.mean(o * o, axis=-1, keepdims=True) + EPS)
            on = o * r
            g = g_ref[:, cols].astype(F32)
            sg = _sigmoid(g)
            dyv = dy_ref[:, cols].astype(F32)
            dg_ref[:, cols] = (dyv * on * (sg * (1.0 + g * (1.0 - sg)))).astype(dg_ref.dtype)
            don = dyv * (g * sg)
            do = r * (don - on * jnp.mean(don * on, axis=-1, keepdims=True))
            dob = do.astype(CDT)
            doq = (do * qd_ref[u]).astype(CDT)
            dsb = dstate[u].astype(CDT)

            dv = lax.dot_general(a.astype(CDT), dob, _TN, preferred_element_type=F32)
            dv = dv + jnp.dot(kdb, dsb, preferred_element_type=F32)
            dv_ref[:, cols] = dv.astype(dv_ref.dtype)
            dpb = (lax.dot_general(dob, v, _NT, preferred_element_type=F32) * dmat_v).astype(CDT)
            dqs = jnp.dot(dpb, krb, preferred_element_type=F32)
            dqs = dqs + lax.dot_general(doq, st_ref[u], _NT, preferred_element_type=F32)
            dkr = lax.dot_general(dpb, qs, _TN, preferred_element_type=F32)
            dkr = dkr + lax.dot_general(v, dsb, _NT, preferred_element_type=F32) * kd_ref[u]
            dstate[u] = dstate[u] * cd_ref[u, 0:1, :] + lax.dot_general(
                qs, doq, _TN, preferred_element_type=F32)
            dq_ref[:, cols] = _rot_bwd(dqs * scale, c, s).astype(dq_ref.dtype)
            dk_ref[:, cols] = _rot_bwd(dkr, c, s).astype(dk_ref.dtype)

    w = hp * HEAD_DIM

    def pj(off):
        return pl.BlockSpec((blk, w), lambda h, i: (nb - 1 - i, off * H // hp + h))

    tok = pl.BlockSpec((blk, HEAD_DIM), lambda h, i: (nb - 1 - i, 0))
    out = pl.BlockSpec((blk, w), lambda h, i: (nb - 1 - i, h))

    def per_head(r, c):
        return pl.BlockSpec((hp, r, c), lambda h, i: (h, 0, 0))

    shp = jax.ShapeDtypeStruct((T, A), CDT)
    return pl.pallas_call(
        body, name="ret_bwd", grid=(H // hp, nb),
        in_specs=[pj(3), pj(4), pj(5), pj(6), tok, tok, per_head(blk, blk),
                  per_head(blk, HEAD_DIM), per_head(blk, HEAD_DIM), per_head(8, HEAD_DIM),
                  out, pl.BlockSpec((hp, None, HEAD_DIM, HEAD_DIM), lambda h, i: (h, nb - 1 - i, 0, 0)),
                  out],
        out_specs=[out, out, out, out], out_shape=[shp, shp, shp, shp],
        scratch_shapes=[pltpu.VMEM((hp, HEAD_DIM, HEAD_DIM), F32)],
        compiler_params=_params(("parallel", "arbitrary")))(
            proj, proj, proj, proj, rc, rs, dmat, qd, kd, cd, o_raw, states, dy)


def _mesh_pos():
    return lax.axis_index("x"), lax.axis_index("y"), lax.axis_index("c")


def _flat(pos):
    return 4 * pos[0] + 2 * pos[1] + pos[2]


_HBM = pl.BlockSpec(memory_space=pltpu.HBM)


def cast_shard(name, w, rows_p, cols_p, me_arr, after=()):
    r, c = w.shape
    tr = _row_tile(math.gcd(r, rows_p), 256)
    nr = r // tr

    def body(me_ref, w_ref, *rest):
        o_ref = rest[-1]
        i = pl.program_id(0)
        o_ref[...] = jnp.zeros_like(o_ref)

        @pl.when(i < nr)
        def _():
            o_ref[:, 0:c] = w_ref[...].astype(o_ref.dtype)

    return pl.pallas_call(
        body, name=name,
        grid_spec=pltpu.PrefetchScalarGridSpec(
            num_scalar_prefetch=1, grid=(rows_p // tr,),
            in_specs=[pl.BlockSpec((tr, c), lambda i, me: (jnp.minimum(i, nr - 1), 0))] + [_ANY] * len(after),
            out_specs=pl.BlockSpec((None, tr, cols_p), lambda i, me: (me[0], i, 0))),
        out_shape=jax.ShapeDtypeStruct((N_DEV, rows_p, cols_p), CDT),
        compiler_params=_params(("arbitrary",)))(me_arr, w, *after)


def exchange_partials(arrays, after):
    n, na = len(arrays), len(after)

    def body(*refs):
        ins, outs = refs[:n], refs[n + na:2 * n + na]
        send_sems, recv_sems, local_sems = refs[2 * n + na:]
        me = _mesh_pos()
        copies, locals_ = [], []
        for t in range(n):
            cp = pltpu.make_async_copy(ins[t], outs[t].at[_flat(me)], local_sems.at[t])
            cp.start()
            locals_.append(cp)
            for k in range(1, N_DEV):
                peer = _peer(me, k)
                send = pltpu.make_async_remote_copy(
                    src_ref=ins[t], dst_ref=outs[t].at[_flat(me)],
                    send_sem=send_sems.at[t, k - 1], recv_sem=recv_sems.at[t, k - 1],
                    device_id=peer, device_id_type=pl.DeviceIdType.MESH)
                send.start()
                recv = pltpu.make_async_remote_copy(
                    src_ref=ins[t], dst_ref=outs[t].at[_flat(peer)],
                    send_sem=send_sems.at[t, k - 1], recv_sem=recv_sems.at[t, k - 1],
                    device_id=peer, device_id_type=pl.DeviceIdType.MESH)
                copies.append((send, recv))
        for send, recv in copies:
            recv.wait_recv()
        for send, recv in copies:
            send.wait_send()
        for cp in locals_:
            cp.wait()

    return pl.pallas_call(
        body, name="exchange_partials",
        in_specs=[_HBM] * n + [_ANY] * na, out_specs=[_HBM] * n,
        out_shape=[jax.ShapeDtypeStruct((N_DEV,) + a.shape, a.dtype) for a in arrays],
        scratch_shapes=[pltpu.SemaphoreType.DMA((n, 7)), pltpu.SemaphoreType.DMA((n, 7)),
                        pltpu.SemaphoreType.DMA((n,))],
        )(*arrays, *after)


_SEM = pl.BlockSpec(memory_space=pltpu.SEMAPHORE)
_ANY = pl.BlockSpec(memory_space=pl.ANY)
_EFFECT = pltpu.SideEffectType.DATAFLOW_SIDE_EFFECTING


def _peer(me, k):
    return tuple(1 - v if bit else v for v, bit in zip(me, (k >> 2, (k >> 1) & 1, k & 1)))


_CHIP_MASKS = (2, 4, 6)

_EXCHANGE_MODES = {"gather": (7, None), "gather_chips": (4, None), "forward": (3, None),
                   "scatter": (7, 7), "scatter_pair": (4, 4), "scatter_chips": (3, 3)}


def _plan(mode, bufs, n, me):
    per = _EXCHANGE_MODES[mode][0]
    sib = _peer(me, 1)
    plan = []
    for t in range(n):
        src_arr, land_arr = bufs[t], bufs[n + t] if _EXCHANGE_MODES[mode][1] else None
        if mode in ("gather", "gather_chips"):
            masks = range(1, N_DEV) if mode == "gather" else (1,) + _CHIP_MASKS
            rows = [(src_arr.at[_flat(me)], src_arr.at[_flat(me)], _peer(me, k), src_arr.at[_flat(_peer(me, k))])
                    for k in masks]
        elif mode == "forward":
            rows = [(src_arr.at[_flat(_peer(me, k))], src_arr.at[_flat(_peer(me, k))], sib,
                     src_arr.at[_flat(_peer(sib, k))]) for k in _CHIP_MASKS]
        elif mode == "scatter":
            rows = [(src_arr.at[_flat(_peer(me, k))], land_arr.at[k - 1], _peer(me, k), land_arr.at[k - 1])
                    for k in range(1, N_DEV)]
        elif mode == "scatter_pair":
            rows = [(src_arr.at[_flat(_peer(me, q + 1))], land_arr.at[qi], sib, land_arr.at[qi])
                    for qi, q in enumerate((0,) + _CHIP_MASKS)]
        else:
            assert mode == "scatter_chips"
            rows = [(src_arr.at[qi + 1], land_arr.at[qi], _peer(me, q), land_arr.at[qi])
                    for qi, q in enumerate(_CHIP_MASKS)]
        assert len(rows) == per
        plan += [(t * per + s,) + row for s, row in enumerate(rows)]
    return plan


def exchange_start(name, arrays, mode, after):
    n, na = len(arrays), len(after)
    per, slots = _EXCHANGE_MODES[mode]
    bufs = list(arrays)
    if slots:
        bufs += [lax.empty((slots,) + a.shape[1:], a.dtype) for a in arrays]
    nb = len(bufs)

    def body(*refs):
        send_sems, recv_sems = refs[nb + na], refs[nb + na + 1]
        token = refs[-1]
        for s, src, dst, dev, _ in _plan(mode, refs[:nb], n, _mesh_pos()):
            pltpu.make_async_remote_copy(
                src_ref=src, dst_ref=dst, send_sem=send_sems.at[s], recv_sem=recv_sems.at[s],
                device_id=dev, device_id_type=pl.DeviceIdType.MESH).start()
        token[...] = jnp.zeros_like(token)

    out_shape = [pltpu.SemaphoreType.DMA((n * per,)), pltpu.SemaphoreType.DMA((n * per,))]
    out_shape += [pltpu.HBM(a.shape, a.dtype) for a in bufs]
    out_shape.append(jax.ShapeDtypeStruct((8, LANE), F32))
    args = [pltpu.with_memory_space_constraint(a, pltpu.HBM) for a in bufs] + list(after)
    outs = pl.pallas_call(
        body, name=name, out_shape=out_shape,
        in_specs=[_HBM] * nb + [_ANY] * na,
        out_specs=[_SEM, _SEM] + [_HBM] * nb + [pl.BlockSpec(memory_space=pltpu.VMEM)],
        input_output_aliases={i: 2 + i for i in range(nb)},
        compiler_params=pltpu.CompilerParams(has_side_effects=_EFFECT))(*args)
    return outs[0], outs[1], list(outs[2:2 + nb]), outs[-1]


def exchange_wait(name, started, mode, after):
    send_sems, recv_sems, bufs, _ = started
    nb, na = len(bufs), len(after)
    n = nb // 2 if _EXCHANGE_MODES[mode][1] else nb

    def body(*refs):
        send_sems_ref, recv_sems_ref = refs[nb], refs[nb + 1]
        for s, src, _, dev, land in _plan(mode, refs[:nb], n, _mesh_pos()):
            cp = pltpu.make_async_remote_copy(
                src_ref=src, dst_ref=land, send_sem=send_sems_ref.at[s], recv_sem=recv_sems_ref.at[s],
                device_id=dev, device_id_type=pl.DeviceIdType.MESH)
            cp.wait_send()
            cp.wait_recv()

    outs = pl.pallas_call(
        body, name=name, out_shape=[pltpu.HBM(a.shape, a.dtype) for a in bufs],
        in_specs=[_HBM] * nb + [_SEM, _SEM] + [_ANY] * na, out_specs=[_HBM] * nb,
        input_output_aliases={i: i for i in range(nb)},
        compiler_params=pltpu.CompilerParams(has_side_effects=_EFFECT))(
            *bufs, send_sems, recv_sems, *after)
    return list(outs)


def pair_sum(name, own, landed, blocks):
    _, r, c = own.shape
    tr = _row_tile(r, 256)

    def body(idx_ref, o_ref, l_ref, s_ref):
        s_ref[...] = (o_ref[...].astype(F32) + l_ref[...].astype(F32)).astype(s_ref.dtype)

    blk = pl.BlockSpec((None, tr, c), lambda q, i, idx: (q, i, 0))
    return pl.pallas_call(
        body, name=name,
        grid_spec=pltpu.PrefetchScalarGridSpec(
            num_scalar_prefetch=1, grid=(4, r // tr),
            in_specs=[pl.BlockSpec((None, tr, c), lambda q, i, idx: (idx[q], i, 0)), blk],
            out_specs=blk),
        out_shape=jax.ShapeDtypeStruct((4, r, c), own.dtype),
        compiler_params=_params(("parallel", "parallel")))(blocks, own, landed)


def _adamw_math(w, g, m, v):
    m = ADAM_B1 * m + (1.0 - ADAM_B1) * g
    v = ADAM_B2 * v + (1.0 - ADAM_B2) * (g * g)
    m_hat = m / (1.0 - ADAM_B1 ** ADAM_STEP)
    v_hat = v / (1.0 - ADAM_B2 ** ADAM_STEP)
    delta = -ADAM_LR * (m_hat / (jnp.sqrt(v_hat) + ADAM_EPS) + ADAM_WD * w)
    return delta, m, v


def reduce_adamw(name, land, w, m, v, tr, own=None, me_arr=None):
    R, C = w.shape
    S, _, Cp = land.shape

    def body(*refs):
        if own is not None:
            _, own_ref, l_ref, w_ref, m_ref, v_ref, g_ref, d_ref, nm_ref, nv_ref = refs
            g = own_ref[:, 0:C].astype(F32)
            first = 0
        else:
            l_ref, w_ref, m_ref, v_ref, g_ref, d_ref, nm_ref, nv_ref = refs
            g = l_ref[0, :, 0:C].astype(F32)
            first = 1
        for s in range(first, S):
            g = g + l_ref[s, :, 0:C].astype(F32)
        delta, nm, nv = _adamw_math(w_ref[...], g, m_ref[...], v_ref[...])
        g_ref[...] = g
        d_ref[...] = delta
        nm_ref[...] = nm
        nv_ref[...] = nv

    shp = jax.ShapeDtypeStruct((R, C), F32)
    if own is None:
        blk = pl.BlockSpec((tr, C), lambda i: (i, 0))
        return pl.pallas_call(
            body, name=name, grid=(R // tr,),
            in_specs=[pl.BlockSpec((S, tr, Cp), lambda i: (0, i, 0)), blk, blk, blk],
            out_specs=[blk, blk, blk, blk], out_shape=[shp, shp, shp, shp],
            compiler_params=_params(("parallel",)))(land, w, m, v)
    blk = pl.BlockSpec((tr, C), lambda i, me: (i, 0))
    return pl.pallas_call(
        body, name=name,
        grid_spec=pltpu.PrefetchScalarGridSpec(
            num_scalar_prefetch=1, grid=(R // tr,),
            in_specs=[pl.BlockSpec((None, tr, Cp), lambda i, me: (me[0], i, 0)),
                      pl.BlockSpec((S, tr, Cp), lambda i, me: (0, i, 0)), blk, blk, blk],
            out_specs=[blk, blk, blk, blk]),
        out_shape=[shp, shp, shp, shp],
        compiler_params=_params(("parallel",)))(me_arr, own, land, w, m, v)


def _row_tile(r, cap):
    t = min(r, cap)
    while r % t or t % 8:
        t -= 8
    return t


def kernel(x, norm_ffn1_g, ffn1_w_gate, ffn1_w_up, ffn1_w_down, norm_mix_g, w_in, rel_bias, w_out_att, w_out_ret, w_out, norm_ffn2_g, ffn2_w_gate, ffn2_w_up, ffn2_w_down, norm_final_g, loss_target, m_norm_ffn1_g, m_ffn1_w_gate, m_ffn1_w_up, m_ffn1_w_down, m_norm_mix_g, m_w_in, m_rel_bias, m_w_out_att, m_w_out_ret, m_w_out, m_norm_ffn2_g, m_ffn2_w_gate, m_ffn2_w_up, m_ffn2_w_down, m_norm_final_g, v_norm_ffn1_g, v_ffn1_w_gate, v_ffn1_w_up, v_ffn1_w_down, v_norm_mix_g, v_w_in, v_rel_bias, v_w_out_att, v_w_out_ret, v_w_out, v_norm_ffn2_g, v_ffn2_w_gate, v_ffn2_w_up, v_ffn2_w_down, v_norm_final_g):
    T, D = x.shape[1], x.shape[2]
    A = w_out_att.shape[1]
    H = A // HEAD_DIM
    nf = ffn1_w_gate.shape[2]
    nfp = _round_up(nf, LANE)
    nin = w_in.shape[2]
    nd = w_out.shape[1]
    assert nin % LANE == 0 and nd % LANE == 0 and (7 * A) % nd == 0 and T % ATT_TQ == 0
    tm = min(512, T)
    tb = min(1024, T)
    tw = min(2048, T)
    tw_in = min(1024, T)
    tn = min(256, T)
    x0 = x[0]
    tgt = loss_target[0]

    me_arr = (4 * lax.axis_index("x") + 2 * lax.axis_index("y") + lax.axis_index("c")).astype(jnp.int32).reshape(1)

    def slot(tag, w, after, rows_p=None):
        return cast_shard("cast_" + tag, w[0], rows_p or w.shape[1], w.shape[2], me_arr, after)

    def slot_t(tag, w, after):
        return cast_shard("cast_" + tag, jnp.transpose(w[0]), nfp, w.shape[1], me_arr, after)

    groups = [("wg1", [("wg1", ffn1_w_gate, slot_t)], True), ("wu1", [("wu1", ffn1_w_up, slot_t)], True),
              ("wd1", [("wd1", ffn1_w_down, functools.partial(slot, rows_p=nfp))], True),
              ("win", [("win", w_in, slot)], True),
              ("wout", [("woa", w_out_att, slot), ("wor", w_out_ret, slot), ("wo", w_out, slot)], False),
              ("wgu2", [("wg2", ffn2_w_gate, slot_t), ("wu2", ffn2_w_up, slot_t)], True),
              ("wd2", [("wd2", ffn2_w_down, functools.partial(slot, rows_p=nfp))], True)]
    ag_started = {}
    order = []
    for tag, members, two_level in groups:
        mode = "gather_chips" if two_level else "gather"
        started = exchange_start("ag_start_" + tag, [make(nm, w, order) for nm, w, make in members], mode, order)
        ag_started[tag] = (started, mode)
        order = [started[3]]

    passing = {}

    def begin_pass(tag, after):
        started, mode = ag_started[tag]
        got = exchange_wait("ag_wait_" + tag, started, mode, [after])
        passing[tag] = exchange_start("ag_pass_" + tag, got, "forward", [])
        return passing[tag][3]

    def gathered(tag, after):
        started, mode = ag_started[tag]
        if mode == "gather":
            return exchange_wait("ag_wait_" + tag, started, mode, [after])
        if tag not in passing:
            begin_pass(tag, after)
        return exchange_wait("ag_passed_" + tag, passing[tag], "forward", [after])

    xi, yi, ci = lax.axis_index("x"), lax.axis_index("y"), lax.axis_index("c")
    my_side = jnp.stack([4 * (1 - xi if q & 4 else xi) + 2 * (1 - yi if q & 2 else yi) + ci
                         for q in (0,) + _CHIP_MASKS]).astype(jnp.int32)
    first_block = jnp.zeros((1,), jnp.int32)

    def swiglu(prods, _):
        a, b = prods
        return a, b, a * _sigmoid(a) * b

    def ffn_fwd(tag, xin, g, get_wgu, get_wd, after=()):
        h = rmsnorm_fwd(tag + "_norm", xin, g, tn, after)
        if isinstance(get_wgu, tuple):
            Wg, = get_wgu[0](h)
            a, = mm_block(tag + "_gate", T, tb, [(h, "full", D, 0)], [(Wg, 0, True)], [], [("3d", nfp, CDT)],
                          lambda p, _: p)
            Wu, = get_wgu[1](a)
            b, mid = mm_block(tag + "_up", T, tb, [(h, "full", D, 0)], [(Wu, 0, True)], [(a, "3d", nfp, 0)],
                              [("3d", nfp, CDT)] * 2, lambda p, ex: (p[0], ex[0] * _sigmoid(ex[0]) * p[0]))
        else:
            Wg, Wu = get_wgu(h)
            a, b, mid = mm_block(tag + "_up", T, tb, [(h, "full", D, 0)], [(Wg, 0, True), (Wu, 0, True)],
                                 [], [("3d", nfp, CDT)] * 3, swiglu)
        Wd, = get_wd(mid)
        xo = mm_reduce_j(tag + "_down", T, tm, [(mid, "3d", nfp, 0, Wd, False)], D, F32, res=xin, scale=0.5,
                         jstep=2)
        return h, a, b, mid, xo, (Wg, Wu, Wd)

    h1, a1, b1, mid1, x1, (Wg1, Wu1, Wd1) = ffn_fwd(
        "ffn1", x0, norm_ffn1_g, (lambda h: gathered("wg1", h), lambda a: gathered("wu1", a)),
        lambda mid: gathered("wd1", mid), after=order)
    h2 = rmsnorm_fwd("mix_norm", x1, norm_mix_g, tn)
    Win, = gathered("win", h2)
    proj, = mm_block("in_proj", T, tb, [(h2, "full", D, 0)], [(Win, 0, False)], [], [("col", nin, CDT)],
                     lambda p, _: p)
    biasm = attn_bias(rel_bias[0], ATT_TQ)
    att = attn_fwd(proj, biasm, A, ATT_TQ)
    tables = _retention_tables(T, H, RET_BLK)
    retg, ret_raw, states = ret_fwd(proj, tables, A, RET_BLK)
    Woa, Wor, Wo = gathered("wout", retg)
    goff = 7 * A // nd

    def merge(prods, ex):
        ba, br = prods
        ga, gr = ex
        return ba, br, _sigmoid(ga) * ba + _sigmoid(gr) * br

    ba, br, merged = mm_block(
        "branches", T, tm, [(att, "full", A, 0), (retg, "full", A, 0)], [(Woa, 0, False), (Wor, 1, False)],
        [(proj, "col", nd, goff), (proj, "col", nd, goff + N_DEV)], [("col", nd, CDT)] * 3, merge, order="ij", jgroup=4)
    x2 = mm_reduce_j("out_proj", T, tm, [(merged, "col", nd, 0, Wo, False)], D, F32, res=x1, scale=1.0,
                     jstep=N_DEV, after=[begin_pass("wgu2", merged)])
    h3, a2, b2, mid2, x3, (Wg2, Wu2, Wd2) = ffn_fwd(
        "ffn2", x2, norm_ffn2_g, lambda h: gathered("wgu2", h), lambda mid: gathered("wd2", mid))

    dx3, dx3h, dgf, loss_part = loss_head(x3, norm_final_g.reshape(1, D), tgt, tn)

    def swiglu_bwd(prods, ex):
        dm, = prods
        a, b = ex
        sg = _sigmoid(a)
        return dm * b * (sg * (1.0 + a * (1.0 - sg))), dm * (a * sg)

    def ffn_bwd(tag, dxh, h, a, b, mid, Wg, Wu, Wd, two_level=False):
        da, db = mm_block(tag + "_down_bwd", T, tb, [(dxh, "full", D, 0)], [(Wd, 0, True)],
                          [(a, "3d", nfp, 0), (b, "3d", nfp, 0)], [("3d", nfp, CDT)] * 2, swiglu_bwd)

        def up_bwd(after):
            return mm_reduce_j(tag + "_up_bwd", T, tm,
                               [(da, "3d", nfp, 0, Wg, False), (db, "3d", nfp, 0, Wu, False)],
                               D, CDT, after=after, jstep=2)

        if not two_level:
            dWd = mm_reduce_i(tag + "_dwd", T, tw, (mid, "3d", nfp, 0), (dxh, "full", D, 0))
            dWg = mm_reduce_i(tag + "_dwg", T, tw, (da, "3d", nfp, 0), (h, "full", D, 0))
            dWu = mm_reduce_i(tag + "_dwu", T, tw, (db, "3d", nfp, 0), (h, "full", D, 0))
            sent = exchange_start("rs_start_" + tag, [dWg, dWu, dWd], "scatter", [])
            return up_bwd([sent[3]]), sent

        def swap(nm, grad, after):
            return exchange_start("rs_pair_start_%s_%s" % (tag, nm), [grad], "scatter_pair", after)

        def to_chips(nm, swapping, after):
            own, landed = exchange_wait("rs_pair_wait_%s_%s" % (tag, nm), swapping, "scatter_pair", [after])
            sums = pair_sum("%s_pair_sum_%s" % (tag, nm), own, landed, my_side)
            return exchange_start("rs_chips_start_%s_%s" % (tag, nm), [sums], "scatter_chips", [])

        dWd = mm_reduce_i(tag + "_dwd", T, tw, (mid, "3d", nfp, 0), (dxh, "full", D, 0))
        swap_d = swap("d", dWd, [])
        dWg = mm_reduce_i(tag + "_dwg", T, tw, (da, "3d", nfp, 0), (h, "full", D, 0), after=[swap_d[3]])
        sent_d = to_chips("d", swap_d, dWg)
        swap_g = swap("g", dWg, [sent_d[3]])
        dWu = mm_reduce_i(tag + "_dwu", T, tw, (db, "3d", nfp, 0), (h, "full", D, 0), after=[swap_g[3]])
        sent_g = to_chips("g", swap_g, dWu)
        swap_u = swap("u", dWu, [sent_g[3]])
        dh = up_bwd([swap_u[3]])
        sent_u = to_chips("u", swap_u, dh)
        return dh, [sent_g, sent_u, sent_d]

    dh3, sent_ffn2 = ffn_bwd("ffn2", dx3h, h3, a2, b2, mid2, Wg2, Wu2, Wd2)
    dx2, dx2c, dg2 = rmsnorm_bwd("ffn2_norm_bwd", x2, norm_ffn2_g, dh3, dx3, 1.0, tn)

    def merge_bwd(prods, ex):
        dmg, = prods
        ba_, br_, ga, gr = ex
        sa, sr = _sigmoid(ga), _sigmoid(gr)
        return dmg * sa, dmg * sr, dmg * ba_ * sa * (1.0 - sa), dmg * br_ * sr * (1.0 - sr)

    dba, dbr, dga, dgr = mm_block(
        "out_proj_bwd", T, tm, [(dx2c, "full", D, 0)], [(Wo, 0, True)],
        [(ba, "col", nd, 0), (br, "col", nd, 0), (proj, "col", nd, goff), (proj, "col", nd, goff + N_DEV)],
        [("col", nd, CDT)] * 4, merge_bwd, order="ij", jgroup=4)
    dWo = mm_reduce_i("dwo", T, tw, (merged, "col", nd, 0), (dx2c, "full", D, 0))
    dWoa = mm_reduce_i("dwoa", T, tw, (att, "full", A, 0), (dba, "col", nd, 0))
    dWor = mm_reduce_i("dwor", T, tw, (retg, "full", A, 0), (dbr, "col", nd, 0))
    sent_mix = exchange_start("rs_start_mix", [dWoa, dWor, dWo], "scatter", [])
    datt = mm_reduce_j("att_out_bwd", T, tm, [(dba, "col", nd, 0, Woa, True)], A, CDT, after=[sent_mix[3]],
                       jstep=N_DEV)
    dretg = mm_reduce_j("ret_out_bwd", T, tm, [(dbr, "col", nd, 0, Wor, True)], A, CDT, jstep=N_DEV)
    dq_r, dk_r, dv_r, dg_r = ret_bwd(proj, tables, ret_raw, states, dretg, A, RET_BLK)
    dq_a, dk_a, dv_a, dst = attn_bwd(proj, biasm, datt, A, ATT_TQ)
    dbias = jnp.pad(attn_bias_grad(dst, ATT_TQ), ((0, 0), (0, N_REL_PAD - N_REL)))
    dproj = jnp.concatenate([dq_a, dk_a, dv_a, dq_r, dk_r, dv_r, dg_r, dga, dgr], axis=1)
    dWin = mm_reduce_i("dwin", T, tw_in, (h2, "full", D, 0), (dproj, "col", nin, 0))
    sent_win = exchange_start("rs_start_win", [dWin], "scatter", [])
    dh2 = mm_reduce_j("in_proj_bwd", T, tm, [(dproj, "col", nin, 0, Win, True)], D, CDT, after=[sent_win[3]],
                      jstep=2)
    dx1, dx1h, dgm = rmsnorm_bwd("mix_norm_bwd", x1, norm_mix_g, dh2, dx2, 0.5, tn)
    dh1, sent_ffn1 = ffn_bwd("ffn1", dx1h, h1, a1, b1, mid1, Wg1, Wu1, Wd1, two_level=True)
    grad_x, _, dg1 = rmsnorm_bwd("ffn1_norm_bwd", x0, norm_ffn1_g, dh1, dx1, 1.0, tn, after=[sent_ffn1[1][3]])

    dgains = jnp.concatenate([dg1, dgm, dg2, dgf, jnp.zeros((4, D), F32)], axis=0)

    def upd(name, own, land, w, m, v, own_block=me_arr, transposed=False):
        w2, m2, v2 = [jnp.transpose(t[0]) if transposed else t[0] for t in (w, m, v)]
        outs = reduce_adamw(name, land, w2, m2, v2, _row_tile(w2.shape[0], 256), own=own, me_arr=own_block)
        return [jnp.transpose(o)[None] if transposed else o[None] for o in outs]

    res = {}
    oWg2, oWu2, oWd2, lWg2, lWu2, lWd2 = exchange_wait("rs_wait_ffn2", sent_ffn2, "scatter", [grad_x])
    res["ffn2_w_gate"] = upd("adamw_wg2", oWg2, lWg2, ffn2_w_gate, m_ffn2_w_gate, v_ffn2_w_gate, transposed=True)
    res["ffn2_w_up"] = upd("adamw_wu2", oWu2, lWu2, ffn2_w_up, m_ffn2_w_up, v_ffn2_w_up, transposed=True)
    res["ffn2_w_down"] = upd("adamw_wd2", oWd2, lWd2, ffn2_w_down, m_ffn2_w_down, v_ffn2_w_down)
    oWoa, oWor, oWo, lWoa, lWor, lWo = exchange_wait("rs_wait_mix", sent_mix, "scatter", [res["ffn2_w_down"][1]])
    res["w_out_att"] = upd("adamw_woa", oWoa, lWoa, w_out_att, m_w_out_att, v_w_out_att)
    res["w_out_ret"] = upd("adamw_wor", oWor, lWor, w_out_ret, m_w_out_ret, v_w_out_ret)
    res["w_out"] = upd("adamw_wo", oWo, lWo, w_out, m_w_out, v_w_out)
    oWin, lWin = exchange_wait("rs_wait_win", sent_win, "scatter", [res["w_out"][1]])
    res["w_in"] = upd("adamw_win", oWin, lWin, w_in, m_w_in, v_w_in)
    lgains, lbias = exchange_partials([dgains, dbias], [res["w_in"][1]])
    (oWg1, lWg1), (oWu1, lWu1), (oWd1, lWd1) = [
        exchange_wait("rs_wait_ffn1_" + nm, started, "scatter_chips", [lgains])
        for nm, started in zip("gud", sent_ffn1)]
    res["ffn1_w_gate"] = upd("adamw_wg1", oWg1, lWg1, ffn1_w_gate, m_ffn1_w_gate, v_ffn1_w_gate, first_block, transposed=True)
    res["ffn1_w_up"] = upd("adamw_wu1", oWu1, lWu1, ffn1_w_up, m_ffn1_w_up, v_ffn1_w_up, first_block, transposed=True)
    res["ffn1_w_down"] = upd("adamw_wd1", oWd1, lWd1, ffn1_w_down, m_ffn1_w_down, v_ffn1_w_down, first_block)

    def stack_gains(a, b, c_, d):
        return jnp.concatenate([a, b, c_, d.reshape(1, D), jnp.zeros((4, D), F32)], axis=0)

    gw = stack_gains(norm_ffn1_g, norm_mix_g, norm_ffn2_g, norm_final_g)
    gm = stack_gains(m_norm_ffn1_g, m_norm_mix_g, m_norm_ffn2_g, m_norm_final_g)
    gv = stack_gains(v_norm_ffn1_g, v_norm_mix_g, v_norm_ffn2_g, v_norm_final_g)
    gains = reduce_adamw("adamw_gains", lgains, gw, gm, gv, 8)

    def padb(t):
        return jnp.pad(t[0], ((0, 0), (0, N_REL_PAD - N_REL)))

    bias = [o[:, :N_REL][None] for o in
            reduce_adamw("adamw_bias", lbias, padb(rel_bias), padb(m_rel_bias), padb(v_rel_bias), H)]
    res["norm_ffn1_g"] = [o[0:1] for o in gains]
    res["norm_mix_g"] = [o[1:2] for o in gains]
    res["norm_ffn2_g"] = [o[2:3] for o in gains]
    res["norm_final_g"] = [o[3] for o in gains]
    res["rel_bias"] = bias

    loss = lax.psum(loss_part[0, 0], MESH_AXES)
    names = ["norm_ffn1_g", "ffn1_w_gate", "ffn1_w_up", "ffn1_w_down", "norm_mix_g", "w_in", "rel_bias",
             "w_out_att", "w_out_ret", "w_out", "norm_ffn2_g", "ffn2_w_gate", "ffn2_w_up", "ffn2_w_down",
             "norm_final_g"]
    out = [loss, grad_x[None]]
    for k in range(4):
        out += [res[nm][k] for nm in names]
    return tuple(out)
```

```python
import functools
import math

import jax
import jax.numpy as jnp
import numpy as np
from jax import lax
from jax.experimental import pallas as pl
from jax.experimental.pallas import tpu as pltpu

F32 = jnp.float32
CDT = jnp.bfloat16

N_DEV = 8
CHUNK = 64
N_PREV_CHUNKS = 8
BAND = N_PREV_CHUNKS * CHUNK
HEAD_DIM = 128
MAX_REL_DIST = 128
N_REL = 2 * MAX_REL_DIST + 1
N_REL_PAD = 384
ROPE_BASE = 10000.0
EPS = 1e-6
NEG = -1e30
LANE = 128
ATT_TQ = 256
RET_BLK = 256
RET_HEADS_PER_STEP = 8
RET_BWD_HEADS_PER_STEP = 4
VMEM_LIMIT = 48 * 1024 * 1024

ADAM_LR = 0.001
ADAM_B1 = 0.9
ADAM_B2 = 0.999
ADAM_EPS = 1e-08
ADAM_WD = 0.01
ADAM_STEP = 10

MESH_AXES = ("x", "y", "c")
_NT = (((1,), (1,)), ((), ()))
_TN = (((0,), (0,)), ((), ()))


def _round_up(v, m):
    return (v + m - 1) // m * m


def _params(sem=None):
    return pltpu.CompilerParams(dimension_semantics=sem, vmem_limit_bytes=VMEM_LIMIT)


def _sigmoid(v):
    return 0.5 * (jnp.tanh(0.5 * v) + 1.0)


def _bspec(kind, tm, w, off, order, jmap=lambda j: j):
    def wrap(f):
        if order == "ji":
            return lambda j, i: f(i, jmap(j))
        return lambda i, j: f(i, jmap(j))
    if kind == "full":
        return pl.BlockSpec((tm, w), wrap(lambda i, j: (i, 0)))
    if kind == "col":
        return pl.BlockSpec((tm, w), wrap(lambda i, j: (i, j + off)))
    assert kind == "3d"
    return pl.BlockSpec((None, tm, w), wrap(lambda i, j: (j, i, 0)))


def _wspec(w, order, jmap=lambda j: j):
    if order == "ji":
        return pl.BlockSpec((None,) + w.shape[1:], lambda j, i: (jmap(j), 0, 0))
    return pl.BlockSpec((None,) + w.shape[1:], lambda i, j: (jmap(j), 0, 0))


def _width(arr, kind, w):
    return arr.shape[-1] if kind in ("full", "3d") else w


def mm_block(name, T, tm, lhs, wts, extras, outs, epilogue, order="ji", after=(), jgroup=1):
    nl, nw, ne = len(lhs), len(wts), len(extras)
    ni = T // tm
    wo = outs[0][1]

    def body(*refs):
        l = refs[:nl]
        w = refs[nl:nl + nw * jgroup]
        e = refs[nl + nw * jgroup:nl + nw * jgroup + ne]
        o = refs[nl + nw * jgroup + ne + len(after):]
        for u in range(jgroup):
            cols = slice(None) if jgroup == 1 else pl.ds(u * wo, wo)
            prods = []
            for k, (_, li, tr) in enumerate(wts):
                a = l[li][...]
                wk = w[k * jgroup + u][...]
                if tr:
                    prods.append(lax.dot_general(a, wk, _NT, preferred_element_type=F32))
                else:
                    prods.append(jnp.dot(a, wk, preferred_element_type=F32))
            res = epilogue(prods, [r[:, cols].astype(F32) for r in e])
            for r, val in zip(o, res):
                r[:, cols] = val.astype(r.dtype)

    def blocked(kind, w, off):
        if jgroup == 1:
            return _bspec(kind, tm, w, off, order)
        assert kind == "col" and off % jgroup == 0
        return _bspec(kind, tm, w * jgroup, off // jgroup, order)

    in_specs = [_bspec(k, tm, _width(a, k, w), off, order) for (a, k, w, off) in lhs]
    in_specs += [_wspec(w, order, lambda j, u=u: j * jgroup + u) for (w, _, _) in wts for u in range(jgroup)]
    in_specs += [blocked(k, _width(a, k, w), off) for (a, k, w, off) in extras]
    in_specs += [_ANY] * len(after)
    out_specs, out_shape = [], []
    for (kind, w, dt) in outs:
        out_specs.append(blocked(kind, w, 0))
        if kind == "3d":
            out_shape.append(jax.ShapeDtypeStruct((N_DEV, T, w), dt))
        else:
            out_shape.append(jax.ShapeDtypeStruct((T, N_DEV * w), dt))
    args = [a for (a, _, _, _) in lhs] + [w for (w, _, _) in wts for _ in range(jgroup)]
    args += [a for (a, _, _, _) in extras] + list(after)
    nj = N_DEV // jgroup
    return pl.pallas_call(
        body, name=name, grid=(nj, ni) if order == "ji" else (ni, nj), in_specs=in_specs,
        out_specs=out_specs, out_shape=out_shape, compiler_params=_params(("parallel", "parallel")))(*args)


def mm_reduce_j(name, T, tm, pairs, out_w, out_dtype, res=None, scale=1.0, after=(), jstep=1):
    terms = [(p, u) for u in range(jstep) for p in pairs]
    nt = len(terms)
    nj = N_DEV // jstep
    ni = T // tm

    def body(*refs):
        xs = refs[:nt]
        ws = refs[nt:2 * nt]
        rest = refs[2 * nt:len(refs) - 2 - len(after)] + refs[len(refs) - 2:]
        if res is not None:
            res_ref, o_ref, acc = rest
        else:
            o_ref, acc = rest
        j = pl.program_id(1)

        @pl.when(j == 0)
        def _():
            acc[...] = jnp.zeros_like(acc)

        tot = None
        for k, (p, _) in enumerate(terms):
            if p[5]:
                d = lax.dot_general(xs[k][...], ws[k][...], _NT, preferred_element_type=F32)
            else:
                d = jnp.dot(xs[k][...], ws[k][...], preferred_element_type=F32)
            tot = d if tot is None else tot + d
        acc[...] += tot

        @pl.when(j == nj - 1)
        def _():
            if res is not None:
                o_ref[...] = (res_ref[...] + scale * acc[...]).astype(o_ref.dtype)
            else:
                o_ref[...] = acc[...].astype(o_ref.dtype)

    def jmap(u):
        return lambda j: j * jstep + u

    in_specs = [_bspec(p[1], tm, _width(p[0], p[1], p[2]), p[3], "ij", jmap(u)) for (p, u) in terms]
    in_specs += [_wspec(p[4], "ij", jmap(u)) for (p, u) in terms]
    args = [p[0] for (p, _) in terms] + [p[4] for (p, _) in terms]
    if res is not None:
        in_specs.append(pl.BlockSpec((tm, out_w), lambda i, j: (i, 0)))
        args.append(res)
    in_specs += [_ANY] * len(after)
    args += list(after)
    return pl.pallas_call(
        body, name=name, grid=(ni, nj), in_specs=in_specs,
        out_specs=pl.BlockSpec((tm, out_w), lambda i, j: (i, 0)),
        out_shape=jax.ShapeDtypeStruct((T, out_w), out_dtype),
        scratch_shapes=[pltpu.VMEM((tm, out_w), F32)],
        compiler_params=_params(("parallel", "arbitrary")))(*args)


def mm_reduce_i(name, T, tm, a, b, after=()):
    ni = T // tm
    rows = _width(a[0], a[1], a[2])
    cols = _width(b[0], b[1], b[2])

    def body(a_ref, b_ref, *rest):
        o_ref, acc = rest[len(after):]
        i = pl.program_id(1)

        @pl.when(i == 0)
        def _():
            acc[...] = jnp.zeros_like(acc)

        acc[...] += lax.dot_general(a_ref[...], b_ref[...], _TN, preferred_element_type=F32)

        @pl.when(i == ni - 1)
        def _():
            o_ref[...] = acc[...].astype(o_ref.dtype)

    return pl.pallas_call(
        body, name=name, grid=(N_DEV, ni),
        in_specs=[_bspec(a[1], tm, rows, a[3], "ji"), _bspec(b[1], tm, cols, b[3], "ji")] + [_ANY] * len(after),
        out_specs=pl.BlockSpec((None, rows, cols), lambda j, i: (j, 0, 0)),
        out_shape=jax.ShapeDtypeStruct((N_DEV, rows, cols), CDT),
        scratch_shapes=[pltpu.VMEM((rows, cols), F32)],
        compiler_params=_params(("parallel", "arbitrary")))(a[0], b[0], *after)


def _rms_bwd_math(xv, g, dy):
    r = lax.rsqrt(jnp.mean(xv * xv, axis=-1, keepdims=True) + EPS)
    xn = xv * r
    dxn = dy * g
    dx = r * (dxn - xn * jnp.mean(dxn * xn, axis=-1, keepdims=True))
    dg = jnp.sum(dy * xn, axis=0, keepdims=True)
    return dx, dg


def rmsnorm_fwd(name, x, g, tm, after=()):
    T, D = x.shape

    def body(x_ref, g_ref, *rest):
        o_ref = rest[-1]
        xv = x_ref[...]
        r = lax.rsqrt(jnp.mean(xv * xv, axis=-1, keepdims=True) + EPS)
        o_ref[...] = (xv * r * g_ref[...]).astype(o_ref.dtype)

    return pl.pallas_call(
        body, name=name, grid=(T // tm,),
        in_specs=[pl.BlockSpec((tm, D), lambda i: (i, 0)), pl.BlockSpec((1, D), lambda i: (0, 0))]
        + [_ANY] * len(after),
        out_specs=pl.BlockSpec((tm, D), lambda i: (i, 0)),
        out_shape=jax.ShapeDtypeStruct((T, D), CDT),
        compiler_params=_params(("parallel",)))(x, g, *after)


def rmsnorm_bwd(name, x, g, dh, dres, cscale, tm, after=()):
    T, D = x.shape

    def body(x_ref, g_ref, dh_ref, dres_ref, *rest):
        dx_ref, dxc_ref, dg_ref = rest[len(after):]
        i = pl.program_id(0)
        dx, dg = _rms_bwd_math(x_ref[...], g_ref[...], dh_ref[...].astype(F32))
        dx = dres_ref[...] + dx
        dx_ref[...] = dx
        dxc_ref[...] = (cscale * dx).astype(dxc_ref.dtype)

        @pl.when(i == 0)
        def _():
            dg_ref[...] = jnp.zeros_like(dg_ref)

        dg_ref[...] += dg

    row = pl.BlockSpec((tm, D), lambda i: (i, 0))
    vec = pl.BlockSpec((1, D), lambda i: (0, 0))
    return pl.pallas_call(
        body, name=name, grid=(T // tm,), in_specs=[row, vec, row, row] + [_ANY] * len(after),
        out_specs=[row, row, vec],
        out_shape=[jax.ShapeDtypeStruct((T, D), F32), jax.ShapeDtypeStruct((T, D), CDT),
                   jax.ShapeDtypeStruct((1, D), F32)],
        compiler_params=_params(("arbitrary",)))(x, g, dh, dres, *after)


def loss_head(x, g, tgt, tm):
    T, D = x.shape

    def body(x_ref, g_ref, t_ref, dx_ref, dxc_ref, dg_ref, loss_ref):
        i = pl.program_id(0)
        xv = x_ref[...]
        gv = g_ref[...]
        r = lax.rsqrt(jnp.mean(xv * xv, axis=-1, keepdims=True) + EPS)
        err = xv * r * gv - t_ref[...]
        part = jnp.sum(jnp.mean(err * err, axis=-1, keepdims=True), axis=0, keepdims=True)
        dx, dg = _rms_bwd_math(xv, gv, err / D)
        dx_ref[...] = dx
        dxc_ref[...] = (0.5 * dx).astype(dxc_ref.dtype)

        @pl.when(i == 0)
        def _():
            dg_ref[...] = jnp.zeros_like(dg_ref)
            loss_ref[...] = jnp.zeros_like(loss_ref)

        dg_ref[...] += dg
        loss_ref[...] += jnp.broadcast_to(0.5 * part, loss_ref.shape)

    row = pl.BlockSpec((tm, D), lambda i: (i, 0))
    vec = pl.BlockSpec((1, D), lambda i: (0, 0))
    return pl.pallas_call(
        body, name="loss_head", grid=(T // tm,), in_specs=[row, vec, row],
        out_specs=[row, row, vec, pl.BlockSpec((1, LANE), lambda i: (0, 0))],
        out_shape=[jax.ShapeDtypeStruct((T, D), F32), jax.ShapeDtypeStruct((T, D), CDT),
                   jax.ShapeDtypeStruct((1, D), F32), jax.ShapeDtypeStruct((1, LANE), F32)],
        compiler_params=_params(("arbitrary",)))(x, g, tgt)


def _skew_rows(z, left):
    tq, kw = z.shape
    row = lax.broadcasted_iota(jnp.int32, (tq, kw), 0)
    s = 1
    while s < tq:
        z = jnp.where((row & s) != 0, pltpu.roll(z, kw - s if left else s, 1), z)
        s *= 2
    return z


REL_HI = BAND + MAX_REL_DIST
REL_LO = BAND - MAX_REL_DIST


def attn_bias(rel_bias, tq):
    H = rel_bias.shape[0]
    kw = BAND + tq
    by_skew = jnp.concatenate(
        [jnp.broadcast_to(rel_bias[:, N_REL - 1:], (H, REL_LO)), rel_bias[:, ::-1],
         jnp.broadcast_to(rel_bias[:, :1], (H, kw - REL_HI - 1))], axis=1).reshape(H, 1, kw)

    def body(t_ref, o_ref):
        t = t_ref[...]
        qi = lax.broadcasted_iota(jnp.int32, (tq, kw), 0)
        kj = lax.broadcasted_iota(jnp.int32, (tq, kw), 1)
        b = _skew_rows(jnp.broadcast_to(t, (tq, kw)), left=False)
        b = jnp.where(kj < qi, t[:, 0:1], b)
        qc = qi // CHUNK
        kc = kj // CHUNK - N_PREV_CHUNKS
        valid = (kc <= qc) & (kc >= qc - N_PREV_CHUNKS)
        o_ref[...] = jnp.where(valid, b, NEG)

    return pl.pallas_call(
        body, name="attn_bias", grid=(H,),
        in_specs=[pl.BlockSpec((None, 1, kw), lambda h: (h, 0, 0))],
        out_specs=pl.BlockSpec((None, tq, kw), lambda h: (h, 0, 0)),
        out_shape=jax.ShapeDtypeStruct((H, tq, kw), F32),
        compiler_params=_params(("parallel",)))(by_skew)


def attn_bias_grad(dst, tq):
    H = dst.shape[0]
    kw = BAND + tq

    def body(d_ref, o_ref):
        z = _skew_rows(d_ref[...], left=True)
        qi = lax.broadcasted_iota(jnp.int32, (tq, kw), 0)
        kj = lax.broadcasted_iota(jnp.int32, (tq, kw), 1)
        wrapped = kj + qi >= kw
        c = jnp.sum(jnp.where(wrapped, 0.0, z), axis=0, keepdims=True)
        cw = jnp.sum(jnp.sum(jnp.where(wrapped, z, 0.0), axis=0, keepdims=True), axis=1, keepdims=True)
        lane = lax.broadcasted_iota(jnp.int32, (1, kw), 1)
        ahead = jnp.sum(jnp.where(lane >= REL_HI, c, 0.0), axis=1, keepdims=True)
        behind = jnp.sum(jnp.where(lane <= REL_LO, c, 0.0), axis=1, keepdims=True) + cw
        o_ref[...] = jnp.where(lane == REL_HI, ahead, jnp.where(lane == REL_LO, behind, c))

    by_skew = pl.pallas_call(
        body, name="attn_bias_grad", grid=(H,),
        in_specs=[pl.BlockSpec((None, tq, kw), lambda h: (h, 0, 0))],
        out_specs=pl.BlockSpec((None, 1, kw), lambda h: (h, 0, 0)),
        out_shape=jax.ShapeDtypeStruct((H, 1, kw), F32),
        compiler_params=_params(("parallel",)))(dst)
    return by_skew[:, 0, REL_LO:REL_HI + 1][:, ::-1]


def _attn_scores(q, kpad, bm_ref, start, kw):
    k = kpad[pl.ds(start, kw), :]
    s = lax.dot_general(q, k, _NT, preferred_element_type=F32) * (HEAD_DIM ** -0.5) + bm_ref[...]
    col = lax.broadcasted_iota(jnp.int32, s.shape, 1)
    s = jnp.where(col < BAND - start, NEG, s)
    m = jnp.max(s, axis=-1, keepdims=True)
    e = jnp.exp(s - m)
    return e, 1.0 / jnp.sum(e, axis=-1, keepdims=True), k


def _fill_padded(pad_ref, src_ref, T):
    pad_ref[pl.ds(0, BAND), :] = jnp.zeros((BAND, HEAD_DIM), pad_ref.dtype)
    pad_ref[pl.ds(BAND, T), :] = src_ref[...]


def attn_fwd(proj, biasm, A, tq):
    T = proj.shape[0]
    H = A // HEAD_DIM
    kw = BAND + tq

    def body(q_ref, k_ref, v_ref, bm_ref, o_ref, kpad, vpad):
        qi = pl.program_id(1)

        @pl.when(qi == 0)
        def _():
            _fill_padded(kpad, k_ref, T)
            _fill_padded(vpad, v_ref, T)

        start = pl.multiple_of(qi * tq, tq)
        e, rinv, _ = _attn_scores(q_ref[...], kpad, bm_ref, start, kw)
        v = vpad[pl.ds(start, kw), :]
        o_ref[...] = (jnp.dot(e.astype(CDT), v, preferred_element_type=F32) * rinv).astype(o_ref.dtype)

    return pl.pallas_call(
        body, name="attn_fwd", grid=(H, T // tq),
        in_specs=[pl.BlockSpec((tq, HEAD_DIM), lambda h, i: (i, h)),
                  pl.BlockSpec((T, HEAD_DIM), lambda h, i: (0, H + h)),
                  pl.BlockSpec((T, HEAD_DIM), lambda h, i: (0, 2 * H + h)),
                  pl.BlockSpec((None, tq, kw), lambda h, i: (h, 0, 0))],
        out_specs=pl.BlockSpec((tq, HEAD_DIM), lambda h, i: (i, h)),
        out_shape=jax.ShapeDtypeStruct((T, A), CDT),
        scratch_shapes=[pltpu.VMEM((BAND + T, HEAD_DIM), CDT), pltpu.VMEM((BAND + T, HEAD_DIM), CDT)],
        compiler_params=_params(("parallel", "arbitrary")))(proj, proj, proj, biasm)


def attn_bwd(proj, biasm, datt, A, tq):
    T = proj.shape[0]
    H = A // HEAD_DIM
    kw = BAND + tq
    nq = T // tq
    scale = HEAD_DIM ** -0.5

    def body(q_ref, k_ref, v_ref, bm_ref, do_ref, dq_ref, dk_ref, dv_ref, dst_ref,
             kpad, vpad, dkacc, dvacc):
        qi = pl.program_id(1)

        @pl.when(qi == 0)
        def _():
            _fill_padded(kpad, k_ref, T)
            _fill_padded(vpad, v_ref, T)
            dkacc[...] = jnp.zeros_like(dkacc)
            dvacc[...] = jnp.zeros_like(dvacc)
            dst_ref[...] = jnp.zeros_like(dst_ref)

        start = pl.multiple_of(qi * tq, tq)
        q = q_ref[...]
        e, rinv, k = _attn_scores(q, kpad, bm_ref, start, kw)
        p = e * rinv
        v = vpad[pl.ds(start, kw), :]
        do = do_ref[...]
        dp = lax.dot_general(do, v, _NT, preferred_element_type=F32)
        ds = p * (dp - jnp.sum(dp * p, axis=-1, keepdims=True))
        dst_ref[...] += ds
        dsb = ds.astype(CDT)
        dq_ref[...] = (jnp.dot(dsb, k, preferred_element_type=F32) * scale).astype(dq_ref.dtype)
        dkacc[pl.ds(start, kw), :] += lax.dot_general(dsb, q, _TN, preferred_element_type=F32) * scale
        dvacc[pl.ds(start, kw), :] += lax.dot_general(p.astype(CDT), do, _TN, preferred_element_type=F32)

        @pl.when(qi == nq - 1)
        def _():
            dk_ref[...] = dkacc[pl.ds(BAND, T), :].astype(dk_ref.dtype)
            dv_ref[...] = dvacc[pl.ds(BAND, T), :].astype(dv_ref.dtype)

    blk = pl.BlockSpec((tq, HEAD_DIM), lambda h, i: (i, h))
    col = pl.BlockSpec((T, HEAD_DIM), lambda h, i: (0, h))
    bias = pl.BlockSpec((None, tq, kw), lambda h, i: (h, 0, 0))
    return pl.pallas_call(
        body, name="attn_bwd", grid=(H, nq),
        in_specs=[blk,
                  pl.BlockSpec((T, HEAD_DIM), lambda h, i: (0, H + h)),
                  pl.BlockSpec((T, HEAD_DIM), lambda h, i: (0, 2 * H + h)),
                  bias, blk],
        out_specs=[blk, col, col, bias],
        out_shape=[jax.ShapeDtypeStruct((T, A), CDT), jax.ShapeDtypeStruct((T, A), CDT),
                   jax.ShapeDtypeStruct((T, A), CDT), jax.ShapeDtypeStruct((H, tq, kw), F32)],
        scratch_shapes=[pltpu.VMEM((BAND + T, HEAD_DIM), CDT), pltpu.VMEM((BAND + T, HEAD_DIM), CDT),
                        pltpu.VMEM((BAND + T, HEAD_DIM), F32), pltpu.VMEM((BAND + T, HEAD_DIM), F32)],
        compiler_params=_params(("parallel", "arbitrary")))(proj, proj, proj, biasm, datt)


def _retention_tables(T, H, blk):
    half = HEAD_DIM // 2
    inv = 1.0 / (ROPE_BASE ** (jnp.arange(0, HEAD_DIM, 2, dtype=F32) / HEAD_DIM))
    ang = jnp.arange(T, dtype=F32)[:, None] * inv[None, :]
    cos, sin = jnp.cos(ang), jnp.sin(ang)
    rc = jnp.concatenate([cos, cos], axis=1)
    rs = jnp.concatenate([-sin, sin], axis=1)
    assert rc.shape == (T, 2 * half)
    log_g = jnp.log(1.0 - 2.0 ** (-5.0 - jnp.arange(H, dtype=F32)))[:, None, None]
    idx = jnp.arange(blk, dtype=F32)
    n, m = idx[:, None], idx[None, :]
    same = (n // CHUNK) == (m // CHUNK)
    earlier = (m // CHUNK) < (n // CHUNK)
    dist = jnp.where(same, jnp.abs(n - m), n - m)[None]
    dmat = jnp.where((same | earlier)[None], jnp.exp(log_g * dist), 0.0)
    ones = jnp.ones((1, 1, HEAD_DIM), F32)
    qd = jnp.exp(log_g * (idx[None, :, None] + 1.0)) * ones
    kd = jnp.exp(log_g * (blk - 1.0 - idx[None, :, None])) * ones
    cd = jnp.exp(log_g * blk) * jnp.ones((1, 8, HEAD_DIM), F32)
    return rc, rs, dmat, qd, kd, cd


def _rot(v, rc, rs):
    return v * rc + pltpu.roll(v, HEAD_DIM // 2, 1) * rs


def _rot_bwd(dv, rc, rs):
    return dv * rc + pltpu.roll(dv * rs, HEAD_DIM // 2, 1)


def ret_fwd(proj, tables, A, blk):
    T = proj.shape[0]
    H = A // HEAD_DIM
    nb = T // blk
    hp = RET_HEADS_PER_STEP
    rc, rs, dmat, qd, kd, cd = tables
    scale = HEAD_DIM ** -0.5

    def body(q_ref, k_ref, v_ref, g_ref, rc_ref, rs_ref, d_ref, qd_ref, kd_ref, cd_ref,
             y_ref, o_ref, st_ref, state):
        b = pl.program_id(1)

        @pl.when(b == 0)
        def _():
            state[...] = jnp.zeros_like(state)

        c, s = rc_ref[...], rs_ref[...]
        for u in range(hp):
            cols = pl.ds(u * HEAD_DIM, HEAD_DIM)
            qs = (_rot(q_ref[:, cols].astype(F32), c, s) * scale).astype(CDT)
            kr = _rot(k_ref[:, cols].astype(F32), c, s)
            v = v_ref[:, cols]
            sb = state[u].astype(CDT)
            a = lax.dot_general(qs, kr.astype(CDT), _NT, preferred_element_type=F32) * d_ref[u]
            o = jnp.dot(a.astype(CDT), v, preferred_element_type=F32)
            o = o + jnp.dot(qs, sb, preferred_element_type=F32) * qd_ref[u]
            st_ref[u] = sb
            state[u] = state[u] * cd_ref[u, 0:1, :] + lax.dot_general(
                (kr * kd_ref[u]).astype(CDT), v, _TN, preferred_element_type=F32)
            o_ref[:, cols] = o
            on = o * lax.rsqrt(jnp.mean(o * o, axis=-1, keepdims=True) + EPS)
            g = g_ref[:, cols].astype(F32)
            y_ref[:, cols] = (g * _sigmoid(g) * on).astype(y_ref.dtype)

    w = hp * HEAD_DIM

    def pj(off):
        return pl.BlockSpec((blk, w), lambda h, i: (i, off * H // hp + h))

    tok = pl.BlockSpec((blk, HEAD_DIM), lambda h, i: (i, 0))
    out = pl.BlockSpec((blk, w), lambda h, i: (i, h))

    def per_head(r, c):
        return pl.BlockSpec((hp, r, c), lambda h, i: (h, 0, 0))

    return pl.pallas_call(
        body, name="ret_fwd", grid=(H // hp, nb),
        in_specs=[pj(3), pj(4), pj(5), pj(6), tok, tok, per_head(blk, blk),
                  per_head(blk, HEAD_DIM), per_head(blk, HEAD_DIM), per_head(8, HEAD_DIM)],
        out_specs=[out, out, pl.BlockSpec((hp, None, HEAD_DIM, HEAD_DIM), lambda h, i: (h, i, 0, 0))],
        out_shape=[jax.ShapeDtypeStruct((T, A), CDT), jax.ShapeDtypeStruct((T, A), F32),
                   jax.ShapeDtypeStruct((H, nb, HEAD_DIM, HEAD_DIM), CDT)],
        scratch_shapes=[pltpu.VMEM((hp, HEAD_DIM, HEAD_DIM), F32)],
        compiler_params=_params(("parallel", "arbitrary")))(
            proj, proj, proj, proj, rc, rs, dmat, qd, kd, cd)


def ret_bwd(proj, tables, o_raw, states, dy, A, blk):
    T = proj.shape[0]
    H = A // HEAD_DIM
    nb = T // blk
    hp = RET_BWD_HEADS_PER_STEP
    rc, rs, dmat, qd, kd, cd = tables
    scale = HEAD_DIM ** -0.5

    def body(q_ref, k_ref, v_ref, g_ref, rc_ref, rs_ref, d_ref, qd_ref, kd_ref, cd_ref,
             o_ref, st_ref, dy_ref, dq_ref, dk_ref, dv_ref, dg_ref, dstate):
        b = pl.program_id(1)

        @pl.when(b == 0)
        def _():
            dstate[...] = jnp.zeros_like(dstate)

        c, s = rc_ref[...], rs_ref[...]
        for u in range(hp):
            cols = pl.ds(u * HEAD_DIM, HEAD_DIM)
            qs = (_rot(q_ref[:, cols].astype(F32), c, s) * scale).astype(CDT)
            kr = _rot(k_ref[:, cols].astype(F32), c, s)
            krb = kr.astype(CDT)
            kdb = (kr * kd_ref[u]).astype(CDT)
            v = v_ref[:, cols]
            dmat_v = d_ref[u]
            a = lax.dot_general(qs, krb, _NT, preferred_element_type=F32) * dmat_v

            o = o_ref[:, cols]
            r = lax.rsqrt(jnp.mean(o * o, axis=-1, keepdims=True) + EPS)
            on = o * r
            g = g_ref[:, cols].astype(F32)
            sg = _sigmoid(g)
            dyv = dy_ref[:, cols].astype(F32)
            dg_ref[:, cols] = (dyv * on * (sg * (1.0 + g * (1.0 - sg)))).astype(dg_ref.dtype)
            don = dyv * (g * sg)
            do = r * (don - on * jnp.mean(don * on, axis=-1, keepdims=True))
            dob = do.astype(CDT)
            doq = (do * qd_ref[u]).astype(CDT)
            dsb = dstate[u].astype(CDT)

            dv = lax.dot_general(a.astype(CDT), dob, _TN, preferred_element_type=F32)
            dv = dv + jnp.dot(kdb, dsb, preferred_element_type=F32)
            dv_ref[:, cols] = dv.astype(dv_ref.dtype)
            dpb = (lax.dot_general(dob, v, _NT, preferred_element_type=F32) * dmat_v).astype(CDT)
            dqs = jnp.dot(dpb, krb, preferred_element_type=F32)
            dqs = dqs + lax.dot_general(doq, st_ref[u], _NT, preferred_element_type=F32)
            dkr = lax.dot_general(dpb, qs, _TN, preferred_element_type=F32)
            dkr = dkr + lax.dot_general(v, dsb, _NT, preferred_element_type=F32) * kd_ref[u]
            dstate[u] = dstate[u] * cd_ref[u, 0:1, :] + lax.dot_general(
                qs, doq, _TN, preferred_element_type=F32)
            dq_ref[:, cols] = _rot_bwd(dqs * scale, c, s).astype(dq_ref.dtype)
            dk_ref[:, cols] = _rot_bwd(dkr, c, s).astype(dk_ref.dtype)

    w = hp * HEAD_DIM

    def pj(off):
        return pl.BlockSpec((blk, w), lambda h, i: (nb - 1 - i, off * H // hp + h))

    tok = pl.BlockSpec((blk, HEAD_DIM), lambda h, i: (nb - 1 - i, 0))
    out = pl.BlockSpec((blk, w), lambda h, i: (nb - 1 - i, h))

    def per_head(r, c):
        return pl.BlockSpec((hp, r, c), lambda h, i: (h, 0, 0))

    shp = jax.ShapeDtypeStruct((T, A), CDT)
    return pl.pallas_call(
        body, name="ret_bwd", grid=(H // hp, nb),
        in_specs=[pj(3), pj(4), pj(5), pj(6), tok, tok, per_head(blk, blk),
                  per_head(blk, HEAD_DIM), per_head(blk, HEAD_DIM), per_head(8, HEAD_DIM),
                  out, pl.BlockSpec((hp, None, HEAD_DIM, HEAD_DIM), lambda h, i: (h, nb - 1 - i, 0, 0)),
                  out],
        out_specs=[out, out, out, out], out_shape=[shp, shp, shp, shp],
        scratch_shapes=[pltpu.VMEM((hp, HEAD_DIM, HEAD_DIM), F32)],
        compiler_params=_params(("parallel", "arbitrary")))(
            proj, proj, proj, proj, rc, rs, dmat, qd, kd, cd, o_raw, states, dy)


def _mesh_pos():
    return lax.axis_index("x"), lax.axis_index("y"), lax.axis_index("c")


def _flat(pos):
    return 4 * pos[0] + 2 * pos[1] + pos[2]


_HBM = pl.BlockSpec(memory_space=pltpu.HBM)


def cast_shard(name, w, rows_p, cols_p, me_arr, after=()):
    r, c = w.shape
    tr = _row_tile(math.gcd(r, rows_p), 256)
    nr = r // tr

    def body(me_ref, w_ref, *rest):
        o_ref = rest[-1]
        i = pl.program_id(0)
        o_ref[...] = jnp.zeros_like(o_ref)

        @pl.when(i < nr)
        def _():
            o_ref[:, 0:c] = w_ref[...].astype(o_ref.dtype)

    return pl.pallas_call(
        body, name=name,
        grid_spec=pltpu.PrefetchScalarGridSpec(
            num_scalar_prefetch=1, grid=(rows_p // tr,),
            in_specs=[pl.BlockSpec((tr, c), lambda i, me: (jnp.minimum(i, nr - 1), 0))] + [_ANY] * len(after),
            out_specs=pl.BlockSpec((None, tr, cols_p), lambda i, me: (me[0], i, 0))),
        out_shape=jax.ShapeDtypeStruct((N_DEV, rows_p, cols_p), CDT),
        compiler_params=_params(("arbitrary",)))(me_arr, w, *after)


def exchange_partials(arrays, after):
    n, na = len(arrays), len(after)

    def body(*refs):
        ins, outs = refs[:n], refs[n + na:2 * n + na]
        send_sems, recv_sems, local_sems = refs[2 * n + na:]
        me = _mesh_pos()
        copies, locals_ = [], []
        for t in range(n):
            cp = pltpu.make_async_copy(ins[t], outs[t].at[_flat(me)], local_sems.at[t])
            cp.start()
            locals_.append(cp)
            for k in range(1, N_DEV):
                peer = _peer(me, k)
                send = pltpu.make_async_remote_copy(
                    src_ref=ins[t], dst_ref=outs[t].at[_flat(me)],
                    send_sem=send_sems.at[t, k - 1], recv_sem=recv_sems.at[t, k - 1],
                    device_id=peer, device_id_type=pl.DeviceIdType.MESH)
                send.start()
                recv = pltpu.make_async_remote_copy(
                    src_ref=ins[t], dst_ref=outs[t].at[_flat(peer)],
                    send_sem=send_sems.at[t, k - 1], recv_sem=recv_sems.at[t, k - 1],
                    device_id=peer, device_id_type=pl.DeviceIdType.MESH)
                copies.append((send, recv))
        for send, recv in copies:
            recv.wait_recv()
        for send, recv in copies:
            send.wait_send()
        for cp in locals_:
            cp.wait()

    return pl.pallas_call(
        body, name="exchange_partials",
        in_specs=[_HBM] * n + [_ANY] * na, out_specs=[_HBM] * n,
        out_shape=[jax.ShapeDtypeStruct((N_DEV,) + a.shape, a.dtype) for a in arrays],
        scratch_shapes=[pltpu.SemaphoreType.DMA((n, 7)), pltpu.SemaphoreType.DMA((n, 7)),
                        pltpu.SemaphoreType.DMA((n,))],
        )(*arrays, *after)


_SEM = pl.BlockSpec(memory_space=pltpu.SEMAPHORE)
_ANY = pl.BlockSpec(memory_space=pl.ANY)
_EFFECT = pltpu.SideEffectType.DATAFLOW_SIDE_EFFECTING


def _peer(me, k):
    return tuple(1 - v if bit else v for v, bit in zip(me, (k >> 2, (k >> 1) & 1, k & 1)))


_CHIP_MASKS = (2, 4, 6)

_EXCHANGE_MODES = {"gather": (7, None), "gather_chips": (4, None), "forward": (3, None),
                   "scatter": (7, 7), "scatter_pair": (4, 4), "scatter_chips": (3, 3)}


def _plan(mode, bufs, n, me):
    per = _EXCHANGE_MODES[mode][0]
    sib = _peer(me, 1)
    plan = []
    for t in range(n):
        src_arr, land_arr = bufs[t], bufs[n + t] if _EXCHANGE_MODES[mode][1] else None
        if mode in ("gather", "gather_chips"):
            masks = range(1, N_DEV) if mode == "gather" else (1,) + _CHIP_MASKS
            rows = [(src_arr.at[_flat(me)], src_arr.at[_flat(me)], _peer(me, k), src_arr.at[_flat(_peer(me, k))])
                    for k in masks]
        elif mode == "forward":
            rows = [(src_arr.at[_flat(_peer(me, k))], src_arr.at[_flat(_peer(me, k))], sib,
                     src_arr.at[_flat(_peer(sib, k))]) for k in _CHIP_MASKS]
        elif mode == "scatter":
            rows = [(src_arr.at[_flat(_peer(me, k))], land_arr.at[k - 1], _peer(me, k), land_arr.at[k - 1])
                    for k in range(1, N_DEV)]
        elif mode == "scatter_pair":
            rows = [(src_arr.at[_flat(_peer(me, q + 1))], land_arr.at[qi], sib, land_arr.at[qi])
                    for qi, q in enumerate((0,) + _CHIP_MASKS)]
        else:
            assert mode == "scatter_chips"
            rows = [(src_arr.at[qi + 1], land_arr.at[qi], _peer(me, q), land_arr.at[qi])
                    for qi, q in enumerate(_CHIP_MASKS)]
        assert len(rows) == per
        plan += [(t * per + s,) + row for s, row in enumerate(rows)]
    return plan


def exchange_start(name, arrays, mode, after):
    n, na = len(arrays), len(after)
    per, slots = _EXCHANGE_MODES[mode]
    bufs = list(arrays)
    if slots:
        bufs += [lax.empty((slots,) + a.shape[1:], a.dtype) for a in arrays]
    nb = len(bufs)

    def body(*refs):
        send_sems, recv_sems = refs[nb + na], refs[nb + na + 1]
        token = refs[-1]
        for s, src, dst, dev, _ in _plan(mode, refs[:nb], n, _mesh_pos()):
            pltpu.make_async_remote_copy(
                src_ref=src, dst_ref=dst, send_sem=send_sems.at[s], recv_sem=recv_sems.at[s],
                device_id=dev, device_id_type=pl.DeviceIdType.MESH).start()
        token[...] = jnp.zeros_like(token)

    out_shape = [pltpu.SemaphoreType.DMA((n * per,)), pltpu.SemaphoreType.DMA((n * per,))]
    out_shape += [pltpu.HBM(a.shape, a.dtype) for a in bufs]
    out_shape.append(jax.ShapeDtypeStruct((8, LANE), F32))
    args = [pltpu.with_memory_space_constraint(a, pltpu.HBM) for a in bufs] + list(after)
    outs = pl.pallas_call(
        body, name=name, out_shape=out_shape,
        in_specs=[_HBM] * nb + [_ANY] * na,
        out_specs=[_SEM, _SEM] + [_HBM] * nb + [pl.BlockSpec(memory_space=pltpu.VMEM)],
        input_output_aliases={i: 2 + i for i in range(nb)},
        compiler_params=pltpu.CompilerParams(has_side_effects=_EFFECT))(*args)
    return outs[0], outs[1], list(outs[2:2 + nb]), outs[-1]


def exchange_wait(name, started, mode, after):
    send_sems, recv_sems, bufs, _ = started
    nb, na = len(bufs), len(after)
    n = nb // 2 if _EXCHANGE_MODES[mode][1] else nb

    def body(*refs):
        send_sems_ref, recv_sems_ref = refs[nb], refs[nb + 1]
        for s, src, _, dev, land in _plan(mode, refs[:nb], n, _mesh_pos()):
            cp = pltpu.make_async_remote_copy(
                src_ref=src, dst_ref=land, send_sem=send_sems_ref.at[s], recv_sem=recv_sems_ref.at[s],
                device_id=dev, device_id_type=pl.DeviceIdType.MESH)
            cp.wait_send()
            cp.wait_recv()

    outs = pl.pallas_call(
        body, name=name, out_shape=[pltpu.HBM(a.shape, a.dtype) for a in bufs],
        in_specs=[_HBM] * nb + [_SEM, _SEM] + [_ANY] * na, out_specs=[_HBM] * nb,
        input_output_aliases={i: i for i in range(nb)},
        compiler_params=pltpu.CompilerParams(has_side_effects=_EFFECT))(
            *bufs, send_sems, recv_sems, *after)
    return list(outs)


def pair_sum(name, own, landed, blocks):
    _, r, c = own.shape
    tr = _row_tile(r, 256)

    def body(idx_ref, o_ref, l_ref, s_ref):
        s_ref[...] = (o_ref[...].astype(F32) + l_ref[...].astype(F32)).astype(s_ref.dtype)

    blk = pl.BlockSpec((None, tr, c), lambda q, i, idx: (q, i, 0))
    return pl.pallas_call(
        body, name=name,
        grid_spec=pltpu.PrefetchScalarGridSpec(
            num_scalar_prefetch=1, grid=(4, r // tr),
            in_specs=[pl.BlockSpec((None, tr, c), lambda q, i, idx: (idx[q], i, 0)), blk],
            out_specs=blk),
        out_shape=jax.ShapeDtypeStruct((4, r, c), own.dtype),
        compiler_params=_params(("parallel", "parallel")))(blocks, own, landed)


def _adamw_math(w, g, m, v):
    m = ADAM_B1 * m + (1.0 - ADAM_B1) * g
    v = ADAM_B2 * v + (1.0 - ADAM_B2) * (g * g)
    m_hat = m / (1.0 - ADAM_B1 ** ADAM_STEP)
    v_hat = v / (1.0 - ADAM_B2 ** ADAM_STEP)
    delta = -ADAM_LR * (m_hat / (jnp.sqrt(v_hat) + ADAM_EPS) + ADAM_WD * w)
    return delta, m, v


def reduce_adamw(name, land, w, m, v, tr, own=None, me_arr=None):
    R, C = w.shape
    S, _, Cp = land.shape

    def body(*refs):
        if own is not None:
            _, own_ref, l_ref, w_ref, m_ref, v_ref, g_ref, d_ref, nm_ref, nv_ref = refs
            g = own_ref[:, 0:C].astype(F32)
            first = 0
        else:
            l_ref, w_ref, m_ref, v_ref, g_ref, d_ref, nm_ref, nv_ref = refs
            g = l_ref[0, :, 0:C].astype(F32)
            first = 1
        for s in range(first, S):
            g = g + l_ref[s, :, 0:C].astype(F32)
        delta, nm, nv = _adamw_math(w_ref[...], g, m_ref[...], v_ref[...])
        g_ref[...] = g
        d_ref[...] = delta
        nm_ref[...] = nm
        nv_ref[...] = nv

    shp = jax.ShapeDtypeStruct((R, C), F32)
    if own is None:
        blk = pl.BlockSpec((tr, C), lambda i: (i, 0))
        return pl.pallas_call(
            body, name=name, grid=(R // tr,),
            in_specs=[pl.BlockSpec((S, tr, Cp), lambda i: (0, i, 0)), blk, blk, blk],
            out_specs=[blk, blk, blk, blk], out_shape=[shp, shp, shp, shp],
            compiler_params=_params(("parallel",)))(land, w, m, v)
    blk = pl.BlockSpec((tr, C), lambda i, me: (i, 0))
    return pl.pallas_call(
        body, name=name,
        grid_spec=pltpu.PrefetchScalarGridSpec(
            num_scalar_prefetch=1, grid=(R // tr,),
            in_specs=[pl.BlockSpec((None, tr, Cp), lambda i, me: (me[0], i, 0)),
                      pl.BlockSpec((S, tr, Cp), lambda i, me: (0, i, 0)), blk, blk, blk],
            out_specs=[blk, blk, blk, blk]),
        out_shape=[shp, shp, shp, shp],
        compiler_params=_params(("parallel",)))(me_arr, own, land, w, m, v)


def _row_tile(r, cap):
    t = min(r, cap)
    while r % t or t % 8:
        t -= 8
    return t


def kernel(x, norm_ffn1_g, ffn1_w_gate, ffn1_w_up, ffn1_w_down, norm_mix_g, w_in, rel_bias, w_out_att, w_out_ret, w_out, norm_ffn2_g, ffn2_w_gate, ffn2_w_up, ffn2_w_down, norm_final_g, loss_target, m_norm_ffn1_g, m_ffn1_w_gate, m_ffn1_w_up, m_ffn1_w_down, m_norm_mix_g, m_w_in, m_rel_bias, m_w_out_att, m_w_out_ret, m_w_out, m_norm_ffn2_g, m_ffn2_w_gate, m_ffn2_w_up, m_ffn2_w_down, m_norm_final_g, v_norm_ffn1_g, v_ffn1_w_gate, v_ffn1_w_up, v_ffn1_w_down, v_norm_mix_g, v_w_in, v_rel_bias, v_w_out_att, v_w_out_ret, v_w_out, v_norm_ffn2_g, v_ffn2_w_gate, v_ffn2_w_up, v_ffn2_w_down, v_norm_final_g):
    T, D = x.shape[1], x.shape[2]
    A = w_out_att.shape[1]
    H = A // HEAD_DIM
    nf = ffn1_w_gate.shape[2]
    nfp = _round_up(nf, LANE)
    nin = w_in.shape[2]
    nd = w_out.shape[1]
    assert nin % LANE == 0 and nd % LANE == 0 and (7 * A) % nd == 0 and T % ATT_TQ == 0
    tm = min(512, T)
    tb = min(1024, T)
    tw = min(2048, T)
    tw_in = min(1024, T)
    tn = min(256, T)
    x0 = x[0]
    tgt = loss_target[0]

    me_arr = (4 * lax.axis_index("x") + 2 * lax.axis_index("y") + lax.axis_index("c")).astype(jnp.int32).reshape(1)

    def slot(tag, w, after, rows_p=None):
        return cast_shard("cast_" + tag, w[0], rows_p or w.shape[1], w.shape[2], me_arr, after)

    def slot_t(tag, w, after):
        return cast_shard("cast_" + tag, jnp.transpose(w[0]), nfp, w.shape[1], me_arr, after)

    groups = [("wg1", [("wg1", ffn1_w_gate, slot_t)], True), ("wu1", [("wu1", ffn1_w_up, slot_t)], True),
              ("wd1", [("wd1", ffn1_w_down, functools.partial(slot, rows_p=nfp))], True),
              ("win", [("win", w_in, slot)], True),
              ("wout", [("woa", w_out_att, slot), ("wor", w_out_ret, slot), ("wo", w_out, slot)], False),
              ("wgu2", [("wg2", ffn2_w_gate, slot_t), ("wu2", ffn2_w_up, slot_t)], True),
              ("wd2", [("wd2", ffn2_w_down, functools.partial(slot, rows_p=nfp))], True)]
    ag_started = {}
    order = []
    for tag, members, two_level in groups:
        mode = "gather_chips" if two_level else "gather"
        started = exchange_start("ag_start_" + tag, [make(nm, w, order) for nm, w, make in members], mode, order)
        ag_started[tag] = (started, mode)
        order = [started[3]]

    passing = {}

    def begin_pass(tag, after):
        started, mode = ag_started[tag]
        got = exchange_wait("ag_wait_" + tag, started, mode, [after])
        passing[tag] = exchange_start("ag_pass_" + tag, got, "forward", [])
        return passing[tag][3]

    def gathered(tag, after):
        started, mode = ag_started[tag]
        if mode == "gather":
            return exchange_wait("ag_wait_" + tag, started, mode, [after])
        if tag not in passing:
            begin_pass(tag, after)
        return exchange_wait("ag_passed_" + tag, passing[tag], "forward", [after])

    xi, yi, ci = lax.axis_index("x"), lax.axis_index("y"), lax.axis_index("c")
    my_side = jnp.stack([4 * (1 - xi if q & 4 else xi) + 2 * (1 - yi if q & 2 else yi) + ci
                         for q in (0,) + _CHIP_MASKS]).astype(jnp.int32)
    first_block = jnp.zeros((1,), jnp.int32)

    def swiglu(prods, _):
        a, b = prods
        return a, b, a * _sigmoid(a) * b

    def ffn_fwd(tag, xin, g, get_wgu, get_wd, after=()):
        h = rmsnorm_fwd(tag + "_norm", xin, g, tm, after)
        if isinstance(get_wgu, tuple):
            Wg, = get_wgu[0](h)
            a, = mm_block(tag + "_gate", T, tb, [(h, "full", D, 0)], [(Wg, 0, True)], [], [("3d", nfp, CDT)],
                          lambda p, _: p)
            Wu, = get_wgu[1](a)
            b, mid = mm_block(tag + "_up", T, tb, [(h, "full", D, 0)], [(Wu, 0, True)], [(a, "3d", nfp, 0)],
                              [("3d", nfp, CDT)] * 2, lambda p, ex: (p[0], ex[0] * _sigmoid(ex[0]) * p[0]))
        else:
            Wg, Wu = get_wgu(h)
            a, b, mid = mm_block(tag + "_up", T, tb, [(h, "full", D, 0)], [(Wg, 0, True), (Wu, 0, True)],
                                 [], [("3d", nfp, CDT)] * 3, swiglu)
        Wd, = get_wd(mid)
        xo = mm_reduce_j(tag + "_down", T, tm, [(mid, "3d", nfp, 0, Wd, False)], D, F32, res=xin, scale=0.5,
                         jstep=2)
        return h, a, b, mid, xo, (Wg, Wu, Wd)

    h1, a1, b1, mid1, x1, (Wg1, Wu1, Wd1) = ffn_fwd(
        "ffn1", x0, norm_ffn1_g, (lambda h: gathered("wg1", h), lambda a: gathered("wu1", a)),
        lambda mid: gathered("wd1", mid), after=order)
    h2 = rmsnorm_fwd("mix_norm", x1, norm_mix_g, tm)
    Win, = gathered("win", h2)
    proj, = mm_block("in_proj", T, tb, [(h2, "full", D, 0)], [(Win, 0, False)], [], [("col", nin, CDT)],
                     lambda p, _: p)
    biasm = attn_bias(rel_bias[0], ATT_TQ)
    att = attn_fwd(proj, biasm, A, ATT_TQ)
    tables = _retention_tables(T, H, RET_BLK)
    retg, ret_raw, states = ret_fwd(proj, tables, A, RET_BLK)
    Woa, Wor, Wo = gathered("wout", retg)
    goff = 7 * A // nd

    def merge(prods, ex):
        ba, br = prods
        ga, gr = ex
        return ba, br, _sigmoid(ga) * ba + _sigmoid(gr) * br

    ba, br, merged = mm_block(
        "branches", T, tm, [(att, "full", A, 0), (retg, "full", A, 0)], [(Woa, 0, False), (Wor, 1, False)],
        [(proj, "col", nd, goff), (proj, "col", nd, goff + N_DEV)], [("col", nd, CDT)] * 3, merge, order="ij", jgroup=4)
    x2 = mm_reduce_j("out_proj", T, tm, [(merged, "col", nd, 0, Wo, False)], D, F32, res=x1, scale=1.0,
                     jstep=N_DEV, after=[begin_pass("wgu2", merged)])
    h3, a2, b2, mid2, x3, (Wg2, Wu2, Wd2) = ffn_fwd(
        "ffn2", x2, norm_ffn2_g, lambda h: gathered("wgu2", h), lambda mid: gathered("wd2", mid))

    dx3, dx3h, dgf, loss_part = loss_head(x3, norm_final_g.reshape(1, D), tgt, tn)

    def swiglu_bwd(prods, ex):
        dm, = prods
        a, b = ex
        sg = _sigmoid(a)
        return dm * b * (sg * (1.0 + a * (1.0 - sg))), dm * (a * sg)

    def ffn_bwd(tag, dxh, h, a, b, mid, Wg, Wu, Wd, two_level=False):
        da, db = mm_block(tag + "_down_bwd", T, tb, [(dxh, "full", D, 0)], [(Wd, 0, True)],
                          [(a, "3d", nfp, 0), (b, "3d", nfp, 0)], [("3d", nfp, CDT)] * 2, swiglu_bwd)

        def up_bwd(after):
            return mm_reduce_j(tag + "_up_bwd", T, tm,
                               [(da, "3d", nfp, 0, Wg, False), (db, "3d", nfp, 0, Wu, False)],
                               D, CDT, after=after, jstep=2)

        if not two_level:
            dWd = mm_reduce_i(tag + "_dwd", T, tw, (mid, "3d", nfp, 0), (dxh, "full", D, 0))
            dWg = mm_reduce_i(tag + "_dwg", T, tw, (da, "3d", nfp, 0), (h, "full", D, 0))
            dWu = mm_reduce_i(tag + "_dwu", T, tw, (db, "3d", nfp, 0), (h, "full", D, 0))
            sent = exchange_start("rs_start_" + tag, [dWg, dWu, dWd], "scatter", [])
            return up_bwd([sent[3]]), sent

        def swap(nm, grad, after):
            return exchange_start("rs_pair_start_%s_%s" % (tag, nm), [grad], "scatter_pair", after)

        def to_chips(nm, swapping, after):
            own, landed = exchange_wait("rs_pair_wait_%s_%s" % (tag, nm), swapping, "scatter_pair", [after])
            sums = pair_sum("%s_pair_sum_%s" % (tag, nm), own, landed, my_side)
            return exchange_start("rs_chips_start_%s_%s" % (tag, nm), [sums], "scatter_chips", [])

        dWd = mm_reduce_i(tag + "_dwd", T, tw, (mid, "3d", nfp, 0), (dxh, "full", D, 0))
        swap_d = swap("d", dWd, [])
        dWg = mm_reduce_i(tag + "_dwg", T, tw, (da, "3d", nfp, 0), (h, "full", D, 0), after=[swap_d[3]])
        sent_d = to_chips("d", swap_d, dWg)
        swap_g = swap("g", dWg, [sent_d[3]])
        dWu = mm_reduce_i(tag + "_dwu", T, tw, (db, "3d", nfp, 0), (h, "full", D, 0), after=[swap_g[3]])
        sent_g = to_chips("g", swap_g, dWu)
        swap_u = swap("u", dWu, [sent_g[3]])
        dh = up_bwd([swap_u[3]])
        sent_u = to_chips("u", swap_u, dh)
        return dh, [sent_g, sent_u, sent_d]

    dh3, sent_ffn2 = ffn_bwd("ffn2", dx3h, h3, a2, b2, mid2, Wg2, Wu2, Wd2)
    dx2, dx2c, dg2 = rmsnorm_bwd("ffn2_norm_bwd", x2, norm_ffn2_g, dh3, dx3, 1.0, tn)

    def merge_bwd(prods, ex):
        dmg, = prods
        ba_, br_, ga, gr = ex
        sa, sr = _sigmoid(ga), _sigmoid(gr)
        return dmg * sa, dmg * sr, dmg * ba_ * sa * (1.0 - sa), dmg * br_ * sr * (1.0 - sr)

    dba, dbr, dga, dgr = mm_block(
        "out_proj_bwd", T, tm, [(dx2c, "full", D, 0)], [(Wo, 0, True)],
        [(ba, "col", nd, 0), (br, "col", nd, 0), (proj, "col", nd, goff), (proj, "col", nd, goff + N_DEV)],
        [("col", nd, CDT)] * 4, merge_bwd, order="ij", jgroup=4)
    dWo = mm_reduce_i("dwo", T, tw, (merged, "col", nd, 0), (dx2c, "full", D, 0))
    dWoa = mm_reduce_i("dwoa", T, tw, (att, "full", A, 0), (dba, "col", nd, 0))
    dWor = mm_reduce_i("dwor", T, tw, (retg, "full", A, 0), (dbr, "col", nd, 0))
    sent_mix = exchange_start("rs_start_mix", [dWoa, dWor, dWo], "scatter", [])
    datt = mm_reduce_j("att_out_bwd", T, tm, [(dba, "col", nd, 0, Woa, True)], A, CDT, after=[sent_mix[3]],
                       jstep=N_DEV)
    dretg = mm_reduce_j("ret_out_bwd", T, tm, [(dbr, "col", nd, 0, Wor, True)], A, CDT, jstep=N_DEV)
    dq_r, dk_r, dv_r, dg_r = ret_bwd(proj, tables, ret_raw, states, dretg, A, RET_BLK)
    dq_a, dk_a, dv_a, dst = attn_bwd(proj, biasm, datt, A, ATT_TQ)
    dbias = jnp.pad(attn_bias_grad(dst, ATT_TQ), ((0, 0), (0, N_REL_PAD - N_REL)))
    dproj = jnp.concatenate([dq_a, dk_a, dv_a, dq_r, dk_r, dv_r, dg_r, dga, dgr], axis=1)
    dWin = mm_reduce_i("dwin", T, tw_in, (h2, "full", D, 0), (dproj, "col", nin, 0))
    sent_win = exchange_start("rs_start_win", [dWin], "scatter", [])
    dh2 = mm_reduce_j("in_proj_bwd", T, tm, [(dproj, "col", nin, 0, Win, True)], D, CDT, after=[sent_win[3]],
                      jstep=2)
    dx1, dx1h, dgm = rmsnorm_bwd("mix_norm_bwd", x1, norm_mix_g, dh2, dx2, 0.5, tn)
    dh1, sent_ffn1 = ffn_bwd("ffn1", dx1h, h1, a1, b1, mid1, Wg1, Wu1, Wd1, two_level=True)
    grad_x, _, dg1 = rmsnorm_bwd("ffn1_norm_bwd", x0, norm_ffn1_g, dh1, dx1, 1.0, tn, after=[sent_ffn1[1][3]])

    dgains = jnp.concatenate([dg1, dgm, dg2, dgf, jnp.zeros((4, D), F32)], axis=0)

    def upd(name, own, land, w, m, v, own_block=me_arr, transposed=False):
        w2, m2, v2 = [jnp.transpose(t[0]) if transposed else t[0] for t in (w, m, v)]
        outs = reduce_adamw(name, land, w2, m2, v2, _row_tile(w2.shape[0], 256), own=own, me_arr=own_block)
        return [jnp.transpose(o)[None] if transposed else o[None] for o in outs]

    res = {}
    oWg2, oWu2, oWd2, lWg2, lWu2, lWd2 = exchange_wait("rs_wait_ffn2", sent_ffn2, "scatter", [grad_x])
    res["ffn2_w_gate"] = upd("adamw_wg2", oWg2, lWg2, ffn2_w_gate, m_ffn2_w_gate, v_ffn2_w_gate, transposed=True)
    res["ffn2_w_up"] = upd("adamw_wu2", oWu2, lWu2, ffn2_w_up, m_ffn2_w_up, v_ffn2_w_up, transposed=True)
    res["ffn2_w_down"] = upd("adamw_wd2", oWd2, lWd2, ffn2_w_down, m_ffn2_w_down, v_ffn2_w_down)
    oWoa, oWor, oWo, lWoa, lWor, lWo = exchange_wait("rs_wait_mix", sent_mix, "scatter", [res["ffn2_w_down"][1]])
    res["w_out_att"] = upd("adamw_woa", oWoa, lWoa, w_out_att, m_w_out_att, v_w_out_att)
    res["w_out_ret"] = upd("adamw_wor", oWor, lWor, w_out_ret, m_w_out_ret, v_w_out_ret)
    res["w_out"] = upd("adamw_wo", oWo, lWo, w_out, m_w_out, v_w_out)
    oWin, lWin = exchange_wait("rs_wait_win", sent_win, "scatter", [res["w_out"][1]])
    res["w_in"] = upd("adamw_win", oWin, lWin, w_in, m_w_in, v_w_in)
    lgains, lbias = exchange_partials([dgains, dbias], [res["w_in"][1]])
    (oWg1, lWg1), (oWu1, lWu1), (oWd1, lWd1) = [
        exchange_wait("rs_wait_ffn1_" + nm, started, "scatter_chips", [lgains])
        for nm, started in zip("gud", sent_ffn1)]
    res["ffn1_w_gate"] = upd("adamw_wg1", oWg1, lWg1, ffn1_w_gate, m_ffn1_w_gate, v_ffn1_w_gate, first_block, transposed=True)
    res["ffn1_w_up"] = upd("adamw_wu1", oWu1, lWu1, ffn1_w_up, m_ffn1_w_up, v_ffn1_w_up, first_block, transposed=True)
    res["ffn1_w_down"] = upd("adamw_wd1", oWd1, lWd1, ffn1_w_down, m_ffn1_w_down, v_ffn1_w_down, first_block)

    def stack_gains(a, b, c_, d):
        return jnp.concatenate([a, b, c_, d.reshape(1, D), jnp.zeros((4, D), F32)], axis=0)

    gw = stack_gains(norm_ffn1_g, norm_mix_g, norm_ffn2_g, norm_final_g)
    gm = stack_gains(m_norm_ffn1_g, m_norm_mix_g, m_norm_ffn2_g, m_norm_final_g)
    gv = stack_gains(v_norm_ffn1_g, v_norm_mix_g, v_norm_ffn2_g, v_norm_final_g)
    gains = reduce_adamw("adamw_gains", lgains, gw, gm, gv, 8)

    def padb(t):
        return jnp.pad(t[0], ((0, 0), (0, N_REL_PAD - N_REL)))

    bias = [o[:, :N_REL][None] for o in
            reduce_adamw("adamw_bias", lbias, padb(rel_bias), padb(m_rel_bias), padb(v_rel_bias), H)]
    res["norm_ffn1_g"] = [o[0:1] for o in gains]
    res["norm_mix_g"] = [o[1:2] for o in gains]
    res["norm_ffn2_g"] = [o[2:3] for o in gains]
    res["norm_final_g"] = [o[3] for o in gains]
    res["rel_bias"] = bias

    loss = lax.psum(loss_part[0, 0], MESH_AXES)
    names = ["norm_ffn1_g", "ffn1_w_gate", "ffn1_w_up", "ffn1_w_down", "norm_mix_g", "w_in", "rel_bias",
             "w_out_att", "w_out_ret", "w_out", "norm_ffn2_g", "ffn2_w_gate", "ffn2_w_up", "ffn2_w_down",
             "norm_final_g"]
    out = [loss, grad_x[None]]
    for k in range(4):
        out += [res[nm][k] for nm in names]
    return tuple(out)
```

```python
import functools
import math

import jax
import jax.numpy as jnp
from jax import lax
from jax.experimental import pallas as pl
from jax.experimental.pallas import tpu as pltpu

F32 = jnp.float32
CDT = jnp.bfloat16

N_DEV = 8
CHUNK = 64
N_PREV_CHUNKS = 8
BAND = N_PREV_CHUNKS * CHUNK
HEAD_DIM = 128
MAX_REL_DIST = 128
N_REL = 2 * MAX_REL_DIST + 1
N_REL_PAD = 384
ROPE_BASE = 10000.0
EPS = 1e-6
NEG = -1e30
LANE = 128
ATT_TQ = 256
RET_BLK = 256
RET_HEADS_PER_STEP = 8
RET_BWD_HEADS_PER_STEP = 4
VMEM_LIMIT = 48 * 1024 * 1024

ADAM_LR = 0.001
ADAM_B1 = 0.9
ADAM_B2 = 0.999
ADAM_EPS = 1e-08
ADAM_WD = 0.01
ADAM_STEP = 10

MESH_AXES = ("x", "y", "c")
_NT = (((1,), (1,)), ((), ()))
_TN = (((0,), (0,)), ((), ()))


def _round_up(v, m):
    return (v + m - 1) // m * m


def _params(sem=None):
    return pltpu.CompilerParams(dimension_semantics=sem, vmem_limit_bytes=VMEM_LIMIT)


def _sigmoid(v):
    return 0.5 * (jnp.tanh(0.5 * v) + 1.0)


def _bspec(kind, tm, w, off, order, jmap=lambda j: j):
    def wrap(f):
        if order == "ji":
            return lambda j, i: f(i, jmap(j))
        return lambda i, j: f(i, jmap(j))
    if kind == "full":
        return pl.BlockSpec((tm, w), wrap(lambda i, j: (i, 0)))
    if kind == "col":
        return pl.BlockSpec((tm, w), wrap(lambda i, j: (i, j + off)))
    assert kind == "3d"
    return pl.BlockSpec((None, tm, w), wrap(lambda i, j: (j, i, 0)))


def _wspec(w, order, jmap=lambda j: j):
    if order == "ji":
        return pl.BlockSpec((None,) + w.shape[1:], lambda j, i: (jmap(j), 0, 0))
    return pl.BlockSpec((None,) + w.shape[1:], lambda i, j: (jmap(j), 0, 0))


def _width(arr, kind, w):
    return arr.shape[-1] if kind in ("full", "3d") else w


def mm_block(name, T, tm, lhs, wts, extras, outs, epilogue, order="ji", after=(), jgroup=1):
    nl, nw, ne = len(lhs), len(wts), len(extras)
    ni = T // tm
    wo = outs[0][1]

    def body(*refs):
        l = refs[:nl]
        w = refs[nl:nl + nw * jgroup]
        e = refs[nl + nw * jgroup:nl + nw * jgroup + ne]
        o = refs[nl + nw * jgroup + ne + len(after):]
        for u in range(jgroup):
            cols = slice(None) if jgroup == 1 else pl.ds(u * wo, wo)
            prods = []
            for k, (_, li, tr) in enumerate(wts):
                a = l[li][...]
                wk = w[k * jgroup + u][...]
                if tr:
                    prods.append(lax.dot_general(a, wk, _NT, preferred_element_type=F32))
                else:
                    prods.append(jnp.dot(a, wk, preferred_element_type=F32))
            res = epilogue(prods, [r[:, cols].astype(F32) for r in e])
            for r, val in zip(o, res):
                r[:, cols] = val.astype(r.dtype)

    def blocked(kind, w, off):
        if jgroup == 1:
            return _bspec(kind, tm, w, off, order)
        assert kind == "col" and off % jgroup == 0
        return _bspec(kind, tm, w * jgroup, off // jgroup, order)

    in_specs = [_bspec(k, tm, _width(a, k, w), off, order) for (a, k, w, off) in lhs]
    in_specs += [_wspec(w, order, lambda j, u=u: j * jgroup + u) for (w, _, _) in wts for u in range(jgroup)]
    in_specs += [blocked(k, _width(a, k, w), off) for (a, k, w, off) in extras]
    in_specs += [_ANY] * len(after)
    out_specs, out_shape = [], []
    for (kind, w, dt) in outs:
        out_specs.append(blocked(kind, w, 0))
        if kind == "3d":
            out_shape.append(jax.ShapeDtypeStruct((N_DEV, T, w), dt))
        else:
            out_shape.append(jax.ShapeDtypeStruct((T, N_DEV * w), dt))
    args = [a for (a, _, _, _) in lhs] + [w for (w, _, _) in wts for _ in range(jgroup)]
    args += [a for (a, _, _, _) in extras] + list(after)
    nj = N_DEV // jgroup
    return pl.pallas_call(
        body, name=name, grid=(nj, ni) if order == "ji" else (ni, nj), in_specs=in_specs,
        out_specs=out_specs, out_shape=out_shape, compiler_params=_params(("parallel", "parallel")))(*args)


def mm_reduce_j(name, T, tm, pairs, out_w, out_dtype, res=None, scale=1.0, after=(), jstep=1):
    terms = [(p, u) for u in range(jstep) for p in pairs]
    nt = len(terms)
    nj = N_DEV // jstep
    ni = T // tm

    def body(*refs):
        xs = refs[:nt]
        ws = refs[nt:2 * nt]
        rest = refs[2 * nt:len(refs) - 2 - len(after)] + refs[len(refs) - 2:]
        if res is not None:
            res_ref, o_ref, acc = rest
        else:
            o_ref, acc = rest
        j = pl.program_id(1)

        @pl.when(j == 0)
        def _():
            acc[...] = jnp.zeros_like(acc)

        tot = None
        for k, (p, _) in enumerate(terms):
            if p[5]:
                d = lax.dot_general(xs[k][...], ws[k][...], _NT, preferred_element_type=F32)
            else:
                d = jnp.dot(xs[k][...], ws[k][...], preferred_element_type=F32)
            tot = d if tot is None else tot + d
        acc[...] += tot

        @pl.when(j == nj - 1)
        def _():
            if res is not None:
                o_ref[...] = (res_ref[...] + scale * acc[...]).astype(o_ref.dtype)
            else:
                o_ref[...] = acc[...].astype(o_ref.dtype)

    def jmap(u):
        return lambda j: j * jstep + u

    in_specs = [_bspec(p[1], tm, _width(p[0], p[1], p[2]), p[3], "ij", jmap(u)) for (p, u) in terms]
    in_specs += [_wspec(p[4], "ij", jmap(u)) for (p, u) in terms]
    args = [p[0] for (p, _) in terms] + [p[4] for (p, _) in terms]
    if res is not None:
        in_specs.append(pl.BlockSpec((tm, out_w), lambda i, j: (i, 0)))
        args.append(res)
    in_specs += [_ANY] * len(after)
    args += list(after)
    return pl.pallas_call(
        body, name=name, grid=(ni, nj), in_specs=in_specs,
        out_specs=pl.BlockSpec((tm, out_w), lambda i, j: (i, 0)),
        out_shape=jax.ShapeDtypeStruct((T, out_w), out_dtype),
        scratch_shapes=[pltpu.VMEM((tm, out_w), F32)],
        compiler_params=_params(("parallel", "arbitrary")))(*args)


def mm_reduce_i(name, T, tm, a, b, after=()):
    ni = T // tm
    rows = _width(a[0], a[1], a[2])
    cols = _width(b[0], b[1], b[2])

    def body(a_ref, b_ref, *rest):
        o_ref, acc = rest[len(after):]
        i = pl.program_id(1)

        @pl.when(i == 0)
        def _():
            acc[...] = jnp.zeros_like(acc)

        acc[...] += lax.dot_general(a_ref[...], b_ref[...], _TN, preferred_element_type=F32)

        @pl.when(i == ni - 1)
        def _():
            o_ref[...] = acc[...].astype(o_ref.dtype)

    return pl.pallas_call(
        body, name=name, grid=(N_DEV, ni),
        in_specs=[_bspec(a[1], tm, rows, a[3], "ji"), _bspec(b[1], tm, cols, b[3], "ji")] + [_ANY] * len(after),
        out_specs=pl.BlockSpec((None, rows, cols), lambda j, i: (j, 0, 0)),
        out_shape=jax.ShapeDtypeStruct((N_DEV, rows, cols), CDT),
        scratch_shapes=[pltpu.VMEM((rows, cols), F32)],
        compiler_params=_params(("parallel", "arbitrary")))(a[0], b[0], *after)


def _rms_bwd_math(xv, g, dy):
    r = lax.rsqrt(jnp.mean(xv * xv, axis=-1, keepdims=True) + EPS)
    xn = xv * r
    dxn = dy * g
    dx = r * (dxn - xn * jnp.mean(dxn * xn, axis=-1, keepdims=True))
    dg = jnp.sum(dy * xn, axis=0, keepdims=True)
    return dx, dg


def rmsnorm_fwd(name, x, g, tm, after=()):
    T, D = x.shape

    def body(x_ref, g_ref, *rest):
        o_ref = rest[-1]
        xv = x_ref[...]
        r = lax.rsqrt(jnp.mean(xv * xv, axis=-1, keepdims=True) + EPS)
        o_ref[...] = (xv * r * g_ref[...]).astype(o_ref.dtype)

    return pl.pallas_call(
        body, name=name, grid=(T // tm,),
        in_specs=[pl.BlockSpec((tm, D), lambda i: (i, 0)), pl.BlockSpec((1, D), lambda i: (0, 0))]
        + [_ANY] * len(after),
        out_specs=pl.BlockSpec((tm, D), lambda i: (i, 0)),
        out_shape=jax.ShapeDtypeStruct((T, D), CDT),
        compiler_params=_params(("parallel",)))(x, g, *after)


def rmsnorm_bwd(name, x, g, dh, dres, cscale, tm, after=()):
    T, D = x.shape

    def body(x_ref, g_ref, dh_ref, dres_ref, *rest):
        dx_ref, dxc_ref, dg_ref = rest[len(after):]
        i = pl.program_id(0)
        dx, dg = _rms_bwd_math(x_ref[...], g_ref[...], dh_ref[...].astype(F32))
        dx = dres_ref[...] + dx
        dx_ref[...] = dx
        dxc_ref[...] = (cscale * dx).astype(dxc_ref.dtype)

        @pl.when(i == 0)
        def _():
            dg_ref[...] = jnp.zeros_like(dg_ref)

        dg_ref[...] += dg

    row = pl.BlockSpec((tm, D), lambda i: (i, 0))
    vec = pl.BlockSpec((1, D), lambda i: (0, 0))
    return pl.pallas_call(
        body, name=name, grid=(T // tm,), in_specs=[row, vec, row, row] + [_ANY] * len(after),
        out_specs=[row, row, vec],
        out_shape=[jax.ShapeDtypeStruct((T, D), F32), jax.ShapeDtypeStruct((T, D), CDT),
                   jax.ShapeDtypeStruct((1, D), F32)],
        compiler_params=_params(("arbitrary",)))(x, g, dh, dres, *after)


def loss_head(x, g, tgt, tm):
    T, D = x.shape

    def body(x_ref, g_ref, t_ref, dx_ref, dxc_ref, dg_ref, loss_ref):
        i = pl.program_id(0)
        xv = x_ref[...]
        gv = g_ref[...]
        r = lax.rsqrt(jnp.mean(xv * xv, axis=-1, keepdims=True) + EPS)
        err = xv * r * gv - t_ref[...]
        part = jnp.sum(jnp.mean(err * err, axis=-1, keepdims=True), axis=0, keepdims=True)
        dx, dg = _rms_bwd_math(xv, gv, err / D)
        dx_ref[...] = dx
        dxc_ref[...] = (0.5 * dx).astype(dxc_ref.dtype)

        @pl.when(i == 0)
        def _():
            dg_ref[...] = jnp.zeros_like(dg_ref)
            loss_ref[...] = jnp.zeros_like(loss_ref)

        dg_ref[...] += dg
        loss_ref[...] += jnp.broadcast_to(0.5 * part, loss_ref.shape)

    row = pl.BlockSpec((tm, D), lambda i: (i, 0))
    vec = pl.BlockSpec((1, D), lambda i: (0, 0))
    return pl.pallas_call(
        body, name="loss_head", grid=(T // tm,), in_specs=[row, vec, row],
        out_specs=[row, row, vec, pl.BlockSpec((1, LANE), lambda i: (0, 0))],
        out_shape=[jax.ShapeDtypeStruct((T, D), F32), jax.ShapeDtypeStruct((T, D), CDT),
                   jax.ShapeDtypeStruct((1, D), F32), jax.ShapeDtypeStruct((1, LANE), F32)],
        compiler_params=_params(("arbitrary",)))(x, g, tgt)


def _skew_rows(z, left):
    tq, kw = z.shape
    row = lax.broadcasted_iota(jnp.int32, (tq, kw), 0)
    s = 1
    while s < tq:
        z = jnp.where((row & s) != 0, pltpu.roll(z, kw - s if left else s, 1), z)
        s *= 2
    return z


REL_HI = BAND + MAX_REL_DIST
REL_LO = BAND - MAX_REL_DIST


def attn_bias(rel_bias, tq):
    H = rel_bias.shape[0]
    kw = BAND + tq
    by_skew = jnp.concatenate(
        [jnp.broadcast_to(rel_bias[:, N_REL - 1:], (H, REL_LO)), rel_bias[:, ::-1],
         jnp.broadcast_to(rel_bias[:, :1], (H, kw - REL_HI - 1))], axis=1).reshape(H, 1, kw)

    def body(t_ref, o_ref):
        t = t_ref[...]
        qi = lax.broadcasted_iota(jnp.int32, (tq, kw), 0)
        kj = lax.broadcasted_iota(jnp.int32, (tq, kw), 1)
        b = _skew_rows(jnp.broadcast_to(t, (tq, kw)), left=False)
        b = jnp.where(kj < qi, t[:, 0:1], b)
        qc = qi // CHUNK
        kc = kj // CHUNK - N_PREV_CHUNKS
        valid = (kc <= qc) & (kc >= qc - N_PREV_CHUNKS)
        o_ref[...] = jnp.where(valid, b, NEG)

    return pl.pallas_call(
        body, name="attn_bias", grid=(H,),
        in_specs=[pl.BlockSpec((None, 1, kw), lambda h: (h, 0, 0))],
        out_specs=pl.BlockSpec((None, tq, kw), lambda h: (h, 0, 0)),
        out_shape=jax.ShapeDtypeStruct((H, tq, kw), F32),
        compiler_params=_params(("parallel",)))(by_skew)


def attn_bias_grad(dst, tq):
    H = dst.shape[0]
    kw = BAND + tq

    def body(d_ref, o_ref):
        z = _skew_rows(d_ref[...], left=True)
        qi = lax.broadcasted_iota(jnp.int32, (tq, kw), 0)
        kj = lax.broadcasted_iota(jnp.int32, (tq, kw), 1)
        wrapped = kj + qi >= kw
        c = jnp.sum(jnp.where(wrapped, 0.0, z), axis=0, keepdims=True)
        cw = jnp.sum(jnp.sum(jnp.where(wrapped, z, 0.0), axis=0, keepdims=True), axis=1, keepdims=True)
        lane = lax.broadcasted_iota(jnp.int32, (1, kw), 1)
        ahead = jnp.sum(jnp.where(lane >= REL_HI, c, 0.0), axis=1, keepdims=True)
        behind = jnp.sum(jnp.where(lane <= REL_LO, c, 0.0), axis=1, keepdims=True) + cw
        o_ref[...] = jnp.where(lane == REL_HI, ahead, jnp.where(lane == REL_LO, behind, c))

    by_skew = pl.pallas_call(
        body, name="attn_bias_grad", grid=(H,),
        in_specs=[pl.BlockSpec((None, tq, kw), lambda h: (h, 0, 0))],
        out_specs=pl.BlockSpec((None, 1, kw), lambda h: (h, 0, 0)),
        out_shape=jax.ShapeDtypeStruct((H, 1, kw), F32),
        compiler_params=_params(("parallel",)))(dst)
    return by_skew[:, 0, REL_LO:REL_HI + 1][:, ::-1]


def _attn_scores(q, kpad, bm_ref, start, kw):
    k = kpad[pl.ds(start, kw), :]
    s = lax.dot_general(q, k, _NT, preferred_element_type=F32) * (HEAD_DIM ** -0.5) + bm_ref[...]
    col = lax.broadcasted_iota(jnp.int32, s.shape, 1)
    s = jnp.where(col < BAND - start, NEG, s)
    m = jnp.max(s, axis=-1, keepdims=True)
    e = jnp.exp(s - m)
    return e, 1.0 / jnp.sum(e, axis=-1, keepdims=True), k


def _fill_padded(pad_ref, src_ref, T):
    pad_ref[pl.ds(0, BAND), :] = jnp.zeros((BAND, HEAD_DIM), pad_ref.dtype)
    pad_ref[pl.ds(BAND, T), :] = src_ref[...]


def attn_fwd(proj, biasm, A, tq):
    T = proj.shape[0]
    H = A // HEAD_DIM
    kw = BAND + tq

    def body(q_ref, k_ref, v_ref, bm_ref, o_ref, kpad, vpad):
        qi = pl.program_id(1)

        @pl.when(qi == 0)
        def _():
            _fill_padded(kpad, k_ref, T)
            _fill_padded(vpad, v_ref, T)

        start = pl.multiple_of(qi * tq, tq)
        e, rinv, _ = _attn_scores(q_ref[...], kpad, bm_ref, start, kw)
        v = vpad[pl.ds(start, kw), :]
        o_ref[...] = (jnp.dot(e.astype(CDT), v, preferred_element_type=F32) * rinv).astype(o_ref.dtype)

    return pl.pallas_call(
        body, name="attn_fwd", grid=(H, T // tq),
        in_specs=[pl.BlockSpec((tq, HEAD_DIM), lambda h, i: (i, h)),
                  pl.BlockSpec((T, HEAD_DIM), lambda h, i: (0, H + h)),
                  pl.BlockSpec((T, HEAD_DIM), lambda h, i: (0, 2 * H + h)),
                  pl.BlockSpec((None, tq, kw), lambda h, i: (h, 0, 0))],
        out_specs=pl.BlockSpec((tq, HEAD_DIM), lambda h, i: (i, h)),
        out_shape=jax.ShapeDtypeStruct((T, A), CDT),
        scratch_shapes=[pltpu.VMEM((BAND + T, HEAD_DIM), CDT), pltpu.VMEM((BAND + T, HEAD_DIM), CDT)],
        compiler_params=_params(("parallel", "arbitrary")))(proj, proj, proj, biasm)


def attn_bwd(proj, biasm, datt, A, tq):
    T = proj.shape[0]
    H = A // HEAD_DIM
    kw = BAND + tq
    nq = T // tq
    scale = HEAD_DIM ** -0.5

    def body(q_ref, k_ref, v_ref, bm_ref, do_ref, dq_ref, dk_ref, dv_ref, dst_ref,
             kpad, vpad, dkacc, dvacc):
        qi = pl.program_id(1)

        @pl.when(qi == 0)
        def _():
            _fill_padded(kpad, k_ref, T)
            _fill_padded(vpad, v_ref, T)
            dkacc[...] = jnp.zeros_like(dkacc)
            dvacc[...] = jnp.zeros_like(dvacc)
            dst_ref[...] = jnp.zeros_like(dst_ref)

        start = pl.multiple_of(qi * tq, tq)
        q = q_ref[...]
        e, rinv, k = _attn_scores(q, kpad, bm_ref, start, kw)
        p = e * rinv
        v = vpad[pl.ds(start, kw), :]
        do = do_ref[...]
        dp = lax.dot_general(do, v, _NT, preferred_element_type=F32)
        ds = p * (dp - jnp.sum(dp * p, axis=-1, keepdims=True))
        dst_ref[...] += ds
        dsb = ds.astype(CDT)
        dq_ref[...] = (jnp.dot(dsb, k, preferred_element_type=F32) * scale).astype(dq_ref.dtype)
        dkacc[pl.ds(start, kw), :] += lax.dot_general(dsb, q, _TN, preferred_element_type=F32) * scale
        dvacc[pl.ds(start, kw), :] += lax.dot_general(p.astype(CDT), do, _TN, preferred_element_type=F32)

        @pl.when(qi == nq - 1)
        def _():
            dk_ref[...] = dkacc[pl.ds(BAND, T), :].astype(dk_ref.dtype)
            dv_ref[...] = dvacc[pl.ds(BAND, T), :].astype(dv_ref.dtype)

    blk = pl.BlockSpec((tq, HEAD_DIM), lambda h, i: (i, h))
    col = pl.BlockSpec((T, HEAD_DIM), lambda h, i: (0, h))
    bias = pl.BlockSpec((None, tq, kw), lambda h, i: (h, 0, 0))
    return pl.pallas_call(
        body, name="attn_bwd", grid=(H, nq),
        in_specs=[blk,
                  pl.BlockSpec((T, HEAD_DIM), lambda h, i: (0, H + h)),
                  pl.BlockSpec((T, HEAD_DIM), lambda h, i: (0, 2 * H + h)),
                  bias, blk],
        out_specs=[blk, col, col, bias],
        out_shape=[jax.ShapeDtypeStruct((T, A), CDT), jax.ShapeDtypeStruct((T, A), CDT),
                   jax.ShapeDtypeStruct((T, A), CDT), jax.ShapeDtypeStruct((H, tq, kw), F32)],
        scratch_shapes=[pltpu.VMEM((BAND + T, HEAD_DIM), CDT), pltpu.VMEM((BAND + T, HEAD_DIM), CDT),
                        pltpu.VMEM((BAND + T, HEAD_DIM), F32), pltpu.VMEM((BAND + T, HEAD_DIM), F32)],
        compiler_params=_params(("parallel", "arbitrary")))(proj, proj, proj, biasm, datt)


def _retention_tables(T, H, blk):
    half = HEAD_DIM // 2
    inv = 1.0 / (ROPE_BASE ** (jnp.arange(0, HEAD_DIM, 2, dtype=F32) / HEAD_DIM))
    ang = jnp.arange(T, dtype=F32)[:, None] * inv[None, :]
    cos, sin = jnp.cos(ang), jnp.sin(ang)
    rc = jnp.concatenate([cos, cos], axis=1)
    rs = jnp.concatenate([-sin, sin], axis=1)
    assert rc.shape == (T, 2 * half)
    log_g = jnp.log(1.0 - 2.0 ** (-5.0 - jnp.arange(H, dtype=F32)))[:, None, None]
    idx = jnp.arange(blk, dtype=F32)
    n, m = idx[:, None], idx[None, :]
    same = (n // CHUNK) == (m // CHUNK)
    earlier = (m // CHUNK) < (n // CHUNK)
    dist = jnp.where(same, jnp.abs(n - m), n - m)[None]
    dmat = jnp.where((same | earlier)[None], jnp.exp(log_g * dist), 0.0)
    ones = jnp.ones((1, 1, HEAD_DIM), F32)
    qd = jnp.exp(log_g * (idx[None, :, None] + 1.0)) * ones
    kd = jnp.exp(log_g * (blk - 1.0 - idx[None, :, None])) * ones
    cd = jnp.exp(log_g * blk) * jnp.ones((1, 8, HEAD_DIM), F32)
    return rc, rs, dmat, qd, kd, cd


def _rot(v, rc, rs):
    return v * rc + pltpu.roll(v, HEAD_DIM // 2, 1) * rs


def _rot_bwd(dv, rc, rs):
    return dv * rc + pltpu.roll(dv * rs, HEAD_DIM // 2, 1)


def ret_fwd(proj, tables, A, blk):
    T = proj.shape[0]
    H = A // HEAD_DIM
    nb = T // blk
    hp = RET_HEADS_PER_STEP
    rc, rs, dmat, qd, kd, cd = tables
    scale = HEAD_DIM ** -0.5

    def body(q_ref, k_ref, v_ref, g_ref, rc_ref, rs_ref, d_ref, qd_ref, kd_ref, cd_ref,
             y_ref, o_ref, st_ref, state):
        b = pl.program_id(1)

        @pl.when(b == 0)
        def _():
            state[...] = jnp.zeros_like(state)

        c, s = rc_ref[...], rs_ref[...]
        for u in range(hp):
            cols = pl.ds(u * HEAD_DIM, HEAD_DIM)
            qs = (_rot(q_ref[:, cols].astype(F32), c, s) * scale).astype(CDT)
            kr = _rot(k_ref[:, cols].astype(F32), c, s)
            v = v_ref[:, cols]
            sb = state[u].astype(CDT)
            a = lax.dot_general(qs, kr.astype(CDT), _NT, preferred_element_type=F32) * d_ref[u]
            o = jnp.dot(a.astype(CDT), v, preferred_element_type=F32)
            o = o + jnp.dot(qs, sb, preferred_element_type=F32) * qd_ref[u]
            st_ref[u] = sb
            state[u] = state[u] * cd_ref[u, 0:1, :] + lax.dot_general(
                (kr * kd_ref[u]).astype(CDT), v, _TN, preferred_element_type=F32)
            o_ref[:, cols] = o
            on = o * lax.rsqrt(jnp.mean(o * o, axis=-1, keepdims=True) + EPS)
            g = g_ref[:, cols].astype(F32)
            y_ref[:, cols] = (g * _sigmoid(g) * on).astype(y_ref.dtype)

    w = hp * HEAD_DIM

    def pj(off):
        return pl.BlockSpec((blk, w), lambda h, i: (i, off * H // hp + h))

    tok = pl.BlockSpec((blk, HEAD_DIM), lambda h, i: (i, 0))
    out = pl.BlockSpec((blk, w), lambda h, i: (i, h))

    def per_head(r, c):
        return pl.BlockSpec((hp, r, c), lambda h, i: (h, 0, 0))

    return pl.pallas_call(
        body, name="ret_fwd", grid=(H // hp, nb),
        in_specs=[pj(3), pj(4), pj(5), pj(6), tok, tok, per_head(blk, blk),
                  per_head(blk, HEAD_DIM), per_head(blk, HEAD_DIM), per_head(8, HEAD_DIM)],
        out_specs=[out, out, pl.BlockSpec((hp, None, HEAD_DIM, HEAD_DIM), lambda h, i: (h, i, 0, 0))],
        out_shape=[jax.ShapeDtypeStruct((T, A), CDT), jax.ShapeDtypeStruct((T, A), F32),
                   jax.ShapeDtypeStruct((H, nb, HEAD_DIM, HEAD_DIM), CDT)],
        scratch_shapes=[pltpu.VMEM((hp, HEAD_DIM, HEAD_DIM), F32)],
        compiler_params=_params(("parallel", "arbitrary")))(
            proj, proj, proj, proj, rc, rs, dmat, qd, kd, cd)


def ret_bwd(proj, tables, o_raw, states, dy, A, blk):
    T = proj.shape[0]
    H = A // HEAD_DIM
    nb = T // blk
    hp = RET_BWD_HEADS_PER_STEP
    rc, rs, dmat, qd, kd, cd = tables
    scale = HEAD_DIM ** -0.5

    def body(q_ref, k_ref, v_ref, g_ref, rc_ref, rs_ref, d_ref, qd_ref, kd_ref, cd_ref,
             o_ref, st_ref, dy_ref, dq_ref, dk_ref, dv_ref, dg_ref, dstate):
        b = pl.program_id(1)

        @pl.when(b == 0)
        def _():
            dstate[...] = jnp.zeros_like(dstate)

        c, s = rc_ref[...], rs_ref[...]
        for u in range(hp):
            cols = pl.ds(u * HEAD_DIM, HEAD_DIM)
            qs = (_rot(q_ref[:, cols].astype(F32), c, s) * scale).astype(CDT)
            kr = _rot(k_ref[:, cols].astype(F32), c, s)
            krb = kr.astype(CDT)
            kdb = (kr * kd_ref[u]).astype(CDT)
            v = v_ref[:, cols]
            dmat_v = d_ref[u]
            a = lax.dot_general(qs, krb, _NT, preferred_element_type=F32) * dmat_v

            o = o_ref[:, cols]
            r = lax.rsqrt(jnp.mean(o * o, axis=-1, keepdims=True) + EPS)
            on = o * r
            g = g_ref[:, cols].astype(F32)
            sg = _sigmoid(g)
            dyv = dy_ref[:, cols].astype(F32)
            dg_ref[:, cols] = (dyv * on * (sg * (1.0 + g * (1.0 - sg)))).astype(dg_ref.dtype)
            don = dyv * (g * sg)
            do = r * (don - on * jnp.mean(don * on, axis=-1, keepdims=True))
            dob = do.astype(CDT)
            doq = (do * qd_ref[u]).astype(CDT)
            dsb = dstate[u].astype(CDT)

            dv = lax.dot_general(a.astype(CDT), dob, _TN, preferred_element_type=F32)
            dv = dv + jnp.dot(kdb, dsb, preferred_element_type=F32)
            dv_ref[:, cols] = dv.astype(dv_ref.dtype)
            dpb = (lax.dot_general(dob, v, _NT, preferred_element_type=F32) * dmat_v).astype(CDT)
            dqs = jnp.dot(dpb, krb, preferred_element_type=F32)
            dqs = dqs + lax.dot_general(doq, st_ref[u], _NT, preferred_element_type=F32)
            dkr = lax.dot_general(dpb, qs, _TN, preferred_element_type=F32)
            dkr = dkr + lax.dot_general(v, dsb, _NT, preferred_element_type=F32) * kd_ref[u]
            dstate[u] = dstate[u] * cd_ref[u, 0:1, :] + lax.dot_general(
                qs, doq, _TN, preferred_element_type=F32)
            dq_ref[:, cols] = _rot_bwd(dqs * scale, c, s).astype(dq_ref.dtype)
            dk_ref[:, cols] = _rot_bwd(dkr, c, s).astype(dk_ref.dtype)

    w = hp * HEAD_DIM

    def pj(off):
        return pl.BlockSpec((blk, w), lambda h, i: (nb - 1 - i, off * H // hp + h))

    tok = pl.BlockSpec((blk, HEAD_DIM), lambda h, i: (nb - 1 - i, 0))
    out = pl.BlockSpec((blk, w), lambda h, i: (nb - 1 - i, h))

    def per_head(r, c):
        return pl.BlockSpec((hp, r, c), lambda h, i: (h, 0, 0))

    shp = jax.ShapeDtypeStruct((T, A), CDT)
    return pl.pallas_call(
        body, name="ret_bwd", grid=(H // hp, nb),
        in_specs=[pj(3), pj(4), pj(5), pj(6), tok, tok, per_head(blk, blk),
                  per_head(blk, HEAD_DIM), per_head(blk, HEAD_DIM), per_head(8, HEAD_DIM),
                  out, pl.BlockSpec((hp, None, HEAD_DIM, HEAD_DIM), lambda h, i: (h, nb - 1 - i, 0, 0)),
                  out],
        out_specs=[out, out, out, out], out_shape=[shp, shp, shp, shp],
        scratch_shapes=[pltpu.VMEM((hp, HEAD_DIM, HEAD_DIM), F32)],
        compiler_params=_params(("parallel", "arbitrary")))(
            proj, proj, proj, proj, rc, rs, dmat, qd, kd, cd, o_raw, states, dy)


def _mesh_pos():
    return lax.axis_index("x"), lax.axis_index("y"), lax.axis_index("c")


def _flat(pos):
    return 4 * pos[0] + 2 * pos[1] + pos[2]


_HBM = pl.BlockSpec(memory_space=pltpu.HBM)


def cast_shard(name, w, rows_p, cols_p, me_arr, after=()):
    r, c = w.shape
    tr = _row_tile(math.gcd(r, rows_p), 256)
    nr = r // tr

    def body(me_ref, w_ref, *rest):
        o_ref = rest[-1]
        i = pl.program_id(0)
        o_ref[...] = jnp.zeros_like(o_ref)

        @pl.when(i < nr)
        def _():
            o_ref[:, 0:c] = w_ref[...].astype(o_ref.dtype)

    return pl.pallas_call(
        body, name=name,
        grid_spec=pltpu.PrefetchScalarGridSpec(
            num_scalar_prefetch=1, grid=(rows_p // tr,),
            in_specs=[pl.BlockSpec((tr, c), lambda i, me: (jnp.minimum(i, nr - 1), 0))] + [_ANY] * len(after),
            out_specs=pl.BlockSpec((None, tr, cols_p), lambda i, me: (me[0], i, 0))),
        out_shape=jax.ShapeDtypeStruct((N_DEV, rows_p, cols_p), CDT),
        compiler_params=_params(("arbitrary",)))(me_arr, w, *after)


def exchange_partials(arrays, after):
    n, na = len(arrays), len(after)

    def body(*refs):
        ins, outs = refs[:n], refs[n + na:2 * n + na]
        send_sems, recv_sems, local_sems = refs[2 * n + na:]
        me = _mesh_pos()
        copies, locals_ = [], []
        for t in range(n):
            cp = pltpu.make_async_copy(ins[t], outs[t].at[_flat(me)], local_sems.at[t])
            cp.start()
            locals_.append(cp)
            for k in range(1, N_DEV):
                peer = _peer(me, k)
                send = pltpu.make_async_remote_copy(
                    src_ref=ins[t], dst_ref=outs[t].at[_flat(me)],
                    send_sem=send_sems.at[t, k - 1], recv_sem=recv_sems.at[t, k - 1],
                    device_id=peer, device_id_type=pl.DeviceIdType.MESH)
                send.start()
                recv = pltpu.make_async_remote_copy(
                    src_ref=ins[t], dst_ref=outs[t].at[_flat(peer)],
                    send_sem=send_sems.at[t, k - 1], recv_sem=recv_sems.at[t, k - 1],
                    device_id=peer, device_id_type=pl.DeviceIdType.MESH)
                copies.append((send, recv))
        for send, recv in copies:
            recv.wait_recv()
        for send, recv in copies:
            send.wait_send()
        for cp in locals_:
            cp.wait()

    return pl.pallas_call(
        body, name="exchange_partials",
        in_specs=[_HBM] * n + [_ANY] * na, out_specs=[_HBM] * n,
        out_shape=[jax.ShapeDtypeStruct((N_DEV,) + a.shape, a.dtype) for a in arrays],
        scratch_shapes=[pltpu.SemaphoreType.DMA((n, 7)), pltpu.SemaphoreType.DMA((n, 7)),
                        pltpu.SemaphoreType.DMA((n,))],
        )(*arrays, *after)


_SEM = pl.BlockSpec(memory_space=pltpu.SEMAPHORE)
_ANY = pl.BlockSpec(memory_space=pl.ANY)
_EFFECT = pltpu.SideEffectType.DATAFLOW_SIDE_EFFECTING


def _peer(me, k):
    return tuple(1 - v if bit else v for v, bit in zip(me, (k >> 2, (k >> 1) & 1, k & 1)))


_CHIP_MASKS = (2, 4, 6)

_EXCHANGE_MODES = {"gather": (7, None), "gather_chips": (4, None), "forward": (3, None),
                   "scatter": (7, 7), "scatter_pair": (4, 4), "scatter_chips": (3, 3)}


def _plan(mode, bufs, n, me):
    per = _EXCHANGE_MODES[mode][0]
    sib = _peer(me, 1)
    plan = []
    for t in range(n):
        src_arr, land_arr = bufs[t], bufs[n + t] if _EXCHANGE_MODES[mode][1] else None
        if mode in ("gather", "gather_chips"):
            masks = range(1, N_DEV) if mode == "gather" else (1,) + _CHIP_MASKS
            rows = [(src_arr.at[_flat(me)], src_arr.at[_flat(me)], _peer(me, k), src_arr.at[_flat(_peer(me, k))])
                    for k in masks]
        elif mode == "forward":
            rows = [(src_arr.at[_flat(_peer(me, k))], src_arr.at[_flat(_peer(me, k))], sib,
                     src_arr.at[_flat(_peer(sib, k))]) for k in _CHIP_MASKS]
        elif mode == "scatter":
            rows = [(src_arr.at[_flat(_peer(me, k))], land_arr.at[k - 1], _peer(me, k), land_arr.at[k - 1])
                    for k in range(1, N_DEV)]
        elif mode == "scatter_pair":
            rows = [(src_arr.at[_flat(_peer(me, q + 1))], land_arr.at[qi], sib, land_arr.at[qi])
                    for qi, q in enumerate((0,) + _CHIP_MASKS)]
        else:
            assert mode == "scatter_chips"
            rows = [(src_arr.at[qi + 1], land_arr.at[qi], _peer(me, q), land_arr.at[qi])
                    for qi, q in enumerate(_CHIP_MASKS)]
        assert len(rows) == per
        plan += [(t * per + s,) + row for s, row in enumerate(rows)]
    return plan


def exchange_start(name, arrays, mode, after):
    n, na = len(arrays), len(after)
    per, slots = _EXCHANGE_MODES[mode]
    bufs = list(arrays)
    if slots:
        bufs += [lax.empty((slots,) + a.shape[1:], a.dtype) for a in arrays]
    nb = len(bufs)

    def body(*refs):
        send_sems, recv_sems = refs[nb + na], refs[nb + na + 1]
        token = refs[-1]
        for s, src, dst, dev, _ in _plan(mode, refs[:nb], n, _mesh_pos()):
            pltpu.make_async_remote_copy(
                src_ref=src, dst_ref=dst, send_sem=send_sems.at[s], recv_sem=recv_sems.at[s],
                device_id=dev, device_id_type=pl.DeviceIdType.MESH).start()
        token[...] = jnp.zeros_like(token)

    out_shape = [pltpu.SemaphoreType.DMA((n * per,)), pltpu.SemaphoreType.DMA((n * per,))]
    out_shape += [pltpu.HBM(a.shape, a.dtype) for a in bufs]
    out_shape.append(jax.ShapeDtypeStruct((8, LANE), F32))
    args = [pltpu.with_memory_space_constraint(a, pltpu.HBM) for a in bufs] + list(after)
    outs = pl.pallas_call(
        body, name=name, out_shape=out_shape,
        in_specs=[_HBM] * nb + [_ANY] * na,
        out_specs=[_SEM, _SEM] + [_HBM] * nb + [pl.BlockSpec(memory_space=pltpu.VMEM)],
        input_output_aliases={i: 2 + i for i in range(nb)},
        compiler_params=pltpu.CompilerParams(has_side_effects=_EFFECT))(*args)
    return outs[0], outs[1], list(outs[2:2 + nb]), outs[-1]


def exchange_wait(name, started, mode, after):
    send_sems, recv_sems, bufs, _ = started
    nb, na = len(bufs), len(after)
    n = nb // 2 if _EXCHANGE_MODES[mode][1] else nb

    def body(*refs):
        send_sems_ref, recv_sems_ref = refs[nb], refs[nb + 1]
        for s, src, _, dev, land in _plan(mode, refs[:nb], n, _mesh_pos()):
            cp = pltpu.make_async_remote_copy(
                src_ref=src, dst_ref=land, send_sem=send_sems_ref.at[s], recv_sem=recv_sems_ref.at[s],
                device_id=dev, device_id_type=pl.DeviceIdType.MESH)
            cp.wait_send()
            cp.wait_recv()

    outs = pl.pallas_call(
        body, name=name, out_shape=[pltpu.HBM(a.shape, a.dtype) for a in bufs],
        in_specs=[_HBM] * nb + [_SEM, _SEM] + [_ANY] * na, out_specs=[_HBM] * nb,
        input_output_aliases={i: i for i in range(nb)},
        compiler_params=pltpu.CompilerParams(has_side_effects=_EFFECT))(
            *bufs, send_sems, recv_sems, *after)
    return list(outs)


def pair_sum(name, own, landed, blocks):
    _, r, c = own.shape
    tr = _row_tile(r, 256)

    def body(idx_ref, o_ref, l_ref, s_ref):
        s_ref[...] = (o_ref[...].astype(F32) + l_ref[...].astype(F32)).astype(s_ref.dtype)

    blk = pl.BlockSpec((None, tr, c), lambda q, i, idx: (q, i, 0))
    return pl.pallas_call(
        body, name=name,
        grid_spec=pltpu.PrefetchScalarGridSpec(
            num_scalar_prefetch=1, grid=(4, r // tr),
            in_specs=[pl.BlockSpec((None, tr, c), lambda q, i, idx: (idx[q], i, 0)), blk],
            out_specs=blk),
        out_shape=jax.ShapeDtypeStruct((4, r, c), own.dtype),
        compiler_params=_params(("parallel", "parallel")))(blocks, own, landed)


def _adamw_math(w, g, m, v):
    m = ADAM_B1 * m + (1.0 - ADAM_B1) * g
    v = ADAM_B2 * v + (1.0 - ADAM_B2) * (g * g)
    m_hat = m / (1.0 - ADAM_B1 ** ADAM_STEP)
    v_hat = v / (1.0 - ADAM_B2 ** ADAM_STEP)
    delta = -ADAM_LR * (m_hat / (jnp.sqrt(v_hat) + ADAM_EPS) + ADAM_WD * w)
    return delta, m, v


def reduce_adamw(name, land, w, m, v, tr, own=None, me_arr=None):
    R, C = w.shape
    S, _, Cp = land.shape

    def body(*refs):
        if own is not None:
            _, own_ref, l_ref, w_ref, m_ref, v_ref, g_ref, d_ref, nm_ref, nv_ref = refs
            g = own_ref[:, 0:C].astype(F32)
            first = 0
        else:
            l_ref, w_ref, m_ref, v_ref, g_ref, d_ref, nm_ref, nv_ref = refs
            g = l_ref[0, :, 0:C].astype(F32)
            first = 1
        for s in range(first, S):
            g = g + l_ref[s, :, 0:C].astype(F32)
        delta, nm, nv = _adamw_math(w_ref[...], g, m_ref[...], v_ref[...])
        g_ref[...] = g
        d_ref[...] = delta
        nm_ref[...] = nm
        nv_ref[...] = nv

    shp = jax.ShapeDtypeStruct((R, C), F32)
    if own is None:
        blk = pl.BlockSpec((tr, C), lambda i: (i, 0))
        return pl.pallas_call(
            body, name=name, grid=(R // tr,),
            in_specs=[pl.BlockSpec((S, tr, Cp), lambda i: (0, i, 0)), blk, blk, blk],
            out_specs=[blk, blk, blk, blk], out_shape=[shp, shp, shp, shp],
            compiler_params=_params(("parallel",)))(land, w, m, v)
    blk = pl.BlockSpec((tr, C), lambda i, me: (i, 0))
    return pl.pallas_call(
        body, name=name,
        grid_spec=pltpu.PrefetchScalarGridSpec(
            num_scalar_prefetch=1, grid=(R // tr,),
            in_specs=[pl.BlockSpec((None, tr, Cp), lambda i, me: (me[0], i, 0)),
                      pl.BlockSpec((S, tr, Cp), lambda i, me: (0, i, 0)), blk, blk, blk],
            out_specs=[blk, blk, blk, blk]),
        out_shape=[shp, shp, shp, shp],
        compiler_params=_params(("parallel",)))(me_arr, own, land, w, m, v)


def _row_tile(r, cap):
    t = min(r, cap)
    while r % t or t % 8:
        t -= 8
    return t


def kernel(x, norm_ffn1_g, ffn1_w_gate, ffn1_w_up, ffn1_w_down, norm_mix_g, w_in, rel_bias, w_out_att, w_out_ret, w_out, norm_ffn2_g, ffn2_w_gate, ffn2_w_up, ffn2_w_down, norm_final_g, loss_target, m_norm_ffn1_g, m_ffn1_w_gate, m_ffn1_w_up, m_ffn1_w_down, m_norm_mix_g, m_w_in, m_rel_bias, m_w_out_att, m_w_out_ret, m_w_out, m_norm_ffn2_g, m_ffn2_w_gate, m_ffn2_w_up, m_ffn2_w_down, m_norm_final_g, v_norm_ffn1_g, v_ffn1_w_gate, v_ffn1_w_up, v_ffn1_w_down, v_norm_mix_g, v_w_in, v_rel_bias, v_w_out_att, v_w_out_ret, v_w_out, v_norm_ffn2_g, v_ffn2_w_gate, v_ffn2_w_up, v_ffn2_w_down, v_norm_final_g):
    T, D = x.shape[1], x.shape[2]
    A = w_out_att.shape[1]
    H = A // HEAD_DIM
    nf = ffn1_w_gate.shape[2]
    nfp = _round_up(nf, LANE)
    nin = w_in.shape[2]
    nd = w_out.shape[1]
    assert nin % LANE == 0 and nd % LANE == 0 and (7 * A) % nd == 0 and T % ATT_TQ == 0
    tm = min(512, T)
    tb = min(1024, T)
    tw = min(2048, T)
    tw_in = min(1024, T)
    tn = min(256, T)
    x0 = x[0]
    tgt = loss_target[0]

    me_arr = (4 * lax.axis_index("x") + 2 * lax.axis_index("y") + lax.axis_index("c")).astype(jnp.int32).reshape(1)

    def slot(tag, w, after, rows_p=None):
        return cast_shard("cast_" + tag, w[0], rows_p or w.shape[1], w.shape[2], me_arr, after)

    def slot_t(tag, w, after):
        return cast_shard("cast_" + tag, jnp.transpose(w[0]), nfp, w.shape[1], me_arr, after)

    groups = [("wg1", [("wg1", ffn1_w_gate, slot_t)], True), ("wu1", [("wu1", ffn1_w_up, slot_t)], True),
              ("wd1", [("wd1", ffn1_w_down, functools.partial(slot, rows_p=nfp))], True),
              ("win", [("win", w_in, slot)], True),
              ("wout", [("woa", w_out_att, slot), ("wor", w_out_ret, slot), ("wo", w_out, slot)], False),
              ("wgu2", [("wg2", ffn2_w_gate, slot_t), ("wu2", ffn2_w_up, slot_t)], True),
              ("wd2", [("wd2", ffn2_w_down, functools.partial(slot, rows_p=nfp))], True)]
    ag_started = {}
    order = []
    for tag, members, two_level in groups:
        mode = "gather_chips" if two_level else "gather"
        started = exchange_start("ag_start_" + tag, [make(nm, w, order) for nm, w, make in members], mode, order)
        ag_started[tag] = (started, mode)
        order = [started[3]]

    passing = {}

    def begin_pass(tag, after):
        started, mode = ag_started[tag]
        got = exchange_wait("ag_wait_" + tag, started, mode, list(after) if isinstance(after, tuple) else [after])
        passing[tag] = exchange_start("ag_pass_" + tag, got, "forward", [])
        return passing[tag][3]

    def gathered(tag, after):
        started, mode = ag_started[tag]
        if mode == "gather":
            return exchange_wait("ag_wait_" + tag, started, mode, [after])
        if tag not in passing:
            begin_pass(tag, after)
        return exchange_wait("ag_passed_" + tag, passing[tag], "forward",
                             list(after) if isinstance(after, tuple) else [after])

    xi, yi, ci = lax.axis_index("x"), lax.axis_index("y"), lax.axis_index("c")
    my_side = jnp.stack([4 * (1 - xi if q & 4 else xi) + 2 * (1 - yi if q & 2 else yi) + ci
                         for q in (0,) + _CHIP_MASKS]).astype(jnp.int32)
    first_block = jnp.zeros((1,), jnp.int32)

    def swiglu(prods, _):
        a, b = prods
        return a, b, a * _sigmoid(a) * b

    def ffn_fwd(tag, xin, g, get_wgu, get_wd, after=()):
        h = rmsnorm_fwd(tag + "_norm", xin, g, tm, after)
        if isinstance(get_wgu, tuple):
            Wg, = get_wgu[0](h)
            a, = mm_block(tag + "_gate", T, tb, [(h, "full", D, 0)], [(Wg, 0, True)], [], [("3d", nfp, CDT)],
                          lambda p, _: p)
            Wu, = get_wgu[1](a)
            b, mid = mm_block(tag + "_up", T, tb, [(h, "full", D, 0)], [(Wu, 0, True)], [(a, "3d", nfp, 0)],
                              [("3d", nfp, CDT)] * 2, lambda p, ex: (p[0], ex[0] * _sigmoid(ex[0]) * p[0]))
        else:
            Wg, Wu = get_wgu(h)
            a, b, mid = mm_block(tag + "_up", T, tb, [(h, "full", D, 0)], [(Wg, 0, True), (Wu, 0, True)],
                                 [], [("3d", nfp, CDT)] * 3, swiglu)
        Wd, = get_wd(mid)
        xo = mm_reduce_j(tag + "_down", T, tm, [(mid, "3d", nfp, 0, Wd, False)], D, F32, res=xin, scale=0.5,
                         jstep=2)
        return h, a, b, mid, xo, (Wg, Wu, Wd)

    biasm = attn_bias(rel_bias[0], ATT_TQ)
    tables = _retention_tables(T, H, RET_BLK)
    h1, a1, b1, mid1, x1, (Wg1, Wu1, Wd1) = ffn_fwd(
        "ffn1", x0, norm_ffn1_g, (lambda h: gathered("wg1", (h, biasm) + tuple(tables)), lambda a: gathered("wu1", a)),
        lambda mid: gathered("wd1", mid), after=order)
    h2 = rmsnorm_fwd("mix_norm", x1, norm_mix_g, tm)
    Win, = gathered("win", h2)
    proj, = mm_block("in_proj", T, tb, [(h2, "full", D, 0)], [(Win, 0, False)], [], [("col", nin, CDT)],
                     lambda p, _: p)
    att = attn_fwd(proj, biasm, A, ATT_TQ)
    retg, ret_raw, states = ret_fwd(proj, tables, A, RET_BLK)
    Woa, Wor, Wo = gathered("wout", retg)
    goff = 7 * A // nd

    def merge(prods, ex):
        ba, br = prods
        ga, gr = ex
        return ba, br, _sigmoid(ga) * ba + _sigmoid(gr) * br

    ba, br, merged = mm_block(
        "branches", T, tm, [(att, "full", A, 0), (retg, "full", A, 0)], [(Woa, 0, False), (Wor, 1, False)],
        [(proj, "col", nd, goff), (proj, "col", nd, goff + N_DEV)], [("col", nd, CDT)] * 3, merge, order="ij", jgroup=4)
    x2 = mm_reduce_j("out_proj", T, tm, [(merged, "col", nd, 0, Wo, False)], D, F32, res=x1, scale=1.0,
                     jstep=N_DEV, after=[begin_pass("wgu2", merged)])
    h3, a2, b2, mid2, x3, (Wg2, Wu2, Wd2) = ffn_fwd(
        "ffn2", x2, norm_ffn2_g, lambda h: gathered("wgu2", h), lambda mid: gathered("wd2", mid))

    dx3, dx3h, dgf, loss_part = loss_head(x3, norm_final_g.reshape(1, D), tgt, tn)

    def swiglu_bwd(prods, ex):
        dm, = prods
        a, b = ex
        sg = _sigmoid(a)
        return dm * b * (sg * (1.0 + a * (1.0 - sg))), dm * (a * sg)

    def ffn_bwd(tag, dxh, h, a, b, mid, Wg, Wu, Wd, two_level=False):
        da, db = mm_block(tag + "_down_bwd", T, tb, [(dxh, "full", D, 0)], [(Wd, 0, True)],
                          [(a, "3d", nfp, 0), (b, "3d", nfp, 0)], [("3d", nfp, CDT)] * 2, swiglu_bwd)

        def up_bwd(after):
            return mm_reduce_j(tag + "_up_bwd", T, tm,
                               [(da, "3d", nfp, 0, Wg, False), (db, "3d", nfp, 0, Wu, False)],
                               D, CDT, after=after, jstep=2)

        if not two_level:
            dWd = mm_reduce_i(tag + "_dwd", T, tw, (mid, "3d", nfp, 0), (dxh, "full", D, 0))
            dWg = mm_reduce_i(tag + "_dwg", T, tw, (da, "3d", nfp, 0), (h, "full", D, 0))
            dWu = mm_reduce_i(tag + "_dwu", T, tw, (db, "3d", nfp, 0), (h, "full", D, 0))
            sent = exchange_start("rs_start_" + tag, [dWg, dWu, dWd], "scatter", [])
            return up_bwd([sent[3]]), sent

        def swap(nm, grad, after):
            return exchange_start("rs_pair_start_%s_%s" % (tag, nm), [grad], "scatter_pair", after)

        def to_chips(nm, swapping, after):
            own, landed = exchange_wait("rs_pair_wait_%s_%s" % (tag, nm), swapping, "scatter_pair", [after])
            sums = pair_sum("%s_pair_sum_%s" % (tag, nm), own, landed, my_side)
            return exchange_start("rs_chips_start_%s_%s" % (tag, nm), [sums], "scatter_chips", [])

        dWd = mm_reduce_i(tag + "_dwd", T, tw, (mid, "3d", nfp, 0), (dxh, "full", D, 0))
        swap_d = swap("d", dWd, [])
        dWg = mm_reduce_i(tag + "_dwg", T, tw, (da, "3d", nfp, 0), (h, "full", D, 0), after=[swap_d[3]])
        sent_d = to_chips("d", swap_d, dWg)
        swap_g = swap("g", dWg, [sent_d[3]])
        dWu = mm_reduce_i(tag + "_dwu", T, tw, (db, "3d", nfp, 0), (h, "full", D, 0), after=[swap_g[3]])
        sent_g = to_chips("g", swap_g, dWu)
        swap_u = swap("u", dWu, [sent_g[3]])
        dh = up_bwd([swap_u[3]])
        sent_u = to_chips("u", swap_u, dh)
        return dh, [sent_g, sent_u, sent_d]

    dh3, sent_ffn2 = ffn_bwd("ffn2", dx3h, h3, a2, b2, mid2, Wg2, Wu2, Wd2)
    dx2, dx2c, dg2 = rmsnorm_bwd("ffn2_norm_bwd", x2, norm_ffn2_g, dh3, dx3, 1.0, tn)

    def merge_bwd(prods, ex):
        dmg, = prods
        ba_, br_, ga, gr = ex
        sa, sr = _sigmoid(ga), _sigmoid(gr)
        return dmg * sa, dmg * sr, dmg * ba_ * sa * (1.0 - sa), dmg * br_ * sr * (1.0 - sr)

    dba, dbr, dga, dgr = mm_block(
        "out_proj_bwd", T, tm, [(dx2c, "full", D, 0)], [(Wo, 0, True)],
        [(ba, "col", nd, 0), (br, "col", nd, 0), (proj, "col", nd, goff), (proj, "col", nd, goff + N_DEV)],
        [("col", nd, CDT)] * 4, merge_bwd, order="ij", jgroup=4)
    dWo = mm_reduce_i("dwo", T, tw, (merged, "col", nd, 0), (dx2c, "full", D, 0))
    dWoa = mm_reduce_i("dwoa", T, tw, (att, "full", A, 0), (dba, "col", nd, 0))
    dWor = mm_reduce_i("dwor", T, tw, (retg, "full", A, 0), (dbr, "col", nd, 0))
    sent_mix = exchange_start("rs_start_mix", [dWoa, dWor, dWo], "scatter", [])
    datt = mm_reduce_j("att_out_bwd", T, tm, [(dba, "col", nd, 0, Woa, True)], A, CDT, after=[sent_mix[3]],
                       jstep=N_DEV)
    dretg = mm_reduce_j("ret_out_bwd", T, tm, [(dbr, "col", nd, 0, Wor, True)], A, CDT, jstep=N_DEV)
    dq_r, dk_r, dv_r, dg_r = ret_bwd(proj, tables, ret_raw, states, dretg, A, RET_BLK)
    dq_a, dk_a, dv_a, dst = attn_bwd(proj, biasm, datt, A, ATT_TQ)
    dbias = jnp.pad(attn_bias_grad(dst, ATT_TQ), ((0, 0), (0, N_REL_PAD - N_REL)))
    dproj = jnp.concatenate([dq_a, dk_a, dv_a, dq_r, dk_r, dv_r, dg_r, dga, dgr], axis=1)
    dWin = mm_reduce_i("dwin", T, tw_in, (h2, "full", D, 0), (dproj, "col", nin, 0))
    sent_win = exchange_start("rs_start_win", [dWin], "scatter", [])
    dh2 = mm_reduce_j("in_proj_bwd", T, tm, [(dproj, "col", nin, 0, Win, True)], D, CDT, after=[sent_win[3]],
                      jstep=2)
    dx1, dx1h, dgm = rmsnorm_bwd("mix_norm_bwd", x1, norm_mix_g, dh2, dx2, 0.5, tn)
    dh1, sent_ffn1 = ffn_bwd("ffn1", dx1h, h1, a1, b1, mid1, Wg1, Wu1, Wd1, two_level=True)
    grad_x, _, dg1 = rmsnorm_bwd("ffn1_norm_bwd", x0, norm_ffn1_g, dh1, dx1, 1.0, tn, after=[sent_ffn1[1][3]])

    dgains = jnp.concatenate([dg1, dgm, dg2, dgf, jnp.zeros((4, D), F32)], axis=0)

    def upd(name, own, land, w, m, v, own_block=me_arr, transposed=False):
        w2, m2, v2 = [jnp.transpose(t[0]) if transposed else t[0] for t in (w, m, v)]
        outs = reduce_adamw(name, land, w2, m2, v2, _row_tile(w2.shape[0], 256), own=own, me_arr=own_block)
        return [jnp.transpose(o)[None] if transposed else o[None] for o in outs]

    res = {}
    oWg2, oWu2, oWd2, lWg2, lWu2, lWd2 = exchange_wait("rs_wait_ffn2", sent_ffn2, "scatter", [grad_x])
    res["ffn2_w_gate"] = upd("adamw_wg2", oWg2, lWg2, ffn2_w_gate, m_ffn2_w_gate, v_ffn2_w_gate, transposed=True)
    res["ffn2_w_up"] = upd("adamw_wu2", oWu2, lWu2, ffn2_w_up, m_ffn2_w_up, v_ffn2_w_up, transposed=True)
    res["ffn2_w_down"] = upd("adamw_wd2", oWd2, lWd2, ffn2_w_down, m_ffn2_w_down, v_ffn2_w_down)
    oWoa, oWor, oWo, lWoa, lWor, lWo = exchange_wait("rs_wait_mix", sent_mix, "scatter", [res["ffn2_w_down"][1]])
    res["w_out_att"] = upd("adamw_woa", oWoa, lWoa, w_out_att, m_w_out_att, v_w_out_att)
    res["w_out_ret"] = upd("adamw_wor", oWor, lWor, w_out_ret, m_w_out_ret, v_w_out_ret)
    res["w_out"] = upd("adamw_wo", oWo, lWo, w_out, m_w_out, v_w_out)
    oWin, lWin = exchange_wait("rs_wait_win", sent_win, "scatter", [res["w_out"][1]])
    res["w_in"] = upd("adamw_win", oWin, lWin, w_in, m_w_in, v_w_in)
    lgains, lbias = exchange_partials([dgains, dbias], [res["w_in"][1]])
    (oWg1, lWg1), (oWu1, lWu1), (oWd1, lWd1) = [
        exchange_wait("rs_wait_ffn1_" + nm, started, "scatter_chips", [lgains])
        for nm, started in zip("gud", sent_ffn1)]
    res["ffn1_w_gate"] = upd("adamw_wg1", oWg1, lWg1, ffn1_w_gate, m_ffn1_w_gate, v_ffn1_w_gate, first_block, transposed=True)
    res["ffn1_w_up"] = upd("adamw_wu1", oWu1, lWu1, ffn1_w_up, m_ffn1_w_up, v_ffn1_w_up, first_block, transposed=True)
    res["ffn1_w_down"] = upd("adamw_wd1", oWd1, lWd1, ffn1_w_down, m_ffn1_w_down, v_ffn1_w_down, first_block)

    def stack_gains(a, b, c_, d):
        return jnp.concatenate([a, b, c_, d.reshape(1, D), jnp.zeros((4, D), F32)], axis=0)

    gw = stack_gains(norm_ffn1_g, norm_mix_g, norm_ffn2_g, norm_final_g)
    gm = stack_gains(m_norm_ffn1_g, m_norm_mix_g, m_norm_ffn2_g, m_norm_final_g)
    gv = stack_gains(v_norm_ffn1_g, v_norm_mix_g, v_norm_ffn2_g, v_norm_final_g)
    gains = reduce_adamw("adamw_gains", lgains, gw, gm, gv, 8)

    def padb(t):
        return jnp.pad(t[0], ((0, 0), (0, N_REL_PAD - N_REL)))

    bias = [o[:, :N_REL][None] for o in
            reduce_adamw("adamw_bias", lbias, padb(rel_bias), padb(m_rel_bias), padb(v_rel_bias), H)]
    res["norm_ffn1_g"] = [o[0:1] for o in gains]
    res["norm_mix_g"] = [o[1:2] for o in gains]
    res["norm_ffn2_g"] = [o[2:3] for o in gains]
    res["norm_final_g"] = [o[3] for o in gains]
    res["rel_bias"] = bias

    loss = lax.psum(loss_part[0, 0], MESH_AXES)
    names = ["norm_ffn1_g", "ffn1_w_gate", "ffn1_w_up", "ffn1_w_down", "norm_mix_g", "w_in", "rel_bias",
             "w_out_att", "w_out_ret", "w_out", "norm_ffn2_g", "ffn2_w_gate", "ffn2_w_up", "ffn2_w_down",
             "norm_final_g"]
    out = [loss, grad_x[None]]
    for k in range(4):
        out += [res[nm][k] for nm in names]
    return tuple(out)
```

```python
import functools
import math

import jax
import jax.numpy as jnp
from jax import lax
from jax.experimental import pallas as pl
from jax.experimental.pallas import tpu as pltpu

F32 = jnp.float32
CDT = jnp.bfloat16

N_DEV = 8
CHUNK = 64
N_PREV_CHUNKS = 8
BAND = N_PREV_CHUNKS * CHUNK
HEAD_DIM = 128
MAX_REL_DIST = 128
N_REL = 2 * MAX_REL_DIST + 1
N_REL_PAD = 384
ROPE_BASE = 10000.0
EPS = 1e-6
NEG = -1e30
LANE = 128
ATT_TQ = 256
ATT_HEADS_PER_STEP = 2
RET_BLK = 256
RET_HEADS_PER_STEP = 8
RET_BWD_HEADS_PER_STEP = 4
VMEM_LIMIT = 48 * 1024 * 1024

ADAM_LR = 0.001
ADAM_B1 = 0.9
ADAM_B2 = 0.999
ADAM_EPS = 1e-08
ADAM_WD = 0.01
ADAM_STEP = 10

MESH_AXES = ("x", "y", "c")
_NT = (((1,), (1,)), ((), ()))
_TN = (((0,), (0,)), ((), ()))


def _round_up(v, m):
    return (v + m - 1) // m * m


def _params(sem=None):
    return pltpu.CompilerParams(dimension_semantics=sem, vmem_limit_bytes=VMEM_LIMIT)


def _sigmoid(v):
    return 0.5 * (jnp.tanh(0.5 * v) + 1.0)


def _bspec(kind, tm, w, off, order, jmap=lambda j: j):
    def wrap(f):
        if order == "ji":
            return lambda j, i: f(i, jmap(j))
        return lambda i, j: f(i, jmap(j))
    if kind == "full":
        return pl.BlockSpec((tm, w), wrap(lambda i, j: (i, 0)))
    if kind == "col":
        return pl.BlockSpec((tm, w), wrap(lambda i, j: (i, j + off)))
    assert kind == "3d"
    return pl.BlockSpec((None, tm, w), wrap(lambda i, j: (j, i, 0)))


def _wspec(w, order, jmap=lambda j: j):
    if order == "ji":
        return pl.BlockSpec((None,) + w.shape[1:], lambda j, i: (jmap(j), 0, 0))
    return pl.BlockSpec((None,) + w.shape[1:], lambda i, j: (jmap(j), 0, 0))


def _width(arr, kind, w):
    return arr.shape[-1] if kind in ("full", "3d") else w


def mm_block(name, T, tm, lhs, wts, extras, outs, epilogue, order="ji", after=(), jgroup=1):
    nl, nw, ne = len(lhs), len(wts), len(extras)
    ni = T // tm
    wo = outs[0][1]

    def body(*refs):
        l = refs[:nl]
        w = refs[nl:nl + nw * jgroup]
        e = refs[nl + nw * jgroup:nl + nw * jgroup + ne]
        o = refs[nl + nw * jgroup + ne + len(after):]
        for u in range(jgroup):
            cols = slice(None) if jgroup == 1 else pl.ds(u * wo, wo)
            prods = []
            for k, (_, li, tr) in enumerate(wts):
                a = l[li][...]
                wk = w[k * jgroup + u][...]
                if tr:
                    prods.append(lax.dot_general(a, wk, _NT, preferred_element_type=F32))
                else:
                    prods.append(jnp.dot(a, wk, preferred_element_type=F32))
            res = epilogue(prods, [r[:, cols].astype(F32) for r in e])
            for r, val in zip(o, res):
                r[:, cols] = val.astype(r.dtype)

    def blocked(kind, w, off):
        if jgroup == 1:
            return _bspec(kind, tm, w, off, order)
        assert kind == "col" and off % jgroup == 0
        return _bspec(kind, tm, w * jgroup, off // jgroup, order)

    in_specs = [_bspec(k, tm, _width(a, k, w), off, order) for (a, k, w, off) in lhs]
    in_specs += [_wspec(w, order, lambda j, u=u: j * jgroup + u) for (w, _, _) in wts for u in range(jgroup)]
    in_specs += [blocked(k, _width(a, k, w), off) for (a, k, w, off) in extras]
    in_specs += [_ANY] * len(after)
    out_specs, out_shape = [], []
    for (kind, w, dt) in outs:
        out_specs.append(blocked(kind, w, 0))
        if kind == "3d":
            out_shape.append(jax.ShapeDtypeStruct((N_DEV, T, w), dt))
        else:
            out_shape.append(jax.ShapeDtypeStruct((T, N_DEV * w), dt))
    args = [a for (a, _, _, _) in lhs] + [w for (w, _, _) in wts for _ in range(jgroup)]
    args += [a for (a, _, _, _) in extras] + list(after)
    nj = N_DEV // jgroup
    return pl.pallas_call(
        body, name=name, grid=(nj, ni) if order == "ji" else (ni, nj), in_specs=in_specs,
        out_specs=out_specs, out_shape=out_shape, compiler_params=_params(("parallel", "parallel")))(*args)


def mm_reduce_j(name, T, tm, pairs, out_w, out_dtype, res=None, scale=1.0, after=(), jstep=1):
    terms = [(p, u) for u in range(jstep) for p in pairs]
    nt = len(terms)
    nj = N_DEV // jstep
    ni = T // tm

    def body(*refs):
        xs = refs[:nt]
        ws = refs[nt:2 * nt]
        rest = refs[2 * nt:len(refs) - 2 - len(after)] + refs[len(refs) - 2:]
        if res is not None:
            res_ref, o_ref, acc = rest
        else:
            o_ref, acc = rest
        j = pl.program_id(1)

        @pl.when(j == 0)
        def _():
            acc[...] = jnp.zeros_like(acc)

        tot = None
        for k, (p, _) in enumerate(terms):
            if p[5]:
                d = lax.dot_general(xs[k][...], ws[k][...], _NT, preferred_element_type=F32)
            else:
                d = jnp.dot(xs[k][...], ws[k][...], preferred_element_type=F32)
            tot = d if tot is None else tot + d
        acc[...] += tot

        @pl.when(j == nj - 1)
        def _():
            if res is not None:
                o_ref[...] = (res_ref[...] + scale * acc[...]).astype(o_ref.dtype)
            else:
                o_ref[...] = acc[...].astype(o_ref.dtype)

    def jmap(u):
        return lambda j: j * jstep + u

    in_specs = [_bspec(p[1], tm, _width(p[0], p[1], p[2]), p[3], "ij", jmap(u)) for (p, u) in terms]
    in_specs += [_wspec(p[4], "ij", jmap(u)) for (p, u) in terms]
    args = [p[0] for (p, _) in terms] + [p[4] for (p, _) in terms]
    if res is not None:
        in_specs.append(pl.BlockSpec((tm, out_w), lambda i, j: (i, 0)))
        args.append(res)
    in_specs += [_ANY] * len(after)
    args += list(after)
    return pl.pallas_call(
        body, name=name, grid=(ni, nj), in_specs=in_specs,
        out_specs=pl.BlockSpec((tm, out_w), lambda i, j: (i, 0)),
        out_shape=jax.ShapeDtypeStruct((T, out_w), out_dtype),
        scratch_shapes=[pltpu.VMEM((tm, out_w), F32)],
        compiler_params=_params(("parallel", "arbitrary")))(*args)


def mm_reduce_i(name, T, tm, a, b, after=()):
    ni = T // tm
    rows = _width(a[0], a[1], a[2])
    cols = _width(b[0], b[1], b[2])

    def body(a_ref, b_ref, *rest):
        o_ref, acc = rest[len(after):]
        i = pl.program_id(1)

        @pl.when(i == 0)
        def _():
            acc[...] = jnp.zeros_like(acc)

        acc[...] += lax.dot_general(a_ref[...], b_ref[...], _TN, preferred_element_type=F32)

        @pl.when(i == ni - 1)
        def _():
            o_ref[...] = acc[...].astype(o_ref.dtype)

    return pl.pallas_call(
        body, name=name, grid=(N_DEV, ni),
        in_specs=[_bspec(a[1], tm, rows, a[3], "ji"), _bspec(b[1], tm, cols, b[3], "ji")] + [_ANY] * len(after),
        out_specs=pl.BlockSpec((None, rows, cols), lambda j, i: (j, 0, 0)),
        out_shape=jax.ShapeDtypeStruct((N_DEV, rows, cols), CDT),
        scratch_shapes=[pltpu.VMEM((rows, cols), F32)],
        compiler_params=_params(("parallel", "arbitrary")))(a[0], b[0], *after)


def _rms_bwd_math(xv, g, dy):
    r = lax.rsqrt(jnp.mean(xv * xv, axis=-1, keepdims=True) + EPS)
    xn = xv * r
    dxn = dy * g
    dx = r * (dxn - xn * jnp.mean(dxn * xn, axis=-1, keepdims=True))
    dg = jnp.sum(dy * xn, axis=0, keepdims=True)
    return dx, dg


def rmsnorm_fwd(name, x, g, tm, after=()):
    T, D = x.shape

    def body(x_ref, g_ref, *rest):
        o_ref = rest[-1]
        xv = x_ref[...]
        r = lax.rsqrt(jnp.mean(xv * xv, axis=-1, keepdims=True) + EPS)
        o_ref[...] = (xv * r * g_ref[...]).astype(o_ref.dtype)

    return pl.pallas_call(
        body, name=name, grid=(T // tm,),
        in_specs=[pl.BlockSpec((tm, D), lambda i: (i, 0)), pl.BlockSpec((1, D), lambda i: (0, 0))]
        + [_ANY] * len(after),
        out_specs=pl.BlockSpec((tm, D), lambda i: (i, 0)),
        out_shape=jax.ShapeDtypeStruct((T, D), CDT),
        compiler_params=_params(("parallel",)))(x, g, *after)


def rmsnorm_bwd(name, x, g, dh, dres, cscale, tm, after=()):
    T, D = x.shape

    def body(x_ref, g_ref, dh_ref, dres_ref, *rest):
        dx_ref, dxc_ref, dg_ref = rest[len(after):]
        i = pl.program_id(0)
        dx, dg = _rms_bwd_math(x_ref[...], g_ref[...], dh_ref[...].astype(F32))
        dx = dres_ref[...] + dx
        dx_ref[...] = dx
        dxc_ref[...] = (cscale * dx).astype(dxc_ref.dtype)

        @pl.when(i == 0)
        def _():
            dg_ref[...] = jnp.zeros_like(dg_ref)

        dg_ref[...] += dg

    row = pl.BlockSpec((tm, D), lambda i: (i, 0))
    vec = pl.BlockSpec((1, D), lambda i: (0, 0))
    return pl.pallas_call(
        body, name=name, grid=(T // tm,), in_specs=[row, vec, row, row] + [_ANY] * len(after),
        out_specs=[row, row, vec],
        out_shape=[jax.ShapeDtypeStruct((T, D), F32), jax.ShapeDtypeStruct((T, D), CDT),
                   jax.ShapeDtypeStruct((1, D), F32)],
        compiler_params=_params(("arbitrary",)))(x, g, dh, dres, *after)


def loss_head(x, g, tgt, tm):
    T, D = x.shape

    def body(x_ref, g_ref, t_ref, dx_ref, dxc_ref, dg_ref, loss_ref):
        i = pl.program_id(0)
        xv = x_ref[...]
        gv = g_ref[...]
        r = lax.rsqrt(jnp.mean(xv * xv, axis=-1, keepdims=True) + EPS)
        err = xv * r * gv - t_ref[...]
        part = jnp.sum(jnp.mean(err * err, axis=-1, keepdims=True), axis=0, keepdims=True)
        dx, dg = _rms_bwd_math(xv, gv, err / D)
        dx_ref[...] = dx
        dxc_ref[...] = (0.5 * dx).astype(dxc_ref.dtype)

        @pl.when(i == 0)
        def _():
            dg_ref[...] = jnp.zeros_like(dg_ref)
            loss_ref[...] = jnp.zeros_like(loss_ref)

        dg_ref[...] += dg
        loss_ref[...] += jnp.broadcast_to(0.5 * part, loss_ref.shape)

    row = pl.BlockSpec((tm, D), lambda i: (i, 0))
    vec = pl.BlockSpec((1, D), lambda i: (0, 0))
    return pl.pallas_call(
        body, name="loss_head", grid=(T // tm,), in_specs=[row, vec, row],
        out_specs=[row, row, vec, pl.BlockSpec((1, LANE), lambda i: (0, 0))],
        out_shape=[jax.ShapeDtypeStruct((T, D), F32), jax.ShapeDtypeStruct((T, D), CDT),
                   jax.ShapeDtypeStruct((1, D), F32), jax.ShapeDtypeStruct((1, LANE), F32)],
        compiler_params=_params(("arbitrary",)))(x, g, tgt)


def _skew_rows(z, left):
    tq, kw = z.shape
    row = lax.broadcasted_iota(jnp.int32, (tq, kw), 0)
    s = 1
    while s < tq:
        z = jnp.where((row & s) != 0, pltpu.roll(z, kw - s if left else s, 1), z)
        s *= 2
    return z


REL_HI = BAND + MAX_REL_DIST
REL_LO = BAND - MAX_REL_DIST


def attn_bias(rel_bias, tq):
    H = rel_bias.shape[0]
    kw = BAND + tq
    by_skew = jnp.concatenate(
        [jnp.broadcast_to(rel_bias[:, N_REL - 1:], (H, REL_LO)), rel_bias[:, ::-1],
         jnp.broadcast_to(rel_bias[:, :1], (H, kw - REL_HI - 1))], axis=1).reshape(H, 1, kw)

    def body(t_ref, o_ref):
        t = t_ref[...]
        qi = lax.broadcasted_iota(jnp.int32, (tq, kw), 0)
        kj = lax.broadcasted_iota(jnp.int32, (tq, kw), 1)
        b = _skew_rows(jnp.broadcast_to(t, (tq, kw)), left=False)
        b = jnp.where(kj < qi, t[:, 0:1], b)
        qc = qi // CHUNK
        kc = kj // CHUNK - N_PREV_CHUNKS
        valid = (kc <= qc) & (kc >= qc - N_PREV_CHUNKS)
        o_ref[...] = jnp.where(valid, b, NEG)

    return pl.pallas_call(
        body, name="attn_bias", grid=(H,),
        in_specs=[pl.BlockSpec((None, 1, kw), lambda h: (h, 0, 0))],
        out_specs=pl.BlockSpec((None, tq, kw), lambda h: (h, 0, 0)),
        out_shape=jax.ShapeDtypeStruct((H, tq, kw), F32),
        compiler_params=_params(("parallel",)))(by_skew)


def attn_bias_grad(dst, tq):
    H = dst.shape[0]
    kw = BAND + tq

    def body(d_ref, o_ref):
        z = _skew_rows(d_ref[...], left=True)
        qi = lax.broadcasted_iota(jnp.int32, (tq, kw), 0)
        kj = lax.broadcasted_iota(jnp.int32, (tq, kw), 1)
        wrapped = kj + qi >= kw
        c = jnp.sum(jnp.where(wrapped, 0.0, z), axis=0, keepdims=True)
        cw = jnp.sum(jnp.sum(jnp.where(wrapped, z, 0.0), axis=0, keepdims=True), axis=1, keepdims=True)
        lane = lax.broadcasted_iota(jnp.int32, (1, kw), 1)
        ahead = jnp.sum(jnp.where(lane >= REL_HI, c, 0.0), axis=1, keepdims=True)
        behind = jnp.sum(jnp.where(lane <= REL_LO, c, 0.0), axis=1, keepdims=True) + cw
        o_ref[...] = jnp.where(lane == REL_HI, ahead, jnp.where(lane == REL_LO, behind, c))

    by_skew = pl.pallas_call(
        body, name="attn_bias_grad", grid=(H,),
        in_specs=[pl.BlockSpec((None, tq, kw), lambda h: (h, 0, 0))],
        out_specs=pl.BlockSpec((None, 1, kw), lambda h: (h, 0, 0)),
        out_shape=jax.ShapeDtypeStruct((H, 1, kw), F32),
        compiler_params=_params(("parallel",)))(dst)
    return by_skew[:, 0, REL_LO:REL_HI + 1][:, ::-1]


def _attn_scores(q, kpad, bm_ref, start, kw):
    k = kpad[pl.ds(start, kw), :]
    s = lax.dot_general(q, k, _NT, preferred_element_type=F32) * (HEAD_DIM ** -0.5) + bm_ref[...]
    col = lax.broadcasted_iota(jnp.int32, s.shape, 1)
    s = jnp.where(col < BAND - start, NEG, s)
    m = jnp.max(s, axis=-1, keepdims=True)
    e = jnp.exp(s - m)
    return e, 1.0 / jnp.sum(e, axis=-1, keepdims=True), k


def _fill_padded(pad_ref, src_ref, T):
    pad_ref[pl.ds(0, BAND), :] = jnp.zeros((BAND, HEAD_DIM), pad_ref.dtype)
    pad_ref[pl.ds(BAND, T), :] = src_ref[...]


def attn_fwd(proj, biasm, A, tq):
    T = proj.shape[0]
    H = A // HEAD_DIM
    kw = BAND + tq
    hp = ATT_HEADS_PER_STEP
    w = hp * HEAD_DIM

    def body(q_ref, k_ref, v_ref, bm_ref, o_ref, kpad, vpad):
        qi = pl.program_id(1)

        @pl.when(qi == 0)
        def _():
            for u in range(hp):
                cols = pl.ds(u * HEAD_DIM, HEAD_DIM)
                for pad, src in ((kpad, k_ref), (vpad, v_ref)):
                    pad[u, pl.ds(0, BAND), :] = jnp.zeros((BAND, HEAD_DIM), pad.dtype)
                    pad[u, pl.ds(BAND, T), :] = src[:, cols]

        start = pl.multiple_of(qi * tq, tq)
        for u in range(hp):
            cols = pl.ds(u * HEAD_DIM, HEAD_DIM)
            e, rinv, _ = _attn_scores(q_ref[:, cols], kpad.at[u], bm_ref.at[u], start, kw)
            v = vpad[u, pl.ds(start, kw), :]
            o_ref[:, cols] = (jnp.dot(e.astype(CDT), v, preferred_element_type=F32) * rinv).astype(o_ref.dtype)

    return pl.pallas_call(
        body, name="attn_fwd", grid=(H // hp, T // tq),
        in_specs=[pl.BlockSpec((tq, w), lambda h, i: (i, h)),
                  pl.BlockSpec((T, w), lambda h, i: (0, H // hp + h)),
                  pl.BlockSpec((T, w), lambda h, i: (0, 2 * H // hp + h)),
                  pl.BlockSpec((hp, tq, kw), lambda h, i: (h, 0, 0))],
        out_specs=pl.BlockSpec((tq, w), lambda h, i: (i, h)),
        out_shape=jax.ShapeDtypeStruct((T, A), CDT),
        scratch_shapes=[pltpu.VMEM((hp, BAND + T, HEAD_DIM), CDT), pltpu.VMEM((hp, BAND + T, HEAD_DIM), CDT)],
        compiler_params=_params(("parallel", "arbitrary")))(proj, proj, proj, biasm)


def attn_bwd(proj, biasm, datt, A, tq):
    T = proj.shape[0]
    H = A // HEAD_DIM
    kw = BAND + tq
    nq = T // tq
    scale = HEAD_DIM ** -0.5

    def body(q_ref, k_ref, v_ref, bm_ref, do_ref, dq_ref, dk_ref, dv_ref, dst_ref,
             kpad, vpad, dkacc, dvacc):
        qi = pl.program_id(1)

        @pl.when(qi == 0)
        def _():
            _fill_padded(kpad, k_ref, T)
            _fill_padded(vpad, v_ref, T)
            dkacc[...] = jnp.zeros_like(dkacc)
            dvacc[...] = jnp.zeros_like(dvacc)
            dst_ref[...] = jnp.zeros_like(dst_ref)

        start = pl.multiple_of(qi * tq, tq)
        q = q_ref[...]
        e, rinv, k = _attn_scores(q, kpad, bm_ref, start, kw)
        p = e * rinv
        v = vpad[pl.ds(start, kw), :]
        do = do_ref[...]
        dp = lax.dot_general(do, v, _NT, preferred_element_type=F32)
        ds = p * (dp - jnp.sum(dp * p, axis=-1, keepdims=True))
        dst_ref[...] += ds
        dsb = ds.astype(CDT)
        dq_ref[...] = (jnp.dot(dsb, k, preferred_element_type=F32) * scale).astype(dq_ref.dtype)
        dkacc[pl.ds(start, kw), :] += lax.dot_general(dsb, q, _TN, preferred_element_type=F32) * scale
        dvacc[pl.ds(start, kw), :] += lax.dot_general(p.astype(CDT), do, _TN, preferred_element_type=F32)

        @pl.when(qi == nq - 1)
        def _():
            dk_ref[...] = dkacc[pl.ds(BAND, T), :].astype(dk_ref.dtype)
            dv_ref[...] = dvacc[pl.ds(BAND, T), :].astype(dv_ref.dtype)

    blk = pl.BlockSpec((tq, HEAD_DIM), lambda h, i: (i, h))
    col = pl.BlockSpec((T, HEAD_DIM), lambda h, i: (0, h))
    bias = pl.BlockSpec((None, tq, kw), lambda h, i: (h, 0, 0))
    return pl.pallas_call(
        body, name="attn_bwd", grid=(H, nq),
        in_specs=[blk,
                  pl.BlockSpec((T, HEAD_DIM), lambda h, i: (0, H + h)),
                  pl.BlockSpec((T, HEAD_DIM), lambda h, i: (0, 2 * H + h)),
                  bias, blk],
        out_specs=[blk, col, col, bias],
        out_shape=[jax.ShapeDtypeStruct((T, A), CDT), jax.ShapeDtypeStruct((T, A), CDT),
                   jax.ShapeDtypeStruct((T, A), CDT), jax.ShapeDtypeStruct((H, tq, kw), F32)],
        scratch_shapes=[pltpu.VMEM((BAND + T, HEAD_DIM), CDT), pltpu.VMEM((BAND + T, HEAD_DIM), CDT),
                        pltpu.VMEM((BAND + T, HEAD_DIM), F32), pltpu.VMEM((BAND + T, HEAD_DIM), F32)],
        compiler_params=_params(("parallel", "arbitrary")))(proj, proj, proj, biasm, datt)


def _retention_tables(T, H, blk):
    half = HEAD_DIM // 2
    inv = 1.0 / (ROPE_BASE ** (jnp.arange(0, HEAD_DIM, 2, dtype=F32) / HEAD_DIM))
    ang = jnp.arange(T, dtype=F32)[:, None] * inv[None, :]
    cos, sin = jnp.cos(ang), jnp.sin(ang)
    rc = jnp.concatenate([cos, cos], axis=1)
    rs = jnp.concatenate([-sin, sin], axis=1)
    assert rc.shape == (T, 2 * half)
    log_g = jnp.log(1.0 - 2.0 ** (-5.0 - jnp.arange(H, dtype=F32)))[:, None, None]
    idx = jnp.arange(blk, dtype=F32)
    n, m = idx[:, None], idx[None, :]
    same = (n // CHUNK) == (m // CHUNK)
    earlier = (m // CHUNK) < (n // CHUNK)
    dist = jnp.where(same, jnp.abs(n - m), n - m)[None]
    dmat = jnp.where((same | earlier)[None], jnp.exp(log_g * dist), 0.0)
    ones = jnp.ones((1, 1, HEAD_DIM), F32)
    qd = jnp.exp(log_g * (idx[None, :, None] + 1.0)) * ones
    kd = jnp.exp(log_g * (blk - 1.0 - idx[None, :, None])) * ones
    cd = jnp.exp(log_g * blk) * jnp.ones((1, 8, HEAD_DIM), F32)
    return rc, rs, dmat, qd, kd, cd


def _rot(v, rc, rs):
    return v * rc + pltpu.roll(v, HEAD_DIM // 2, 1) * rs


def _rot_bwd(dv, rc, rs):
    return dv * rc + pltpu.roll(dv * rs, HEAD_DIM // 2, 1)


def ret_fwd(proj, tables, A, blk):
    T = proj.shape[0]
    H = A // HEAD_DIM
    nb = T // blk
    hp = RET_HEADS_PER_STEP
    rc, rs, dmat, qd, kd, cd = tables
    scale = HEAD_DIM ** -0.5

    def body(q_ref, k_ref, v_ref, g_ref, rc_ref, rs_ref, d_ref, qd_ref, kd_ref, cd_ref,
             y_ref, o_ref, st_ref, state):
        b = pl.program_id(1)

        @pl.when(b == 0)
        def _():
            state[...] = jnp.zeros_like(state)

        c, s = rc_ref[...], rs_ref[...]
        for u in range(hp):
            cols = pl.ds(u * HEAD_DIM, HEAD_DIM)
            qs = (_rot(q_ref[:, cols].astype(F32), c, s) * scale).astype(CDT)
            kr = _rot(k_ref[:, cols].astype(F32), c, s)
            v = v_ref[:, cols]
            sb = state[u].astype(CDT)
            a = lax.dot_general(qs, kr.astype(CDT), _NT, preferred_element_type=F32) * d_ref[u]
            o = jnp.dot(a.astype(CDT), v, preferred_element_type=F32)
            o = o + jnp.dot(qs, sb, preferred_element_type=F32) * qd_ref[u]
            st_ref[u] = sb
            state[u] = state[u] * cd_ref[u, 0:1, :] + lax.dot_general(
                (kr * kd_ref[u]).astype(CDT), v, _TN, preferred_element_type=F32)
            o_ref[:, cols] = o
            on = o * lax.rsqrt(jnp.mean(o * o, axis=-1, keepdims=True) + EPS)
            g = g_ref[:, cols].astype(F32)
            y_ref[:, cols] = (g * _sigmoid(g) * on).astype(y_ref.dtype)

    w = hp * HEAD_DIM

    def pj(off):
        return pl.BlockSpec((blk, w), lambda h, i: (i, off * H // hp + h))

    tok = pl.BlockSpec((blk, HEAD_DIM), lambda h, i: (i, 0))
    out = pl.BlockSpec((blk, w), lambda h, i: (i, h))

    def per_head(r, c):
        return pl.BlockSpec((hp, r, c), lambda h, i: (h, 0, 0))

    return pl.pallas_call(
        body, name="ret_fwd", grid=(H // hp, nb),
        in_specs=[pj(3), pj(4), pj(5), pj(6), tok, tok, per_head(blk, blk),
                  per_head(blk, HEAD_DIM), per_head(blk, HEAD_DIM), per_head(8, HEAD_DIM)],
        out_specs=[out, out, pl.BlockSpec((hp, None, HEAD_DIM, HEAD_DIM), lambda h, i: (h, i, 0, 0))],
        out_shape=[jax.ShapeDtypeStruct((T, A), CDT), jax.ShapeDtypeStruct((T, A), F32),
                   jax.ShapeDtypeStruct((H, nb, HEAD_DIM, HEAD_DIM), CDT)],
        scratch_shapes=[pltpu.VMEM((hp, HEAD_DIM, HEAD_DIM), F32)],
        compiler_params=_params(("parallel", "arbitrary")))(
            proj, proj, proj, proj, rc, rs, dmat, qd, kd, cd)


def ret_bwd(proj, tables, o_raw, states, dy, A, blk):
    T = proj.shape[0]
    H = A // HEAD_DIM
    nb = T // blk
    hp = RET_BWD_HEADS_PER_STEP
    rc, rs, dmat, qd, kd, cd = tables
    scale = HEAD_DIM ** -0.5

    def body(q_ref, k_ref, v_ref, g_ref, rc_ref, rs_ref, d_ref, qd_ref, kd_ref, cd_ref,
             o_ref, st_ref, dy_ref, dq_ref, dk_ref, dv_ref, dg_ref, dstate):
        b = pl.program_id(1)

        @pl.when(b == 0)
        def _():
            dstate[...] = jnp.zeros_like(dstate)

        c, s = rc_ref[...], rs_ref[...]
        for u in range(hp):
            cols = pl.ds(u * HEAD_DIM, HEAD_DIM)
            qs = (_rot(q_ref[:, cols].astype(F32), c, s) * scale).astype(CDT)
            kr = _rot(k_ref[:, cols].astype(F32), c, s)
            krb = kr.astype(CDT)
            kdb = (kr * kd_ref[u]).astype(CDT)
            v = v_ref[:, cols]
            dmat_v = d_ref[u]
            a = lax.dot_general(qs, krb, _NT, preferred_element_type=F32) * dmat_v

            o = o_ref[:, cols]
            r = lax.rsqrt(jnp.mean(o * o, axis=-1, keepdims=True) + EPS)
            on = o * r
            g = g_ref[:, cols].astype(F32)
            sg = _sigmoid(g)
            dyv = dy_ref[:, cols].astype(F32)
            dg_ref[:, cols] = (dyv * on * (sg * (1.0 + g * (1.0 - sg)))).astype(dg_ref.dtype)
            don = dyv * (g * sg)
            do = r * (don - on * jnp.mean(don * on, axis=-1, keepdims=True))
            dob = do.astype(CDT)
            doq = (do * qd_ref[u]).astype(CDT)
            dsb = dstate[u].astype(CDT)

            dv = lax.dot_general(a.astype(CDT), dob, _TN, preferred_element_type=F32)
            dv = dv + jnp.dot(kdb, dsb, preferred_element_type=F32)
            dv_ref[:, cols] = dv.astype(dv_ref.dtype)
            dpb = (lax.dot_general(dob, v, _NT, preferred_element_type=F32) * dmat_v).astype(CDT)
            dqs = jnp.dot(dpb, krb, preferred_element_type=F32)
            dqs = dqs + lax.dot_general(doq, st_ref[u], _NT, preferred_element_type=F32)
            dkr = lax.dot_general(dpb, qs, _TN, preferred_element_type=F32)
            dkr = dkr + lax.dot_general(v, dsb, _NT, preferred_element_type=F32) * kd_ref[u]
            dstate[u] = dstate[u] * cd_ref[u, 0:1, :] + lax.dot_general(
                qs, doq, _TN, preferred_element_type=F32)
            dq_ref[:, cols] = _rot_bwd(dqs * scale, c, s).astype(dq_ref.dtype)
            dk_ref[:, cols] = _rot_bwd(dkr, c, s).astype(dk_ref.dtype)

    w = hp * HEAD_DIM

    def pj(off):
        return pl.BlockSpec((blk, w), lambda h, i: (nb - 1 - i, off * H // hp + h))

    tok = pl.BlockSpec((blk, HEAD_DIM), lambda h, i: (nb - 1 - i, 0))
    out = pl.BlockSpec((blk, w), lambda h, i: (nb - 1 - i, h))

    def per_head(r, c):
        return pl.BlockSpec((hp, r, c), lambda h, i: (h, 0, 0))

    shp = jax.ShapeDtypeStruct((T, A), CDT)
    return pl.pallas_call(
        body, name="ret_bwd", grid=(H // hp, nb),
        in_specs=[pj(3), pj(4), pj(5), pj(6), tok, tok, per_head(blk, blk),
                  per_head(blk, HEAD_DIM), per_head(blk, HEAD_DIM), per_head(8, HEAD_DIM),
                  out, pl.BlockSpec((hp, None, HEAD_DIM, HEAD_DIM), lambda h, i: (h, nb - 1 - i, 0, 0)),
                  out],
        out_specs=[out, out, out, out], out_shape=[shp, shp, shp, shp],
        scratch_shapes=[pltpu.VMEM((hp, HEAD_DIM, HEAD_DIM), F32)],
        compiler_params=_params(("parallel", "arbitrary")))(
            proj, proj, proj, proj, rc, rs, dmat, qd, kd, cd, o_raw, states, dy)


def _mesh_pos():
    return lax.axis_index("x"), lax.axis_index("y"), lax.axis_index("c")


def _flat(pos):
    return 4 * pos[0] + 2 * pos[1] + pos[2]


_HBM = pl.BlockSpec(memory_space=pltpu.HBM)


def cast_shard(name, w, rows_p, cols_p, me_arr, after=()):
    r, c = w.shape
    tr = _row_tile(math.gcd(r, rows_p), 256)
    nr = r // tr

    def body(me_ref, w_ref, *rest):
        o_ref = rest[-1]
        i = pl.program_id(0)
        o_ref[...] = jnp.zeros_like(o_ref)

        @pl.when(i < nr)
        def _():
            o_ref[:, 0:c] = w_ref[...].astype(o_ref.dtype)

    return pl.pallas_call(
        body, name=name,
        grid_spec=pltpu.PrefetchScalarGridSpec(
            num_scalar_prefetch=1, grid=(rows_p // tr,),
            in_specs=[pl.BlockSpec((tr, c), lambda i, me: (jnp.minimum(i, nr - 1), 0))] + [_ANY] * len(after),
            out_specs=pl.BlockSpec((None, tr, cols_p), lambda i, me: (me[0], i, 0))),
        out_shape=jax.ShapeDtypeStruct((N_DEV, rows_p, cols_p), CDT),
        compiler_params=_params(("arbitrary",)))(me_arr, w, *after)


def exchange_partials(arrays, after):
    n, na = len(arrays), len(after)

    def body(*refs):
        ins, outs = refs[:n], refs[n + na:2 * n + na]
        send_sems, recv_sems, local_sems = refs[2 * n + na:]
        me = _mesh_pos()
        copies, locals_ = [], []
        for t in range(n):
            cp = pltpu.make_async_copy(ins[t], outs[t].at[_flat(me)], local_sems.at[t])
            cp.start()
            locals_.append(cp)
            for k in range(1, N_DEV):
                peer = _peer(me, k)
                send = pltpu.make_async_remote_copy(
                    src_ref=ins[t], dst_ref=outs[t].at[_flat(me)],
                    send_sem=send_sems.at[t, k - 1], recv_sem=recv_sems.at[t, k - 1],
                    device_id=peer, device_id_type=pl.DeviceIdType.MESH)
                send.start()
                recv = pltpu.make_async_remote_copy(
                    src_ref=ins[t], dst_ref=outs[t].at[_flat(peer)],
                    send_sem=send_sems.at[t, k - 1], recv_sem=recv_sems.at[t, k - 1],
                    device_id=peer, device_id_type=pl.DeviceIdType.MESH)
                copies.append((send, recv))
        for send, recv in copies:
            recv.wait_recv()
        for send, recv in copies:
            send.wait_send()
        for cp in locals_:
            cp.wait()

    return pl.pallas_call(
        body, name="exchange_partials",
        in_specs=[_HBM] * n + [_ANY] * na, out_specs=[_HBM] * n,
        out_shape=[jax.ShapeDtypeStruct((N_DEV,) + a.shape, a.dtype) for a in arrays],
        scratch_shapes=[pltpu.SemaphoreType.DMA((n, 7)), pltpu.SemaphoreType.DMA((n, 7)),
                        pltpu.SemaphoreType.DMA((n,))],
        )(*arrays, *after)


_SEM = pl.BlockSpec(memory_space=pltpu.SEMAPHORE)
_ANY = pl.BlockSpec(memory_space=pl.ANY)
_EFFECT = pltpu.SideEffectType.DATAFLOW_SIDE_EFFECTING


def _peer(me, k):
    return tuple(1 - v if bit else v for v, bit in zip(me, (k >> 2, (k >> 1) & 1, k & 1)))


_CHIP_MASKS = (2, 4, 6)

_EXCHANGE_MODES = {"gather": (7, None), "gather_chips": (4, None), "forward": (3, None),
                   "scatter": (7, 7), "scatter_pair": (4, 4), "scatter_chips": (3, 3)}


def _plan(mode, bufs, n, me):
    per = _EXCHANGE_MODES[mode][0]
    sib = _peer(me, 1)
    plan = []
    for t in range(n):
        src_arr, land_arr = bufs[t], bufs[n + t] if _EXCHANGE_MODES[mode][1] else None
        if mode in ("gather", "gather_chips"):
            masks = range(1, N_DEV) if mode == "gather" else (1,) + _CHIP_MASKS
            rows = [(src_arr.at[_flat(me)], src_arr.at[_flat(me)], _peer(me, k), src_arr.at[_flat(_peer(me, k))])
                    for k in masks]
        elif mode == "forward":
            rows = [(src_arr.at[_flat(_peer(me, k))], src_arr.at[_flat(_peer(me, k))], sib,
                     src_arr.at[_flat(_peer(sib, k))]) for k in _CHIP_MASKS]
        elif mode == "scatter":
            rows = [(src_arr.at[_flat(_peer(me, k))], land_arr.at[k - 1], _peer(me, k), land_arr.at[k - 1])
                    for k in range(1, N_DEV)]
        elif mode == "scatter_pair":
            rows = [(src_arr.at[_flat(_peer(me, q + 1))], land_arr.at[qi], sib, land_arr.at[qi])
                    for qi, q in enumerate((0,) + _CHIP_MASKS)]
        else:
            assert mode == "scatter_chips"
            rows = [(src_arr.at[qi + 1], land_arr.at[qi], _peer(me, q), land_arr.at[qi])
                    for qi, q in enumerate(_CHIP_MASKS)]
        assert len(rows) == per
        plan += [(t * per + s,) + row for s, row in enumerate(rows)]
    return plan


def exchange_start(name, arrays, mode, after):
    n, na = len(arrays), len(after)
    per, slots = _EXCHANGE_MODES[mode]
    bufs = list(arrays)
    if slots:
        bufs += [lax.empty((slots,) + a.shape[1:], a.dtype) for a in arrays]
    nb = len(bufs)

    def body(*refs):
        send_sems, recv_sems = refs[nb + na], refs[nb + na + 1]
        token = refs[-1]
        for s, src, dst, dev, _ in _plan(mode, refs[:nb], n, _mesh_pos()):
            pltpu.make_async_remote_copy(
                src_ref=src, dst_ref=dst, send_sem=send_sems.at[s], recv_sem=recv_sems.at[s],
                device_id=dev, device_id_type=pl.DeviceIdType.MESH).start()
        token[...] = jnp.zeros_like(token)

    out_shape = [pltpu.SemaphoreType.DMA((n * per,)), pltpu.SemaphoreType.DMA((n * per,))]
    out_shape += [pltpu.HBM(a.shape, a.dtype) for a in bufs]
    out_shape.append(jax.ShapeDtypeStruct((8, LANE), F32))
    args = [pltpu.with_memory_space_constraint(a, pltpu.HBM) for a in bufs] + list(after)
    outs = pl.pallas_call(
        body, name=name, out_shape=out_shape,
        in_specs=[_HBM] * nb + [_ANY] * na,
        out_specs=[_SEM, _SEM] + [_HBM] * nb + [pl.BlockSpec(memory_space=pltpu.VMEM)],
        input_output_aliases={i: 2 + i for i in range(nb)},
        compiler_params=pltpu.CompilerParams(has_side_effects=_EFFECT))(*args)
    return outs[0], outs[1], list(outs[2:2 + nb]), outs[-1]


def exchange_wait(name, started, mode, after):
    send_sems, recv_sems, bufs, _ = started
    nb, na = len(bufs), len(after)
    n = nb // 2 if _EXCHANGE_MODES[mode][1] else nb

    def body(*refs):
        send_sems_ref, recv_sems_ref = refs[nb], refs[nb + 1]
        for s, src, _, dev, land in _plan(mode, refs[:nb], n, _mesh_pos()):
            cp = pltpu.make_async_remote_copy(
                src_ref=src, dst_ref=land, send_sem=send_sems_ref.at[s], recv_sem=recv_sems_ref.at[s],
                device_id=dev, device_id_type=pl.DeviceIdType.MESH)
            cp.wait_send()
            cp.wait_recv()

    outs = pl.pallas_call(
        body, name=name, out_shape=[pltpu.HBM(a.shape, a.dtype) for a in bufs],
        in_specs=[_HBM] * nb + [_SEM, _SEM] + [_ANY] * na, out_specs=[_HBM] * nb,
        input_output_aliases={i: i for i in range(nb)},
        compiler_params=pltpu.CompilerParams(has_side_effects=_EFFECT))(
            *bufs, send_sems, recv_sems, *after)
    return list(outs)


def pair_sum(name, own, landed, blocks):
    _, r, c = own.shape
    tr = _row_tile(r, 256)

    def body(idx_ref, o_ref, l_ref, s_ref):
        s_ref[...] = (o_ref[...].astype(F32) + l_ref[...].astype(F32)).astype(s_ref.dtype)

    blk = pl.BlockSpec((None, tr, c), lambda q, i, idx: (q, i, 0))
    return pl.pallas_call(
        body, name=name,
        grid_spec=pltpu.PrefetchScalarGridSpec(
            num_scalar_prefetch=1, grid=(4, r // tr),
            in_specs=[pl.BlockSpec((None, tr, c), lambda q, i, idx: (idx[q], i, 0)), blk],
            out_specs=blk),
        out_shape=jax.ShapeDtypeStruct((4, r, c), own.dtype),
        compiler_params=_params(("parallel", "parallel")))(blocks, own, landed)


def _adamw_math(w, g, m, v):
    m = ADAM_B1 * m + (1.0 - ADAM_B1) * g
    v = ADAM_B2 * v + (1.0 - ADAM_B2) * (g * g)
    m_hat = m / (1.0 - ADAM_B1 ** ADAM_STEP)
    v_hat = v / (1.0 - ADAM_B2 ** ADAM_STEP)
    delta = -ADAM_LR * (m_hat / (jnp.sqrt(v_hat) + ADAM_EPS) + ADAM_WD * w)
    return delta, m, v


def reduce_adamw(name, land, w, m, v, tr, own=None, me_arr=None):
    R, C = w.shape
    S, _, Cp = land.shape

    def body(*refs):
        if own is not None:
            _, own_ref, l_ref, w_ref, m_ref, v_ref, g_ref, d_ref, nm_ref, nv_ref = refs
            g = own_ref[:, 0:C].astype(F32)
            first = 0
        else:
            l_ref, w_ref, m_ref, v_ref, g_ref, d_ref, nm_ref, nv_ref = refs
            g = l_ref[0, :, 0:C].astype(F32)
            first = 1
        for s in range(first, S):
            g = g + l_ref[s, :, 0:C].astype(F32)
        delta, nm, nv = _adamw_math(w_ref[...], g, m_ref[...], v_ref[...])
        g_ref[...] = g
        d_ref[...] = delta
        nm_ref[...] = nm
        nv_ref[...] = nv

    shp = jax.ShapeDtypeStruct((R, C), F32)
    if own is None:
        blk = pl.BlockSpec((tr, C), lambda i: (i, 0))
        return pl.pallas_call(
            body, name=name, grid=(R // tr,),
            in_specs=[pl.BlockSpec((S, tr, Cp), lambda i: (0, i, 0)), blk, blk, blk],
            out_specs=[blk, blk, blk, blk], out_shape=[shp, shp, shp, shp],
            compiler_params=_params(("parallel",)))(land, w, m, v)
    blk = pl.BlockSpec((tr, C), lambda i, me: (i, 0))
    return pl.pallas_call(
        body, name=name,
        grid_spec=pltpu.PrefetchScalarGridSpec(
            num_scalar_prefetch=1, grid=(R // tr,),
            in_specs=[pl.BlockSpec((None, tr, Cp), lambda i, me: (me[0], i, 0)),
                      pl.BlockSpec((S, tr, Cp), lambda i, me: (0, i, 0)), blk, blk, blk],
            out_specs=[blk, blk, blk, blk]),
        out_shape=[shp, shp, shp, shp],
        compiler_params=_params(("parallel",)))(me_arr, own, land, w, m, v)


def _row_tile(r, cap):
    t = min(r, cap)
    while r % t or t % 8:
        t -= 8
    return t


def kernel(x, norm_ffn1_g, ffn1_w_gate, ffn1_w_up, ffn1_w_down, norm_mix_g, w_in, rel_bias, w_out_att, w_out_ret, w_out, norm_ffn2_g, ffn2_w_gate, ffn2_w_up, ffn2_w_down, norm_final_g, loss_target, m_norm_ffn1_g, m_ffn1_w_gate, m_ffn1_w_up, m_ffn1_w_down, m_norm_mix_g, m_w_in, m_rel_bias, m_w_out_att, m_w_out_ret, m_w_out, m_norm_ffn2_g, m_ffn2_w_gate, m_ffn2_w_up, m_ffn2_w_down, m_norm_final_g, v_norm_ffn1_g, v_ffn1_w_gate, v_ffn1_w_up, v_ffn1_w_down, v_norm_mix_g, v_w_in, v_rel_bias, v_w_out_att, v_w_out_ret, v_w_out, v_norm_ffn2_g, v_ffn2_w_gate, v_ffn2_w_up, v_ffn2_w_down, v_norm_final_g):
    T, D = x.shape[1], x.shape[2]
    A = w_out_att.shape[1]
    H = A // HEAD_DIM
    nf = ffn1_w_gate.shape[2]
    nfp = _round_up(nf, LANE)
    nin = w_in.shape[2]
    nd = w_out.shape[1]
    assert nin % LANE == 0 and nd % LANE == 0 and (7 * A) % nd == 0 and T % ATT_TQ == 0
    tm = min(512, T)
    tb = min(1024, T)
    tw = min(2048, T)
    tw_in = min(1024, T)
    tn = min(256, T)
    x0 = x[0]
    tgt = loss_target[0]

    me_arr = (4 * lax.axis_index("x") + 2 * lax.axis_index("y") + lax.axis_index("c")).astype(jnp.int32).reshape(1)

    def slot(tag, w, after, rows_p=None):
        return cast_shard("cast_" + tag, w[0], rows_p or w.shape[1], w.shape[2], me_arr, after)

    def slot_t(tag, w, after):
        return cast_shard("cast_" + tag, jnp.transpose(w[0]), nfp, w.shape[1], me_arr, after)

    groups = [("wg1", [("wg1", ffn1_w_gate, slot_t)], True), ("wu1", [("wu1", ffn1_w_up, slot_t)], True),
              ("wd1", [("wd1", ffn1_w_down, functools.partial(slot, rows_p=nfp))], True),
              ("win", [("win", w_in, slot)], True),
              ("wout", [("woa", w_out_att, slot), ("wor", w_out_ret, slot), ("wo", w_out, slot)], False),
              ("wgu2", [("wg2", ffn2_w_gate, slot_t), ("wu2", ffn2_w_up, slot_t)], True),
              ("wd2", [("wd2", ffn2_w_down, functools.partial(slot, rows_p=nfp))], True)]
    ag_started = {}
    order = []
    for tag, members, two_level in groups:
        mode = "gather_chips" if two_level else "gather"
        started = exchange_start("ag_start_" + tag, [make(nm, w, order) for nm, w, make in members], mode, order)
        ag_started[tag] = (started, mode)
        order = [started[3]]

    passing = {}

    def begin_pass(tag, after):
        started, mode = ag_started[tag]
        got = exchange_wait("ag_wait_" + tag, started, mode, list(after) if isinstance(after, tuple) else [after])
        passing[tag] = exchange_start("ag_pass_" + tag, got, "forward", [])
        return passing[tag][3]

    def gathered(tag, after):
        started, mode = ag_started[tag]
        if mode == "gather":
            return exchange_wait("ag_wait_" + tag, started, mode, [after])
        if tag not in passing:
            begin_pass(tag, after)
        return exchange_wait("ag_passed_" + tag, passing[tag], "forward",
                             list(after) if isinstance(after, tuple) else [after])

    xi, yi, ci = lax.axis_index("x"), lax.axis_index("y"), lax.axis_index("c")
    my_side = jnp.stack([4 * (1 - xi if q & 4 else xi) + 2 * (1 - yi if q & 2 else yi) + ci
                         for q in (0,) + _CHIP_MASKS]).astype(jnp.int32)
    first_block = jnp.zeros((1,), jnp.int32)

    def swiglu(prods, _):
        a, b = prods
        return a, b, a * _sigmoid(a) * b

    def ffn_fwd(tag, xin, g, get_wgu, get_wd, after=()):
        h = rmsnorm_fwd(tag + "_norm", xin, g, tm, after)
        if isinstance(get_wgu, tuple):
            Wg, = get_wgu[0](h)
            a, = mm_block(tag + "_gate", T, tb, [(h, "full", D, 0)], [(Wg, 0, True)], [], [("3d", nfp, CDT)],
                          lambda p, _: p)
            Wu, = get_wgu[1](a)
            b, mid = mm_block(tag + "_up", T, tb, [(h, "full", D, 0)], [(Wu, 0, True)], [(a, "3d", nfp, 0)],
                              [("3d", nfp, CDT)] * 2, lambda p, ex: (p[0], ex[0] * _sigmoid(ex[0]) * p[0]))
        else:
            Wg, Wu = get_wgu(h)
            a, b, mid = mm_block(tag + "_up", T, tb, [(h, "full", D, 0)], [(Wg, 0, True), (Wu, 0, True)],
                                 [], [("3d", nfp, CDT)] * 3, swiglu)
        Wd, = get_wd(mid)
        xo = mm_reduce_j(tag + "_down", T, tm, [(mid, "3d", nfp, 0, Wd, False)], D, F32, res=xin, scale=0.5,
                         jstep=2)
        return h, a, b, mid, xo, (Wg, Wu, Wd)

    biasm = attn_bias(rel_bias[0], ATT_TQ)
    tables = _retention_tables(T, H, RET_BLK)
    h1, a1, b1, mid1, x1, (Wg1, Wu1, Wd1) = ffn_fwd(
        "ffn1", x0, norm_ffn1_g, (lambda h: gathered("wg1", (h, biasm) + tuple(tables)), lambda a: gathered("wu1", a)),
        lambda mid: gathered("wd1", mid), after=order)
    h2 = rmsnorm_fwd("mix_norm", x1, norm_mix_g, tm)
    Win, = gathered("win", h2)
    proj, = mm_block("in_proj", T, tb, [(h2, "full", D, 0)], [(Win, 0, False)], [], [("col", nin, CDT)],
                     lambda p, _: p)
    att = attn_fwd(proj, biasm, A, ATT_TQ)
    retg, ret_raw, states = ret_fwd(proj, tables, A, RET_BLK)
    Woa, Wor, Wo = gathered("wout", retg)
    goff = 7 * A // nd

    def merge(prods, ex):
        ba, br = prods
        ga, gr = ex
        return ba, br, _sigmoid(ga) * ba + _sigmoid(gr) * br

    ba, br, merged = mm_block(
        "branches", T, tm, [(att, "full", A, 0), (retg, "full", A, 0)], [(Woa, 0, False), (Wor, 1, False)],
        [(proj, "col", nd, goff), (proj, "col", nd, goff + N_DEV)], [("col", nd, CDT)] * 3, merge, order="ij", jgroup=4)
    x2 = mm_reduce_j("out_proj", T, tm, [(merged, "col", nd, 0, Wo, False)], D, F32, res=x1, scale=1.0,
                     jstep=N_DEV, after=[begin_pass("wgu2", merged)])
    h3, a2, b2, mid2, x3, (Wg2, Wu2, Wd2) = ffn_fwd(
        "ffn2", x2, norm_ffn2_g, lambda h: gathered("wgu2", h), lambda mid: gathered("wd2", mid))

    dx3, dx3h, dgf, loss_part = loss_head(x3, norm_final_g.reshape(1, D), tgt, tn)

    def swiglu_bwd(prods, ex):
        dm, = prods
        a, b = ex
        sg = _sigmoid(a)
        return dm * b * (sg * (1.0 + a * (1.0 - sg))), dm * (a * sg)

    def ffn_bwd(tag, dxh, h, a, b, mid, Wg, Wu, Wd, two_level=False):
        da, db = mm_block(tag + "_down_bwd", T, tb, [(dxh, "full", D, 0)], [(Wd, 0, True)],
                          [(a, "3d", nfp, 0), (b, "3d", nfp, 0)], [("3d", nfp, CDT)] * 2, swiglu_bwd)

        def up_bwd(after):
            return mm_reduce_j(tag + "_up_bwd", T, tm,
                               [(da, "3d", nfp, 0, Wg, False), (db, "3d", nfp, 0, Wu, False)],
                               D, CDT, after=after, jstep=2)

        if not two_level:
            dWd = mm_reduce_i(tag + "_dwd", T, tw, (mid, "3d", nfp, 0), (dxh, "full", D, 0))
            dWg = mm_reduce_i(tag + "_dwg", T, tw, (da, "3d", nfp, 0), (h, "full", D, 0))
            dWu = mm_reduce_i(tag + "_dwu", T, tw, (db, "3d", nfp, 0), (h, "full", D, 0))
            sent = exchange_start("rs_start_" + tag, [dWg, dWu, dWd], "scatter", [])
            return up_bwd([sent[3]]), sent

        def swap(nm, grad, after):
            return exchange_start("rs_pair_start_%s_%s" % (tag, nm), [grad], "scatter_pair", after)

        def to_chips(nm, swapping, after):
            own, landed = exchange_wait("rs_pair_wait_%s_%s" % (tag, nm), swapping, "scatter_pair", [after])
            sums = pair_sum("%s_pair_sum_%s" % (tag, nm), own, landed, my_side)
            return exchange_start("rs_chips_start_%s_%s" % (tag, nm), [sums], "scatter_chips", [])

        dWd = mm_reduce_i(tag + "_dwd", T, tw, (mid, "3d", nfp, 0), (dxh, "full", D, 0))
        swap_d = swap("d", dWd, [])
        dWg = mm_reduce_i(tag + "_dwg", T, tw, (da, "3d", nfp, 0), (h, "full", D, 0), after=[swap_d[3]])
        sent_d = to_chips("d", swap_d, dWg)
        swap_g = swap("g", dWg, [sent_d[3]])
        dWu = mm_reduce_i(tag + "_dwu", T, tw, (db, "3d", nfp, 0), (h, "full", D, 0), after=[swap_g[3]])
        sent_g = to_chips("g", swap_g, dWu)
        swap_u = swap("u", dWu, [sent_g[3]])
        dh = up_bwd([swap_u[3]])
        sent_u = to_chips("u", swap_u, dh)
        return dh, [sent_g, sent_u, sent_d]

    dh3, sent_ffn2 = ffn_bwd("ffn2", dx3h, h3, a2, b2, mid2, Wg2, Wu2, Wd2)
    dx2, dx2c, dg2 = rmsnorm_bwd("ffn2_norm_bwd", x2, norm_ffn2_g, dh3, dx3, 1.0, tn)

    def merge_bwd(prods, ex):
        dmg, = prods
        ba_, br_, ga, gr = ex
        sa, sr = _sigmoid(ga), _sigmoid(gr)
        return dmg * sa, dmg * sr, dmg * ba_ * sa * (1.0 - sa), dmg * br_ * sr * (1.0 - sr)

    dba, dbr, dga, dgr = mm_block(
        "out_proj_bwd", T, tm, [(dx2c, "full", D, 0)], [(Wo, 0, True)],
        [(ba, "col", nd, 0), (br, "col", nd, 0), (proj, "col", nd, goff), (proj, "col", nd, goff + N_DEV)],
        [("col", nd, CDT)] * 4, merge_bwd, order="ij", jgroup=4)
    dWo = mm_reduce_i("dwo", T, tw, (merged, "col", nd, 0), (dx2c, "full", D, 0))
    dWoa = mm_reduce_i("dwoa", T, tw, (att, "full", A, 0), (dba, "col", nd, 0))
    dWor = mm_reduce_i("dwor", T, tw, (retg, "full", A, 0), (dbr, "col", nd, 0))
    sent_mix = exchange_start("rs_start_mix", [dWoa, dWor, dWo], "scatter", [])
    datt = mm_reduce_j("att_out_bwd", T, tm, [(dba, "col", nd, 0, Woa, True)], A, CDT, after=[sent_mix[3]],
                       jstep=N_DEV)
    dretg = mm_reduce_j("ret_out_bwd", T, tm, [(dbr, "col", nd, 0, Wor, True)], A, CDT, jstep=N_DEV)
    dq_r, dk_r, dv_r, dg_r = ret_bwd(proj, tables, ret_raw, states, dretg, A, RET_BLK)
    dq_a, dk_a, dv_a, dst = attn_bwd(proj, biasm, datt, A, ATT_TQ)
    dbias = jnp.pad(attn_bias_grad(dst, ATT_TQ), ((0, 0), (0, N_REL_PAD - N_REL)))
    dproj = jnp.concatenate([dq_a, dk_a, dv_a, dq_r, dk_r, dv_r, dg_r, dga, dgr], axis=1)
    dWin = mm_reduce_i("dwin", T, tw_in, (h2, "full", D, 0), (dproj, "col", nin, 0))
    sent_win = exchange_start("rs_start_win", [dWin], "scatter", [])
    dh2 = mm_reduce_j("in_proj_bwd", T, tm, [(dproj, "col", nin, 0, Win, True)], D, CDT, after=[sent_win[3]],
                      jstep=2)
    dx1, dx1h, dgm = rmsnorm_bwd("mix_norm_bwd", x1, norm_mix_g, dh2, dx2, 0.5, tn)
    dh1, sent_ffn1 = ffn_bwd("ffn1", dx1h, h1, a1, b1, mid1, Wg1, Wu1, Wd1, two_level=True)
    grad_x, _, dg1 = rmsnorm_bwd("ffn1_norm_bwd", x0, norm_ffn1_g, dh1, dx1, 1.0, tn, after=[sent_ffn1[1][3]])

    dgains = jnp.concatenate([dg1, dgm, dg2, dgf, jnp.zeros((4, D), F32)], axis=0)

    def upd(name, own, land, w, m, v, own_block=me_arr, transposed=False):
        w2, m2, v2 = [jnp.transpose(t[0]) if transposed else t[0] for t in (w, m, v)]
        outs = reduce_adamw(name, land, w2, m2, v2, _row_tile(w2.shape[0], 256), own=own, me_arr=own_block)
        return [jnp.transpose(o)[None] if transposed else o[None] for o in outs]

    res = {}
    oWg2, oWu2, oWd2, lWg2, lWu2, lWd2 = exchange_wait("rs_wait_ffn2", sent_ffn2, "scatter", [grad_x])
    res["ffn2_w_gate"] = upd("adamw_wg2", oWg2, lWg2, ffn2_w_gate, m_ffn2_w_gate, v_ffn2_w_gate, transposed=True)
    res["ffn2_w_up"] = upd("adamw_wu2", oWu2, lWu2, ffn2_w_up, m_ffn2_w_up, v_ffn2_w_up, transposed=True)
    res["ffn2_w_down"] = upd("adamw_wd2", oWd2, lWd2, ffn2_w_down, m_ffn2_w_down, v_ffn2_w_down)
    oWoa, oWor, oWo, lWoa, lWor, lWo = exchange_wait("rs_wait_mix", sent_mix, "scatter", [res["ffn2_w_down"][1]])
    res["w_out_att"] = upd("adamw_woa", oWoa, lWoa, w_out_att, m_w_out_att, v_w_out_att)
    res["w_out_ret"] = upd("adamw_wor", oWor, lWor, w_out_ret, m_w_out_ret, v_w_out_ret)
    res["w_out"] = upd("adamw_wo", oWo, lWo, w_out, m_w_out, v_w_out)
    oWin, lWin = exchange_wait("rs_wait_win", sent_win, "scatter", [res["w_out"][1]])
    res["w_in"] = upd("adamw_win", oWin, lWin, w_in, m_w_in, v_w_in)
    lgains, lbias = exchange_partials([dgains, dbias], [res["w_in"][1]])
    (oWg1, lWg1), (oWu1, lWu1), (oWd1, lWd1) = [
        exchange_wait("rs_wait_ffn1_" + nm, started, "scatter_chips", [lgains])
        for nm, started in zip("gud", sent_ffn1)]
    res["ffn1_w_gate"] = upd("adamw_wg1", oWg1, lWg1, ffn1_w_gate, m_ffn1_w_gate, v_ffn1_w_gate, first_block, transposed=True)
    res["ffn1_w_up"] = upd("adamw_wu1", oWu1, lWu1, ffn1_w_up, m_ffn1_w_up, v_ffn1_w_up, first_block, transposed=True)
    res["ffn1_w_down"] = upd("adamw_wd1", oWd1, lWd1, ffn1_w_down, m_ffn1_w_down, v_ffn1_w_down, first_block)

    def stack_gains(a, b, c_, d):
        return jnp.concatenate([a, b, c_, d.reshape(1, D), jnp.zeros((4, D), F32)], axis=0)

    gw = stack_gains(norm_ffn1_g, norm_mix_g, norm_ffn2_g, norm_final_g)
    gm = stack_gains(m_norm_ffn1_g, m_norm_mix_g, m_norm_ffn2_g, m_norm_final_g)
    gv = stack_gains(v_norm_ffn1_g, v_norm_mix_g, v_norm_ffn2_g, v_norm_final_g)
    gains = reduce_adamw("adamw_gains", lgains, gw, gm, gv, 8)

    def padb(t):
        return jnp.pad(t[0], ((0, 0), (0, N_REL_PAD - N_REL)))

    bias = [o[:, :N_REL][None] for o in
            reduce_adamw("adamw_bias", lbias, padb(rel_bias), padb(m_rel_bias), padb(v_rel_bias), H)]
    res["norm_ffn1_g"] = [o[0:1] for o in gains]
    res["norm_mix_g"] = [o[1:2] for o in gains]
    res["norm_ffn2_g"] = [o[2:3] for o in gains]
    res["norm_final_g"] = [o[3] for o in gains]
    res["rel_bias"] = bias

    loss = lax.psum(loss_part[0, 0], MESH_AXES)
    names = ["norm_ffn1_g", "ffn1_w_gate", "ffn1_w_up", "ffn1_w_down", "norm_mix_g", "w_in", "rel_bias",
             "w_out_att", "w_out_ret", "w_out", "norm_ffn2_g", "ffn2_w_gate", "ffn2_w_up", "ffn2_w_down",
             "norm_final_g"]
    out = [loss, grad_x[None]]
    for k in range(4):
        out += [res[nm][k] for nm in names]
    return tuple(out)
```
